```python
import jax, jax.numpy as jnp
from jax import lax
import numpy as np

D_MODEL = 1024
BATCH = 8
SEQ = 4096
DEPTH = 1

PLE_DIM = 256
NORM_EPS = 1e-6
RW_HEADS = 8
RW_HEAD_DIM = 64
RW_WIDTH = RW_HEADS * RW_HEAD_DIM
DECAY_LORA = 64
AAA_LORA = 64
GATE_LORA = 160
RW_COLS = 3 * RW_WIDTH + DECAY_LORA + AAA_LORA + GATE_LORA
RW_LN_EPS = 64e-5
ATT_GROUPS = ((128, 1), (512, 4), (2048, 16))
ATT_HEADS_PER_GROUP = 4
ATT_HEADS = ATT_HEADS_PER_GROUP * len(ATT_GROUPS)
ATT_HEAD_DIM = 64
ATT_WIDTH = ATT_HEADS * ATT_HEAD_DIM
ATT_OUT = ATT_HEADS_PER_GROUP * ATT_HEAD_DIM
ATT_COLS = 3 * ATT_WIDTH
IN_COLS = RW_COLS + ATT_COLS
D_FF = 3 * D_MODEL
CONV_WIDTH = 3

kernel_name = "hybrid_rwkv7_dilated_alibi_convglu"


def rms_norm(x, g):
    xf = x.astype(jnp.float32)
    y = xf * lax.rsqrt(jnp.mean(xf * xf, axis=-1, keepdims=True) + NORM_EPS)
    return (y * g.astype(jnp.float32)).astype(x.dtype)


def shift_right(u, n):
    if n == 0:
        return u
    return jnp.pad(u, ((0, 0), (n, 0), (0, 0)))[:, :-n]


def alibi_slopes(n):
    return jnp.asarray(np.array([2.0 ** (-8.0 * (h + 1) / n) for h in range(n)], dtype=np.float32))


def rwkv7_time_mix(P, mu, w0, w_up, a0, a_up, g_up, k_k, k_a, r_k, ln_g, ln_b):
    B, T, _ = P.shape
    H, N = RW_HEADS, RW_HEAD_DIM
    Pm = P + (shift_right(P, 1) - P) * mu
    cuts = list(np.cumsum([RW_WIDTH, RW_WIDTH, RW_WIDTH, DECAY_LORA, AAA_LORA]))
    r, k, v, xw, xa, xg = jnp.split(Pm, cuts, axis=-1)
    w = -jax.nn.softplus(-(w0 + jnp.tanh(xw) @ w_up)) - 0.5
    decay = jnp.exp(-jnp.exp(w.astype(jnp.float32)))
    a = jax.nn.sigmoid(a0 + xa @ a_up)
    g = jax.nn.sigmoid(xg) @ g_up
    hs = lambda z: z.astype(jnp.float32).reshape(B, T, H, N)
    r, k, v, a, decay = hs(r), hs(k), hs(v), hs(a), hs(decay)
    kk = k * k_k.reshape(H, N)
    kk = kk / jnp.maximum(jnp.linalg.norm(kk, axis=-1, keepdims=True), 1e-12)
    k = k * (1.0 + (a - 1.0) * k_a.reshape(H, N))
    a_vec = -kk
    b_vec = kk * a

    def step(S, inp):
        r_t, w_t, k_t, v_t, a_t, b_t = inp
        Sa = jnp.einsum("bhvk,bhk->bhv", S, a_t)
        S = S * w_t[:, :, None, :] + Sa[..., None] * b_t[:, :, None, :] + v_t[..., None] * k_t[:, :, None, :]
        y = jnp.einsum("bhvk,bhk->bhv", S, r_t)
        return S, y

    tm = lambda z: jnp.swapaxes(z, 0, 1)
    S0 = jnp.zeros((B, H, N, N), jnp.float32)
    _, y = lax.scan(step, S0, (tm(r), tm(decay), tm(k), tm(v), tm(a_vec), tm(b_vec)))
    y = jnp.swapaxes(y, 0, 1)
    mean = jnp.mean(y, axis=-1, keepdims=True)
    var = jnp.mean(jnp.square(y - mean), axis=-1, keepdims=True)
    y = ((y - mean) * lax.rsqrt(var + RW_LN_EPS)).reshape(B, T, RW_WIDTH) * ln_g + ln_b
    bonus = (jnp.sum(r * k * r_k, axis=-1, keepdims=True) * v).reshape(B, T, RW_WIDTH)
    return ((y + bonus) * g).astype(P.dtype)


def dilated_group_attention(q, k, v, window, dilation, slopes):
    B, T, H, E = q.shape
    L = window // dilation
    span = L * dilation
    Tp = -(-T // span) * span
    pad = Tp - T
    nb = Tp // span

    def to_blocks(z):
        z = jnp.pad(z, ((0, 0), (0, pad), (0, 0), (0, 0)))
        return z.reshape(B, nb, L, dilation, H, E)

    def with_prev(z):
        zp = jnp.pad(z, ((0, 0), (1, 0), (0, 0), (0, 0), (0, 0), (0, 0)))[:, :-1]
        return jnp.concatenate([zp, z], axis=2)

    qb = to_blocks(q)
    kc = with_prev(to_blocks(k))
    vc = with_prev(to_blocks(v))
    s = jnp.einsum("bnqrhe,bnkrhe->bnrhqk", qb, kc).astype(jnp.float32) * (E ** -0.5)
    qi = jnp.arange(L)[:, None]
    kj = jnp.arange(2 * L)[None, :]
    steps = qi + L - kj
    blk = jnp.arange(nb)[:, None, None]
    valid = (steps >= 0) & (steps <= L) & (blk * L - L + kj >= 0)
    bias = -slopes[:, None, None] * (dilation * steps).astype(jnp.float32)[None]
    logits = jnp.where(valid[None, :, None, None], s + bias, -jnp.inf)
    lse = jax.nn.logsumexp(logits, axis=-1)
    prob = jnp.exp(logits - lse[..., None])
    o = jnp.einsum("bnrhqk,bnkrhe->bnqrhe", prob.astype(v.dtype), vc)
    o = o.reshape(B, Tp, H, E)[:, :T]
    lse = jnp.moveaxis(lse, 4, 2).reshape(B, Tp, H)[:, :T]
    return o, lse


def dilated_mixture_attention(P, slopes):
    B, T, _ = P.shape
    q, k, v = [z.reshape(B, T, ATT_HEADS, ATT_HEAD_DIM) for z in jnp.split(P, 3, axis=-1)]
    outs, lses = [], []
    for gi, (window, dilation) in enumerate(ATT_GROUPS):
        hsl = slice(gi * ATT_HEADS_PER_GROUP, (gi + 1) * ATT_HEADS_PER_GROUP)
        o, l = dilated_group_attention(q[:, :, hsl], k[:, :, hsl], v[:, :, hsl], window, dilation, slopes[hsl])
        outs.append(o.astype(jnp.float32))
        lses.append(l)
    wts = jax.nn.softmax(jnp.stack(lses, axis=0), axis=0)
    o = jnp.sum(wts[..., None] * jnp.stack(outs, axis=0), axis=0)
    return o.reshape(B, T, ATT_OUT).astype(P.dtype)


def conv_glu_ffn(h, w_up, conv_w, conv_b, w_down):
    u = h @ w_up
    u = conv_b + sum(conv_w[j] * shift_right(u, j) for j in range(CONV_WIDTH))
    gate, val = jnp.split(u, 2, axis=-1)
    return (jax.nn.gelu(gate, approximate=True) * val) @ w_down


def _fwd_setup_inputs(seed: int = 0) -> dict:
    key = jax.random.key(seed)
    ks = jax.random.split(key, 32)
    Ld = DEPTH
    nrm = lambda kk, shape, fan: jax.random.normal(kk, shape, jnp.float32) * (fan ** -0.5)
    gain = lambda kk, n: 1.0 + 0.1 * jax.random.normal(kk, (Ld, n), jnp.float32)
    small = lambda kk, shape, s: s * jax.random.normal(kk, shape, jnp.float32)
    conv_w = jnp.array([1.0, 0.0, 0.0], jnp.float32)[None, :, None] + small(ks[22], (Ld, CONV_WIDTH, 2 * D_FF), 0.2)
    return {
        "x": jax.random.normal(ks[0], (BATCH, SEQ, D_MODEL), jnp.float32),
        "p": jax.random.normal(ks[1], (DEPTH, BATCH, SEQ, PLE_DIM), jnp.float32),
        "g_mix": gain(ks[2], D_MODEL),
        "w_in": nrm(ks[3], (Ld, D_MODEL, IN_COLS), D_MODEL),
        "rw_mu": jax.random.uniform(ks[4], (Ld, RW_COLS), jnp.float32),
        "rw_w0": jax.random.uniform(ks[5], (Ld, RW_WIDTH), jnp.float32, minval=-6.5, maxval=-1.5),
        "rw_w_up": 0.1 * nrm(ks[6], (Ld, DECAY_LORA, RW_WIDTH), DECAY_LORA),
        "rw_a0": small(ks[7], (Ld, RW_WIDTH), 0.1),
        "rw_a_up": nrm(ks[8], (Ld, AAA_LORA, RW_WIDTH), AAA_LORA),
        "rw_g_up": nrm(ks[9], (Ld, GATE_LORA, RW_WIDTH), GATE_LORA),
        "rw_k_k": 0.85 + small(ks[10], (Ld, RW_WIDTH), 0.05),
        "rw_k_a": 1.0 + small(ks[11], (Ld, RW_WIDTH), 0.05),
        "rw_r_k": small(ks[12], (Ld, RW_HEADS, RW_HEAD_DIM), 0.1),
        "rw_ln_g": gain(ks[13], RW_WIDTH),
        "rw_ln_b": small(ks[14], (Ld, RW_WIDTH), 0.01),
        "w_branch_a": nrm(ks[15], (Ld, RW_WIDTH, D_MODEL), RW_WIDTH),
        "w_branch_b": nrm(ks[16], (Ld, ATT_OUT, D_MODEL), ATT_OUT),
        "w_gate": nrm(ks[17], (Ld, D_MODEL, 2 * D_MODEL), D_MODEL),
        "b_gate": small(ks[18], (Ld, 2 * D_MODEL), 0.01),
        "w_out": nrm(ks[19], (Ld, D_MODEL, D_MODEL), D_MODEL),
        "g_ffn": gain(ks[20], D_MODEL),
        "w_up": nrm(ks[21], (Ld, D_MODEL, 2 * D_FF), D_MODEL),
        "conv_w": conv_w,
        "conv_b": small(ks[23], (Ld, 2 * D_FF), 0.01),
        "w_down": nrm(ks[24], (Ld, D_FF, D_MODEL), D_FF),
        "g_ple": gain(ks[25], D_MODEL),
        "w_ple_gate": nrm(ks[26], (Ld, D_MODEL, D_MODEL), D_MODEL),
        "w_ple": nrm(ks[27], (Ld, PLE_DIM, D_MODEL), PLE_DIM),
        "g_final": 1.0 + 0.1 * jax.random.normal(ks[28], (D_MODEL,), jnp.float32),
    }


def _fwd_reference(x, p, g_mix, w_in, rw_mu, rw_w0, rw_w_up, rw_a0, rw_a_up, rw_g_up, rw_k_k, rw_k_a, rw_r_k,
              rw_ln_g, rw_ln_b, w_branch_a, w_branch_b, w_gate, b_gate, w_out, g_ffn, w_up, conv_w, conv_b,
              w_down, g_ple, w_ple_gate, w_ple, g_final):
    slopes = alibi_slopes(ATT_HEADS)
    for i in range(DEPTH):
        h = rms_norm(x, g_mix[i])
        proj = h @ w_in[i]
        y_a = rwkv7_time_mix(proj[..., :RW_COLS], rw_mu[i], rw_w0[i], rw_w_up[i], rw_a0[i], rw_a_up[i],
                             rw_g_up[i], rw_k_k[i], rw_k_a[i], rw_r_k[i], rw_ln_g[i], rw_ln_b[i])
        y_b = dilated_mixture_attention(proj[..., RW_COLS:], slopes)
        gate_a, gate_b = jnp.split(jax.nn.sigmoid(h @ w_gate[i] + b_gate[i]), 2, axis=-1)
        merged = gate_a * (y_a @ w_branch_a[i]) + gate_b * (y_b @ w_branch_b[i])
        x = x + merged @ w_out[i]
        x = x + conv_glu_ffn(rms_norm(x, g_ffn[i]), w_up[i], conv_w[i], conv_b[i], w_down[i])
        ple_gate = jax.nn.sigmoid(rms_norm(x, g_ple[i]) @ w_ple_gate[i])
        x = x + ple_gate * (p[i] @ w_ple[i])
    return rms_norm(x, g_final)


import jax as _jax
import jax.numpy as _jnp

TWIN_FORMAT = 'train_step'
FWD_PARAMS = ['x', 'p', 'g_mix', 'w_in', 'rw_mu', 'rw_w0', 'rw_w_up', 'rw_a0', 'rw_a_up', 'rw_g_up', 'rw_k_k', 'rw_k_a', 'rw_r_k', 'rw_ln_g', 'rw_ln_b', 'w_branch_a', 'w_branch_b', 'w_gate', 'b_gate', 'w_out', 'g_ffn', 'w_up', 'conv_w', 'conv_b', 'w_down', 'g_ple', 'w_ple_gate', 'w_ple', 'g_final']
TWIN_WEIGHTS = ['g_mix', 'w_in', 'rw_mu', 'rw_w0', 'rw_w_up', 'rw_a0', 'rw_a_up', 'rw_g_up', 'rw_k_k', 'rw_k_a', 'rw_r_k', 'rw_ln_g', 'rw_ln_b', 'w_branch_a', 'w_branch_b', 'w_gate', 'b_gate', 'w_out', 'g_ffn', 'w_up', 'conv_w', 'conv_b', 'w_down', 'g_ple', 'w_ple_gate', 'w_ple', 'g_final']
TWIN_DIFF_INPUT = 'x'
TWIN_INPUTS = ['x', 'p', 'g_mix', 'w_in', 'rw_mu', 'rw_w0', 'rw_w_up', 'rw_a0', 'rw_a_up', 'rw_g_up', 'rw_k_k', 'rw_k_a', 'rw_r_k', 'rw_ln_g', 'rw_ln_b', 'w_branch_a', 'w_branch_b', 'w_gate', 'b_gate', 'w_out', 'g_ffn', 'w_up', 'conv_w', 'conv_b', 'w_down', 'g_ple', 'w_ple_gate', 'w_ple', 'g_final', 'loss_target', 'm_g_mix', 'm_w_in', 'm_rw_mu', 'm_rw_w0', 'm_rw_w_up', 'm_rw_a0', 'm_rw_a_up', 'm_rw_g_up', 'm_rw_k_k', 'm_rw_k_a', 'm_rw_r_k', 'm_rw_ln_g', 'm_rw_ln_b', 'm_w_branch_a', 'm_w_branch_b', 'm_w_gate', 'm_b_gate', 'm_w_out', 'm_g_ffn', 'm_w_up', 'm_conv_w', 'm_conv_b', 'm_w_down', 'm_g_ple', 'm_w_ple_gate', 'm_w_ple', 'm_g_final', 'v_g_mix', 'v_w_in', 'v_rw_mu', 'v_rw_w0', 'v_rw_w_up', 'v_rw_a0', 'v_rw_a_up', 'v_rw_g_up', 'v_rw_k_k', 'v_rw_k_a', 'v_rw_r_k', 'v_rw_ln_g', 'v_rw_ln_b', 'v_w_branch_a', 'v_w_branch_b', 'v_w_gate', 'v_b_gate', 'v_w_out', 'v_g_ffn', 'v_w_up', 'v_conv_w', 'v_conv_b', 'v_w_down', 'v_g_ple', 'v_w_ple_gate', 'v_w_ple', 'v_g_final']
TWIN_OUTPUTS = ['loss', 'grad_x', 'grad_g_mix', 'grad_w_in', 'grad_rw_mu', 'grad_rw_w0', 'grad_rw_w_up', 'grad_rw_a0', 'grad_rw_a_up', 'grad_rw_g_up', 'grad_rw_k_k', 'grad_rw_k_a', 'grad_rw_r_k', 'grad_rw_ln_g', 'grad_rw_ln_b', 'grad_w_branch_a', 'grad_w_branch_b', 'grad_w_gate', 'grad_b_gate', 'grad_w_out', 'grad_g_ffn', 'grad_w_up', 'grad_conv_w', 'grad_conv_b', 'grad_w_down', 'grad_g_ple', 'grad_w_ple_gate', 'grad_w_ple', 'grad_g_final', 'delta_g_mix', 'delta_w_in', 'delta_rw_mu', 'delta_rw_w0', 'delta_rw_w_up', 'delta_rw_a0', 'delta_rw_a_up', 'delta_rw_g_up', 'delta_rw_k_k', 'delta_rw_k_a', 'delta_rw_r_k', 'delta_rw_ln_g', 'delta_rw_ln_b', 'delta_w_branch_a', 'delta_w_branch_b', 'delta_w_gate', 'delta_b_gate', 'delta_w_out', 'delta_g_ffn', 'delta_w_up', 'delta_conv_w', 'delta_conv_b', 'delta_w_down', 'delta_g_ple', 'delta_w_ple_gate', 'delta_w_ple', 'delta_g_final', 'new_m_g_mix', 'new_m_w_in', 'new_m_rw_mu', 'new_m_rw_w0', 'new_m_rw_w_up', 'new_m_rw_a0', 'new_m_rw_a_up', 'new_m_rw_g_up', 'new_m_rw_k_k', 'new_m_rw_k_a', 'new_m_rw_r_k', 'new_m_rw_ln_g', 'new_m_rw_ln_b', 'new_m_w_branch_a', 'new_m_w_branch_b', 'new_m_w_gate', 'new_m_b_gate', 'new_m_w_out', 'new_m_g_ffn', 'new_m_w_up', 'new_m_conv_w', 'new_m_conv_b', 'new_m_w_down', 'new_m_g_ple', 'new_m_w_ple_gate', 'new_m_w_ple', 'new_m_g_final', 'new_v_g_mix', 'new_v_w_in', 'new_v_rw_mu', 'new_v_rw_w0', 'new_v_rw_w_up', 'new_v_rw_a0', 'new_v_rw_a_up', 'new_v_rw_g_up', 'new_v_rw_k_k', 'new_v_rw_k_a', 'new_v_rw_r_k', 'new_v_rw_ln_g', 'new_v_rw_ln_b', 'new_v_w_branch_a', 'new_v_w_branch_b', 'new_v_w_gate', 'new_v_b_gate', 'new_v_w_out', 'new_v_g_ffn', 'new_v_w_up', 'new_v_conv_w', 'new_v_conv_b', 'new_v_w_down', 'new_v_g_ple', 'new_v_w_ple_gate', 'new_v_w_ple', 'new_v_g_final']
TWIN_LEAF_KINDS = {'loss': 'loss', 'grad_x': 'grad_x', 'grad_g_mix': 'grad_w', 'grad_w_in': 'grad_w', 'grad_rw_mu': 'grad_w', 'grad_rw_w0': 'grad_w', 'grad_rw_w_up': 'grad_w', 'grad_rw_a0': 'grad_w', 'grad_rw_a_up': 'grad_w', 'grad_rw_g_up': 'grad_w', 'grad_rw_k_k': 'grad_w', 'grad_rw_k_a': 'grad_w', 'grad_rw_r_k': 'grad_w', 'grad_rw_ln_g': 'grad_w', 'grad_rw_ln_b': 'grad_w', 'grad_w_branch_a': 'grad_w', 'grad_w_branch_b': 'grad_w', 'grad_w_gate': 'grad_w', 'grad_b_gate': 'grad_w', 'grad_w_out': 'grad_w', 'grad_g_ffn': 'grad_w', 'grad_w_up': 'grad_w', 'grad_conv_w': 'grad_w', 'grad_conv_b': 'grad_w', 'grad_w_down': 'grad_w', 'grad_g_ple': 'grad_w', 'grad_w_ple_gate': 'grad_w', 'grad_w_ple': 'grad_w', 'grad_g_final': 'grad_w', 'delta_g_mix': 'delta_w', 'delta_w_in': 'delta_w', 'delta_rw_mu': 'delta_w', 'delta_rw_w0': 'delta_w', 'delta_rw_w_up': 'delta_w', 'delta_rw_a0': 'delta_w', 'delta_rw_a_up': 'delta_w', 'delta_rw_g_up': 'delta_w', 'delta_rw_k_k': 'delta_w', 'delta_rw_k_a': 'delta_w', 'delta_rw_r_k': 'delta_w', 'delta_rw_ln_g': 'delta_w', 'delta_rw_ln_b': 'delta_w', 'delta_w_branch_a': 'delta_w', 'delta_w_branch_b': 'delta_w', 'delta_w_gate': 'delta_w', 'delta_b_gate': 'delta_w', 'delta_w_out': 'delta_w', 'delta_g_ffn': 'delta_w', 'delta_w_up': 'delta_w', 'delta_conv_w': 'delta_w', 'delta_conv_b': 'delta_w', 'delta_w_down': 'delta_w', 'delta_g_ple': 'delta_w', 'delta_w_ple_gate': 'delta_w', 'delta_w_ple': 'delta_w', 'delta_g_final': 'delta_w', 'new_m_g_mix': 'new_m', 'new_m_w_in': 'new_m', 'new_m_rw_mu': 'new_m', 'new_m_rw_w0': 'new_m', 'new_m_rw_w_up': 'new_m', 'new_m_rw_a0': 'new_m', 'new_m_rw_a_up': 'new_m', 'new_m_rw_g_up': 'new_m', 'new_m_rw_k_k': 'new_m', 'new_m_rw_k_a': 'new_m', 'new_m_rw_r_k': 'new_m', 'new_m_rw_ln_g': 'new_m', 'new_m_rw_ln_b': 'new_m', 'new_m_w_branch_a': 'new_m', 'new_m_w_branch_b': 'new_m', 'new_m_w_gate': 'new_m', 'new_m_b_gate': 'new_m', 'new_m_w_out': 'new_m', 'new_m_g_ffn': 'new_m', 'new_m_w_up': 'new_m', 'new_m_conv_w': 'new_m', 'new_m_conv_b': 'new_m', 'new_m_w_down': 'new_m', 'new_m_g_ple': 'new_m', 'new_m_w_ple_gate': 'new_m', 'new_m_w_ple': 'new_m', 'new_m_g_final': 'new_m', 'new_v_g_mix': 'new_v', 'new_v_w_in': 'new_v', 'new_v_rw_mu': 'new_v', 'new_v_rw_w0': 'new_v', 'new_v_rw_w_up': 'new_v', 'new_v_rw_a0': 'new_v', 'new_v_rw_a_up': 'new_v', 'new_v_rw_g_up': 'new_v', 'new_v_rw_k_k': 'new_v', 'new_v_rw_k_a': 'new_v', 'new_v_rw_r_k': 'new_v', 'new_v_rw_ln_g': 'new_v', 'new_v_rw_ln_b': 'new_v', 'new_v_w_branch_a': 'new_v', 'new_v_w_branch_b': 'new_v', 'new_v_w_gate': 'new_v', 'new_v_b_gate': 'new_v', 'new_v_w_out': 'new_v', 'new_v_g_ffn': 'new_v', 'new_v_w_up': 'new_v', 'new_v_conv_w': 'new_v', 'new_v_conv_b': 'new_v', 'new_v_w_down': 'new_v', 'new_v_g_ple': 'new_v', 'new_v_w_ple_gate': 'new_v', 'new_v_w_ple': 'new_v', 'new_v_g_final': 'new_v'}


def _forward(args):
    return _fwd_reference(*[args[k] for k in FWD_PARAMS])


def _output_shape():
    out = _jax.eval_shape(lambda: _forward(_fwd_setup_inputs(0)))
    return out.shape, out.dtype

N_MICROBATCH = 1
ADAM_LR = 0.001
ADAM_B1 = 0.9
ADAM_B2 = 0.999
ADAM_EPS = 1e-08
ADAM_WD = 0.01
ADAM_STEP = 10
PER_EXAMPLE_BATCH_AXIS = {'x': 0, 'p': 1, 'loss_target': 0}
SHARED_INPUTS = []
_WEIGHT_DTYPES = {'g_mix': _jnp.float32, 'w_in': _jnp.float32, 'rw_mu': _jnp.float32, 'rw_w0': _jnp.float32, 'rw_w_up': _jnp.float32, 'rw_a0': _jnp.float32, 'rw_a_up': _jnp.float32, 'rw_g_up': _jnp.float32, 'rw_k_k': _jnp.float32, 'rw_k_a': _jnp.float32, 'rw_r_k': _jnp.float32, 'rw_ln_g': _jnp.float32, 'rw_ln_b': _jnp.float32, 'w_branch_a': _jnp.float32, 'w_branch_b': _jnp.float32, 'w_gate': _jnp.float32, 'b_gate': _jnp.float32, 'w_out': _jnp.float32, 'g_ffn': _jnp.float32, 'w_up': _jnp.float32, 'conv_w': _jnp.float32, 'conv_b': _jnp.float32, 'w_down': _jnp.float32, 'g_ple': _jnp.float32, 'w_ple_gate': _jnp.float32, 'w_ple': _jnp.float32, 'g_final': _jnp.float32}
MOMENT_SCALE = {'g_mix': 1.192860e-01, 'w_in': 5.933534e-02, 'rw_mu': 1.312985e-01, 'rw_w0': 3.519737e-02, 'rw_w_up': 3.158987e-03, 'rw_a0': 3.021009e-02, 'rw_a_up': 2.891498e-02, 'rw_g_up': 7.853262e-02, 'rw_k_k': 1.360414e-01, 'rw_k_a': 8.926919e-02, 'rw_r_k': 1.709251e-01, 'rw_ln_g': 7.876489e-02, 'rw_ln_b': 9.709578e-02, 'w_branch_a': 5.711235e-02, 'w_branch_b': 3.375667e-02, 'w_gate': 1.812323e-02, 'b_gate': 1.816494e-02, 'w_out': 6.565761e-02, 'g_ffn': 1.440089e-01, 'w_up': 5.840921e-02, 'conv_w': 5.823082e-02, 'conv_b': 6.026401e-02, 'w_down': 1.025817e-01, 'g_ple': 2.704539e-02, 'w_ple_gate': 2.770873e-02, 'w_ple': 7.668912e-02, 'g_final': 3.199660e+01}


def _to_microbatches(a, axis):
    t = _jnp.moveaxis(a, axis, 0)
    t = t.reshape((N_MICROBATCH, t.shape[0] // N_MICROBATCH) + t.shape[1:])
    return _jnp.moveaxis(t, 1, axis + 1)


def setup_inputs(seed: int = 0) -> dict:
    inp = _fwd_setup_inputs(seed)
    key = _jax.random.fold_in(_jax.random.key(seed), 7919)
    shape, _ = _output_shape()
    out = dict(inp)
    out["loss_target"] = _jax.random.normal(_jax.random.fold_in(key, 0), shape, _jnp.float32)
    for i, name in enumerate(TWIN_WEIGHTS):
        w = inp[name].astype(_jnp.float32)
        if MOMENT_SCALE is None:
            s = _jnp.sqrt(_jnp.mean(_jnp.square(w)) + 1e-30)
        else:
            s = MOMENT_SCALE[name]
        km, kv = _jax.random.split(_jax.random.fold_in(key, i + 1))
        out[name] = w
        out["m_" + name] = s * _jax.random.normal(km, w.shape, _jnp.float32)
        out["v_" + name] = (s * s) * _jax.random.uniform(kv, w.shape, _jnp.float32, 0.5, 1.5)
    if N_MICROBATCH > 1:
        for name, axis in PER_EXAMPLE_BATCH_AXIS.items():
            out[name] = _to_microbatches(out[name], axis)
    return {'x': out['x'], 'p': out['p'], 'g_mix': out['g_mix'], 'w_in': out['w_in'], 'rw_mu': out['rw_mu'], 'rw_w0': out['rw_w0'], 'rw_w_up': out['rw_w_up'], 'rw_a0': out['rw_a0'], 'rw_a_up': out['rw_a_up'], 'rw_g_up': out['rw_g_up'], 'rw_k_k': out['rw_k_k'], 'rw_k_a': out['rw_k_a'], 'rw_r_k': out['rw_r_k'], 'rw_ln_g': out['rw_ln_g'], 'rw_ln_b': out['rw_ln_b'], 'w_branch_a': out['w_branch_a'], 'w_branch_b': out['w_branch_b'], 'w_gate': out['w_gate'], 'b_gate': out['b_gate'], 'w_out': out['w_out'], 'g_ffn': out['g_ffn'], 'w_up': out['w_up'], 'conv_w': out['conv_w'], 'conv_b': out['conv_b'], 'w_down': out['w_down'], 'g_ple': out['g_ple'], 'w_ple_gate': out['w_ple_gate'], 'w_ple': out['w_ple'], 'g_final': out['g_final'], 'loss_target': out['loss_target'], 'm_g_mix': out['m_g_mix'], 'm_w_in': out['m_w_in'], 'm_rw_mu': out['m_rw_mu'], 'm_rw_w0': out['m_rw_w0'], 'm_rw_w_up': out['m_rw_w_up'], 'm_rw_a0': out['m_rw_a0'], 'm_rw_a_up': out['m_rw_a_up'], 'm_rw_g_up': out['m_rw_g_up'], 'm_rw_k_k': out['m_rw_k_k'], 'm_rw_k_a': out['m_rw_k_a'], 'm_rw_r_k': out['m_rw_r_k'], 'm_rw_ln_g': out['m_rw_ln_g'], 'm_rw_ln_b': out['m_rw_ln_b'], 'm_w_branch_a': out['m_w_branch_a'], 'm_w_branch_b': out['m_w_branch_b'], 'm_w_gate': out['m_w_gate'], 'm_b_gate': out['m_b_gate'], 'm_w_out': out['m_w_out'], 'm_g_ffn': out['m_g_ffn'], 'm_w_up': out['m_w_up'], 'm_conv_w': out['m_conv_w'], 'm_conv_b': out['m_conv_b'], 'm_w_down': out['m_w_down'], 'm_g_ple': out['m_g_ple'], 'm_w_ple_gate': out['m_w_ple_gate'], 'm_w_ple': out['m_w_ple'], 'm_g_final': out['m_g_final'], 'v_g_mix': out['v_g_mix'], 'v_w_in': out['v_w_in'], 'v_rw_mu': out['v_rw_mu'], 'v_rw_w0': out['v_rw_w0'], 'v_rw_w_up': out['v_rw_w_up'], 'v_rw_a0': out['v_rw_a0'], 'v_rw_a_up': out['v_rw_a_up'], 'v_rw_g_up': out['v_rw_g_up'], 'v_rw_k_k': out['v_rw_k_k'], 'v_rw_k_a': out['v_rw_k_a'], 'v_rw_r_k': out['v_rw_r_k'], 'v_rw_ln_g': out['v_rw_ln_g'], 'v_rw_ln_b': out['v_rw_ln_b'], 'v_w_branch_a': out['v_w_branch_a'], 'v_w_branch_b': out['v_w_branch_b'], 'v_w_gate': out['v_w_gate'], 'v_b_gate': out['v_b_gate'], 'v_w_out': out['v_w_out'], 'v_g_ffn': out['v_g_ffn'], 'v_w_up': out['v_w_up'], 'v_conv_w': out['v_conv_w'], 'v_conv_b': out['v_conv_b'], 'v_w_down': out['v_w_down'], 'v_g_ple': out['v_g_ple'], 'v_w_ple_gate': out['v_w_ple_gate'], 'v_w_ple': out['v_w_ple'], 'v_g_final': out['v_g_final']}


def _loss(weights, diff, rest, loss_target):
    with _jax.named_scope("forward"):
        args = {**rest, TWIN_DIFF_INPUT: diff, **{k: w.astype(_WEIGHT_DTYPES[k]) for k, w in weights.items()}}
        y = _forward(args)
    with _jax.named_scope("loss_head"):
        err = _jnp.square(y.astype(_jnp.float32) - loss_target)
        return 0.5 * _jnp.sum(_jnp.mean(err, axis=-1)) if err.ndim else 0.5 * err


def _adamw(w, g, m, v):
    m = ADAM_B1 * m + (1.0 - ADAM_B1) * g
    v = ADAM_B2 * v + (1.0 - ADAM_B2) * _jnp.square(g)
    m_hat = m / (1.0 - ADAM_B1 ** ADAM_STEP)
    v_hat = v / (1.0 - ADAM_B2 ** ADAM_STEP)
    delta = -ADAM_LR * (m_hat / (_jnp.sqrt(v_hat) + ADAM_EPS) + ADAM_WD * w)
    return delta, m, v


def reference(x, p, g_mix, w_in, rw_mu, rw_w0, rw_w_up, rw_a0, rw_a_up, rw_g_up, rw_k_k, rw_k_a, rw_r_k, rw_ln_g, rw_ln_b, w_branch_a, w_branch_b, w_gate, b_gate, w_out, g_ffn, w_up, conv_w, conv_b, w_down, g_ple, w_ple_gate, w_ple, g_final, loss_target, m_g_mix, m_w_in, m_rw_mu, m_rw_w0, m_rw_w_up, m_rw_a0, m_rw_a_up, m_rw_g_up, m_rw_k_k, m_rw_k_a, m_rw_r_k, m_rw_ln_g, m_rw_ln_b, m_w_branch_a, m_w_branch_b, m_w_gate, m_b_gate, m_w_out, m_g_ffn, m_w_up, m_conv_w, m_conv_b, m_w_down, m_g_ple, m_w_ple_gate, m_w_ple, m_g_final, v_g_mix, v_w_in, v_rw_mu, v_rw_w0, v_rw_w_up, v_rw_a0, v_rw_a_up, v_rw_g_up, v_rw_k_k, v_rw_k_a, v_rw_r_k, v_rw_ln_g, v_rw_ln_b, v_w_branch_a, v_w_branch_b, v_w_gate, v_b_gate, v_w_out, v_g_ffn, v_w_up, v_conv_w, v_conv_b, v_w_down, v_g_ple, v_w_ple_gate, v_w_ple, v_g_final):
    given = dict(x=x, p=p, g_mix=g_mix, w_in=w_in, rw_mu=rw_mu, rw_w0=rw_w0, rw_w_up=rw_w_up, rw_a0=rw_a0, rw_a_up=rw_a_up, rw_g_up=rw_g_up, rw_k_k=rw_k_k, rw_k_a=rw_k_a, rw_r_k=rw_r_k, rw_ln_g=rw_ln_g, rw_ln_b=rw_ln_b, w_branch_a=w_branch_a, w_branch_b=w_branch_b, w_gate=w_gate, b_gate=b_gate, w_out=w_out, g_ffn=g_ffn, w_up=w_up, conv_w=conv_w, conv_b=conv_b, w_down=w_down, g_ple=g_ple, w_ple_gate=w_ple_gate, w_ple=w_ple, g_final=g_final, loss_target=loss_target, m_g_mix=m_g_mix, m_w_in=m_w_in, m_rw_mu=m_rw_mu, m_rw_w0=m_rw_w0, m_rw_w_up=m_rw_w_up, m_rw_a0=m_rw_a0, m_rw_a_up=m_rw_a_up, m_rw_g_up=m_rw_g_up, m_rw_k_k=m_rw_k_k, m_rw_k_a=m_rw_k_a, m_rw_r_k=m_rw_r_k, m_rw_ln_g=m_rw_ln_g, m_rw_ln_b=m_rw_ln_b, m_w_branch_a=m_w_branch_a, m_w_branch_b=m_w_branch_b, m_w_gate=m_w_gate, m_b_gate=m_b_gate, m_w_out=m_w_out, m_g_ffn=m_g_ffn, m_w_up=m_w_up, m_conv_w=m_conv_w, m_conv_b=m_conv_b, m_w_down=m_w_down, m_g_ple=m_g_ple, m_w_ple_gate=m_w_ple_gate, m_w_ple=m_w_ple, m_g_final=m_g_final, v_g_mix=v_g_mix, v_w_in=v_w_in, v_rw_mu=v_rw_mu, v_rw_w0=v_rw_w0, v_rw_w_up=v_rw_w_up, v_rw_a0=v_rw_a0, v_rw_a_up=v_rw_a_up, v_rw_g_up=v_rw_g_up, v_rw_k_k=v_rw_k_k, v_rw_k_a=v_rw_k_a, v_rw_r_k=v_rw_r_k, v_rw_ln_g=v_rw_ln_g, v_rw_ln_b=v_rw_ln_b, v_w_branch_a=v_w_branch_a, v_w_branch_b=v_w_branch_b, v_w_gate=v_w_gate, v_b_gate=v_b_gate, v_w_out=v_w_out, v_g_ffn=v_g_ffn, v_w_up=v_w_up, v_conv_w=v_conv_w, v_conv_b=v_conv_b, v_w_down=v_w_down, v_g_ple=v_g_ple, v_w_ple_gate=v_w_ple_gate, v_w_ple=v_w_ple, v_g_final=v_g_final)
    weights = {n: given[n] for n in TWIN_WEIGHTS}
    shared = {n: given[n] for n in SHARED_INPUTS}
    per_example = {n: given[n] for n in ['x', 'p']}
    grad_fn = _jax.value_and_grad(_loss, argnums=(0, 1))

    def one_microbatch(ex, loss_target):
        ex = dict(ex)
        diff = ex.pop(TWIN_DIFF_INPUT)
        return grad_fn(weights, diff, {**shared, **ex}, loss_target)

    if N_MICROBATCH == 1:
        loss, (grad_w, grad_x) = one_microbatch(per_example, given["loss_target"])
    else:
        def body(carry, xs):
            loss_sum, grad_sum = carry
            l_k, (gw_k, gx_k) = one_microbatch(xs[0], xs[1])
            with _jax.named_scope("update"):
                return (loss_sum + l_k, _jax.tree.map(_jnp.add, grad_sum, gw_k)), gx_k

        init = (_jnp.zeros((), _jnp.float32), _jax.tree.map(_jnp.zeros_like, weights))
        (loss, grad_w), grad_x = _jax.lax.scan(body, init, (per_example, given["loss_target"]))
    with _jax.named_scope("update"):
        delta_w, new_m, new_v = {}, {}, {}
        for n in TWIN_WEIGHTS:
            delta_w[n], new_m[n], new_v[n] = _adamw(weights[n], grad_w[n], given["m_" + n], given["v_" + n])
    return (loss, grad_x, *[grad_w[n] for n in TWIN_WEIGHTS], *[delta_w[n] for n in TWIN_WEIGHTS],
            *[new_m[n] for n in TWIN_WEIGHTS], *[new_v[n] for n in TWIN_WEIGHTS])
```

```python
import functools
import math

import numpy as np
import jax
import jax.numpy as jnp
from jax import lax
from jax.experimental import pallas as pl
from jax.experimental.pallas import tpu as pltpu

F32 = jnp.float32
BF16 = jnp.bfloat16
GRAD_WIRE = jnp.bfloat16

N_DEV = 8
NORM_EPS = 1e-6
RW_LN_EPS = 64e-5
HEAD = 64
RW_WIDTH = 512
ATT_GROUPS = ((128, 1), (512, 4), (2048, 16))
ATT_HEADS = 12
ATT_OUT = 256
ATT_COLS = 2304
OFF_XW, OFF_XA, OFF_XG, RW_PAD, PROJ_PAD = 1536, 1664, 1792, 2048, 4608
PROJ_TAIL = PROJ_PAD - RW_PAD - ATT_COLS
D_FF = 3072

ADAM_LR, ADAM_B1, ADAM_B2, ADAM_EPS, ADAM_WD, ADAM_STEP = 0.001, 0.9, 0.999, 1e-08, 0.01, 10

VMEM_LIMIT_BYTES = 56 * 1024 * 1024
ADAM_TILE_ELEMS = 256 * 1024
NEG_BIG = -1e30

NT_DIMS = (((1,), (1,)), ((), ()))
TN_DIMS = (((0,), (0,)), ((), ()))
NN_DIMS = (((1,), (0,)), ((), ()))


def _cparams(n_axes):
    return pltpu.CompilerParams(dimension_semantics=("arbitrary",) * n_axes,
                                vmem_limit_bytes=VMEM_LIMIT_BYTES)


def _split2(x):
    hi = x.astype(BF16)
    lo = (x - hi.astype(F32)).astype(BF16)
    return hi, lo


def _seg_mat(n):
    r = lax.shift_right_logical(lax.broadcasted_iota(jnp.int32, (n, n), 0), 6)
    c = lax.shift_right_logical(lax.broadcasted_iota(jnp.int32, (n, n), 1), 6)
    return jnp.where(r == c, 1.0, 0.0).astype(BF16)


def _segb(x, seg):
    hi, lo = _split2(x)
    return (jnp.dot(hi, seg, preferred_element_type=F32)
            + jnp.dot(lo, seg, preferred_element_type=F32))


def _segb1(x, seg):
    return jnp.dot(x.astype(BF16), seg, preferred_element_type=F32)


@jax.custom_vjp
def segsum(x):
    return _segb(x, _seg_mat(x.shape[1]))


def _segsum_fwd(x):
    return segsum(x), None


def _segsum_bwd(_, ct):
    return (segsum(ct),)


segsum.defvjp(_segsum_fwd, _segsum_bwd)


@jax.custom_vjp
def bdot(a, b):
    return jnp.dot(a.astype(BF16), b.astype(BF16), preferred_element_type=F32)


def _bdot_fwd(a, b):
    return bdot(a, b), (a, b)


def _bdot_bwd(res, ct):
    a, b = res
    ctb = ct.astype(BF16)
    da = lax.dot_general(ctb, b.astype(BF16), NT_DIMS, preferred_element_type=F32)
    db = lax.dot_general(a.astype(BF16), ctb, TN_DIMS, preferred_element_type=F32)
    return da.astype(a.dtype), db.astype(b.dtype)


bdot.defvjp(_bdot_fwd, _bdot_bwd)


def _sig(x):
    return 1.0 / (1.0 + jnp.exp(-x))


def _softplus(z):
    return jnp.maximum(z, 0.0) + jnp.log(1.0 + jnp.exp(-jnp.abs(z)))


def _gelu_tanh(x):
    return 0.5 * x * (1.0 + jnp.tanh(0.7978845608028654 * (x + 0.044715 * (x * x * x))))


def _rms(x, g):
    return x * lax.rsqrt(jnp.mean(x * x, axis=-1, keepdims=True) + NORM_EPS) * g


def _shift_down(x, prev8, n):
    rolled = pltpu.roll(x, n, 0)
    top = pltpu.roll(prev8, n, 0)
    rid = lax.broadcasted_iota(jnp.int32, (8, x.shape[1]), 0)
    head = jnp.where(rid < n, top, rolled[:8])
    return jnp.concatenate([head, rolled[8:]], axis=0)


def _shift_up(x, next8, n):
    rows = x.shape[0]
    rolled = pltpu.roll(x, rows - n, 0)
    bottom = pltpu.roll(next8, 8 - n, 0)
    rid = lax.broadcasted_iota(jnp.int32, (8, x.shape[1]), 0)
    tail = jnp.where(rid >= 8 - n, bottom, rolled[rows - 8:])
    return jnp.concatenate([rolled[:rows - 8], tail], axis=0)


def tile_call(name, fn, grid, ins, outs, scratch=()):
    n_in, n_out = len(ins), len(outs)
    acc_axes = [o[4] for o in outs]

    def body(*refs):
        pids = tuple(pl.program_id(a) for a in range(len(grid)))
        vals = fn(pids, *[r[...] for r in refs[:n_in]], *refs[n_in + n_out:])
        if not isinstance(vals, (tuple, list)):
            vals = (vals,)
        for o_ref, val, ax in zip(refs[n_in:n_in + n_out], vals, acc_axes):
            if ax is None:
                o_ref[...] = val.astype(o_ref.dtype)
            else:
                @pl.when(pids[ax] == 0)
                def _(o_ref=o_ref):
                    o_ref[...] = jnp.zeros_like(o_ref)

                o_ref[...] += val.astype(o_ref.dtype)

    res = pl.pallas_call(
        body, name=name, grid=grid,
        in_specs=[pl.BlockSpec(b, im) for _, b, im in ins],
        out_specs=[pl.BlockSpec(o[2], o[3]) for o in outs],
        out_shape=[jax.ShapeDtypeStruct(o[0], o[1]) for o in outs],
        scratch_shapes=[pltpu.VMEM(s, d) for s, d in scratch],
        compiler_params=_cparams(len(grid)),
    )(*[a for a, _, _ in ins])
    return res


def _rows(a, tm):
    return (a, (tm, a.shape[1]), lambda i: (i, 0))


def _par(a):
    return (a, a.shape, lambda i: (0, 0))


def _row_out(T, C, dtype, tm):
    return ((T, C), dtype, (tm, C), lambda i: (i, 0), None)


def _acc_out(R, C):
    return ((R, C), F32, (R, C), lambda i: (0, 0), 0)


def _prev_halo(a, tm, C):
    return (a, (8, C), lambda i: (jnp.maximum(i * (tm // 8) - 1, 0), 0))


def _next_halo(a, tm, C, T):
    return (a, (8, C), lambda i: (jnp.minimum((i + 1) * (tm // 8), T // 8 - 1), 0))


def _pick(n, target):
    for t in (target, 1024, 768, 512, 384, 256, 128):
        if t <= target and n % t == 0:
            return t
    return n


def matmul(name, a, b, mode="nn", res=None, out_dtype=F32, tm=1024, tn=1024, tk=1024):
    if mode == "nn":
        (M, K), (K2, N) = a.shape, b.shape
    elif mode == "tn":
        (K, M), (K2, N) = a.shape, b.shape
    else:
        (M, K), (N, K2) = a.shape, b.shape
    assert K == K2, (name, a.shape, b.shape, mode)
    tm, tn, tk = _pick(M, tm), _pick(N, tn), _pick(K, tk)
    nk = K // tk
    dims = {"nn": NN_DIMS, "tn": TN_DIMS, "nt": NT_DIMS}[mode]
    a_spec = {"nn": pl.BlockSpec((tm, tk), lambda i, j, k: (i, k)),
              "tn": pl.BlockSpec((tk, tm), lambda i, j, k: (k, i)),
              "nt": pl.BlockSpec((tm, tk), lambda i, j, k: (i, k))}[mode]
    b_spec = {"nn": pl.BlockSpec((tk, tn), lambda i, j, k: (k, j)),
              "tn": pl.BlockSpec((tk, tn), lambda i, j, k: (k, j)),
              "nt": pl.BlockSpec((tn, tk), lambda i, j, k: (j, k))}[mode]
    has_res = res is not None

    def body(*refs):
        if has_res:
            a_ref, b_ref, r_ref, o_ref, acc_ref = refs
        else:
            a_ref, b_ref, o_ref, acc_ref = refs
        k = pl.program_id(2)

        @pl.when(k == 0)
        def _():
            acc_ref[...] = jnp.zeros_like(acc_ref)

        acc_ref[...] += lax.dot_general(a_ref[...].astype(BF16), b_ref[...].astype(BF16), dims,
                                        preferred_element_type=F32)

        @pl.when(k == nk - 1)
        def _():
            out = acc_ref[...]
            if has_res:
                out = out + r_ref[...].astype(F32)
            o_ref[...] = out.astype(o_ref.dtype)

    in_specs = [a_spec, b_spec]
    args = [a, b]
    if has_res:
        in_specs.append(pl.BlockSpec((tm, tn), lambda i, j, k: (i, j)))
        args.append(res)
    return pl.pallas_call(
        body, name=name, grid=(M // tm, N // tn, nk),
        in_specs=in_specs,
        out_specs=pl.BlockSpec((tm, tn), lambda i, j, k: (i, j)),
        out_shape=jax.ShapeDtypeStruct((M, N), out_dtype),
        scratch_shapes=[pltpu.VMEM((tm, tn), F32)],
        compiler_params=_cparams(3),
    )(*args)


def rw_pre(Pc, Ps, mu, w0, w_up, a0, a_up, g_up, k_k, k_a):
    Pm = Pc + (Ps - Pc) * mu
    r, k, v = Pm[:, 0:512], Pm[:, 512:1024], Pm[:, 1024:1536]
    xw, xa, xg = Pm[:, OFF_XW:OFF_XA], Pm[:, OFF_XA:OFF_XG], Pm[:, OFF_XG:RW_PAD]
    w = -_softplus(-(w0 + bdot(jnp.tanh(xw), w_up))) - 0.5
    decay = jnp.exp(-jnp.exp(w))
    a = _sig(a0 + bdot(xa, a_up))
    g = bdot(_sig(xg), g_up)
    kk = k * k_k
    kk = kk / jnp.maximum(jnp.sqrt(segsum(kk * kk)), 1e-12)
    k2 = k * (1.0 + (a - 1.0) * k_a)
    return r, decay, k2, v, -kk, kk * a, g


def rw_post(y, r, k2, v, g, ln_g, ln_b, r_k):
    mean = segsum(y) * (1.0 / HEAD)
    d = y - mean
    var = segsum(d * d) * (1.0 / HEAD)
    yn = d * lax.rsqrt(var + RW_LN_EPS) * ln_g + ln_b
    bonus = segsum(r * k2 * r_k) * v
    return (yn + bonus) * g


def att_combine(o1, o2, o3, l1, l2, l3):
    m = jnp.maximum(jnp.maximum(l1, l2), l3)
    e1, e2, e3 = jnp.exp(l1 - m), jnp.exp(l2 - m), jnp.exp(l3 - m)
    return (e1 * o1 + e2 * o2 + e3 * o3) / (e1 + e2 + e3)


def merge_fn(gp, bg, za, zb):
    s = _sig(gp + bg)
    half = za.shape[1]
    return s[:, :half] * za + s[:, half:] * zb


def tail_loss(x2, zg, pe, g_final, target):
    x3 = x2 + _sig(zg) * pe
    y = _rms(x3, g_final)
    err = (y - target) * (y - target)
    return 0.5 * jnp.sum(jnp.mean(err, axis=-1, keepdims=True))


SCAN_CHUNK = HEAD
SCAN_LANES = 256
SCAN_UNROLL_FWD, SCAN_UNROLL_BWD = 4, 4


def _to_head_time(z):
    T = z.shape[0]
    return z.reshape(T // HEAD, HEAD, RW_WIDTH // HEAD, HEAD).transpose(0, 3, 2, 1).reshape(T // HEAD, HEAD, RW_WIDTH)


def _from_head_time(zt):
    C = zt.shape[0]
    return zt.reshape(C, HEAD, RW_WIDTH // HEAD, HEAD).transpose(0, 3, 2, 1).reshape(C * HEAD, RW_WIDTH)


def _unrolled_loop(n, step, init, unroll):
    def body(i, carry):
        for j in range(unroll):
            carry = step(i * unroll + j, carry)
        return carry

    return lax.fori_loop(0, n // unroll, body, init)


def _lane_groups():
    return [slice(j * SCAN_LANES, (j + 1) * SCAN_LANES) for j in range(RW_WIDTH // SCAN_LANES)]


def scan_pair_terms(a, w, b, k, tm=256):
    T = a.shape[0]

    def fn(pid, a_t, nxt, w_t, b_t, k_t):
        a_next = _shift_up(a_t, jnp.where(pid[0] < T // tm - 1, nxt, 0.0), 1)
        return w_t * a_next, segsum(b_t * a_next), segsum(k_t * a_next)

    return tile_call("scan_pair_terms", fn, (T // tm,),
                     [_rows(a, tm), _next_halo(a, tm, RW_WIDTH, T), _rows(w, tm), _rows(b, tm), _rows(k, tm)],
                     [_row_out(T, RW_WIDTH, F32, tm)] * 3)


def rwkv_scan_fwd(a, w, b, k, r, vT, wa, ba, ka, exchange=()):
    T = a.shape[0]
    C, LW = SCAN_CHUNK, SCAN_LANES
    nC = T // C
    nx = len(exchange)

    def body(*refs):
        a_ref, w_ref, b_ref, k_ref, r_ref, vT_ref, wa_ref, ba_ref, ka_ref = refs[:9]
        x_refs, refs = refs[9:9 + nx], refs[9 + nx:]
        yT_ref, S_ref = refs[:2]
        land_refs, refs = refs[2:2 + nx], refs[2 + nx:]
        st_ref, vb0_ref, vb1_ref = refs[:3]
        if nx:
            start, wait = _exchange_ops([c for _, c in exchange], x_refs, land_refs, *refs[3:])

        @pl.when(pl.program_id(0) == 0)
        def _():
            st_ref[...] = jnp.zeros_like(st_ref)
            if nx:
                start()

        seg = _seg_mat(LW)
        lane = jnp.bitwise_and(lax.broadcasted_iota(jnp.int32, (1, LW), 1), HEAD - 1)
        groups = _lane_groups()

        def vcol(t, gsl):
            return _segb1(jnp.where(lane == t, vT_ref[0, :, gsl], 0.0), seg)

        for gsl in groups:
            vb0_ref[:, gsl] = vcol(0, gsl)
            vb1_ref[:, gsl] = vcol(1, gsl)

        def pair(i, yacc):
            t = 2 * i
            t1 = t + 1
            tp = jnp.maximum(t - 1, 0)
            row = lambda ref, s, gsl: ref[pl.ds(s, 1), gsl]
            Sps = [st_ref[:, gsl] for gsl in groups]
            sas = [_segb(Sp * row(a_ref, t, gsl), seg) for gsl, Sp in zip(groups, Sps)]
            us = [_segb(Sp * row(wa_ref, t, gsl), seg) for gsl, Sp in zip(groups, Sps)]
            S1s = []
            for gsl, Sp, sa, u in zip(groups, Sps, sas, us):
                vb0, vb1 = vb0_ref[:, gsl], vb1_ref[:, gsl]
                S1 = Sp * row(w_ref, t, gsl) + sa * row(b_ref, t, gsl) + vb0 * row(k_ref, t, gsl)
                sa1 = u + sa * row(ba_ref, t, gsl) + vb0 * row(ka_ref, t, gsl)
                st_ref[:, gsl] = S1 * row(w_ref, t1, gsl) + sa1 * row(b_ref, t1, gsl) + vb1 * row(k_ref, t1, gsl)
                S_ref[t, :, gsl] = Sp
                S_ref[t1, :, gsl] = S1
                S1s.append(S1)
            out = []
            for gsl, Sp, S1, ya in zip(groups, Sps, S1s, yacc):
                yb0 = _segb1(Sp * row(r_ref, tp, gsl), seg)
                yb1 = _segb1(S1 * row(r_ref, t, gsl), seg)
                out.append(jnp.where(lane == t, yb1, jnp.where(lane == t - 1, yb0, ya)))
                vb0_ref[:, gsl] = vcol(t + 2, gsl)
                vb1_ref[:, gsl] = vcol(t + 3, gsl)
            return tuple(out)

        yacc = _unrolled_loop(C // 2, pair, tuple(jnp.zeros((HEAD, LW), F32) for _ in groups), SCAN_UNROLL_FWD)
        for gsl, ya in zip(groups, yacc):
            yb = _segb1(st_ref[:, gsl] * r_ref[pl.ds(C - 1, 1), gsl], seg)
            yT_ref[0, :, gsl] = jnp.where(lane == C - 1, yb, ya)

        if nx:
            @pl.when(pl.program_id(0) == nC - 1)
            def _():
                wait()

    row = pl.BlockSpec((C, RW_WIDTH), lambda c: (c, 0))
    ht = pl.BlockSpec((1, HEAD, RW_WIDTH), lambda c: (c, 0, 0))
    hbm = pl.BlockSpec(memory_space=pl.ANY)
    res = pl.pallas_call(
        body, name="rwkv_scan_fwd", grid=(nC,),
        in_specs=[row, row, row, row, row, ht, row, row, row] + [hbm] * nx,
        out_specs=[ht, pl.BlockSpec((C, HEAD, RW_WIDTH), lambda c: (c, 0, 0))] + [hbm] * nx,
        out_shape=[jax.ShapeDtypeStruct((nC, HEAD, RW_WIDTH), F32),
                   jax.ShapeDtypeStruct((T, HEAD, RW_WIDTH), F32)] + _exchange_shapes(exchange),
        scratch_shapes=[pltpu.VMEM((HEAD, RW_WIDTH), F32)] * 3 + (_exchange_sems(nx) if nx else []),
        compiler_params=pltpu.CompilerParams(dimension_semantics=("arbitrary",), vmem_limit_bytes=VMEM_LIMIT_BYTES,
                                             has_side_effects=bool(nx)),
    )(a, w, b, k, r, vT, wa, ba, ka, *[z for z, _ in exchange])
    return res[:2], res[2:]


def rwkv_scan_bwd(a, w, b, k, r, vT, dyT, S_all, exchange=()):
    T = a.shape[0]
    C, LW = SCAN_CHUNK, SCAN_LANES
    nC = T // C
    nx = len(exchange)

    def body(*refs):
        a_ref, w_ref, b_ref, k_ref, r_ref, vT_ref, dyT_ref, S_ref = refs[:8]
        x_refs, refs = refs[8:8 + nx], refs[8 + nx:]
        da_ref, dw_ref, db_ref, dk_ref, dr_ref, dvT_ref = refs[:6]
        land_refs, refs = refs[6:6 + nx], refs[6 + nx:]
        ds_ref, dyb_ref = refs[:2]
        if nx:
            start, wait = _exchange_ops([c for _, c in exchange], x_refs, land_refs, *refs[2:])

        @pl.when(pl.program_id(0) == 0)
        def _():
            ds_ref[...] = jnp.zeros_like(ds_ref)
            if nx:
                start()

        seg = _seg_mat(LW)
        lane = jnp.bitwise_and(lax.broadcasted_iota(jnp.int32, (1, LW), 1), HEAD - 1)
        groups = _lane_groups()

        def colsum(z):
            return jnp.sum(z, axis=0, keepdims=True)

        for gsl in groups:
            dyb_ref[:, gsl] = _segb1(jnp.where(lane == C - 1, dyT_ref[0, :, gsl], 0.0), seg)

        def step(i, dvacc):
            t = C - 1 - i
            dybs = [dyb_ref[:, gsl] for gsl in groups]
            dSs = [ds_ref[:, gsl] + dyb * r_ref[pl.ds(t, 1), gsl] for gsl, dyb in zip(groups, dybs)]
            dsabs = [_segb(dS * b_ref[pl.ds(t, 1), gsl], seg) for gsl, dS in zip(groups, dSs)]
            for gsl, dS, dsab in zip(groups, dSs, dsabs):
                ds_ref[:, gsl] = dS * w_ref[pl.ds(t, 1), gsl] + dsab * a_ref[pl.ds(t, 1), gsl]
            out = []
            for gsl, dva, dyb, dS, dsab in zip(groups, dvacc, dybs, dSs, dsabs):
                ar, wr, br = a_ref[pl.ds(t, 1), gsl], w_ref[pl.ds(t, 1), gsl], b_ref[pl.ds(t, 1), gsl]
                kr = k_ref[pl.ds(t, 1), gsl]
                Sp = S_ref[t, :, gsl]
                vb = _segb1(jnp.where(lane == t, vT_ref[0, :, gsl], 0.0), seg)
                sab = _segb1(Sp * ar, seg)
                S = Sp * wr + sab * br + vb * kr
                dr_ref[pl.ds(t, 1), gsl] = colsum(S * dyb)
                dvb = _segb1(dS * kr, seg)
                dk_ref[pl.ds(t, 1), gsl] = colsum(dS * vb)
                db_ref[pl.ds(t, 1), gsl] = colsum(dS * sab)
                dw_ref[pl.ds(t, 1), gsl] = colsum(dS * Sp)
                da_ref[pl.ds(t, 1), gsl] = colsum(Sp * dsab)
                dyb_ref[:, gsl] = _segb1(jnp.where(lane == t - 1, dyT_ref[0, :, gsl], 0.0), seg)
                out.append(jnp.where(lane == t, dvb, dva))
            return tuple(out)

        dvacc = _unrolled_loop(C, step, tuple(jnp.zeros((HEAD, LW), F32) for _ in groups), SCAN_UNROLL_BWD)
        for gsl, dva in zip(groups, dvacc):
            dvT_ref[0, :, gsl] = dva

        if nx:
            @pl.when(pl.program_id(0) == nC - 1)
            def _():
                wait()

    row = pl.BlockSpec((C, RW_WIDTH), lambda c: (nC - 1 - c, 0))
    ht = pl.BlockSpec((1, HEAD, RW_WIDTH), lambda c: (nC - 1 - c, 0, 0))
    hbm = pl.BlockSpec(memory_space=pl.ANY)
    rows_shape = jax.ShapeDtypeStruct((T, RW_WIDTH), F32)
    res = pl.pallas_call(
        body, name="rwkv_scan_bwd", grid=(nC,),
        in_specs=[row, row, row, row, row, ht, ht,
                  pl.BlockSpec((C, HEAD, RW_WIDTH), lambda c: (nC - 1 - c, 0, 0))] + [hbm] * nx,
        out_specs=[row, row, row, row, row, ht] + [hbm] * nx,
        out_shape=[rows_shape] * 5 + [jax.ShapeDtypeStruct((nC, HEAD, RW_WIDTH), F32)] + _exchange_shapes(exchange),
        scratch_shapes=[pltpu.VMEM((HEAD, RW_WIDTH), F32), pltpu.VMEM((HEAD, RW_WIDTH), F32)]
        + (_exchange_sems(nx) if nx else []),
        compiler_params=pltpu.CompilerParams(dimension_semantics=("arbitrary",), vmem_limit_bytes=VMEM_LIMIT_BYTES,
                                             has_side_effects=bool(nx)),
    )(a, w, b, k, r, vT, dyT, S_all, *[z for z, _ in exchange])
    return res[:6], res[6:]


def _alibi_slope(h):
    return float(np.float32(2.0 ** (-8.0 * (h + 1) / ATT_HEADS)))


def _att_scores(q, kcat, lane_head, hh, gi, d, L, n, steps, valid):
    hm = lane_head == hh
    kh = jnp.where(hm, kcat, 0.0).astype(BF16)
    s = lax.dot_general(q.astype(BF16), kh, NT_DIMS, preferred_element_type=F32) * (HEAD ** -0.5)
    s = s - (_alibi_slope(gi * 4 + hh) * d) * steps
    return hm, kh, jnp.where(valid, s, NEG_BIG)


def _att_mask(L, n):
    qi = lax.broadcasted_iota(jnp.int32, (L, 2 * L), 0)
    kj = lax.broadcasted_iota(jnp.int32, (L, 2 * L), 1)
    steps = qi + L - kj
    valid = (steps >= 0) & (steps <= L) & ((kj >= L) | (n > 0))
    return steps.astype(F32), valid


def att_fwd(pa, gi, T):
    window, d = ATT_GROUPS[gi]
    L = window // d
    Tj = T // d
    nb = Tj // L
    pv = pa.reshape(Tj, d * ATT_COLS)
    nblk = ATT_COLS // ATT_OUT

    def fn(pids, q, kp, kc, vp, vc):
        n = pids[1]
        steps, valid = _att_mask(L, n)
        lane_head = lax.shift_right_logical(lax.broadcasted_iota(jnp.int32, (1, ATT_OUT), 1), 6)
        kcat = jnp.concatenate([kp, kc], axis=0)
        vcat = jnp.concatenate([vp, vc], axis=0)
        o = jnp.zeros((L, ATT_OUT), F32)
        lseb = jnp.zeros((L, ATT_OUT), F32)
        for hh in range(4):
            hm, _, s = _att_scores(q, kcat, lane_head, hh, gi, d, L, n, steps, valid)
            m = jnp.max(s, axis=-1, keepdims=True)
            p = jnp.exp(s - m)
            l = jnp.sum(p, axis=-1, keepdims=True)
            vh = jnp.where(hm, vcat, 0.0).astype(BF16)
            o = o + jnp.dot(p.astype(BF16), vh, preferred_element_type=F32) / l
            lseb = jnp.where(hm, m + jnp.log(l), lseb)
        return o, lseb

    blk = (L, ATT_OUT)
    ins = [(pv, blk, lambda r, n: (n, r * nblk + gi)),
           (pv, blk, lambda r, n: (jnp.maximum(n - 1, 0), r * nblk + 3 + gi)),
           (pv, blk, lambda r, n: (n, r * nblk + 3 + gi)),
           (pv, blk, lambda r, n: (jnp.maximum(n - 1, 0), r * nblk + 6 + gi)),
           (pv, blk, lambda r, n: (n, r * nblk + 6 + gi))]
    out = ((Tj, d * ATT_OUT), F32, blk, lambda r, n: (n, r), None)
    o, lseb = tile_call(f"att_fwd_g{gi}", fn, (d, nb), ins, [out, out])
    return o.reshape(T, ATT_OUT), lseb.reshape(T, ATT_OUT)


def att_bwd(pa, o, lseb, do, dlseb, gi, T):
    window, d = ATT_GROUPS[gi]
    L = window // d
    Tj = T // d
    nb = Tj // L
    pv = pa.reshape(Tj, d * ATT_COLS)
    nblk = ATT_COLS // ATT_OUT
    view = lambda z: z.reshape(Tj, d * ATT_OUT)

    def body(q_ref, kp_ref, kc_ref, vp_ref, vc_ref, o_ref, l_ref, do_ref, dl_ref, dq_ref, dk_ref, dv_ref):
        n = pl.program_id(1)

        @pl.when(n == 0)
        def _():
            dk_ref[...] = jnp.zeros_like(dk_ref)
            dv_ref[...] = jnp.zeros_like(dv_ref)

        steps, valid = _att_mask(L, n)
        lane_head = lax.shift_right_logical(lax.broadcasted_iota(jnp.int32, (1, ATT_OUT), 1), 6)
        q = q_ref[...]
        kcat = jnp.concatenate([kp_ref[...], kc_ref[...]], axis=0)
        vcat = jnp.concatenate([vp_ref[...], vc_ref[...]], axis=0)
        o_t, l_t, do_t, dl_t = o_ref[...], l_ref[...], do_ref[...], dl_ref[...]
        dq = jnp.zeros((L, ATT_OUT), F32)
        dkc = jnp.zeros((2 * L, ATT_OUT), F32)
        dvc = jnp.zeros((2 * L, ATT_OUT), F32)
        for hh in range(4):
            hm, kh, s = _att_scores(q, kcat, lane_head, hh, gi, d, L, n, steps, valid)
            lse = jnp.max(jnp.where(hm, l_t, NEG_BIG), axis=-1, keepdims=True)
            p = jnp.exp(s - lse)
            vh = jnp.where(hm, vcat, 0.0).astype(BF16)
            do_h = jnp.where(hm, do_t, 0.0)
            dp = lax.dot_general(do_h.astype(BF16), vh, NT_DIMS, preferred_element_type=F32)
            delta = jnp.sum(do_h * o_t, axis=-1, keepdims=True)
            dlse = jnp.sum(jnp.where(hm, dl_t, 0.0), axis=-1, keepdims=True)
            ds = (p * (dp - delta + dlse)).astype(BF16)
            dq = dq + jnp.dot(ds, kh, preferred_element_type=F32)
            qh = jnp.where(hm, q, 0.0).astype(BF16)
            dkc = dkc + lax.dot_general(ds, qh, TN_DIMS, preferred_element_type=F32)
            dvc = dvc + lax.dot_general(p.astype(BF16), do_h.astype(BF16), TN_DIMS, preferred_element_type=F32)
        scale = HEAD ** -0.5
        dq_ref[...] = dq * scale
        cur = pl.ds(pl.multiple_of(n * L, L), L)
        dk_ref[cur, :] += dkc[L:] * scale
        dv_ref[cur, :] += dvc[L:]

        @pl.when(n > 0)
        def _():
            prev = pl.ds(pl.multiple_of((n - 1) * L, L), L)
            dk_ref[prev, :] += dkc[:L] * scale
            dv_ref[prev, :] += dvc[:L]

    blk = pl.BlockSpec((L, ATT_OUT), lambda r, n: (n, r))
    res = pl.BlockSpec((Tj, ATT_OUT), lambda r, n: (0, r))
    qspec = lambda off, prev: pl.BlockSpec(
        (L, ATT_OUT), (lambda r, n: (jnp.maximum(n - 1, 0), r * nblk + off + gi)) if prev
        else (lambda r, n: (n, r * nblk + off + gi)))
    shape = jax.ShapeDtypeStruct((Tj, d * ATT_OUT), F32)
    dq, dk, dv = pl.pallas_call(
        body, name=f"att_bwd_g{gi}", grid=(d, nb),
        in_specs=[qspec(0, False), qspec(3, True), qspec(3, False), qspec(6, True), qspec(6, False),
                  blk, blk, blk, blk],
        out_specs=[blk, res, res],
        out_shape=[shape, shape, shape],
        compiler_params=_cparams(2),
    )(pv, pv, pv, pv, pv, view(o), view(lseb), view(do), view(dlseb))
    return dq.reshape(T, ATT_OUT), dk.reshape(T, ATT_OUT), dv.reshape(T, ATT_OUT)


FFN_TM, FFN_TC = 512, 512


def _conv3(u, prev8, cw, cb):
    return cb + cw[0:1] * u + cw[1:2] * _shift_down(u, prev8, 1) + cw[2:3] * _shift_down(u, prev8, 2)


def conv_glu_fwd(u, conv_w, conv_b):
    T = u.shape[0]
    tm, tc = FFN_TM, FFN_TC
    nj, ni = D_FF // tc, T // tm

    def fn(pids, ug, ugh, uv, uvh, cwg, cbg, cwv, cbv):
        first = pids[1] > 0
        cg = _conv3(ug, jnp.where(first, ugh, 0.0), cwg, cbg)
        cv = _conv3(uv, jnp.where(first, uvh, 0.0), cwv, cbv)
        return _gelu_tanh(cg) * cv

    halo = lambda off: (lambda j, i: (jnp.maximum(i * (tm // 8) - 1, 0), j + off))
    ins = [(u, (tm, tc), lambda j, i: (i, j)), (u, (8, tc), halo(0)),
           (u, (tm, tc), lambda j, i: (i, j + nj)), (u, (8, tc), halo(nj)),
           (conv_w, (3, tc), lambda j, i: (0, j)), (conv_b, (1, tc), lambda j, i: (0, j)),
           (conv_w, (3, tc), lambda j, i: (0, j + nj)), (conv_b, (1, tc), lambda j, i: (0, j + nj))]
    out = ((T, D_FF), BF16, (tm, tc), lambda j, i: (i, j), None)
    return tile_call("conv_glu_fwd", fn, (nj, ni), ins, [out])[0]


def conv_glu_bwd(u, conv_w, conv_b, df):
    T = u.shape[0]
    tm, tc = FFN_TM, FFN_TC
    nj, ni = D_FF // tc, T // tm

    def fn(pids, ug, ugh, uv, uvh, cwg, cbg, cwv, cbv, df_t, nxt_g, nxt_v):
        i = ni - 1 - pids[1]
        ugh = jnp.where(i > 0, ugh, 0.0)
        uvh = jnp.where(i > 0, uvh, 0.0)
        cg = _conv3(ug, ugh, cwg, cbg)
        cv = _conv3(uv, uvh, cwv, cbv)
        _, vjp = jax.vjp(lambda g_, v_: _gelu_tanh(g_) * v_, cg, cv)
        dcg, dcv = vjp(df_t.astype(F32))
        cs = lambda z: jnp.sum(z, axis=0, keepdims=True)

        @pl.when(pids[1] == 0)
        def _():
            nxt_g[...] = jnp.zeros_like(nxt_g)
            nxt_v[...] = jnp.zeros_like(nxt_v)

        outs = []
        for dc, cw, nxt_ref in ((dcg, cwg, nxt_g), (dcv, cwv, nxt_v)):
            nxt = nxt_ref[...]
            outs.append(cw[0:1] * dc + cw[1:2] * _shift_up(dc, nxt, 1) + cw[2:3] * _shift_up(dc, nxt, 2))
            nxt_ref[...] = dc[:8]
        for dc, uu, hh in ((dcg, ug, ugh), (dcv, uv, uvh)):
            outs += [cs(dc * uu), cs(dc * _shift_down(uu, hh, 1)), cs(dc * _shift_down(uu, hh, 2)), cs(dc)]
        return outs

    rows = lambda off: (lambda j, r: (ni - 1 - r, j + off))
    halo = lambda off: (lambda j, r: (jnp.maximum((ni - 1 - r) * (tm // 8) - 1, 0), j + off))
    ins = [(u, (tm, tc), rows(0)), (u, (8, tc), halo(0)),
           (u, (tm, tc), rows(nj)), (u, (8, tc), halo(nj)),
           (conv_w, (3, tc), lambda j, r: (0, j)), (conv_b, (1, tc), lambda j, r: (0, j)),
           (conv_w, (3, tc), lambda j, r: (0, j + nj)), (conv_b, (1, tc), lambda j, r: (0, j + nj)),
           (df, (tm, tc), rows(0))]
    big = ((T, D_FF), BF16, (tm, tc), rows(0), None)
    acc = ((1, D_FF), F32, (1, tc), lambda j, r: (0, j), 1)
    res = tile_call("conv_glu_bwd", fn, (nj, ni), ins, [big, big] + [acc] * 8,
                    scratch=[((8, tc), F32), ((8, tc), F32)])
    dconv_w = jnp.concatenate([jnp.concatenate([res[2 + j], res[6 + j]], axis=1) for j in range(3)], axis=0)
    dconv_b = jnp.concatenate([res[5], res[9]], axis=1)
    return res[0], res[1], dconv_w, dconv_b


def _pad_cols(w, total):
    return jnp.pad(w, ((0, 0), (0, total - w.shape[1])))


def _pad_rows(w, total):
    return jnp.pad(w, ((0, total - w.shape[0]), (0, 0)))


def _proj_pad(w):
    z = lambda n: jnp.zeros((w.shape[0], n), w.dtype)
    return jnp.concatenate([w[:, :1600], z(64), w[:, 1600:1664], z(64), w[:, 1664:1824], z(96), w[:, 1824:],
                            z(PROJ_TAIL)], axis=1)


def _proj_unpad(g):
    return jnp.concatenate([g[:, :1600], g[:, OFF_XA:OFF_XA + 64], g[:, OFF_XG:OFF_XG + 160],
                            g[:, RW_PAD:RW_PAD + ATT_COLS]], axis=1)


def _rw_unpad(g):
    return jnp.concatenate([g[:, :1600], g[:, OFF_XA:OFF_XA + 64], g[:, OFF_XG:OFF_XG + 160]], axis=1)


def rms_fwd(name, x, g, tm=256):
    T, D = x.shape
    return tile_call(name, lambda pid, x_t, g_t: _rms(x_t, g_t), (T // tm,),
                     [_rows(x, tm), _par(g)], [_row_out(T, D, BF16, tm)])[0]


def rms_bwd(name, x, g, dh, dres, tm=256):
    T, D = x.shape

    def fn(pid, x_t, g_t, dh_t, dres_t):
        _, vjp = jax.vjp(_rms, x_t, g_t)
        dx, dg = vjp(dh_t.astype(F32))
        return dres_t + dx, dg

    return tile_call(name, fn, (T // tm,), [_rows(x, tm), _par(g), _rows(dh, tm), _rows(dres, tm)],
                     [_row_out(T, D, F32, tm), _acc_out(1, D)])


def local_step(x, p, target, W):
    T, D = x.shape
    G = {}

    w_in_p = W["w_in_p"]
    mu_p = _proj_pad(_pad_cols(W["rw_mu"], 4128))[:, :RW_PAD]
    w_up_p = _pad_rows(W["rw_w_up"], 128)
    a_up_p = _pad_rows(W["rw_a_up"], 128)
    g_up_p = _pad_rows(W["rw_g_up"], 256)
    r_k = W["rw_r_k"].reshape(1, RW_WIDTH)
    rw_params = [mu_p, W["rw_w0"], w_up_p, W["rw_a0"], a_up_p, g_up_p, W["rw_k_k"], W["rw_k_a"]]

    h = rms_fwd("rms_mix", x, W["g_mix"])
    proj = matmul("proj_in", h, w_in_p)
    gp = matmul("proj_gate", h, W["w_gate"])

    tm = 256
    rw_in = (proj, (tm, RW_PAD), lambda i: (i, 0))
    rw_halo = _prev_halo(proj, tm, RW_PAD)

    def rw_pre_tile(pid, Pc, halo, *params):
        prev8 = jnp.where(pid[0] > 0, halo, 0.0)
        params = [q.astype(F32) for q in params]
        return rw_pre(Pc, _shift_down(Pc, prev8, 1), *params)

    r, decay, k2, v, avec, bvec, g = tile_call(
        "rw_pre", rw_pre_tile, (T // tm,), [rw_in, rw_halo] + [_par(q) for q in rw_params],
        [_row_out(T, RW_WIDTH, F32, tm)] * 7)

    vT = _to_head_time(v).astype(BF16)
    wa, ba, ka = scan_pair_terms(avec, decay, bvec, k2)
    (yT, S_all), late_slots = rwkv_scan_fwd(avec, decay, bvec, k2, r, vT, wa, ba, ka,
                                            exchange=_late_weight_sources(W))
    W = dict(W, **_late_weights(late_slots))
    y = _from_head_time(yT)

    post_params = [W["rw_ln_g"], W["rw_ln_b"], r_k]
    ya = tile_call("rw_post", lambda pid, *t: rw_post(*t), (T // tm,),
                   [_rows(z, tm) for z in (y, r, k2, v, g)] + [_par(q) for q in post_params],
                   [_row_out(T, RW_WIDTH, BF16, tm)])[0]

    pa = proj[:, RW_PAD:RW_PAD + ATT_COLS]
    att = [att_fwd(pa, gi, T) for gi in range(3)]
    o_l = [att[0][0], att[1][0], att[2][0], att[0][1], att[1][1], att[2][1]]
    yb = tile_call("att_combine", lambda pid, *t: att_combine(*t), (T // tm,),
                   [_rows(z, tm) for z in o_l], [_row_out(T, ATT_OUT, BF16, tm)])[0]

    za = matmul("branch_a", ya, W["w_branch_a"])
    zb = matmul("branch_b", yb, W["w_branch_b"])
    merged = tile_call("merge", lambda pid, *t: merge_fn(*t), (T // tm,),
                       [_rows(gp, tm), _par(W["b_gate"]), _rows(za, tm), _rows(zb, tm)],
                       [_row_out(T, D, BF16, tm)])[0]
    x1 = matmul("mix_out", merged, W["w_out"], res=x)

    h2 = rms_fwd("rms_ffn", x1, W["g_ffn"])
    u = matmul("ffn_up", h2, W["w_up"])
    f = conv_glu_fwd(u, W["conv_w"], W["conv_b"])
    x2 = matmul("ffn_down", f, W["w_down"], res=x1)

    h3 = rms_fwd("rms_ple", x2, W["g_ple"])
    zg = matmul("ple_gate", h3, W["w_ple_gate"])
    pe = matmul("ple_embed", p, W["w_ple"])

    def tail_tile(pid, x2_t, zg_t, pe_t, gf, tgt):
        loss, vjp = jax.vjp(lambda a_, b_, c_, d_: tail_loss(a_, b_, c_, d_, tgt), x2_t, zg_t, pe_t, gf)
        dx2, dzg, dpe, dgf = vjp(jnp.ones((), F32))
        return dx2, dzg, dpe, dgf, jnp.full((1, 128), loss, F32)

    tmt = 128
    dx3, dzg, dpe, dgf, loss_acc = tile_call(
        "tail_loss", tail_tile, (T // tmt,),
        [_rows(x2, tmt), _rows(zg, tmt), _rows(pe, tmt), _par(W["g_final"]), _rows(target, tmt)],
        [_row_out(T, D, F32, tmt), _row_out(T, D, BF16, tmt), _row_out(T, D, BF16, tmt),
         _acc_out(1, D), _acc_out(1, 128)])
    loss = loss_acc[0, 0]
    G["g_final"] = dgf

    wgrad = functools.partial(matmul, mode="tn", out_dtype=GRAD_WIRE)
    G["w_ple"] = wgrad("d_w_ple", p, dpe)
    G["w_ple_gate"] = wgrad("d_w_ple_gate", h3, dzg)
    dh3 = matmul("d_h3", dzg, W["w_ple_gate"], "nt")
    dx2, G["g_ple"] = rms_bwd("rms_ple_bwd", x2, W["g_ple"], dh3, dx3)

    dx2b = dx2.astype(BF16)
    G["w_down"] = wgrad("d_w_down", f, dx2b)
    df = matmul("d_f", dx2b, W["w_down"], "nt", out_dtype=BF16)
    du_g, du_v, G["conv_w"], G["conv_b"] = conv_glu_bwd(u, W["conv_w"], W["conv_b"], df)
    du = jnp.concatenate([du_g, du_v], axis=1)
    G["w_up"] = wgrad("d_w_up", h2, du)
    dh2 = matmul("d_h2", du, W["w_up"], "nt")
    dx1, G["g_ffn"] = rms_bwd("rms_ffn_bwd", x1, W["g_ffn"], dh2, dx2)

    dx1b = dx1.astype(BF16)
    G["w_out"] = wgrad("d_w_out", merged, dx1b)
    dmerged = matmul("d_merged", dx1b, W["w_out"], "nt", out_dtype=BF16)

    def merge_bwd_tile(pid, gp_t, bg, za_t, zb_t, dm_t):
        _, vjp = jax.vjp(merge_fn, gp_t, bg, za_t, zb_t)
        return vjp(dm_t.astype(F32))

    dgp, G["b_gate"], dza, dzb = tile_call(
        "merge_bwd", merge_bwd_tile, (T // tm,),
        [_rows(gp, tm), _par(W["b_gate"]), _rows(za, tm), _rows(zb, tm), _rows(dmerged, tm)],
        [_row_out(T, 2 * D, BF16, tm), _acc_out(1, 2 * D), _row_out(T, D, BF16, tm), _row_out(T, D, BF16, tm)])
    G["w_branch_a"] = wgrad("d_w_branch_a", ya, dza)
    dya = matmul("d_ya", dza, W["w_branch_a"], "nt")
    G["w_branch_b"] = wgrad("d_w_branch_b", yb, dzb)
    dyb = matmul("d_yb", dzb, W["w_branch_b"], "nt")
    G["w_gate"] = wgrad("d_w_gate", h, dgp)
    dh_gate = matmul("d_h_gate", dgp, W["w_gate"], "nt")

    def comb_bwd_tile(pid, *t):
        _, vjp = jax.vjp(att_combine, *t[:6])
        return vjp(t[6])

    d_ol = tile_call("att_combine_bwd", comb_bwd_tile, (T // tm,),
                     [_rows(z, tm) for z in o_l] + [_rows(dyb, tm)],
                     [_row_out(T, ATT_OUT, F32, tm)] * 6)
    dqkv = [att_bwd(pa, att[gi][0], att[gi][1], d_ol[gi], d_ol[3 + gi], gi, T) for gi in range(3)]
    d_att = [dqkv[gi][j] for j in range(3) for gi in range(3)]

    def post_bwd_tile(pid, *t):
        _, vjp = jax.vjp(rw_post, *t[:8])
        return vjp(t[8])

    dy, dr_p, dk2_p, dv_p, dg, G["rw_ln_g"], G["rw_ln_b"], d_rk = tile_call(
        "rw_post_bwd", post_bwd_tile, (T // tm,),
        [_rows(z, tm) for z in (y, r, k2, v, g)] + [_par(q) for q in post_params] + [_rows(dya, tm)],
        [_row_out(T, RW_WIDTH, F32, tm)] * 5 + [_acc_out(1, RW_WIDTH)] * 3)
    G["rw_r_k"] = d_rk.reshape(W["rw_r_k"].shape)

    (da, dw, db, dk_s, dr_s, dvT), G["_early_parts"] = rwkv_scan_bwd(
        avec, decay, bvec, k2, r, vT, _to_head_time(dy).astype(BF16), S_all, exchange=_early_grad_sources(G))
    dv_s = _from_head_time(dvT)

    tmb = 128
    rw_in_b = (proj, (tmb, RW_PAD), lambda i: (i, 0))

    def pre_bwd_tile(pid, Pc, halo, *t):
        prev8 = jnp.where(pid[0] > 0, halo, 0.0)
        params = [q.astype(F32) for q in t[:8]]
        dr1, dr2, dw_, dk1, dk2_, dv1, dv2, da_, db_, dg_ = t[8:]
        _, vjp = jax.vjp(rw_pre, Pc, _shift_down(Pc, prev8, 1), *params)
        return vjp((dr1 + dr2, dw_, dk1 + dk2_, dv1 + dv2, da_, db_, dg_))

    cts = (dr_s, dr_p, dw, dk_s, dk2_p, dv_s, dv_p, da, db, dg)
    res = tile_call(
        "rw_pre_bwd", pre_bwd_tile, (T // tmb,),
        [rw_in_b, _prev_halo(proj, tmb, RW_PAD)] + [_par(q) for q in rw_params] + [_rows(z, tmb) for z in cts],
        [_row_out(T, RW_PAD, F32, tmb)] * 2 + [_acc_out(*q.shape) for q in rw_params])
    dPc, dPs = res[0], res[1]
    d_mu, G["rw_w0"], d_wup, G["rw_a0"], d_aup, d_gup, G["rw_k_k"], G["rw_k_a"] = res[2:]
    G["rw_mu"] = _rw_unpad(d_mu)
    G["rw_w_up"], G["rw_a_up"], G["rw_g_up"] = d_wup[:64], d_aup[:64], d_gup[:160]

    def dproj_tile(pid, dPc_t, dPs_t, nxt, *att_t):
        nxt = jnp.where(pid[0] < T // tm - 1, nxt, 0.0)
        tail = jnp.zeros((dPc_t.shape[0], PROJ_TAIL), F32)
        return jnp.concatenate([dPc_t + _shift_up(dPs_t, nxt, 1)] + list(att_t) + [tail], axis=1)

    dproj = tile_call("d_proj", dproj_tile, (T // tm,),
                      [_rows(dPc, tm), _rows(dPs, tm), _next_halo(dPs, tm, RW_PAD, T)] + [_rows(z, tm) for z in d_att],
                      [_row_out(T, PROJ_PAD, BF16, tm)])[0]
    G["w_in_p"] = wgrad("d_w_in", h, dproj)
    dh = matmul("d_h", dproj, w_in_p, "nt", res=dh_gate)
    dx, G["g_mix"] = rms_bwd("rms_mix_bwd", x, W["g_mix"], dh, dx1)
    return loss, dx, G


def _mesh_pos():
    return lax.axis_index("x"), lax.axis_index("y"), lax.axis_index("c")


def _peer(pos, k):
    x, y, c = pos
    px = 1 - x if k & 4 else x
    py = 1 - y if k & 2 else y
    pc = 1 - c if k & 1 else c
    return (px, py, pc), 4 * px + 2 * py + pc


def all_gather_blocks(name, blocks):
    n = len(blocks)

    def body(*refs):
        x_refs, out_refs = refs[:n], refs[n:2 * n]
        send_sems, recv_sems, local_sems = refs[2 * n:]
        x, y, c = _mesh_pos()
        me, sibling = (x, y, c), (x, y, 1 - c)
        chips = [(1 - x, y), (x, 1 - y), (1 - x, 1 - y)]
        ops = range(n)

        def slot(i, px, py, pc):
            return out_refs[i].at[4 * px + 2 * py + pc]

        def copy(k, i, block, to, own=False):
            return pltpu.make_async_remote_copy(
                src_ref=x_refs[i] if own else slot(i, *block), dst_ref=slot(i, *block),
                send_sem=send_sems.at[k, i], recv_sem=recv_sems.at[k, i],
                device_id=to, device_id_type=pl.DeviceIdType.MESH)

        mine = [pltpu.make_async_copy(x_refs[i], slot(i, *me), local_sems.at[i]) for i in ops]
        first = [copy(0, i, me, sibling, own=True) for i in ops]
        first += [copy(1 + j, i, me, (*chip, c), own=True) for j, chip in enumerate(chips) for i in ops]
        for cp in mine + first:
            cp.start()
        passed = []
        for j, chip in enumerate(chips):
            for i in ops:
                copy(1 + j, i, (*chip, c), me).wait_recv()
                passed.append(copy(4 + j, i, (*chip, c), sibling))
                passed[-1].start()
        for i in ops:
            copy(0, i, sibling, me).wait_recv()
        for j, chip in enumerate(chips):
            for i in ops:
                copy(4 + j, i, (*chip, 1 - c), me).wait_recv()
        for cp in first + passed:
            cp.wait_send()
        for cp in mine:
            cp.wait()

    return pl.pallas_call(
        body, name=name,
        in_specs=[pl.BlockSpec(memory_space=pl.ANY)] * n,
        out_specs=[pl.BlockSpec(memory_space=pl.ANY)] * n,
        out_shape=[jax.ShapeDtypeStruct((N_DEV,) + b.shape, b.dtype) for b in blocks],
        scratch_shapes=[pltpu.SemaphoreType.DMA((N_DEV - 1, n)), pltpu.SemaphoreType.DMA((N_DEV - 1, n)),
                        pltpu.SemaphoreType.DMA((n,))],
        compiler_params=pltpu.CompilerParams(has_side_effects=True),
    )(*blocks)


WHOLE = 0


def _exchange_shapes(srcs):
    shapes = [a.shape[1:] if cols is None else a.shape if cols == WHOLE else (a.shape[0], cols) for a, cols in srcs]
    return [jax.ShapeDtypeStruct((N_DEV,) + s, a.dtype) for s, (a, _) in zip(shapes, srcs)]


def _exchange_sems(n):
    return [pltpu.SemaphoreType.DMA((N_DEV - 1, n)), pltpu.SemaphoreType.DMA((N_DEV - 1, n)),
            pltpu.SemaphoreType.DMA((n,))]


def _exchange_ops(col_widths, x_refs, out_refs, send_sems, recv_sems, local_sems):
    n = len(col_widths)
    pos = _mesh_pos()
    me = 4 * pos[0] + 2 * pos[1] + pos[2]

    def piece(i, d):
        cols = col_widths[i]
        if cols is None:
            return x_refs[i].at[d]
        if cols == WHOLE:
            return x_refs[i]
        return x_refs[i].at[:, pl.ds(pl.multiple_of(d * cols, 128), cols)]

    def local(i):
        return pltpu.make_async_copy(piece(i, me), out_refs[i].at[me], local_sems.at[i])

    def remote(k, i, landing):
        peer, idx = _peer(pos, k)
        return pltpu.make_async_remote_copy(
            src_ref=piece(i, idx), dst_ref=out_refs[i].at[idx if landing else me],
            send_sem=send_sems.at[k - 1, i], recv_sem=recv_sems.at[k - 1, i],
            device_id=peer, device_id_type=pl.DeviceIdType.MESH)

    pairs = [(k, i) for k in range(1, N_DEV) for i in range(n)]

    def start():
        for i in range(n):
            local(i).start()
        for k, i in pairs:
            remote(k, i, False).start()

    def wait():
        for k, i in pairs:
            remote(k, i, True).wait_recv()
        for k, i in pairs:
            remote(k, i, False).wait_send()
        for i in range(n):
            local(i).wait()

    return start, wait


def all_to_all_blocks(name, srcs):
    n = len(srcs)

    def body(*refs):
        start, wait = _exchange_ops([c for _, c in srcs], refs[:n], refs[n:2 * n], *refs[2 * n:])
        start()
        wait()

    return pl.pallas_call(
        body, name=name,
        in_specs=[pl.BlockSpec(memory_space=pl.ANY)] * n,
        out_specs=[pl.BlockSpec(memory_space=pl.ANY)] * n,
        out_shape=_exchange_shapes(srcs),
        scratch_shapes=_exchange_sems(n),
        compiler_params=pltpu.CompilerParams(has_side_effects=True),
    )(*[a for a, _ in srcs])


def _adam_row_tile(R, C):
    best = None
    for t in range(16, R + 1, 16):
        if R % t == 0 and t * C <= ADAM_TILE_ELEMS:
            best = t
    return best if best is not None else R


def reduce_adamw(name, parts, w, m, v):
    _, R, C = parts.shape
    tr = _adam_row_tile(R, C)

    def fn(pid, parts_t, w_t, m_t, v_t):
        g = parts_t[0].astype(F32)
        for i in range(1, N_DEV):
            g = g + parts_t[i].astype(F32)
        m_n = ADAM_B1 * m_t + (1.0 - ADAM_B1) * g
        v_n = ADAM_B2 * v_t + (1.0 - ADAM_B2) * (g * g)
        m_hat = m_n / (1.0 - ADAM_B1 ** ADAM_STEP)
        v_hat = v_n / (1.0 - ADAM_B2 ** ADAM_STEP)
        delta = -ADAM_LR * (m_hat / (jnp.sqrt(v_hat) + ADAM_EPS) + ADAM_WD * w_t)
        return g, delta, m_n, v_n

    row = lambda a: (a, (tr, C), lambda i: (i, 0))
    out = ((R, C), F32, (tr, C), lambda i: (i, 0), None)
    return tile_call(name, fn, (R // tr,),
                     [(parts, (N_DEV, tr, C), lambda i: (0, i, 0)), row(w), row(m), row(v)], [out] * 4)


PARAMS = (
    ("g_mix", (1, 1024), None), ("w_in", (1024, 4128), 1), ("rw_mu", (1, 1824), None), ("rw_w0", (1, 512), None),
    ("rw_w_up", (64, 512), 1), ("rw_a0", (1, 512), None), ("rw_a_up", (64, 512), 1), ("rw_g_up", (160, 512), 1),
    ("rw_k_k", (1, 512), None), ("rw_k_a", (1, 512), None), ("rw_r_k", (8, 64), None), ("rw_ln_g", (1, 512), None),
    ("rw_ln_b", (1, 512), None), ("w_branch_a", (512, 1024), 1), ("w_branch_b", (256, 1024), 1),
    ("w_gate", (1024, 2048), 1), ("b_gate", (1, 2048), None), ("w_out", (1024, 1024), 0), ("g_ffn", (1, 1024), None),
    ("w_up", (1024, 6144), 1), ("conv_w", (3, 6144), 1), ("conv_b", (1, 6144), None), ("w_down", (3072, 1024), 0),
    ("g_ple", (1, 1024), None), ("w_ple_gate", (1024, 1024), 0), ("w_ple", (256, 1024), 1), ("g_final", (1, 1024), None),
)
SHARDED = tuple(q for q in PARAMS if q[2] is not None)
REPLICATED = tuple(q for q in PARAMS if q[2] is None)
BIG_NAMES = ("w_in", "w_up", "w_gate", "w_out", "w_down", "w_ple_gate", "w_branch_a", "w_branch_b", "w_ple")
BIG = tuple(q for q in SHARDED if q[0] in BIG_NAMES)
SMALL_SHARDED = tuple(q for q in SHARDED if q[0] not in BIG_NAMES)
PACK_COLS = 1024
F32_GATHERED = ("conv_w",)


def _local_shape(shape, axis):
    s = list(shape)
    s[axis] //= N_DEV
    return tuple(s)


def _numel(shape):
    return int(np.prod(shape))


def _pad_flat(z, mult):
    n = z.shape[-1]
    total = -(-n // mult) * mult
    return jnp.pad(z, [(0, 0)] * (z.ndim - 1) + [(0, total - n)])


def _full_from_slots(slots, shape, axis):
    loc = _local_shape(shape, axis)
    z = slots.reshape((N_DEV,) + loc)
    if axis == 0:
        return z.reshape(shape)
    return z.transpose(1, 0, 2).reshape(shape)


def _slots_from_full(full, shape, axis):
    loc = _local_shape(shape, axis)
    if axis == 0:
        return full.reshape(N_DEV, _numel(loc))
    return full.reshape(shape[0], N_DEV, loc[1]).transpose(1, 0, 2).reshape(N_DEV, _numel(loc))


W_IN_SLOT = 640
W_IN_LOCAL = 4128 // N_DEV


def _block_shape(shape, axis):
    return _local_shape(shape, axis) if axis is not None else shape


def _pad_w_in(block):
    return jnp.pad(block, ((0, 0), (0, W_IN_SLOT - W_IN_LOCAL)))


def _proj_col(s):
    return s + jnp.where(s >= 1600, 64, 0) + jnp.where(s >= 1664, 64, 0) + jnp.where(s >= 1824, 96, 0)


def _perm_tile(d, c0, width):
    j = lax.broadcasted_iota(jnp.int32, (W_IN_SLOT, width), 0)
    c = c0 + lax.broadcasted_iota(jnp.int32, (W_IN_SLOT, width), 1)
    hit = (_proj_col(d * W_IN_LOCAL + j) == c) & (j < W_IN_LOCAL)
    return jnp.where(hit, 1.0, 0.0).astype(BF16)


PERM_TILE = 768


def w_in_unshuffle(slots):
    _, K, _ = slots.shape
    tn = PERM_TILE

    def body(a_ref, o_ref, acc_ref):
        j, d = pl.program_id(0), pl.program_id(1)

        @pl.when(d == 0)
        def _():
            acc_ref[...] = jnp.zeros_like(acc_ref)

        acc_ref[...] += jnp.dot(a_ref[0], _perm_tile(d, j * tn, tn), preferred_element_type=F32)

        @pl.when(d == N_DEV - 1)
        def _():
            o_ref[...] = acc_ref[...].astype(o_ref.dtype)

    return pl.pallas_call(
        body, name="w_in_unshuffle", grid=(PROJ_PAD // tn, N_DEV),
        in_specs=[pl.BlockSpec((1, K, W_IN_SLOT), lambda j, d: (d, 0, 0))],
        out_specs=pl.BlockSpec((K, tn), lambda j, d: (0, j)),
        out_shape=jax.ShapeDtypeStruct((K, PROJ_PAD), BF16),
        scratch_shapes=[pltpu.VMEM((K, tn), F32)],
        compiler_params=_cparams(2),
    )(slots)


def w_in_shuffle_grad(dw):
    K = dw.shape[0]
    tk = PERM_TILE

    def first_tile(d):
        return _proj_col(d * W_IN_LOCAL) // tk

    def body(g_ref, o_ref, acc_ref):
        d, kk = pl.program_id(0), pl.program_id(1)
        perm = _perm_tile(d, (first_tile(d) + kk) * tk, tk)
        part = lax.dot_general(g_ref[...].astype(BF16), perm, NT_DIMS, preferred_element_type=F32)

        @pl.when(kk == 0)
        def _():
            acc_ref[...] = part

        @pl.when(kk == 1)
        def _():
            o_ref[0] = (acc_ref[...] + part).astype(o_ref.dtype)

    return pl.pallas_call(
        body, name="w_in_shuffle_grad", grid=(N_DEV, 2),
        in_specs=[pl.BlockSpec((K, tk), lambda d, kk: (0, first_tile(d) + kk))],
        out_specs=pl.BlockSpec((1, K, W_IN_SLOT), lambda d, kk: (d, 0, 0)),
        out_shape=jax.ShapeDtypeStruct((N_DEV, K, W_IN_SLOT), GRAD_WIRE),
        scratch_shapes=[pltpu.VMEM((K, W_IN_SLOT), F32)],
        compiler_params=_cparams(2),
    )(dw)


def _flat_rows(pieces, dtype, row_mult):
    flat = jnp.concatenate([z.astype(dtype) for z in pieces], axis=-1)
    flat = _pad_flat(flat, row_mult * PACK_COLS)
    return flat.reshape(flat.shape[:-1] + (-1, PACK_COLS))


FIRST = tuple(q for q in BIG if q[0] in ("w_in", "w_gate"))
LATE = tuple(q for q in BIG if q not in FIRST)


def _matrix_from_slots(slots, shape, axis):
    return slots.reshape(shape) if axis == 0 else slots.transpose(1, 0, 2).reshape(shape)


def _late_weight_sources(W):
    return [(blk, WHOLE) for blk in W["_late_blocks"]]


def _late_weights(slots):
    return {n: _matrix_from_slots(s, shape, axis) for (n, shape, axis), s in zip(LATE, slots)}


def gather_weights(local):
    blocks = [(_pad_w_in(local[n]) if n == "w_in" else local[n]).astype(BF16) for n, _, _ in FIRST]
    small = [q for q in SMALL_SHARDED if q[0] not in F32_GATHERED]
    exact = [q for q in SMALL_SHARDED if q[0] in F32_GATHERED]
    blocks.append(_flat_rows([local[n].reshape(-1) for n, _, _ in small], BF16, 16))
    blocks.append(_flat_rows([local[n].reshape(-1) for n, _, _ in exact], F32, 8))
    got = all_gather_blocks("weight_all_gather", blocks)
    full = {"_late_blocks": [local[n].astype(BF16) for n, _, _ in LATE]}
    for (n, shape, axis), slots in zip(FIRST, got):
        if n == "w_in":
            full["w_in_p"] = w_in_unshuffle(slots)
        else:
            full[n] = _matrix_from_slots(slots, shape, axis)
    for group, slots in ((small, got[-2]), (exact, got[-1])):
        slots, off = slots.reshape(N_DEV, -1), 0
        for n, shape, axis in group:
            size = _numel(_local_shape(shape, axis))
            full[n] = _full_from_slots(slots[:, off:off + size], shape, axis)
            off += size
    for n, _, _ in REPLICATED:
        full[n] = local[n]
    return full


LOSS_SLOT = ("_loss", (1, 2), None)
PACKED_SMALL = SMALL_SHARDED + REPLICATED + (LOSS_SLOT,)


def _pack_small(vals):
    pieces = [vals[n].reshape(-1) if n in vals else jnp.zeros((_numel(shape),), F32) for n, shape, _ in PACKED_SMALL]
    return _flat_rows(pieces, F32, 16)


def _unpack_small(packed):
    flat, out, off = packed.reshape(-1), {}, 0
    for n, shape, axis in PACKED_SMALL:
        loc = _block_shape(shape, axis)
        out[n] = flat[off:off + _numel(loc)].reshape(loc)
        off += _numel(loc)
    return out


EARLY = tuple(q for q in BIG if q[0] != "w_in")


def _early_grad_sources(G):
    srcs = []
    for n, shape, axis in EARLY:
        if axis == 0:
            srcs.append((G[n].astype(GRAD_WIRE).reshape((N_DEV,) + _local_shape(shape, axis)), None))
        else:
            srcs.append((G[n].astype(GRAD_WIRE), shape[1] // N_DEV))
    return srcs


def _late_grad_sources(G, loss_local):
    srcs = [(w_in_shuffle_grad(G["w_in_p"]), None)]
    rows = [_slots_from_full(G[n].reshape(shape), shape, axis) for n, shape, axis in SMALL_SHARDED]
    loss_hi = loss_local.astype(GRAD_WIRE).astype(F32)
    rep = jnp.concatenate([G[n].reshape(-1) for n, _, _ in REPLICATED] + [jnp.stack([loss_hi, loss_local - loss_hi])])
    rows.append(jnp.broadcast_to(rep[None, :], (N_DEV, rep.shape[0])))
    srcs.append((_flat_rows(rows, GRAD_WIRE, 16), None))
    return srcs


def _step(x, p, target, local_w, local_m, local_v):
    full = gather_weights(local_w)
    loss_local, dx, G = local_step(x, p, target, full)
    late = all_to_all_blocks("grad_all_to_all", _late_grad_sources(G, loss_local))
    parts = [late[0]] + list(G["_early_parts"]) + [late[1]]
    outs = [{}, {}, {}, {}]
    for (n, shape, axis), part in zip((BIG[0],) + EARLY, parts):
        prep = _pad_w_in if n == "w_in" else (lambda z: z)
        res = reduce_adamw("adamw_" + n, part, prep(local_w[n]), prep(local_m[n]), prep(local_v[n]))
        for o, z in zip(outs, res):
            o[n] = z[:, :W_IN_LOCAL] if n == "w_in" else z
    res = reduce_adamw("adamw_small", parts[-1], _pack_small(local_w), _pack_small(local_m), _pack_small(local_v))
    for o, z in zip(outs, res):
        o.update(_unpack_small(z))
    loss = jnp.sum(outs[0]["_loss"])
    return loss, dx, outs


def kernel(x, p, g_mix, w_in, rw_mu, rw_w0, rw_w_up, rw_a0, rw_a_up, rw_g_up, rw_k_k, rw_k_a, rw_r_k, rw_ln_g, rw_ln_b, w_branch_a, w_branch_b, w_gate, b_gate, w_out, g_ffn, w_up, conv_w, conv_b, w_down, g_ple, w_ple_gate, w_ple, g_final, loss_target, m_g_mix, m_w_in, m_rw_mu, m_rw_w0, m_rw_w_up, m_rw_a0, m_rw_a_up, m_rw_g_up, m_rw_k_k, m_rw_k_a, m_rw_r_k, m_rw_ln_g, m_rw_ln_b, m_w_branch_a, m_w_branch_b, m_w_gate, m_b_gate, m_w_out, m_g_ffn, m_w_up, m_conv_w, m_conv_b, m_w_down, m_g_ple, m_w_ple_gate, m_w_ple, m_g_final, v_g_mix, v_w_in, v_rw_mu, v_rw_w0, v_rw_w_up, v_rw_a0, v_rw_a_up, v_rw_g_up, v_rw_k_k, v_rw_k_a, v_rw_r_k, v_rw_ln_g, v_rw_ln_b, v_w_branch_a, v_w_branch_b, v_w_gate, v_b_gate, v_w_out, v_g_ffn, v_w_up, v_conv_w, v_conv_b, v_w_down, v_g_ple, v_w_ple_gate, v_w_ple, v_g_final):
    args = dict(locals())
    names = [n for n, _, _ in PARAMS]
    orig_shape = {n: args[n].shape for n in names}

    def strip(prefix):
        out = {}
        for n, shape, axis in PARAMS:
            a = args[prefix + n]
            loc = _local_shape(shape, axis) if axis is not None else shape
            out[n] = a.reshape(loc)
        return out

    local_w, local_m, local_v = strip(""), strip("m_"), strip("v_")
    T, D = x.shape[-2], x.shape[-1]
    loss, dx, (g, delta, m_n, v_n) = _step(x.reshape(T, D), p.reshape(T, p.shape[-1]), loss_target.reshape(T, D),
                                           local_w, local_m, local_v)
    outs = [loss, dx.reshape(x.shape)]
    for group in (g, delta, m_n, v_n):
        outs += [group[n].reshape(orig_shape[n]) for n in names]
    return tuple(outs)
```

```python
import functools
import math

import numpy as np
import jax
import jax.numpy as jnp
from jax import lax
from jax.experimental import pallas as pl
from jax.experimental.pallas import tpu as pltpu

F32 = jnp.float32
BF16 = jnp.bfloat16
GRAD_WIRE = jnp.bfloat16

N_DEV = 8
NORM_EPS = 1e-6
RW_LN_EPS = 64e-5
HEAD = 64
RW_WIDTH = 512
ATT_GROUPS = ((128, 1), (512, 4), (2048, 16))
ATT_HEADS = 12
ATT_OUT = 256
ATT_COLS = 2304
OFF_XW, OFF_XA, OFF_XG, RW_PAD, PROJ_PAD = 1536, 1664, 1792, 2048, 4608
PROJ_TAIL = PROJ_PAD - RW_PAD - ATT_COLS
D_FF = 3072

ADAM_LR, ADAM_B1, ADAM_B2, ADAM_EPS, ADAM_WD, ADAM_STEP = 0.001, 0.9, 0.999, 1e-08, 0.01, 10

VMEM_LIMIT_BYTES = 56 * 1024 * 1024
ADAM_TILE_ELEMS = 256 * 1024
NEG_BIG = -1e30

NT_DIMS = (((1,), (1,)), ((), ()))
TN_DIMS = (((0,), (0,)), ((), ()))
NN_DIMS = (((1,), (0,)), ((), ()))


def _cparams(n_axes):
    return pltpu.CompilerParams(dimension_semantics=("arbitrary",) * n_axes,
                                vmem_limit_bytes=VMEM_LIMIT_BYTES)


def _split2(x):
    hi = x.astype(BF16)
    lo = (x - hi.astype(F32)).astype(BF16)
    return hi, lo


def _seg_mat(n):
    r = lax.shift_right_logical(lax.broadcasted_iota(jnp.int32, (n, n), 0), 6)
    c = lax.shift_right_logical(lax.broadcasted_iota(jnp.int32, (n, n), 1), 6)
    return jnp.where(r == c, 1.0, 0.0).astype(BF16)


def _segb(x, seg):
    hi, lo = _split2(x)
    return (jnp.dot(hi, seg, preferred_element_type=F32)
            + jnp.dot(lo, seg, preferred_element_type=F32))


def _segb1(x, seg):
    return jnp.dot(x.astype(BF16), seg, preferred_element_type=F32)


@jax.custom_vjp
def segsum(x):
    return _segb(x, _seg_mat(x.shape[1]))


def _segsum_fwd(x):
    return segsum(x), None


def _segsum_bwd(_, ct):
    return (segsum(ct),)


segsum.defvjp(_segsum_fwd, _segsum_bwd)


@jax.custom_vjp
def bdot(a, b):
    return jnp.dot(a.astype(BF16), b.astype(BF16), preferred_element_type=F32)


def _bdot_fwd(a, b):
    return bdot(a, b), (a, b)


def _bdot_bwd(res, ct):
    a, b = res
    ctb = ct.astype(BF16)
    da = lax.dot_general(ctb, b.astype(BF16), NT_DIMS, preferred_element_type=F32)
    db = lax.dot_general(a.astype(BF16), ctb, TN_DIMS, preferred_element_type=F32)
    return da.astype(a.dtype), db.astype(b.dtype)


bdot.defvjp(_bdot_fwd, _bdot_bwd)


def _sig(x):
    return 1.0 / (1.0 + jnp.exp(-x))


def _softplus(z):
    return jnp.maximum(z, 0.0) + jnp.log(1.0 + jnp.exp(-jnp.abs(z)))


def _gelu_tanh(x):
    return 0.5 * x * (1.0 + jnp.tanh(0.7978845608028654 * (x + 0.044715 * (x * x * x))))


def _rms(x, g):
    return x * lax.rsqrt(jnp.mean(x * x, axis=-1, keepdims=True) + NORM_EPS) * g


def _shift_down(x, prev8, n):
    rolled = pltpu.roll(x, n, 0)
    top = pltpu.roll(prev8, n, 0)
    rid = lax.broadcasted_iota(jnp.int32, (8, x.shape[1]), 0)
    head = jnp.where(rid < n, top, rolled[:8])
    return jnp.concatenate([head, rolled[8:]], axis=0)


def _shift_up(x, next8, n):
    rows = x.shape[0]
    rolled = pltpu.roll(x, rows - n, 0)
    bottom = pltpu.roll(next8, 8 - n, 0)
    rid = lax.broadcasted_iota(jnp.int32, (8, x.shape[1]), 0)
    tail = jnp.where(rid >= 8 - n, bottom, rolled[rows - 8:])
    return jnp.concatenate([rolled[:rows - 8], tail], axis=0)


def tile_call(name, fn, grid, ins, outs, scratch=()):
    n_in, n_out = len(ins), len(outs)
    acc_axes = [o[4] for o in outs]

    def body(*refs):
        pids = tuple(pl.program_id(a) for a in range(len(grid)))
        vals = fn(pids, *[r[...] for r in refs[:n_in]], *refs[n_in + n_out:])
        if not isinstance(vals, (tuple, list)):
            vals = (vals,)
        for o_ref, val, ax in zip(refs[n_in:n_in + n_out], vals, acc_axes):
            if ax is None:
                o_ref[...] = val.astype(o_ref.dtype)
            else:
                @pl.when(pids[ax] == 0)
                def _(o_ref=o_ref):
                    o_ref[...] = jnp.zeros_like(o_ref)

                o_ref[...] += val.astype(o_ref.dtype)

    res = pl.pallas_call(
        body, name=name, grid=grid,
        in_specs=[pl.BlockSpec(b, im) for _, b, im in ins],
        out_specs=[pl.BlockSpec(o[2], o[3]) for o in outs],
        out_shape=[jax.ShapeDtypeStruct(o[0], o[1]) for o in outs],
        scratch_shapes=[pltpu.VMEM(s, d) for s, d in scratch],
        compiler_params=_cparams(len(grid)),
    )(*[a for a, _, _ in ins])
    return res


def _rows(a, tm):
    return (a, (tm, a.shape[1]), lambda i: (i, 0))


def _par(a):
    return (a, a.shape, lambda i: (0, 0))


def _row_out(T, C, dtype, tm):
    return ((T, C), dtype, (tm, C), lambda i: (i, 0), None)


def _acc_out(R, C):
    return ((R, C), F32, (R, C), lambda i: (0, 0), 0)


def _prev_halo(a, tm, C):
    return (a, (8, C), lambda i: (jnp.maximum(i * (tm // 8) - 1, 0), 0))


def _next_halo(a, tm, C, T):
    return (a, (8, C), lambda i: (jnp.minimum((i + 1) * (tm // 8), T // 8 - 1), 0))


def _pick(n, target):
    for t in (target, 1024, 768, 512, 384, 256, 128):
        if t <= target and n % t == 0:
            return t
    return n


def matmul(name, a, b, mode="nn", res=None, out_dtype=F32, tm=1024, tn=1024, tk=1024):
    if mode == "nn":
        (M, K), (K2, N) = a.shape, b.shape
    elif mode == "tn":
        (K, M), (K2, N) = a.shape, b.shape
    else:
        (M, K), (N, K2) = a.shape, b.shape
    assert K == K2, (name, a.shape, b.shape, mode)
    tm, tn, tk = _pick(M, tm), _pick(N, tn), _pick(K, tk)
    nk = K // tk
    dims = {"nn": NN_DIMS, "tn": TN_DIMS, "nt": NT_DIMS}[mode]
    a_spec = {"nn": pl.BlockSpec((tm, tk), lambda i, j, k: (i, k)),
              "tn": pl.BlockSpec((tk, tm), lambda i, j, k: (k, i)),
              "nt": pl.BlockSpec((tm, tk), lambda i, j, k: (i, k))}[mode]
    b_spec = {"nn": pl.BlockSpec((tk, tn), lambda i, j, k: (k, j)),
              "tn": pl.BlockSpec((tk, tn), lambda i, j, k: (k, j)),
              "nt": pl.BlockSpec((tn, tk), lambda i, j, k: (j, k))}[mode]
    has_res = res is not None

    def body(*refs):
        if has_res:
            a_ref, b_ref, r_ref, o_ref, acc_ref = refs
        else:
            a_ref, b_ref, o_ref, acc_ref = refs
        k = pl.program_id(2)

        @pl.when(k == 0)
        def _():
            acc_ref[...] = jnp.zeros_like(acc_ref)

        acc_ref[...] += lax.dot_general(a_ref[...].astype(BF16), b_ref[...].astype(BF16), dims,
                                        preferred_element_type=F32)

        @pl.when(k == nk - 1)
        def _():
            out = acc_ref[...]
            if has_res:
                out = out + r_ref[...].astype(F32)
            o_ref[...] = out.astype(o_ref.dtype)

    in_specs = [a_spec, b_spec]
    args = [a, b]
    if has_res:
        in_specs.append(pl.BlockSpec((tm, tn), lambda i, j, k: (i, j)))
        args.append(res)
    return pl.pallas_call(
        body, name=name, grid=(M // tm, N // tn, nk),
        in_specs=in_specs,
        out_specs=pl.BlockSpec((tm, tn), lambda i, j, k: (i, j)),
        out_shape=jax.ShapeDtypeStruct((M, N), out_dtype),
        scratch_shapes=[pltpu.VMEM((tm, tn), F32)],
        compiler_params=_cparams(3),
    )(*args)


def rw_pre(Pc, Ps, mu, w0, w_up, a0, a_up, g_up, k_k, k_a):
    Pm = Pc + (Ps - Pc) * mu
    r, k, v = Pm[:, 0:512], Pm[:, 512:1024], Pm[:, 1024:1536]
    xw, xa, xg = Pm[:, OFF_XW:OFF_XA], Pm[:, OFF_XA:OFF_XG], Pm[:, OFF_XG:RW_PAD]
    w = -_softplus(-(w0 + bdot(jnp.tanh(xw), w_up))) - 0.5
    decay = jnp.exp(-jnp.exp(w))
    a = _sig(a0 + bdot(xa, a_up))
    g = bdot(_sig(xg), g_up)
    kk = k * k_k
    kk = kk / jnp.maximum(jnp.sqrt(segsum(kk * kk)), 1e-12)
    k2 = k * (1.0 + (a - 1.0) * k_a)
    return r, decay, k2, v, -kk, kk * a, g


def rw_post(y, r, k2, v, g, ln_g, ln_b, r_k):
    mean = segsum(y) * (1.0 / HEAD)
    d = y - mean
    var = segsum(d * d) * (1.0 / HEAD)
    yn = d * lax.rsqrt(var + RW_LN_EPS) * ln_g + ln_b
    bonus = segsum(r * k2 * r_k) * v
    return (yn + bonus) * g


def att_combine(o1, o2, o3, l1, l2, l3):
    m = jnp.maximum(jnp.maximum(l1, l2), l3)
    e1, e2, e3 = jnp.exp(l1 - m), jnp.exp(l2 - m), jnp.exp(l3 - m)
    return (e1 * o1 + e2 * o2 + e3 * o3) / (e1 + e2 + e3)


def merge_fn(gp, bg, za, zb):
    s = _sig(gp + bg)
    half = za.shape[1]
    return s[:, :half] * za + s[:, half:] * zb


def tail_loss(x2, zg, pe, g_final, target):
    x3 = x2 + _sig(zg) * pe
    y = _rms(x3, g_final)
    err = (y - target) * (y - target)
    return 0.5 * jnp.sum(jnp.mean(err, axis=-1, keepdims=True))


SCAN_CHUNK = HEAD
SCAN_LANES = 256
SCAN_UNROLL_FWD, SCAN_UNROLL_BWD = 4, 4


def _to_head_time(z):
    T = z.shape[0]
    return z.reshape(T // HEAD, HEAD, RW_WIDTH // HEAD, HEAD).transpose(0, 3, 2, 1).reshape(T // HEAD, HEAD, RW_WIDTH)


def _from_head_time(zt):
    C = zt.shape[0]
    return zt.reshape(C, HEAD, RW_WIDTH // HEAD, HEAD).transpose(0, 3, 2, 1).reshape(C * HEAD, RW_WIDTH)


def _unrolled_loop(n, step, init, unroll):
    def body(i, carry):
        for j in range(unroll):
            carry = step(i * unroll + j, carry)
        return carry

    return lax.fori_loop(0, n // unroll, body, init)


def _lane_groups():
    return [slice(j * SCAN_LANES, (j + 1) * SCAN_LANES) for j in range(RW_WIDTH // SCAN_LANES)]


def scan_pair_terms(a, w, b, k, tm=256):
    T = a.shape[0]

    def fn(pid, a_t, nxt, w_t, b_t, k_t):
        a_next = _shift_up(a_t, jnp.where(pid[0] < T // tm - 1, nxt, 0.0), 1)
        return w_t * a_next, segsum(b_t * a_next), segsum(k_t * a_next)

    return tile_call("scan_pair_terms", fn, (T // tm,),
                     [_rows(a, tm), _next_halo(a, tm, RW_WIDTH, T), _rows(w, tm), _rows(b, tm), _rows(k, tm)],
                     [_row_out(T, RW_WIDTH, F32, tm)] * 3)


def rwkv_scan_fwd(a, w, b, k, r, vT, wa, ba, ka, exchange=()):
    T = a.shape[0]
    C, LW = SCAN_CHUNK, SCAN_LANES
    nC = T // C
    nx = len(exchange)

    def body(*refs):
        a_ref, w_ref, b_ref, k_ref, r_ref, vT_ref, wa_ref, ba_ref, ka_ref = refs[:9]
        x_refs, refs = refs[9:9 + nx], refs[9 + nx:]
        yT_ref, S_ref = refs[:2]
        land_refs, refs = refs[2:2 + nx], refs[2 + nx:]
        st_ref, vb0_ref, vb1_ref = refs[:3]
        if nx:
            start, wait = _exchange_ops([c for _, c in exchange], x_refs, land_refs, *refs[3:])

        @pl.when(pl.program_id(0) == 0)
        def _():
            st_ref[...] = jnp.zeros_like(st_ref)
            if nx:
                start()

        seg = _seg_mat(LW)
        lane = jnp.bitwise_and(lax.broadcasted_iota(jnp.int32, (1, LW), 1), HEAD - 1)
        groups = _lane_groups()

        def vcol(t, gsl):
            return _segb1(jnp.where(lane == t, vT_ref[0, :, gsl], 0.0), seg)

        for gsl in groups:
            vb0_ref[:, gsl] = vcol(0, gsl)
            vb1_ref[:, gsl] = vcol(1, gsl)

        def pair(i, yacc):
            t = 2 * i
            t1 = t + 1
            tp = jnp.maximum(t - 1, 0)
            row = lambda ref, s, gsl: ref[pl.ds(s, 1), gsl]
            Sps = [st_ref[:, gsl] for gsl in groups]
            sas = [_segb(Sp * row(a_ref, t, gsl), seg) for gsl, Sp in zip(groups, Sps)]
            us = [_segb(Sp * row(wa_ref, t, gsl), seg) for gsl, Sp in zip(groups, Sps)]
            S1s = []
            for gsl, Sp, sa, u in zip(groups, Sps, sas, us):
                vb0, vb1 = vb0_ref[:, gsl], vb1_ref[:, gsl]
                S1 = Sp * row(w_ref, t, gsl) + sa * row(b_ref, t, gsl) + vb0 * row(k_ref, t, gsl)
                sa1 = u + sa * row(ba_ref, t, gsl) + vb0 * row(ka_ref, t, gsl)
                st_ref[:, gsl] = S1 * row(w_ref, t1, gsl) + sa1 * row(b_ref, t1, gsl) + vb1 * row(k_ref, t1, gsl)
                S_ref[0, t, :, gsl] = Sp
                S_ref[0, t1, :, gsl] = S1
                S1s.append(S1)
            out = []
            for gsl, Sp, S1, ya in zip(groups, Sps, S1s, yacc):
                yb0 = _segb1(Sp * row(r_ref, tp, gsl), seg)
                yb1 = _segb1(S1 * row(r_ref, t, gsl), seg)
                out.append(jnp.where(lane == t, yb1, jnp.where(lane == t - 1, yb0, ya)))
                vb0_ref[:, gsl] = vcol(t + 2, gsl)
                vb1_ref[:, gsl] = vcol(t + 3, gsl)
            return tuple(out)

        yacc = _unrolled_loop(C // 2, pair, tuple(jnp.zeros((HEAD, LW), F32) for _ in groups), SCAN_UNROLL_FWD)
        for gsl, ya in zip(groups, yacc):
            S_last = st_ref[:, gsl]
            S_ref[0, C, :, gsl] = S_last
            yb = _segb1(S_last * r_ref[pl.ds(C - 1, 1), gsl], seg)
            yT_ref[0, :, gsl] = jnp.where(lane == C - 1, yb, ya)

        if nx:
            @pl.when(pl.program_id(0) == nC - 1)
            def _():
                wait()

    row = pl.BlockSpec((C, RW_WIDTH), lambda c: (c, 0))
    ht = pl.BlockSpec((1, HEAD, RW_WIDTH), lambda c: (c, 0, 0))
    hbm = pl.BlockSpec(memory_space=pl.ANY)
    res = pl.pallas_call(
        body, name="rwkv_scan_fwd", grid=(nC,),
        in_specs=[row, row, row, row, row, ht, row, row, row] + [hbm] * nx,
        out_specs=[ht, pl.BlockSpec((1, C + 1, HEAD, RW_WIDTH), lambda c: (c, 0, 0, 0))] + [hbm] * nx,
        out_shape=[jax.ShapeDtypeStruct((nC, HEAD, RW_WIDTH), F32),
                   jax.ShapeDtypeStruct((nC, C + 1, HEAD, RW_WIDTH), F32)] + _exchange_shapes(exchange),
        scratch_shapes=[pltpu.VMEM((HEAD, RW_WIDTH), F32)] * 3 + (_exchange_sems(nx) if nx else []),
        compiler_params=pltpu.CompilerParams(dimension_semantics=("arbitrary",), vmem_limit_bytes=VMEM_LIMIT_BYTES,
                                             has_side_effects=bool(nx)),
    )(a, w, b, k, r, vT, wa, ba, ka, *[z for z, _ in exchange])
    return res[:2], res[2:]


def rwkv_scan_bwd(a, w, b, k, r, v, dy, S_all, exchange=()):
    T = a.shape[0]
    C, LW = SCAN_CHUNK, SCAN_LANES
    nC = T // C
    nx = len(exchange)
    n_heads = RW_WIDTH // HEAD
    dyT = _to_head_time(dy).astype(BF16)
    v_rows, dy_rows = v.reshape(T, n_heads, HEAD), dy.reshape(T, n_heads, HEAD)

    def body(*refs):
        a_ref, w_ref, b_ref, k_ref, r_ref, vR_ref, dyR_ref, dyT_ref, S_ref = refs[:9]
        x_refs, refs = refs[9:9 + nx], refs[9 + nx:]
        da_ref, dw_ref, db_ref, dk_ref, dr_ref, dvT_ref = refs[:6]
        land_refs, refs = refs[6:6 + nx], refs[6 + nx:]
        ds_ref, dyb_ref = refs[:2]
        if nx:
            start, wait = _exchange_ops([c for _, c in exchange], x_refs, land_refs, *refs[2:])

        @pl.when(pl.program_id(0) == 0)
        def _():
            ds_ref[...] = jnp.zeros_like(ds_ref)
            if nx:
                start()

        seg = _seg_mat(LW)
        lane = jnp.bitwise_and(lax.broadcasted_iota(jnp.int32, (1, LW), 1), HEAD - 1)
        groups = _lane_groups()
        head_row = lax.broadcasted_iota(jnp.int32, (n_heads, LW), 0)
        lane_head = lax.shift_right_logical(lax.broadcasted_iota(jnp.int32, (n_heads, LW), 1), 6)

        def colsum(z):
            return jnp.sum(z, axis=0, keepdims=True)

        for gsl in groups:
            dyb_ref[:, gsl] = _segb1(jnp.where(lane == C - 1, dyT_ref[0, :, gsl], 0.0), seg)

        def step(i, dvacc):
            t = C - 1 - i
            dybs = [dyb_ref[:, gsl] for gsl in groups]
            dSs = [ds_ref[:, gsl] + dyb * r_ref[pl.ds(t, 1), gsl] for gsl, dyb in zip(groups, dybs)]
            dsabs = [_segb(dS * b_ref[pl.ds(t, 1), gsl], seg) for gsl, dS in zip(groups, dSs)]
            for gsl, dS, dsab in zip(groups, dSs, dsabs):
                ds_ref[:, gsl] = dS * w_ref[pl.ds(t, 1), gsl] + dsab * a_ref[pl.ds(t, 1), gsl]
            out = []
            dy_rows = dyR_ref[t].astype(BF16)
            v_rows = vR_ref[t].astype(BF16)
            for g, (gsl, dva, dS, dsab) in enumerate(zip(groups, dvacc, dSs, dsabs)):
                ar, kr = a_ref[pl.ds(t, 1), gsl], k_ref[pl.ds(t, 1), gsl]
                Sp = S_ref[0, t, :, gsl]
                own = head_row == lane_head + g * (LW // HEAD)

                def rows_in(rows, mat):
                    full = jnp.dot(rows, mat.astype(BF16), preferred_element_type=F32)
                    return jnp.sum(jnp.where(own, full, 0.0), axis=0, keepdims=True)

                dr_ref[pl.ds(t, 1), gsl] = rows_in(dy_rows, S_ref[0, t + 1, :, gsl])
                dk_ref[pl.ds(t, 1), gsl] = rows_in(v_rows, dS)
                sab = _segb1(Sp * ar, seg)
                dvb = _segb1(dS * kr, seg)
                db_ref[pl.ds(t, 1), gsl] = colsum(dS * sab)
                dw_ref[pl.ds(t, 1), gsl] = colsum(dS * Sp)
                da_ref[pl.ds(t, 1), gsl] = colsum(Sp * dsab)
                dyb_ref[:, gsl] = _segb1(jnp.where(lane == t - 1, dyT_ref[0, :, gsl], 0.0), seg)
                out.append(jnp.where(lane == t, dvb, dva))
            return tuple(out)

        dvacc = _unrolled_loop(C, step, tuple(jnp.zeros((HEAD, LW), F32) for _ in groups), SCAN_UNROLL_BWD)
        for gsl, dva in zip(groups, dvacc):
            dvT_ref[0, :, gsl] = dva

        if nx:
            @pl.when(pl.program_id(0) == nC - 1)
            def _():
                wait()

    row = pl.BlockSpec((C, RW_WIDTH), lambda c: (nC - 1 - c, 0))
    ht = pl.BlockSpec((1, HEAD, RW_WIDTH), lambda c: (nC - 1 - c, 0, 0))
    hbm = pl.BlockSpec(memory_space=pl.ANY)
    per_head = pl.BlockSpec((C, n_heads, HEAD), lambda c: (nC - 1 - c, 0, 0))
    rows_shape = jax.ShapeDtypeStruct((T, RW_WIDTH), F32)
    res = pl.pallas_call(
        body, name="rwkv_scan_bwd", grid=(nC,),
        in_specs=[row, row, row, row, row, per_head, per_head, ht,
                  pl.BlockSpec((1, C + 1, HEAD, RW_WIDTH), lambda c: (nC - 1 - c, 0, 0, 0))] + [hbm] * nx,
        out_specs=[row, row, row, row, row, ht] + [hbm] * nx,
        out_shape=[rows_shape] * 5 + [jax.ShapeDtypeStruct((nC, HEAD, RW_WIDTH), F32)] + _exchange_shapes(exchange),
        scratch_shapes=[pltpu.VMEM((HEAD, RW_WIDTH), F32), pltpu.VMEM((HEAD, RW_WIDTH), F32)]
        + (_exchange_sems(nx) if nx else []),
        compiler_params=pltpu.CompilerParams(dimension_semantics=("arbitrary",), vmem_limit_bytes=VMEM_LIMIT_BYTES,
                                             has_side_effects=bool(nx)),
    )(a, w, b, k, r, v_rows, dy_rows, dyT, S_all, *[z for z, _ in exchange])
    return res[:6], res[6:]


def _alibi_slope(h):
    return float(np.float32(2.0 ** (-8.0 * (h + 1) / ATT_HEADS)))


def _att_scores(q, kcat, lane_head, hh, gi, d, L, n, steps, valid):
    hm = lane_head == hh
    kh = jnp.where(hm, kcat, 0.0).astype(BF16)
    s = lax.dot_general(q.astype(BF16), kh, NT_DIMS, preferred_element_type=F32) * (HEAD ** -0.5)
    s = s - (_alibi_slope(gi * 4 + hh) * d) * steps
    return hm, kh, jnp.where(valid, s, NEG_BIG)


def _att_mask(L, n):
    qi = lax.broadcasted_iota(jnp.int32, (L, 2 * L), 0)
    kj = lax.broadcasted_iota(jnp.int32, (L, 2 * L), 1)
    steps = qi + L - kj
    valid = (steps >= 0) & (steps <= L) & ((kj >= L) | (n > 0))
    return steps.astype(F32), valid


def att_fwd(pa, gi, T):
    window, d = ATT_GROUPS[gi]
    L = window // d
    Tj = T // d
    nb = Tj // L
    pv = pa.reshape(Tj, d * ATT_COLS)
    nblk = ATT_COLS // ATT_OUT

    def fn(pids, q, kp, kc, vp, vc):
        n = pids[1]
        steps, valid = _att_mask(L, n)
        lane_head = lax.shift_right_logical(lax.broadcasted_iota(jnp.int32, (1, ATT_OUT), 1), 6)
        kcat = jnp.concatenate([kp, kc], axis=0)
        vcat = jnp.concatenate([vp, vc], axis=0)
        o = jnp.zeros((L, ATT_OUT), F32)
        lseb = jnp.zeros((L, ATT_OUT), F32)
        for hh in range(4):
            hm, _, s = _att_scores(q, kcat, lane_head, hh, gi, d, L, n, steps, valid)
            m = jnp.max(s, axis=-1, keepdims=True)
            p = jnp.exp(s - m)
            l = jnp.sum(p, axis=-1, keepdims=True)
            vh = jnp.where(hm, vcat, 0.0).astype(BF16)
            o = o + jnp.dot(p.astype(BF16), vh, preferred_element_type=F32) / l
            lseb = jnp.where(hm, m + jnp.log(l), lseb)
        return o, lseb

    blk = (L, ATT_OUT)
    ins = [(pv, blk, lambda r, n: (n, r * nblk + gi)),
           (pv, blk, lambda r, n: (jnp.maximum(n - 1, 0), r * nblk + 3 + gi)),
           (pv, blk, lambda r, n: (n, r * nblk + 3 + gi)),
           (pv, blk, lambda r, n: (jnp.maximum(n - 1, 0), r * nblk + 6 + gi)),
           (pv, blk, lambda r, n: (n, r * nblk + 6 + gi))]
    out = ((Tj, d * ATT_OUT), F32, blk, lambda r, n: (n, r), None)
    o, lseb = tile_call(f"att_fwd_g{gi}", fn, (d, nb), ins, [out, out])
    return o.reshape(T, ATT_OUT), lseb.reshape(T, ATT_OUT)


def att_bwd(pa, o, lseb, do, dlseb, gi, T):
    window, d = ATT_GROUPS[gi]
    L = window // d
    Tj = T // d
    nb = Tj // L
    pv = pa.reshape(Tj, d * ATT_COLS)
    nblk = ATT_COLS // ATT_OUT
    view = lambda z: z.reshape(Tj, d * ATT_OUT)

    def body(q_ref, kp_ref, kc_ref, vp_ref, vc_ref, o_ref, l_ref, do_ref, dl_ref, dq_ref, dk_ref, dv_ref):
        n = pl.program_id(1)

        @pl.when(n == 0)
        def _():
            dk_ref[...] = jnp.zeros_like(dk_ref)
            dv_ref[...] = jnp.zeros_like(dv_ref)

        steps, valid = _att_mask(L, n)
        lane_head = lax.shift_right_logical(lax.broadcasted_iota(jnp.int32, (1, ATT_OUT), 1), 6)
        q = q_ref[...]
        kcat = jnp.concatenate([kp_ref[...], kc_ref[...]], axis=0)
        vcat = jnp.concatenate([vp_ref[...], vc_ref[...]], axis=0)
        o_t, l_t, do_t, dl_t = o_ref[...], l_ref[...], do_ref[...], dl_ref[...]
        dq = jnp.zeros((L, ATT_OUT), F32)
        dkc = jnp.zeros((2 * L, ATT_OUT), F32)
        dvc = jnp.zeros((2 * L, ATT_OUT), F32)
        for hh in range(4):
            hm, kh, s = _att_scores(q, kcat, lane_head, hh, gi, d, L, n, steps, valid)
            lse = jnp.max(jnp.where(hm, l_t, NEG_BIG), axis=-1, keepdims=True)
            p = jnp.exp(s - lse)
            vh = jnp.where(hm, vcat, 0.0).astype(BF16)
            do_h = jnp.where(hm, do_t, 0.0)
            dp = lax.dot_general(do_h.astype(BF16), vh, NT_DIMS, preferred_element_type=F32)
            delta = jnp.sum(do_h * o_t, axis=-1, keepdims=True)
            dlse = jnp.sum(jnp.where(hm, dl_t, 0.0), axis=-1, keepdims=True)
            ds = (p * (dp - delta + dlse)).astype(BF16)
            dq = dq + jnp.dot(ds, kh, preferred_element_type=F32)
            qh = jnp.where(hm, q, 0.0).astype(BF16)
            dkc = dkc + lax.dot_general(ds, qh, TN_DIMS, preferred_element_type=F32)
            dvc = dvc + lax.dot_general(p.astype(BF16), do_h.astype(BF16), TN_DIMS, preferred_element_type=F32)
        scale = HEAD ** -0.5
        dq_ref[...] = dq * scale
        cur = pl.ds(pl.multiple_of(n * L, L), L)
        dk_ref[cur, :] += dkc[L:] * scale
        dv_ref[cur, :] += dvc[L:]

        @pl.when(n > 0)
        def _():
            prev = pl.ds(pl.multiple_of((n - 1) * L, L), L)
            dk_ref[prev, :] += dkc[:L] * scale
            dv_ref[prev, :] += dvc[:L]

    blk = pl.BlockSpec((L, ATT_OUT), lambda r, n: (n, r))
    res = pl.BlockSpec((Tj, ATT_OUT), lambda r, n: (0, r))
    qspec = lambda off, prev: pl.BlockSpec(
        (L, ATT_OUT), (lambda r, n: (jnp.maximum(n - 1, 0), r * nblk + off + gi)) if prev
        else (lambda r, n: (n, r * nblk + off + gi)))
    shape = jax.ShapeDtypeStruct((Tj, d * ATT_OUT), F32)
    dq, dk, dv = pl.pallas_call(
        body, name=f"att_bwd_g{gi}", grid=(d, nb),
        in_specs=[qspec(0, False), qspec(3, True), qspec(3, False), qspec(6, True), qspec(6, False),
                  blk, blk, blk, blk],
        out_specs=[blk, res, res],
        out_shape=[shape, shape, shape],
        compiler_params=_cparams(2),
    )(pv, pv, pv, pv, pv, view(o), view(lseb), view(do), view(dlseb))
    return dq.reshape(T, ATT_OUT), dk.reshape(T, ATT_OUT), dv.reshape(T, ATT_OUT)


FFN_TM, FFN_TC = 512, 512


def _conv3(u, prev8, cw, cb):
    return cb + cw[0:1] * u + cw[1:2] * _shift_down(u, prev8, 1) + cw[2:3] * _shift_down(u, prev8, 2)


def conv_glu_fwd(u, conv_w, conv_b):
    T = u.shape[0]
    tm, tc = FFN_TM, FFN_TC
    nj, ni = D_FF // tc, T // tm

    def fn(pids, ug, ugh, uv, uvh, cwg, cbg, cwv, cbv):
        first = pids[1] > 0
        cg = _conv3(ug, jnp.where(first, ugh, 0.0), cwg, cbg)
        cv = _conv3(uv, jnp.where(first, uvh, 0.0), cwv, cbv)
        return _gelu_tanh(cg) * cv

    halo = lambda off: (lambda j, i: (jnp.maximum(i * (tm // 8) - 1, 0), j + off))
    ins = [(u, (tm, tc), lambda j, i: (i, j)), (u, (8, tc), halo(0)),
           (u, (tm, tc), lambda j, i: (i, j + nj)), (u, (8, tc), halo(nj)),
           (conv_w, (3, tc), lambda j, i: (0, j)), (conv_b, (1, tc), lambda j, i: (0, j)),
           (conv_w, (3, tc), lambda j, i: (0, j + nj)), (conv_b, (1, tc), lambda j, i: (0, j + nj))]
    out = ((T, D_FF), BF16, (tm, tc), lambda j, i: (i, j), None)
    return tile_call("conv_glu_fwd", fn, (nj, ni), ins, [out])[0]


def conv_glu_bwd(u, conv_w, conv_b, df):
    T = u.shape[0]
    tm, tc = FFN_TM, FFN_TC
    nj, ni = D_FF // tc, T // tm

    def fn(pids, ug, ugh, uv, uvh, cwg, cbg, cwv, cbv, df_t, nxt_g, nxt_v):
        i = ni - 1 - pids[1]
        ugh = jnp.where(i > 0, ugh, 0.0)
        uvh = jnp.where(i > 0, uvh, 0.0)
        cg = _conv3(ug, ugh, cwg, cbg)
        cv = _conv3(uv, uvh, cwv, cbv)
        _, vjp = jax.vjp(lambda g_, v_: _gelu_tanh(g_) * v_, cg, cv)
        dcg, dcv = vjp(df_t.astype(F32))
        cs = lambda z: jnp.sum(z, axis=0, keepdims=True)

        @pl.when(pids[1] == 0)
        def _():
            nxt_g[...] = jnp.zeros_like(nxt_g)
            nxt_v[...] = jnp.zeros_like(nxt_v)

        outs = []
        for dc, cw, nxt_ref in ((dcg, cwg, nxt_g), (dcv, cwv, nxt_v)):
            nxt = nxt_ref[...]
            outs.append(cw[0:1] * dc + cw[1:2] * _shift_up(dc, nxt, 1) + cw[2:3] * _shift_up(dc, nxt, 2))
            nxt_ref[...] = dc[:8]
        for dc, uu, hh in ((dcg, ug, ugh), (dcv, uv, uvh)):
            outs += [cs(dc * uu), cs(dc * _shift_down(uu, hh, 1)), cs(dc * _shift_down(uu, hh, 2)), cs(dc)]
        return outs

    rows = lambda off: (lambda j, r: (ni - 1 - r, j + off))
    halo = lambda off: (lambda j, r: (jnp.maximum((ni - 1 - r) * (tm // 8) - 1, 0), j + off))
    ins = [(u, (tm, tc), rows(0)), (u, (8, tc), halo(0)),
           (u, (tm, tc), rows(nj)), (u, (8, tc), halo(nj)),
           (conv_w, (3, tc), lambda j, r: (0, j)), (conv_b, (1, tc), lambda j, r: (0, j)),
           (conv_w, (3, tc), lambda j, r: (0, j + nj)), (conv_b, (1, tc), lambda j, r: (0, j + nj)),
           (df, (tm, tc), rows(0))]
    big = ((T, D_FF), BF16, (tm, tc), rows(0), None)
    acc = ((1, D_FF), F32, (1, tc), lambda j, r: (0, j), 1)
    res = tile_call("conv_glu_bwd", fn, (nj, ni), ins, [big, big] + [acc] * 8,
                    scratch=[((8, tc), F32), ((8, tc), F32)])
    dconv_w = jnp.concatenate([jnp.concatenate([res[2 + j], res[6 + j]], axis=1) for j in range(3)], axis=0)
    dconv_b = jnp.concatenate([res[5], res[9]], axis=1)
    return res[0], res[1], dconv_w, dconv_b


def _pad_cols(w, total):
    return jnp.pad(w, ((0, 0), (0, total - w.shape[1])))


def _pad_rows(w, total):
    return jnp.pad(w, ((0, total - w.shape[0]), (0, 0)))


def _proj_pad(w):
    z = lambda n: jnp.zeros((w.shape[0], n), w.dtype)
    return jnp.concatenate([w[:, :1600], z(64), w[:, 1600:1664], z(64), w[:, 1664:1824], z(96), w[:, 1824:],
                            z(PROJ_TAIL)], axis=1)


def _proj_unpad(g):
    return jnp.concatenate([g[:, :1600], g[:, OFF_XA:OFF_XA + 64], g[:, OFF_XG:OFF_XG + 160],
                            g[:, RW_PAD:RW_PAD + ATT_COLS]], axis=1)


def _rw_unpad(g):
    return jnp.concatenate([g[:, :1600], g[:, OFF_XA:OFF_XA + 64], g[:, OFF_XG:OFF_XG + 160]], axis=1)


def rms_fwd(name, x, g, tm=256):
    T, D = x.shape
    return tile_call(name, lambda pid, x_t, g_t: _rms(x_t, g_t), (T // tm,),
                     [_rows(x, tm), _par(g)], [_row_out(T, D, BF16, tm)])[0]


def rms_bwd(name, x, g, dh, dres, with_bf16=True, tm=256):
    T, D = x.shape
    out_dtypes = (F32, BF16) if with_bf16 else (F32,)

    def fn(pid, x_t, g_t, dh_t, dres_t):
        _, vjp = jax.vjp(_rms, x_t, g_t)
        dx, dg = vjp(dh_t.astype(F32))
        return (dres_t + dx,) * len(out_dtypes) + (dg,)

    return tile_call(name, fn, (T // tm,), [_rows(x, tm), _par(g), _rows(dh, tm), _rows(dres, tm)],
                     [_row_out(T, D, dt, tm) for dt in out_dtypes] + [_acc_out(1, D)])


def local_step(x, p, target, W):
    T, D = x.shape
    G = {}

    w_in_p = W["w_in_p"]
    mu_p = _proj_pad(_pad_cols(W["rw_mu"], 4128))[:, :RW_PAD]
    w_up_p = _pad_rows(W["rw_w_up"], 128)
    a_up_p = _pad_rows(W["rw_a_up"], 128)
    g_up_p = _pad_rows(W["rw_g_up"], 256)
    r_k = W["rw_r_k"].reshape(1, RW_WIDTH)
    rw_params = [mu_p, W["rw_w0"], w_up_p, W["rw_a0"], a_up_p, g_up_p, W["rw_k_k"], W["rw_k_a"]]

    h = rms_fwd("rms_mix", x, W["g_mix"])
    proj = matmul("proj_in_rw", h, w_in_p[:, :RW_PAD])
    pa = matmul("proj_in_att", h, w_in_p[:, RW_PAD:RW_PAD + ATT_COLS])
    gp = matmul("proj_gate", h, W["w_gate"])

    tm = 256
    rw_in = (proj, (tm, RW_PAD), lambda i: (i, 0))
    rw_halo = _prev_halo(proj, tm, RW_PAD)

    def rw_pre_tile(pid, Pc, halo, *params):
        prev8 = jnp.where(pid[0] > 0, halo, 0.0)
        params = [q.astype(F32) for q in params]
        return rw_pre(Pc, _shift_down(Pc, prev8, 1), *params)

    r, decay, k2, v, avec, bvec, g = tile_call(
        "rw_pre", rw_pre_tile, (T // tm,), [rw_in, rw_halo] + [_par(q) for q in rw_params],
        [_row_out(T, RW_WIDTH, F32, tm)] * 7)

    vT = _to_head_time(v).astype(BF16)
    wa, ba, ka = scan_pair_terms(avec, decay, bvec, k2)
    (yT, S_all), late_slots = rwkv_scan_fwd(avec, decay, bvec, k2, r, vT, wa, ba, ka,
                                            exchange=_late_weight_sources(W))
    y = _from_head_time(yT)
    W = dict(W, **_late_weights(late_slots))

    post_params = [W["rw_ln_g"], W["rw_ln_b"], r_k]
    ya = tile_call("rw_post", lambda pid, *t: rw_post(*t), (T // tm,),
                   [_rows(z, tm) for z in (y, r, k2, v, g)] + [_par(q) for q in post_params],
                   [_row_out(T, RW_WIDTH, BF16, tm)])[0]

    att = [att_fwd(pa, gi, T) for gi in range(3)]
    o_l = [att[0][0], att[1][0], att[2][0], att[0][1], att[1][1], att[2][1]]
    yb = tile_call("att_combine", lambda pid, *t: att_combine(*t), (T // tm,),
                   [_rows(z, tm) for z in o_l], [_row_out(T, ATT_OUT, BF16, tm)])[0]

    za = matmul("branch_a", ya, W["w_branch_a"])
    zb = matmul("branch_b", yb, W["w_branch_b"])
    merged = tile_call("merge", lambda pid, *t: merge_fn(*t), (T // tm,),
                       [_rows(gp, tm), _par(W["b_gate"]), _rows(za, tm), _rows(zb, tm)],
                       [_row_out(T, D, BF16, tm)])[0]
    x1 = matmul("mix_out", merged, W["w_out"], res=x)

    h2 = rms_fwd("rms_ffn", x1, W["g_ffn"])
    u = matmul("ffn_up", h2, W["w_up"])
    f = conv_glu_fwd(u, W["conv_w"], W["conv_b"])
    x2 = matmul("ffn_down", f, W["w_down"], res=x1)

    h3 = rms_fwd("rms_ple", x2, W["g_ple"])
    zg = matmul("ple_gate", h3, W["w_ple_gate"])
    pe = matmul("ple_embed", p, W["w_ple"])

    def tail_tile(pid, x2_t, zg_t, pe_t, gf, tgt):
        loss, vjp = jax.vjp(lambda a_, b_, c_, d_: tail_loss(a_, b_, c_, d_, tgt), x2_t, zg_t, pe_t, gf)
        dx2, dzg, dpe, dgf = vjp(jnp.ones((), F32))
        return dx2, dzg, dpe, dgf, jnp.full((1, 128), loss, F32)

    tmt = 128
    dx3, dzg, dpe, dgf, loss_acc = tile_call(
        "tail_loss", tail_tile, (T // tmt,),
        [_rows(x2, tmt), _rows(zg, tmt), _rows(pe, tmt), _par(W["g_final"]), _rows(target, tmt)],
        [_row_out(T, D, F32, tmt), _row_out(T, D, BF16, tmt), _row_out(T, D, BF16, tmt),
         _acc_out(1, D), _acc_out(1, 128)])
    loss = loss_acc[0, 0]
    G["g_final"] = dgf

    wgrad = functools.partial(matmul, mode="tn", out_dtype=GRAD_WIRE)
    G["w_ple"] = wgrad("d_w_ple", p, dpe)
    G["w_ple_gate"] = wgrad("d_w_ple_gate", h3, dzg)
    dh3 = matmul("d_h3", dzg, W["w_ple_gate"], "nt")
    dx2, dx2b, G["g_ple"] = rms_bwd("rms_ple_bwd", x2, W["g_ple"], dh3, dx3)

    G["w_down"] = wgrad("d_w_down", f, dx2b)
    df = matmul("d_f", dx2b, W["w_down"], "nt", out_dtype=BF16)
    du_g, du_v, G["conv_w"], G["conv_b"] = conv_glu_bwd(u, W["conv_w"], W["conv_b"], df)
    du = jnp.concatenate([du_g, du_v], axis=1)
    G["w_up"] = wgrad("d_w_up", h2, du)
    dh2 = matmul("d_h2", du, W["w_up"], "nt")
    dx1, dx1b, G["g_ffn"] = rms_bwd("rms_ffn_bwd", x1, W["g_ffn"], dh2, dx2)

    G["w_out"] = wgrad("d_w_out", merged, dx1b)
    dmerged = matmul("d_merged", dx1b, W["w_out"], "nt", out_dtype=BF16)

    def merge_bwd_tile(pid, gp_t, bg, za_t, zb_t, dm_t):
        _, vjp = jax.vjp(merge_fn, gp_t, bg, za_t, zb_t)
        return vjp(dm_t.astype(F32))

    dgp, G["b_gate"], dza, dzb = tile_call(
        "merge_bwd", merge_bwd_tile, (T // tm,),
        [_rows(gp, tm), _par(W["b_gate"]), _rows(za, tm), _rows(zb, tm), _rows(dmerged, tm)],
        [_row_out(T, 2 * D, BF16, tm), _acc_out(1, 2 * D), _row_out(T, D, BF16, tm), _row_out(T, D, BF16, tm)])
    G["w_branch_a"] = wgrad("d_w_branch_a", ya, dza)
    dya = matmul("d_ya", dza, W["w_branch_a"], "nt")
    G["w_branch_b"] = wgrad("d_w_branch_b", yb, dzb)
    dyb = matmul("d_yb", dzb, W["w_branch_b"], "nt")
    G["w_gate"] = wgrad("d_w_gate", h, dgp)
    dh_gate = matmul("d_h_gate", dgp, W["w_gate"], "nt")

    def comb_bwd_tile(pid, *t):
        _, vjp = jax.vjp(att_combine, *t[:6])
        return vjp(t[6])

    d_ol = tile_call("att_combine_bwd", comb_bwd_tile, (T // tm,),
                     [_rows(z, tm) for z in o_l] + [_rows(dyb, tm)],
                     [_row_out(T, ATT_OUT, F32, tm)] * 6)
    dqkv = [att_bwd(pa, att[gi][0], att[gi][1], d_ol[gi], d_ol[3 + gi], gi, T) for gi in range(3)]
    d_att = [dqkv[gi][j] for j in range(3) for gi in range(3)]

    def post_bwd_tile(pid, *t):
        _, vjp = jax.vjp(rw_post, *t[:8])
        return vjp(t[8])

    dy, dr_p, dk2_p, dv_p, dg, G["rw_ln_g"], G["rw_ln_b"], d_rk = tile_call(
        "rw_post_bwd", post_bwd_tile, (T // tm,),
        [_rows(z, tm) for z in (y, r, k2, v, g)] + [_par(q) for q in post_params] + [_rows(dya, tm)],
        [_row_out(T, RW_WIDTH, F32, tm)] * 5 + [_acc_out(1, RW_WIDTH)] * 3)
    G["rw_r_k"] = d_rk.reshape(W["rw_r_k"].shape)

    (da, dw, db, dk_s, dr_s, dvT), G["_early_parts"] = rwkv_scan_bwd(
        avec, decay, bvec, k2, r, v, dy, S_all, exchange=_early_grad_sources(G))
    dv_s = _from_head_time(dvT)

    tmb = 128
    rw_in_b = (proj, (tmb, RW_PAD), lambda i: (i, 0))

    def pre_bwd_tile(pid, Pc, halo, *t):
        prev8 = jnp.where(pid[0] > 0, halo, 0.0)
        params = [q.astype(F32) for q in t[:8]]
        dr1, dr2, dw_, dk1, dk2_, dv1, dv2, da_, db_, dg_ = t[8:]
        _, vjp = jax.vjp(rw_pre, Pc, _shift_down(Pc, prev8, 1), *params)
        return vjp((dr1 + dr2, dw_, dk1 + dk2_, dv1 + dv2, da_, db_, dg_))

    cts = (dr_s, dr_p, dw, dk_s, dk2_p, dv_s, dv_p, da, db, dg)
    res = tile_call(
        "rw_pre_bwd", pre_bwd_tile, (T // tmb,),
        [rw_in_b, _prev_halo(proj, tmb, RW_PAD)] + [_par(q) for q in rw_params] + [_rows(z, tmb) for z in cts],
        [_row_out(T, RW_PAD, F32, tmb)] * 2 + [_acc_out(*q.shape) for q in rw_params])
    dPc, dPs = res[0], res[1]
    d_mu, G["rw_w0"], d_wup, G["rw_a0"], d_aup, d_gup, G["rw_k_k"], G["rw_k_a"] = res[2:]
    G["rw_mu"] = _rw_unpad(d_mu)
    G["rw_w_up"], G["rw_a_up"], G["rw_g_up"] = d_wup[:64], d_aup[:64], d_gup[:160]

    def dproj_tile(pid, dPc_t, dPs_t, nxt, *att_t):
        nxt = jnp.where(pid[0] < T // tm - 1, nxt, 0.0)
        tail = jnp.zeros((dPc_t.shape[0], PROJ_TAIL), F32)
        return jnp.concatenate([dPc_t + _shift_up(dPs_t, nxt, 1)] + list(att_t) + [tail], axis=1)

    dproj = tile_call("d_proj", dproj_tile, (T // tm,),
                      [_rows(dPc, tm), _rows(dPs, tm), _next_halo(dPs, tm, RW_PAD, T)] + [_rows(z, tm) for z in d_att],
                      [_row_out(T, PROJ_PAD, BF16, tm)])[0]
    G["w_in_p"] = wgrad("d_w_in", h, dproj)
    dh = matmul("d_h", dproj, w_in_p, "nt", res=dh_gate)
    dx, G["g_mix"] = rms_bwd("rms_mix_bwd", x, W["g_mix"], dh, dx1, with_bf16=False)
    return loss, dx, G


def _mesh_pos():
    return lax.axis_index("x"), lax.axis_index("y"), lax.axis_index("c")


def _peer(pos, k):
    x, y, c = pos
    px = 1 - x if k & 4 else x
    py = 1 - y if k & 2 else y
    pc = 1 - c if k & 1 else c
    return (px, py, pc), 4 * px + 2 * py + pc


def all_gather_blocks(name, blocks):
    n = len(blocks)

    def body(*refs):
        x_refs, out_refs = refs[:n], refs[n:2 * n]
        send_sems, recv_sems, local_sems = refs[2 * n:]
        x, y, c = _mesh_pos()
        me, sibling = (x, y, c), (x, y, 1 - c)
        chips = [(1 - x, y), (x, 1 - y), (1 - x, 1 - y)]
        ops = range(n)

        def slot(i, px, py, pc):
            return out_refs[i].at[4 * px + 2 * py + pc]

        def copy(k, i, block, to, own=False):
            return pltpu.make_async_remote_copy(
                src_ref=x_refs[i] if own else slot(i, *block), dst_ref=slot(i, *block),
                send_sem=send_sems.at[k, i], recv_sem=recv_sems.at[k, i],
                device_id=to, device_id_type=pl.DeviceIdType.MESH)

        mine = [pltpu.make_async_copy(x_refs[i], slot(i, *me), local_sems.at[i]) for i in ops]
        first = [copy(0, i, me, sibling, own=True) for i in ops]
        first += [copy(1 + j, i, me, (*chip, c), own=True) for j, chip in enumerate(chips) for i in ops]
        for cp in mine + first:
            cp.start()
        passed = []
        for j, chip in enumerate(chips):
            for i in ops:
                copy(1 + j, i, (*chip, c), me).wait_recv()
                passed.append(copy(4 + j, i, (*chip, c), sibling))
                passed[-1].start()
        for i in ops:
            copy(0, i, sibling, me).wait_recv()
        for j, chip in enumerate(chips):
            for i in ops:
                copy(4 + j, i, (*chip, 1 - c), me).wait_recv()
        for cp in first + passed:
            cp.wait_send()
        for cp in mine:
            cp.wait()

    return pl.pallas_call(
        body, name=name,
        in_specs=[pl.BlockSpec(memory_space=pl.ANY)] * n,
        out_specs=[pl.BlockSpec(memory_space=pl.ANY)] * n,
        out_shape=[jax.ShapeDtypeStruct((N_DEV,) + b.shape, b.dtype) for b in blocks],
        scratch_shapes=[pltpu.SemaphoreType.DMA((N_DEV - 1, n)), pltpu.SemaphoreType.DMA((N_DEV - 1, n)),
                        pltpu.SemaphoreType.DMA((n,))],
        compiler_params=pltpu.CompilerParams(has_side_effects=True),
    )(*blocks)


WHOLE = 0


def _exchange_shapes(srcs):
    shapes = [a.shape[1:] if cols is None else a.shape if cols == WHOLE else (a.shape[0], cols) for a, cols in srcs]
    return [jax.ShapeDtypeStruct((N_DEV,) + s, a.dtype) for s, (a, _) in zip(shapes, srcs)]


def _exchange_sems(n):
    return [pltpu.SemaphoreType.DMA((N_DEV - 1, n)), pltpu.SemaphoreType.DMA((N_DEV - 1, n)),
            pltpu.SemaphoreType.DMA((n,))]


def _exchange_ops(col_widths, x_refs, out_refs, send_sems, recv_sems, local_sems):
    n = len(col_widths)
    pos = _mesh_pos()
    me = 4 * pos[0] + 2 * pos[1] + pos[2]

    def piece(i, d):
        cols = col_widths[i]
        if cols is None:
            return x_refs[i].at[d]
        if cols == WHOLE:
            return x_refs[i]
        return x_refs[i].at[:, pl.ds(pl.multiple_of(d * cols, 128), cols)]

    def local(i):
        return pltpu.make_async_copy(piece(i, me), out_refs[i].at[me], local_sems.at[i])

    def remote(k, i, landing):
        peer, idx = _peer(pos, k)
        return pltpu.make_async_remote_copy(
            src_ref=piece(i, idx), dst_ref=out_refs[i].at[idx if landing else me],
            send_sem=send_sems.at[k - 1, i], recv_sem=recv_sems.at[k - 1, i],
            device_id=peer, device_id_type=pl.DeviceIdType.MESH)

    pairs = [(k, i) for k in range(1, N_DEV) for i in range(n)]

    def start():
        for i in range(n):
            local(i).start()
        for k, i in pairs:
            remote(k, i, False).start()

    def wait():
        for k, i in pairs:
            remote(k, i, True).wait_recv()
        for k, i in pairs:
            remote(k, i, False).wait_send()
        for i in range(n):
            local(i).wait()

    return start, wait


def all_to_all_blocks(name, srcs):
    n = len(srcs)

    def body(*refs):
        start, wait = _exchange_ops([c for _, c in srcs], refs[:n], refs[n:2 * n], *refs[2 * n:])
        start()
        wait()

    return pl.pallas_call(
        body, name=name,
        in_specs=[pl.BlockSpec(memory_space=pl.ANY)] * n,
        out_specs=[pl.BlockSpec(memory_space=pl.ANY)] * n,
        out_shape=_exchange_shapes(srcs),
        scratch_shapes=_exchange_sems(n),
        compiler_params=pltpu.CompilerParams(has_side_effects=True),
    )(*[a for a, _ in srcs])


def _adam_row_tile(R, C):
    best = None
    for t in range(16, R + 1, 16):
        if R % t == 0 and t * C <= ADAM_TILE_ELEMS:
            best = t
    return best if best is not None else R


def reduce_adamw(name, parts, w, m, v):
    _, R, C = parts.shape
    tr = _adam_row_tile(R, C)

    def fn(pid, parts_t, w_t, m_t, v_t):
        g = parts_t[0].astype(F32)
        for i in range(1, N_DEV):
            g = g + parts_t[i].astype(F32)
        m_n = ADAM_B1 * m_t + (1.0 - ADAM_B1) * g
        v_n = ADAM_B2 * v_t + (1.0 - ADAM_B2) * (g * g)
        m_hat = m_n / (1.0 - ADAM_B1 ** ADAM_STEP)
        v_hat = v_n / (1.0 - ADAM_B2 ** ADAM_STEP)
        delta = -ADAM_LR * (m_hat / (jnp.sqrt(v_hat) + ADAM_EPS) + ADAM_WD * w_t)
        return g, delta, m_n, v_n

    row = lambda a: (a, (tr, C), lambda i: (i, 0))
    out = ((R, C), F32, (tr, C), lambda i: (i, 0), None)
    return tile_call(name, fn, (R // tr,),
                     [(parts, (N_DEV, tr, C), lambda i: (0, i, 0)), row(w), row(m), row(v)], [out] * 4)


PARAMS = (
    ("g_mix", (1, 1024), None), ("w_in", (1024, 4128), 1), ("rw_mu", (1, 1824), None), ("rw_w0", (1, 512), None),
    ("rw_w_up", (64, 512), 1), ("rw_a0", (1, 512), None), ("rw_a_up", (64, 512), 1), ("rw_g_up", (160, 512), 1),
    ("rw_k_k", (1, 512), None), ("rw_k_a", (1, 512), None), ("rw_r_k", (8, 64), None), ("rw_ln_g", (1, 512), None),
    ("rw_ln_b", (1, 512), None), ("w_branch_a", (512, 1024), 1), ("w_branch_b", (256, 1024), 1),
    ("w_gate", (1024, 2048), 1), ("b_gate", (1, 2048), None), ("w_out", (1024, 1024), 0), ("g_ffn", (1, 1024), None),
    ("w_up", (1024, 6144), 1), ("conv_w", (3, 6144), 1), ("conv_b", (1, 6144), None), ("w_down", (3072, 1024), 0),
    ("g_ple", (1, 1024), None), ("w_ple_gate", (1024, 1024), 0), ("w_ple", (256, 1024), 1), ("g_final", (1, 1024), None),
)
SHARDED = tuple(q for q in PARAMS if q[2] is not None)
REPLICATED = tuple(q for q in PARAMS if q[2] is None)
BIG_NAMES = ("w_in", "w_up", "w_gate", "w_out", "w_down", "w_ple_gate", "w_branch_a", "w_branch_b", "w_ple")
BIG = tuple(q for q in SHARDED if q[0] in BIG_NAMES)
SMALL_SHARDED = tuple(q for q in SHARDED if q[0] not in BIG_NAMES)
PACK_COLS = 1024
F32_GATHERED = ("conv_w",)


def _local_shape(shape, axis):
    s = list(shape)
    s[axis] //= N_DEV
    return tuple(s)


def _numel(shape):
    return int(np.prod(shape))


def _pad_flat(z, mult):
    n = z.shape[-1]
    total = -(-n // mult) * mult
    return jnp.pad(z, [(0, 0)] * (z.ndim - 1) + [(0, total - n)])


def _full_from_slots(slots, shape, axis):
    loc = _local_shape(shape, axis)
    z = slots.reshape((N_DEV,) + loc)
    if axis == 0:
        return z.reshape(shape)
    return z.transpose(1, 0, 2).reshape(shape)


def _slots_from_full(full, shape, axis):
    loc = _local_shape(shape, axis)
    if axis == 0:
        return full.reshape(N_DEV, _numel(loc))
    return full.reshape(shape[0], N_DEV, loc[1]).transpose(1, 0, 2).reshape(N_DEV, _numel(loc))


W_IN_SLOT = 640
W_IN_LOCAL = 4128 // N_DEV


def _block_shape(shape, axis):
    return _local_shape(shape, axis) if axis is not None else shape


def _pad_w_in(block):
    return jnp.pad(block, ((0, 0), (0, W_IN_SLOT - W_IN_LOCAL)))


def _proj_col(s):
    return s + jnp.where(s >= 1600, 64, 0) + jnp.where(s >= 1664, 64, 0) + jnp.where(s >= 1824, 96, 0)


def _perm_tile(d, c0, width):
    j = lax.broadcasted_iota(jnp.int32, (W_IN_SLOT, width), 0)
    c = c0 + lax.broadcasted_iota(jnp.int32, (W_IN_SLOT, width), 1)
    hit = (_proj_col(d * W_IN_LOCAL + j) == c) & (j < W_IN_LOCAL)
    return jnp.where(hit, 1.0, 0.0).astype(BF16)


PERM_TILE = 768


def w_in_unshuffle(slots):
    _, K, _ = slots.shape
    tn = PERM_TILE
    reach = 3

    def first_slot(j):
        return j + jnp.where(j >= 3, 1, 0) + jnp.where(j >= 5, 1, 0)

    def body(a_ref, o_ref, acc_ref):
        j, kk = pl.program_id(0), pl.program_id(1)
        d = first_slot(j) + kk

        @pl.when(kk == 0)
        def _():
            acc_ref[...] = jnp.zeros_like(acc_ref)

        @pl.when(d < N_DEV)
        def _():
            acc_ref[...] += jnp.dot(a_ref[0], _perm_tile(d, j * tn, tn), preferred_element_type=F32)

        @pl.when(kk == reach - 1)
        def _():
            o_ref[...] = acc_ref[...].astype(o_ref.dtype)

    return pl.pallas_call(
        body, name="w_in_unshuffle", grid=(PROJ_PAD // tn, reach),
        in_specs=[pl.BlockSpec((1, K, W_IN_SLOT), lambda j, kk: (jnp.minimum(first_slot(j) + kk, N_DEV - 1), 0, 0))],
        out_specs=pl.BlockSpec((K, tn), lambda j, kk: (0, j)),
        out_shape=jax.ShapeDtypeStruct((K, PROJ_PAD), BF16),
        scratch_shapes=[pltpu.VMEM((K, tn), F32)],
        compiler_params=_cparams(2),
    )(slots)


def w_in_shuffle_grad(dw):
    K = dw.shape[0]
    tk = PERM_TILE

    def first_tile(d):
        return _proj_col(d * W_IN_LOCAL) // tk

    def body(g_ref, o_ref, acc_ref):
        d, kk = pl.program_id(0), pl.program_id(1)
        perm = _perm_tile(d, (first_tile(d) + kk) * tk, tk)
        part = lax.dot_general(g_ref[...].astype(BF16), perm, NT_DIMS, preferred_element_type=F32)

        @pl.when(kk == 0)
        def _():
            acc_ref[...] = part

        @pl.when(kk == 1)
        def _():
            o_ref[0] = (acc_ref[...] + part).astype(o_ref.dtype)

    return pl.pallas_call(
        body, name="w_in_shuffle_grad", grid=(N_DEV, 2),
        in_specs=[pl.BlockSpec((K, tk), lambda d, kk: (0, first_tile(d) + kk))],
        out_specs=pl.BlockSpec((1, K, W_IN_SLOT), lambda d, kk: (d, 0, 0)),
        out_shape=jax.ShapeDtypeStruct((N_DEV, K, W_IN_SLOT), GRAD_WIRE),
        scratch_shapes=[pltpu.VMEM((K, W_IN_SLOT), F32)],
        compiler_params=_cparams(2),
    )(dw)


def _flat_rows(pieces, dtype, row_mult):
    flat = jnp.concatenate([z.astype(dtype) for z in pieces], axis=-1)
    flat = _pad_flat(flat, row_mult * PACK_COLS)
    return flat.reshape(flat.shape[:-1] + (-1, PACK_COLS))


FIRST = tuple(q for q in BIG if q[0] in ("w_in", "w_gate"))
LATE = tuple(q for q in BIG if q not in FIRST)


def _matrix_from_slots(slots, shape, axis):
    return slots.reshape(shape) if axis == 0 else slots.transpose(1, 0, 2).reshape(shape)


def _late_weight_sources(W):
    return [(blk, WHOLE) for blk in W["_late_blocks"]]


def _late_weights(slots):
    return {n: _matrix_from_slots(s, shape, axis) for (n, shape, axis), s in zip(LATE, slots)}


def gather_weights(local):
    blocks = [(_pad_w_in(local[n]) if n == "w_in" else local[n]).astype(BF16) for n, _, _ in FIRST]
    small = [q for q in SMALL_SHARDED if q[0] not in F32_GATHERED]
    exact = [q for q in SMALL_SHARDED if q[0] in F32_GATHERED]
    blocks.append(_flat_rows([local[n].reshape(-1) for n, _, _ in small], BF16, 16))
    blocks.append(_flat_rows([local[n].reshape(-1) for n, _, _ in exact], F32, 8))
    got = all_gather_blocks("weight_all_gather", blocks)
    full = {"_late_blocks": [local[n].astype(BF16) for n, _, _ in LATE]}
    for (n, shape, axis), slots in zip(FIRST, got):
        if n == "w_in":
            full["w_in_p"] = w_in_unshuffle(slots)
        else:
            full[n] = _matrix_from_slots(slots, shape, axis)
    for group, slots in ((small, got[-2]), (exact, got[-1])):
        slots, off = slots.reshape(N_DEV, -1), 0
        for n, shape, axis in group:
            size = _numel(_local_shape(shape, axis))
            full[n] = _full_from_slots(slots[:, off:off + size], shape, axis)
            off += size
    for n, _, _ in REPLICATED:
        full[n] = local[n]
    return full


LOSS_SLOT = ("_loss", (1, 2), None)
PACKED_SMALL = SMALL_SHARDED + REPLICATED + (LOSS_SLOT,)


def _pack_small(vals):
    pieces = [vals[n].reshape(-1) if n in vals else jnp.zeros((_numel(shape),), F32) for n, shape, _ in PACKED_SMALL]
    return _flat_rows(pieces, F32, 16)


def _unpack_small(packed):
    flat, out, off = packed.reshape(-1), {}, 0
    for n, shape, axis in PACKED_SMALL:
        loc = _block_shape(shape, axis)
        out[n] = flat[off:off + _numel(loc)].reshape(loc)
        off += _numel(loc)
    return out


EARLY = tuple(q for q in BIG if q[0] != "w_in")


def _early_grad_sources(G):
    srcs = []
    for n, shape, axis in EARLY:
        if axis == 0:
            srcs.append((G[n].astype(GRAD_WIRE).reshape((N_DEV,) + _local_shape(shape, axis)), None))
        else:
            srcs.append((G[n].astype(GRAD_WIRE), shape[1] // N_DEV))
    return srcs


def _late_grad_sources(G, loss_local):
    srcs = [(w_in_shuffle_grad(G["w_in_p"]), None)]
    rows = [_slots_from_full(G[n].reshape(shape), shape, axis) for n, shape, axis in SMALL_SHARDED]
    loss_hi = loss_local.astype(GRAD_WIRE).astype(F32)
    rep = jnp.concatenate([G[n].reshape(-1) for n, _, _ in REPLICATED] + [jnp.stack([loss_hi, loss_local - loss_hi])])
    rows.append(jnp.broadcast_to(rep[None, :], (N_DEV, rep.shape[0])))
    srcs.append((_flat_rows(rows, GRAD_WIRE, 16), None))
    return srcs


def _step(x, p, target, local_w, local_m, local_v):
    full = gather_weights(local_w)
    loss_local, dx, G = local_step(x, p, target, full)
    late = all_to_all_blocks("grad_all_to_all", _late_grad_sources(G, loss_local))
    parts = [late[0]] + list(G["_early_parts"]) + [late[1]]
    outs = [{}, {}, {}, {}]
    for (n, shape, axis), part in zip((BIG[0],) + EARLY, parts):
        prep = _pad_w_in if n == "w_in" else (lambda z: z)
        res = reduce_adamw("adamw_" + n, part, prep(local_w[n]), prep(local_m[n]), prep(local_v[n]))
        for o, z in zip(outs, res):
            o[n] = z[:, :W_IN_LOCAL] if n == "w_in" else z
    res = reduce_adamw("adamw_small", parts[-1], _pack_small(local_w), _pack_small(local_m), _pack_small(local_v))
    for o, z in zip(outs, res):
        o.update(_unpack_small(z))
    loss = jnp.sum(outs[0]["_loss"])
    return loss, dx, outs


def kernel(x, p, g_mix, w_in, rw_mu, rw_w0, rw_w_up, rw_a0, rw_a_up, rw_g_up, rw_k_k, rw_k_a, rw_r_k, rw_ln_g, rw_ln_b, w_branch_a, w_branch_b, w_gate, b_gate, w_out, g_ffn, w_up, conv_w, conv_b, w_down, g_ple, w_ple_gate, w_ple, g_final, loss_target, m_g_mix, m_w_in, m_rw_mu, m_rw_w0, m_rw_w_up, m_rw_a0, m_rw_a_up, m_rw_g_up, m_rw_k_k, m_rw_k_a, m_rw_r_k, m_rw_ln_g, m_rw_ln_b, m_w_branch_a, m_w_branch_b, m_w_gate, m_b_gate, m_w_out, m_g_ffn, m_w_up, m_conv_w, m_conv_b, m_w_down, m_g_ple, m_w_ple_gate, m_w_ple, m_g_final, v_g_mix, v_w_in, v_rw_mu, v_rw_w0, v_rw_w_up, v_rw_a0, v_rw_a_up, v_rw_g_up, v_rw_k_k, v_rw_k_a, v_rw_r_k, v_rw_ln_g, v_rw_ln_b, v_w_branch_a, v_w_branch_b, v_w_gate, v_b_gate, v_w_out, v_g_ffn, v_w_up, v_conv_w, v_conv_b, v_w_down, v_g_ple, v_w_ple_gate, v_w_ple, v_g_final):
    args = dict(locals())
    names = [n for n, _, _ in PARAMS]
    orig_shape = {n: args[n].shape for n in names}

    def strip(prefix):
        out = {}
        for n, shape, axis in PARAMS:
            a = args[prefix + n]
            loc = _local_shape(shape, axis) if axis is not None else shape
            out[n] = a.reshape(loc)
        return out

    local_w, local_m, local_v = strip(""), strip("m_"), strip("v_")
    T, D = x.shape[-2], x.shape[-1]
    loss, dx, (g, delta, m_n, v_n) = _step(x.reshape(T, D), p.reshape(T, p.shape[-1]), loss_target.reshape(T, D),
                                           local_w, local_m, local_v)
    outs = [loss, dx.reshape(x.shape)]
    for group in (g, delta, m_n, v_n):
        outs += [group[n].reshape(orig_shape[n]) for n in names]
    return tuple(outs)
```

```python
import functools
import math

import numpy as np
import jax
import jax.numpy as jnp
from jax import lax
from jax.experimental import pallas as pl
from jax.experimental.pallas import tpu as pltpu

F32 = jnp.float32
BF16 = jnp.bfloat16
GRAD_WIRE = jnp.bfloat16

N_DEV = 8
NORM_EPS = 1e-6
RW_LN_EPS = 64e-5
HEAD = 64
RW_WIDTH = 512
ATT_GROUPS = ((128, 1), (512, 4), (2048, 16))
ATT_HEADS = 12
ATT_OUT = 256
ATT_COLS = 2304
OFF_XW, OFF_XA, OFF_XG, RW_PAD, PROJ_PAD = 1536, 1664, 1792, 2048, 4608
PROJ_TAIL = PROJ_PAD - RW_PAD - ATT_COLS
D_FF = 3072

ADAM_LR, ADAM_B1, ADAM_B2, ADAM_EPS, ADAM_WD, ADAM_STEP = 0.001, 0.9, 0.999, 1e-08, 0.01, 10

VMEM_LIMIT_BYTES = 56 * 1024 * 1024
ADAM_TILE_ELEMS = 256 * 1024
NEG_BIG = -1e30

NT_DIMS = (((1,), (1,)), ((), ()))
TN_DIMS = (((0,), (0,)), ((), ()))
NN_DIMS = (((1,), (0,)), ((), ()))


def _cparams(n_axes):
    return pltpu.CompilerParams(dimension_semantics=("arbitrary",) * n_axes,
                                vmem_limit_bytes=VMEM_LIMIT_BYTES)


def _split2(x):
    hi = x.astype(BF16)
    lo = (x - hi.astype(F32)).astype(BF16)
    return hi, lo


def _seg_mat(n):
    r = lax.shift_right_logical(lax.broadcasted_iota(jnp.int32, (n, n), 0), 6)
    c = lax.shift_right_logical(lax.broadcasted_iota(jnp.int32, (n, n), 1), 6)
    return jnp.where(r == c, 1.0, 0.0).astype(BF16)


def _segb(x, seg):
    hi, lo = _split2(x)
    return (jnp.dot(hi, seg, preferred_element_type=F32)
            + jnp.dot(lo, seg, preferred_element_type=F32))


def _segb1(x, seg):
    return jnp.dot(x.astype(BF16), seg, preferred_element_type=F32)


@jax.custom_vjp
def segsum(x):
    return _segb(x, _seg_mat(x.shape[1]))


def _segsum_fwd(x):
    return segsum(x), None


def _segsum_bwd(_, ct):
    return (segsum(ct),)


segsum.defvjp(_segsum_fwd, _segsum_bwd)


@jax.custom_vjp
def bdot(a, b):
    return jnp.dot(a.astype(BF16), b.astype(BF16), preferred_element_type=F32)


def _bdot_fwd(a, b):
    return bdot(a, b), (a, b)


def _bdot_bwd(res, ct):
    a, b = res
    ctb = ct.astype(BF16)
    da = lax.dot_general(ctb, b.astype(BF16), NT_DIMS, preferred_element_type=F32)
    db = lax.dot_general(a.astype(BF16), ctb, TN_DIMS, preferred_element_type=F32)
    return da.astype(a.dtype), db.astype(b.dtype)


bdot.defvjp(_bdot_fwd, _bdot_bwd)


def _sig(x):
    return 1.0 / (1.0 + jnp.exp(-x))


def _softplus(z):
    return jnp.maximum(z, 0.0) + jnp.log(1.0 + jnp.exp(-jnp.abs(z)))


def _gelu_tanh(x):
    return 0.5 * x * (1.0 + jnp.tanh(0.7978845608028654 * (x + 0.044715 * (x * x * x))))


def _rms(x, g):
    return x * lax.rsqrt(jnp.mean(x * x, axis=-1, keepdims=True) + NORM_EPS) * g


def _shift_down(x, prev8, n):
    rolled = pltpu.roll(x, n, 0)
    top = pltpu.roll(prev8, n, 0)
    rid = lax.broadcasted_iota(jnp.int32, (8, x.shape[1]), 0)
    head = jnp.where(rid < n, top, rolled[:8])
    return jnp.concatenate([head, rolled[8:]], axis=0)


def _shift_up(x, next8, n):
    rows = x.shape[0]
    rolled = pltpu.roll(x, rows - n, 0)
    bottom = pltpu.roll(next8, 8 - n, 0)
    rid = lax.broadcasted_iota(jnp.int32, (8, x.shape[1]), 0)
    tail = jnp.where(rid >= 8 - n, bottom, rolled[rows - 8:])
    return jnp.concatenate([rolled[:rows - 8], tail], axis=0)


def tile_call(name, fn, grid, ins, outs, scratch=()):
    n_in, n_out = len(ins), len(outs)
    acc_axes = [o[4] for o in outs]

    def body(*refs):
        pids = tuple(pl.program_id(a) for a in range(len(grid)))
        vals = fn(pids, *[r[...] for r in refs[:n_in]], *refs[n_in + n_out:])
        if not isinstance(vals, (tuple, list)):
            vals = (vals,)
        for o_ref, val, ax in zip(refs[n_in:n_in + n_out], vals, acc_axes):
            if ax is None:
                o_ref[...] = val.astype(o_ref.dtype)
            else:
                @pl.when(pids[ax] == 0)
                def _(o_ref=o_ref):
                    o_ref[...] = jnp.zeros_like(o_ref)

                o_ref[...] += val.astype(o_ref.dtype)

    res = pl.pallas_call(
        body, name=name, grid=grid,
        in_specs=[pl.BlockSpec(b, im) for _, b, im in ins],
        out_specs=[pl.BlockSpec(o[2], o[3]) for o in outs],
        out_shape=[jax.ShapeDtypeStruct(o[0], o[1]) for o in outs],
        scratch_shapes=[pltpu.VMEM(s, d) for s, d in scratch],
        compiler_params=_cparams(len(grid)),
    )(*[a for a, _, _ in ins])
    return res


def _rows(a, tm):
    return (a, (tm, a.shape[1]), lambda i: (i, 0))


def _par(a):
    return (a, a.shape, lambda i: (0, 0))


def _row_out(T, C, dtype, tm):
    return ((T, C), dtype, (tm, C), lambda i: (i, 0), None)


def _acc_out(R, C):
    return ((R, C), F32, (R, C), lambda i: (0, 0), 0)


def _prev_halo(a, tm, C):
    return (a, (8, C), lambda i: (jnp.maximum(i * (tm // 8) - 1, 0), 0))


def _next_halo(a, tm, C, T):
    return (a, (8, C), lambda i: (jnp.minimum((i + 1) * (tm // 8), T // 8 - 1), 0))


def _pick(n, target):
    for t in (target, 1024, 768, 512, 384, 256, 128):
        if t <= target and n % t == 0:
            return t
    return n


def matmul(name, a, b, mode="nn", res=None, out_dtype=F32, tm=1024, tn=1024, tk=1024):
    if mode == "nn":
        (M, K), (K2, N) = a.shape, b.shape
    elif mode == "tn":
        (K, M), (K2, N) = a.shape, b.shape
    else:
        (M, K), (N, K2) = a.shape, b.shape
    assert K == K2, (name, a.shape, b.shape, mode)
    tm, tn, tk = _pick(M, tm), _pick(N, tn), _pick(K, tk)
    nk = K // tk
    dims = {"nn": NN_DIMS, "tn": TN_DIMS, "nt": NT_DIMS}[mode]
    a_spec = {"nn": pl.BlockSpec((tm, tk), lambda i, j, k: (i, k)),
              "tn": pl.BlockSpec((tk, tm), lambda i, j, k: (k, i)),
              "nt": pl.BlockSpec((tm, tk), lambda i, j, k: (i, k))}[mode]
    b_spec = {"nn": pl.BlockSpec((tk, tn), lambda i, j, k: (k, j)),
              "tn": pl.BlockSpec((tk, tn), lambda i, j, k: (k, j)),
              "nt": pl.BlockSpec((tn, tk), lambda i, j, k: (j, k))}[mode]
    has_res = res is not None

    def body(*refs):
        if has_res:
            a_ref, b_ref, r_ref, o_ref, acc_ref = refs
        else:
            a_ref, b_ref, o_ref, acc_ref = refs
        k = pl.program_id(2)

        @pl.when(k == 0)
        def _():
            acc_ref[...] = jnp.zeros_like(acc_ref)

        acc_ref[...] += lax.dot_general(a_ref[...].astype(BF16), b_ref[...].astype(BF16), dims,
                                        preferred_element_type=F32)

        @pl.when(k == nk - 1)
        def _():
            out = acc_ref[...]
            if has_res:
                out = out + r_ref[...].astype(F32)
            o_ref[...] = out.astype(o_ref.dtype)

    in_specs = [a_spec, b_spec]
    args = [a, b]
    if has_res:
        in_specs.append(pl.BlockSpec((tm, tn), lambda i, j, k: (i, j)))
        args.append(res)
    return pl.pallas_call(
        body, name=name, grid=(M // tm, N // tn, nk),
        in_specs=in_specs,
        out_specs=pl.BlockSpec((tm, tn), lambda i, j, k: (i, j)),
        out_shape=jax.ShapeDtypeStruct((M, N), out_dtype),
        scratch_shapes=[pltpu.VMEM((tm, tn), F32)],
        compiler_params=_cparams(3),
    )(*args)


def rw_pre(Pc, Ps, mu, w0, w_up, a0, a_up, g_up, k_k, k_a):
    Pm = Pc + (Ps - Pc) * mu
    r, k, v = Pm[:, 0:512], Pm[:, 512:1024], Pm[:, 1024:1536]
    xw, xa, xg = Pm[:, OFF_XW:OFF_XA], Pm[:, OFF_XA:OFF_XG], Pm[:, OFF_XG:RW_PAD]
    w = -_softplus(-(w0 + bdot(jnp.tanh(xw), w_up))) - 0.5
    decay = jnp.exp(-jnp.exp(w))
    a = _sig(a0 + bdot(xa, a_up))
    g = bdot(_sig(xg), g_up)
    kk = k * k_k
    kk = kk / jnp.maximum(jnp.sqrt(segsum(kk * kk)), 1e-12)
    k2 = k * (1.0 + (a - 1.0) * k_a)
    return r, decay, k2, v, -kk, kk * a, g


def rw_post(y, r, k2, v, g, ln_g, ln_b, r_k):
    mean = segsum(y) * (1.0 / HEAD)
    d = y - mean
    var = segsum(d * d) * (1.0 / HEAD)
    yn = d * lax.rsqrt(var + RW_LN_EPS) * ln_g + ln_b
    bonus = segsum(r * k2 * r_k) * v
    return (yn + bonus) * g


def att_combine(o1, o2, o3, l1, l2, l3):
    m = jnp.maximum(jnp.maximum(l1, l2), l3)
    e1, e2, e3 = jnp.exp(l1 - m), jnp.exp(l2 - m), jnp.exp(l3 - m)
    return (e1 * o1 + e2 * o2 + e3 * o3) / (e1 + e2 + e3)


def merge_fn(gp, bg, za, zb):
    s = _sig(gp + bg)
    half = za.shape[1]
    return s[:, :half] * za + s[:, half:] * zb


def tail_loss(x2, zg, pe, g_final, target):
    x3 = x2 + _sig(zg) * pe
    y = _rms(x3, g_final)
    err = (y - target) * (y - target)
    return 0.5 * jnp.sum(jnp.mean(err, axis=-1, keepdims=True))


SCAN_CHUNK = HEAD
SCAN_LANES = 256
SCAN_UNROLL_FWD, SCAN_UNROLL_BWD = 4, 4


def _to_head_time(z):
    T = z.shape[0]
    return z.reshape(T // HEAD, HEAD, RW_WIDTH // HEAD, HEAD).transpose(0, 3, 2, 1).reshape(T // HEAD, HEAD, RW_WIDTH)


def _from_head_time(zt):
    C = zt.shape[0]
    return zt.reshape(C, HEAD, RW_WIDTH // HEAD, HEAD).transpose(0, 3, 2, 1).reshape(C * HEAD, RW_WIDTH)


def _unrolled_loop(n, step, init, unroll):
    def body(i, carry):
        for j in range(unroll):
            carry = step(i * unroll + j, carry)
        return carry

    return lax.fori_loop(0, n // unroll, body, init)


def _lane_groups():
    return [slice(j * SCAN_LANES, (j + 1) * SCAN_LANES) for j in range(RW_WIDTH // SCAN_LANES)]


def scan_pair_terms(a, w, b, k, tm=256):
    T = a.shape[0]

    def fn(pid, a_t, nxt, w_t, b_t, k_t):
        a_next = _shift_up(a_t, jnp.where(pid[0] < T // tm - 1, nxt, 0.0), 1)
        return w_t * a_next, segsum(b_t * a_next), segsum(k_t * a_next)

    return tile_call("scan_pair_terms", fn, (T // tm,),
                     [_rows(a, tm), _next_halo(a, tm, RW_WIDTH, T), _rows(w, tm), _rows(b, tm), _rows(k, tm)],
                     [_row_out(T, RW_WIDTH, F32, tm)] * 3)


def rwkv_scan_fwd(a, w, b, k, r, vT, wa, ba, ka, exchange=()):
    T = a.shape[0]
    C, LW = SCAN_CHUNK, SCAN_LANES
    nC = T // C
    nx = len(exchange)

    def body(*refs):
        a_ref, w_ref, b_ref, k_ref, r_ref, vT_ref, wa_ref, ba_ref, ka_ref = refs[:9]
        x_refs, refs = refs[9:9 + nx], refs[9 + nx:]
        yT_ref, S_ref, saT_ref = refs[:3]
        land_refs, refs = refs[3:3 + nx], refs[3 + nx:]
        st_ref, vb0_ref, vb1_ref = refs[:3]
        if nx:
            start, wait = _exchange_ops([c for _, c in exchange], x_refs, land_refs, *refs[3:])

        @pl.when(pl.program_id(0) == 0)
        def _():
            st_ref[...] = jnp.zeros_like(st_ref)
            if nx:
                start()

        seg = _seg_mat(LW)
        lane = jnp.bitwise_and(lax.broadcasted_iota(jnp.int32, (1, LW), 1), HEAD - 1)
        groups = _lane_groups()

        def vcol(t, gsl):
            return _segb1(jnp.where(lane == t, vT_ref[0, :, gsl], 0.0), seg)

        for gsl in groups:
            vb0_ref[:, gsl] = vcol(0, gsl)
            vb1_ref[:, gsl] = vcol(1, gsl)
        saT_ref[...] = jnp.zeros_like(saT_ref)

        def pair(i, yacc):
            t = 2 * i
            t1 = t + 1
            tp = jnp.maximum(t - 1, 0)
            row = lambda ref, s, gsl: ref[pl.ds(s, 1), gsl]
            Sps = [st_ref[:, gsl] for gsl in groups]
            sas = [_segb(Sp * row(a_ref, t, gsl), seg) for gsl, Sp in zip(groups, Sps)]
            us = [_segb(Sp * row(wa_ref, t, gsl), seg) for gsl, Sp in zip(groups, Sps)]
            S1s = []
            for gsl, Sp, sa, u in zip(groups, Sps, sas, us):
                vb0, vb1 = vb0_ref[:, gsl], vb1_ref[:, gsl]
                S1 = Sp * row(w_ref, t, gsl) + sa * row(b_ref, t, gsl) + vb0 * row(k_ref, t, gsl)
                sa1 = u + sa * row(ba_ref, t, gsl) + vb0 * row(ka_ref, t, gsl)
                st_ref[:, gsl] = S1 * row(w_ref, t1, gsl) + sa1 * row(b_ref, t1, gsl) + vb1 * row(k_ref, t1, gsl)
                S_ref[0, t, :, gsl] = Sp
                S_ref[0, t1, :, gsl] = S1
                S1s.append(S1)
                saT_ref[0, :, gsl] = jnp.where(lane == t, sa, jnp.where(lane == t1, sa1, saT_ref[0, :, gsl]))
            out = []
            for gsl, Sp, S1, ya in zip(groups, Sps, S1s, yacc):
                yb0 = _segb1(Sp * row(r_ref, tp, gsl), seg)
                yb1 = _segb1(S1 * row(r_ref, t, gsl), seg)
                out.append(jnp.where(lane == t, yb1, jnp.where(lane == t - 1, yb0, ya)))
                vb0_ref[:, gsl] = vcol(t + 2, gsl)
                vb1_ref[:, gsl] = vcol(t + 3, gsl)
            return tuple(out)

        yacc = _unrolled_loop(C // 2, pair, tuple(jnp.zeros((HEAD, LW), F32) for _ in groups), SCAN_UNROLL_FWD)
        for gsl, ya in zip(groups, yacc):
            S_last = st_ref[:, gsl]
            S_ref[0, C, :, gsl] = S_last
            yb = _segb1(S_last * r_ref[pl.ds(C - 1, 1), gsl], seg)
            yT_ref[0, :, gsl] = jnp.where(lane == C - 1, yb, ya)

        if nx:
            @pl.when(pl.program_id(0) == nC - 1)
            def _():
                wait()

    row = pl.BlockSpec((C, RW_WIDTH), lambda c: (c, 0))
    ht = pl.BlockSpec((1, HEAD, RW_WIDTH), lambda c: (c, 0, 0))
    hbm = pl.BlockSpec(memory_space=pl.ANY)
    res = pl.pallas_call(
        body, name="rwkv_scan_fwd", grid=(nC,),
        in_specs=[row, row, row, row, row, ht, row, row, row] + [hbm] * nx,
        out_specs=[ht, pl.BlockSpec((1, C + 1, HEAD, RW_WIDTH), lambda c: (c, 0, 0, 0)), ht] + [hbm] * nx,
        out_shape=[jax.ShapeDtypeStruct((nC, HEAD, RW_WIDTH), F32),
                   jax.ShapeDtypeStruct((nC, C + 1, HEAD, RW_WIDTH), F32),
                   jax.ShapeDtypeStruct((nC, HEAD, RW_WIDTH), F32)] + _exchange_shapes(exchange),
        scratch_shapes=[pltpu.VMEM((HEAD, RW_WIDTH), F32)] * 3 + (_exchange_sems(nx) if nx else []),
        compiler_params=pltpu.CompilerParams(dimension_semantics=("arbitrary",), vmem_limit_bytes=VMEM_LIMIT_BYTES,
                                             has_side_effects=bool(nx)),
    )(a, w, b, k, r, vT, wa, ba, ka, *[z for z, _ in exchange])
    return res[:3], res[3:]


def rwkv_scan_bwd(a, w, b, k, r, v, dy, S_all, saT, exchange=()):
    T = a.shape[0]
    C, LW = SCAN_CHUNK, SCAN_LANES
    nC = T // C
    nx = len(exchange)
    n_heads = RW_WIDTH // HEAD
    dyT = _to_head_time(dy).astype(BF16)
    v_rows, dy_rows = v.reshape(T, n_heads, HEAD), dy.reshape(T, n_heads, HEAD)
    sa_rows = _from_head_time(saT).reshape(T, n_heads, HEAD)

    def body(*refs):
        a_ref, w_ref, b_ref, k_ref, r_ref, vR_ref, saR_ref, dyR_ref, dyT_ref, S_ref = refs[:10]
        x_refs, refs = refs[10:10 + nx], refs[10 + nx:]
        da_ref, dw_ref, db_ref, dk_ref, dr_ref, dvT_ref = refs[:6]
        land_refs, refs = refs[6:6 + nx], refs[6 + nx:]
        ds_ref, dyb_ref = refs[:2]
        if nx:
            start, wait = _exchange_ops([c for _, c in exchange], x_refs, land_refs, *refs[2:])

        @pl.when(pl.program_id(0) == 0)
        def _():
            ds_ref[...] = jnp.zeros_like(ds_ref)
            if nx:
                start()

        seg = _seg_mat(LW)
        lane = jnp.bitwise_and(lax.broadcasted_iota(jnp.int32, (1, LW), 1), HEAD - 1)
        groups = _lane_groups()
        head_row = lax.broadcasted_iota(jnp.int32, (n_heads, LW), 0)
        lane_head = lax.shift_right_logical(lax.broadcasted_iota(jnp.int32, (n_heads, LW), 1), 6)

        def colsum(z):
            return jnp.sum(z, axis=0, keepdims=True)

        for gsl in groups:
            dyb_ref[:, gsl] = _segb1(jnp.where(lane == C - 1, dyT_ref[0, :, gsl], 0.0), seg)

        def step(i, dvacc):
            t = C - 1 - i
            dybs = [dyb_ref[:, gsl] for gsl in groups]
            dSs = [ds_ref[:, gsl] + dyb * r_ref[pl.ds(t, 1), gsl] for gsl, dyb in zip(groups, dybs)]
            dsabs = [_segb(dS * b_ref[pl.ds(t, 1), gsl], seg) for gsl, dS in zip(groups, dSs)]
            for gsl, dS, dsab in zip(groups, dSs, dsabs):
                ds_ref[:, gsl] = dS * w_ref[pl.ds(t, 1), gsl] + dsab * a_ref[pl.ds(t, 1), gsl]
            out = []
            dy_rows = dyR_ref[t].astype(BF16)
            v_sa_rows = jnp.concatenate([vR_ref[t], saR_ref[t]], axis=0).astype(BF16)
            for g, (gsl, dva, dS, dsab) in enumerate(zip(groups, dvacc, dSs, dsabs)):
                kr = k_ref[pl.ds(t, 1), gsl]
                Sp = S_ref[0, t, :, gsl]
                own = head_row == lane_head + g * (LW // HEAD)

                def rows_in(rows, mat):
                    full = jnp.dot(rows, mat.astype(BF16), preferred_element_type=F32)
                    return [jnp.sum(jnp.where(own, full[s:s + n_heads], 0.0), axis=0, keepdims=True)
                            for s in range(0, rows.shape[0], n_heads)]

                (dr,) = rows_in(dy_rows, S_ref[0, t + 1, :, gsl])
                dk, db = rows_in(v_sa_rows, dS)
                dr_ref[pl.ds(t, 1), gsl] = dr
                dk_ref[pl.ds(t, 1), gsl] = dk
                db_ref[pl.ds(t, 1), gsl] = db
                dvb = _segb1(dS * kr, seg)
                dw_ref[pl.ds(t, 1), gsl] = colsum(dS * Sp)
                da_ref[pl.ds(t, 1), gsl] = colsum(Sp * dsab)
                dyb_ref[:, gsl] = _segb1(jnp.where(lane == t - 1, dyT_ref[0, :, gsl], 0.0), seg)
                out.append(jnp.where(lane == t, dvb, dva))
            return tuple(out)

        dvacc = _unrolled_loop(C, step, tuple(jnp.zeros((HEAD, LW), F32) for _ in groups), SCAN_UNROLL_BWD)
        for gsl, dva in zip(groups, dvacc):
            dvT_ref[0, :, gsl] = dva

        if nx:
            @pl.when(pl.program_id(0) == nC - 1)
            def _():
                wait()

    row = pl.BlockSpec((C, RW_WIDTH), lambda c: (nC - 1 - c, 0))
    ht = pl.BlockSpec((1, HEAD, RW_WIDTH), lambda c: (nC - 1 - c, 0, 0))
    hbm = pl.BlockSpec(memory_space=pl.ANY)
    per_head = pl.BlockSpec((C, n_heads, HEAD), lambda c: (nC - 1 - c, 0, 0))
    rows_shape = jax.ShapeDtypeStruct((T, RW_WIDTH), F32)
    res = pl.pallas_call(
        body, name="rwkv_scan_bwd", grid=(nC,),
        in_specs=[row, row, row, row, row, per_head, per_head, per_head, ht,
                  pl.BlockSpec((1, C + 1, HEAD, RW_WIDTH), lambda c: (nC - 1 - c, 0, 0, 0))] + [hbm] * nx,
        out_specs=[row, row, row, row, row, ht] + [hbm] * nx,
        out_shape=[rows_shape] * 5 + [jax.ShapeDtypeStruct((nC, HEAD, RW_WIDTH), F32)] + _exchange_shapes(exchange),
        scratch_shapes=[pltpu.VMEM((HEAD, RW_WIDTH), F32), pltpu.VMEM((HEAD, RW_WIDTH), F32)]
        + (_exchange_sems(nx) if nx else []),
        compiler_params=pltpu.CompilerParams(dimension_semantics=("arbitrary",), vmem_limit_bytes=VMEM_LIMIT_BYTES,
                                             has_side_effects=bool(nx)),
    )(a, w, b, k, r, v_rows, sa_rows, dy_rows, dyT, S_all, *[z for z, _ in exchange])
    return res[:6], res[6:]


def _alibi_slope(h):
    return float(np.float32(2.0 ** (-8.0 * (h + 1) / ATT_HEADS)))


ATT_GROUP_HEADS = 4


def _stack_heads(x, lane_head, fill=0.0):
    return jnp.concatenate([jnp.where(lane_head == hh, x, fill) for hh in range(ATT_GROUP_HEADS)], axis=0)


def _unstack_heads(x, lane_head, L):
    out = jnp.zeros((L, x.shape[1]), F32)
    for hh in range(ATT_GROUP_HEADS):
        out = jnp.where(lane_head == hh, x[hh * L:(hh + 1) * L], out)
    return out


def _att_logits(qs, kcat, gi, d, L, n):
    qi = lax.broadcasted_iota(jnp.int32, (L, 2 * L), 0)
    kj = lax.broadcasted_iota(jnp.int32, (L, 2 * L), 1)
    steps = qi + L - kj
    valid = (steps >= 0) & (steps <= L) & ((kj >= L) | (n > 0))
    dist = (d * steps).astype(F32)
    bias = jnp.concatenate([jnp.where(valid, -_alibi_slope(gi * ATT_GROUP_HEADS + hh) * dist, NEG_BIG)
                            for hh in range(ATT_GROUP_HEADS)], axis=0)
    s = lax.dot_general(qs.astype(BF16), kcat, NT_DIMS, preferred_element_type=F32) * (HEAD ** -0.5)
    return jnp.where(bias > 0.5 * NEG_BIG, s + bias, NEG_BIG)


def att_fwd(pa, gi, T):
    window, d = ATT_GROUPS[gi]
    L = window // d
    Tj = T // d
    nb = Tj // L
    pv = pa.reshape(Tj, d * ATT_COLS)
    nblk = ATT_COLS // ATT_OUT

    def fn(pids, q, kp, kc, vp, vc):
        lane_head = lax.shift_right_logical(lax.broadcasted_iota(jnp.int32, (1, ATT_OUT), 1), 6)
        kcat = jnp.concatenate([kp, kc], axis=0).astype(BF16)
        vcat = jnp.concatenate([vp, vc], axis=0).astype(BF16)
        s = _att_logits(_stack_heads(q, lane_head), kcat, gi, d, L, pids[1])
        m = jnp.max(s, axis=-1, keepdims=True)
        p = jnp.exp(s - m)
        l = jnp.sum(p, axis=-1, keepdims=True)
        o = jnp.dot(p.astype(BF16), vcat, preferred_element_type=F32) / l
        lse = jnp.broadcast_to(m + jnp.log(l), o.shape)
        return _unstack_heads(o, lane_head, L), _unstack_heads(lse, lane_head, L)

    blk = (L, ATT_OUT)
    ins = [(pv, blk, lambda r, n: (n, r * nblk + gi)),
           (pv, blk, lambda r, n: (jnp.maximum(n - 1, 0), r * nblk + 3 + gi)),
           (pv, blk, lambda r, n: (n, r * nblk + 3 + gi)),
           (pv, blk, lambda r, n: (jnp.maximum(n - 1, 0), r * nblk + 6 + gi)),
           (pv, blk, lambda r, n: (n, r * nblk + 6 + gi))]
    out = ((Tj, d * ATT_OUT), F32, blk, lambda r, n: (n, r), None)
    o, lseb = tile_call(f"att_fwd_g{gi}", fn, (d, nb), ins, [out, out])
    return o.reshape(T, ATT_OUT), lseb.reshape(T, ATT_OUT)


def att_bwd(pa, o, lseb, do, dlseb, gi, T):
    window, d = ATT_GROUPS[gi]
    L = window // d
    Tj = T // d
    nb = Tj // L
    pv = pa.reshape(Tj, d * ATT_COLS)
    nblk = ATT_COLS // ATT_OUT
    view = lambda z: z.reshape(Tj, d * ATT_OUT)

    def body(q_ref, kp_ref, kc_ref, vp_ref, vc_ref, o_ref, l_ref, do_ref, dl_ref, dq_ref, dk_ref, dv_ref):
        n = pl.program_id(1)

        @pl.when(n == 0)
        def _():
            dk_ref[...] = jnp.zeros_like(dk_ref)
            dv_ref[...] = jnp.zeros_like(dv_ref)

        lane_head = lax.shift_right_logical(lax.broadcasted_iota(jnp.int32, (1, ATT_OUT), 1), 6)
        kcat = jnp.concatenate([kp_ref[...], kc_ref[...]], axis=0).astype(BF16)
        vcat = jnp.concatenate([vp_ref[...], vc_ref[...]], axis=0).astype(BF16)
        qs = _stack_heads(q_ref[...], lane_head)
        dos = _stack_heads(do_ref[...], lane_head)
        lse = jnp.max(_stack_heads(l_ref[...], lane_head, NEG_BIG), axis=-1, keepdims=True)
        dlse = jnp.sum(_stack_heads(dl_ref[...], lane_head), axis=-1, keepdims=True)
        delta = jnp.sum(dos * jnp.concatenate([o_ref[...]] * ATT_GROUP_HEADS, axis=0), axis=-1, keepdims=True)
        p = jnp.exp(_att_logits(qs, kcat, gi, d, L, n) - lse)
        dp = lax.dot_general(dos.astype(BF16), vcat, NT_DIMS, preferred_element_type=F32)
        ds = (p * (dp - delta + dlse)).astype(BF16)
        dq = _unstack_heads(jnp.dot(ds, kcat, preferred_element_type=F32), lane_head, L)
        dkc = lax.dot_general(ds, qs.astype(BF16), TN_DIMS, preferred_element_type=F32)
        dvc = lax.dot_general(p.astype(BF16), dos.astype(BF16), TN_DIMS, preferred_element_type=F32)
        scale = HEAD ** -0.5
        dq_ref[...] = dq * scale
        cur = pl.ds(pl.multiple_of(n * L, L), L)
        dk_ref[cur, :] += dkc[L:] * scale
        dv_ref[cur, :] += dvc[L:]

        @pl.when(n > 0)
        def _():
            prev = pl.ds(pl.multiple_of((n - 1) * L, L), L)
            dk_ref[prev, :] += dkc[:L] * scale
            dv_ref[prev, :] += dvc[:L]

    blk = pl.BlockSpec((L, ATT_OUT), lambda r, n: (n, r))
    res = pl.BlockSpec((Tj, ATT_OUT), lambda r, n: (0, r))
    qspec = lambda off, prev: pl.BlockSpec(
        (L, ATT_OUT), (lambda r, n: (jnp.maximum(n - 1, 0), r * nblk + off + gi)) if prev
        else (lambda r, n: (n, r * nblk + off + gi)))
    shape = jax.ShapeDtypeStruct((Tj, d * ATT_OUT), F32)
    dq, dk, dv = pl.pallas_call(
        body, name=f"att_bwd_g{gi}", grid=(d, nb),
        in_specs=[qspec(0, False), qspec(3, True), qspec(3, False), qspec(6, True), qspec(6, False),
                  blk, blk, blk, blk],
        out_specs=[blk, res, res],
        out_shape=[shape, shape, shape],
        compiler_params=_cparams(2),
    )(pv, pv, pv, pv, pv, view(o), view(lseb), view(do), view(dlseb))
    return dq.reshape(T, ATT_OUT), dk.reshape(T, ATT_OUT), dv.reshape(T, ATT_OUT)


FFN_TM, FFN_TC = 512, 512


def _conv3(u, prev8, cw, cb):
    return cb + cw[0:1] * u + cw[1:2] * _shift_down(u, prev8, 1) + cw[2:3] * _shift_down(u, prev8, 2)


def conv_glu_fwd(u, conv_w, conv_b):
    T = u.shape[0]
    tm, tc = FFN_TM, FFN_TC
    nj, ni = D_FF // tc, T // tm

    def fn(pids, ug, ugh, uv, uvh, cwg, cbg, cwv, cbv):
        first = pids[1] > 0
        cg = _conv3(ug, jnp.where(first, ugh, 0.0), cwg, cbg)
        cv = _conv3(uv, jnp.where(first, uvh, 0.0), cwv, cbv)
        return _gelu_tanh(cg) * cv

    halo = lambda off: (lambda j, i: (jnp.maximum(i * (tm // 8) - 1, 0), j + off))
    ins = [(u, (tm, tc), lambda j, i: (i, j)), (u, (8, tc), halo(0)),
           (u, (tm, tc), lambda j, i: (i, j + nj)), (u, (8, tc), halo(nj)),
           (conv_w, (3, tc), lambda j, i: (0, j)), (conv_b, (1, tc), lambda j, i: (0, j)),
           (conv_w, (3, tc), lambda j, i: (0, j + nj)), (conv_b, (1, tc), lambda j, i: (0, j + nj))]
    out = ((T, D_FF), BF16, (tm, tc), lambda j, i: (i, j), None)
    return tile_call("conv_glu_fwd", fn, (nj, ni), ins, [out])[0]


def conv_glu_bwd(u, conv_w, conv_b, df):
    T = u.shape[0]
    tm, tc = FFN_TM, FFN_TC
    nj, ni = D_FF // tc, T // tm

    def fn(pids, ug, ugh, uv, uvh, cwg, cbg, cwv, cbv, df_t, nxt_g, nxt_v):
        i = ni - 1 - pids[1]
        ugh = jnp.where(i > 0, ugh, 0.0)
        uvh = jnp.where(i > 0, uvh, 0.0)
        cg = _conv3(ug, ugh, cwg, cbg)
        cv = _conv3(uv, uvh, cwv, cbv)
        _, vjp = jax.vjp(lambda g_, v_: _gelu_tanh(g_) * v_, cg, cv)
        dcg, dcv = vjp(df_t.astype(F32))
        cs = lambda z: jnp.sum(z, axis=0, keepdims=True)

        @pl.when(pids[1] == 0)
        def _():
            nxt_g[...] = jnp.zeros_like(nxt_g)
            nxt_v[...] = jnp.zeros_like(nxt_v)

        outs = []
        for dc, cw, nxt_ref in ((dcg, cwg, nxt_g), (dcv, cwv, nxt_v)):
            nxt = nxt_ref[...]
            outs.append(cw[0:1] * dc + cw[1:2] * _shift_up(dc, nxt, 1) + cw[2:3] * _shift_up(dc, nxt, 2))
            nxt_ref[...] = dc[:8]
        for dc, uu, hh in ((dcg, ug, ugh), (dcv, uv, uvh)):
            outs += [cs(dc * uu), cs(dc * _shift_down(uu, hh, 1)), cs(dc * _shift_down(uu, hh, 2)), cs(dc)]
        return outs

    rows = lambda off: (lambda j, r: (ni - 1 - r, j + off))
    halo = lambda off: (lambda j, r: (jnp.maximum((ni - 1 - r) * (tm // 8) - 1, 0), j + off))
    ins = [(u, (tm, tc), rows(0)), (u, (8, tc), halo(0)),
           (u, (tm, tc), rows(nj)), (u, (8, tc), halo(nj)),
           (conv_w, (3, tc), lambda j, r: (0, j)), (conv_b, (1, tc), lambda j, r: (0, j)),
           (conv_w, (3, tc), lambda j, r: (0, j + nj)), (conv_b, (1, tc), lambda j, r: (0, j + nj)),
           (df, (tm, tc), rows(0))]
    big = ((T, D_FF), BF16, (tm, tc), rows(0), None)
    acc = ((1, D_FF), F32, (1, tc), lambda j, r: (0, j), 1)
    res = tile_call("conv_glu_bwd", fn, (nj, ni), ins, [big, big] + [acc] * 8,
                    scratch=[((8, tc), F32), ((8, tc), F32)])
    dconv_w = jnp.concatenate([jnp.concatenate([res[2 + j], res[6 + j]], axis=1) for j in range(3)], axis=0)
    dconv_b = jnp.concatenate([res[5], res[9]], axis=1)
    return res[0], res[1], dconv_w, dconv_b


def _pad_cols(w, total):
    return jnp.pad(w, ((0, 0), (0, total - w.shape[1])))


def _pad_rows(w, total):
    return jnp.pad(w, ((0, total - w.shape[0]), (0, 0)))


def _proj_pad(w):
    z = lambda n: jnp.zeros((w.shape[0], n), w.dtype)
    return jnp.concatenate([w[:, :1600], z(64), w[:, 1600:1664], z(64), w[:, 1664:1824], z(96), w[:, 1824:],
                            z(PROJ_TAIL)], axis=1)


def _proj_unpad(g):
    return jnp.concatenate([g[:, :1600], g[:, OFF_XA:OFF_XA + 64], g[:, OFF_XG:OFF_XG + 160],
                            g[:, RW_PAD:RW_PAD + ATT_COLS]], axis=1)


def _rw_unpad(g):
    return jnp.concatenate([g[:, :1600], g[:, OFF_XA:OFF_XA + 64], g[:, OFF_XG:OFF_XG + 160]], axis=1)


def rms_fwd(name, x, g, tm=256):
    T, D = x.shape
    return tile_call(name, lambda pid, x_t, g_t: _rms(x_t, g_t), (T // tm,),
                     [_rows(x, tm), _par(g)], [_row_out(T, D, BF16, tm)])[0]


def rms_bwd(name, x, g, dh, dres, with_bf16=True, tm=256):
    T, D = x.shape
    out_dtypes = (F32, BF16) if with_bf16 else (F32,)

    def fn(pid, x_t, g_t, dh_t, dres_t):
        _, vjp = jax.vjp(_rms, x_t, g_t)
        dx, dg = vjp(dh_t.astype(F32))
        return (dres_t + dx,) * len(out_dtypes) + (dg,)

    return tile_call(name, fn, (T // tm,), [_rows(x, tm), _par(g), _rows(dh, tm), _rows(dres, tm)],
                     [_row_out(T, D, dt, tm) for dt in out_dtypes] + [_acc_out(1, D)])


def local_step(x, p, target, W):
    T, D = x.shape
    G = {}

    w_in_p = W["w_in_p"]
    mu_p = _proj_pad(_pad_cols(W["rw_mu"], 4128))[:, :RW_PAD]
    w_up_p = _pad_rows(W["rw_w_up"], 128)
    a_up_p = _pad_rows(W["rw_a_up"], 128)
    g_up_p = _pad_rows(W["rw_g_up"], 256)
    r_k = W["rw_r_k"].reshape(1, RW_WIDTH)
    rw_params = [mu_p, W["rw_w0"], w_up_p, W["rw_a0"], a_up_p, g_up_p, W["rw_k_k"], W["rw_k_a"]]

    h = rms_fwd("rms_mix", x, W["g_mix"])
    proj = matmul("proj_in_rw", h, w_in_p[:, :RW_PAD])
    pa = matmul("proj_in_att", h, w_in_p[:, RW_PAD:RW_PAD + ATT_COLS])
    gp = matmul("proj_gate", h, W["w_gate"])

    tm = 256
    rw_in = (proj, (tm, RW_PAD), lambda i: (i, 0))
    rw_halo = _prev_halo(proj, tm, RW_PAD)

    def rw_pre_tile(pid, Pc, halo, *params):
        prev8 = jnp.where(pid[0] > 0, halo, 0.0)
        params = [q.astype(F32) for q in params]
        return rw_pre(Pc, _shift_down(Pc, prev8, 1), *params)

    r, decay, k2, v, avec, bvec, g = tile_call(
        "rw_pre", rw_pre_tile, (T // tm,), [rw_in, rw_halo] + [_par(q) for q in rw_params],
        [_row_out(T, RW_WIDTH, F32, tm)] * 7)

    vT = _to_head_time(v).astype(BF16)
    wa, ba, ka = scan_pair_terms(avec, decay, bvec, k2)
    (yT, S_all, saT), late_slots = rwkv_scan_fwd(avec, decay, bvec, k2, r, vT, wa, ba, ka,
                                            exchange=_late_weight_sources(W))
    y = _from_head_time(yT)
    W = dict(W, **_late_weights(late_slots))

    post_params = [W["rw_ln_g"], W["rw_ln_b"], r_k]
    ya = tile_call("rw_post", lambda pid, *t: rw_post(*t), (T // tm,),
                   [_rows(z, tm) for z in (y, r, k2, v, g)] + [_par(q) for q in post_params],
                   [_row_out(T, RW_WIDTH, BF16, tm)])[0]

    att = [att_fwd(pa, gi, T) for gi in range(3)]
    o_l = [att[0][0], att[1][0], att[2][0], att[0][1], att[1][1], att[2][1]]
    yb = tile_call("att_combine", lambda pid, *t: att_combine(*t), (T // tm,),
                   [_rows(z, tm) for z in o_l], [_row_out(T, ATT_OUT, BF16, tm)])[0]

    za = matmul("branch_a", ya, W["w_branch_a"])
    zb = matmul("branch_b", yb, W["w_branch_b"])
    merged = tile_call("merge", lambda pid, *t: merge_fn(*t), (T // tm,),
                       [_rows(gp, tm), _par(W["b_gate"]), _rows(za, tm), _rows(zb, tm)],
                       [_row_out(T, D, BF16, tm)])[0]
    x1 = matmul("mix_out", merged, W["w_out"], res=x)

    h2 = rms_fwd("rms_ffn", x1, W["g_ffn"])
    u = matmul("ffn_up", h2, W["w_up"])
    f = conv_glu_fwd(u, W["conv_w"], W["conv_b"])
    x2 = matmul("ffn_down", f, W["w_down"], res=x1)

    h3 = rms_fwd("rms_ple", x2, W["g_ple"])
    zg = matmul("ple_gate", h3, W["w_ple_gate"])
    pe = matmul("ple_embed", p, W["w_ple"])

    def tail_tile(pid, x2_t, zg_t, pe_t, gf, tgt):
        loss, vjp = jax.vjp(lambda a_, b_, c_, d_: tail_loss(a_, b_, c_, d_, tgt), x2_t, zg_t, pe_t, gf)
        dx2, dzg, dpe, dgf = vjp(jnp.ones((), F32))
        return dx2, dzg, dpe, dgf, jnp.full((1, 128), loss, F32)

    tmt = 128
    dx3, dzg, dpe, dgf, loss_acc = tile_call(
        "tail_loss", tail_tile, (T // tmt,),
        [_rows(x2, tmt), _rows(zg, tmt), _rows(pe, tmt), _par(W["g_final"]), _rows(target, tmt)],
        [_row_out(T, D, F32, tmt), _row_out(T, D, BF16, tmt), _row_out(T, D, BF16, tmt),
         _acc_out(1, D), _acc_out(1, 128)])
    loss = loss_acc[0, 0]
    G["g_final"] = dgf

    wgrad = functools.partial(matmul, mode="tn", out_dtype=GRAD_WIRE)
    G["w_ple"] = wgrad("d_w_ple", p, dpe)
    G["w_ple_gate"] = wgrad("d_w_ple_gate", h3, dzg)
    dh3 = matmul("d_h3", dzg, W["w_ple_gate"], "nt")
    dx2, dx2b, G["g_ple"] = rms_bwd("rms_ple_bwd", x2, W["g_ple"], dh3, dx3)

    G["w_down"] = wgrad("d_w_down", f, dx2b)
    df = matmul("d_f", dx2b, W["w_down"], "nt", out_dtype=BF16)
    du_g, du_v, G["conv_w"], G["conv_b"] = conv_glu_bwd(u, W["conv_w"], W["conv_b"], df)
    du = jnp.concatenate([du_g, du_v], axis=1)
    G["w_up"] = wgrad("d_w_up", h2, du)
    dh2 = matmul("d_h2", du, W["w_up"], "nt")
    dx1, dx1b, G["g_ffn"] = rms_bwd("rms_ffn_bwd", x1, W["g_ffn"], dh2, dx2)

    G["w_out"] = wgrad("d_w_out", merged, dx1b)
    dmerged = matmul("d_merged", dx1b, W["w_out"], "nt", out_dtype=BF16)

    def merge_bwd_tile(pid, gp_t, bg, za_t, zb_t, dm_t):
        _, vjp = jax.vjp(merge_fn, gp_t, bg, za_t, zb_t)
        return vjp(dm_t.astype(F32))

    dgp, G["b_gate"], dza, dzb = tile_call(
        "merge_bwd", merge_bwd_tile, (T // tm,),
        [_rows(gp, tm), _par(W["b_gate"]), _rows(za, tm), _rows(zb, tm), _rows(dmerged, tm)],
        [_row_out(T, 2 * D, BF16, tm), _acc_out(1, 2 * D), _row_out(T, D, BF16, tm), _row_out(T, D, BF16, tm)])
    G["w_branch_a"] = wgrad("d_w_branch_a", ya, dza)
    dya = matmul("d_ya", dza, W["w_branch_a"], "nt")
    G["w_branch_b"] = wgrad("d_w_branch_b", yb, dzb)
    dyb = matmul("d_yb", dzb, W["w_branch_b"], "nt")
    G["w_gate"] = wgrad("d_w_gate", h, dgp)
    dh_gate = matmul("d_h_gate", dgp, W["w_gate"], "nt")

    def comb_bwd_tile(pid, *t):
        _, vjp = jax.vjp(att_combine, *t[:6])
        return vjp(t[6])

    d_ol = tile_call("att_combine_bwd", comb_bwd_tile, (T // tm,),
                     [_rows(z, tm) for z in o_l] + [_rows(dyb, tm)],
                     [_row_out(T, ATT_OUT, F32, tm)] * 6)
    dqkv = [att_bwd(pa, att[gi][0], att[gi][1], d_ol[gi], d_ol[3 + gi], gi, T) for gi in range(3)]
    d_att = [dqkv[gi][j] for j in range(3) for gi in range(3)]

    def post_bwd_tile(pid, *t):
        _, vjp = jax.vjp(rw_post, *t[:8])
        return vjp(t[8])

    dy, dr_p, dk2_p, dv_p, dg, G["rw_ln_g"], G["rw_ln_b"], d_rk = tile_call(
        "rw_post_bwd", post_bwd_tile, (T // tm,),
        [_rows(z, tm) for z in (y, r, k2, v, g)] + [_par(q) for q in post_params] + [_rows(dya, tm)],
        [_row_out(T, RW_WIDTH, F32, tm)] * 5 + [_acc_out(1, RW_WIDTH)] * 3)
    G["rw_r_k"] = d_rk.reshape(W["rw_r_k"].shape)

    (da, dw, db, dk_s, dr_s, dvT), G["_early_parts"] = rwkv_scan_bwd(
        avec, decay, bvec, k2, r, v, dy, S_all, saT, exchange=_early_grad_sources(G))
    dv_s = _from_head_time(dvT)

    tmb = 128
    rw_in_b = (proj, (tmb, RW_PAD), lambda i: (i, 0))

    def pre_bwd_tile(pid, Pc, halo, *t):
        prev8 = jnp.where(pid[0] > 0, halo, 0.0)
        params = [q.astype(F32) for q in t[:8]]
        dr1, dr2, dw_, dk1, dk2_, dv1, dv2, da_, db_, dg_ = t[8:]
        _, vjp = jax.vjp(rw_pre, Pc, _shift_down(Pc, prev8, 1), *params)
        return vjp((dr1 + dr2, dw_, dk1 + dk2_, dv1 + dv2, da_, db_, dg_))

    cts = (dr_s, dr_p, dw, dk_s, dk2_p, dv_s, dv_p, da, db, dg)
    res = tile_call(
        "rw_pre_bwd", pre_bwd_tile, (T // tmb,),
        [rw_in_b, _prev_halo(proj, tmb, RW_PAD)] + [_par(q) for q in rw_params] + [_rows(z, tmb) for z in cts],
        [_row_out(T, RW_PAD, F32, tmb)] * 2 + [_acc_out(*q.shape) for q in rw_params])
    dPc, dPs = res[0], res[1]
    d_mu, G["rw_w0"], d_wup, G["rw_a0"], d_aup, d_gup, G["rw_k_k"], G["rw_k_a"] = res[2:]
    G["rw_mu"] = _rw_unpad(d_mu)
    G["rw_w_up"], G["rw_a_up"], G["rw_g_up"] = d_wup[:64], d_aup[:64], d_gup[:160]

    def dproj_tile(pid, dPc_t, dPs_t, nxt, *att_t):
        nxt = jnp.where(pid[0] < T // tm - 1, nxt, 0.0)
        tail = jnp.zeros((dPc_t.shape[0], PROJ_TAIL), F32)
        return jnp.concatenate([dPc_t + _shift_up(dPs_t, nxt, 1)] + list(att_t) + [tail], axis=1)

    dproj = tile_call("d_proj", dproj_tile, (T // tm,),
                      [_rows(dPc, tm), _rows(dPs, tm), _next_halo(dPs, tm, RW_PAD, T)] + [_rows(z, tm) for z in d_att],
                      [_row_out(T, PROJ_PAD, BF16, tm)])[0]
    G["w_in_p"] = wgrad("d_w_in", h, dproj)
    dh = matmul("d_h", dproj, w_in_p, "nt", res=dh_gate)
    dx, G["g_mix"] = rms_bwd("rms_mix_bwd", x, W["g_mix"], dh, dx1, with_bf16=False)
    return loss, dx, G


def _mesh_pos():
    return lax.axis_index("x"), lax.axis_index("y"), lax.axis_index("c")


def _peer(pos, k):
    x, y, c = pos
    px = 1 - x if k & 4 else x
    py = 1 - y if k & 2 else y
    pc = 1 - c if k & 1 else c
    return (px, py, pc), 4 * px + 2 * py + pc


def all_gather_blocks(name, blocks):
    n = len(blocks)

    def body(*refs):
        x_refs, out_refs = refs[:n], refs[n:2 * n]
        send_sems, recv_sems, local_sems = refs[2 * n:]
        x, y, c = _mesh_pos()
        me, sibling = (x, y, c), (x, y, 1 - c)
        chips = [(1 - x, y), (x, 1 - y), (1 - x, 1 - y)]
        ops = range(n)

        def slot(i, px, py, pc):
            return out_refs[i].at[4 * px + 2 * py + pc]

        def copy(k, i, block, to, own=False):
            return pltpu.make_async_remote_copy(
                src_ref=x_refs[i] if own else slot(i, *block), dst_ref=slot(i, *block),
                send_sem=send_sems.at[k, i], recv_sem=recv_sems.at[k, i],
                device_id=to, device_id_type=pl.DeviceIdType.MESH)

        mine = [pltpu.make_async_copy(x_refs[i], slot(i, *me), local_sems.at[i]) for i in ops]
        first = [copy(0, i, me, sibling, own=True) for i in ops]
        first += [copy(1 + j, i, me, (*chip, c), own=True) for j, chip in enumerate(chips) for i in ops]
        for cp in mine + first:
            cp.start()
        passed = []
        for j, chip in enumerate(chips):
            for i in ops:
                copy(1 + j, i, (*chip, c), me).wait_recv()
                passed.append(copy(4 + j, i, (*chip, c), sibling))
                passed[-1].start()
        for i in ops:
            copy(0, i, sibling, me).wait_recv()
        for j, chip in enumerate(chips):
            for i in ops:
                copy(4 + j, i, (*chip, 1 - c), me).wait_recv()
        for cp in first + passed:
            cp.wait_send()
        for cp in mine:
            cp.wait()

    return pl.pallas_call(
        body, name=name,
        in_specs=[pl.BlockSpec(memory_space=pl.ANY)] * n,
        out_specs=[pl.BlockSpec(memory_space=pl.ANY)] * n,
        out_shape=[jax.ShapeDtypeStruct((N_DEV,) + b.shape, b.dtype) for b in blocks],
        scratch_shapes=[pltpu.SemaphoreType.DMA((N_DEV - 1, n)), pltpu.SemaphoreType.DMA((N_DEV - 1, n)),
                        pltpu.SemaphoreType.DMA((n,))],
        compiler_params=pltpu.CompilerParams(has_side_effects=True),
    )(*blocks)


WHOLE = 0


def _exchange_shapes(srcs):
    shapes = [a.shape[1:] if cols is None else a.shape if cols == WHOLE else (a.shape[0], cols) for a, cols in srcs]
    return [jax.ShapeDtypeStruct((N_DEV,) + s, a.dtype) for s, (a, _) in zip(shapes, srcs)]


def _exchange_sems(n):
    return [pltpu.SemaphoreType.DMA((N_DEV - 1, n)), pltpu.SemaphoreType.DMA((N_DEV - 1, n)),
            pltpu.SemaphoreType.DMA((n,))]


def _exchange_ops(col_widths, x_refs, out_refs, send_sems, recv_sems, local_sems):
    n = len(col_widths)
    pos = _mesh_pos()
    me = 4 * pos[0] + 2 * pos[1] + pos[2]

    def piece(i, d):
        cols = col_widths[i]
        if cols is None:
            return x_refs[i].at[d]
        if cols == WHOLE:
            return x_refs[i]
        return x_refs[i].at[:, pl.ds(pl.multiple_of(d * cols, 128), cols)]

    def local(i):
        return pltpu.make_async_copy(piece(i, me), out_refs[i].at[me], local_sems.at[i])

    def remote(k, i, landing):
        peer, idx = _peer(pos, k)
        return pltpu.make_async_remote_copy(
            src_ref=piece(i, idx), dst_ref=out_refs[i].at[idx if landing else me],
            send_sem=send_sems.at[k - 1, i], recv_sem=recv_sems.at[k - 1, i],
            device_id=peer, device_id_type=pl.DeviceIdType.MESH)

    pairs = [(k, i) for k in range(1, N_DEV) for i in range(n)]

    def start():
        for i in range(n):
            local(i).start()
        for k, i in pairs:
            remote(k, i, False).start()

    def wait():
        for k, i in pairs:
            remote(k, i, True).wait_recv()
        for k, i in pairs:
            remote(k, i, False).wait_send()
        for i in range(n):
            local(i).wait()

    return start, wait


def all_to_all_blocks(name, srcs):
    n = len(srcs)

    def body(*refs):
        start, wait = _exchange_ops([c for _, c in srcs], refs[:n], refs[n:2 * n], *refs[2 * n:])
        start()
        wait()

    return pl.pallas_call(
        body, name=name,
        in_specs=[pl.BlockSpec(memory_space=pl.ANY)] * n,
        out_specs=[pl.BlockSpec(memory_space=pl.ANY)] * n,
        out_shape=_exchange_shapes(srcs),
        scratch_shapes=_exchange_sems(n),
        compiler_params=pltpu.CompilerParams(has_side_effects=True),
    )(*[a for a, _ in srcs])


def _adam_row_tile(R, C):
    best = None
    for t in range(16, R + 1, 16):
        if R % t == 0 and t * C <= ADAM_TILE_ELEMS:
            best = t
    return best if best is not None else R


def reduce_adamw(name, parts, w, m, v):
    _, R, C = parts.shape
    tr = _adam_row_tile(R, C)

    def fn(pid, parts_t, w_t, m_t, v_t):
        g = parts_t[0].astype(F32)
        for i in range(1, N_DEV):
            g = g + parts_t[i].astype(F32)
        m_n = ADAM_B1 * m_t + (1.0 - ADAM_B1) * g
        v_n = ADAM_B2 * v_t + (1.0 - ADAM_B2) * (g * g)
        m_hat = m_n / (1.0 - ADAM_B1 ** ADAM_STEP)
        v_hat = v_n / (1.0 - ADAM_B2 ** ADAM_STEP)
        delta = -ADAM_LR * (m_hat / (jnp.sqrt(v_hat) + ADAM_EPS) + ADAM_WD * w_t)
        return g, delta, m_n, v_n

    row = lambda a: (a, (tr, C), lambda i: (i, 0))
    out = ((R, C), F32, (tr, C), lambda i: (i, 0), None)
    return tile_call(name, fn, (R // tr,),
                     [(parts, (N_DEV, tr, C), lambda i: (0, i, 0)), row(w), row(m), row(v)], [out] * 4)


PARAMS = (
    ("g_mix", (1, 1024), None), ("w_in", (1024, 4128), 1), ("rw_mu", (1, 1824), None), ("rw_w0", (1, 512), None),
    ("rw_w_up", (64, 512), 1), ("rw_a0", (1, 512), None), ("rw_a_up", (64, 512), 1), ("rw_g_up", (160, 512), 1),
    ("rw_k_k", (1, 512), None), ("rw_k_a", (1, 512), None), ("rw_r_k", (8, 64), None), ("rw_ln_g", (1, 512), None),
    ("rw_ln_b", (1, 512), None), ("w_branch_a", (512, 1024), 1), ("w_branch_b", (256, 1024), 1),
    ("w_gate", (1024, 2048), 1), ("b_gate", (1, 2048), None), ("w_out", (1024, 1024), 0), ("g_ffn", (1, 1024), None),
    ("w_up", (1024, 6144), 1), ("conv_w", (3, 6144), 1), ("conv_b", (1, 6144), None), ("w_down", (3072, 1024), 0),
    ("g_ple", (1, 1024), None), ("w_ple_gate", (1024, 1024), 0), ("w_ple", (256, 1024), 1), ("g_final", (1, 1024), None),
)
SHARDED = tuple(q for q in PARAMS if q[2] is not None)
REPLICATED = tuple(q for q in PARAMS if q[2] is None)
BIG_NAMES = ("w_in", "w_up", "w_gate", "w_out", "w_down", "w_ple_gate", "w_branch_a", "w_branch_b", "w_ple")
BIG = tuple(q for q in SHARDED if q[0] in BIG_NAMES)
SMALL_SHARDED = tuple(q for q in SHARDED if q[0] not in BIG_NAMES)
PACK_COLS = 1024
F32_GATHERED = ("conv_w",)


def _local_shape(shape, axis):
    s = list(shape)
    s[axis] //= N_DEV
    return tuple(s)


def _numel(shape):
    return int(np.prod(shape))


def _pad_flat(z, mult):
    n = z.shape[-1]
    total = -(-n // mult) * mult
    return jnp.pad(z, [(0, 0)] * (z.ndim - 1) + [(0, total - n)])


def _full_from_slots(slots, shape, axis):
    loc = _local_shape(shape, axis)
    z = slots.reshape((N_DEV,) + loc)
    if axis == 0:
        return z.reshape(shape)
    return z.transpose(1, 0, 2).reshape(shape)


def _slots_from_full(full, shape, axis):
    loc = _local_shape(shape, axis)
    if axis == 0:
        return full.reshape(N_DEV, _numel(loc))
    return full.reshape(shape[0], N_DEV, loc[1]).transpose(1, 0, 2).reshape(N_DEV, _numel(loc))


W_IN_SLOT = 640
W_IN_LOCAL = 4128 // N_DEV


def _block_shape(shape, axis):
    return _local_shape(shape, axis) if axis is not None else shape


def _pad_w_in(block):
    return jnp.pad(block, ((0, 0), (0, W_IN_SLOT - W_IN_LOCAL)))


def _proj_col(s):
    return s + jnp.where(s >= 1600, 64, 0) + jnp.where(s >= 1664, 64, 0) + jnp.where(s >= 1824, 96, 0)


def _perm_tile(d, c0, width):
    j = lax.broadcasted_iota(jnp.int32, (W_IN_SLOT, width), 0)
    c = c0 + lax.broadcasted_iota(jnp.int32, (W_IN_SLOT, width), 1)
    hit = (_proj_col(d * W_IN_LOCAL + j) == c) & (j < W_IN_LOCAL)
    return jnp.where(hit, 1.0, 0.0).astype(BF16)


PERM_TILE = 768


def w_in_unshuffle(slots):
    _, K, _ = slots.shape
    tn = PERM_TILE
    reach = 3

    def first_slot(j):
        return j + jnp.where(j >= 3, 1, 0) + jnp.where(j >= 5, 1, 0)

    def body(a_ref, o_ref, acc_ref):
        j, kk = pl.program_id(0), pl.program_id(1)
        d = first_slot(j) + kk

        @pl.when(kk == 0)
        def _():
            acc_ref[...] = jnp.zeros_like(acc_ref)

        @pl.when(d < N_DEV)
        def _():
            acc_ref[...] += jnp.dot(a_ref[0], _perm_tile(d, j * tn, tn), preferred_element_type=F32)

        @pl.when(kk == reach - 1)
        def _():
            o_ref[...] = acc_ref[...].astype(o_ref.dtype)

    return pl.pallas_call(
        body, name="w_in_unshuffle", grid=(PROJ_PAD // tn, reach),
        in_specs=[pl.BlockSpec((1, K, W_IN_SLOT), lambda j, kk: (jnp.minimum(first_slot(j) + kk, N_DEV - 1), 0, 0))],
        out_specs=pl.BlockSpec((K, tn), lambda j, kk: (0, j)),
        out_shape=jax.ShapeDtypeStruct((K, PROJ_PAD), BF16),
        scratch_shapes=[pltpu.VMEM((K, tn), F32)],
        compiler_params=_cparams(2),
    )(slots)


def w_in_shuffle_grad(dw):
    K = dw.shape[0]
    tk = PERM_TILE

    def first_tile(d):
        return _proj_col(d * W_IN_LOCAL) // tk

    def body(g_ref, o_ref, acc_ref):
        d, kk = pl.program_id(0), pl.program_id(1)
        perm = _perm_tile(d, (first_tile(d) + kk) * tk, tk)
        part = lax.dot_general(g_ref[...].astype(BF16), perm, NT_DIMS, preferred_element_type=F32)

        @pl.when(kk == 0)
        def _():
            acc_ref[...] = part

        @pl.when(kk == 1)
        def _():
            o_ref[0] = (acc_ref[...] + part).astype(o_ref.dtype)

    return pl.pallas_call(
        body, name="w_in_shuffle_grad", grid=(N_DEV, 2),
        in_specs=[pl.BlockSpec((K, tk), lambda d, kk: (0, first_tile(d) + kk))],
        out_specs=pl.BlockSpec((1, K, W_IN_SLOT), lambda d, kk: (d, 0, 0)),
        out_shape=jax.ShapeDtypeStruct((N_DEV, K, W_IN_SLOT), GRAD_WIRE),
        scratch_shapes=[pltpu.VMEM((K, W_IN_SLOT), F32)],
        compiler_params=_cparams(2),
    )(dw)


def _flat_rows(pieces, dtype, row_mult):
    flat = jnp.concatenate([z.astype(dtype) for z in pieces], axis=-1)
    flat = _pad_flat(flat, row_mult * PACK_COLS)
    return flat.reshape(flat.shape[:-1] + (-1, PACK_COLS))


FIRST = tuple(q for q in BIG if q[0] in ("w_in", "w_gate"))
LATE = tuple(q for q in BIG if q not in FIRST)


def _matrix_from_slots(slots, shape, axis):
    return slots.reshape(shape) if axis == 0 else slots.transpose(1, 0, 2).reshape(shape)


def _late_weight_sources(W):
    return [(blk, WHOLE) for blk in W["_late_blocks"]]


def _late_weights(slots):
    return {n: _matrix_from_slots(s, shape, axis) for (n, shape, axis), s in zip(LATE, slots)}


def gather_weights(local):
    blocks = [(_pad_w_in(local[n]) if n == "w_in" else local[n]).astype(BF16) for n, _, _ in FIRST]
    small = [q for q in SMALL_SHARDED if q[0] not in F32_GATHERED]
    exact = [q for q in SMALL_SHARDED if q[0] in F32_GATHERED]
    blocks.append(_flat_rows([local[n].reshape(-1) for n, _, _ in small], BF16, 16))
    blocks.append(_flat_rows([local[n].reshape(-1) for n, _, _ in exact], F32, 8))
    got = all_gather_blocks("weight_all_gather", blocks)
    full = {"_late_blocks": [local[n].astype(BF16) for n, _, _ in LATE]}
    for (n, shape, axis), slots in zip(FIRST, got):
        if n == "w_in":
            full["w_in_p"] = w_in_unshuffle(slots)
        else:
            full[n] = _matrix_from_slots(slots, shape, axis)
    for group, slots in ((small, got[-2]), (exact, got[-1])):
        slots, off = slots.reshape(N_DEV, -1), 0
        for n, shape, axis in group:
            size = _numel(_local_shape(shape, axis))
            full[n] = _full_from_slots(slots[:, off:off + size], shape, axis)
            off += size
    for n, _, _ in REPLICATED:
        full[n] = local[n]
    return full


LOSS_SLOT = ("_loss", (1, 2), None)
PACKED_SMALL = SMALL_SHARDED + REPLICATED + (LOSS_SLOT,)


def _pack_small(vals):
    pieces = [vals[n].reshape(-1) if n in vals else jnp.zeros((_numel(shape),), F32) for n, shape, _ in PACKED_SMALL]
    return _flat_rows(pieces, F32, 16)


def _unpack_small(packed):
    flat, out, off = packed.reshape(-1), {}, 0
    for n, shape, axis in PACKED_SMALL:
        loc = _block_shape(shape, axis)
        out[n] = flat[off:off + _numel(loc)].reshape(loc)
        off += _numel(loc)
    return out


EARLY = tuple(q for q in BIG if q[0] != "w_in")


def _early_grad_sources(G):
    srcs = []
    for n, shape, axis in EARLY:
        if axis == 0:
            srcs.append((G[n].astype(GRAD_WIRE).reshape((N_DEV,) + _local_shape(shape, axis)), None))
        else:
            srcs.append((G[n].astype(GRAD_WIRE), shape[1] // N_DEV))
    return srcs


def _late_grad_sources(G, loss_local):
    srcs = [(w_in_shuffle_grad(G["w_in_p"]), None)]
    rows = [_slots_from_full(G[n].reshape(shape), shape, axis) for n, shape, axis in SMALL_SHARDED]
    loss_hi = loss_local.astype(GRAD_WIRE).astype(F32)
    rep = jnp.concatenate([G[n].reshape(-1) for n, _, _ in REPLICATED] + [jnp.stack([loss_hi, loss_local - loss_hi])])
    rows.append(jnp.broadcast_to(rep[None, :], (N_DEV, rep.shape[0])))
    srcs.append((_flat_rows(rows, GRAD_WIRE, 16), None))
    return srcs


def _step(x, p, target, local_w, local_m, local_v):
    full = gather_weights(local_w)
    loss_local, dx, G = local_step(x, p, target, full)
    late = all_to_all_blocks("grad_all_to_all", _late_grad_sources(G, loss_local))
    parts = [late[0]] + list(G["_early_parts"]) + [late[1]]
    outs = [{}, {}, {}, {}]
    for (n, shape, axis), part in zip((BIG[0],) + EARLY, parts):
        prep = _pad_w_in if n == "w_in" else (lambda z: z)
        res = reduce_adamw("adamw_" + n, part, prep(local_w[n]), prep(local_m[n]), prep(local_v[n]))
        for o, z in zip(outs, res):
            o[n] = z[:, :W_IN_LOCAL] if n == "w_in" else z
    res = reduce_adamw("adamw_small", parts[-1], _pack_small(local_w), _pack_small(local_m), _pack_small(local_v))
    for o, z in zip(outs, res):
        o.update(_unpack_small(z))
    loss = jnp.sum(outs[0]["_loss"])
    return loss, dx, outs


def kernel(x, p, g_mix, w_in, rw_mu, rw_w0, rw_w_up, rw_a0, rw_a_up, rw_g_up, rw_k_k, rw_k_a, rw_r_k, rw_ln_g, rw_ln_b, w_branch_a, w_branch_b, w_gate, b_gate, w_out, g_ffn, w_up, conv_w, conv_b, w_down, g_ple, w_ple_gate, w_ple, g_final, loss_target, m_g_mix, m_w_in, m_rw_mu, m_rw_w0, m_rw_w_up, m_rw_a0, m_rw_a_up, m_rw_g_up, m_rw_k_k, m_rw_k_a, m_rw_r_k, m_rw_ln_g, m_rw_ln_b, m_w_branch_a, m_w_branch_b, m_w_gate, m_b_gate, m_w_out, m_g_ffn, m_w_up, m_conv_w, m_conv_b, m_w_down, m_g_ple, m_w_ple_gate, m_w_ple, m_g_final, v_g_mix, v_w_in, v_rw_mu, v_rw_w0, v_rw_w_up, v_rw_a0, v_rw_a_up, v_rw_g_up, v_rw_k_k, v_rw_k_a, v_rw_r_k, v_rw_ln_g, v_rw_ln_b, v_w_branch_a, v_w_branch_b, v_w_gate, v_b_gate, v_w_out, v_g_ffn, v_w_up, v_conv_w, v_conv_b, v_w_down, v_g_ple, v_w_ple_gate, v_w_ple, v_g_final):
    args = dict(locals())
    names = [n for n, _, _ in PARAMS]
    orig_shape = {n: args[n].shape for n in names}

    def strip(prefix):
        out = {}
        for n, shape, axis in PARAMS:
            a = args[prefix + n]
            loc = _local_shape(shape, axis) if axis is not None else shape
            out[n] = a.reshape(loc)
        return out

    local_w, local_m, local_v = strip(""), strip("m_"), strip("v_")
    T, D = x.shape[-2], x.shape[-1]
    loss, dx, (g, delta, m_n, v_n) = _step(x.reshape(T, D), p.reshape(T, p.shape[-1]), loss_target.reshape(T, D),
                                           local_w, local_m, local_v)
    outs = [loss, dx.reshape(x.shape)]
    for group in (g, delta, m_n, v_n):
        outs += [group[n].reshape(orig_shape[n]) for n in names]
    return tuple(outs)
```

```python
import functools
import math

import numpy as np
import jax
import jax.numpy as jnp
from jax import lax
from jax.experimental import pallas as pl
from jax.experimental.pallas import tpu as pltpu

F32 = jnp.float32
BF16 = jnp.bfloat16
GRAD_WIRE = jnp.bfloat16

N_DEV = 8
NORM_EPS = 1e-6
RW_LN_EPS = 64e-5
HEAD = 64
RW_WIDTH = 512
ATT_GROUPS = ((128, 1), (512, 4), (2048, 16))
ATT_HEADS = 12
ATT_OUT = 256
ATT_COLS = 2304
OFF_XW, OFF_XA, OFF_XG, RW_PAD, PROJ_PAD = 1536, 1664, 1792, 2048, 4608
PROJ_TAIL = PROJ_PAD - RW_PAD - ATT_COLS
D_FF = 3072

ADAM_LR, ADAM_B1, ADAM_B2, ADAM_EPS, ADAM_WD, ADAM_STEP = 0.001, 0.9, 0.999, 1e-08, 0.01, 10

VMEM_LIMIT_BYTES = 56 * 1024 * 1024
ADAM_TILE_ELEMS = 256 * 1024
NEG_BIG = -1e30

NT_DIMS = (((1,), (1,)), ((), ()))
TN_DIMS = (((0,), (0,)), ((), ()))
NN_DIMS = (((1,), (0,)), ((), ()))


def _cparams(n_axes):
    return pltpu.CompilerParams(dimension_semantics=("arbitrary",) * n_axes,
                                vmem_limit_bytes=VMEM_LIMIT_BYTES)


def _split2(x):
    hi = x.astype(BF16)
    lo = (x - hi.astype(F32)).astype(BF16)
    return hi, lo


def _seg_mat(n):
    r = lax.shift_right_logical(lax.broadcasted_iota(jnp.int32, (n, n), 0), 6)
    c = lax.shift_right_logical(lax.broadcasted_iota(jnp.int32, (n, n), 1), 6)
    return jnp.where(r == c, 1.0, 0.0).astype(BF16)


def _segb(x, seg):
    hi, lo = _split2(x)
    return (jnp.dot(hi, seg, preferred_element_type=F32)
            + jnp.dot(lo, seg, preferred_element_type=F32))


def _segb1(x, seg):
    return jnp.dot(x.astype(BF16), seg, preferred_element_type=F32)


@jax.custom_vjp
def segsum(x):
    return _segb(x, _seg_mat(x.shape[1]))


def _segsum_fwd(x):
    return segsum(x), None


def _segsum_bwd(_, ct):
    return (segsum(ct),)


segsum.defvjp(_segsum_fwd, _segsum_bwd)


@jax.custom_vjp
def bdot(a, b):
    return jnp.dot(a.astype(BF16), b.astype(BF16), preferred_element_type=F32)


def _bdot_fwd(a, b):
    return bdot(a, b), (a, b)


def _bdot_bwd(res, ct):
    a, b = res
    ctb = ct.astype(BF16)
    da = lax.dot_general(ctb, b.astype(BF16), NT_DIMS, preferred_element_type=F32)
    db = lax.dot_general(a.astype(BF16), ctb, TN_DIMS, preferred_element_type=F32)
    return da.astype(a.dtype), db.astype(b.dtype)


bdot.defvjp(_bdot_fwd, _bdot_bwd)


def _sig(x):
    return 1.0 / (1.0 + jnp.exp(-x))


def _softplus(z):
    return jnp.maximum(z, 0.0) + jnp.log(1.0 + jnp.exp(-jnp.abs(z)))


def _gelu_tanh(x):
    return 0.5 * x * (1.0 + jnp.tanh(0.7978845608028654 * (x + 0.044715 * (x * x * x))))


def _rms(x, g):
    return x * lax.rsqrt(jnp.mean(x * x, axis=-1, keepdims=True) + NORM_EPS) * g


def _shift_down(x, prev8, n):
    rolled = pltpu.roll(x, n, 0)
    top = pltpu.roll(prev8, n, 0)
    rid = lax.broadcasted_iota(jnp.int32, (8, x.shape[1]), 0)
    head = jnp.where(rid < n, top, rolled[:8])
    return jnp.concatenate([head, rolled[8:]], axis=0)


def _shift_up(x, next8, n):
    rows = x.shape[0]
    rolled = pltpu.roll(x, rows - n, 0)
    bottom = pltpu.roll(next8, 8 - n, 0)
    rid = lax.broadcasted_iota(jnp.int32, (8, x.shape[1]), 0)
    tail = jnp.where(rid >= 8 - n, bottom, rolled[rows - 8:])
    return jnp.concatenate([rolled[:rows - 8], tail], axis=0)


def tile_call(name, fn, grid, ins, outs, scratch=()):
    n_in, n_out = len(ins), len(outs)
    acc_axes = [o[4] for o in outs]

    def body(*refs):
        pids = tuple(pl.program_id(a) for a in range(len(grid)))
        vals = fn(pids, *[r[...] for r in refs[:n_in]], *refs[n_in + n_out:])
        if not isinstance(vals, (tuple, list)):
            vals = (vals,)
        for o_ref, val, ax in zip(refs[n_in:n_in + n_out], vals, acc_axes):
            if ax is None:
                o_ref[...] = val.astype(o_ref.dtype)
            else:
                @pl.when(pids[ax] == 0)
                def _(o_ref=o_ref):
                    o_ref[...] = jnp.zeros_like(o_ref)

                o_ref[...] += val.astype(o_ref.dtype)

    res = pl.pallas_call(
        body, name=name, grid=grid,
        in_specs=[pl.BlockSpec(b, im) for _, b, im in ins],
        out_specs=[pl.BlockSpec(o[2], o[3]) for o in outs],
        out_shape=[jax.ShapeDtypeStruct(o[0], o[1]) for o in outs],
        scratch_shapes=[pltpu.VMEM(s, d) for s, d in scratch],
        compiler_params=_cparams(len(grid)),
    )(*[a for a, _, _ in ins])
    return res


def _rows(a, tm):
    return (a, (tm, a.shape[1]), lambda i: (i, 0))


def _par(a):
    return (a, a.shape, lambda i: (0, 0))


def _row_out(T, C, dtype, tm):
    return ((T, C), dtype, (tm, C), lambda i: (i, 0), None)


def _acc_out(R, C):
    return ((R, C), F32, (R, C), lambda i: (0, 0), 0)


def _prev_halo(a, tm, C):
    return (a, (8, C), lambda i: (jnp.maximum(i * (tm // 8) - 1, 0), 0))


def _next_halo(a, tm, C, T):
    return (a, (8, C), lambda i: (jnp.minimum((i + 1) * (tm // 8), T // 8 - 1), 0))


def _pick(n, target):
    for t in (target, 1024, 768, 512, 384, 256, 128):
        if t <= target and n % t == 0:
            return t
    return n


def matmul(name, a, b, mode="nn", res=None, out_dtype=F32, tm=1024, tn=1024, tk=1024):
    if mode == "nn":
        (M, K), (K2, N) = a.shape, b.shape
    elif mode == "tn":
        (K, M), (K2, N) = a.shape, b.shape
    else:
        (M, K), (N, K2) = a.shape, b.shape
    assert K == K2, (name, a.shape, b.shape, mode)
    tm, tn, tk = _pick(M, tm), _pick(N, tn), _pick(K, tk)
    nk = K // tk
    dims = {"nn": NN_DIMS, "tn": TN_DIMS, "nt": NT_DIMS}[mode]
    a_spec = {"nn": pl.BlockSpec((tm, tk), lambda i, j, k: (i, k)),
              "tn": pl.BlockSpec((tk, tm), lambda i, j, k: (k, i)),
              "nt": pl.BlockSpec((tm, tk), lambda i, j, k: (i, k))}[mode]
    b_spec = {"nn": pl.BlockSpec((tk, tn), lambda i, j, k: (k, j)),
              "tn": pl.BlockSpec((tk, tn), lambda i, j, k: (k, j)),
              "nt": pl.BlockSpec((tn, tk), lambda i, j, k: (j, k))}[mode]
    has_res = res is not None

    def body(*refs):
        if has_res:
            a_ref, b_ref, r_ref, o_ref, acc_ref = refs
        else:
            a_ref, b_ref, o_ref, acc_ref = refs
        k = pl.program_id(2)

        @pl.when(k == 0)
        def _():
            acc_ref[...] = jnp.zeros_like(acc_ref)

        acc_ref[...] += lax.dot_general(a_ref[...].astype(BF16), b_ref[...].astype(BF16), dims,
                                        preferred_element_type=F32)

        @pl.when(k == nk - 1)
        def _():
            out = acc_ref[...]
            if has_res:
                out = out + r_ref[...].astype(F32)
            o_ref[...] = out.astype(o_ref.dtype)

    in_specs = [a_spec, b_spec]
    args = [a, b]
    if has_res:
        in_specs.append(pl.BlockSpec((tm, tn), lambda i, j, k: (i, j)))
        args.append(res)
    return pl.pallas_call(
        body, name=name, grid=(M // tm, N // tn, nk),
        in_specs=in_specs,
        out_specs=pl.BlockSpec((tm, tn), lambda i, j, k: (i, j)),
        out_shape=jax.ShapeDtypeStruct((M, N), out_dtype),
        scratch_shapes=[pltpu.VMEM((tm, tn), F32)],
        compiler_params=_cparams(3),
    )(*args)


def rw_pre(Pc, Ps, mu, w0, w_up, a0, a_up, g_up, k_k, k_a):
    Pm = Pc + (Ps - Pc) * mu
    r, k, v = Pm[:, 0:512], Pm[:, 512:1024], Pm[:, 1024:1536]
    xw, xa, xg = Pm[:, OFF_XW:OFF_XA], Pm[:, OFF_XA:OFF_XG], Pm[:, OFF_XG:RW_PAD]
    w = -_softplus(-(w0 + bdot(jnp.tanh(xw), w_up))) - 0.5
    decay = jnp.exp(-jnp.exp(w))
    a = _sig(a0 + bdot(xa, a_up))
    g = bdot(_sig(xg), g_up)
    kk = k * k_k
    kk = kk / jnp.maximum(jnp.sqrt(segsum(kk * kk)), 1e-12)
    k2 = k * (1.0 + (a - 1.0) * k_a)
    return r, decay, k2, v, -kk, kk * a, g


def rw_post(y, r, k2, v, g, ln_g, ln_b, r_k):
    mean = segsum(y) * (1.0 / HEAD)
    d = y - mean
    var = segsum(d * d) * (1.0 / HEAD)
    yn = d * lax.rsqrt(var + RW_LN_EPS) * ln_g + ln_b
    bonus = segsum(r * k2 * r_k) * v
    return (yn + bonus) * g


def att_combine(o1, o2, o3, l1, l2, l3):
    m = jnp.maximum(jnp.maximum(l1, l2), l3)
    e1, e2, e3 = jnp.exp(l1 - m), jnp.exp(l2 - m), jnp.exp(l3 - m)
    return (e1 * o1 + e2 * o2 + e3 * o3) / (e1 + e2 + e3)


def merge_fn(gp, bg, za, zb):
    s = _sig(gp + bg)
    half = za.shape[1]
    return s[:, :half] * za + s[:, half:] * zb


def tail_loss(x2, zg, pe, g_final, target):
    x3 = x2 + _sig(zg) * pe
    y = _rms(x3, g_final)
    err = (y - target) * (y - target)
    return 0.5 * jnp.sum(jnp.mean(err, axis=-1, keepdims=True))


SCAN_CHUNK = HEAD
SCAN_LANES = 256
SCAN_UNROLL_FWD, SCAN_UNROLL_BWD = 4, 4


def _to_head_time(z):
    T = z.shape[0]
    return z.reshape(T // HEAD, HEAD, RW_WIDTH // HEAD, HEAD).transpose(0, 3, 2, 1).reshape(T // HEAD, HEAD, RW_WIDTH)


def _from_head_time(zt):
    C = zt.shape[0]
    return zt.reshape(C, HEAD, RW_WIDTH // HEAD, HEAD).transpose(0, 3, 2, 1).reshape(C * HEAD, RW_WIDTH)


def _unrolled_loop(n, step, init, unroll):
    def body(i, carry):
        for j in range(unroll):
            carry = step(i * unroll + j, carry)
        return carry

    return lax.fori_loop(0, n // unroll, body, init)


def _lane_groups():
    return [slice(j * SCAN_LANES, (j + 1) * SCAN_LANES) for j in range(RW_WIDTH // SCAN_LANES)]


def scan_pair_terms(a, w, b, k, tm=256):
    T = a.shape[0]

    def fn(pid, a_t, nxt, w_t, b_t, k_t):
        a_next = _shift_up(a_t, jnp.where(pid[0] < T // tm - 1, nxt, 0.0), 1)
        return w_t * a_next, segsum(b_t * a_next), segsum(k_t * a_next)

    return tile_call("scan_pair_terms", fn, (T // tm,),
                     [_rows(a, tm), _next_halo(a, tm, RW_WIDTH, T), _rows(w, tm), _rows(b, tm), _rows(k, tm)],
                     [_row_out(T, RW_WIDTH, F32, tm)] * 3)


def _column_broadcast(z):
    T, n_heads = z.shape[0], RW_WIDTH // HEAD
    down = z.astype(BF16).reshape(T, n_heads, HEAD).transpose(0, 2, 1)
    return jnp.broadcast_to(down[:, :, :, None], (T, HEAD, n_heads, HEAD)).reshape(T, HEAD, RW_WIDTH)


def rwkv_scan_fwd(a, w, b, k, r, vb, wa, ba, ka, exchange=()):
    T = a.shape[0]
    C, LW = SCAN_CHUNK, SCAN_LANES
    nC = T // C
    nx = len(exchange)

    def body(*refs):
        a_ref, w_ref, b_ref, k_ref, r_ref, vb_ref, wa_ref, ba_ref, ka_ref = refs[:9]
        x_refs, refs = refs[9:9 + nx], refs[9 + nx:]
        yT_ref, S_ref, saT_ref = refs[:3]
        land_refs, refs = refs[3:3 + nx], refs[3 + nx:]
        st_ref, seg_ref = refs[:2]
        if nx:
            start, wait = _exchange_ops([c for _, c in exchange], x_refs, land_refs, *refs[2:])

        @pl.when(pl.program_id(0) == 0)
        def _():
            st_ref[...] = jnp.zeros_like(st_ref)
            seg_ref[...] = _seg_mat(LW)
            if nx:
                start()

        seg = seg_ref[...]
        lane = jnp.bitwise_and(lax.broadcasted_iota(jnp.int32, (1, LW), 1), HEAD - 1)
        groups = _lane_groups()

        saT_ref[...] = jnp.zeros_like(saT_ref)

        def pair(i, yacc):
            t = 2 * i
            t1 = t + 1
            tp = jnp.maximum(t - 1, 0)
            row = lambda ref, s, gsl: ref[pl.ds(s, 1), gsl]
            Sps = [st_ref[:, gsl] for gsl in groups]
            sas = [_segb(Sp * row(a_ref, t, gsl), seg) for gsl, Sp in zip(groups, Sps)]
            us = [_segb(Sp * row(wa_ref, t, gsl), seg) for gsl, Sp in zip(groups, Sps)]
            S1s = []
            for gsl, Sp, sa, u in zip(groups, Sps, sas, us):
                vb0, vb1 = vb_ref[t, :, gsl].astype(F32), vb_ref[t1, :, gsl].astype(F32)
                S1 = Sp * row(w_ref, t, gsl) + sa * row(b_ref, t, gsl) + vb0 * row(k_ref, t, gsl)
                sa1 = u + sa * row(ba_ref, t, gsl) + vb0 * row(ka_ref, t, gsl)
                st_ref[:, gsl] = S1 * row(w_ref, t1, gsl) + sa1 * row(b_ref, t1, gsl) + vb1 * row(k_ref, t1, gsl)
                S_ref[0, t, :, gsl] = Sp
                S_ref[0, t1, :, gsl] = S1
                S1s.append(S1)
                saT_ref[0, :, gsl] = jnp.where(lane == t, sa, jnp.where(lane == t1, sa1, saT_ref[0, :, gsl]))
            out = []
            for gsl, Sp, S1, ya in zip(groups, Sps, S1s, yacc):
                yb0 = _segb1(Sp * row(r_ref, tp, gsl), seg)
                yb1 = _segb1(S1 * row(r_ref, t, gsl), seg)
                out.append(jnp.where(lane == t, yb1, jnp.where(lane == t - 1, yb0, ya)))
            return tuple(out)

        yacc = _unrolled_loop(C // 2, pair, tuple(jnp.zeros((HEAD, LW), F32) for _ in groups), SCAN_UNROLL_FWD)
        for gsl, ya in zip(groups, yacc):
            S_last = st_ref[:, gsl]
            S_ref[0, C, :, gsl] = S_last
            yb = _segb1(S_last * r_ref[pl.ds(C - 1, 1), gsl], seg)
            yT_ref[0, :, gsl] = jnp.where(lane == C - 1, yb, ya)

        if nx:
            @pl.when(pl.program_id(0) == nC - 1)
            def _():
                wait()

    row = pl.BlockSpec((C, RW_WIDTH), lambda c: (c, 0))
    ht = pl.BlockSpec((1, HEAD, RW_WIDTH), lambda c: (c, 0, 0))
    hbm = pl.BlockSpec(memory_space=pl.ANY)
    res = pl.pallas_call(
        body, name="rwkv_scan_fwd", grid=(nC,),
        in_specs=[row, row, row, row, row, pl.BlockSpec((C, HEAD, RW_WIDTH), lambda c: (c, 0, 0)), row, row, row]
        + [hbm] * nx,
        out_specs=[ht, pl.BlockSpec((1, C + 1, HEAD, RW_WIDTH), lambda c: (c, 0, 0, 0)), ht] + [hbm] * nx,
        out_shape=[jax.ShapeDtypeStruct((nC, HEAD, RW_WIDTH), F32),
                   jax.ShapeDtypeStruct((nC, C + 1, HEAD, RW_WIDTH), F32),
                   jax.ShapeDtypeStruct((nC, HEAD, RW_WIDTH), F32)] + _exchange_shapes(exchange),
        scratch_shapes=[pltpu.VMEM((HEAD, RW_WIDTH), F32), pltpu.VMEM((LW, LW), BF16)]
        + (_exchange_sems(nx) if nx else []),
        compiler_params=pltpu.CompilerParams(dimension_semantics=("arbitrary",), vmem_limit_bytes=VMEM_LIMIT_BYTES,
                                             has_side_effects=bool(nx)),
    )(a, w, b, k, r, vb, wa, ba, ka, *[z for z, _ in exchange])
    return res[:3], res[3:]


def rwkv_scan_bwd(a, w, b, k, r, v, dy, S_all, saT, exchange=()):
    T = a.shape[0]
    C, LW = SCAN_CHUNK, SCAN_LANES
    nC = T // C
    nx = len(exchange)
    n_heads = RW_WIDTH // HEAD
    dyb_all = _column_broadcast(dy)
    v_rows, dy_rows = v.reshape(T, n_heads, HEAD), dy.reshape(T, n_heads, HEAD)
    sa_rows = _from_head_time(saT).reshape(T, n_heads, HEAD)

    def body(*refs):
        a_ref, w_ref, b_ref, k_ref, r_ref, vR_ref, saR_ref, dyR_ref, dyb_ref, S_ref = refs[:10]
        x_refs, refs = refs[10:10 + nx], refs[10 + nx:]
        da_ref, dw_ref, db_ref, dk_ref, dr_ref, dvT_ref = refs[:6]
        land_refs, refs = refs[6:6 + nx], refs[6 + nx:]
        ds_ref, seg_ref = refs[:2]
        if nx:
            start, wait = _exchange_ops([c for _, c in exchange], x_refs, land_refs, *refs[2:])

        @pl.when(pl.program_id(0) == 0)
        def _():
            ds_ref[...] = jnp.zeros_like(ds_ref)
            seg_ref[...] = _seg_mat(LW)
            if nx:
                start()

        seg = seg_ref[...]
        lane = jnp.bitwise_and(lax.broadcasted_iota(jnp.int32, (1, LW), 1), HEAD - 1)
        groups = _lane_groups()
        head_row = lax.broadcasted_iota(jnp.int32, (n_heads, LW), 0)
        lane_head = lax.shift_right_logical(lax.broadcasted_iota(jnp.int32, (n_heads, LW), 1), 6)

        def colsum(z):
            return jnp.sum(z, axis=0, keepdims=True)


        def step(i, dvacc):
            t = C - 1 - i
            dybs = [dyb_ref[t, :, gsl].astype(F32) for gsl in groups]
            dSs = [ds_ref[:, gsl] + dyb * r_ref[pl.ds(t, 1), gsl] for gsl, dyb in zip(groups, dybs)]
            dsabs = [_segb(dS * b_ref[pl.ds(t, 1), gsl], seg) for gsl, dS in zip(groups, dSs)]
            for gsl, dS, dsab in zip(groups, dSs, dsabs):
                ds_ref[:, gsl] = dS * w_ref[pl.ds(t, 1), gsl] + dsab * a_ref[pl.ds(t, 1), gsl]
            out = []
            dy_rows = dyR_ref[t].astype(BF16)
            v_sa_rows = jnp.concatenate([vR_ref[t], saR_ref[t]], axis=0).astype(BF16)
            for g, (gsl, dva, dS, dsab) in enumerate(zip(groups, dvacc, dSs, dsabs)):
                kr = k_ref[pl.ds(t, 1), gsl]
                Sp = S_ref[0, t, :, gsl]
                own = head_row == lane_head + g * (LW // HEAD)

                def rows_in(rows, mat):
                    full = jnp.dot(rows, mat.astype(BF16), preferred_element_type=F32)
                    return [jnp.sum(jnp.where(own, full[s:s + n_heads], 0.0), axis=0, keepdims=True)
                            for s in range(0, rows.shape[0], n_heads)]

                (dr,) = rows_in(dy_rows, S_ref[0, t + 1, :, gsl])
                dk, db = rows_in(v_sa_rows, dS)
                dr_ref[pl.ds(t, 1), gsl] = dr
                dk_ref[pl.ds(t, 1), gsl] = dk
                db_ref[pl.ds(t, 1), gsl] = db
                dvb = _segb1(dS * kr, seg)
                dw_ref[pl.ds(t, 1), gsl] = colsum(dS * Sp)
                da_ref[pl.ds(t, 1), gsl] = colsum(Sp * dsab)
                out.append(jnp.where(lane == t, dvb, dva))
            return tuple(out)

        dvacc = _unrolled_loop(C, step, tuple(jnp.zeros((HEAD, LW), F32) for _ in groups), SCAN_UNROLL_BWD)
        for gsl, dva in zip(groups, dvacc):
            dvT_ref[0, :, gsl] = dva

        if nx:
            @pl.when(pl.program_id(0) == nC - 1)
            def _():
                wait()

    row = pl.BlockSpec((C, RW_WIDTH), lambda c: (nC - 1 - c, 0))
    ht = pl.BlockSpec((1, HEAD, RW_WIDTH), lambda c: (nC - 1 - c, 0, 0))
    hbm = pl.BlockSpec(memory_space=pl.ANY)
    per_head = pl.BlockSpec((C, n_heads, HEAD), lambda c: (nC - 1 - c, 0, 0))
    rows_shape = jax.ShapeDtypeStruct((T, RW_WIDTH), F32)
    res = pl.pallas_call(
        body, name="rwkv_scan_bwd", grid=(nC,),
        in_specs=[row, row, row, row, row, per_head, per_head, per_head,
                  pl.BlockSpec((C, HEAD, RW_WIDTH), lambda c: (nC - 1 - c, 0, 0)),
                  pl.BlockSpec((1, C + 1, HEAD, RW_WIDTH), lambda c: (nC - 1 - c, 0, 0, 0))] + [hbm] * nx,
        out_specs=[row, row, row, row, row, ht] + [hbm] * nx,
        out_shape=[rows_shape] * 5 + [jax.ShapeDtypeStruct((nC, HEAD, RW_WIDTH), F32)] + _exchange_shapes(exchange),
        scratch_shapes=[pltpu.VMEM((HEAD, RW_WIDTH), F32), pltpu.VMEM((LW, LW), BF16)]
        + (_exchange_sems(nx) if nx else []),
        compiler_params=pltpu.CompilerParams(dimension_semantics=("arbitrary",), vmem_limit_bytes=VMEM_LIMIT_BYTES,
                                             has_side_effects=bool(nx)),
    )(a, w, b, k, r, v_rows, sa_rows, dy_rows, dyb_all, S_all, *[z for z, _ in exchange])
    return res[:6], res[6:]


def _alibi_slope(h):
    return float(np.float32(2.0 ** (-8.0 * (h + 1) / ATT_HEADS)))


ATT_GROUP_HEADS = 4


def _stack_heads(x, lane_head, fill=0.0):
    return jnp.concatenate([jnp.where(lane_head == hh, x, fill) for hh in range(ATT_GROUP_HEADS)], axis=0)


def _unstack_heads(x, lane_head, L):
    out = jnp.zeros((L, x.shape[1]), F32)
    for hh in range(ATT_GROUP_HEADS):
        out = jnp.where(lane_head == hh, x[hh * L:(hh + 1) * L], out)
    return out


def _att_logits(qs, kcat, gi, d, L, n):
    qi = lax.broadcasted_iota(jnp.int32, (L, 2 * L), 0)
    kj = lax.broadcasted_iota(jnp.int32, (L, 2 * L), 1)
    steps = qi + L - kj
    valid = (steps >= 0) & (steps <= L) & ((kj >= L) | (n > 0))
    dist = (d * steps).astype(F32)
    bias = jnp.concatenate([jnp.where(valid, -_alibi_slope(gi * ATT_GROUP_HEADS + hh) * dist, NEG_BIG)
                            for hh in range(ATT_GROUP_HEADS)], axis=0)
    s = lax.dot_general(qs.astype(BF16), kcat, NT_DIMS, preferred_element_type=F32) * (HEAD ** -0.5)
    return jnp.where(bias > 0.5 * NEG_BIG, s + bias, NEG_BIG)


def att_fwd(pa, gi, T):
    window, d = ATT_GROUPS[gi]
    L = window // d
    Tj = T // d
    nb = Tj // L
    pv = pa.reshape(Tj, d * ATT_COLS)
    nblk = ATT_COLS // ATT_OUT

    def fn(pids, q, kp, kc, vp, vc):
        lane_head = lax.shift_right_logical(lax.broadcasted_iota(jnp.int32, (1, ATT_OUT), 1), 6)
        kcat = jnp.concatenate([kp, kc], axis=0).astype(BF16)
        vcat = jnp.concatenate([vp, vc], axis=0).astype(BF16)
        s = _att_logits(_stack_heads(q, lane_head), kcat, gi, d, L, pids[1])
        m = jnp.max(s, axis=-1, keepdims=True)
        p = jnp.exp(s - m)
        l = jnp.sum(p, axis=-1, keepdims=True)
        o = jnp.dot(p.astype(BF16), vcat, preferred_element_type=F32) / l
        lse = jnp.broadcast_to(m + jnp.log(l), o.shape)
        return _unstack_heads(o, lane_head, L), _unstack_heads(lse, lane_head, L)

    blk = (L, ATT_OUT)
    ins = [(pv, blk, lambda r, n: (n, r * nblk + gi)),
           (pv, blk, lambda r, n: (jnp.maximum(n - 1, 0), r * nblk + 3 + gi)),
           (pv, blk, lambda r, n: (n, r * nblk + 3 + gi)),
           (pv, blk, lambda r, n: (jnp.maximum(n - 1, 0), r * nblk + 6 + gi)),
           (pv, blk, lambda r, n: (n, r * nblk + 6 + gi))]
    out = ((Tj, d * ATT_OUT), F32, blk, lambda r, n: (n, r), None)
    o, lseb = tile_call(f"att_fwd_g{gi}", fn, (d, nb), ins, [out, out])
    return o.reshape(T, ATT_OUT), lseb.reshape(T, ATT_OUT)


def att_bwd(pa, o, lseb, do, dlseb, gi, T):
    window, d = ATT_GROUPS[gi]
    L = window // d
    Tj = T // d
    nb = Tj // L
    pv = pa.reshape(Tj, d * ATT_COLS)
    nblk = ATT_COLS // ATT_OUT
    view = lambda z: z.reshape(Tj, d * ATT_OUT)

    def body(q_ref, kp_ref, kc_ref, vp_ref, vc_ref, o_ref, l_ref, do_ref, dl_ref, dq_ref, dk_ref, dv_ref):
        n = pl.program_id(1)

        @pl.when(n == 0)
        def _():
            dk_ref[...] = jnp.zeros_like(dk_ref)
            dv_ref[...] = jnp.zeros_like(dv_ref)

        lane_head = lax.shift_right_logical(lax.broadcasted_iota(jnp.int32, (1, ATT_OUT), 1), 6)
        kcat = jnp.concatenate([kp_ref[...], kc_ref[...]], axis=0).astype(BF16)
        vcat = jnp.concatenate([vp_ref[...], vc_ref[...]], axis=0).astype(BF16)
        qs = _stack_heads(q_ref[...], lane_head)
        dos = _stack_heads(do_ref[...], lane_head)
        lse = jnp.max(_stack_heads(l_ref[...], lane_head, NEG_BIG), axis=-1, keepdims=True)
        dlse = jnp.sum(_stack_heads(dl_ref[...], lane_head), axis=-1, keepdims=True)
        delta = jnp.sum(dos * jnp.concatenate([o_ref[...]] * ATT_GROUP_HEADS, axis=0), axis=-1, keepdims=True)
        p = jnp.exp(_att_logits(qs, kcat, gi, d, L, n) - lse)
        dp = lax.dot_general(dos.astype(BF16), vcat, NT_DIMS, preferred_element_type=F32)
        ds = (p * (dp - delta + dlse)).astype(BF16)
        dq = _unstack_heads(jnp.dot(ds, kcat, preferred_element_type=F32), lane_head, L)
        dkc = lax.dot_general(ds, qs.astype(BF16), TN_DIMS, preferred_element_type=F32)
        dvc = lax.dot_general(p.astype(BF16), dos.astype(BF16), TN_DIMS, preferred_element_type=F32)
        scale = HEAD ** -0.5
        dq_ref[...] = dq * scale
        cur = pl.ds(pl.multiple_of(n * L, L), L)
        dk_ref[cur, :] += dkc[L:] * scale
        dv_ref[cur, :] += dvc[L:]

        @pl.when(n > 0)
        def _():
            prev = pl.ds(pl.multiple_of((n - 1) * L, L), L)
            dk_ref[prev, :] += dkc[:L] * scale
            dv_ref[prev, :] += dvc[:L]

    blk = pl.BlockSpec((L, ATT_OUT), lambda r, n: (n, r))
    res = pl.BlockSpec((Tj, ATT_OUT), lambda r, n: (0, r))
    qspec = lambda off, prev: pl.BlockSpec(
        (L, ATT_OUT), (lambda r, n: (jnp.maximum(n - 1, 0), r * nblk + off + gi)) if prev
        else (lambda r, n: (n, r * nblk + off + gi)))
    shape = jax.ShapeDtypeStruct((Tj, d * ATT_OUT), F32)
    dq, dk, dv = pl.pallas_call(
        body, name=f"att_bwd_g{gi}", grid=(d, nb),
        in_specs=[qspec(0, False), qspec(3, True), qspec(3, False), qspec(6, True), qspec(6, False),
                  blk, blk, blk, blk],
        out_specs=[blk, res, res],
        out_shape=[shape, shape, shape],
        compiler_params=_cparams(2),
    )(pv, pv, pv, pv, pv, view(o), view(lseb), view(do), view(dlseb))
    return dq.reshape(T, ATT_OUT), dk.reshape(T, ATT_OUT), dv.reshape(T, ATT_OUT)


FFN_TM, FFN_TC = 512, 512


def _conv3(u, prev8, cw, cb):
    return cb + cw[0:1] * u + cw[1:2] * _shift_down(u, prev8, 1) + cw[2:3] * _shift_down(u, prev8, 2)


def conv_glu_fwd(u, conv_w, conv_b):
    T = u.shape[0]
    tm, tc = FFN_TM, FFN_TC
    nj, ni = D_FF // tc, T // tm

    def fn(pids, ug, ugh, uv, uvh, cwg, cbg, cwv, cbv):
        first = pids[1] > 0
        cg = _conv3(ug, jnp.where(first, ugh, 0.0), cwg, cbg)
        cv = _conv3(uv, jnp.where(first, uvh, 0.0), cwv, cbv)
        return _gelu_tanh(cg) * cv

    halo = lambda off: (lambda j, i: (jnp.maximum(i * (tm // 8) - 1, 0), j + off))
    ins = [(u, (tm, tc), lambda j, i: (i, j)), (u, (8, tc), halo(0)),
           (u, (tm, tc), lambda j, i: (i, j + nj)), (u, (8, tc), halo(nj)),
           (conv_w, (3, tc), lambda j, i: (0, j)), (conv_b, (1, tc), lambda j, i: (0, j)),
           (conv_w, (3, tc), lambda j, i: (0, j + nj)), (conv_b, (1, tc), lambda j, i: (0, j + nj))]
    out = ((T, D_FF), BF16, (tm, tc), lambda j, i: (i, j), None)
    return tile_call("conv_glu_fwd", fn, (nj, ni), ins, [out])[0]


def conv_glu_bwd(u, conv_w, conv_b, df):
    T = u.shape[0]
    tm, tc = FFN_TM, FFN_TC
    nj, ni = D_FF // tc, T // tm

    def fn(pids, ug, ugh, uv, uvh, cwg, cbg, cwv, cbv, df_t, nxt_g, nxt_v):
        i = ni - 1 - pids[1]
        ugh = jnp.where(i > 0, ugh, 0.0)
        uvh = jnp.where(i > 0, uvh, 0.0)
        cg = _conv3(ug, ugh, cwg, cbg)
        cv = _conv3(uv, uvh, cwv, cbv)
        _, vjp = jax.vjp(lambda g_, v_: _gelu_tanh(g_) * v_, cg, cv)
        dcg, dcv = vjp(df_t.astype(F32))
        cs = lambda z: jnp.sum(z, axis=0, keepdims=True)

        @pl.when(pids[1] == 0)
        def _():
            nxt_g[...] = jnp.zeros_like(nxt_g)
            nxt_v[...] = jnp.zeros_like(nxt_v)

        outs = []
        for dc, cw, nxt_ref in ((dcg, cwg, nxt_g), (dcv, cwv, nxt_v)):
            nxt = nxt_ref[...]
            outs.append(cw[0:1] * dc + cw[1:2] * _shift_up(dc, nxt, 1) + cw[2:3] * _shift_up(dc, nxt, 2))
            nxt_ref[...] = dc[:8]
        for dc, uu, hh in ((dcg, ug, ugh), (dcv, uv, uvh)):
            outs += [cs(dc * uu), cs(dc * _shift_down(uu, hh, 1)), cs(dc * _shift_down(uu, hh, 2)), cs(dc)]
        return outs

    rows = lambda off: (lambda j, r: (ni - 1 - r, j + off))
    halo = lambda off: (lambda j, r: (jnp.maximum((ni - 1 - r) * (tm // 8) - 1, 0), j + off))
    ins = [(u, (tm, tc), rows(0)), (u, (8, tc), halo(0)),
           (u, (tm, tc), rows(nj)), (u, (8, tc), halo(nj)),
           (conv_w, (3, tc), lambda j, r: (0, j)), (conv_b, (1, tc), lambda j, r: (0, j)),
           (conv_w, (3, tc), lambda j, r: (0, j + nj)), (conv_b, (1, tc), lambda j, r: (0, j + nj)),
           (df, (tm, tc), rows(0))]
    big = ((T, D_FF), BF16, (tm, tc), rows(0), None)
    acc = ((1, D_FF), F32, (1, tc), lambda j, r: (0, j), 1)
    res = tile_call("conv_glu_bwd", fn, (nj, ni), ins, [big, big] + [acc] * 8,
                    scratch=[((8, tc), F32), ((8, tc), F32)])
    dconv_w = jnp.concatenate([jnp.concatenate([res[2 + j], res[6 + j]], axis=1) for j in range(3)], axis=0)
    dconv_b = jnp.concatenate([res[5], res[9]], axis=1)
    return res[0], res[1], dconv_w, dconv_b


def _pad_cols(w, total):
    return jnp.pad(w, ((0, 0), (0, total - w.shape[1])))


def _pad_rows(w, total):
    return jnp.pad(w, ((0, total - w.shape[0]), (0, 0)))


def _proj_pad(w):
    z = lambda n: jnp.zeros((w.shape[0], n), w.dtype)
    return jnp.concatenate([w[:, :1600], z(64), w[:, 1600:1664], z(64), w[:, 1664:1824], z(96), w[:, 1824:],
                            z(PROJ_TAIL)], axis=1)


def _proj_unpad(g):
    return jnp.concatenate([g[:, :1600], g[:, OFF_XA:OFF_XA + 64], g[:, OFF_XG:OFF_XG + 160],
                            g[:, RW_PAD:RW_PAD + ATT_COLS]], axis=1)


def _rw_unpad(g):
    return jnp.concatenate([g[:, :1600], g[:, OFF_XA:OFF_XA + 64], g[:, OFF_XG:OFF_XG + 160]], axis=1)


def rms_fwd(name, x, g, tm=256):
    T, D = x.shape
    return tile_call(name, lambda pid, x_t, g_t: _rms(x_t, g_t), (T // tm,),
                     [_rows(x, tm), _par(g)], [_row_out(T, D, BF16, tm)])[0]


def rms_bwd(name, x, g, dh, dres, with_bf16=True, tm=256):
    T, D = x.shape
    out_dtypes = (F32, BF16) if with_bf16 else (F32,)

    def fn(pid, x_t, g_t, dh_t, dres_t):
        _, vjp = jax.vjp(_rms, x_t, g_t)
        dx, dg = vjp(dh_t.astype(F32))
        return (dres_t + dx,) * len(out_dtypes) + (dg,)

    return tile_call(name, fn, (T // tm,), [_rows(x, tm), _par(g), _rows(dh, tm), _rows(dres, tm)],
                     [_row_out(T, D, dt, tm) for dt in out_dtypes] + [_acc_out(1, D)])


def local_step(x, p, target, W):
    T, D = x.shape
    G = {}

    w_in_p = W["w_in_p"]
    mu_p = _proj_pad(_pad_cols(W["rw_mu"], 4128))[:, :RW_PAD]
    w_up_p = _pad_rows(W["rw_w_up"], 128)
    a_up_p = _pad_rows(W["rw_a_up"], 128)
    g_up_p = _pad_rows(W["rw_g_up"], 256)
    r_k = W["rw_r_k"].reshape(1, RW_WIDTH)
    rw_params = [mu_p, W["rw_w0"], w_up_p, W["rw_a0"], a_up_p, g_up_p, W["rw_k_k"], W["rw_k_a"]]

    h = rms_fwd("rms_mix", x, W["g_mix"])
    proj = matmul("proj_in_rw", h, w_in_p[:, :RW_PAD])
    pa = matmul("proj_in_att", h, w_in_p[:, RW_PAD:RW_PAD + ATT_COLS])
    gp = matmul("proj_gate", h, W["w_gate"])

    tm = 256
    rw_in = (proj, (tm, RW_PAD), lambda i: (i, 0))
    rw_halo = _prev_halo(proj, tm, RW_PAD)

    def rw_pre_tile(pid, Pc, halo, *params):
        prev8 = jnp.where(pid[0] > 0, halo, 0.0)
        params = [q.astype(F32) for q in params]
        return rw_pre(Pc, _shift_down(Pc, prev8, 1), *params)

    r, decay, k2, v, avec, bvec, g = tile_call(
        "rw_pre", rw_pre_tile, (T // tm,), [rw_in, rw_halo] + [_par(q) for q in rw_params],
        [_row_out(T, RW_WIDTH, F32, tm)] * 7)

    wa, ba, ka = scan_pair_terms(avec, decay, bvec, k2)
    (yT, S_all, saT), late_slots = rwkv_scan_fwd(avec, decay, bvec, k2, r, _column_broadcast(v), wa, ba, ka,
                                            exchange=_late_weight_sources(W))
    y = _from_head_time(yT)
    W = dict(W, **_late_weights(late_slots))

    post_params = [W["rw_ln_g"], W["rw_ln_b"], r_k]
    ya = tile_call("rw_post", lambda pid, *t: rw_post(*t), (T // tm,),
                   [_rows(z, tm) for z in (y, r, k2, v, g)] + [_par(q) for q in post_params],
                   [_row_out(T, RW_WIDTH, BF16, tm)])[0]

    att = [att_fwd(pa, gi, T) for gi in range(3)]
    o_l = [att[0][0], att[1][0], att[2][0], att[0][1], att[1][1], att[2][1]]
    yb = tile_call("att_combine", lambda pid, *t: att_combine(*t), (T // tm,),
                   [_rows(z, tm) for z in o_l], [_row_out(T, ATT_OUT, BF16, tm)])[0]

    za = matmul("branch_a", ya, W["w_branch_a"])
    zb = matmul("branch_b", yb, W["w_branch_b"])
    merged = tile_call("merge", lambda pid, *t: merge_fn(*t), (T // tm,),
                       [_rows(gp, tm), _par(W["b_gate"]), _rows(za, tm), _rows(zb, tm)],
                       [_row_out(T, D, BF16, tm)])[0]
    x1 = matmul("mix_out", merged, W["w_out"], res=x)

    h2 = rms_fwd("rms_ffn", x1, W["g_ffn"])
    u = matmul("ffn_up", h2, W["w_up"])
    f = conv_glu_fwd(u, W["conv_w"], W["conv_b"])
    x2 = matmul("ffn_down", f, W["w_down"], res=x1)

    h3 = rms_fwd("rms_ple", x2, W["g_ple"])
    zg = matmul("ple_gate", h3, W["w_ple_gate"])
    pe = matmul("ple_embed", p, W["w_ple"])

    def tail_tile(pid, x2_t, zg_t, pe_t, gf, tgt):
        loss, vjp = jax.vjp(lambda a_, b_, c_, d_: tail_loss(a_, b_, c_, d_, tgt), x2_t, zg_t, pe_t, gf)
        dx2, dzg, dpe, dgf = vjp(jnp.ones((), F32))
        return dx2, dzg, dpe, dgf, jnp.full((1, 128), loss, F32)

    tmt = 128
    dx3, dzg, dpe, dgf, loss_acc = tile_call(
        "tail_loss", tail_tile, (T // tmt,),
        [_rows(x2, tmt), _rows(zg, tmt), _rows(pe, tmt), _par(W["g_final"]), _rows(target, tmt)],
        [_row_out(T, D, F32, tmt), _row_out(T, D, BF16, tmt), _row_out(T, D, BF16, tmt),
         _acc_out(1, D), _acc_out(1, 128)])
    loss = loss_acc[0, 0]
    G["g_final"] = dgf

    wgrad = functools.partial(matmul, mode="tn", out_dtype=GRAD_WIRE)
    G["w_ple"] = wgrad("d_w_ple", p, dpe)
    G["w_ple_gate"] = wgrad("d_w_ple_gate", h3, dzg)
    dh3 = matmul("d_h3", dzg, W["w_ple_gate"], "nt")
    dx2, dx2b, G["g_ple"] = rms_bwd("rms_ple_bwd", x2, W["g_ple"], dh3, dx3)

    G["w_down"] = wgrad("d_w_down", f, dx2b)
    df = matmul("d_f", dx2b, W["w_down"], "nt", out_dtype=BF16)
    du_g, du_v, G["conv_w"], G["conv_b"] = conv_glu_bwd(u, W["conv_w"], W["conv_b"], df)
    du = jnp.concatenate([du_g, du_v], axis=1)
    G["w_up"] = wgrad("d_w_up", h2, du)
    dh2 = matmul("d_h2", du, W["w_up"], "nt")
    dx1, dx1b, G["g_ffn"] = rms_bwd("rms_ffn_bwd", x1, W["g_ffn"], dh2, dx2)

    G["w_out"] = wgrad("d_w_out", merged, dx1b)
    dmerged = matmul("d_merged", dx1b, W["w_out"], "nt", out_dtype=BF16)

    def merge_bwd_tile(pid, gp_t, bg, za_t, zb_t, dm_t):
        _, vjp = jax.vjp(merge_fn, gp_t, bg, za_t, zb_t)
        return vjp(dm_t.astype(F32))

    dgp, G["b_gate"], dza, dzb = tile_call(
        "merge_bwd", merge_bwd_tile, (T // tm,),
        [_rows(gp, tm), _par(W["b_gate"]), _rows(za, tm), _rows(zb, tm), _rows(dmerged, tm)],
        [_row_out(T, 2 * D, BF16, tm), _acc_out(1, 2 * D), _row_out(T, D, BF16, tm), _row_out(T, D, BF16, tm)])
    G["w_branch_a"] = wgrad("d_w_branch_a", ya, dza)
    dya = matmul("d_ya", dza, W["w_branch_a"], "nt")
    G["w_branch_b"] = wgrad("d_w_branch_b", yb, dzb)
    dyb = matmul("d_yb", dzb, W["w_branch_b"], "nt")
    G["w_gate"] = wgrad("d_w_gate", h, dgp)
    dh_gate = matmul("d_h_gate", dgp, W["w_gate"], "nt")

    def comb_bwd_tile(pid, *t):
        _, vjp = jax.vjp(att_combine, *t[:6])
        return vjp(t[6])

    d_ol = tile_call("att_combine_bwd", comb_bwd_tile, (T // tm,),
                     [_rows(z, tm) for z in o_l] + [_rows(dyb, tm)],
                     [_row_out(T, ATT_OUT, F32, tm)] * 6)
    dqkv = [att_bwd(pa, att[gi][0], att[gi][1], d_ol[gi], d_ol[3 + gi], gi, T) for gi in range(3)]
    d_att = [dqkv[gi][j] for j in range(3) for gi in range(3)]

    def post_bwd_tile(pid, *t):
        _, vjp = jax.vjp(rw_post, *t[:8])
        return vjp(t[8])

    dy, dr_p, dk2_p, dv_p, dg, G["rw_ln_g"], G["rw_ln_b"], d_rk = tile_call(
        "rw_post_bwd", post_bwd_tile, (T // tm,),
        [_rows(z, tm) for z in (y, r, k2, v, g)] + [_par(q) for q in post_params] + [_rows(dya, tm)],
        [_row_out(T, RW_WIDTH, F32, tm)] * 5 + [_acc_out(1, RW_WIDTH)] * 3)
    G["rw_r_k"] = d_rk.reshape(W["rw_r_k"].shape)

    (da, dw, db, dk_s, dr_s, dvT), G["_early_parts"] = rwkv_scan_bwd(
        avec, decay, bvec, k2, r, v, dy, S_all, saT, exchange=_early_grad_sources(G))
    dv_s = _from_head_time(dvT)

    tmb = 128
    rw_in_b = (proj, (tmb, RW_PAD), lambda i: (i, 0))

    def pre_bwd_tile(pid, Pc, halo, *t):
        prev8 = jnp.where(pid[0] > 0, halo, 0.0)
        params = [q.astype(F32) for q in t[:8]]
        dr1, dr2, dw_, dk1, dk2_, dv1, dv2, da_, db_, dg_ = t[8:]
        _, vjp = jax.vjp(rw_pre, Pc, _shift_down(Pc, prev8, 1), *params)
        return vjp((dr1 + dr2, dw_, dk1 + dk2_, dv1 + dv2, da_, db_, dg_))

    cts = (dr_s, dr_p, dw, dk_s, dk2_p, dv_s, dv_p, da, db, dg)
    res = tile_call(
        "rw_pre_bwd", pre_bwd_tile, (T // tmb,),
        [rw_in_b, _prev_halo(proj, tmb, RW_PAD)] + [_par(q) for q in rw_params] + [_rows(z, tmb) for z in cts],
        [_row_out(T, RW_PAD, F32, tmb)] * 2 + [_acc_out(*q.shape) for q in rw_params])
    dPc, dPs = res[0], res[1]
    d_mu, G["rw_w0"], d_wup, G["rw_a0"], d_aup, d_gup, G["rw_k_k"], G["rw_k_a"] = res[2:]
    G["rw_mu"] = _rw_unpad(d_mu)
    G["rw_w_up"], G["rw_a_up"], G["rw_g_up"] = d_wup[:64], d_aup[:64], d_gup[:160]

    def dproj_tile(pid, dPc_t, dPs_t, nxt, *att_t):
        nxt = jnp.where(pid[0] < T // tm - 1, nxt, 0.0)
        tail = jnp.zeros((dPc_t.shape[0], PROJ_TAIL), F32)
        return jnp.concatenate([dPc_t + _shift_up(dPs_t, nxt, 1)] + list(att_t) + [tail], axis=1)

    dproj = tile_call("d_proj", dproj_tile, (T // tm,),
                      [_rows(dPc, tm), _rows(dPs, tm), _next_halo(dPs, tm, RW_PAD, T)] + [_rows(z, tm) for z in d_att],
                      [_row_out(T, PROJ_PAD, BF16, tm)])[0]
    G["w_in_p"] = wgrad("d_w_in", h, dproj)
    dh = matmul("d_h", dproj, w_in_p, "nt", res=dh_gate)
    dx, G["g_mix"] = rms_bwd("rms_mix_bwd", x, W["g_mix"], dh, dx1, with_bf16=False)
    return loss, dx, G


def _mesh_pos():
    return lax.axis_index("x"), lax.axis_index("y"), lax.axis_index("c")


def _peer(pos, k):
    x, y, c = pos
    px = 1 - x if k & 4 else x
    py = 1 - y if k & 2 else y
    pc = 1 - c if k & 1 else c
    return (px, py, pc), 4 * px + 2 * py + pc


def all_gather_blocks(name, blocks):
    n = len(blocks)

    def body(*refs):
        x_refs, out_refs = refs[:n], refs[n:2 * n]
        send_sems, recv_sems, local_sems = refs[2 * n:]
        x, y, c = _mesh_pos()
        me, sibling = (x, y, c), (x, y, 1 - c)
        chips = [(1 - x, y), (x, 1 - y), (1 - x, 1 - y)]
        ops = range(n)

        def slot(i, px, py, pc):
            return out_refs[i].at[4 * px + 2 * py + pc]

        def copy(k, i, block, to, own=False):
            return pltpu.make_async_remote_copy(
                src_ref=x_refs[i] if own else slot(i, *block), dst_ref=slot(i, *block),
                send_sem=send_sems.at[k, i], recv_sem=recv_sems.at[k, i],
                device_id=to, device_id_type=pl.DeviceIdType.MESH)

        mine = [pltpu.make_async_copy(x_refs[i], slot(i, *me), local_sems.at[i]) for i in ops]
        first = [copy(0, i, me, sibling, own=True) for i in ops]
        first += [copy(1 + j, i, me, (*chip, c), own=True) for j, chip in enumerate(chips) for i in ops]
        for cp in mine + first:
            cp.start()
        passed = []
        for j, chip in enumerate(chips):
            for i in ops:
                copy(1 + j, i, (*chip, c), me).wait_recv()
                passed.append(copy(4 + j, i, (*chip, c), sibling))
                passed[-1].start()
        for i in ops:
            copy(0, i, sibling, me).wait_recv()
        for j, chip in enumerate(chips):
            for i in ops:
                copy(4 + j, i, (*chip, 1 - c), me).wait_recv()
        for cp in first + passed:
            cp.wait_send()
        for cp in mine:
            cp.wait()

    return pl.pallas_call(
        body, name=name,
        in_specs=[pl.BlockSpec(memory_space=pl.ANY)] * n,
        out_specs=[pl.BlockSpec(memory_space=pl.ANY)] * n,
        out_shape=[jax.ShapeDtypeStruct((N_DEV,) + b.shape, b.dtype) for b in blocks],
        scratch_shapes=[pltpu.SemaphoreType.DMA((N_DEV - 1, n)), pltpu.SemaphoreType.DMA((N_DEV - 1, n)),
                        pltpu.SemaphoreType.DMA((n,))],
        compiler_params=pltpu.CompilerParams(has_side_effects=True),
    )(*blocks)


WHOLE = 0


def _exchange_shapes(srcs):
    shapes = [a.shape[1:] if cols is None else a.shape if cols == WHOLE else (a.shape[0], cols) for a, cols in srcs]
    return [jax.ShapeDtypeStruct((N_DEV,) + s, a.dtype) for s, (a, _) in zip(shapes, srcs)]


def _exchange_sems(n):
    return [pltpu.SemaphoreType.DMA((N_DEV - 1, n)), pltpu.SemaphoreType.DMA((N_DEV - 1, n)),
            pltpu.SemaphoreType.DMA((n,))]


def _exchange_ops(col_widths, x_refs, out_refs, send_sems, recv_sems, local_sems):
    n = len(col_widths)
    pos = _mesh_pos()
    me = 4 * pos[0] + 2 * pos[1] + pos[2]

    def piece(i, d):
        cols = col_widths[i]
        if cols is None:
            return x_refs[i].at[d]
        if cols == WHOLE:
            return x_refs[i]
        return x_refs[i].at[:, pl.ds(pl.multiple_of(d * cols, 128), cols)]

    def local(i):
        return pltpu.make_async_copy(piece(i, me), out_refs[i].at[me], local_sems.at[i])

    def remote(k, i, landing):
        peer, idx = _peer(pos, k)
        return pltpu.make_async_remote_copy(
            src_ref=piece(i, idx), dst_ref=out_refs[i].at[idx if landing else me],
            send_sem=send_sems.at[k - 1, i], recv_sem=recv_sems.at[k - 1, i],
            device_id=peer, device_id_type=pl.DeviceIdType.MESH)

    pairs = [(k, i) for k in range(1, N_DEV) for i in range(n)]

    def start():
        for i in range(n):
            local(i).start()
        for k, i in pairs:
            remote(k, i, False).start()

    def wait():
        for k, i in pairs:
            remote(k, i, True).wait_recv()
        for k, i in pairs:
            remote(k, i, False).wait_send()
        for i in range(n):
            local(i).wait()

    return start, wait


def all_to_all_blocks(name, srcs):
    n = len(srcs)

    def body(*refs):
        start, wait = _exchange_ops([c for _, c in srcs], refs[:n], refs[n:2 * n], *refs[2 * n:])
        start()
        wait()

    return pl.pallas_call(
        body, name=name,
        in_specs=[pl.BlockSpec(memory_space=pl.ANY)] * n,
        out_specs=[pl.BlockSpec(memory_space=pl.ANY)] * n,
        out_shape=_exchange_shapes(srcs),
        scratch_shapes=_exchange_sems(n),
        compiler_params=pltpu.CompilerParams(has_side_effects=True),
    )(*[a for a, _ in srcs])


def _adam_row_tile(R, C):
    best = None
    for t in range(16, R + 1, 16):
        if R % t == 0 and t * C <= ADAM_TILE_ELEMS:
            best = t
    return best if best is not None else R


def reduce_adamw(name, parts, w, m, v):
    _, R, C = parts.shape
    tr = _adam_row_tile(R, C)

    def fn(pid, parts_t, w_t, m_t, v_t):
        g = parts_t[0].astype(F32)
        for i in range(1, N_DEV):
            g = g + parts_t[i].astype(F32)
        m_n = ADAM_B1 * m_t + (1.0 - ADAM_B1) * g
        v_n = ADAM_B2 * v_t + (1.0 - ADAM_B2) * (g * g)
        m_hat = m_n / (1.0 - ADAM_B1 ** ADAM_STEP)
        v_hat = v_n / (1.0 - ADAM_B2 ** ADAM_STEP)
        delta = -ADAM_LR * (m_hat / (jnp.sqrt(v_hat) + ADAM_EPS) + ADAM_WD * w_t)
        return g, delta, m_n, v_n

    row = lambda a: (a, (tr, C), lambda i: (i, 0))
    out = ((R, C), F32, (tr, C), lambda i: (i, 0), None)
    return tile_call(name, fn, (R // tr,),
                     [(parts, (N_DEV, tr, C), lambda i: (0, i, 0)), row(w), row(m), row(v)], [out] * 4)


PARAMS = (
    ("g_mix", (1, 1024), None), ("w_in", (1024, 4128), 1), ("rw_mu", (1, 1824), None), ("rw_w0", (1, 512), None),
    ("rw_w_up", (64, 512), 1), ("rw_a0", (1, 512), None), ("rw_a_up", (64, 512), 1), ("rw_g_up", (160, 512), 1),
    ("rw_k_k", (1, 512), None), ("rw_k_a", (1, 512), None), ("rw_r_k", (8, 64), None), ("rw_ln_g", (1, 512), None),
    ("rw_ln_b", (1, 512), None), ("w_branch_a", (512, 1024), 1), ("w_branch_b", (256, 1024), 1),
    ("w_gate", (1024, 2048), 1), ("b_gate", (1, 2048), None), ("w_out", (1024, 1024), 0), ("g_ffn", (1, 1024), None),
    ("w_up", (1024, 6144), 1), ("conv_w", (3, 6144), 1), ("conv_b", (1, 6144), None), ("w_down", (3072, 1024), 0),
    ("g_ple", (1, 1024), None), ("w_ple_gate", (1024, 1024), 0), ("w_ple", (256, 1024), 1), ("g_final", (1, 1024), None),
)
SHARDED = tuple(q for q in PARAMS if q[2] is not None)
REPLICATED = tuple(q for q in PARAMS if q[2] is None)
BIG_NAMES = ("w_in", "w_up", "w_gate", "w_out", "w_down", "w_ple_gate", "w_branch_a", "w_branch_b", "w_ple")
BIG = tuple(q for q in SHARDED if q[0] in BIG_NAMES)
SMALL_SHARDED = tuple(q for q in SHARDED if q[0] not in BIG_NAMES)
PACK_COLS = 1024
F32_GATHERED = ("conv_w",)


def _local_shape(shape, axis):
    s = list(shape)
    s[axis] //= N_DEV
    return tuple(s)


def _numel(shape):
    return int(np.prod(shape))


def _pad_flat(z, mult):
    n = z.shape[-1]
    total = -(-n // mult) * mult
    return jnp.pad(z, [(0, 0)] * (z.ndim - 1) + [(0, total - n)])


def _full_from_slots(slots, shape, axis):
    loc = _local_shape(shape, axis)
    z = slots.reshape((N_DEV,) + loc)
    if axis == 0:
        return z.reshape(shape)
    return z.transpose(1, 0, 2).reshape(shape)


def _slots_from_full(full, shape, axis):
    loc = _local_shape(shape, axis)
    if axis == 0:
        return full.reshape(N_DEV, _numel(loc))
    return full.reshape(shape[0], N_DEV, loc[1]).transpose(1, 0, 2).reshape(N_DEV, _numel(loc))


W_IN_SLOT = 640
W_IN_LOCAL = 4128 // N_DEV


def _block_shape(shape, axis):
    return _local_shape(shape, axis) if axis is not None else shape


def _pad_w_in(block):
    return jnp.pad(block, ((0, 0), (0, W_IN_SLOT - W_IN_LOCAL)))


def _proj_col(s):
    return s + jnp.where(s >= 1600, 64, 0) + jnp.where(s >= 1664, 64, 0) + jnp.where(s >= 1824, 96, 0)


def _perm_tile(d, c0, width):
    j = lax.broadcasted_iota(jnp.int32, (W_IN_SLOT, width), 0)
    c = c0 + lax.broadcasted_iota(jnp.int32, (W_IN_SLOT, width), 1)
    hit = (_proj_col(d * W_IN_LOCAL + j) == c) & (j < W_IN_LOCAL)
    return jnp.where(hit, 1.0, 0.0).astype(BF16)


PERM_TILE = 768


def w_in_unshuffle(slots):
    _, K, _ = slots.shape
    tn = PERM_TILE
    reach = 3

    def first_slot(j):
        return j + jnp.where(j >= 3, 1, 0) + jnp.where(j >= 5, 1, 0)

    def body(a_ref, o_ref, acc_ref):
        j, kk = pl.program_id(0), pl.program_id(1)
        d = first_slot(j) + kk

        @pl.when(kk == 0)
        def _():
            acc_ref[...] = jnp.zeros_like(acc_ref)

        @pl.when(d < N_DEV)
        def _():
            acc_ref[...] += jnp.dot(a_ref[0], _perm_tile(d, j * tn, tn), preferred_element_type=F32)

        @pl.when(kk == reach - 1)
        def _():
            o_ref[...] = acc_ref[...].astype(o_ref.dtype)

    return pl.pallas_call(
        body, name="w_in_unshuffle", grid=(PROJ_PAD // tn, reach),
        in_specs=[pl.BlockSpec((1, K, W_IN_SLOT), lambda j, kk: (jnp.minimum(first_slot(j) + kk, N_DEV - 1), 0, 0))],
        out_specs=pl.BlockSpec((K, tn), lambda j, kk: (0, j)),
        out_shape=jax.ShapeDtypeStruct((K, PROJ_PAD), BF16),
        scratch_shapes=[pltpu.VMEM((K, tn), F32)],
        compiler_params=_cparams(2),
    )(slots)


def w_in_shuffle_grad(dw):
    K = dw.shape[0]
    tk = PERM_TILE

    def first_tile(d):
        return _proj_col(d * W_IN_LOCAL) // tk

    def body(g_ref, o_ref, acc_ref):
        d, kk = pl.program_id(0), pl.program_id(1)
        perm = _perm_tile(d, (first_tile(d) + kk) * tk, tk)
        part = lax.dot_general(g_ref[...].astype(BF16), perm, NT_DIMS, preferred_element_type=F32)

        @pl.when(kk == 0)
        def _():
            acc_ref[...] = part

        @pl.when(kk == 1)
        def _():
            o_ref[0] = (acc_ref[...] + part).astype(o_ref.dtype)

    return pl.pallas_call(
        body, name="w_in_shuffle_grad", grid=(N_DEV, 2),
        in_specs=[pl.BlockSpec((K, tk), lambda d, kk: (0, first_tile(d) + kk))],
        out_specs=pl.BlockSpec((1, K, W_IN_SLOT), lambda d, kk: (d, 0, 0)),
        out_shape=jax.ShapeDtypeStruct((N_DEV, K, W_IN_SLOT), GRAD_WIRE),
        scratch_shapes=[pltpu.VMEM((K, W_IN_SLOT), F32)],
        compiler_params=_cparams(2),
    )(dw)


def _flat_rows(pieces, dtype, row_mult):
    flat = jnp.concatenate([z.astype(dtype) for z in pieces], axis=-1)
    flat = _pad_flat(flat, row_mult * PACK_COLS)
    return flat.reshape(flat.shape[:-1] + (-1, PACK_COLS))


FIRST = tuple(q for q in BIG if q[0] in ("w_in", "w_gate"))
LATE = tuple(q for q in BIG if q not in FIRST)


def _matrix_from_slots(slots, shape, axis):
    return slots.reshape(shape) if axis == 0 else slots.transpose(1, 0, 2).reshape(shape)


def _late_weight_sources(W):
    return [(blk, WHOLE) for blk in W["_late_blocks"]]


def _late_weights(slots):
    return {n: _matrix_from_slots(s, shape, axis) for (n, shape, axis), s in zip(LATE, slots)}


def gather_weights(local):
    blocks = [(_pad_w_in(local[n]) if n == "w_in" else local[n]).astype(BF16) for n, _, _ in FIRST]
    small = [q for q in SMALL_SHARDED if q[0] not in F32_GATHERED]
    exact = [q for q in SMALL_SHARDED if q[0] in F32_GATHERED]
    blocks.append(_flat_rows([local[n].reshape(-1) for n, _, _ in small], BF16, 16))
    blocks.append(_flat_rows([local[n].reshape(-1) for n, _, _ in exact], F32, 8))
    got = all_gather_blocks("weight_all_gather", blocks)
    full = {"_late_blocks": [local[n].astype(BF16) for n, _, _ in LATE]}
    for (n, shape, axis), slots in zip(FIRST, got):
        if n == "w_in":
            full["w_in_p"] = w_in_unshuffle(slots)
        else:
            full[n] = _matrix_from_slots(slots, shape, axis)
    for group, slots in ((small, got[-2]), (exact, got[-1])):
        slots, off = slots.reshape(N_DEV, -1), 0
        for n, shape, axis in group:
            size = _numel(_local_shape(shape, axis))
            full[n] = _full_from_slots(slots[:, off:off + size], shape, axis)
            off += size
    for n, _, _ in REPLICATED:
        full[n] = local[n]
    return full


LOSS_SLOT = ("_loss", (1, 2), None)
PACKED_SMALL = SMALL_SHARDED + REPLICATED + (LOSS_SLOT,)


def _pack_small(vals):
    pieces = [vals[n].reshape(-1) if n in vals else jnp.zeros((_numel(shape),), F32) for n, shape, _ in PACKED_SMALL]
    return _flat_rows(pieces, F32, 16)


def _unpack_small(packed):
    flat, out, off = packed.reshape(-1), {}, 0
    for n, shape, axis in PACKED_SMALL:
        loc = _block_shape(shape, axis)
        out[n] = flat[off:off + _numel(loc)].reshape(loc)
        off += _numel(loc)
    return out


EARLY = tuple(q for q in BIG if q[0] != "w_in")


def _early_grad_sources(G):
    srcs = []
    for n, shape, axis in EARLY:
        if axis == 0:
            srcs.append((G[n].astype(GRAD_WIRE).reshape((N_DEV,) + _local_shape(shape, axis)), None))
        else:
            srcs.append((G[n].astype(GRAD_WIRE), shape[1] // N_DEV))
    return srcs


def _late_grad_sources(G, loss_local):
    srcs = [(w_in_shuffle_grad(G["w_in_p"]), None)]
    rows = [_slots_from_full(G[n].reshape(shape), shape, axis) for n, shape, axis in SMALL_SHARDED]
    loss_hi = loss_local.astype(GRAD_WIRE).astype(F32)
    rep = jnp.concatenate([G[n].reshape(-1) for n, _, _ in REPLICATED] + [jnp.stack([loss_hi, loss_local - loss_hi])])
    rows.append(jnp.broadcast_to(rep[None, :], (N_DEV, rep.shape[0])))
    srcs.append((_flat_rows(rows, GRAD_WIRE, 16), None))
    return srcs


def _step(x, p, target, local_w, local_m, local_v):
    full = gather_weights(local_w)
    loss_local, dx, G = local_step(x, p, target, full)
    late = all_to_all_blocks("grad_all_to_all", _late_grad_sources(G, loss_local))
    parts = [late[0]] + list(G["_early_parts"]) + [late[1]]
    outs = [{}, {}, {}, {}]
    for (n, shape, axis), part in zip((BIG[0],) + EARLY, parts):
        prep = _pad_w_in if n == "w_in" else (lambda z: z)
        res = reduce_adamw("adamw_" + n, part, prep(local_w[n]), prep(local_m[n]), prep(local_v[n]))
        for o, z in zip(outs, res):
            o[n] = z[:, :W_IN_LOCAL] if n == "w_in" else z
    res = reduce_adamw("adamw_small", parts[-1], _pack_small(local_w), _pack_small(local_m), _pack_small(local_v))
    for o, z in zip(outs, res):
        o.update(_unpack_small(z))
    loss = jnp.sum(outs[0]["_loss"])
    return loss, dx, outs


def kernel(x, p, g_mix, w_in, rw_mu, rw_w0, rw_w_up, rw_a0, rw_a_up, rw_g_up, rw_k_k, rw_k_a, rw_r_k, rw_ln_g, rw_ln_b, w_branch_a, w_branch_b, w_gate, b_gate, w_out, g_ffn, w_up, conv_w, conv_b, w_down, g_ple, w_ple_gate, w_ple, g_final, loss_target, m_g_mix, m_w_in, m_rw_mu, m_rw_w0, m_rw_w_up, m_rw_a0, m_rw_a_up, m_rw_g_up, m_rw_k_k, m_rw_k_a, m_rw_r_k, m_rw_ln_g, m_rw_ln_b, m_w_branch_a, m_w_branch_b, m_w_gate, m_b_gate, m_w_out, m_g_ffn, m_w_up, m_conv_w, m_conv_b, m_w_down, m_g_ple, m_w_ple_gate, m_w_ple, m_g_final, v_g_mix, v_w_in, v_rw_mu, v_rw_w0, v_rw_w_up, v_rw_a0, v_rw_a_up, v_rw_g_up, v_rw_k_k, v_rw_k_a, v_rw_r_k, v_rw_ln_g, v_rw_ln_b, v_w_branch_a, v_w_branch_b, v_w_gate, v_b_gate, v_w_out, v_g_ffn, v_w_up, v_conv_w, v_conv_b, v_w_down, v_g_ple, v_w_ple_gate, v_w_ple, v_g_final):
    args = dict(locals())
    names = [n for n, _, _ in PARAMS]
    orig_shape = {n: args[n].shape for n in names}

    def strip(prefix):
        out = {}
        for n, shape, axis in PARAMS:
            a = args[prefix + n]
            loc = _local_shape(shape, axis) if axis is not None else shape
            out[n] = a.reshape(loc)
        return out

    local_w, local_m, local_v = strip(""), strip("m_"), strip("v_")
    T, D = x.shape[-2], x.shape[-1]
    loss, dx, (g, delta, m_n, v_n) = _step(x.reshape(T, D), p.reshape(T, p.shape[-1]), loss_target.reshape(T, D),
                                           local_w, local_m, local_v)
    outs = [loss, dx.reshape(x.shape)]
    for group in (g, delta, m_n, v_n):
        outs += [group[n].reshape(orig_shape[n]) for n in names]
    return tuple(outs)
```

```python
import functools
import math

import numpy as np
import jax
import jax.numpy as jnp
from jax import lax
from jax.experimental import pallas as pl
from jax.experimental.pallas import tpu as pltpu

F32 = jnp.float32
BF16 = jnp.bfloat16
GRAD_WIRE = jnp.bfloat16

N_DEV = 8
NORM_EPS = 1e-6
RW_LN_EPS = 64e-5
HEAD = 64
RW_WIDTH = 512
ATT_GROUPS = ((128, 1), (512, 4), (2048, 16))
ATT_HEADS = 12
ATT_OUT = 256
ATT_COLS = 2304
OFF_XW, OFF_XA, OFF_XG, RW_PAD, PROJ_PAD = 1536, 1664, 1792, 2048, 4608
PROJ_TAIL = PROJ_PAD - RW_PAD - ATT_COLS
D_FF = 3072

ADAM_LR, ADAM_B1, ADAM_B2, ADAM_EPS, ADAM_WD, ADAM_STEP = 0.001, 0.9, 0.999, 1e-08, 0.01, 10

VMEM_LIMIT_BYTES = 56 * 1024 * 1024
ADAM_TILE_ELEMS = 256 * 1024
NEG_BIG = -1e30

NT_DIMS = (((1,), (1,)), ((), ()))
TN_DIMS = (((0,), (0,)), ((), ()))
NN_DIMS = (((1,), (0,)), ((), ()))


def _cparams(n_axes):
    return pltpu.CompilerParams(dimension_semantics=("arbitrary",) * n_axes,
                                vmem_limit_bytes=VMEM_LIMIT_BYTES)


def _split2(x):
    hi = x.astype(BF16)
    lo = (x - hi.astype(F32)).astype(BF16)
    return hi, lo


def _seg_mat(n):
    r = lax.shift_right_logical(lax.broadcasted_iota(jnp.int32, (n, n), 0), 6)
    c = lax.shift_right_logical(lax.broadcasted_iota(jnp.int32, (n, n), 1), 6)
    return jnp.where(r == c, 1.0, 0.0).astype(BF16)


def _segb(x, seg):
    hi, lo = _split2(x)
    return (jnp.dot(hi, seg, preferred_element_type=F32)
            + jnp.dot(lo, seg, preferred_element_type=F32))


def _segb1(x, seg):
    return jnp.dot(x.astype(BF16), seg, preferred_element_type=F32)


@jax.custom_vjp
def segsum(x):
    return _segb(x, _seg_mat(x.shape[1]))


def _segsum_fwd(x):
    return segsum(x), None


def _segsum_bwd(_, ct):
    return (segsum(ct),)


segsum.defvjp(_segsum_fwd, _segsum_bwd)


@jax.custom_vjp
def bdot(a, b):
    return jnp.dot(a.astype(BF16), b.astype(BF16), preferred_element_type=F32)


def _bdot_fwd(a, b):
    return bdot(a, b), (a, b)


def _bdot_bwd(res, ct):
    a, b = res
    ctb = ct.astype(BF16)
    da = lax.dot_general(ctb, b.astype(BF16), NT_DIMS, preferred_element_type=F32)
    db = lax.dot_general(a.astype(BF16), ctb, TN_DIMS, preferred_element_type=F32)
    return da.astype(a.dtype), db.astype(b.dtype)


bdot.defvjp(_bdot_fwd, _bdot_bwd)


def _sig(x):
    return 1.0 / (1.0 + jnp.exp(-x))


def _softplus(z):
    return jnp.maximum(z, 0.0) + jnp.log(1.0 + jnp.exp(-jnp.abs(z)))


def _gelu_tanh(x):
    return 0.5 * x * (1.0 + jnp.tanh(0.7978845608028654 * (x + 0.044715 * (x * x * x))))


def _rms(x, g):
    return x * lax.rsqrt(jnp.mean(x * x, axis=-1, keepdims=True) + NORM_EPS) * g


def _shift_down(x, prev8, n):
    rolled = pltpu.roll(x, n, 0)
    top = pltpu.roll(prev8, n, 0)
    rid = lax.broadcasted_iota(jnp.int32, (8, x.shape[1]), 0)
    head = jnp.where(rid < n, top, rolled[:8])
    return jnp.concatenate([head, rolled[8:]], axis=0)


def _shift_up(x, next8, n):
    rows = x.shape[0]
    rolled = pltpu.roll(x, rows - n, 0)
    bottom = pltpu.roll(next8, 8 - n, 0)
    rid = lax.broadcasted_iota(jnp.int32, (8, x.shape[1]), 0)
    tail = jnp.where(rid >= 8 - n, bottom, rolled[rows - 8:])
    return jnp.concatenate([rolled[:rows - 8], tail], axis=0)


def tile_call(name, fn, grid, ins, outs, scratch=()):
    n_in, n_out = len(ins), len(outs)
    acc_axes = [o[4] for o in outs]

    def body(*refs):
        pids = tuple(pl.program_id(a) for a in range(len(grid)))
        vals = fn(pids, *[r[...] for r in refs[:n_in]], *refs[n_in + n_out:])
        if not isinstance(vals, (tuple, list)):
            vals = (vals,)
        for o_ref, val, ax in zip(refs[n_in:n_in + n_out], vals, acc_axes):
            if ax is None:
                o_ref[...] = val.astype(o_ref.dtype)
            else:
                @pl.when(pids[ax] == 0)
                def _(o_ref=o_ref):
                    o_ref[...] = jnp.zeros_like(o_ref)

                o_ref[...] += val.astype(o_ref.dtype)

    res = pl.pallas_call(
        body, name=name, grid=grid,
        in_specs=[pl.BlockSpec(b, im) for _, b, im in ins],
        out_specs=[pl.BlockSpec(o[2], o[3]) for o in outs],
        out_shape=[jax.ShapeDtypeStruct(o[0], o[1]) for o in outs],
        scratch_shapes=[pltpu.VMEM(s, d) for s, d in scratch],
        compiler_params=_cparams(len(grid)),
    )(*[a for a, _, _ in ins])
    return res


def _rows(a, tm):
    return (a, (tm, a.shape[1]), lambda i: (i, 0))


def _par(a):
    return (a, a.shape, lambda i: (0, 0))


def _row_out(T, C, dtype, tm):
    return ((T, C), dtype, (tm, C), lambda i: (i, 0), None)


def _acc_out(R, C):
    return ((R, C), F32, (R, C), lambda i: (0, 0), 0)


def _prev_halo(a, tm, C):
    return (a, (8, C), lambda i: (jnp.maximum(i * (tm // 8) - 1, 0), 0))


def _next_halo(a, tm, C, T):
    return (a, (8, C), lambda i: (jnp.minimum((i + 1) * (tm // 8), T // 8 - 1), 0))


def _pick(n, target):
    for t in (target, 1024, 768, 512, 384, 256, 128):
        if t <= target and n % t == 0:
            return t
    return n


def matmul(name, a, b, mode="nn", res=None, out_dtype=F32, tm=1024, tn=1024, tk=1024):
    if mode == "nn":
        (M, K), (K2, N) = a.shape, b.shape
    elif mode == "tn":
        (K, M), (K2, N) = a.shape, b.shape
    else:
        (M, K), (N, K2) = a.shape, b.shape
    assert K == K2, (name, a.shape, b.shape, mode)
    tm, tn, tk = _pick(M, tm), _pick(N, tn), _pick(K, tk)
    nk = K // tk
    dims = {"nn": NN_DIMS, "tn": TN_DIMS, "nt": NT_DIMS}[mode]
    a_spec = {"nn": pl.BlockSpec((tm, tk), lambda i, j, k: (i, k)),
              "tn": pl.BlockSpec((tk, tm), lambda i, j, k: (k, i)),
              "nt": pl.BlockSpec((tm, tk), lambda i, j, k: (i, k))}[mode]
    b_spec = {"nn": pl.BlockSpec((tk, tn), lambda i, j, k: (k, j)),
              "tn": pl.BlockSpec((tk, tn), lambda i, j, k: (k, j)),
              "nt": pl.BlockSpec((tn, tk), lambda i, j, k: (j, k))}[mode]
    has_res = res is not None

    def body(*refs):
        if has_res:
            a_ref, b_ref, r_ref, o_ref, acc_ref = refs
        else:
            a_ref, b_ref, o_ref, acc_ref = refs
        k = pl.program_id(2)

        @pl.when(k == 0)
        def _():
            acc_ref[...] = jnp.zeros_like(acc_ref)

        acc_ref[...] += lax.dot_general(a_ref[...].astype(BF16), b_ref[...].astype(BF16), dims,
                                        preferred_element_type=F32)

        @pl.when(k == nk - 1)
        def _():
            out = acc_ref[...]
            if has_res:
                out = out + r_ref[...].astype(F32)
            o_ref[...] = out.astype(o_ref.dtype)

    in_specs = [a_spec, b_spec]
    args = [a, b]
    if has_res:
        in_specs.append(pl.BlockSpec((tm, tn), lambda i, j, k: (i, j)))
        args.append(res)
    return pl.pallas_call(
        body, name=name, grid=(M // tm, N // tn, nk),
        in_specs=in_specs,
        out_specs=pl.BlockSpec((tm, tn), lambda i, j, k: (i, j)),
        out_shape=jax.ShapeDtypeStruct((M, N), out_dtype),
        scratch_shapes=[pltpu.VMEM((tm, tn), F32)],
        compiler_params=_cparams(3),
    )(*args)


def rw_pre(Pc, Ps, mu, w0, w_up, a0, a_up, g_up, k_k, k_a):
    Pm = Pc + (Ps - Pc) * mu
    r, k, v = Pm[:, 0:512], Pm[:, 512:1024], Pm[:, 1024:1536]
    xw, xa, xg = Pm[:, OFF_XW:OFF_XA], Pm[:, OFF_XA:OFF_XG], Pm[:, OFF_XG:RW_PAD]
    w = -_softplus(-(w0 + bdot(jnp.tanh(xw), w_up))) - 0.5
    decay = jnp.exp(-jnp.exp(w))
    a = _sig(a0 + bdot(xa, a_up))
    g = bdot(_sig(xg), g_up)
    kk = k * k_k
    kk = kk / jnp.maximum(jnp.sqrt(segsum(kk * kk)), 1e-12)
    k2 = k * (1.0 + (a - 1.0) * k_a)
    return r, decay, k2, v, -kk, kk * a, g


def rw_post(y, r, k2, v, g, ln_g, ln_b, r_k):
    mean = segsum(y) * (1.0 / HEAD)
    d = y - mean
    var = segsum(d * d) * (1.0 / HEAD)
    yn = d * lax.rsqrt(var + RW_LN_EPS) * ln_g + ln_b
    bonus = segsum(r * k2 * r_k) * v
    return (yn + bonus) * g


def att_combine(o1, o2, o3, l1, l2, l3):
    m = jnp.maximum(jnp.maximum(l1, l2), l3)
    e1, e2, e3 = jnp.exp(l1 - m), jnp.exp(l2 - m), jnp.exp(l3 - m)
    return (e1 * o1 + e2 * o2 + e3 * o3) / (e1 + e2 + e3)


def merge_fn(gp, bg, za, zb):
    s = _sig(gp + bg)
    half = za.shape[1]
    return s[:, :half] * za + s[:, half:] * zb


def tail_loss(x2, zg, pe, g_final, target):
    x3 = x2 + _sig(zg) * pe
    y = _rms(x3, g_final)
    err = (y - target) * (y - target)
    return 0.5 * jnp.sum(jnp.mean(err, axis=-1, keepdims=True))


SCAN_CHUNK = HEAD
SCAN_LANES = 256
SCAN_UNROLL_FWD, SCAN_UNROLL_BWD = 8, 8


def _to_head_time(z):
    T = z.shape[0]
    return z.reshape(T // HEAD, HEAD, RW_WIDTH // HEAD, HEAD).transpose(0, 3, 2, 1).reshape(T // HEAD, HEAD, RW_WIDTH)


def _from_head_time(zt):
    C = zt.shape[0]
    return zt.reshape(C, HEAD, RW_WIDTH // HEAD, HEAD).transpose(0, 3, 2, 1).reshape(C * HEAD, RW_WIDTH)


def _unrolled_loop(n, step, init, unroll):
    def body(i, carry):
        for j in range(unroll):
            carry = step(i * unroll + j, carry)
        return carry

    return lax.fori_loop(0, n // unroll, body, init)


def _lane_groups():
    return [slice(j * SCAN_LANES, (j + 1) * SCAN_LANES) for j in range(RW_WIDTH // SCAN_LANES)]


def scan_pair_terms(a, w, b, k, tm=256):
    T = a.shape[0]

    def fn(pid, a_t, nxt, w_t, b_t, k_t):
        a_next = _shift_up(a_t, jnp.where(pid[0] < T // tm - 1, nxt, 0.0), 1)
        return w_t * a_next, segsum(b_t * a_next), segsum(k_t * a_next)

    return tile_call("scan_pair_terms", fn, (T // tm,),
                     [_rows(a, tm), _next_halo(a, tm, RW_WIDTH, T), _rows(w, tm), _rows(b, tm), _rows(k, tm)],
                     [_row_out(T, RW_WIDTH, F32, tm)] * 3)


def rwkv_scan_fwd(a, w, b, k, r, vT, wa, ba, ka, exchange=()):
    T = a.shape[0]
    C, LW = SCAN_CHUNK, SCAN_LANES
    nC = T // C
    nx = len(exchange)

    def body(*refs):
        a_ref, w_ref, b_ref, k_ref, r_ref, vT_ref, wa_ref, ba_ref, ka_ref = refs[:9]
        x_refs, refs = refs[9:9 + nx], refs[9 + nx:]
        yT_ref, S_ref, saT_ref = refs[:3]
        land_refs, refs = refs[3:3 + nx], refs[3 + nx:]
        st_ref, vb0_ref, vb1_ref, seg_ref = refs[:4]
        if nx:
            start, wait = _exchange_ops([c for _, c in exchange], x_refs, land_refs, *refs[4:])

        @pl.when(pl.program_id(0) == 0)
        def _():
            st_ref[...] = jnp.zeros_like(st_ref)
            seg_ref[...] = _seg_mat(LW)
            if nx:
                start()

        seg = seg_ref[...]
        lane = jnp.bitwise_and(lax.broadcasted_iota(jnp.int32, (1, LW), 1), HEAD - 1)
        groups = _lane_groups()

        def vcol(t, gsl):
            return _segb1(jnp.where(lane == t, vT_ref[0, :, gsl], 0.0), seg)

        for gsl in groups:
            vb0_ref[:, gsl] = vcol(0, gsl)
            vb1_ref[:, gsl] = vcol(1, gsl)
        saT_ref[...] = jnp.zeros_like(saT_ref)

        def pair(i, yacc):
            t = 2 * i
            t1 = t + 1
            tp = jnp.maximum(t - 1, 0)
            row = lambda ref, s, gsl: ref[pl.ds(s, 1), gsl]
            Sps = [st_ref[:, gsl] for gsl in groups]
            sas = [_segb(Sp * row(a_ref, t, gsl), seg) for gsl, Sp in zip(groups, Sps)]
            us = [_segb(Sp * row(wa_ref, t, gsl), seg) for gsl, Sp in zip(groups, Sps)]
            S1s = []
            for gsl, Sp, sa, u in zip(groups, Sps, sas, us):
                vb0, vb1 = vb0_ref[:, gsl], vb1_ref[:, gsl]
                S1 = Sp * row(w_ref, t, gsl) + sa * row(b_ref, t, gsl) + vb0 * row(k_ref, t, gsl)
                sa1 = u + sa * row(ba_ref, t, gsl) + vb0 * row(ka_ref, t, gsl)
                st_ref[:, gsl] = S1 * row(w_ref, t1, gsl) + sa1 * row(b_ref, t1, gsl) + vb1 * row(k_ref, t1, gsl)
                S_ref[0, t, :, gsl] = Sp
                S_ref[0, t1, :, gsl] = S1
                S1s.append(S1)
                saT_ref[0, :, gsl] = jnp.where(lane == t, sa, jnp.where(lane == t1, sa1, saT_ref[0, :, gsl]))
            out = []
            for gsl, Sp, S1, ya in zip(groups, Sps, S1s, yacc):
                yb0 = _segb1(Sp * row(r_ref, tp, gsl), seg)
                yb1 = _segb1(S1 * row(r_ref, t, gsl), seg)
                out.append(jnp.where(lane == t, yb1, jnp.where(lane == t - 1, yb0, ya)))
                vb0_ref[:, gsl] = vcol(t + 2, gsl)
                vb1_ref[:, gsl] = vcol(t + 3, gsl)
            return tuple(out)

        yacc = _unrolled_loop(C // 2, pair, tuple(jnp.zeros((HEAD, LW), F32) for _ in groups), SCAN_UNROLL_FWD)
        for gsl, ya in zip(groups, yacc):
            S_last = st_ref[:, gsl]
            S_ref[0, C, :, gsl] = S_last
            yb = _segb1(S_last * r_ref[pl.ds(C - 1, 1), gsl], seg)
            yT_ref[0, :, gsl] = jnp.where(lane == C - 1, yb, ya)

        if nx:
            @pl.when(pl.program_id(0) == nC - 1)
            def _():
                wait()

    row = pl.BlockSpec((C, RW_WIDTH), lambda c: (c, 0))
    ht = pl.BlockSpec((1, HEAD, RW_WIDTH), lambda c: (c, 0, 0))
    hbm = pl.BlockSpec(memory_space=pl.ANY)
    res = pl.pallas_call(
        body, name="rwkv_scan_fwd", grid=(nC,),
        in_specs=[row, row, row, row, row, ht, row, row, row] + [hbm] * nx,
        out_specs=[ht, pl.BlockSpec((1, C + 1, HEAD, RW_WIDTH), lambda c: (c, 0, 0, 0)), ht] + [hbm] * nx,
        out_shape=[jax.ShapeDtypeStruct((nC, HEAD, RW_WIDTH), F32),
                   jax.ShapeDtypeStruct((nC, C + 1, HEAD, RW_WIDTH), F32),
                   jax.ShapeDtypeStruct((nC, HEAD, RW_WIDTH), F32)] + _exchange_shapes(exchange),
        scratch_shapes=[pltpu.VMEM((HEAD, RW_WIDTH), F32)] * 3 + [pltpu.VMEM((LW, LW), BF16)]
        + (_exchange_sems(nx) if nx else []),
        compiler_params=pltpu.CompilerParams(dimension_semantics=("arbitrary",), vmem_limit_bytes=VMEM_LIMIT_BYTES,
                                             has_side_effects=bool(nx)),
    )(a, w, b, k, r, vT, wa, ba, ka, *[z for z, _ in exchange])
    return res[:3], res[3:]


def rwkv_scan_bwd(a, w, b, k, r, v, dy, S_all, saT, exchange=()):
    T = a.shape[0]
    C, LW = SCAN_CHUNK, SCAN_LANES
    nC = T // C
    nx = len(exchange)
    n_heads = RW_WIDTH // HEAD
    dyT = _to_head_time(dy).astype(BF16)
    v_rows, dy_rows = v.reshape(T, n_heads, HEAD), dy.reshape(T, n_heads, HEAD)
    sa_rows = _from_head_time(saT).reshape(T, n_heads, HEAD)

    def body(*refs):
        a_ref, w_ref, b_ref, k_ref, r_ref, vR_ref, saR_ref, dyR_ref, dyT_ref, S_ref = refs[:10]
        x_refs, refs = refs[10:10 + nx], refs[10 + nx:]
        da_ref, dw_ref, db_ref, dk_ref, dr_ref, dvT_ref = refs[:6]
        land_refs, refs = refs[6:6 + nx], refs[6 + nx:]
        ds_ref, dyb_ref, seg_ref = refs[:3]
        if nx:
            start, wait = _exchange_ops([c for _, c in exchange], x_refs, land_refs, *refs[3:])

        @pl.when(pl.program_id(0) == 0)
        def _():
            ds_ref[...] = jnp.zeros_like(ds_ref)
            seg_ref[...] = _seg_mat(LW)
            if nx:
                start()

        seg = seg_ref[...]
        lane = jnp.bitwise_and(lax.broadcasted_iota(jnp.int32, (1, LW), 1), HEAD - 1)
        groups = _lane_groups()
        head_row = lax.broadcasted_iota(jnp.int32, (n_heads, LW), 0)
        lane_head = lax.shift_right_logical(lax.broadcasted_iota(jnp.int32, (n_heads, LW), 1), 6)

        def colsum(z):
            return jnp.sum(z, axis=0, keepdims=True)

        for gsl in groups:
            dyb_ref[:, gsl] = _segb1(jnp.where(lane == C - 1, dyT_ref[0, :, gsl], 0.0), seg)

        def step(i, dvacc):
            t = C - 1 - i
            dybs = [dyb_ref[:, gsl] for gsl in groups]
            dSs = [ds_ref[:, gsl] + dyb * r_ref[pl.ds(t, 1), gsl] for gsl, dyb in zip(groups, dybs)]
            dsabs = [_segb(dS * b_ref[pl.ds(t, 1), gsl], seg) for gsl, dS in zip(groups, dSs)]
            for gsl, dS, dsab in zip(groups, dSs, dsabs):
                ds_ref[:, gsl] = dS * w_ref[pl.ds(t, 1), gsl] + dsab * a_ref[pl.ds(t, 1), gsl]
            out = []
            dy_rows = dyR_ref[t].astype(BF16)
            v_sa_rows = jnp.concatenate([vR_ref[t], saR_ref[t]], axis=0).astype(BF16)
            for g, (gsl, dva, dS, dsab) in enumerate(zip(groups, dvacc, dSs, dsabs)):
                kr = k_ref[pl.ds(t, 1), gsl]
                Sp = S_ref[0, t, :, gsl]
                own = head_row == lane_head + g * (LW // HEAD)

                def rows_in(rows, mat):
                    full = jnp.dot(rows, mat.astype(BF16), preferred_element_type=F32)
                    return [jnp.sum(jnp.where(own, full[s:s + n_heads], 0.0), axis=0, keepdims=True)
                            for s in range(0, rows.shape[0], n_heads)]

                (dr,) = rows_in(dy_rows, S_ref[0, t + 1, :, gsl])
                dk, db = rows_in(v_sa_rows, dS)
                dr_ref[pl.ds(t, 1), gsl] = dr
                dk_ref[pl.ds(t, 1), gsl] = dk
                db_ref[pl.ds(t, 1), gsl] = db
                dvb = _segb1(dS * kr, seg)
                dw_ref[pl.ds(t, 1), gsl] = colsum(dS * Sp)
                da_ref[pl.ds(t, 1), gsl] = colsum(Sp * dsab)
                dyb_ref[:, gsl] = _segb1(jnp.where(lane == t - 1, dyT_ref[0, :, gsl], 0.0), seg)
                out.append(jnp.where(lane == t, dvb, dva))
            return tuple(out)

        dvacc = _unrolled_loop(C, step, tuple(jnp.zeros((HEAD, LW), F32) for _ in groups), SCAN_UNROLL_BWD)
        for gsl, dva in zip(groups, dvacc):
            dvT_ref[0, :, gsl] = dva

        if nx:
            @pl.when(pl.program_id(0) == nC - 1)
            def _():
                wait()

    row = pl.BlockSpec((C, RW_WIDTH), lambda c: (nC - 1 - c, 0))
    ht = pl.BlockSpec((1, HEAD, RW_WIDTH), lambda c: (nC - 1 - c, 0, 0))
    hbm = pl.BlockSpec(memory_space=pl.ANY)
    per_head = pl.BlockSpec((C, n_heads, HEAD), lambda c: (nC - 1 - c, 0, 0))
    rows_shape = jax.ShapeDtypeStruct((T, RW_WIDTH), F32)
    res = pl.pallas_call(
        body, name="rwkv_scan_bwd", grid=(nC,),
        in_specs=[row, row, row, row, row, per_head, per_head, per_head, ht,
                  pl.BlockSpec((1, C + 1, HEAD, RW_WIDTH), lambda c: (nC - 1 - c, 0, 0, 0))] + [hbm] * nx,
        out_specs=[row, row, row, row, row, ht] + [hbm] * nx,
        out_shape=[rows_shape] * 5 + [jax.ShapeDtypeStruct((nC, HEAD, RW_WIDTH), F32)] + _exchange_shapes(exchange),
        scratch_shapes=[pltpu.VMEM((HEAD, RW_WIDTH), F32), pltpu.VMEM((HEAD, RW_WIDTH), F32),
                        pltpu.VMEM((LW, LW), BF16)] + (_exchange_sems(nx) if nx else []),
        compiler_params=pltpu.CompilerParams(dimension_semantics=("arbitrary",), vmem_limit_bytes=VMEM_LIMIT_BYTES,
                                             has_side_effects=bool(nx)),
    )(a, w, b, k, r, v_rows, sa_rows, dy_rows, dyT, S_all, *[z for z, _ in exchange])
    return res[:6], res[6:]


def _alibi_slope(h):
    return float(np.float32(2.0 ** (-8.0 * (h + 1) / ATT_HEADS)))


ATT_GROUP_HEADS = 4


def _stack_heads(x, lane_head, fill=0.0):
    return jnp.concatenate([jnp.where(lane_head == hh, x, fill) for hh in range(ATT_GROUP_HEADS)], axis=0)


def _unstack_heads(x, lane_head, L):
    out = jnp.zeros((L, x.shape[1]), F32)
    for hh in range(ATT_GROUP_HEADS):
        out = jnp.where(lane_head == hh, x[hh * L:(hh + 1) * L], out)
    return out


def _att_logits(qs, kcat, gi, d, L, n):
    qi = lax.broadcasted_iota(jnp.int32, (L, 2 * L), 0)
    kj = lax.broadcasted_iota(jnp.int32, (L, 2 * L), 1)
    steps = qi + L - kj
    valid = (steps >= 0) & (steps <= L) & ((kj >= L) | (n > 0))
    dist = (d * steps).astype(F32)
    bias = jnp.concatenate([jnp.where(valid, -_alibi_slope(gi * ATT_GROUP_HEADS + hh) * dist, NEG_BIG)
                            for hh in range(ATT_GROUP_HEADS)], axis=0)
    s = lax.dot_general(qs.astype(BF16), kcat, NT_DIMS, preferred_element_type=F32) * (HEAD ** -0.5)
    return jnp.where(bias > 0.5 * NEG_BIG, s + bias, NEG_BIG)


def att_fwd(pa, gi, T):
    window, d = ATT_GROUPS[gi]
    L = window // d
    Tj = T // d
    nb = Tj // L
    pv = pa.reshape(Tj, d * ATT_COLS)
    nblk = ATT_COLS // ATT_OUT

    def fn(pids, q, kp, kc, vp, vc):
        lane_head = lax.shift_right_logical(lax.broadcasted_iota(jnp.int32, (1, ATT_OUT), 1), 6)
        kcat = jnp.concatenate([kp, kc], axis=0).astype(BF16)
        vcat = jnp.concatenate([vp, vc], axis=0).astype(BF16)
        s = _att_logits(_stack_heads(q, lane_head), kcat, gi, d, L, pids[1])
        m = jnp.max(s, axis=-1, keepdims=True)
        p = jnp.exp(s - m)
        l = jnp.sum(p, axis=-1, keepdims=True)
        o = jnp.dot(p.astype(BF16), vcat, preferred_element_type=F32) / l
        lse = jnp.broadcast_to(m + jnp.log(l), o.shape)
        return _unstack_heads(o, lane_head, L), _unstack_heads(lse, lane_head, L)

    blk = (L, ATT_OUT)
    ins = [(pv, blk, lambda r, n: (n, r * nblk + gi)),
           (pv, blk, lambda r, n: (jnp.maximum(n - 1, 0), r * nblk + 3 + gi)),
           (pv, blk, lambda r, n: (n, r * nblk + 3 + gi)),
           (pv, blk, lambda r, n: (jnp.maximum(n - 1, 0), r * nblk + 6 + gi)),
           (pv, blk, lambda r, n: (n, r * nblk + 6 + gi))]
    out = ((Tj, d * ATT_OUT), F32, blk, lambda r, n: (n, r), None)
    o, lseb = tile_call(f"att_fwd_g{gi}", fn, (d, nb), ins, [out, out])
    return o.reshape(T, ATT_OUT), lseb.reshape(T, ATT_OUT)


def att_bwd(pa, o, lseb, do, dlseb, gi, T):
    window, d = ATT_GROUPS[gi]
    L = window // d
    Tj = T // d
    nb = Tj // L
    pv = pa.reshape(Tj, d * ATT_COLS)
    nblk = ATT_COLS // ATT_OUT
    view = lambda z: z.reshape(Tj, d * ATT_OUT)

    def body(q_ref, kp_ref, kc_ref, vp_ref, vc_ref, o_ref, l_ref, do_ref, dl_ref, dq_ref, dk_ref, dv_ref):
        n = pl.program_id(1)

        @pl.when(n == 0)
        def _():
            dk_ref[...] = jnp.zeros_like(dk_ref)
            dv_ref[...] = jnp.zeros_like(dv_ref)

        lane_head = lax.shift_right_logical(lax.broadcasted_iota(jnp.int32, (1, ATT_OUT), 1), 6)
        kcat = jnp.concatenate([kp_ref[...], kc_ref[...]], axis=0).astype(BF16)
        vcat = jnp.concatenate([vp_ref[...], vc_ref[...]], axis=0).astype(BF16)
        qs = _stack_heads(q_ref[...], lane_head)
        dos = _stack_heads(do_ref[...], lane_head)
        lse = jnp.max(_stack_heads(l_ref[...], lane_head, NEG_BIG), axis=-1, keepdims=True)
        dlse = jnp.sum(_stack_heads(dl_ref[...], lane_head), axis=-1, keepdims=True)
        delta = jnp.sum(dos * jnp.concatenate([o_ref[...]] * ATT_GROUP_HEADS, axis=0), axis=-1, keepdims=True)
        p = jnp.exp(_att_logits(qs, kcat, gi, d, L, n) - lse)
        dp = lax.dot_general(dos.astype(BF16), vcat, NT_DIMS, preferred_element_type=F32)
        ds = (p * (dp - delta + dlse)).astype(BF16)
        dq = _unstack_heads(jnp.dot(ds, kcat, preferred_element_type=F32), lane_head, L)
        dkc = lax.dot_general(ds, qs.astype(BF16), TN_DIMS, preferred_element_type=F32)
        dvc = lax.dot_general(p.astype(BF16), dos.astype(BF16), TN_DIMS, preferred_element_type=F32)
        scale = HEAD ** -0.5
        dq_ref[...] = dq * scale
        cur = pl.ds(pl.multiple_of(n * L, L), L)
        dk_ref[cur, :] += dkc[L:] * scale
        dv_ref[cur, :] += dvc[L:]

        @pl.when(n > 0)
        def _():
            prev = pl.ds(pl.multiple_of((n - 1) * L, L), L)
            dk_ref[prev, :] += dkc[:L] * scale
            dv_ref[prev, :] += dvc[:L]

    blk = pl.BlockSpec((L, ATT_OUT), lambda r, n: (n, r))
    res = pl.BlockSpec((Tj, ATT_OUT), lambda r, n: (0, r))
    qspec = lambda off, prev: pl.BlockSpec(
        (L, ATT_OUT), (lambda r, n: (jnp.maximum(n - 1, 0), r * nblk + off + gi)) if prev
        else (lambda r, n: (n, r * nblk + off + gi)))
    shape = jax.ShapeDtypeStruct((Tj, d * ATT_OUT), F32)
    dq, dk, dv = pl.pallas_call(
        body, name=f"att_bwd_g{gi}", grid=(d, nb),
        in_specs=[qspec(0, False), qspec(3, True), qspec(3, False), qspec(6, True), qspec(6, False),
                  blk, blk, blk, blk],
        out_specs=[blk, res, res],
        out_shape=[shape, shape, shape],
        compiler_params=_cparams(2),
    )(pv, pv, pv, pv, pv, view(o), view(lseb), view(do), view(dlseb))
    return dq.reshape(T, ATT_OUT), dk.reshape(T, ATT_OUT), dv.reshape(T, ATT_OUT)


FFN_TM, FFN_TC = 512, 512


def _conv3(u, prev8, cw, cb):
    return cb + cw[0:1] * u + cw[1:2] * _shift_down(u, prev8, 1) + cw[2:3] * _shift_down(u, prev8, 2)


def conv_glu_fwd(u, conv_w, conv_b):
    T = u.shape[0]
    tm, tc = FFN_TM, FFN_TC
    nj, ni = D_FF // tc, T // tm

    def fn(pids, ug, ugh, uv, uvh, cwg, cbg, cwv, cbv):
        first = pids[1] > 0
        cg = _conv3(ug, jnp.where(first, ugh, 0.0), cwg, cbg)
        cv = _conv3(uv, jnp.where(first, uvh, 0.0), cwv, cbv)
        return _gelu_tanh(cg) * cv

    halo = lambda off: (lambda j, i: (jnp.maximum(i * (tm // 8) - 1, 0), j + off))
    ins = [(u, (tm, tc), lambda j, i: (i, j)), (u, (8, tc), halo(0)),
           (u, (tm, tc), lambda j, i: (i, j + nj)), (u, (8, tc), halo(nj)),
           (conv_w, (3, tc), lambda j, i: (0, j)), (conv_b, (1, tc), lambda j, i: (0, j)),
           (conv_w, (3, tc), lambda j, i: (0, j + nj)), (conv_b, (1, tc), lambda j, i: (0, j + nj))]
    out = ((T, D_FF), BF16, (tm, tc), lambda j, i: (i, j), None)
    return tile_call("conv_glu_fwd", fn, (nj, ni), ins, [out])[0]


def conv_glu_bwd(u, conv_w, conv_b, df):
    T = u.shape[0]
    tm, tc = FFN_TM, FFN_TC
    nj, ni = D_FF // tc, T // tm

    def fn(pids, ug, ugh, uv, uvh, cwg, cbg, cwv, cbv, df_t, nxt_g, nxt_v):
        i = ni - 1 - pids[1]
        ugh = jnp.where(i > 0, ugh, 0.0)
        uvh = jnp.where(i > 0, uvh, 0.0)
        cg = _conv3(ug, ugh, cwg, cbg)
        cv = _conv3(uv, uvh, cwv, cbv)
        _, vjp = jax.vjp(lambda g_, v_: _gelu_tanh(g_) * v_, cg, cv)
        dcg, dcv = vjp(df_t.astype(F32))
        cs = lambda z: jnp.sum(z, axis=0, keepdims=True)

        @pl.when(pids[1] == 0)
        def _():
            nxt_g[...] = jnp.zeros_like(nxt_g)
            nxt_v[...] = jnp.zeros_like(nxt_v)

        outs = []
        for dc, cw, nxt_ref in ((dcg, cwg, nxt_g), (dcv, cwv, nxt_v)):
            nxt = nxt_ref[...]
            outs.append(cw[0:1] * dc + cw[1:2] * _shift_up(dc, nxt, 1) + cw[2:3] * _shift_up(dc, nxt, 2))
            nxt_ref[...] = dc[:8]
        for dc, uu, hh in ((dcg, ug, ugh), (dcv, uv, uvh)):
            outs += [cs(dc * uu), cs(dc * _shift_down(uu, hh, 1)), cs(dc * _shift_down(uu, hh, 2)), cs(dc)]
        return outs

    rows = lambda off: (lambda j, r: (ni - 1 - r, j + off))
    halo = lambda off: (lambda j, r: (jnp.maximum((ni - 1 - r) * (tm // 8) - 1, 0), j + off))
    ins = [(u, (tm, tc), rows(0)), (u, (8, tc), halo(0)),
           (u, (tm, tc), rows(nj)), (u, (8, tc), halo(nj)),
           (conv_w, (3, tc), lambda j, r: (0, j)), (conv_b, (1, tc), lambda j, r: (0, j)),
           (conv_w, (3, tc), lambda j, r: (0, j + nj)), (conv_b, (1, tc), lambda j, r: (0, j + nj)),
           (df, (tm, tc), rows(0))]
    big = ((T, D_FF), BF16, (tm, tc), rows(0), None)
    acc = ((1, D_FF), F32, (1, tc), lambda j, r: (0, j), 1)
    res = tile_call("conv_glu_bwd", fn, (nj, ni), ins, [big, big] + [acc] * 8,
                    scratch=[((8, tc), F32), ((8, tc), F32)])
    dconv_w = jnp.concatenate([jnp.concatenate([res[2 + j], res[6 + j]], axis=1) for j in range(3)], axis=0)
    dconv_b = jnp.concatenate([res[5], res[9]], axis=1)
    return res[0], res[1], dconv_w, dconv_b


def _pad_cols(w, total):
    return jnp.pad(w, ((0, 0), (0, total - w.shape[1])))


def _pad_rows(w, total):
    return jnp.pad(w, ((0, total - w.shape[0]), (0, 0)))


def _proj_pad(w):
    z = lambda n: jnp.zeros((w.shape[0], n), w.dtype)
    return jnp.concatenate([w[:, :1600], z(64), w[:, 1600:1664], z(64), w[:, 1664:1824], z(96), w[:, 1824:],
                            z(PROJ_TAIL)], axis=1)


def _proj_unpad(g):
    return jnp.concatenate([g[:, :1600], g[:, OFF_XA:OFF_XA + 64], g[:, OFF_XG:OFF_XG + 160],
                            g[:, RW_PAD:RW_PAD + ATT_COLS]], axis=1)


def _rw_unpad(g):
    return jnp.concatenate([g[:, :1600], g[:, OFF_XA:OFF_XA + 64], g[:, OFF_XG:OFF_XG + 160]], axis=1)


def rms_fwd(name, x, g, tm=256):
    T, D = x.shape
    return tile_call(name, lambda pid, x_t, g_t: _rms(x_t, g_t), (T // tm,),
                     [_rows(x, tm), _par(g)], [_row_out(T, D, BF16, tm)])[0]


def rms_bwd(name, x, g, dh, dres, with_bf16=True, tm=256):
    T, D = x.shape
    out_dtypes = (F32, BF16) if with_bf16 else (F32,)

    def fn(pid, x_t, g_t, dh_t, dres_t):
        _, vjp = jax.vjp(_rms, x_t, g_t)
        dx, dg = vjp(dh_t.astype(F32))
        return (dres_t + dx,) * len(out_dtypes) + (dg,)

    return tile_call(name, fn, (T // tm,), [_rows(x, tm), _par(g), _rows(dh, tm), _rows(dres, tm)],
                     [_row_out(T, D, dt, tm) for dt in out_dtypes] + [_acc_out(1, D)])


def local_step(x, p, target, W):
    T, D = x.shape
    G = {}

    w_in_p = W["w_in_p"]
    mu_p = _proj_pad(_pad_cols(W["rw_mu"], 4128))[:, :RW_PAD]
    w_up_p = _pad_rows(W["rw_w_up"], 128)
    a_up_p = _pad_rows(W["rw_a_up"], 128)
    g_up_p = _pad_rows(W["rw_g_up"], 256)
    r_k = W["rw_r_k"].reshape(1, RW_WIDTH)
    rw_params = [mu_p, W["rw_w0"], w_up_p, W["rw_a0"], a_up_p, g_up_p, W["rw_k_k"], W["rw_k_a"]]

    h = rms_fwd("rms_mix", x, W["g_mix"])
    proj = matmul("proj_in_rw", h, w_in_p[:, :RW_PAD])
    pa = matmul("proj_in_att", h, w_in_p[:, RW_PAD:RW_PAD + ATT_COLS])
    gp = matmul("proj_gate", h, W["w_gate"])

    tm = 256
    rw_in = (proj, (tm, RW_PAD), lambda i: (i, 0))
    rw_halo = _prev_halo(proj, tm, RW_PAD)

    def rw_pre_tile(pid, Pc, halo, *params):
        prev8 = jnp.where(pid[0] > 0, halo, 0.0)
        params = [q.astype(F32) for q in params]
        return rw_pre(Pc, _shift_down(Pc, prev8, 1), *params)

    r, decay, k2, v, avec, bvec, g = tile_call(
        "rw_pre", rw_pre_tile, (T // tm,), [rw_in, rw_halo] + [_par(q) for q in rw_params],
        [_row_out(T, RW_WIDTH, F32, tm)] * 7)

    wa, ba, ka = scan_pair_terms(avec, decay, bvec, k2)
    vT = _to_head_time(v).astype(BF16)
    (yT, S_all, saT), late_slots = rwkv_scan_fwd(avec, decay, bvec, k2, r, vT, wa, ba, ka,
                                            exchange=_late_weight_sources(W))
    y = _from_head_time(yT)
    W = dict(W, **_late_weights(late_slots))

    post_params = [W["rw_ln_g"], W["rw_ln_b"], r_k]
    ya = tile_call("rw_post", lambda pid, *t: rw_post(*t), (T // tm,),
                   [_rows(z, tm) for z in (y, r, k2, v, g)] + [_par(q) for q in post_params],
                   [_row_out(T, RW_WIDTH, BF16, tm)])[0]

    att = [att_fwd(pa, gi, T) for gi in range(3)]
    o_l = [att[0][0], att[1][0], att[2][0], att[0][1], att[1][1], att[2][1]]
    yb = tile_call("att_combine", lambda pid, *t: att_combine(*t), (T // tm,),
                   [_rows(z, tm) for z in o_l], [_row_out(T, ATT_OUT, BF16, tm)])[0]

    za = matmul("branch_a", ya, W["w_branch_a"])
    zb = matmul("branch_b", yb, W["w_branch_b"])
    merged = tile_call("merge", lambda pid, *t: merge_fn(*t), (T // tm,),
                       [_rows(gp, tm), _par(W["b_gate"]), _rows(za, tm), _rows(zb, tm)],
                       [_row_out(T, D, BF16, tm)])[0]
    x1 = matmul("mix_out", merged, W["w_out"], res=x)

    h2 = rms_fwd("rms_ffn", x1, W["g_ffn"])
    u = matmul("ffn_up", h2, W["w_up"])
    f = conv_glu_fwd(u, W["conv_w"], W["conv_b"])
    x2 = matmul("ffn_down", f, W["w_down"], res=x1)

    h3 = rms_fwd("rms_ple", x2, W["g_ple"])
    zg = matmul("ple_gate", h3, W["w_ple_gate"])
    pe = matmul("ple_embed", p, W["w_ple"])

    def tail_tile(pid, x2_t, zg_t, pe_t, gf, tgt):
        loss, vjp = jax.vjp(lambda a_, b_, c_, d_: tail_loss(a_, b_, c_, d_, tgt), x2_t, zg_t, pe_t, gf)
        dx2, dzg, dpe, dgf = vjp(jnp.ones((), F32))
        return dx2, dzg, dpe, dgf, jnp.full((1, 128), loss, F32)

    tmt = 128
    dx3, dzg, dpe, dgf, loss_acc = tile_call(
        "tail_loss", tail_tile, (T // tmt,),
        [_rows(x2, tmt), _rows(zg, tmt), _rows(pe, tmt), _par(W["g_final"]), _rows(target, tmt)],
        [_row_out(T, D, F32, tmt), _row_out(T, D, BF16, tmt), _row_out(T, D, BF16, tmt),
         _acc_out(1, D), _acc_out(1, 128)])
    loss = loss_acc[0, 0]
    G["g_final"] = dgf

    wgrad = functools.partial(matmul, mode="tn", out_dtype=GRAD_WIRE)
    G["w_ple"] = wgrad("d_w_ple", p, dpe)
    G["w_ple_gate"] = wgrad("d_w_ple_gate", h3, dzg)
    dh3 = matmul("d_h3", dzg, W["w_ple_gate"], "nt")
    dx2, dx2b, G["g_ple"] = rms_bwd("rms_ple_bwd", x2, W["g_ple"], dh3, dx3)

    G["w_down"] = wgrad("d_w_down", f, dx2b)
    df = matmul("d_f", dx2b, W["w_down"], "nt", out_dtype=BF16)
    du_g, du_v, G["conv_w"], G["conv_b"] = conv_glu_bwd(u, W["conv_w"], W["conv_b"], df)
    du = jnp.concatenate([du_g, du_v], axis=1)
    G["w_up"] = wgrad("d_w_up", h2, du)
    dh2 = matmul("d_h2", du, W["w_up"], "nt")
    dx1, dx1b, G["g_ffn"] = rms_bwd("rms_ffn_bwd", x1, W["g_ffn"], dh2, dx2)

    G["w_out"] = wgrad("d_w_out", merged, dx1b)
    dmerged = matmul("d_merged", dx1b, W["w_out"], "nt", out_dtype=BF16)

    def merge_bwd_tile(pid, gp_t, bg, za_t, zb_t, dm_t):
        _, vjp = jax.vjp(merge_fn, gp_t, bg, za_t, zb_t)
        return vjp(dm_t.astype(F32))

    dgp, G["b_gate"], dza, dzb = tile_call(
        "merge_bwd", merge_bwd_tile, (T // tm,),
        [_rows(gp, tm), _par(W["b_gate"]), _rows(za, tm), _rows(zb, tm), _rows(dmerged, tm)],
        [_row_out(T, 2 * D, BF16, tm), _acc_out(1, 2 * D), _row_out(T, D, BF16, tm), _row_out(T, D, BF16, tm)])
    G["w_branch_a"] = wgrad("d_w_branch_a", ya, dza)
    dya = matmul("d_ya", dza, W["w_branch_a"], "nt")
    G["w_branch_b"] = wgrad("d_w_branch_b", yb, dzb)
    dyb = matmul("d_yb", dzb, W["w_branch_b"], "nt")
    G["w_gate"] = wgrad("d_w_gate", h, dgp)
    dh_gate = matmul("d_h_gate", dgp, W["w_gate"], "nt")

    def comb_bwd_tile(pid, *t):
        _, vjp = jax.vjp(att_combine, *t[:6])
        return vjp(t[6])

    d_ol = tile_call("att_combine_bwd", comb_bwd_tile, (T // tm,),
                     [_rows(z, tm) for z in o_l] + [_rows(dyb, tm)],
                     [_row_out(T, ATT_OUT, F32, tm)] * 6)
    dqkv = [att_bwd(pa, att[gi][0], att[gi][1], d_ol[gi], d_ol[3 + gi], gi, T) for gi in range(3)]
    d_att = [dqkv[gi][j] for j in range(3) for gi in range(3)]

    def post_bwd_tile(pid, *t):
        _, vjp = jax.vjp(rw_post, *t[:8])
        return vjp(t[8])

    dy, dr_p, dk2_p, dv_p, dg, G["rw_ln_g"], G["rw_ln_b"], d_rk = tile_call(
        "rw_post_bwd", post_bwd_tile, (T // tm,),
        [_rows(z, tm) for z in (y, r, k2, v, g)] + [_par(q) for q in post_params] + [_rows(dya, tm)],
        [_row_out(T, RW_WIDTH, F32, tm)] * 5 + [_acc_out(1, RW_WIDTH)] * 3)
    G["rw_r_k"] = d_rk.reshape(W["rw_r_k"].shape)

    (da, dw, db, dk_s, dr_s, dvT), G["_early_parts"] = rwkv_scan_bwd(
        avec, decay, bvec, k2, r, v, dy, S_all, saT, exchange=_early_grad_sources(G))
    dv_s = _from_head_time(dvT)

    tmb = 128
    rw_in_b = (proj, (tmb, RW_PAD), lambda i: (i, 0))

    def pre_bwd_tile(pid, Pc, halo, *t):
        prev8 = jnp.where(pid[0] > 0, halo, 0.0)
        params = [q.astype(F32) for q in t[:8]]
        dr1, dr2, dw_, dk1, dk2_, dv1, dv2, da_, db_, dg_ = t[8:]
        _, vjp = jax.vjp(rw_pre, Pc, _shift_down(Pc, prev8, 1), *params)
        return vjp((dr1 + dr2, dw_, dk1 + dk2_, dv1 + dv2, da_, db_, dg_))

    cts = (dr_s, dr_p, dw, dk_s, dk2_p, dv_s, dv_p, da, db, dg)
    res = tile_call(
        "rw_pre_bwd", pre_bwd_tile, (T // tmb,),
        [rw_in_b, _prev_halo(proj, tmb, RW_PAD)] + [_par(q) for q in rw_params] + [_rows(z, tmb) for z in cts],
        [_row_out(T, RW_PAD, F32, tmb)] * 2 + [_acc_out(*q.shape) for q in rw_params])
    dPc, dPs = res[0], res[1]
    d_mu, G["rw_w0"], d_wup, G["rw_a0"], d_aup, d_gup, G["rw_k_k"], G["rw_k_a"] = res[2:]
    G["rw_mu"] = _rw_unpad(d_mu)
    G["rw_w_up"], G["rw_a_up"], G["rw_g_up"] = d_wup[:64], d_aup[:64], d_gup[:160]

    def dproj_tile(pid, dPc_t, dPs_t, nxt, *att_t):
        nxt = jnp.where(pid[0] < T // tm - 1, nxt, 0.0)
        tail = jnp.zeros((dPc_t.shape[0], PROJ_TAIL), F32)
        return jnp.concatenate([dPc_t + _shift_up(dPs_t, nxt, 1)] + list(att_t) + [tail], axis=1)

    dproj = tile_call("d_proj", dproj_tile, (T // tm,),
                      [_rows(dPc, tm), _rows(dPs, tm), _next_halo(dPs, tm, RW_PAD, T)] + [_rows(z, tm) for z in d_att],
                      [_row_out(T, PROJ_PAD, BF16, tm)])[0]
    G["w_in_p"] = wgrad("d_w_in", h, dproj)
    dh = matmul("d_h", dproj, w_in_p, "nt", res=dh_gate)
    dx, G["g_mix"] = rms_bwd("rms_mix_bwd", x, W["g_mix"], dh, dx1, with_bf16=False)
    return loss, dx, G


def _mesh_pos():
    return lax.axis_index("x"), lax.axis_index("y"), lax.axis_index("c")


def _peer(pos, k):
    x, y, c = pos
    px = 1 - x if k & 4 else x
    py = 1 - y if k & 2 else y
    pc = 1 - c if k & 1 else c
    return (px, py, pc), 4 * px + 2 * py + pc


def all_gather_blocks(name, blocks):
    n = len(blocks)

    def body(*refs):
        x_refs, out_refs = refs[:n], refs[n:2 * n]
        send_sems, recv_sems, local_sems = refs[2 * n:]
        x, y, c = _mesh_pos()
        me, sibling = (x, y, c), (x, y, 1 - c)
        chips = [(1 - x, y), (x, 1 - y), (1 - x, 1 - y)]
        ops = range(n)

        def slot(i, px, py, pc):
            return out_refs[i].at[4 * px + 2 * py + pc]

        def copy(k, i, block, to, own=False):
            return pltpu.make_async_remote_copy(
                src_ref=x_refs[i] if own else slot(i, *block), dst_ref=slot(i, *block),
                send_sem=send_sems.at[k, i], recv_sem=recv_sems.at[k, i],
                device_id=to, device_id_type=pl.DeviceIdType.MESH)

        mine = [pltpu.make_async_copy(x_refs[i], slot(i, *me), local_sems.at[i]) for i in ops]
        first = [copy(0, i, me, sibling, own=True) for i in ops]
        first += [copy(1 + j, i, me, (*chip, c), own=True) for j, chip in enumerate(chips) for i in ops]
        for cp in mine + first:
            cp.start()
        passed = []
        for j, chip in enumerate(chips):
            for i in ops:
                copy(1 + j, i, (*chip, c), me).wait_recv()
                passed.append(copy(4 + j, i, (*chip, c), sibling))
                passed[-1].start()
        for i in ops:
            copy(0, i, sibling, me).wait_recv()
        for j, chip in enumerate(chips):
            for i in ops:
                copy(4 + j, i, (*chip, 1 - c), me).wait_recv()
        for cp in first + passed:
            cp.wait_send()
        for cp in mine:
            cp.wait()

    return pl.pallas_call(
        body, name=name,
        in_specs=[pl.BlockSpec(memory_space=pl.ANY)] * n,
        out_specs=[pl.BlockSpec(memory_space=pl.ANY)] * n,
        out_shape=[jax.ShapeDtypeStruct((N_DEV,) + b.shape, b.dtype) for b in blocks],
        scratch_shapes=[pltpu.SemaphoreType.DMA((N_DEV - 1, n)), pltpu.SemaphoreType.DMA((N_DEV - 1, n)),
                        pltpu.SemaphoreType.DMA((n,))],
        compiler_params=pltpu.CompilerParams(has_side_effects=True),
    )(*blocks)


WHOLE = 0


def _exchange_shapes(srcs):
    shapes = [a.shape[1:] if cols is None else a.shape if cols == WHOLE else (a.shape[0], cols) for a, cols in srcs]
    return [jax.ShapeDtypeStruct((N_DEV,) + s, a.dtype) for s, (a, _) in zip(shapes, srcs)]


def _exchange_sems(n):
    return [pltpu.SemaphoreType.DMA((N_DEV - 1, n)), pltpu.SemaphoreType.DMA((N_DEV - 1, n)),
            pltpu.SemaphoreType.DMA((n,))]


def _exchange_ops(col_widths, x_refs, out_refs, send_sems, recv_sems, local_sems):
    n = len(col_widths)
    pos = _mesh_pos()
    me = 4 * pos[0] + 2 * pos[1] + pos[2]

    def piece(i, d):
        cols = col_widths[i]
        if cols is None:
            return x_refs[i].at[d]
        if cols == WHOLE:
            return x_refs[i]
        return x_refs[i].at[:, pl.ds(pl.multiple_of(d * cols, 128), cols)]

    def local(i):
        return pltpu.make_async_copy(piece(i, me), out_refs[i].at[me], local_sems.at[i])

    def remote(k, i, landing):
        peer, idx = _peer(pos, k)
        return pltpu.make_async_remote_copy(
            src_ref=piece(i, idx), dst_ref=out_refs[i].at[idx if landing else me],
            send_sem=send_sems.at[k - 1, i], recv_sem=recv_sems.at[k - 1, i],
            device_id=peer, device_id_type=pl.DeviceIdType.MESH)

    pairs = [(k, i) for k in range(1, N_DEV) for i in range(n)]

    def start():
        for i in range(n):
            local(i).start()
        for k, i in pairs:
            remote(k, i, False).start()

    def wait():
        for k, i in pairs:
            remote(k, i, True).wait_recv()
        for k, i in pairs:
            remote(k, i, False).wait_send()
        for i in range(n):
            local(i).wait()

    return start, wait


def all_to_all_blocks(name, srcs):
    n = len(srcs)

    def body(*refs):
        start, wait = _exchange_ops([c for _, c in srcs], refs[:n], refs[n:2 * n], *refs[2 * n:])
        start()
        wait()

    return pl.pallas_call(
        body, name=name,
        in_specs=[pl.BlockSpec(memory_space=pl.ANY)] * n,
        out_specs=[pl.BlockSpec(memory_space=pl.ANY)] * n,
        out_shape=_exchange_shapes(srcs),
        scratch_shapes=_exchange_sems(n),
        compiler_params=pltpu.CompilerParams(has_side_effects=True),
    )(*[a for a, _ in srcs])


def _adam_row_tile(R, C):
    best = None
    for t in range(16, R + 1, 16):
        if R % t == 0 and t * C <= ADAM_TILE_ELEMS:
            best = t
    return best if best is not None else R


def reduce_adamw(name, parts, w, m, v):
    _, R, C = parts.shape
    tr = _adam_row_tile(R, C)

    def fn(pid, parts_t, w_t, m_t, v_t):
        g = parts_t[0].astype(F32)
        for i in range(1, N_DEV):
            g = g + parts_t[i].astype(F32)
        m_n = ADAM_B1 * m_t + (1.0 - ADAM_B1) * g
        v_n = ADAM_B2 * v_t + (1.0 - ADAM_B2) * (g * g)
        m_hat = m_n / (1.0 - ADAM_B1 ** ADAM_STEP)
        v_hat = v_n / (1.0 - ADAM_B2 ** ADAM_STEP)
        delta = -ADAM_LR * (m_hat / (jnp.sqrt(v_hat) + ADAM_EPS) + ADAM_WD * w_t)
        return g, delta, m_n, v_n

    row = lambda a: (a, (tr, C), lambda i: (i, 0))
    out = ((R, C), F32, (tr, C), lambda i: (i, 0), None)
    return tile_call(name, fn, (R // tr,),
                     [(parts, (N_DEV, tr, C), lambda i: (0, i, 0)), row(w), row(m), row(v)], [out] * 4)


PARAMS = (
    ("g_mix", (1, 1024), None), ("w_in", (1024, 4128), 1), ("rw_mu", (1, 1824), None), ("rw_w0", (1, 512), None),
    ("rw_w_up", (64, 512), 1), ("rw_a0", (1, 512), None), ("rw_a_up", (64, 512), 1), ("rw_g_up", (160, 512), 1),
    ("rw_k_k", (1, 512), None), ("rw_k_a", (1, 512), None), ("rw_r_k", (8, 64), None), ("rw_ln_g", (1, 512), None),
    ("rw_ln_b", (1, 512), None), ("w_branch_a", (512, 1024), 1), ("w_branch_b", (256, 1024), 1),
    ("w_gate", (1024, 2048), 1), ("b_gate", (1, 2048), None), ("w_out", (1024, 1024), 0), ("g_ffn", (1, 1024), None),
    ("w_up", (1024, 6144), 1), ("conv_w", (3, 6144), 1), ("conv_b", (1, 6144), None), ("w_down", (3072, 1024), 0),
    ("g_ple", (1, 1024), None), ("w_ple_gate", (1024, 1024), 0), ("w_ple", (256, 1024), 1), ("g_final", (1, 1024), None),
)
SHARDED = tuple(q for q in PARAMS if q[2] is not None)
REPLICATED = tuple(q for q in PARAMS if q[2] is None)
BIG_NAMES = ("w_in", "w_up", "w_gate", "w_out", "w_down", "w_ple_gate", "w_branch_a", "w_branch_b", "w_ple")
BIG = tuple(q for q in SHARDED if q[0] in BIG_NAMES)
SMALL_SHARDED = tuple(q for q in SHARDED if q[0] not in BIG_NAMES)
PACK_COLS = 1024
F32_GATHERED = ("conv_w",)


def _local_shape(shape, axis):
    s = list(shape)
    s[axis] //= N_DEV
    return tuple(s)


def _numel(shape):
    return int(np.prod(shape))


def _pad_flat(z, mult):
    n = z.shape[-1]
    total = -(-n // mult) * mult
    return jnp.pad(z, [(0, 0)] * (z.ndim - 1) + [(0, total - n)])


def _full_from_slots(slots, shape, axis):
    loc = _local_shape(shape, axis)
    z = slots.reshape((N_DEV,) + loc)
    if axis == 0:
        return z.reshape(shape)
    return z.transpose(1, 0, 2).reshape(shape)


def _slots_from_full(full, shape, axis):
    loc = _local_shape(shape, axis)
    if axis == 0:
        return full.reshape(N_DEV, _numel(loc))
    return full.reshape(shape[0], N_DEV, loc[1]).transpose(1, 0, 2).reshape(N_DEV, _numel(loc))


W_IN_SLOT = 640
W_IN_LOCAL = 4128 // N_DEV


def _block_shape(shape, axis):
    return _local_shape(shape, axis) if axis is not None else shape


def _pad_w_in(block):
    return jnp.pad(block, ((0, 0), (0, W_IN_SLOT - W_IN_LOCAL)))


def _proj_col(s):
    return s + jnp.where(s >= 1600, 64, 0) + jnp.where(s >= 1664, 64, 0) + jnp.where(s >= 1824, 96, 0)


def _perm_tile(d, c0, width):
    j = lax.broadcasted_iota(jnp.int32, (W_IN_SLOT, width), 0)
    c = c0 + lax.broadcasted_iota(jnp.int32, (W_IN_SLOT, width), 1)
    hit = (_proj_col(d * W_IN_LOCAL + j) == c) & (j < W_IN_LOCAL)
    return jnp.where(hit, 1.0, 0.0).astype(BF16)


PERM_TILE = 768


def w_in_unshuffle(slots):
    _, K, _ = slots.shape
    tn = PERM_TILE
    reach = 3

    def first_slot(j):
        return j + jnp.where(j >= 3, 1, 0) + jnp.where(j >= 5, 1, 0)

    def body(a_ref, o_ref, acc_ref):
        j, kk = pl.program_id(0), pl.program_id(1)
        d = first_slot(j) + kk

        @pl.when(kk == 0)
        def _():
            acc_ref[...] = jnp.zeros_like(acc_ref)

        @pl.when(d < N_DEV)
        def _():
            acc_ref[...] += jnp.dot(a_ref[0], _perm_tile(d, j * tn, tn), preferred_element_type=F32)

        @pl.when(kk == reach - 1)
        def _():
            o_ref[...] = acc_ref[...].astype(o_ref.dtype)

    return pl.pallas_call(
        body, name="w_in_unshuffle", grid=(PROJ_PAD // tn, reach),
        in_specs=[pl.BlockSpec((1, K, W_IN_SLOT), lambda j, kk: (jnp.minimum(first_slot(j) + kk, N_DEV - 1), 0, 0))],
        out_specs=pl.BlockSpec((K, tn), lambda j, kk: (0, j)),
        out_shape=jax.ShapeDtypeStruct((K, PROJ_PAD), BF16),
        scratch_shapes=[pltpu.VMEM((K, tn), F32)],
        compiler_params=_cparams(2),
    )(slots)


def w_in_shuffle_grad(dw):
    K = dw.shape[0]
    tk = PERM_TILE

    def first_tile(d):
        return _proj_col(d * W_IN_LOCAL) // tk

    def body(g_ref, o_ref, acc_ref):
        d, kk = pl.program_id(0), pl.program_id(1)
        perm = _perm_tile(d, (first_tile(d) + kk) * tk, tk)
        part = lax.dot_general(g_ref[...].astype(BF16), perm, NT_DIMS, preferred_element_type=F32)

        @pl.when(kk == 0)
        def _():
            acc_ref[...] = part

        @pl.when(kk == 1)
        def _():
            o_ref[0] = (acc_ref[...] + part).astype(o_ref.dtype)

    return pl.pallas_call(
        body, name="w_in_shuffle_grad", grid=(N_DEV, 2),
        in_specs=[pl.BlockSpec((K, tk), lambda d, kk: (0, first_tile(d) + kk))],
        out_specs=pl.BlockSpec((1, K, W_IN_SLOT), lambda d, kk: (d, 0, 0)),
        out_shape=jax.ShapeDtypeStruct((N_DEV, K, W_IN_SLOT), GRAD_WIRE),
        scratch_shapes=[pltpu.VMEM((K, W_IN_SLOT), F32)],
        compiler_params=_cparams(2),
    )(dw)


def _flat_rows(pieces, dtype, row_mult):
    flat = jnp.concatenate([z.astype(dtype) for z in pieces], axis=-1)
    flat = _pad_flat(flat, row_mult * PACK_COLS)
    return flat.reshape(flat.shape[:-1] + (-1, PACK_COLS))


FIRST = tuple(q for q in BIG if q[0] in ("w_in", "w_gate"))
LATE = tuple(q for q in BIG if q not in FIRST)


def _matrix_from_slots(slots, shape, axis):
    return slots.reshape(shape) if axis == 0 else slots.transpose(1, 0, 2).reshape(shape)


def _late_weight_sources(W):
    return [(blk, WHOLE) for blk in W["_late_blocks"]]


def _late_weights(slots):
    return {n: _matrix_from_slots(s, shape, axis) for (n, shape, axis), s in zip(LATE, slots)}


def gather_weights(local):
    blocks = [(_pad_w_in(local[n]) if n == "w_in" else local[n]).astype(BF16) for n, _, _ in FIRST]
    small = [q for q in SMALL_SHARDED if q[0] not in F32_GATHERED]
    exact = [q for q in SMALL_SHARDED if q[0] in F32_GATHERED]
    blocks.append(_flat_rows([local[n].reshape(-1) for n, _, _ in small], BF16, 16))
    blocks.append(_flat_rows([local[n].reshape(-1) for n, _, _ in exact], F32, 8))
    got = all_gather_blocks("weight_all_gather", blocks)
    full = {"_late_blocks": [local[n].astype(BF16) for n, _, _ in LATE]}
    for (n, shape, axis), slots in zip(FIRST, got):
        if n == "w_in":
            full["w_in_p"] = w_in_unshuffle(slots)
        else:
            full[n] = _matrix_from_slots(slots, shape, axis)
    for group, slots in ((small, got[-2]), (exact, got[-1])):
        slots, off = slots.reshape(N_DEV, -1), 0
        for n, shape, axis in group:
            size = _numel(_local_shape(shape, axis))
            full[n] = _full_from_slots(slots[:, off:off + size], shape, axis)
            off += size
    for n, _, _ in REPLICATED:
        full[n] = local[n]
    return full


LOSS_SLOT = ("_loss", (1, 2), None)
PACKED_SMALL = SMALL_SHARDED + REPLICATED + (LOSS_SLOT,)


def _pack_small(vals):
    pieces = [vals[n].reshape(-1) if n in vals else jnp.zeros((_numel(shape),), F32) for n, shape, _ in PACKED_SMALL]
    return _flat_rows(pieces, F32, 16)


def _unpack_small(packed):
    flat, out, off = packed.reshape(-1), {}, 0
    for n, shape, axis in PACKED_SMALL:
        loc = _block_shape(shape, axis)
        out[n] = flat[off:off + _numel(loc)].reshape(loc)
        off += _numel(loc)
    return out


EARLY = tuple(q for q in BIG if q[0] != "w_in")


def _early_grad_sources(G):
    srcs = []
    for n, shape, axis in EARLY:
        if axis == 0:
            srcs.append((G[n].astype(GRAD_WIRE).reshape((N_DEV,) + _local_shape(shape, axis)), None))
        else:
            srcs.append((G[n].astype(GRAD_WIRE), shape[1] // N_DEV))
    return srcs


def _late_grad_sources(G, loss_local):
    srcs = [(w_in_shuffle_grad(G["w_in_p"]), None)]
    rows = [_slots_from_full(G[n].reshape(shape), shape, axis) for n, shape, axis in SMALL_SHARDED]
    loss_hi = loss_local.astype(GRAD_WIRE).astype(F32)
    rep = jnp.concatenate([G[n].reshape(-1) for n, _, _ in REPLICATED] + [jnp.stack([loss_hi, loss_local - loss_hi])])
    rows.append(jnp.broadcast_to(rep[None, :], (N_DEV, rep.shape[0])))
    srcs.append((_flat_rows(rows, GRAD_WIRE, 16), None))
    return srcs


def _step(x, p, target, local_w, local_m, local_v):
    full = gather_weights(local_w)
    loss_local, dx, G = local_step(x, p, target, full)
    late = all_to_all_blocks("grad_all_to_all", _late_grad_sources(G, loss_local))
    parts = [late[0]] + list(G["_early_parts"]) + [late[1]]
    outs = [{}, {}, {}, {}]
    for (n, shape, axis), part in zip((BIG[0],) + EARLY, parts):
        prep = _pad_w_in if n == "w_in" else (lambda z: z)
        res = reduce_adamw("adamw_" + n, part, prep(local_w[n]), prep(local_m[n]), prep(local_v[n]))
        for o, z in zip(outs, res):
            o[n] = z[:, :W_IN_LOCAL] if n == "w_in" else z
    res = reduce_adamw("adamw_small", parts[-1], _pack_small(local_w), _pack_small(local_m), _pack_small(local_v))
    for o, z in zip(outs, res):
        o.update(_unpack_small(z))
    loss = jnp.sum(outs[0]["_loss"])
    return loss, dx, outs


def kernel(x, p, g_mix, w_in, rw_mu, rw_w0, rw_w_up, rw_a0, rw_a_up, rw_g_up, rw_k_k, rw_k_a, rw_r_k, rw_ln_g, rw_ln_b, w_branch_a, w_branch_b, w_gate, b_gate, w_out, g_ffn, w_up, conv_w, conv_b, w_down, g_ple, w_ple_gate, w_ple, g_final, loss_target, m_g_mix, m_w_in, m_rw_mu, m_rw_w0, m_rw_w_up, m_rw_a0, m_rw_a_up, m_rw_g_up, m_rw_k_k, m_rw_k_a, m_rw_r_k, m_rw_ln_g, m_rw_ln_b, m_w_branch_a, m_w_branch_b, m_w_gate, m_b_gate, m_w_out, m_g_ffn, m_w_up, m_conv_w, m_conv_b, m_w_down, m_g_ple, m_w_ple_gate, m_w_ple, m_g_final, v_g_mix, v_w_in, v_rw_mu, v_rw_w0, v_rw_w_up, v_rw_a0, v_rw_a_up, v_rw_g_up, v_rw_k_k, v_rw_k_a, v_rw_r_k, v_rw_ln_g, v_rw_ln_b, v_w_branch_a, v_w_branch_b, v_w_gate, v_b_gate, v_w_out, v_g_ffn, v_w_up, v_conv_w, v_conv_b, v_w_down, v_g_ple, v_w_ple_gate, v_w_ple, v_g_final):
    args = dict(locals())
    names = [n for n, _, _ in PARAMS]
    orig_shape = {n: args[n].shape for n in names}

    def strip(prefix):
        out = {}
        for n, shape, axis in PARAMS:
            a = args[prefix + n]
            loc = _local_shape(shape, axis) if axis is not None else shape
            out[n] = a.reshape(loc)
        return out

    local_w, local_m, local_v = strip(""), strip("m_"), strip("v_")
    T, D = x.shape[-2], x.shape[-1]
    loss, dx, (g, delta, m_n, v_n) = _step(x.reshape(T, D), p.reshape(T, p.shape[-1]), loss_target.reshape(T, D),
                                           local_w, local_m, local_v)
    outs = [loss, dx.reshape(x.shape)]
    for group in (g, delta, m_n, v_n):
        outs += [group[n].reshape(orig_shape[n]) for n in names]
    return tuple(outs)
```

```python
import functools
import math

import numpy as np
import jax
import jax.numpy as jnp
from jax import lax
from jax.experimental import pallas as pl
from jax.experimental.pallas import tpu as pltpu

F32 = jnp.float32
BF16 = jnp.bfloat16
GRAD_WIRE = jnp.bfloat16

N_DEV = 8
NORM_EPS = 1e-6
RW_LN_EPS = 64e-5
HEAD = 64
RW_WIDTH = 512
ATT_GROUPS = ((128, 1), (512, 4), (2048, 16))
ATT_HEADS = 12
ATT_OUT = 256
ATT_COLS = 2304
OFF_XW, OFF_XA, OFF_XG, RW_PAD, PROJ_PAD = 1536, 1664, 1792, 2048, 4608
PROJ_TAIL = PROJ_PAD - RW_PAD - ATT_COLS
D_FF = 3072

ADAM_LR, ADAM_B1, ADAM_B2, ADAM_EPS, ADAM_WD, ADAM_STEP = 0.001, 0.9, 0.999, 1e-08, 0.01, 10

VMEM_LIMIT_BYTES = 56 * 1024 * 1024
ADAM_TILE_ELEMS = 256 * 1024
NEG_BIG = -1e30

NT_DIMS = (((1,), (1,)), ((), ()))
TN_DIMS = (((0,), (0,)), ((), ()))
NN_DIMS = (((1,), (0,)), ((), ()))


def _cparams(n_axes):
    return pltpu.CompilerParams(dimension_semantics=("arbitrary",) * n_axes,
                                vmem_limit_bytes=VMEM_LIMIT_BYTES)


def _split2(x):
    hi = x.astype(BF16)
    lo = (x - hi.astype(F32)).astype(BF16)
    return hi, lo


def _seg_mat(n):
    r = lax.shift_right_logical(lax.broadcasted_iota(jnp.int32, (n, n), 0), 6)
    c = lax.shift_right_logical(lax.broadcasted_iota(jnp.int32, (n, n), 1), 6)
    return jnp.where(r == c, 1.0, 0.0).astype(BF16)


def _segb(x, seg):
    return _segb_stack([(x, 2)], seg)[0]


def _segb_stack(items, seg):
    rows = items[0][0].shape[0]
    parts = []
    for x, passes in items:
        parts += list(_split2(x)) if passes == 2 else [x.astype(BF16)]
    res = jnp.dot(jnp.concatenate(parts, axis=0), seg, preferred_element_type=F32)
    out, at = [], 0
    for _, passes in items:
        piece = res[at * rows:(at + 1) * rows]
        if passes == 2:
            piece = piece + res[(at + 1) * rows:(at + 2) * rows]
        out.append(piece)
        at += passes
    return out


def _segb1(x, seg):
    return jnp.dot(x.astype(BF16), seg, preferred_element_type=F32)


@jax.custom_vjp
def segsum(x):
    return _segb(x, _seg_mat(x.shape[1]))


def _segsum_fwd(x):
    return segsum(x), None


def _segsum_bwd(_, ct):
    return (segsum(ct),)


segsum.defvjp(_segsum_fwd, _segsum_bwd)


@jax.custom_vjp
def bdot(a, b):
    return jnp.dot(a.astype(BF16), b.astype(BF16), preferred_element_type=F32)


def _bdot_fwd(a, b):
    return bdot(a, b), (a, b)


def _bdot_bwd(res, ct):
    a, b = res
    ctb = ct.astype(BF16)
    da = lax.dot_general(ctb, b.astype(BF16), NT_DIMS, preferred_element_type=F32)
    db = lax.dot_general(a.astype(BF16), ctb, TN_DIMS, preferred_element_type=F32)
    return da.astype(a.dtype), db.astype(b.dtype)


bdot.defvjp(_bdot_fwd, _bdot_bwd)


def _sig(x):
    return 1.0 / (1.0 + jnp.exp(-x))


def _softplus(z):
    return jnp.maximum(z, 0.0) + jnp.log(1.0 + jnp.exp(-jnp.abs(z)))


def _gelu_tanh(x):
    return 0.5 * x * (1.0 + jnp.tanh(0.7978845608028654 * (x + 0.044715 * (x * x * x))))


def _rms(x, g):
    return x * lax.rsqrt(jnp.mean(x * x, axis=-1, keepdims=True) + NORM_EPS) * g


def _shift_down(x, prev8, n):
    rolled = pltpu.roll(x, n, 0)
    top = pltpu.roll(prev8, n, 0)
    rid = lax.broadcasted_iota(jnp.int32, (8, x.shape[1]), 0)
    head = jnp.where(rid < n, top, rolled[:8])
    return jnp.concatenate([head, rolled[8:]], axis=0)


def _shift_up(x, next8, n):
    rows = x.shape[0]
    rolled = pltpu.roll(x, rows - n, 0)
    bottom = pltpu.roll(next8, 8 - n, 0)
    rid = lax.broadcasted_iota(jnp.int32, (8, x.shape[1]), 0)
    tail = jnp.where(rid >= 8 - n, bottom, rolled[rows - 8:])
    return jnp.concatenate([rolled[:rows - 8], tail], axis=0)


def tile_call(name, fn, grid, ins, outs, scratch=()):
    n_in, n_out = len(ins), len(outs)
    acc_axes = [o[4] for o in outs]

    def body(*refs):
        pids = tuple(pl.program_id(a) for a in range(len(grid)))
        vals = fn(pids, *[r[...] for r in refs[:n_in]], *refs[n_in + n_out:])
        if not isinstance(vals, (tuple, list)):
            vals = (vals,)
        for o_ref, val, ax in zip(refs[n_in:n_in + n_out], vals, acc_axes):
            if ax is None:
                o_ref[...] = val.astype(o_ref.dtype)
            else:
                @pl.when(pids[ax] == 0)
                def _(o_ref=o_ref):
                    o_ref[...] = jnp.zeros_like(o_ref)

                o_ref[...] += val.astype(o_ref.dtype)

    res = pl.pallas_call(
        body, name=name, grid=grid,
        in_specs=[pl.BlockSpec(b, im) for _, b, im in ins],
        out_specs=[pl.BlockSpec(o[2], o[3]) for o in outs],
        out_shape=[jax.ShapeDtypeStruct(o[0], o[1]) for o in outs],
        scratch_shapes=[pltpu.VMEM(s, d) for s, d in scratch],
        compiler_params=_cparams(len(grid)),
    )(*[a for a, _, _ in ins])
    return res


def _rows(a, tm):
    return (a, (tm, a.shape[1]), lambda i: (i, 0))


def _par(a):
    return (a, a.shape, lambda i: (0, 0))


def _row_out(T, C, dtype, tm):
    return ((T, C), dtype, (tm, C), lambda i: (i, 0), None)


def _acc_out(R, C):
    return ((R, C), F32, (R, C), lambda i: (0, 0), 0)


def _prev_halo(a, tm, C):
    return (a, (8, C), lambda i: (jnp.maximum(i * (tm // 8) - 1, 0), 0))


def _next_halo(a, tm, C, T):
    return (a, (8, C), lambda i: (jnp.minimum((i + 1) * (tm // 8), T // 8 - 1), 0))


def _pick(n, target):
    for t in (target, 1024, 768, 512, 384, 256, 128):
        if t <= target and n % t == 0:
            return t
    return n


def matmul(name, a, b, mode="nn", res=None, out_dtype=F32, tm=1024, tn=1024, tk=1024):
    if mode == "nn":
        (M, K), (K2, N) = a.shape, b.shape
    elif mode == "tn":
        (K, M), (K2, N) = a.shape, b.shape
    else:
        (M, K), (N, K2) = a.shape, b.shape
    assert K == K2, (name, a.shape, b.shape, mode)
    tm, tn, tk = _pick(M, tm), _pick(N, tn), _pick(K, tk)
    nk = K // tk
    dims = {"nn": NN_DIMS, "tn": TN_DIMS, "nt": NT_DIMS}[mode]
    a_spec = {"nn": pl.BlockSpec((tm, tk), lambda i, j, k: (i, k)),
              "tn": pl.BlockSpec((tk, tm), lambda i, j, k: (k, i)),
              "nt": pl.BlockSpec((tm, tk), lambda i, j, k: (i, k))}[mode]
    b_spec = {"nn": pl.BlockSpec((tk, tn), lambda i, j, k: (k, j)),
              "tn": pl.BlockSpec((tk, tn), lambda i, j, k: (k, j)),
              "nt": pl.BlockSpec((tn, tk), lambda i, j, k: (j, k))}[mode]
    has_res = res is not None

    def body(*refs):
        if has_res:
            a_ref, b_ref, r_ref, o_ref, acc_ref = refs
        else:
            a_ref, b_ref, o_ref, acc_ref = refs
        k = pl.program_id(2)

        @pl.when(k == 0)
        def _():
            acc_ref[...] = jnp.zeros_like(acc_ref)

        acc_ref[...] += lax.dot_general(a_ref[...].astype(BF16), b_ref[...].astype(BF16), dims,
                                        preferred_element_type=F32)

        @pl.when(k == nk - 1)
        def _():
            out = acc_ref[...]
            if has_res:
                out = out + r_ref[...].astype(F32)
            o_ref[...] = out.astype(o_ref.dtype)

    in_specs = [a_spec, b_spec]
    args = [a, b]
    if has_res:
        in_specs.append(pl.BlockSpec((tm, tn), lambda i, j, k: (i, j)))
        args.append(res)
    return pl.pallas_call(
        body, name=name, grid=(M // tm, N // tn, nk),
        in_specs=in_specs,
        out_specs=pl.BlockSpec((tm, tn), lambda i, j, k: (i, j)),
        out_shape=jax.ShapeDtypeStruct((M, N), out_dtype),
        scratch_shapes=[pltpu.VMEM((tm, tn), F32)],
        compiler_params=_cparams(3),
    )(*args)


def rw_pre(Pc, Ps, mu, w0, w_up, a0, a_up, g_up, k_k, k_a):
    Pm = Pc + (Ps - Pc) * mu
    r, k, v = Pm[:, 0:512], Pm[:, 512:1024], Pm[:, 1024:1536]
    xw, xa, xg = Pm[:, OFF_XW:OFF_XA], Pm[:, OFF_XA:OFF_XG], Pm[:, OFF_XG:RW_PAD]
    w = -_softplus(-(w0 + bdot(jnp.tanh(xw), w_up))) - 0.5
    decay = jnp.exp(-jnp.exp(w))
    a = _sig(a0 + bdot(xa, a_up))
    g = bdot(_sig(xg), g_up)
    kk = k * k_k
    kk = kk / jnp.maximum(jnp.sqrt(segsum(kk * kk)), 1e-12)
    k2 = k * (1.0 + (a - 1.0) * k_a)
    return r, decay, k2, v, -kk, kk * a, g


def rw_post(y, r, k2, v, g, ln_g, ln_b, r_k):
    mean = segsum(y) * (1.0 / HEAD)
    d = y - mean
    var = segsum(d * d) * (1.0 / HEAD)
    yn = d * lax.rsqrt(var + RW_LN_EPS) * ln_g + ln_b
    bonus = segsum(r * k2 * r_k) * v
    return (yn + bonus) * g


def att_combine(o1, o2, o3, l1, l2, l3):
    m = jnp.maximum(jnp.maximum(l1, l2), l3)
    e1, e2, e3 = jnp.exp(l1 - m), jnp.exp(l2 - m), jnp.exp(l3 - m)
    return (e1 * o1 + e2 * o2 + e3 * o3) / (e1 + e2 + e3)


def merge_fn(gp, bg, za, zb):
    s = _sig(gp + bg)
    half = za.shape[1]
    return s[:, :half] * za + s[:, half:] * zb


def tail_loss(x2, zg, pe, g_final, target):
    x3 = x2 + _sig(zg) * pe
    y = _rms(x3, g_final)
    err = (y - target) * (y - target)
    return 0.5 * jnp.sum(jnp.mean(err, axis=-1, keepdims=True))


SCAN_CHUNK = HEAD
SCAN_LANES = 256
SCAN_UNROLL_FWD, SCAN_UNROLL_BWD = 8, 8


def _to_head_time(z):
    T = z.shape[0]
    return z.reshape(T // HEAD, HEAD, RW_WIDTH // HEAD, HEAD).transpose(0, 3, 2, 1).reshape(T // HEAD, HEAD, RW_WIDTH)


def _from_head_time(zt):
    C = zt.shape[0]
    return zt.reshape(C, HEAD, RW_WIDTH // HEAD, HEAD).transpose(0, 3, 2, 1).reshape(C * HEAD, RW_WIDTH)


def _unrolled_loop(n, step, init, unroll):
    def body(i, carry):
        for j in range(unroll):
            carry = step(i * unroll + j, carry)
        return carry

    return lax.fori_loop(0, n // unroll, body, init)


def _lane_groups():
    return [slice(j * SCAN_LANES, (j + 1) * SCAN_LANES) for j in range(RW_WIDTH // SCAN_LANES)]


def scan_pair_terms(a, w, b, k, tm=256):
    T = a.shape[0]

    def fn(pid, a_t, nxt, w_t, b_t, k_t):
        a_next = _shift_up(a_t, jnp.where(pid[0] < T // tm - 1, nxt, 0.0), 1)
        return w_t * a_next, segsum(b_t * a_next), segsum(k_t * a_next)

    return tile_call("scan_pair_terms", fn, (T // tm,),
                     [_rows(a, tm), _next_halo(a, tm, RW_WIDTH, T), _rows(w, tm), _rows(b, tm), _rows(k, tm)],
                     [_row_out(T, RW_WIDTH, F32, tm)] * 3)


def rwkv_scan_fwd(a, w, b, k, r, vT, wa, ba, ka, exchange=()):
    T = a.shape[0]
    C, LW = SCAN_CHUNK, SCAN_LANES
    nC = T // C
    nx = len(exchange)

    def body(*refs):
        a_ref, w_ref, b_ref, k_ref, r_ref, vT_ref, wa_ref, ba_ref, ka_ref = refs[:9]
        x_refs, refs = refs[9:9 + nx], refs[9 + nx:]
        yT_ref, S_ref, saT_ref = refs[:3]
        land_refs, refs = refs[3:3 + nx], refs[3 + nx:]
        st_ref, vb0_ref, vb1_ref, seg_ref = refs[:4]
        if nx:
            start, wait = _exchange_ops([c for _, c in exchange], x_refs, land_refs, *refs[4:])

        @pl.when(pl.program_id(0) == 0)
        def _():
            st_ref[...] = jnp.zeros_like(st_ref)
            seg_ref[...] = _seg_mat(LW)
            if nx:
                start()

        seg = seg_ref[...]
        lane = jnp.bitwise_and(lax.broadcasted_iota(jnp.int32, (1, LW), 1), HEAD - 1)
        groups = _lane_groups()

        def vsel(t, gsl):
            return jnp.where(lane == t, vT_ref[0, :, gsl], 0.0)

        first = _segb_stack([(vsel(s, gsl), 1) for gsl in groups for s in (0, 1)], seg)
        for g, gsl in enumerate(groups):
            vb0_ref[:, gsl] = first[2 * g]
            vb1_ref[:, gsl] = first[2 * g + 1]
        saT_ref[...] = jnp.zeros_like(saT_ref)

        def pair(i, yacc):
            t = 2 * i
            t1 = t + 1
            tp = jnp.maximum(t - 1, 0)
            row = lambda ref, s, gsl: ref[pl.ds(s, 1), gsl]
            Sps = [st_ref[:, gsl] for gsl in groups]
            chain = _segb_stack([(Sp * row(ref, t, gsl), 2) for gsl, Sp in zip(groups, Sps) for ref in (a_ref, wa_ref)],
                                seg)
            sas, us = chain[0::2], chain[1::2]
            S1s = []
            for gsl, Sp, sa, u in zip(groups, Sps, sas, us):
                vb0, vb1 = vb0_ref[:, gsl], vb1_ref[:, gsl]
                S1 = Sp * row(w_ref, t, gsl) + sa * row(b_ref, t, gsl) + vb0 * row(k_ref, t, gsl)
                sa1 = u + sa * row(ba_ref, t, gsl) + vb0 * row(ka_ref, t, gsl)
                st_ref[:, gsl] = S1 * row(w_ref, t1, gsl) + sa1 * row(b_ref, t1, gsl) + vb1 * row(k_ref, t1, gsl)
                S_ref[0, t, :, gsl] = Sp
                S_ref[0, t1, :, gsl] = S1
                S1s.append(S1)
                saT_ref[0, :, gsl] = jnp.where(lane == t, sa, jnp.where(lane == t1, sa1, saT_ref[0, :, gsl]))
            side = _segb_stack([(x, 1) for gsl, Sp, S1 in zip(groups, Sps, S1s)
                                for x in (Sp * row(r_ref, tp, gsl), S1 * row(r_ref, t, gsl),
                                          vsel(t + 2, gsl), vsel(t + 3, gsl))], seg)
            out = []
            for g, (gsl, ya) in enumerate(zip(groups, yacc)):
                yb0, yb1, vb0_ref[:, gsl], vb1_ref[:, gsl] = side[4 * g:4 * g + 4]
                out.append(jnp.where(lane == t, yb1, jnp.where(lane == t - 1, yb0, ya)))
            return tuple(out)

        yacc = _unrolled_loop(C // 2, pair, tuple(jnp.zeros((HEAD, LW), F32) for _ in groups), SCAN_UNROLL_FWD)
        for gsl, ya in zip(groups, yacc):
            S_last = st_ref[:, gsl]
            S_ref[0, C, :, gsl] = S_last
            yb = _segb1(S_last * r_ref[pl.ds(C - 1, 1), gsl], seg)
            yT_ref[0, :, gsl] = jnp.where(lane == C - 1, yb, ya)

        if nx:
            @pl.when(pl.program_id(0) == nC - 1)
            def _():
                wait()

    row = pl.BlockSpec((C, RW_WIDTH), lambda c: (c, 0))
    ht = pl.BlockSpec((1, HEAD, RW_WIDTH), lambda c: (c, 0, 0))
    hbm = pl.BlockSpec(memory_space=pl.ANY)
    res = pl.pallas_call(
        body, name="rwkv_scan_fwd", grid=(nC,),
        in_specs=[row, row, row, row, row, ht, row, row, row] + [hbm] * nx,
        out_specs=[ht, pl.BlockSpec((1, C + 1, HEAD, RW_WIDTH), lambda c: (c, 0, 0, 0)), ht] + [hbm] * nx,
        out_shape=[jax.ShapeDtypeStruct((nC, HEAD, RW_WIDTH), F32),
                   jax.ShapeDtypeStruct((nC, C + 1, HEAD, RW_WIDTH), F32),
                   jax.ShapeDtypeStruct((nC, HEAD, RW_WIDTH), F32)] + _exchange_shapes(exchange),
        scratch_shapes=[pltpu.VMEM((HEAD, RW_WIDTH), F32)] * 3 + [pltpu.VMEM((LW, LW), BF16)]
        + (_exchange_sems(nx) if nx else []),
        compiler_params=pltpu.CompilerParams(dimension_semantics=("arbitrary",), vmem_limit_bytes=VMEM_LIMIT_BYTES,
                                             has_side_effects=bool(nx)),
    )(a, w, b, k, r, vT, wa, ba, ka, *[z for z, _ in exchange])
    return res[:3], res[3:]


def rwkv_scan_bwd(a, w, b, k, r, v, dy, S_all, saT, exchange=()):
    T = a.shape[0]
    C, LW = SCAN_CHUNK, SCAN_LANES
    nC = T // C
    nx = len(exchange)
    n_heads = RW_WIDTH // HEAD
    dyT = _to_head_time(dy).astype(BF16)
    v_rows, dy_rows = v.reshape(T, n_heads, HEAD), dy.reshape(T, n_heads, HEAD)
    sa_rows = _from_head_time(saT).reshape(T, n_heads, HEAD)

    def body(*refs):
        a_ref, w_ref, b_ref, k_ref, r_ref, vR_ref, saR_ref, dyR_ref, dyT_ref, S_ref = refs[:10]
        x_refs, refs = refs[10:10 + nx], refs[10 + nx:]
        da_ref, dw_ref, db_ref, dk_ref, dr_ref, dvT_ref = refs[:6]
        land_refs, refs = refs[6:6 + nx], refs[6 + nx:]
        ds_ref, dyb_ref, seg_ref = refs[:3]
        if nx:
            start, wait = _exchange_ops([c for _, c in exchange], x_refs, land_refs, *refs[3:])

        @pl.when(pl.program_id(0) == 0)
        def _():
            ds_ref[...] = jnp.zeros_like(ds_ref)
            seg_ref[...] = _seg_mat(LW)
            if nx:
                start()

        seg = seg_ref[...]
        lane = jnp.bitwise_and(lax.broadcasted_iota(jnp.int32, (1, LW), 1), HEAD - 1)
        groups = _lane_groups()
        head_row = lax.broadcasted_iota(jnp.int32, (n_heads, LW), 0)
        lane_head = lax.shift_right_logical(lax.broadcasted_iota(jnp.int32, (n_heads, LW), 1), 6)

        def colsum(z):
            return jnp.sum(z, axis=0, keepdims=True)

        def dysel(t, gsl):
            return jnp.where(lane == t, dyT_ref[0, :, gsl], 0.0)

        for gsl, dyb in zip(groups, _segb_stack([(dysel(C - 1, gsl), 1) for gsl in groups], seg)):
            dyb_ref[:, gsl] = dyb

        def step(i, dvacc):
            t = C - 1 - i
            dybs = [dyb_ref[:, gsl] for gsl in groups]
            dSs = [ds_ref[:, gsl] + dyb * r_ref[pl.ds(t, 1), gsl] for gsl, dyb in zip(groups, dybs)]
            dsabs = _segb_stack([(dS * b_ref[pl.ds(t, 1), gsl], 2) for gsl, dS in zip(groups, dSs)], seg)
            for gsl, dS, dsab in zip(groups, dSs, dsabs):
                ds_ref[:, gsl] = dS * w_ref[pl.ds(t, 1), gsl] + dsab * a_ref[pl.ds(t, 1), gsl]
            out = []
            dy_rows = dyR_ref[t].astype(BF16)
            v_sa_rows = jnp.concatenate([vR_ref[t], saR_ref[t]], axis=0).astype(BF16)
            side = _segb_stack([(x, 1) for gsl, dS in zip(groups, dSs)
                                for x in (dS * k_ref[pl.ds(t, 1), gsl], dysel(t - 1, gsl))], seg)
            for g, (gsl, dva, dS, dsab) in enumerate(zip(groups, dvacc, dSs, dsabs)):
                dvb, dyb_ref[:, gsl] = side[2 * g:2 * g + 2]
                Sp = S_ref[0, t, :, gsl]
                own = head_row == lane_head + g * (LW // HEAD)

                def rows_in(rows, mat):
                    full = jnp.dot(rows, mat.astype(BF16), preferred_element_type=F32)
                    return [jnp.sum(jnp.where(own, full[s:s + n_heads], 0.0), axis=0, keepdims=True)
                            for s in range(0, rows.shape[0], n_heads)]

                (dr,) = rows_in(dy_rows, S_ref[0, t + 1, :, gsl])
                dk, db = rows_in(v_sa_rows, dS)
                dr_ref[pl.ds(t, 1), gsl] = dr
                dk_ref[pl.ds(t, 1), gsl] = dk
                db_ref[pl.ds(t, 1), gsl] = db
                dw_ref[pl.ds(t, 1), gsl] = colsum(dS * Sp)
                da_ref[pl.ds(t, 1), gsl] = colsum(Sp * dsab)
                out.append(jnp.where(lane == t, dvb, dva))
            return tuple(out)

        dvacc = _unrolled_loop(C, step, tuple(jnp.zeros((HEAD, LW), F32) for _ in groups), SCAN_UNROLL_BWD)
        for gsl, dva in zip(groups, dvacc):
            dvT_ref[0, :, gsl] = dva

        if nx:
            @pl.when(pl.program_id(0) == nC - 1)
            def _():
                wait()

    row = pl.BlockSpec((C, RW_WIDTH), lambda c: (nC - 1 - c, 0))
    ht = pl.BlockSpec((1, HEAD, RW_WIDTH), lambda c: (nC - 1 - c, 0, 0))
    hbm = pl.BlockSpec(memory_space=pl.ANY)
    per_head = pl.BlockSpec((C, n_heads, HEAD), lambda c: (nC - 1 - c, 0, 0))
    rows_shape = jax.ShapeDtypeStruct((T, RW_WIDTH), F32)
    res = pl.pallas_call(
        body, name="rwkv_scan_bwd", grid=(nC,),
        in_specs=[row, row, row, row, row, per_head, per_head, per_head, ht,
                  pl.BlockSpec((1, C + 1, HEAD, RW_WIDTH), lambda c: (nC - 1 - c, 0, 0, 0))] + [hbm] * nx,
        out_specs=[row, row, row, row, row, ht] + [hbm] * nx,
        out_shape=[rows_shape] * 5 + [jax.ShapeDtypeStruct((nC, HEAD, RW_WIDTH), F32)] + _exchange_shapes(exchange),
        scratch_shapes=[pltpu.VMEM((HEAD, RW_WIDTH), F32), pltpu.VMEM((HEAD, RW_WIDTH), F32),
                        pltpu.VMEM((LW, LW), BF16)] + (_exchange_sems(nx) if nx else []),
        compiler_params=pltpu.CompilerParams(dimension_semantics=("arbitrary",), vmem_limit_bytes=VMEM_LIMIT_BYTES,
                                             has_side_effects=bool(nx)),
    )(a, w, b, k, r, v_rows, sa_rows, dy_rows, dyT, S_all, *[z for z, _ in exchange])
    return res[:6], res[6:]


def _alibi_slope(h):
    return float(np.float32(2.0 ** (-8.0 * (h + 1) / ATT_HEADS)))


ATT_GROUP_HEADS = 4


def _stack_heads(x, lane_head, fill=0.0):
    return jnp.concatenate([jnp.where(lane_head == hh, x, fill) for hh in range(ATT_GROUP_HEADS)], axis=0)


def _unstack_heads(x, lane_head, L):
    out = jnp.zeros((L, x.shape[1]), F32)
    for hh in range(ATT_GROUP_HEADS):
        out = jnp.where(lane_head == hh, x[hh * L:(hh + 1) * L], out)
    return out


def _att_logits(qs, kcat, gi, d, L, n):
    qi = lax.broadcasted_iota(jnp.int32, (L, 2 * L), 0)
    kj = lax.broadcasted_iota(jnp.int32, (L, 2 * L), 1)
    steps = qi + L - kj
    valid = (steps >= 0) & (steps <= L) & ((kj >= L) | (n > 0))
    dist = (d * steps).astype(F32)
    bias = jnp.concatenate([jnp.where(valid, -_alibi_slope(gi * ATT_GROUP_HEADS + hh) * dist, NEG_BIG)
                            for hh in range(ATT_GROUP_HEADS)], axis=0)
    s = lax.dot_general(qs.astype(BF16), kcat, NT_DIMS, preferred_element_type=F32) * (HEAD ** -0.5)
    return jnp.where(bias > 0.5 * NEG_BIG, s + bias, NEG_BIG)


def att_fwd(pa, gi, T):
    window, d = ATT_GROUPS[gi]
    L = window // d
    Tj = T // d
    nb = Tj // L
    pv = pa.reshape(Tj, d * ATT_COLS)
    nblk = ATT_COLS // ATT_OUT

    def fn(pids, q, kp, kc, vp, vc):
        lane_head = lax.shift_right_logical(lax.broadcasted_iota(jnp.int32, (1, ATT_OUT), 1), 6)
        kcat = jnp.concatenate([kp, kc], axis=0).astype(BF16)
        vcat = jnp.concatenate([vp, vc], axis=0).astype(BF16)
        s = _att_logits(_stack_heads(q, lane_head), kcat, gi, d, L, pids[1])
        m = jnp.max(s, axis=-1, keepdims=True)
        p = jnp.exp(s - m)
        l = jnp.sum(p, axis=-1, keepdims=True)
        o = jnp.dot(p.astype(BF16), vcat, preferred_element_type=F32) / l
        lse = jnp.broadcast_to(m + jnp.log(l), o.shape)
        return _unstack_heads(o, lane_head, L), _unstack_heads(lse, lane_head, L)

    blk = (L, ATT_OUT)
    ins = [(pv, blk, lambda r, n: (n, r * nblk + gi)),
           (pv, blk, lambda r, n: (jnp.maximum(n - 1, 0), r * nblk + 3 + gi)),
           (pv, blk, lambda r, n: (n, r * nblk + 3 + gi)),
           (pv, blk, lambda r, n: (jnp.maximum(n - 1, 0), r * nblk + 6 + gi)),
           (pv, blk, lambda r, n: (n, r * nblk + 6 + gi))]
    out = ((Tj, d * ATT_OUT), F32, blk, lambda r, n: (n, r), None)
    o, lseb = tile_call(f"att_fwd_g{gi}", fn, (d, nb), ins, [out, out])
    return o.reshape(T, ATT_OUT), lseb.reshape(T, ATT_OUT)


def att_bwd(pa, o, lseb, do, dlseb, gi, T):
    window, d = ATT_GROUPS[gi]
    L = window // d
    Tj = T // d
    nb = Tj // L
    pv = pa.reshape(Tj, d * ATT_COLS)
    nblk = ATT_COLS // ATT_OUT
    view = lambda z: z.reshape(Tj, d * ATT_OUT)

    def body(q_ref, kp_ref, kc_ref, vp_ref, vc_ref, o_ref, l_ref, do_ref, dl_ref, dq_ref, dk_ref, dv_ref):
        n = pl.program_id(1)

        @pl.when(n == 0)
        def _():
            dk_ref[...] = jnp.zeros_like(dk_ref)
            dv_ref[...] = jnp.zeros_like(dv_ref)

        lane_head = lax.shift_right_logical(lax.broadcasted_iota(jnp.int32, (1, ATT_OUT), 1), 6)
        kcat = jnp.concatenate([kp_ref[...], kc_ref[...]], axis=0).astype(BF16)
        vcat = jnp.concatenate([vp_ref[...], vc_ref[...]], axis=0).astype(BF16)
        qs = _stack_heads(q_ref[...], lane_head)
        dos = _stack_heads(do_ref[...], lane_head)
        lse = jnp.max(_stack_heads(l_ref[...], lane_head, NEG_BIG), axis=-1, keepdims=True)
        dlse = jnp.sum(_stack_heads(dl_ref[...], lane_head), axis=-1, keepdims=True)
        delta = jnp.sum(dos * jnp.concatenate([o_ref[...]] * ATT_GROUP_HEADS, axis=0), axis=-1, keepdims=True)
        p = jnp.exp(_att_logits(qs, kcat, gi, d, L, n) - lse)
        dp = lax.dot_general(dos.astype(BF16), vcat, NT_DIMS, preferred_element_type=F32)
        ds = (p * (dp - delta + dlse)).astype(BF16)
        dq = _unstack_heads(jnp.dot(ds, kcat, preferred_element_type=F32), lane_head, L)
        dkc = lax.dot_general(ds, qs.astype(BF16), TN_DIMS, preferred_element_type=F32)
        dvc = lax.dot_general(p.astype(BF16), dos.astype(BF16), TN_DIMS, preferred_element_type=F32)
        scale = HEAD ** -0.5
        dq_ref[...] = dq * scale
        cur = pl.ds(pl.multiple_of(n * L, L), L)
        dk_ref[cur, :] += dkc[L:] * scale
        dv_ref[cur, :] += dvc[L:]

        @pl.when(n > 0)
        def _():
            prev = pl.ds(pl.multiple_of((n - 1) * L, L), L)
            dk_ref[prev, :] += dkc[:L] * scale
            dv_ref[prev, :] += dvc[:L]

    blk = pl.BlockSpec((L, ATT_OUT), lambda r, n: (n, r))
    res = pl.BlockSpec((Tj, ATT_OUT), lambda r, n: (0, r))
    qspec = lambda off, prev: pl.BlockSpec(
        (L, ATT_OUT), (lambda r, n: (jnp.maximum(n - 1, 0), r * nblk + off + gi)) if prev
        else (lambda r, n: (n, r * nblk + off + gi)))
    shape = jax.ShapeDtypeStruct((Tj, d * ATT_OUT), F32)
    dq, dk, dv = pl.pallas_call(
        body, name=f"att_bwd_g{gi}", grid=(d, nb),
        in_specs=[qspec(0, False), qspec(3, True), qspec(3, False), qspec(6, True), qspec(6, False),
                  blk, blk, blk, blk],
        out_specs=[blk, res, res],
        out_shape=[shape, shape, shape],
        compiler_params=_cparams(2),
    )(pv, pv, pv, pv, pv, view(o), view(lseb), view(do), view(dlseb))
    return dq.reshape(T, ATT_OUT), dk.reshape(T, ATT_OUT), dv.reshape(T, ATT_OUT)


FFN_TM, FFN_TC = 512, 512


def _conv3(u, prev8, cw, cb):
    return cb + cw[0:1] * u + cw[1:2] * _shift_down(u, prev8, 1) + cw[2:3] * _shift_down(u, prev8, 2)


def conv_glu_fwd(u, conv_w, conv_b):
    T = u.shape[0]
    tm, tc = FFN_TM, FFN_TC
    nj, ni = D_FF // tc, T // tm

    def fn(pids, ug, ugh, uv, uvh, cwg, cbg, cwv, cbv):
        first = pids[1] > 0
        cg = _conv3(ug, jnp.where(first, ugh, 0.0), cwg, cbg)
        cv = _conv3(uv, jnp.where(first, uvh, 0.0), cwv, cbv)
        return _gelu_tanh(cg) * cv

    halo = lambda off: (lambda j, i: (jnp.maximum(i * (tm // 8) - 1, 0), j + off))
    ins = [(u, (tm, tc), lambda j, i: (i, j)), (u, (8, tc), halo(0)),
           (u, (tm, tc), lambda j, i: (i, j + nj)), (u, (8, tc), halo(nj)),
           (conv_w, (3, tc), lambda j, i: (0, j)), (conv_b, (1, tc), lambda j, i: (0, j)),
           (conv_w, (3, tc), lambda j, i: (0, j + nj)), (conv_b, (1, tc), lambda j, i: (0, j + nj))]
    out = ((T, D_FF), BF16, (tm, tc), lambda j, i: (i, j), None)
    return tile_call("conv_glu_fwd", fn, (nj, ni), ins, [out])[0]


def conv_glu_bwd(u, conv_w, conv_b, df):
    T = u.shape[0]
    tm, tc = FFN_TM, FFN_TC
    nj, ni = D_FF // tc, T // tm

    def fn(pids, ug, ugh, uv, uvh, cwg, cbg, cwv, cbv, df_t, nxt_g, nxt_v):
        i = ni - 1 - pids[1]
        ugh = jnp.where(i > 0, ugh, 0.0)
        uvh = jnp.where(i > 0, uvh, 0.0)
        cg = _conv3(ug, ugh, cwg, cbg)
        cv = _conv3(uv, uvh, cwv, cbv)
        _, vjp = jax.vjp(lambda g_, v_: _gelu_tanh(g_) * v_, cg, cv)
        dcg, dcv = vjp(df_t.astype(F32))
        cs = lambda z: jnp.sum(z, axis=0, keepdims=True)

        @pl.when(pids[1] == 0)
        def _():
            nxt_g[...] = jnp.zeros_like(nxt_g)
            nxt_v[...] = jnp.zeros_like(nxt_v)

        outs = []
        for dc, cw, nxt_ref in ((dcg, cwg, nxt_g), (dcv, cwv, nxt_v)):
            nxt = nxt_ref[...]
            outs.append(cw[0:1] * dc + cw[1:2] * _shift_up(dc, nxt, 1) + cw[2:3] * _shift_up(dc, nxt, 2))
            nxt_ref[...] = dc[:8]
        for dc, uu, hh in ((dcg, ug, ugh), (dcv, uv, uvh)):
            outs += [cs(dc * uu), cs(dc * _shift_down(uu, hh, 1)), cs(dc * _shift_down(uu, hh, 2)), cs(dc)]
        return outs

    rows = lambda off: (lambda j, r: (ni - 1 - r, j + off))
    halo = lambda off: (lambda j, r: (jnp.maximum((ni - 1 - r) * (tm // 8) - 1, 0), j + off))
    ins = [(u, (tm, tc), rows(0)), (u, (8, tc), halo(0)),
           (u, (tm, tc), rows(nj)), (u, (8, tc), halo(nj)),
           (conv_w, (3, tc), lambda j, r: (0, j)), (conv_b, (1, tc), lambda j, r: (0, j)),
           (conv_w, (3, tc), lambda j, r: (0, j + nj)), (conv_b, (1, tc), lambda j, r: (0, j + nj)),
           (df, (tm, tc), rows(0))]
    big = ((T, D_FF), BF16, (tm, tc), rows(0), None)
    acc = ((1, D_FF), F32, (1, tc), lambda j, r: (0, j), 1)
    res = tile_call("conv_glu_bwd", fn, (nj, ni), ins, [big, big] + [acc] * 8,
                    scratch=[((8, tc), F32), ((8, tc), F32)])
    dconv_w = jnp.concatenate([jnp.concatenate([res[2 + j], res[6 + j]], axis=1) for j in range(3)], axis=0)
    dconv_b = jnp.concatenate([res[5], res[9]], axis=1)
    return res[0], res[1], dconv_w, dconv_b


def _pad_cols(w, total):
    return jnp.pad(w, ((0, 0), (0, total - w.shape[1])))


def _pad_rows(w, total):
    return jnp.pad(w, ((0, total - w.shape[0]), (0, 0)))


def _proj_pad(w):
    z = lambda n: jnp.zeros((w.shape[0], n), w.dtype)
    return jnp.concatenate([w[:, :1600], z(64), w[:, 1600:1664], z(64), w[:, 1664:1824], z(96), w[:, 1824:],
                            z(PROJ_TAIL)], axis=1)


def _proj_unpad(g):
    return jnp.concatenate([g[:, :1600], g[:, OFF_XA:OFF_XA + 64], g[:, OFF_XG:OFF_XG + 160],
                            g[:, RW_PAD:RW_PAD + ATT_COLS]], axis=1)


def _rw_unpad(g):
    return jnp.concatenate([g[:, :1600], g[:, OFF_XA:OFF_XA + 64], g[:, OFF_XG:OFF_XG + 160]], axis=1)


def rms_fwd(name, x, g, tm=256):
    T, D = x.shape
    return tile_call(name, lambda pid, x_t, g_t: _rms(x_t, g_t), (T // tm,),
                     [_rows(x, tm), _par(g)], [_row_out(T, D, BF16, tm)])[0]


def rms_bwd(name, x, g, dh, dres, with_bf16=True, tm=256):
    T, D = x.shape
    out_dtypes = (F32, BF16) if with_bf16 else (F32,)

    def fn(pid, x_t, g_t, dh_t, dres_t):
        _, vjp = jax.vjp(_rms, x_t, g_t)
        dx, dg = vjp(dh_t.astype(F32))
        return (dres_t + dx,) * len(out_dtypes) + (dg,)

    return tile_call(name, fn, (T // tm,), [_rows(x, tm), _par(g), _rows(dh, tm), _rows(dres, tm)],
                     [_row_out(T, D, dt, tm) for dt in out_dtypes] + [_acc_out(1, D)])


def local_step(x, p, target, W):
    T, D = x.shape
    G = {}

    w_in_p = W["w_in_p"]
    mu_p = _proj_pad(_pad_cols(W["rw_mu"], 4128))[:, :RW_PAD]
    w_up_p = _pad_rows(W["rw_w_up"], 128)
    a_up_p = _pad_rows(W["rw_a_up"], 128)
    g_up_p = _pad_rows(W["rw_g_up"], 256)
    r_k = W["rw_r_k"].reshape(1, RW_WIDTH)
    rw_params = [mu_p, W["rw_w0"], w_up_p, W["rw_a0"], a_up_p, g_up_p, W["rw_k_k"], W["rw_k_a"]]

    h = rms_fwd("rms_mix", x, W["g_mix"])
    proj = matmul("proj_in_rw", h, w_in_p[:, :RW_PAD])
    pa = matmul("proj_in_att", h, w_in_p[:, RW_PAD:RW_PAD + ATT_COLS])
    gp = matmul("proj_gate", h, W["w_gate"])

    tm = 256
    rw_in = (proj, (tm, RW_PAD), lambda i: (i, 0))
    rw_halo = _prev_halo(proj, tm, RW_PAD)

    def rw_pre_tile(pid, Pc, halo, *params):
        prev8 = jnp.where(pid[0] > 0, halo, 0.0)
        params = [q.astype(F32) for q in params]
        return rw_pre(Pc, _shift_down(Pc, prev8, 1), *params)

    r, decay, k2, v, avec, bvec, g = tile_call(
        "rw_pre", rw_pre_tile, (T // tm,), [rw_in, rw_halo] + [_par(q) for q in rw_params],
        [_row_out(T, RW_WIDTH, F32, tm)] * 7)

    wa, ba, ka = scan_pair_terms(avec, decay, bvec, k2)
    vT = _to_head_time(v).astype(BF16)
    (yT, S_all, saT), late_slots = rwkv_scan_fwd(avec, decay, bvec, k2, r, vT, wa, ba, ka,
                                            exchange=_late_weight_sources(W))
    y = _from_head_time(yT)
    W = dict(W, **_late_weights(late_slots))

    post_params = [W["rw_ln_g"], W["rw_ln_b"], r_k]
    ya = tile_call("rw_post", lambda pid, *t: rw_post(*t), (T // tm,),
                   [_rows(z, tm) for z in (y, r, k2, v, g)] + [_par(q) for q in post_params],
                   [_row_out(T, RW_WIDTH, BF16, tm)])[0]

    att = [att_fwd(pa, gi, T) for gi in range(3)]
    o_l = [att[0][0], att[1][0], att[2][0], att[0][1], att[1][1], att[2][1]]
    yb = tile_call("att_combine", lambda pid, *t: att_combine(*t), (T // tm,),
                   [_rows(z, tm) for z in o_l], [_row_out(T, ATT_OUT, BF16, tm)])[0]

    za = matmul("branch_a", ya, W["w_branch_a"])
    zb = matmul("branch_b", yb, W["w_branch_b"])
    merged = tile_call("merge", lambda pid, *t: merge_fn(*t), (T // tm,),
                       [_rows(gp, tm), _par(W["b_gate"]), _rows(za, tm), _rows(zb, tm)],
                       [_row_out(T, D, BF16, tm)])[0]
    x1 = matmul("mix_out", merged, W["w_out"], res=x)

    h2 = rms_fwd("rms_ffn", x1, W["g_ffn"])
    u = matmul("ffn_up", h2, W["w_up"])
    f = conv_glu_fwd(u, W["conv_w"], W["conv_b"])
    x2 = matmul("ffn_down", f, W["w_down"], res=x1)

    h3 = rms_fwd("rms_ple", x2, W["g_ple"])
    zg = matmul("ple_gate", h3, W["w_ple_gate"])
    pe = matmul("ple_embed", p, W["w_ple"])

    def tail_tile(pid, x2_t, zg_t, pe_t, gf, tgt):
        loss, vjp = jax.vjp(lambda a_, b_, c_, d_: tail_loss(a_, b_, c_, d_, tgt), x2_t, zg_t, pe_t, gf)
        dx2, dzg, dpe, dgf = vjp(jnp.ones((), F32))
        return dx2, dzg, dpe, dgf, jnp.full((1, 128), loss, F32)

    tmt = 128
    dx3, dzg, dpe, dgf, loss_acc = tile_call(
        "tail_loss", tail_tile, (T // tmt,),
        [_rows(x2, tmt), _rows(zg, tmt), _rows(pe, tmt), _par(W["g_final"]), _rows(target, tmt)],
        [_row_out(T, D, F32, tmt), _row_out(T, D, BF16, tmt), _row_out(T, D, BF16, tmt),
         _acc_out(1, D), _acc_out(1, 128)])
    loss = loss_acc[0, 0]
    G["g_final"] = dgf

    wgrad = functools.partial(matmul, mode="tn", out_dtype=GRAD_WIRE)
    G["w_ple"] = wgrad("d_w_ple", p, dpe)
    G["w_ple_gate"] = wgrad("d_w_ple_gate", h3, dzg)
    dh3 = matmul("d_h3", dzg, W["w_ple_gate"], "nt")
    dx2, dx2b, G["g_ple"] = rms_bwd("rms_ple_bwd", x2, W["g_ple"], dh3, dx3)

    G["w_down"] = wgrad("d_w_down", f, dx2b)
    df = matmul("d_f", dx2b, W["w_down"], "nt", out_dtype=BF16)
    du_g, du_v, G["conv_w"], G["conv_b"] = conv_glu_bwd(u, W["conv_w"], W["conv_b"], df)
    du = jnp.concatenate([du_g, du_v], axis=1)
    G["w_up"] = wgrad("d_w_up", h2, du)
    dh2 = matmul("d_h2", du, W["w_up"], "nt")
    dx1, dx1b, G["g_ffn"] = rms_bwd("rms_ffn_bwd", x1, W["g_ffn"], dh2, dx2)

    G["w_out"] = wgrad("d_w_out", merged, dx1b)
    dmerged = matmul("d_merged", dx1b, W["w_out"], "nt", out_dtype=BF16)

    def merge_bwd_tile(pid, gp_t, bg, za_t, zb_t, dm_t):
        _, vjp = jax.vjp(merge_fn, gp_t, bg, za_t, zb_t)
        return vjp(dm_t.astype(F32))

    dgp, G["b_gate"], dza, dzb = tile_call(
        "merge_bwd", merge_bwd_tile, (T // tm,),
        [_rows(gp, tm), _par(W["b_gate"]), _rows(za, tm), _rows(zb, tm), _rows(dmerged, tm)],
        [_row_out(T, 2 * D, BF16, tm), _acc_out(1, 2 * D), _row_out(T, D, BF16, tm), _row_out(T, D, BF16, tm)])
    G["w_branch_a"] = wgrad("d_w_branch_a", ya, dza)
    dya = matmul("d_ya", dza, W["w_branch_a"], "nt")
    G["w_branch_b"] = wgrad("d_w_branch_b", yb, dzb)
    dyb = matmul("d_yb", dzb, W["w_branch_b"], "nt")
    G["w_gate"] = wgrad("d_w_gate", h, dgp)
    dh_gate = matmul("d_h_gate", dgp, W["w_gate"], "nt")

    def comb_bwd_tile(pid, *t):
        _, vjp = jax.vjp(att_combine, *t[:6])
        return vjp(t[6])

    d_ol = tile_call("att_combine_bwd", comb_bwd_tile, (T // tm,),
                     [_rows(z, tm) for z in o_l] + [_rows(dyb, tm)],
                     [_row_out(T, ATT_OUT, F32, tm)] * 6)
    dqkv = [att_bwd(pa, att[gi][0], att[gi][1], d_ol[gi], d_ol[3 + gi], gi, T) for gi in range(3)]
    d_att = [dqkv[gi][j] for j in range(3) for gi in range(3)]

    def post_bwd_tile(pid, *t):
        _, vjp = jax.vjp(rw_post, *t[:8])
        return vjp(t[8])

    dy, dr_p, dk2_p, dv_p, dg, G["rw_ln_g"], G["rw_ln_b"], d_rk = tile_call(
        "rw_post_bwd", post_bwd_tile, (T // tm,),
        [_rows(z, tm) for z in (y, r, k2, v, g)] + [_par(q) for q in post_params] + [_rows(dya, tm)],
        [_row_out(T, RW_WIDTH, F32, tm)] * 5 + [_acc_out(1, RW_WIDTH)] * 3)
    G["rw_r_k"] = d_rk.reshape(W["rw_r_k"].shape)

    (da, dw, db, dk_s, dr_s, dvT), G["_early_parts"] = rwkv_scan_bwd(
        avec, decay, bvec, k2, r, v, dy, S_all, saT, exchange=_early_grad_sources(G))
    dv_s = _from_head_time(dvT)

    tmb = 128
    rw_in_b = (proj, (tmb, RW_PAD), lambda i: (i, 0))

    def pre_bwd_tile(pid, Pc, halo, *t):
        prev8 = jnp.where(pid[0] > 0, halo, 0.0)
        params = [q.astype(F32) for q in t[:8]]
        dr1, dr2, dw_, dk1, dk2_, dv1, dv2, da_, db_, dg_ = t[8:]
        _, vjp = jax.vjp(rw_pre, Pc, _shift_down(Pc, prev8, 1), *params)
        return vjp((dr1 + dr2, dw_, dk1 + dk2_, dv1 + dv2, da_, db_, dg_))

    cts = (dr_s, dr_p, dw, dk_s, dk2_p, dv_s, dv_p, da, db, dg)
    res = tile_call(
        "rw_pre_bwd", pre_bwd_tile, (T // tmb,),
        [rw_in_b, _prev_halo(proj, tmb, RW_PAD)] + [_par(q) for q in rw_params] + [_rows(z, tmb) for z in cts],
        [_row_out(T, RW_PAD, F32, tmb)] * 2 + [_acc_out(*q.shape) for q in rw_params])
    dPc, dPs = res[0], res[1]
    d_mu, G["rw_w0"], d_wup, G["rw_a0"], d_aup, d_gup, G["rw_k_k"], G["rw_k_a"] = res[2:]
    G["rw_mu"] = _rw_unpad(d_mu)
    G["rw_w_up"], G["rw_a_up"], G["rw_g_up"] = d_wup[:64], d_aup[:64], d_gup[:160]

    def dproj_tile(pid, dPc_t, dPs_t, nxt, *att_t):
        nxt = jnp.where(pid[0] < T // tm - 1, nxt, 0.0)
        tail = jnp.zeros((dPc_t.shape[0], PROJ_TAIL), F32)
        return jnp.concatenate([dPc_t + _shift_up(dPs_t, nxt, 1)] + list(att_t) + [tail], axis=1)

    dproj = tile_call("d_proj", dproj_tile, (T // tm,),
                      [_rows(dPc, tm), _rows(dPs, tm), _next_halo(dPs, tm, RW_PAD, T)] + [_rows(z, tm) for z in d_att],
                      [_row_out(T, PROJ_PAD, BF16, tm)])[0]
    G["w_in_p"] = wgrad("d_w_in", h, dproj)
    dh = matmul("d_h", dproj, w_in_p, "nt", res=dh_gate)
    dx, G["g_mix"] = rms_bwd("rms_mix_bwd", x, W["g_mix"], dh, dx1, with_bf16=False)
    return loss, dx, G


def _mesh_pos():
    return lax.axis_index("x"), lax.axis_index("y"), lax.axis_index("c")


def _peer(pos, k):
    x, y, c = pos
    px = 1 - x if k & 4 else x
    py = 1 - y if k & 2 else y
    pc = 1 - c if k & 1 else c
    return (px, py, pc), 4 * px + 2 * py + pc


def all_gather_blocks(name, blocks):
    n = len(blocks)

    def body(*refs):
        x_refs, out_refs = refs[:n], refs[n:2 * n]
        send_sems, recv_sems, local_sems = refs[2 * n:]
        x, y, c = _mesh_pos()
        me, sibling = (x, y, c), (x, y, 1 - c)
        chips = [(1 - x, y), (x, 1 - y), (1 - x, 1 - y)]
        ops = range(n)

        def slot(i, px, py, pc):
            return out_refs[i].at[4 * px + 2 * py + pc]

        def copy(k, i, block, to, own=False):
            return pltpu.make_async_remote_copy(
                src_ref=x_refs[i] if own else slot(i, *block), dst_ref=slot(i, *block),
                send_sem=send_sems.at[k, i], recv_sem=recv_sems.at[k, i],
                device_id=to, device_id_type=pl.DeviceIdType.MESH)

        mine = [pltpu.make_async_copy(x_refs[i], slot(i, *me), local_sems.at[i]) for i in ops]
        first = [copy(0, i, me, sibling, own=True) for i in ops]
        first += [copy(1 + j, i, me, (*chip, c), own=True) for j, chip in enumerate(chips) for i in ops]
        for cp in mine + first:
            cp.start()
        passed = []
        for j, chip in enumerate(chips):
            for i in ops:
                copy(1 + j, i, (*chip, c), me).wait_recv()
                passed.append(copy(4 + j, i, (*chip, c), sibling))
                passed[-1].start()
        for i in ops:
            copy(0, i, sibling, me).wait_recv()
        for j, chip in enumerate(chips):
            for i in ops:
                copy(4 + j, i, (*chip, 1 - c), me).wait_recv()
        for cp in first + passed:
            cp.wait_send()
        for cp in mine:
            cp.wait()

    return pl.pallas_call(
        body, name=name,
        in_specs=[pl.BlockSpec(memory_space=pl.ANY)] * n,
        out_specs=[pl.BlockSpec(memory_space=pl.ANY)] * n,
        out_shape=[jax.ShapeDtypeStruct((N_DEV,) + b.shape, b.dtype) for b in blocks],
        scratch_shapes=[pltpu.SemaphoreType.DMA((N_DEV - 1, n)), pltpu.SemaphoreType.DMA((N_DEV - 1, n)),
                        pltpu.SemaphoreType.DMA((n,))],
        compiler_params=pltpu.CompilerParams(has_side_effects=True),
    )(*blocks)


WHOLE = 0


def _exchange_shapes(srcs):
    shapes = [a.shape[1:] if cols is None else a.shape if cols == WHOLE else (a.shape[0], cols) for a, cols in srcs]
    return [jax.ShapeDtypeStruct((N_DEV,) + s, a.dtype) for s, (a, _) in zip(shapes, srcs)]


def _exchange_sems(n):
    return [pltpu.SemaphoreType.DMA((N_DEV - 1, n)), pltpu.SemaphoreType.DMA((N_DEV - 1, n)),
            pltpu.SemaphoreType.DMA((n,))]


def _exchange_ops(col_widths, x_refs, out_refs, send_sems, recv_sems, local_sems):
    n = len(col_widths)
    pos = _mesh_pos()
    me = 4 * pos[0] + 2 * pos[1] + pos[2]

    def piece(i, d):
        cols = col_widths[i]
        if cols is None:
            return x_refs[i].at[d]
        if cols == WHOLE:
            return x_refs[i]
        return x_refs[i].at[:, pl.ds(pl.multiple_of(d * cols, 128), cols)]

    def local(i):
        return pltpu.make_async_copy(piece(i, me), out_refs[i].at[me], local_sems.at[i])

    def remote(k, i, landing):
        peer, idx = _peer(pos, k)
        return pltpu.make_async_remote_copy(
            src_ref=piece(i, idx), dst_ref=out_refs[i].at[idx if landing else me],
            send_sem=send_sems.at[k - 1, i], recv_sem=recv_sems.at[k - 1, i],
            device_id=peer, device_id_type=pl.DeviceIdType.MESH)

    pairs = [(k, i) for k in range(1, N_DEV) for i in range(n)]

    def start():
        for i in range(n):
            local(i).start()
        for k, i in pairs:
            remote(k, i, False).start()

    def wait():
        for k, i in pairs:
            remote(k, i, True).wait_recv()
        for k, i in pairs:
            remote(k, i, False).wait_send()
        for i in range(n):
            local(i).wait()

    return start, wait


def all_to_all_blocks(name, srcs):
    n = len(srcs)

    def body(*refs):
        start, wait = _exchange_ops([c for _, c in srcs], refs[:n], refs[n:2 * n], *refs[2 * n:])
        start()
        wait()

    return pl.pallas_call(
        body, name=name,
        in_specs=[pl.BlockSpec(memory_space=pl.ANY)] * n,
        out_specs=[pl.BlockSpec(memory_space=pl.ANY)] * n,
        out_shape=_exchange_shapes(srcs),
        scratch_shapes=_exchange_sems(n),
        compiler_params=pltpu.CompilerParams(has_side_effects=True),
    )(*[a for a, _ in srcs])


def _adam_row_tile(R, C):
    best = None
    for t in range(16, R + 1, 16):
        if R % t == 0 and t * C <= ADAM_TILE_ELEMS:
            best = t
    return best if best is not None else R


def reduce_adamw(name, parts, w, m, v):
    _, R, C = parts.shape
    tr = _adam_row_tile(R, C)

    def fn(pid, parts_t, w_t, m_t, v_t):
        g = parts_t[0].astype(F32)
        for i in range(1, N_DEV):
            g = g + parts_t[i].astype(F32)
        m_n = ADAM_B1 * m_t + (1.0 - ADAM_B1) * g
        v_n = ADAM_B2 * v_t + (1.0 - ADAM_B2) * (g * g)
        m_hat = m_n / (1.0 - ADAM_B1 ** ADAM_STEP)
        v_hat = v_n / (1.0 - ADAM_B2 ** ADAM_STEP)
        delta = -ADAM_LR * (m_hat / (jnp.sqrt(v_hat) + ADAM_EPS) + ADAM_WD * w_t)
        return g, delta, m_n, v_n

    row = lambda a: (a, (tr, C), lambda i: (i, 0))
    out = ((R, C), F32, (tr, C), lambda i: (i, 0), None)
    return tile_call(name, fn, (R // tr,),
                     [(parts, (N_DEV, tr, C), lambda i: (0, i, 0)), row(w), row(m), row(v)], [out] * 4)


PARAMS = (
    ("g_mix", (1, 1024), None), ("w_in", (1024, 4128), 1), ("rw_mu", (1, 1824), None), ("rw_w0", (1, 512), None),
    ("rw_w_up", (64, 512), 1), ("rw_a0", (1, 512), None), ("rw_a_up", (64, 512), 1), ("rw_g_up", (160, 512), 1),
    ("rw_k_k", (1, 512), None), ("rw_k_a", (1, 512), None), ("rw_r_k", (8, 64), None), ("rw_ln_g", (1, 512), None),
    ("rw_ln_b", (1, 512), None), ("w_branch_a", (512, 1024), 1), ("w_branch_b", (256, 1024), 1),
    ("w_gate", (1024, 2048), 1), ("b_gate", (1, 2048), None), ("w_out", (1024, 1024), 0), ("g_ffn", (1, 1024), None),
    ("w_up", (1024, 6144), 1), ("conv_w", (3, 6144), 1), ("conv_b", (1, 6144), None), ("w_down", (3072, 1024), 0),
    ("g_ple", (1, 1024), None), ("w_ple_gate", (1024, 1024), 0), ("w_ple", (256, 1024), 1), ("g_final", (1, 1024), None),
)
SHARDED = tuple(q for q in PARAMS if q[2] is not None)
REPLICATED = tuple(q for q in PARAMS if q[2] is None)
BIG_NAMES = ("w_in", "w_up", "w_gate", "w_out", "w_down", "w_ple_gate", "w_branch_a", "w_branch_b", "w_ple")
BIG = tuple(q for q in SHARDED if q[0] in BIG_NAMES)
SMALL_SHARDED = tuple(q for q in SHARDED if q[0] not in BIG_NAMES)
PACK_COLS = 1024
F32_GATHERED = ("conv_w",)


def _local_shape(shape, axis):
    s = list(shape)
    s[axis] //= N_DEV
    return tuple(s)


def _numel(shape):
    return int(np.prod(shape))


def _pad_flat(z, mult):
    n = z.shape[-1]
    total = -(-n // mult) * mult
    return jnp.pad(z, [(0, 0)] * (z.ndim - 1) + [(0, total - n)])


def _full_from_slots(slots, shape, axis):
    loc = _local_shape(shape, axis)
    z = slots.reshape((N_DEV,) + loc)
    if axis == 0:
        return z.reshape(shape)
    return z.transpose(1, 0, 2).reshape(shape)


def _slots_from_full(full, shape, axis):
    loc = _local_shape(shape, axis)
    if axis == 0:
        return full.reshape(N_DEV, _numel(loc))
    return full.reshape(shape[0], N_DEV, loc[1]).transpose(1, 0, 2).reshape(N_DEV, _numel(loc))


W_IN_SLOT = 640
W_IN_LOCAL = 4128 // N_DEV


def _block_shape(shape, axis):
    return _local_shape(shape, axis) if axis is not None else shape


def _pad_w_in(block):
    return jnp.pad(block, ((0, 0), (0, W_IN_SLOT - W_IN_LOCAL)))


def _proj_col(s):
    return s + jnp.where(s >= 1600, 64, 0) + jnp.where(s >= 1664, 64, 0) + jnp.where(s >= 1824, 96, 0)


def _perm_tile(d, c0, width):
    j = lax.broadcasted_iota(jnp.int32, (W_IN_SLOT, width), 0)
    c = c0 + lax.broadcasted_iota(jnp.int32, (W_IN_SLOT, width), 1)
    hit = (_proj_col(d * W_IN_LOCAL + j) == c) & (j < W_IN_LOCAL)
    return jnp.where(hit, 1.0, 0.0).astype(BF16)


PERM_TILE = 768


def w_in_unshuffle(slots):
    _, K, _ = slots.shape
    tn = PERM_TILE
    reach = 3

    def first_slot(j):
        return j + jnp.where(j >= 3, 1, 0) + jnp.where(j >= 5, 1, 0)

    def body(a_ref, o_ref, acc_ref):
        j, kk = pl.program_id(0), pl.program_id(1)
        d = first_slot(j) + kk

        @pl.when(kk == 0)
        def _():
            acc_ref[...] = jnp.zeros_like(acc_ref)

        @pl.when(d < N_DEV)
        def _():
            acc_ref[...] += jnp.dot(a_ref[0], _perm_tile(d, j * tn, tn), preferred_element_type=F32)

        @pl.when(kk == reach - 1)
        def _():
            o_ref[...] = acc_ref[...].astype(o_ref.dtype)

    return pl.pallas_call(
        body, name="w_in_unshuffle", grid=(PROJ_PAD // tn, reach),
        in_specs=[pl.BlockSpec((1, K, W_IN_SLOT), lambda j, kk: (jnp.minimum(first_slot(j) + kk, N_DEV - 1), 0, 0))],
        out_specs=pl.BlockSpec((K, tn), lambda j, kk: (0, j)),
        out_shape=jax.ShapeDtypeStruct((K, PROJ_PAD), BF16),
        scratch_shapes=[pltpu.VMEM((K, tn), F32)],
        compiler_params=_cparams(2),
    )(slots)


def w_in_shuffle_grad(dw):
    K = dw.shape[0]
    tk = PERM_TILE

    def first_tile(d):
        return _proj_col(d * W_IN_LOCAL) // tk

    def body(g_ref, o_ref, acc_ref):
        d, kk = pl.program_id(0), pl.program_id(1)
        perm = _perm_tile(d, (first_tile(d) + kk) * tk, tk)
        part = lax.dot_general(g_ref[...].astype(BF16), perm, NT_DIMS, preferred_element_type=F32)

        @pl.when(kk == 0)
        def _():
            acc_ref[...] = part

        @pl.when(kk == 1)
        def _():
            o_ref[0] = (acc_ref[...] + part).astype(o_ref.dtype)

    return pl.pallas_call(
        body, name="w_in_shuffle_grad", grid=(N_DEV, 2),
        in_specs=[pl.BlockSpec((K, tk), lambda d, kk: (0, first_tile(d) + kk))],
        out_specs=pl.BlockSpec((1, K, W_IN_SLOT), lambda d, kk: (d, 0, 0)),
        out_shape=jax.ShapeDtypeStruct((N_DEV, K, W_IN_SLOT), GRAD_WIRE),
        scratch_shapes=[pltpu.VMEM((K, W_IN_SLOT), F32)],
        compiler_params=_cparams(2),
    )(dw)


def _flat_rows(pieces, dtype, row_mult):
    flat = jnp.concatenate([z.astype(dtype) for z in pieces], axis=-1)
    flat = _pad_flat(flat, row_mult * PACK_COLS)
    return flat.reshape(flat.shape[:-1] + (-1, PACK_COLS))


FIRST = tuple(q for q in BIG if q[0] in ("w_in", "w_gate"))
LATE = tuple(q for q in BIG if q not in FIRST)


def _matrix_from_slots(slots, shape, axis):
    return slots.reshape(shape) if axis == 0 else slots.transpose(1, 0, 2).reshape(shape)


def _late_weight_sources(W):
    return [(blk, WHOLE) for blk in W["_late_blocks"]]


def _late_weights(slots):
    return {n: _matrix_from_slots(s, shape, axis) for (n, shape, axis), s in zip(LATE, slots)}


def gather_weights(local):
    blocks = [(_pad_w_in(local[n]) if n == "w_in" else local[n]).astype(BF16) for n, _, _ in FIRST]
    small = [q for q in SMALL_SHARDED if q[0] not in F32_GATHERED]
    exact = [q for q in SMALL_SHARDED if q[0] in F32_GATHERED]
    blocks.append(_flat_rows([local[n].reshape(-1) for n, _, _ in small], BF16, 16))
    blocks.append(_flat_rows([local[n].reshape(-1) for n, _, _ in exact], F32, 8))
    got = all_gather_blocks("weight_all_gather", blocks)
    full = {"_late_blocks": [local[n].astype(BF16) for n, _, _ in LATE]}
    for (n, shape, axis), slots in zip(FIRST, got):
        if n == "w_in":
            full["w_in_p"] = w_in_unshuffle(slots)
        else:
            full[n] = _matrix_from_slots(slots, shape, axis)
    for group, slots in ((small, got[-2]), (exact, got[-1])):
        slots, off = slots.reshape(N_DEV, -1), 0
        for n, shape, axis in group:
            size = _numel(_local_shape(shape, axis))
            full[n] = _full_from_slots(slots[:, off:off + size], shape, axis)
            off += size
    for n, _, _ in REPLICATED:
        full[n] = local[n]
    return full


LOSS_SLOT = ("_loss", (1, 2), None)
PACKED_SMALL = SMALL_SHARDED + REPLICATED + (LOSS_SLOT,)


def _pack_small(vals):
    pieces = [vals[n].reshape(-1) if n in vals else jnp.zeros((_numel(shape),), F32) for n, shape, _ in PACKED_SMALL]
    return _flat_rows(pieces, F32, 16)


def _unpack_small(packed):
    flat, out, off = packed.reshape(-1), {}, 0
    for n, shape, axis in PACKED_SMALL:
        loc = _block_shape(shape, axis)
        out[n] = flat[off:off + _numel(loc)].reshape(loc)
        off += _numel(loc)
    return out


EARLY = tuple(q for q in BIG if q[0] != "w_in")


def _early_grad_sources(G):
    srcs = []
    for n, shape, axis in EARLY:
        if axis == 0:
            srcs.append((G[n].astype(GRAD_WIRE).reshape((N_DEV,) + _local_shape(shape, axis)), None))
        else:
            srcs.append((G[n].astype(GRAD_WIRE), shape[1] // N_DEV))
    return srcs


def _late_grad_sources(G, loss_local):
    srcs = [(w_in_shuffle_grad(G["w_in_p"]), None)]
    rows = [_slots_from_full(G[n].reshape(shape), shape, axis) for n, shape, axis in SMALL_SHARDED]
    loss_hi = loss_local.astype(GRAD_WIRE).astype(F32)
    rep = jnp.concatenate([G[n].reshape(-1) for n, _, _ in REPLICATED] + [jnp.stack([loss_hi, loss_local - loss_hi])])
    rows.append(jnp.broadcast_to(rep[None, :], (N_DEV, rep.shape[0])))
    srcs.append((_flat_rows(rows, GRAD_WIRE, 16), None))
    return srcs


def _step(x, p, target, local_w, local_m, local_v):
    full = gather_weights(local_w)
    loss_local, dx, G = local_step(x, p, target, full)
    late = all_to_all_blocks("grad_all_to_all", _late_grad_sources(G, loss_local))
    parts = [late[0]] + list(G["_early_parts"]) + [late[1]]
    outs = [{}, {}, {}, {}]
    for (n, shape, axis), part in zip((BIG[0],) + EARLY, parts):
        prep = _pad_w_in if n == "w_in" else (lambda z: z)
        res = reduce_adamw("adamw_" + n, part, prep(local_w[n]), prep(local_m[n]), prep(local_v[n]))
        for o, z in zip(outs, res):
            o[n] = z[:, :W_IN_LOCAL] if n == "w_in" else z
    res = reduce_adamw("adamw_small", parts[-1], _pack_small(local_w), _pack_small(local_m), _pack_small(local_v))
    for o, z in zip(outs, res):
        o.update(_unpack_small(z))
    loss = jnp.sum(outs[0]["_loss"])
    return loss, dx, outs


def kernel(x, p, g_mix, w_in, rw_mu, rw_w0, rw_w_up, rw_a0, rw_a_up, rw_g_up, rw_k_k, rw_k_a, rw_r_k, rw_ln_g, rw_ln_b, w_branch_a, w_branch_b, w_gate, b_gate, w_out, g_ffn, w_up, conv_w, conv_b, w_down, g_ple, w_ple_gate, w_ple, g_final, loss_target, m_g_mix, m_w_in, m_rw_mu, m_rw_w0, m_rw_w_up, m_rw_a0, m_rw_a_up, m_rw_g_up, m_rw_k_k, m_rw_k_a, m_rw_r_k, m_rw_ln_g, m_rw_ln_b, m_w_branch_a, m_w_branch_b, m_w_gate, m_b_gate, m_w_out, m_g_ffn, m_w_up, m_conv_w, m_conv_b, m_w_down, m_g_ple, m_w_ple_gate, m_w_ple, m_g_final, v_g_mix, v_w_in, v_rw_mu, v_rw_w0, v_rw_w_up, v_rw_a0, v_rw_a_up, v_rw_g_up, v_rw_k_k, v_rw_k_a, v_rw_r_k, v_rw_ln_g, v_rw_ln_b, v_w_branch_a, v_w_branch_b, v_w_gate, v_b_gate, v_w_out, v_g_ffn, v_w_up, v_conv_w, v_conv_b, v_w_down, v_g_ple, v_w_ple_gate, v_w_ple, v_g_final):
    args = dict(locals())
    names = [n for n, _, _ in PARAMS]
    orig_shape = {n: args[n].shape for n in names}

    def strip(prefix):
        out = {}
        for n, shape, axis in PARAMS:
            a = args[prefix + n]
            loc = _local_shape(shape, axis) if axis is not None else shape
            out[n] = a.reshape(loc)
        return out

    local_w, local_m, local_v = strip(""), strip("m_"), strip("v_")
    T, D = x.shape[-2], x.shape[-1]
    loss, dx, (g, delta, m_n, v_n) = _step(x.reshape(T, D), p.reshape(T, p.shape[-1]), loss_target.reshape(T, D),
                                           local_w, local_m, local_v)
    outs = [loss, dx.reshape(x.shape)]
    for group in (g, delta, m_n, v_n):
        outs += [group[n].reshape(orig_shape[n]) for n in names]
    return tuple(outs)
```

```python
import functools
import math

import numpy as np
import jax
import jax.numpy as jnp
from jax import lax
from jax.experimental import pallas as pl
from jax.experimental.pallas import tpu as pltpu

F32 = jnp.float32
BF16 = jnp.bfloat16
GRAD_WIRE = jnp.bfloat16

N_DEV = 8
NORM_EPS = 1e-6
RW_LN_EPS = 64e-5
HEAD = 64
RW_WIDTH = 512
ATT_GROUPS = ((128, 1), (512, 4), (2048, 16))
ATT_HEADS = 12
ATT_OUT = 256
ATT_COLS = 2304
OFF_XW, OFF_XA, OFF_XG, RW_PAD, PROJ_PAD = 1536, 1664, 1792, 2048, 4608
PROJ_TAIL = PROJ_PAD - RW_PAD - ATT_COLS
D_FF = 3072

ADAM_LR, ADAM_B1, ADAM_B2, ADAM_EPS, ADAM_WD, ADAM_STEP = 0.001, 0.9, 0.999, 1e-08, 0.01, 10

VMEM_LIMIT_BYTES = 56 * 1024 * 1024
ADAM_TILE_ELEMS = 256 * 1024
NEG_BIG = -1e30

NT_DIMS = (((1,), (1,)), ((), ()))
TN_DIMS = (((0,), (0,)), ((), ()))
NN_DIMS = (((1,), (0,)), ((), ()))


def _cparams(n_axes):
    return pltpu.CompilerParams(dimension_semantics=("arbitrary",) * n_axes,
                                vmem_limit_bytes=VMEM_LIMIT_BYTES)


def _split2(x):
    hi = x.astype(BF16)
    lo = (x - hi.astype(F32)).astype(BF16)
    return hi, lo


def _seg_mat(n):
    r = lax.shift_right_logical(lax.broadcasted_iota(jnp.int32, (n, n), 0), 6)
    c = lax.shift_right_logical(lax.broadcasted_iota(jnp.int32, (n, n), 1), 6)
    return jnp.where(r == c, 1.0, 0.0).astype(BF16)


def _segb(x, seg):
    return _segb_stack([(x, 2)], seg)[0]


def _segb_stack(items, seg):
    rows = items[0][0].shape[0]
    parts = []
    for x, passes in items:
        parts += list(_split2(x)) if passes == 2 else [x.astype(BF16)]
    res = jnp.dot(jnp.concatenate(parts, axis=0), seg, preferred_element_type=F32)
    out, at = [], 0
    for _, passes in items:
        piece = res[at * rows:(at + 1) * rows]
        if passes == 2:
            piece = piece + res[(at + 1) * rows:(at + 2) * rows]
        out.append(piece)
        at += passes
    return out


def _segb1(x, seg):
    return jnp.dot(x.astype(BF16), seg, preferred_element_type=F32)


@jax.custom_vjp
def segsum(x):
    return _segb(x, _seg_mat(x.shape[1]))


def _segsum_fwd(x):
    return segsum(x), None


def _segsum_bwd(_, ct):
    return (segsum(ct),)


segsum.defvjp(_segsum_fwd, _segsum_bwd)


@jax.custom_vjp
def bdot(a, b):
    return jnp.dot(a.astype(BF16), b.astype(BF16), preferred_element_type=F32)


def _bdot_fwd(a, b):
    return bdot(a, b), (a, b)


def _bdot_bwd(res, ct):
    a, b = res
    ctb = ct.astype(BF16)
    da = lax.dot_general(ctb, b.astype(BF16), NT_DIMS, preferred_element_type=F32)
    db = lax.dot_general(a.astype(BF16), ctb, TN_DIMS, preferred_element_type=F32)
    return da.astype(a.dtype), db.astype(b.dtype)


bdot.defvjp(_bdot_fwd, _bdot_bwd)


def _sig(x):
    return 1.0 / (1.0 + jnp.exp(-x))


def _softplus(z):
    return jnp.maximum(z, 0.0) + jnp.log(1.0 + jnp.exp(-jnp.abs(z)))


def _gelu_tanh(x):
    return 0.5 * x * (1.0 + jnp.tanh(0.7978845608028654 * (x + 0.044715 * (x * x * x))))


def _rms(x, g):
    return x * lax.rsqrt(jnp.mean(x * x, axis=-1, keepdims=True) + NORM_EPS) * g


def _shift_down(x, prev8, n):
    rolled = pltpu.roll(x, n, 0)
    top = pltpu.roll(prev8, n, 0)
    rid = lax.broadcasted_iota(jnp.int32, (8, x.shape[1]), 0)
    head = jnp.where(rid < n, top, rolled[:8])
    return jnp.concatenate([head, rolled[8:]], axis=0)


def _shift_up(x, next8, n):
    rows = x.shape[0]
    rolled = pltpu.roll(x, rows - n, 0)
    bottom = pltpu.roll(next8, 8 - n, 0)
    rid = lax.broadcasted_iota(jnp.int32, (8, x.shape[1]), 0)
    tail = jnp.where(rid >= 8 - n, bottom, rolled[rows - 8:])
    return jnp.concatenate([rolled[:rows - 8], tail], axis=0)


def tile_call(name, fn, grid, ins, outs, scratch=()):
    n_in, n_out = len(ins), len(outs)
    acc_axes = [o[4] for o in outs]

    def body(*refs):
        pids = tuple(pl.program_id(a) for a in range(len(grid)))
        vals = fn(pids, *[r[...] for r in refs[:n_in]], *refs[n_in + n_out:])
        if not isinstance(vals, (tuple, list)):
            vals = (vals,)
        for o_ref, val, ax in zip(refs[n_in:n_in + n_out], vals, acc_axes):
            if ax is None:
                o_ref[...] = val.astype(o_ref.dtype)
            else:
                @pl.when(pids[ax] == 0)
                def _(o_ref=o_ref):
                    o_ref[...] = jnp.zeros_like(o_ref)

                o_ref[...] += val.astype(o_ref.dtype)

    res = pl.pallas_call(
        body, name=name, grid=grid,
        in_specs=[pl.BlockSpec(b, im) for _, b, im in ins],
        out_specs=[pl.BlockSpec(o[2], o[3]) for o in outs],
        out_shape=[jax.ShapeDtypeStruct(o[0], o[1]) for o in outs],
        scratch_shapes=[pltpu.VMEM(s, d) for s, d in scratch],
        compiler_params=_cparams(len(grid)),
    )(*[a for a, _, _ in ins])
    return res


def _rows(a, tm):
    return (a, (tm, a.shape[1]), lambda i: (i, 0))


def _par(a):
    return (a, a.shape, lambda i: (0, 0))


def _row_out(T, C, dtype, tm):
    return ((T, C), dtype, (tm, C), lambda i: (i, 0), None)


def _acc_out(R, C):
    return ((R, C), F32, (R, C), lambda i: (0, 0), 0)


def _prev_halo(a, tm, C):
    return (a, (8, C), lambda i: (jnp.maximum(i * (tm // 8) - 1, 0), 0))


def _next_halo(a, tm, C, T):
    return (a, (8, C), lambda i: (jnp.minimum((i + 1) * (tm // 8), T // 8 - 1), 0))


def _pick(n, target):
    for t in (target, 2048, 1536, 1024, 768, 512, 384, 256, 128):
        if t <= target and n % t == 0:
            return t
    return n


def matmul(name, a, b, mode="nn", res=None, out_dtype=F32, tm=1024, tn=2048, tk=1024):
    if mode == "nn":
        (M, K), (K2, N) = a.shape, b.shape
    elif mode == "tn":
        (K, M), (K2, N) = a.shape, b.shape
    else:
        (M, K), (N, K2) = a.shape, b.shape
    assert K == K2, (name, a.shape, b.shape, mode)
    tm, tn, tk = _pick(M, tm), _pick(N, tn), _pick(K, tk)
    nk = K // tk
    dims = {"nn": NN_DIMS, "tn": TN_DIMS, "nt": NT_DIMS}[mode]
    a_spec = {"nn": pl.BlockSpec((tm, tk), lambda i, j, k: (i, k)),
              "tn": pl.BlockSpec((tk, tm), lambda i, j, k: (k, i)),
              "nt": pl.BlockSpec((tm, tk), lambda i, j, k: (i, k))}[mode]
    b_spec = {"nn": pl.BlockSpec((tk, tn), lambda i, j, k: (k, j)),
              "tn": pl.BlockSpec((tk, tn), lambda i, j, k: (k, j)),
              "nt": pl.BlockSpec((tn, tk), lambda i, j, k: (j, k))}[mode]
    has_res = res is not None

    def body(*refs):
        if has_res:
            a_ref, b_ref, r_ref, o_ref, acc_ref = refs
        else:
            a_ref, b_ref, o_ref, acc_ref = refs
        k = pl.program_id(2)

        @pl.when(k == 0)
        def _():
            acc_ref[...] = jnp.zeros_like(acc_ref)

        acc_ref[...] += lax.dot_general(a_ref[...].astype(BF16), b_ref[...].astype(BF16), dims,
                                        preferred_element_type=F32)

        @pl.when(k == nk - 1)
        def _():
            out = acc_ref[...]
            if has_res:
                out = out + r_ref[...].astype(F32)
            o_ref[...] = out.astype(o_ref.dtype)

    in_specs = [a_spec, b_spec]
    args = [a, b]
    if has_res:
        in_specs.append(pl.BlockSpec((tm, tn), lambda i, j, k: (i, j)))
        args.append(res)
    return pl.pallas_call(
        body, name=name, grid=(M // tm, N // tn, nk),
        in_specs=in_specs,
        out_specs=pl.BlockSpec((tm, tn), lambda i, j, k: (i, j)),
        out_shape=jax.ShapeDtypeStruct((M, N), out_dtype),
        scratch_shapes=[pltpu.VMEM((tm, tn), F32)],
        compiler_params=_cparams(3),
    )(*args)


def rw_pre(Pc, Ps, mu, w0, w_up, a0, a_up, g_up, k_k, k_a):
    Pm = Pc + (Ps - Pc) * mu
    r, k, v = Pm[:, 0:512], Pm[:, 512:1024], Pm[:, 1024:1536]
    xw, xa, xg = Pm[:, OFF_XW:OFF_XA], Pm[:, OFF_XA:OFF_XG], Pm[:, OFF_XG:RW_PAD]
    w = -_softplus(-(w0 + bdot(jnp.tanh(xw), w_up))) - 0.5
    decay = jnp.exp(-jnp.exp(w))
    a = _sig(a0 + bdot(xa, a_up))
    g = bdot(_sig(xg), g_up)
    kk = k * k_k
    kk = kk / jnp.maximum(jnp.sqrt(segsum(kk * kk)), 1e-12)
    k2 = k * (1.0 + (a - 1.0) * k_a)
    return r, decay, k2, v, -kk, kk * a, g


def rw_post(y, r, k2, v, g, ln_g, ln_b, r_k):
    mean = segsum(y) * (1.0 / HEAD)
    d = y - mean
    var = segsum(d * d) * (1.0 / HEAD)
    yn = d * lax.rsqrt(var + RW_LN_EPS) * ln_g + ln_b
    bonus = segsum(r * k2 * r_k) * v
    return (yn + bonus) * g


def att_combine(o1, o2, o3, l1, l2, l3):
    m = jnp.maximum(jnp.maximum(l1, l2), l3)
    e1, e2, e3 = jnp.exp(l1 - m), jnp.exp(l2 - m), jnp.exp(l3 - m)
    return (e1 * o1 + e2 * o2 + e3 * o3) / (e1 + e2 + e3)


def merge_fn(gp, bg, za, zb):
    s = _sig(gp + bg)
    half = za.shape[1]
    return s[:, :half] * za + s[:, half:] * zb


def tail_loss(x2, zg, pe, g_final, target):
    x3 = x2 + _sig(zg) * pe
    y = _rms(x3, g_final)
    err = (y - target) * (y - target)
    return 0.5 * jnp.sum(jnp.mean(err, axis=-1, keepdims=True))


SCAN_CHUNK = HEAD
SCAN_LANES = 256
SCAN_UNROLL_FWD, SCAN_UNROLL_BWD = 8, 8


def _to_head_time(z):
    T = z.shape[0]
    return z.reshape(T // HEAD, HEAD, RW_WIDTH // HEAD, HEAD).transpose(0, 3, 2, 1).reshape(T // HEAD, HEAD, RW_WIDTH)


def _from_head_time(zt):
    C = zt.shape[0]
    return zt.reshape(C, HEAD, RW_WIDTH // HEAD, HEAD).transpose(0, 3, 2, 1).reshape(C * HEAD, RW_WIDTH)


def _unrolled_loop(n, step, init, unroll):
    def body(i, carry):
        for j in range(unroll):
            carry = step(i * unroll + j, carry)
        return carry

    return lax.fori_loop(0, n // unroll, body, init)


def _lane_groups():
    return [slice(j * SCAN_LANES, (j + 1) * SCAN_LANES) for j in range(RW_WIDTH // SCAN_LANES)]


def scan_pair_terms(a, w, b, k, tm=256):
    T = a.shape[0]

    def fn(pid, a_t, nxt, w_t, b_t, k_t):
        a_next = _shift_up(a_t, jnp.where(pid[0] < T // tm - 1, nxt, 0.0), 1)
        return w_t * a_next, segsum(b_t * a_next), segsum(k_t * a_next)

    return tile_call("scan_pair_terms", fn, (T // tm,),
                     [_rows(a, tm), _next_halo(a, tm, RW_WIDTH, T), _rows(w, tm), _rows(b, tm), _rows(k, tm)],
                     [_row_out(T, RW_WIDTH, F32, tm)] * 3)


def rwkv_scan_fwd(a, w, b, k, r, vT, wa, ba, ka, exchange=()):
    T = a.shape[0]
    C, LW = SCAN_CHUNK, SCAN_LANES
    nC = T // C
    nx = len(exchange)

    def body(*refs):
        a_ref, w_ref, b_ref, k_ref, r_ref, vT_ref, wa_ref, ba_ref, ka_ref = refs[:9]
        x_refs, refs = refs[9:9 + nx], refs[9 + nx:]
        yT_ref, S_ref, saT_ref = refs[:3]
        land_refs, refs = refs[3:3 + nx], refs[3 + nx:]
        st_ref, vb0_ref, vb1_ref, seg_ref = refs[:4]
        if nx:
            start, wait = _exchange_ops([c for _, c in exchange], x_refs, land_refs, *refs[4:])

        @pl.when(pl.program_id(0) == 0)
        def _():
            st_ref[...] = jnp.zeros_like(st_ref)
            seg_ref[...] = _seg_mat(LW)
            if nx:
                start()

        seg = seg_ref[...]
        lane = jnp.bitwise_and(lax.broadcasted_iota(jnp.int32, (1, LW), 1), HEAD - 1)
        groups = _lane_groups()

        def vsel(t, gsl):
            return jnp.where(lane == t, vT_ref[0, :, gsl], 0.0)

        first = _segb_stack([(vsel(s, gsl), 1) for gsl in groups for s in (0, 1)], seg)
        for g, gsl in enumerate(groups):
            vb0_ref[:, gsl] = first[2 * g]
            vb1_ref[:, gsl] = first[2 * g + 1]
        saT_ref[...] = jnp.zeros_like(saT_ref)

        def pair(i, yacc):
            t = 2 * i
            t1 = t + 1
            tp = jnp.maximum(t - 1, 0)
            row = lambda ref, s, gsl: ref[pl.ds(s, 1), gsl]
            Sps = [st_ref[:, gsl] for gsl in groups]
            chain = _segb_stack([(Sp * row(ref, t, gsl), 2) for gsl, Sp in zip(groups, Sps) for ref in (a_ref, wa_ref)],
                                seg)
            sas, us = chain[0::2], chain[1::2]
            S1s = []
            for gsl, Sp, sa, u in zip(groups, Sps, sas, us):
                vb0, vb1 = vb0_ref[:, gsl], vb1_ref[:, gsl]
                S1 = Sp * row(w_ref, t, gsl) + sa * row(b_ref, t, gsl) + vb0 * row(k_ref, t, gsl)
                sa1 = u + sa * row(ba_ref, t, gsl) + vb0 * row(ka_ref, t, gsl)
                st_ref[:, gsl] = S1 * row(w_ref, t1, gsl) + sa1 * row(b_ref, t1, gsl) + vb1 * row(k_ref, t1, gsl)
                S_ref[0, t, :, gsl] = Sp
                S_ref[0, t1, :, gsl] = S1
                S1s.append(S1)
                saT_ref[0, :, gsl] = jnp.where(lane == t, sa, jnp.where(lane == t1, sa1, saT_ref[0, :, gsl]))
            side = _segb_stack([(x, 1) for gsl, Sp, S1 in zip(groups, Sps, S1s)
                                for x in (Sp * row(r_ref, tp, gsl), S1 * row(r_ref, t, gsl),
                                          vsel(t + 2, gsl), vsel(t + 3, gsl))], seg)
            out = []
            for g, (gsl, ya) in enumerate(zip(groups, yacc)):
                yb0, yb1, vb0_ref[:, gsl], vb1_ref[:, gsl] = side[4 * g:4 * g + 4]
                out.append(jnp.where(lane == t, yb1, jnp.where(lane == t - 1, yb0, ya)))
            return tuple(out)

        yacc = _unrolled_loop(C // 2, pair, tuple(jnp.zeros((HEAD, LW), F32) for _ in groups), SCAN_UNROLL_FWD)
        for gsl, ya in zip(groups, yacc):
            S_last = st_ref[:, gsl]
            S_ref[0, C, :, gsl] = S_last
            yb = _segb1(S_last * r_ref[pl.ds(C - 1, 1), gsl], seg)
            yT_ref[0, :, gsl] = jnp.where(lane == C - 1, yb, ya)

        if nx:
            @pl.when(pl.program_id(0) == nC - 1)
            def _():
                wait()

    row = pl.BlockSpec((C, RW_WIDTH), lambda c: (c, 0))
    ht = pl.BlockSpec((1, HEAD, RW_WIDTH), lambda c: (c, 0, 0))
    hbm = pl.BlockSpec(memory_space=pl.ANY)
    res = pl.pallas_call(
        body, name="rwkv_scan_fwd", grid=(nC,),
        in_specs=[row, row, row, row, row, ht, row, row, row] + [hbm] * nx,
        out_specs=[ht, pl.BlockSpec((1, C + 1, HEAD, RW_WIDTH), lambda c: (c, 0, 0, 0)), ht] + [hbm] * nx,
        out_shape=[jax.ShapeDtypeStruct((nC, HEAD, RW_WIDTH), F32),
                   jax.ShapeDtypeStruct((nC, C + 1, HEAD, RW_WIDTH), F32),
                   jax.ShapeDtypeStruct((nC, HEAD, RW_WIDTH), F32)] + _exchange_shapes(exchange),
        scratch_shapes=[pltpu.VMEM((HEAD, RW_WIDTH), F32)] * 3 + [pltpu.VMEM((LW, LW), BF16)]
        + (_exchange_sems(nx) if nx else []),
        compiler_params=pltpu.CompilerParams(dimension_semantics=("arbitrary",), vmem_limit_bytes=VMEM_LIMIT_BYTES,
                                             has_side_effects=bool(nx)),
    )(a, w, b, k, r, vT, wa, ba, ka, *[z for z, _ in exchange])
    return res[:3], res[3:]


def rwkv_scan_bwd(a, w, b, k, r, v, dy, S_all, saT, exchange=()):
    T = a.shape[0]
    C, LW = SCAN_CHUNK, SCAN_LANES
    nC = T // C
    nx = len(exchange)
    n_heads = RW_WIDTH // HEAD
    dyT = _to_head_time(dy).astype(BF16)
    v_rows, dy_rows = v.reshape(T, n_heads, HEAD), dy.reshape(T, n_heads, HEAD)
    sa_rows = _from_head_time(saT).reshape(T, n_heads, HEAD)

    def body(*refs):
        a_ref, w_ref, b_ref, k_ref, r_ref, vR_ref, saR_ref, dyR_ref, dyT_ref, S_ref = refs[:10]
        x_refs, refs = refs[10:10 + nx], refs[10 + nx:]
        da_ref, dw_ref, db_ref, dk_ref, dr_ref, dvT_ref = refs[:6]
        land_refs, refs = refs[6:6 + nx], refs[6 + nx:]
        ds_ref, dyb_ref, seg_ref = refs[:3]
        if nx:
            start, wait = _exchange_ops([c for _, c in exchange], x_refs, land_refs, *refs[3:])

        @pl.when(pl.program_id(0) == 0)
        def _():
            ds_ref[...] = jnp.zeros_like(ds_ref)
            seg_ref[...] = _seg_mat(LW)
            if nx:
                start()

        seg = seg_ref[...]
        lane = jnp.bitwise_and(lax.broadcasted_iota(jnp.int32, (1, LW), 1), HEAD - 1)
        groups = _lane_groups()
        head_row = lax.broadcasted_iota(jnp.int32, (n_heads, LW), 0)
        lane_head = lax.shift_right_logical(lax.broadcasted_iota(jnp.int32, (n_heads, LW), 1), 6)

        def colsum(z):
            return jnp.sum(z, axis=0, keepdims=True)

        def dysel(t, gsl):
            return jnp.where(lane == t, dyT_ref[0, :, gsl], 0.0)

        for gsl, dyb in zip(groups, _segb_stack([(dysel(C - 1, gsl), 1) for gsl in groups], seg)):
            dyb_ref[:, gsl] = dyb

        def step(i, dvacc):
            t = C - 1 - i
            dybs = [dyb_ref[:, gsl] for gsl in groups]
            dSs = [ds_ref[:, gsl] + dyb * r_ref[pl.ds(t, 1), gsl] for gsl, dyb in zip(groups, dybs)]
            dsabs = _segb_stack([(dS * b_ref[pl.ds(t, 1), gsl], 2) for gsl, dS in zip(groups, dSs)], seg)
            for gsl, dS, dsab in zip(groups, dSs, dsabs):
                ds_ref[:, gsl] = dS * w_ref[pl.ds(t, 1), gsl] + dsab * a_ref[pl.ds(t, 1), gsl]
            out = []
            dy_rows = dyR_ref[t].astype(BF16)
            v_sa_rows = jnp.concatenate([vR_ref[t], saR_ref[t]], axis=0).astype(BF16)
            side = _segb_stack([(x, 1) for gsl, dS in zip(groups, dSs)
                                for x in (dS * k_ref[pl.ds(t, 1), gsl], dysel(t - 1, gsl))], seg)
            for g, (gsl, dva, dS, dsab) in enumerate(zip(groups, dvacc, dSs, dsabs)):
                dvb, dyb_ref[:, gsl] = side[2 * g:2 * g + 2]
                Sp = S_ref[0, t, :, gsl]
                own = head_row == lane_head + g * (LW // HEAD)

                def rows_in(rows, mat):
                    full = jnp.dot(rows, mat.astype(BF16), preferred_element_type=F32)
                    return [jnp.sum(jnp.where(own, full[s:s + n_heads], 0.0), axis=0, keepdims=True)
                            for s in range(0, rows.shape[0], n_heads)]

                (dr,) = rows_in(dy_rows, S_ref[0, t + 1, :, gsl])
                dk, db = rows_in(v_sa_rows, dS)
                dr_ref[pl.ds(t, 1), gsl] = dr
                dk_ref[pl.ds(t, 1), gsl] = dk
                db_ref[pl.ds(t, 1), gsl] = db
                dw_ref[pl.ds(t, 1), gsl] = colsum(dS * Sp)
                da_ref[pl.ds(t, 1), gsl] = colsum(Sp * dsab)
                out.append(jnp.where(lane == t, dvb, dva))
            return tuple(out)

        dvacc = _unrolled_loop(C, step, tuple(jnp.zeros((HEAD, LW), F32) for _ in groups), SCAN_UNROLL_BWD)
        for gsl, dva in zip(groups, dvacc):
            dvT_ref[0, :, gsl] = dva

        if nx:
            @pl.when(pl.program_id(0) == nC - 1)
            def _():
                wait()

    row = pl.BlockSpec((C, RW_WIDTH), lambda c: (nC - 1 - c, 0))
    ht = pl.BlockSpec((1, HEAD, RW_WIDTH), lambda c: (nC - 1 - c, 0, 0))
    hbm = pl.BlockSpec(memory_space=pl.ANY)
    per_head = pl.BlockSpec((C, n_heads, HEAD), lambda c: (nC - 1 - c, 0, 0))
    rows_shape = jax.ShapeDtypeStruct((T, RW_WIDTH), F32)
    res = pl.pallas_call(
        body, name="rwkv_scan_bwd", grid=(nC,),
        in_specs=[row, row, row, row, row, per_head, per_head, per_head, ht,
                  pl.BlockSpec((1, C + 1, HEAD, RW_WIDTH), lambda c: (nC - 1 - c, 0, 0, 0))] + [hbm] * nx,
        out_specs=[row, row, row, row, row, ht] + [hbm] * nx,
        out_shape=[rows_shape] * 5 + [jax.ShapeDtypeStruct((nC, HEAD, RW_WIDTH), F32)] + _exchange_shapes(exchange),
        scratch_shapes=[pltpu.VMEM((HEAD, RW_WIDTH), F32), pltpu.VMEM((HEAD, RW_WIDTH), F32),
                        pltpu.VMEM((LW, LW), BF16)] + (_exchange_sems(nx) if nx else []),
        compiler_params=pltpu.CompilerParams(dimension_semantics=("arbitrary",), vmem_limit_bytes=VMEM_LIMIT_BYTES,
                                             has_side_effects=bool(nx)),
    )(a, w, b, k, r, v_rows, sa_rows, dy_rows, dyT, S_all, *[z for z, _ in exchange])
    return res[:6], res[6:]


def _alibi_slope(h):
    return float(np.float32(2.0 ** (-8.0 * (h + 1) / ATT_HEADS)))


ATT_GROUP_HEADS = 4


def _stack_heads(x, lane_head, fill=0.0):
    return jnp.concatenate([jnp.where(lane_head == hh, x, fill) for hh in range(ATT_GROUP_HEADS)], axis=0)


def _unstack_heads(x, lane_head, L):
    out = jnp.zeros((L, x.shape[1]), F32)
    for hh in range(ATT_GROUP_HEADS):
        out = jnp.where(lane_head == hh, x[hh * L:(hh + 1) * L], out)
    return out


def _att_logits(qs, kcat, gi, d, L, n):
    qi = lax.broadcasted_iota(jnp.int32, (L, 2 * L), 0)
    kj = lax.broadcasted_iota(jnp.int32, (L, 2 * L), 1)
    steps = qi + L - kj
    valid = (steps >= 0) & (steps <= L) & ((kj >= L) | (n > 0))
    dist = (d * steps).astype(F32)
    bias = jnp.concatenate([jnp.where(valid, -_alibi_slope(gi * ATT_GROUP_HEADS + hh) * dist, NEG_BIG)
                            for hh in range(ATT_GROUP_HEADS)], axis=0)
    s = lax.dot_general(qs.astype(BF16), kcat, NT_DIMS, preferred_element_type=F32) * (HEAD ** -0.5)
    return jnp.where(bias > 0.5 * NEG_BIG, s + bias, NEG_BIG)


def att_fwd(pa, gi, T):
    window, d = ATT_GROUPS[gi]
    L = window // d
    Tj = T // d
    nb = Tj // L
    pv = pa.reshape(Tj, d * ATT_COLS)
    nblk = ATT_COLS // ATT_OUT

    def fn(pids, q, kp, kc, vp, vc):
        lane_head = lax.shift_right_logical(lax.broadcasted_iota(jnp.int32, (1, ATT_OUT), 1), 6)
        kcat = jnp.concatenate([kp, kc], axis=0).astype(BF16)
        vcat = jnp.concatenate([vp, vc], axis=0).astype(BF16)
        s = _att_logits(_stack_heads(q, lane_head), kcat, gi, d, L, pids[1])
        m = jnp.max(s, axis=-1, keepdims=True)
        p = jnp.exp(s - m)
        l = jnp.sum(p, axis=-1, keepdims=True)
        o = jnp.dot(p.astype(BF16), vcat, preferred_element_type=F32) / l
        lse = jnp.broadcast_to(m + jnp.log(l), o.shape)
        return _unstack_heads(o, lane_head, L), _unstack_heads(lse, lane_head, L)

    blk = (L, ATT_OUT)
    ins = [(pv, blk, lambda r, n: (n, r * nblk + gi)),
           (pv, blk, lambda r, n: (jnp.maximum(n - 1, 0), r * nblk + 3 + gi)),
           (pv, blk, lambda r, n: (n, r * nblk + 3 + gi)),
           (pv, blk, lambda r, n: (jnp.maximum(n - 1, 0), r * nblk + 6 + gi)),
           (pv, blk, lambda r, n: (n, r * nblk + 6 + gi))]
    out = ((Tj, d * ATT_OUT), F32, blk, lambda r, n: (n, r), None)
    o, lseb = tile_call(f"att_fwd_g{gi}", fn, (d, nb), ins, [out, out])
    return o.reshape(T, ATT_OUT), lseb.reshape(T, ATT_OUT)


def att_bwd(pa, o, lseb, do, dlseb, gi, T):
    window, d = ATT_GROUPS[gi]
    L = window // d
    Tj = T // d
    nb = Tj // L
    pv = pa.reshape(Tj, d * ATT_COLS)
    nblk = ATT_COLS // ATT_OUT
    view = lambda z: z.reshape(Tj, d * ATT_OUT)

    def body(q_ref, kp_ref, kc_ref, vp_ref, vc_ref, o_ref, l_ref, do_ref, dl_ref, dq_ref, dk_ref, dv_ref):
        n = pl.program_id(1)

        @pl.when(n == 0)
        def _():
            dk_ref[...] = jnp.zeros_like(dk_ref)
            dv_ref[...] = jnp.zeros_like(dv_ref)

        lane_head = lax.shift_right_logical(lax.broadcasted_iota(jnp.int32, (1, ATT_OUT), 1), 6)
        kcat = jnp.concatenate([kp_ref[...], kc_ref[...]], axis=0).astype(BF16)
        vcat = jnp.concatenate([vp_ref[...], vc_ref[...]], axis=0).astype(BF16)
        qs = _stack_heads(q_ref[...], lane_head)
        dos = _stack_heads(do_ref[...], lane_head)
        lse = jnp.max(_stack_heads(l_ref[...], lane_head, NEG_BIG), axis=-1, keepdims=True)
        dlse = jnp.sum(_stack_heads(dl_ref[...], lane_head), axis=-1, keepdims=True)
        delta = jnp.sum(dos * jnp.concatenate([o_ref[...]] * ATT_GROUP_HEADS, axis=0), axis=-1, keepdims=True)
        p = jnp.exp(_att_logits(qs, kcat, gi, d, L, n) - lse)
        dp = lax.dot_general(dos.astype(BF16), vcat, NT_DIMS, preferred_element_type=F32)
        ds = (p * (dp - delta + dlse)).astype(BF16)
        dq = _unstack_heads(jnp.dot(ds, kcat, preferred_element_type=F32), lane_head, L)
        dkc = lax.dot_general(ds, qs.astype(BF16), TN_DIMS, preferred_element_type=F32)
        dvc = lax.dot_general(p.astype(BF16), dos.astype(BF16), TN_DIMS, preferred_element_type=F32)
        scale = HEAD ** -0.5
        dq_ref[...] = dq * scale
        cur = pl.ds(pl.multiple_of(n * L, L), L)
        dk_ref[cur, :] += dkc[L:] * scale
        dv_ref[cur, :] += dvc[L:]

        @pl.when(n > 0)
        def _():
            prev = pl.ds(pl.multiple_of((n - 1) * L, L), L)
            dk_ref[prev, :] += dkc[:L] * scale
            dv_ref[prev, :] += dvc[:L]

    blk = pl.BlockSpec((L, ATT_OUT), lambda r, n: (n, r))
    res = pl.BlockSpec((Tj, ATT_OUT), lambda r, n: (0, r))
    qspec = lambda off, prev: pl.BlockSpec(
        (L, ATT_OUT), (lambda r, n: (jnp.maximum(n - 1, 0), r * nblk + off + gi)) if prev
        else (lambda r, n: (n, r * nblk + off + gi)))
    shape = jax.ShapeDtypeStruct((Tj, d * ATT_OUT), F32)
    dq, dk, dv = pl.pallas_call(
        body, name=f"att_bwd_g{gi}", grid=(d, nb),
        in_specs=[qspec(0, False), qspec(3, True), qspec(3, False), qspec(6, True), qspec(6, False),
                  blk, blk, blk, blk],
        out_specs=[blk, res, res],
        out_shape=[shape, shape, shape],
        compiler_params=_cparams(2),
    )(pv, pv, pv, pv, pv, view(o), view(lseb), view(do), view(dlseb))
    return dq.reshape(T, ATT_OUT), dk.reshape(T, ATT_OUT), dv.reshape(T, ATT_OUT)


FFN_TM, FFN_TC = 512, 512


def _conv3(u, prev8, cw, cb):
    return cb + cw[0:1] * u + cw[1:2] * _shift_down(u, prev8, 1) + cw[2:3] * _shift_down(u, prev8, 2)


def conv_glu_fwd(u, conv_w, conv_b):
    T = u.shape[0]
    tm, tc = FFN_TM, FFN_TC
    nj, ni = D_FF // tc, T // tm

    def fn(pids, ug, ugh, uv, uvh, cwg, cbg, cwv, cbv):
        first = pids[1] > 0
        cg = _conv3(ug, jnp.where(first, ugh, 0.0), cwg, cbg)
        cv = _conv3(uv, jnp.where(first, uvh, 0.0), cwv, cbv)
        return _gelu_tanh(cg) * cv

    halo = lambda off: (lambda j, i: (jnp.maximum(i * (tm // 8) - 1, 0), j + off))
    ins = [(u, (tm, tc), lambda j, i: (i, j)), (u, (8, tc), halo(0)),
           (u, (tm, tc), lambda j, i: (i, j + nj)), (u, (8, tc), halo(nj)),
           (conv_w, (3, tc), lambda j, i: (0, j)), (conv_b, (1, tc), lambda j, i: (0, j)),
           (conv_w, (3, tc), lambda j, i: (0, j + nj)), (conv_b, (1, tc), lambda j, i: (0, j + nj))]
    out = ((T, D_FF), BF16, (tm, tc), lambda j, i: (i, j), None)
    return tile_call("conv_glu_fwd", fn, (nj, ni), ins, [out])[0]


def conv_glu_bwd(u, conv_w, conv_b, df):
    T = u.shape[0]
    tm, tc = FFN_TM, FFN_TC
    nj, ni = D_FF // tc, T // tm

    def fn(pids, ug, ugh, uv, uvh, cwg, cbg, cwv, cbv, df_t, nxt_g, nxt_v):
        i = ni - 1 - pids[1]
        ugh = jnp.where(i > 0, ugh, 0.0)
        uvh = jnp.where(i > 0, uvh, 0.0)
        cg = _conv3(ug, ugh, cwg, cbg)
        cv = _conv3(uv, uvh, cwv, cbv)
        _, vjp = jax.vjp(lambda g_, v_: _gelu_tanh(g_) * v_, cg, cv)
        dcg, dcv = vjp(df_t.astype(F32))
        cs = lambda z: jnp.sum(z, axis=0, keepdims=True)

        @pl.when(pids[1] == 0)
        def _():
            nxt_g[...] = jnp.zeros_like(nxt_g)
            nxt_v[...] = jnp.zeros_like(nxt_v)

        outs = []
        for dc, cw, nxt_ref in ((dcg, cwg, nxt_g), (dcv, cwv, nxt_v)):
            nxt = nxt_ref[...]
            outs.append(cw[0:1] * dc + cw[1:2] * _shift_up(dc, nxt, 1) + cw[2:3] * _shift_up(dc, nxt, 2))
            nxt_ref[...] = dc[:8]
        for dc, uu, hh in ((dcg, ug, ugh), (dcv, uv, uvh)):
            outs += [cs(dc * uu), cs(dc * _shift_down(uu, hh, 1)), cs(dc * _shift_down(uu, hh, 2)), cs(dc)]
        return outs

    rows = lambda off: (lambda j, r: (ni - 1 - r, j + off))
    halo = lambda off: (lambda j, r: (jnp.maximum((ni - 1 - r) * (tm // 8) - 1, 0), j + off))
    ins = [(u, (tm, tc), rows(0)), (u, (8, tc), halo(0)),
           (u, (tm, tc), rows(nj)), (u, (8, tc), halo(nj)),
           (conv_w, (3, tc), lambda j, r: (0, j)), (conv_b, (1, tc), lambda j, r: (0, j)),
           (conv_w, (3, tc), lambda j, r: (0, j + nj)), (conv_b, (1, tc), lambda j, r: (0, j + nj)),
           (df, (tm, tc), rows(0))]
    big = ((T, D_FF), BF16, (tm, tc), rows(0), None)
    acc = ((1, D_FF), F32, (1, tc), lambda j, r: (0, j), 1)
    res = tile_call("conv_glu_bwd", fn, (nj, ni), ins, [big, big] + [acc] * 8,
                    scratch=[((8, tc), F32), ((8, tc), F32)])
    dconv_w = jnp.concatenate([jnp.concatenate([res[2 + j], res[6 + j]], axis=1) for j in range(3)], axis=0)
    dconv_b = jnp.concatenate([res[5], res[9]], axis=1)
    return res[0], res[1], dconv_w, dconv_b


def _pad_cols(w, total):
    return jnp.pad(w, ((0, 0), (0, total - w.shape[1])))


def _pad_rows(w, total):
    return jnp.pad(w, ((0, total - w.shape[0]), (0, 0)))


def _proj_pad(w):
    z = lambda n: jnp.zeros((w.shape[0], n), w.dtype)
    return jnp.concatenate([w[:, :1600], z(64), w[:, 1600:1664], z(64), w[:, 1664:1824], z(96), w[:, 1824:],
                            z(PROJ_TAIL)], axis=1)


def _proj_unpad(g):
    return jnp.concatenate([g[:, :1600], g[:, OFF_XA:OFF_XA + 64], g[:, OFF_XG:OFF_XG + 160],
                            g[:, RW_PAD:RW_PAD + ATT_COLS]], axis=1)


def _rw_unpad(g):
    return jnp.concatenate([g[:, :1600], g[:, OFF_XA:OFF_XA + 64], g[:, OFF_XG:OFF_XG + 160]], axis=1)


def rms_fwd(name, x, g, tm=256):
    T, D = x.shape
    return tile_call(name, lambda pid, x_t, g_t: _rms(x_t, g_t), (T // tm,),
                     [_rows(x, tm), _par(g)], [_row_out(T, D, BF16, tm)])[0]


def rms_bwd(name, x, g, dh, dres, with_bf16=True, tm=256):
    T, D = x.shape
    out_dtypes = (F32, BF16) if with_bf16 else (F32,)

    def fn(pid, x_t, g_t, dh_t, dres_t):
        _, vjp = jax.vjp(_rms, x_t, g_t)
        dx, dg = vjp(dh_t.astype(F32))
        return (dres_t + dx,) * len(out_dtypes) + (dg,)

    return tile_call(name, fn, (T // tm,), [_rows(x, tm), _par(g), _rows(dh, tm), _rows(dres, tm)],
                     [_row_out(T, D, dt, tm) for dt in out_dtypes] + [_acc_out(1, D)])


def local_step(x, p, target, W):
    T, D = x.shape
    G = {}

    w_in_p = W["w_in_p"]
    mu_p = _proj_pad(_pad_cols(W["rw_mu"], 4128))[:, :RW_PAD]
    w_up_p = _pad_rows(W["rw_w_up"], 128)
    a_up_p = _pad_rows(W["rw_a_up"], 128)
    g_up_p = _pad_rows(W["rw_g_up"], 256)
    r_k = W["rw_r_k"].reshape(1, RW_WIDTH)
    rw_params = [mu_p, W["rw_w0"], w_up_p, W["rw_a0"], a_up_p, g_up_p, W["rw_k_k"], W["rw_k_a"]]

    h = rms_fwd("rms_mix", x, W["g_mix"])
    proj = matmul("proj_in_rw", h, w_in_p[:, :RW_PAD])
    pa = matmul("proj_in_att", h, w_in_p[:, RW_PAD:RW_PAD + ATT_COLS])
    gp = matmul("proj_gate", h, W["w_gate"])

    tm = 256
    rw_in = (proj, (tm, RW_PAD), lambda i: (i, 0))
    rw_halo = _prev_halo(proj, tm, RW_PAD)

    def rw_pre_tile(pid, Pc, halo, *params):
        prev8 = jnp.where(pid[0] > 0, halo, 0.0)
        params = [q.astype(F32) for q in params]
        return rw_pre(Pc, _shift_down(Pc, prev8, 1), *params)

    r, decay, k2, v, avec, bvec, g = tile_call(
        "rw_pre", rw_pre_tile, (T // tm,), [rw_in, rw_halo] + [_par(q) for q in rw_params],
        [_row_out(T, RW_WIDTH, F32, tm)] * 7)

    wa, ba, ka = scan_pair_terms(avec, decay, bvec, k2)
    vT = _to_head_time(v).astype(BF16)
    (yT, S_all, saT), late_slots = rwkv_scan_fwd(avec, decay, bvec, k2, r, vT, wa, ba, ka,
                                            exchange=_late_weight_sources(W))
    y = _from_head_time(yT)
    W = dict(W, **_late_weights(late_slots))

    post_params = [W["rw_ln_g"], W["rw_ln_b"], r_k]
    ya = tile_call("rw_post", lambda pid, *t: rw_post(*t), (T // tm,),
                   [_rows(z, tm) for z in (y, r, k2, v, g)] + [_par(q) for q in post_params],
                   [_row_out(T, RW_WIDTH, BF16, tm)])[0]

    att = [att_fwd(pa, gi, T) for gi in range(3)]
    o_l = [att[0][0], att[1][0], att[2][0], att[0][1], att[1][1], att[2][1]]
    yb = tile_call("att_combine", lambda pid, *t: att_combine(*t), (T // tm,),
                   [_rows(z, tm) for z in o_l], [_row_out(T, ATT_OUT, BF16, tm)])[0]

    za = matmul("branch_a", ya, W["w_branch_a"])
    zb = matmul("branch_b", yb, W["w_branch_b"])
    merged = tile_call("merge", lambda pid, *t: merge_fn(*t), (T // tm,),
                       [_rows(gp, tm), _par(W["b_gate"]), _rows(za, tm), _rows(zb, tm)],
                       [_row_out(T, D, BF16, tm)])[0]
    x1 = matmul("mix_out", merged, W["w_out"], res=x)

    h2 = rms_fwd("rms_ffn", x1, W["g_ffn"])
    u = matmul("ffn_up", h2, W["w_up"])
    f = conv_glu_fwd(u, W["conv_w"], W["conv_b"])
    x2 = matmul("ffn_down", f, W["w_down"], res=x1)

    h3 = rms_fwd("rms_ple", x2, W["g_ple"])
    zg = matmul("ple_gate", h3, W["w_ple_gate"])
    pe = matmul("ple_embed", p, W["w_ple"])

    def tail_tile(pid, x2_t, zg_t, pe_t, gf, tgt):
        loss, vjp = jax.vjp(lambda a_, b_, c_, d_: tail_loss(a_, b_, c_, d_, tgt), x2_t, zg_t, pe_t, gf)
        dx2, dzg, dpe, dgf = vjp(jnp.ones((), F32))
        return dx2, dzg, dpe, dgf, jnp.full((1, 128), loss, F32)

    tmt = 128
    dx3, dzg, dpe, dgf, loss_acc = tile_call(
        "tail_loss", tail_tile, (T // tmt,),
        [_rows(x2, tmt), _rows(zg, tmt), _rows(pe, tmt), _par(W["g_final"]), _rows(target, tmt)],
        [_row_out(T, D, F32, tmt), _row_out(T, D, BF16, tmt), _row_out(T, D, BF16, tmt),
         _acc_out(1, D), _acc_out(1, 128)])
    loss = loss_acc[0, 0]
    G["g_final"] = dgf

    wgrad = functools.partial(matmul, mode="tn", out_dtype=GRAD_WIRE)
    G["w_ple"] = wgrad("d_w_ple", p, dpe)
    G["w_ple_gate"] = wgrad("d_w_ple_gate", h3, dzg)
    dh3 = matmul("d_h3", dzg, W["w_ple_gate"], "nt")
    dx2, dx2b, G["g_ple"] = rms_bwd("rms_ple_bwd", x2, W["g_ple"], dh3, dx3)

    G["w_down"] = wgrad("d_w_down", f, dx2b)
    df = matmul("d_f", dx2b, W["w_down"], "nt", out_dtype=BF16)
    du_g, du_v, G["conv_w"], G["conv_b"] = conv_glu_bwd(u, W["conv_w"], W["conv_b"], df)
    du = jnp.concatenate([du_g, du_v], axis=1)
    G["w_up"] = wgrad("d_w_up", h2, du)
    dh2 = matmul("d_h2", du, W["w_up"], "nt")
    dx1, dx1b, G["g_ffn"] = rms_bwd("rms_ffn_bwd", x1, W["g_ffn"], dh2, dx2)

    G["w_out"] = wgrad("d_w_out", merged, dx1b)
    dmerged = matmul("d_merged", dx1b, W["w_out"], "nt", out_dtype=BF16)

    def merge_bwd_tile(pid, gp_t, bg, za_t, zb_t, dm_t):
        _, vjp = jax.vjp(merge_fn, gp_t, bg, za_t, zb_t)
        return vjp(dm_t.astype(F32))

    dgp, G["b_gate"], dza, dzb = tile_call(
        "merge_bwd", merge_bwd_tile, (T // tm,),
        [_rows(gp, tm), _par(W["b_gate"]), _rows(za, tm), _rows(zb, tm), _rows(dmerged, tm)],
        [_row_out(T, 2 * D, BF16, tm), _acc_out(1, 2 * D), _row_out(T, D, BF16, tm), _row_out(T, D, BF16, tm)])
    G["w_branch_a"] = wgrad("d_w_branch_a", ya, dza)
    dya = matmul("d_ya", dza, W["w_branch_a"], "nt")
    G["w_branch_b"] = wgrad("d_w_branch_b", yb, dzb)
    dyb = matmul("d_yb", dzb, W["w_branch_b"], "nt")
    G["w_gate"] = wgrad("d_w_gate", h, dgp)
    dh_gate = matmul("d_h_gate", dgp, W["w_gate"], "nt")

    def comb_bwd_tile(pid, *t):
        _, vjp = jax.vjp(att_combine, *t[:6])
        return vjp(t[6])

    d_ol = tile_call("att_combine_bwd", comb_bwd_tile, (T // tm,),
                     [_rows(z, tm) for z in o_l] + [_rows(dyb, tm)],
                     [_row_out(T, ATT_OUT, F32, tm)] * 6)
    dqkv = [att_bwd(pa, att[gi][0], att[gi][1], d_ol[gi], d_ol[3 + gi], gi, T) for gi in range(3)]
    d_att = [dqkv[gi][j] for j in range(3) for gi in range(3)]

    def post_bwd_tile(pid, *t):
        _, vjp = jax.vjp(rw_post, *t[:8])
        return vjp(t[8])

    dy, dr_p, dk2_p, dv_p, dg, G["rw_ln_g"], G["rw_ln_b"], d_rk = tile_call(
        "rw_post_bwd", post_bwd_tile, (T // tm,),
        [_rows(z, tm) for z in (y, r, k2, v, g)] + [_par(q) for q in post_params] + [_rows(dya, tm)],
        [_row_out(T, RW_WIDTH, F32, tm)] * 5 + [_acc_out(1, RW_WIDTH)] * 3)
    G["rw_r_k"] = d_rk.reshape(W["rw_r_k"].shape)

    (da, dw, db, dk_s, dr_s, dvT), G["_early_parts"] = rwkv_scan_bwd(
        avec, decay, bvec, k2, r, v, dy, S_all, saT, exchange=_early_grad_sources(G))
    dv_s = _from_head_time(dvT)

    tmb = 128
    rw_in_b = (proj, (tmb, RW_PAD), lambda i: (i, 0))

    def pre_bwd_tile(pid, Pc, halo, *t):
        prev8 = jnp.where(pid[0] > 0, halo, 0.0)
        params = [q.astype(F32) for q in t[:8]]
        dr1, dr2, dw_, dk1, dk2_, dv1, dv2, da_, db_, dg_ = t[8:]
        _, vjp = jax.vjp(rw_pre, Pc, _shift_down(Pc, prev8, 1), *params)
        return vjp((dr1 + dr2, dw_, dk1 + dk2_, dv1 + dv2, da_, db_, dg_))

    cts = (dr_s, dr_p, dw, dk_s, dk2_p, dv_s, dv_p, da, db, dg)
    res = tile_call(
        "rw_pre_bwd", pre_bwd_tile, (T // tmb,),
        [rw_in_b, _prev_halo(proj, tmb, RW_PAD)] + [_par(q) for q in rw_params] + [_rows(z, tmb) for z in cts],
        [_row_out(T, RW_PAD, F32, tmb)] * 2 + [_acc_out(*q.shape) for q in rw_params])
    dPc, dPs = res[0], res[1]
    d_mu, G["rw_w0"], d_wup, G["rw_a0"], d_aup, d_gup, G["rw_k_k"], G["rw_k_a"] = res[2:]
    G["rw_mu"] = _rw_unpad(d_mu)
    G["rw_w_up"], G["rw_a_up"], G["rw_g_up"] = d_wup[:64], d_aup[:64], d_gup[:160]

    def dproj_tile(pid, dPc_t, dPs_t, nxt, *att_t):
        nxt = jnp.where(pid[0] < T // tm - 1, nxt, 0.0)
        tail = jnp.zeros((dPc_t.shape[0], PROJ_TAIL), F32)
        return jnp.concatenate([dPc_t + _shift_up(dPs_t, nxt, 1)] + list(att_t) + [tail], axis=1)

    dproj = tile_call("d_proj", dproj_tile, (T // tm,),
                      [_rows(dPc, tm), _rows(dPs, tm), _next_halo(dPs, tm, RW_PAD, T)] + [_rows(z, tm) for z in d_att],
                      [_row_out(T, PROJ_PAD, BF16, tm)])[0]
    G["w_in_p"] = wgrad("d_w_in", h, dproj)
    dh = matmul("d_h", dproj, w_in_p, "nt", res=dh_gate)
    dx, G["g_mix"] = rms_bwd("rms_mix_bwd", x, W["g_mix"], dh, dx1, with_bf16=False)
    return loss, dx, G


def _mesh_pos():
    return lax.axis_index("x"), lax.axis_index("y"), lax.axis_index("c")


def _peer(pos, k):
    x, y, c = pos
    px = 1 - x if k & 4 else x
    py = 1 - y if k & 2 else y
    pc = 1 - c if k & 1 else c
    return (px, py, pc), 4 * px + 2 * py + pc


def all_gather_blocks(name, blocks):
    n = len(blocks)

    def body(*refs):
        x_refs, out_refs = refs[:n], refs[n:2 * n]
        send_sems, recv_sems, local_sems = refs[2 * n:]
        x, y, c = _mesh_pos()
        me, sibling = (x, y, c), (x, y, 1 - c)
        chips = [(1 - x, y), (x, 1 - y), (1 - x, 1 - y)]
        ops = range(n)

        def slot(i, px, py, pc):
            return out_refs[i].at[4 * px + 2 * py + pc]

        def copy(k, i, block, to, own=False):
            return pltpu.make_async_remote_copy(
                src_ref=x_refs[i] if own else slot(i, *block), dst_ref=slot(i, *block),
                send_sem=send_sems.at[k, i], recv_sem=recv_sems.at[k, i],
                device_id=to, device_id_type=pl.DeviceIdType.MESH)

        mine = [pltpu.make_async_copy(x_refs[i], slot(i, *me), local_sems.at[i]) for i in ops]
        first = [copy(0, i, me, sibling, own=True) for i in ops]
        first += [copy(1 + j, i, me, (*chip, c), own=True) for j, chip in enumerate(chips) for i in ops]
        for cp in mine + first:
            cp.start()
        passed = []
        for j, chip in enumerate(chips):
            for i in ops:
                copy(1 + j, i, (*chip, c), me).wait_recv()
                passed.append(copy(4 + j, i, (*chip, c), sibling))
                passed[-1].start()
        for i in ops:
            copy(0, i, sibling, me).wait_recv()
        for j, chip in enumerate(chips):
            for i in ops:
                copy(4 + j, i, (*chip, 1 - c), me).wait_recv()
        for cp in first + passed:
            cp.wait_send()
        for cp in mine:
            cp.wait()

    return pl.pallas_call(
        body, name=name,
        in_specs=[pl.BlockSpec(memory_space=pl.ANY)] * n,
        out_specs=[pl.BlockSpec(memory_space=pl.ANY)] * n,
        out_shape=[jax.ShapeDtypeStruct((N_DEV,) + b.shape, b.dtype) for b in blocks],
        scratch_shapes=[pltpu.SemaphoreType.DMA((N_DEV - 1, n)), pltpu.SemaphoreType.DMA((N_DEV - 1, n)),
                        pltpu.SemaphoreType.DMA((n,))],
        compiler_params=pltpu.CompilerParams(has_side_effects=True),
    )(*blocks)


WHOLE = 0


def _exchange_shapes(srcs):
    shapes = [a.shape[1:] if cols is None else a.shape if cols == WHOLE else (a.shape[0], cols) for a, cols in srcs]
    return [jax.ShapeDtypeStruct((N_DEV,) + s, a.dtype) for s, (a, _) in zip(shapes, srcs)]


def _exchange_sems(n):
    return [pltpu.SemaphoreType.DMA((N_DEV - 1, n)), pltpu.SemaphoreType.DMA((N_DEV - 1, n)),
            pltpu.SemaphoreType.DMA((n,))]


def _exchange_ops(col_widths, x_refs, out_refs, send_sems, recv_sems, local_sems):
    n = len(col_widths)
    pos = _mesh_pos()
    me = 4 * pos[0] + 2 * pos[1] + pos[2]

    def piece(i, d):
        cols = col_widths[i]
        if cols is None:
            return x_refs[i].at[d]
        if cols == WHOLE:
            return x_refs[i]
        return x_refs[i].at[:, pl.ds(pl.multiple_of(d * cols, 128), cols)]

    def local(i):
        return pltpu.make_async_copy(piece(i, me), out_refs[i].at[me], local_sems.at[i])

    def remote(k, i, landing):
        peer, idx = _peer(pos, k)
        return pltpu.make_async_remote_copy(
            src_ref=piece(i, idx), dst_ref=out_refs[i].at[idx if landing else me],
            send_sem=send_sems.at[k - 1, i], recv_sem=recv_sems.at[k - 1, i],
            device_id=peer, device_id_type=pl.DeviceIdType.MESH)

    pairs = [(k, i) for k in range(1, N_DEV) for i in range(n)]

    def start():
        for i in range(n):
            local(i).start()
        for k, i in pairs:
            remote(k, i, False).start()

    def wait():
        for k, i in pairs:
            remote(k, i, True).wait_recv()
        for k, i in pairs:
            remote(k, i, False).wait_send()
        for i in range(n):
            local(i).wait()

    return start, wait


def all_to_all_blocks(name, srcs):
    n = len(srcs)

    def body(*refs):
        start, wait = _exchange_ops([c for _, c in srcs], refs[:n], refs[n:2 * n], *refs[2 * n:])
        start()
        wait()

    return pl.pallas_call(
        body, name=name,
        in_specs=[pl.BlockSpec(memory_space=pl.ANY)] * n,
        out_specs=[pl.BlockSpec(memory_space=pl.ANY)] * n,
        out_shape=_exchange_shapes(srcs),
        scratch_shapes=_exchange_sems(n),
        compiler_params=pltpu.CompilerParams(has_side_effects=True),
    )(*[a for a, _ in srcs])


def _adam_row_tile(R, C):
    best = None
    for t in range(16, R + 1, 16):
        if R % t == 0 and t * C <= ADAM_TILE_ELEMS:
            best = t
    return best if best is not None else R


def reduce_adamw(name, parts, w, m, v):
    _, R, C = parts.shape
    tr = _adam_row_tile(R, C)

    def fn(pid, parts_t, w_t, m_t, v_t):
        g = parts_t[0].astype(F32)
        for i in range(1, N_DEV):
            g = g + parts_t[i].astype(F32)
        m_n = ADAM_B1 * m_t + (1.0 - ADAM_B1) * g
        v_n = ADAM_B2 * v_t + (1.0 - ADAM_B2) * (g * g)
        m_hat = m_n / (1.0 - ADAM_B1 ** ADAM_STEP)
        v_hat = v_n / (1.0 - ADAM_B2 ** ADAM_STEP)
        delta = -ADAM_LR * (m_hat / (jnp.sqrt(v_hat) + ADAM_EPS) + ADAM_WD * w_t)
        return g, delta, m_n, v_n

    row = lambda a: (a, (tr, C), lambda i: (i, 0))
    out = ((R, C), F32, (tr, C), lambda i: (i, 0), None)
    return tile_call(name, fn, (R // tr,),
                     [(parts, (N_DEV, tr, C), lambda i: (0, i, 0)), row(w), row(m), row(v)], [out] * 4)


PARAMS = (
    ("g_mix", (1, 1024), None), ("w_in", (1024, 4128), 1), ("rw_mu", (1, 1824), None), ("rw_w0", (1, 512), None),
    ("rw_w_up", (64, 512), 1), ("rw_a0", (1, 512), None), ("rw_a_up", (64, 512), 1), ("rw_g_up", (160, 512), 1),
    ("rw_k_k", (1, 512), None), ("rw_k_a", (1, 512), None), ("rw_r_k", (8, 64), None), ("rw_ln_g", (1, 512), None),
    ("rw_ln_b", (1, 512), None), ("w_branch_a", (512, 1024), 1), ("w_branch_b", (256, 1024), 1),
    ("w_gate", (1024, 2048), 1), ("b_gate", (1, 2048), None), ("w_out", (1024, 1024), 0), ("g_ffn", (1, 1024), None),
    ("w_up", (1024, 6144), 1), ("conv_w", (3, 6144), 1), ("conv_b", (1, 6144), None), ("w_down", (3072, 1024), 0),
    ("g_ple", (1, 1024), None), ("w_ple_gate", (1024, 1024), 0), ("w_ple", (256, 1024), 1), ("g_final", (1, 1024), None),
)
SHARDED = tuple(q for q in PARAMS if q[2] is not None)
REPLICATED = tuple(q for q in PARAMS if q[2] is None)
BIG_NAMES = ("w_in", "w_up", "w_gate", "w_out", "w_down", "w_ple_gate", "w_branch_a", "w_branch_b", "w_ple")
BIG = tuple(q for q in SHARDED if q[0] in BIG_NAMES)
SMALL_SHARDED = tuple(q for q in SHARDED if q[0] not in BIG_NAMES)
PACK_COLS = 1024
F32_GATHERED = ("conv_w",)


def _local_shape(shape, axis):
    s = list(shape)
    s[axis] //= N_DEV
    return tuple(s)


def _numel(shape):
    return int(np.prod(shape))


def _pad_flat(z, mult):
    n = z.shape[-1]
    total = -(-n // mult) * mult
    return jnp.pad(z, [(0, 0)] * (z.ndim - 1) + [(0, total - n)])


def _full_from_slots(slots, shape, axis):
    loc = _local_shape(shape, axis)
    z = slots.reshape((N_DEV,) + loc)
    if axis == 0:
        return z.reshape(shape)
    return z.transpose(1, 0, 2).reshape(shape)


def _slots_from_full(full, shape, axis):
    loc = _local_shape(shape, axis)
    if axis == 0:
        return full.reshape(N_DEV, _numel(loc))
    return full.reshape(shape[0], N_DEV, loc[1]).transpose(1, 0, 2).reshape(N_DEV, _numel(loc))


W_IN_SLOT = 640
W_IN_LOCAL = 4128 // N_DEV


def _block_shape(shape, axis):
    return _local_shape(shape, axis) if axis is not None else shape


def _pad_w_in(block):
    return jnp.pad(block, ((0, 0), (0, W_IN_SLOT - W_IN_LOCAL)))


def _proj_col(s):
    return s + jnp.where(s >= 1600, 64, 0) + jnp.where(s >= 1664, 64, 0) + jnp.where(s >= 1824, 96, 0)


def _perm_tile(d, c0, width):
    j = lax.broadcasted_iota(jnp.int32, (W_IN_SLOT, width), 0)
    c = c0 + lax.broadcasted_iota(jnp.int32, (W_IN_SLOT, width), 1)
    hit = (_proj_col(d * W_IN_LOCAL + j) == c) & (j < W_IN_LOCAL)
    return jnp.where(hit, 1.0, 0.0).astype(BF16)


PERM_TILE = 768


def w_in_unshuffle(slots):
    _, K, _ = slots.shape
    tn = PERM_TILE
    reach = 3

    def first_slot(j):
        return j + jnp.where(j >= 3, 1, 0) + jnp.where(j >= 5, 1, 0)

    def body(a_ref, o_ref, acc_ref):
        j, kk = pl.program_id(0), pl.program_id(1)
        d = first_slot(j) + kk

        @pl.when(kk == 0)
        def _():
            acc_ref[...] = jnp.zeros_like(acc_ref)

        @pl.when(d < N_DEV)
        def _():
            acc_ref[...] += jnp.dot(a_ref[0], _perm_tile(d, j * tn, tn), preferred_element_type=F32)

        @pl.when(kk == reach - 1)
        def _():
            o_ref[...] = acc_ref[...].astype(o_ref.dtype)

    return pl.pallas_call(
        body, name="w_in_unshuffle", grid=(PROJ_PAD // tn, reach),
        in_specs=[pl.BlockSpec((1, K, W_IN_SLOT), lambda j, kk: (jnp.minimum(first_slot(j) + kk, N_DEV - 1), 0, 0))],
        out_specs=pl.BlockSpec((K, tn), lambda j, kk: (0, j)),
        out_shape=jax.ShapeDtypeStruct((K, PROJ_PAD), BF16),
        scratch_shapes=[pltpu.VMEM((K, tn), F32)],
        compiler_params=_cparams(2),
    )(slots)


def w_in_shuffle_grad(dw):
    K = dw.shape[0]
    tk = PERM_TILE

    def first_tile(d):
        return _proj_col(d * W_IN_LOCAL) // tk

    def body(g_ref, o_ref, acc_ref):
        d, kk = pl.program_id(0), pl.program_id(1)
        perm = _perm_tile(d, (first_tile(d) + kk) * tk, tk)
        part = lax.dot_general(g_ref[...].astype(BF16), perm, NT_DIMS, preferred_element_type=F32)

        @pl.when(kk == 0)
        def _():
            acc_ref[...] = part

        @pl.when(kk == 1)
        def _():
            o_ref[0] = (acc_ref[...] + part).astype(o_ref.dtype)

    return pl.pallas_call(
        body, name="w_in_shuffle_grad", grid=(N_DEV, 2),
        in_specs=[pl.BlockSpec((K, tk), lambda d, kk: (0, first_tile(d) + kk))],
        out_specs=pl.BlockSpec((1, K, W_IN_SLOT), lambda d, kk: (d, 0, 0)),
        out_shape=jax.ShapeDtypeStruct((N_DEV, K, W_IN_SLOT), GRAD_WIRE),
        scratch_shapes=[pltpu.VMEM((K, W_IN_SLOT), F32)],
        compiler_params=_cparams(2),
    )(dw)


def _flat_rows(pieces, dtype, row_mult):
    flat = jnp.concatenate([z.astype(dtype) for z in pieces], axis=-1)
    flat = _pad_flat(flat, row_mult * PACK_COLS)
    return flat.reshape(flat.shape[:-1] + (-1, PACK_COLS))


FIRST = tuple(q for q in BIG if q[0] in ("w_in", "w_gate"))
LATE = tuple(q for q in BIG if q not in FIRST)


def _matrix_from_slots(slots, shape, axis):
    return slots.reshape(shape) if axis == 0 else slots.transpose(1, 0, 2).reshape(shape)


def _late_weight_sources(W):
    return [(blk, WHOLE) for blk in W["_late_blocks"]]


def _late_weights(slots):
    return {n: _matrix_from_slots(s, shape, axis) for (n, shape, axis), s in zip(LATE, slots)}


def gather_weights(local):
    blocks = [(_pad_w_in(local[n]) if n == "w_in" else local[n]).astype(BF16) for n, _, _ in FIRST]
    small = [q for q in SMALL_SHARDED if q[0] not in F32_GATHERED]
    exact = [q for q in SMALL_SHARDED if q[0] in F32_GATHERED]
    blocks.append(_flat_rows([local[n].reshape(-1) for n, _, _ in small], BF16, 16))
    blocks.append(_flat_rows([local[n].reshape(-1) for n, _, _ in exact], F32, 8))
    got = all_gather_blocks("weight_all_gather", blocks)
    full = {"_late_blocks": [local[n].astype(BF16) for n, _, _ in LATE]}
    for (n, shape, axis), slots in zip(FIRST, got):
        if n == "w_in":
            full["w_in_p"] = w_in_unshuffle(slots)
        else:
            full[n] = _matrix_from_slots(slots, shape, axis)
    for group, slots in ((small, got[-2]), (exact, got[-1])):
        slots, off = slots.reshape(N_DEV, -1), 0
        for n, shape, axis in group:
            size = _numel(_local_shape(shape, axis))
            full[n] = _full_from_slots(slots[:, off:off + size], shape, axis)
            off += size
    for n, _, _ in REPLICATED:
        full[n] = local[n]
    return full


LOSS_SLOT = ("_loss", (1, 2), None)
PACKED_SMALL = SMALL_SHARDED + REPLICATED + (LOSS_SLOT,)


def _pack_small(vals):
    pieces = [vals[n].reshape(-1) if n in vals else jnp.zeros((_numel(shape),), F32) for n, shape, _ in PACKED_SMALL]
    return _flat_rows(pieces, F32, 16)


def _unpack_small(packed):
    flat, out, off = packed.reshape(-1), {}, 0
    for n, shape, axis in PACKED_SMALL:
        loc = _block_shape(shape, axis)
        out[n] = flat[off:off + _numel(loc)].reshape(loc)
        off += _numel(loc)
    return out


EARLY = tuple(q for q in BIG if q[0] != "w_in")


def _early_grad_sources(G):
    srcs = []
    for n, shape, axis in EARLY:
        if axis == 0:
            srcs.append((G[n].astype(GRAD_WIRE).reshape((N_DEV,) + _local_shape(shape, axis)), None))
        else:
            srcs.append((G[n].astype(GRAD_WIRE), shape[1] // N_DEV))
    return srcs


def _late_grad_sources(G, loss_local):
    srcs = [(w_in_shuffle_grad(G["w_in_p"]), None)]
    rows = [_slots_from_full(G[n].reshape(shape), shape, axis) for n, shape, axis in SMALL_SHARDED]
    loss_hi = loss_local.astype(GRAD_WIRE).astype(F32)
    rep = jnp.concatenate([G[n].reshape(-1) for n, _, _ in REPLICATED] + [jnp.stack([loss_hi, loss_local - loss_hi])])
    rows.append(jnp.broadcast_to(rep[None, :], (N_DEV, rep.shape[0])))
    srcs.append((_flat_rows(rows, GRAD_WIRE, 16), None))
    return srcs


def _step(x, p, target, local_w, local_m, local_v):
    full = gather_weights(local_w)
    loss_local, dx, G = local_step(x, p, target, full)
    late = all_to_all_blocks("grad_all_to_all", _late_grad_sources(G, loss_local))
    parts = [late[0]] + list(G["_early_parts"]) + [late[1]]
    outs = [{}, {}, {}, {}]
    for (n, shape, axis), part in zip((BIG[0],) + EARLY, parts):
        prep = _pad_w_in if n == "w_in" else (lambda z: z)
        res = reduce_adamw("adamw_" + n, part, prep(local_w[n]), prep(local_m[n]), prep(local_v[n]))
        for o, z in zip(outs, res):
            o[n] = z[:, :W_IN_LOCAL] if n == "w_in" else z
    res = reduce_adamw("adamw_small", parts[-1], _pack_small(local_w), _pack_small(local_m), _pack_small(local_v))
    for o, z in zip(outs, res):
        o.update(_unpack_small(z))
    loss = jnp.sum(outs[0]["_loss"])
    return loss, dx, outs


def kernel(x, p, g_mix, w_in, rw_mu, rw_w0, rw_w_up, rw_a0, rw_a_up, rw_g_up, rw_k_k, rw_k_a, rw_r_k, rw_ln_g, rw_ln_b, w_branch_a, w_branch_b, w_gate, b_gate, w_out, g_ffn, w_up, conv_w, conv_b, w_down, g_ple, w_ple_gate, w_ple, g_final, loss_target, m_g_mix, m_w_in, m_rw_mu, m_rw_w0, m_rw_w_up, m_rw_a0, m_rw_a_up, m_rw_g_up, m_rw_k_k, m_rw_k_a, m_rw_r_k, m_rw_ln_g, m_rw_ln_b, m_w_branch_a, m_w_branch_b, m_w_gate, m_b_gate, m_w_out, m_g_ffn, m_w_up, m_conv_w, m_conv_b, m_w_down, m_g_ple, m_w_ple_gate, m_w_ple, m_g_final, v_g_mix, v_w_in, v_rw_mu, v_rw_w0, v_rw_w_up, v_rw_a0, v_rw_a_up, v_rw_g_up, v_rw_k_k, v_rw_k_a, v_rw_r_k, v_rw_ln_g, v_rw_ln_b, v_w_branch_a, v_w_branch_b, v_w_gate, v_b_gate, v_w_out, v_g_ffn, v_w_up, v_conv_w, v_conv_b, v_w_down, v_g_ple, v_w_ple_gate, v_w_ple, v_g_final):
    args = dict(locals())
    names = [n for n, _, _ in PARAMS]
    orig_shape = {n: args[n].shape for n in names}

    def strip(prefix):
        out = {}
        for n, shape, axis in PARAMS:
            a = args[prefix + n]
            loc = _local_shape(shape, axis) if axis is not None else shape
            out[n] = a.reshape(loc)
        return out

    local_w, local_m, local_v = strip(""), strip("m_"), strip("v_")
    T, D = x.shape[-2], x.shape[-1]
    loss, dx, (g, delta, m_n, v_n) = _step(x.reshape(T, D), p.reshape(T, p.shape[-1]), loss_target.reshape(T, D),
                                           local_w, local_m, local_v)
    outs = [loss, dx.reshape(x.shape)]
    for group in (g, delta, m_n, v_n):
        outs += [group[n].reshape(orig_shape[n]) for n in names]
    return tuple(outs)
```

```python
import functools
import math

import numpy as np
import jax
import jax.numpy as jnp
from jax import lax
from jax.experimental import pallas as pl
from jax.experimental.pallas import tpu as pltpu

F32 = jnp.float32
BF16 = jnp.bfloat16
GRAD_WIRE = jnp.bfloat16

N_DEV = 8
NORM_EPS = 1e-6
RW_LN_EPS = 64e-5
HEAD = 64
RW_WIDTH = 512
ATT_GROUPS = ((128, 1), (512, 4), (2048, 16))
ATT_HEADS = 12
ATT_OUT = 256
ATT_COLS = 2304
OFF_XW, OFF_XA, OFF_XG, RW_PAD, PROJ_PAD = 1536, 1664, 1792, 2048, 4608
PROJ_TAIL = PROJ_PAD - RW_PAD - ATT_COLS
D_FF = 3072

ADAM_LR, ADAM_B1, ADAM_B2, ADAM_EPS, ADAM_WD, ADAM_STEP = 0.001, 0.9, 0.999, 1e-08, 0.01, 10

VMEM_LIMIT_BYTES = 56 * 1024 * 1024
ADAM_TILE_ELEMS = 256 * 1024
NEG_BIG = -1e30

NT_DIMS = (((1,), (1,)), ((), ()))
TN_DIMS = (((0,), (0,)), ((), ()))
NN_DIMS = (((1,), (0,)), ((), ()))


def _cparams(n_axes):
    return pltpu.CompilerParams(dimension_semantics=("arbitrary",) * n_axes,
                                vmem_limit_bytes=VMEM_LIMIT_BYTES)


def _split2(x):
    hi = x.astype(BF16)
    lo = (x - hi.astype(F32)).astype(BF16)
    return hi, lo


def _seg_mat(n):
    r = lax.shift_right_logical(lax.broadcasted_iota(jnp.int32, (n, n), 0), 6)
    c = lax.shift_right_logical(lax.broadcasted_iota(jnp.int32, (n, n), 1), 6)
    return jnp.where(r == c, 1.0, 0.0).astype(BF16)


def _segb(x, seg):
    return _segb_stack([(x, 2)], seg)[0]


def _segb_stack(items, seg):
    rows = items[0][0].shape[0]
    parts = []
    for x, passes in items:
        parts += list(_split2(x)) if passes == 2 else [x.astype(BF16)]
    res = jnp.dot(jnp.concatenate(parts, axis=0), seg, preferred_element_type=F32)
    out, at = [], 0
    for _, passes in items:
        piece = res[at * rows:(at + 1) * rows]
        if passes == 2:
            piece = piece + res[(at + 1) * rows:(at + 2) * rows]
        out.append(piece)
        at += passes
    return out


def _segb1(x, seg):
    return jnp.dot(x.astype(BF16), seg, preferred_element_type=F32)


@jax.custom_vjp
def segsum(x):
    return _segb(x, _seg_mat(x.shape[1]))


def _segsum_fwd(x):
    return segsum(x), None


def _segsum_bwd(_, ct):
    return (segsum(ct),)


segsum.defvjp(_segsum_fwd, _segsum_bwd)


@jax.custom_vjp
def bdot(a, b):
    return jnp.dot(a.astype(BF16), b.astype(BF16), preferred_element_type=F32)


def _bdot_fwd(a, b):
    return bdot(a, b), (a, b)


def _bdot_bwd(res, ct):
    a, b = res
    ctb = ct.astype(BF16)
    da = lax.dot_general(ctb, b.astype(BF16), NT_DIMS, preferred_element_type=F32)
    db = lax.dot_general(a.astype(BF16), ctb, TN_DIMS, preferred_element_type=F32)
    return da.astype(a.dtype), db.astype(b.dtype)


bdot.defvjp(_bdot_fwd, _bdot_bwd)


def _sig(x):
    return 1.0 / (1.0 + jnp.exp(-x))


def _softplus(z):
    return jnp.maximum(z, 0.0) + jnp.log(1.0 + jnp.exp(-jnp.abs(z)))


def _gelu_tanh(x):
    return 0.5 * x * (1.0 + jnp.tanh(0.7978845608028654 * (x + 0.044715 * (x * x * x))))


def _rms(x, g):
    return x * lax.rsqrt(jnp.mean(x * x, axis=-1, keepdims=True) + NORM_EPS) * g


def _shift_down(x, prev8, n):
    rolled = pltpu.roll(x, n, 0)
    top = pltpu.roll(prev8, n, 0)
    rid = lax.broadcasted_iota(jnp.int32, (8, x.shape[1]), 0)
    head = jnp.where(rid < n, top, rolled[:8])
    return jnp.concatenate([head, rolled[8:]], axis=0)


def _shift_up(x, next8, n):
    rows = x.shape[0]
    rolled = pltpu.roll(x, rows - n, 0)
    bottom = pltpu.roll(next8, 8 - n, 0)
    rid = lax.broadcasted_iota(jnp.int32, (8, x.shape[1]), 0)
    tail = jnp.where(rid >= 8 - n, bottom, rolled[rows - 8:])
    return jnp.concatenate([rolled[:rows - 8], tail], axis=0)


def tile_call(name, fn, grid, ins, outs, scratch=()):
    n_in, n_out = len(ins), len(outs)
    acc_axes = [o[4] for o in outs]

    def body(*refs):
        pids = tuple(pl.program_id(a) for a in range(len(grid)))
        vals = fn(pids, *[r[...] for r in refs[:n_in]], *refs[n_in + n_out:])
        if not isinstance(vals, (tuple, list)):
            vals = (vals,)
        for o_ref, val, ax in zip(refs[n_in:n_in + n_out], vals, acc_axes):
            if ax is None:
                o_ref[...] = val.astype(o_ref.dtype)
            else:
                @pl.when(pids[ax] == 0)
                def _(o_ref=o_ref):
                    o_ref[...] = jnp.zeros_like(o_ref)

                o_ref[...] += val.astype(o_ref.dtype)

    res = pl.pallas_call(
        body, name=name, grid=grid,
        in_specs=[pl.BlockSpec(b, im) for _, b, im in ins],
        out_specs=[pl.BlockSpec(o[2], o[3]) for o in outs],
        out_shape=[jax.ShapeDtypeStruct(o[0], o[1]) for o in outs],
        scratch_shapes=[pltpu.VMEM(s, d) for s, d in scratch],
        compiler_params=_cparams(len(grid)),
    )(*[a for a, _, _ in ins])
    return res


def _rows(a, tm):
    return (a, (tm, a.shape[1]), lambda i: (i, 0))


def _par(a):
    return (a, a.shape, lambda i: (0, 0))


def _row_out(T, C, dtype, tm):
    return ((T, C), dtype, (tm, C), lambda i: (i, 0), None)


def _acc_out(R, C):
    return ((R, C), F32, (R, C), lambda i: (0, 0), 0)


def _prev_halo(a, tm, C):
    return (a, (8, C), lambda i: (jnp.maximum(i * (tm // 8) - 1, 0), 0))


def _next_halo(a, tm, C, T):
    return (a, (8, C), lambda i: (jnp.minimum((i + 1) * (tm // 8), T // 8 - 1), 0))


def _pick(n, target):
    for t in (target, 2048, 1536, 1024, 768, 512, 384, 256, 128):
        if t <= target and n % t == 0:
            return t
    return n


def matmul(name, a, b, mode="nn", res=None, out_dtype=F32, tm=1024, tn=2048, tk=2048):
    if mode == "nn":
        (M, K), (K2, N) = a.shape, b.shape
    elif mode == "tn":
        (K, M), (K2, N) = a.shape, b.shape
    else:
        (M, K), (N, K2) = a.shape, b.shape
    assert K == K2, (name, a.shape, b.shape, mode)
    tm, tn, tk = _pick(M, tm), _pick(N, tn), _pick(K, tk)
    nk = K // tk
    dims = {"nn": NN_DIMS, "tn": TN_DIMS, "nt": NT_DIMS}[mode]
    a_spec = {"nn": pl.BlockSpec((tm, tk), lambda i, j, k: (i, k)),
              "tn": pl.BlockSpec((tk, tm), lambda i, j, k: (k, i)),
              "nt": pl.BlockSpec((tm, tk), lambda i, j, k: (i, k))}[mode]
    b_spec = {"nn": pl.BlockSpec((tk, tn), lambda i, j, k: (k, j)),
              "tn": pl.BlockSpec((tk, tn), lambda i, j, k: (k, j)),
              "nt": pl.BlockSpec((tn, tk), lambda i, j, k: (j, k))}[mode]
    has_res = res is not None

    def body(*refs):
        if has_res:
            a_ref, b_ref, r_ref, o_ref, acc_ref = refs
        else:
            a_ref, b_ref, o_ref, acc_ref = refs
        k = pl.program_id(2)

        @pl.when(k == 0)
        def _():
            acc_ref[...] = jnp.zeros_like(acc_ref)

        acc_ref[...] += lax.dot_general(a_ref[...].astype(BF16), b_ref[...].astype(BF16), dims,
                                        preferred_element_type=F32)

        @pl.when(k == nk - 1)
        def _():
            out = acc_ref[...]
            if has_res:
                out = out + r_ref[...].astype(F32)
            o_ref[...] = out.astype(o_ref.dtype)

    in_specs = [a_spec, b_spec]
    args = [a, b]
    if has_res:
        in_specs.append(pl.BlockSpec((tm, tn), lambda i, j, k: (i, j)))
        args.append(res)
    return pl.pallas_call(
        body, name=name, grid=(M // tm, N // tn, nk),
        in_specs=in_specs,
        out_specs=pl.BlockSpec((tm, tn), lambda i, j, k: (i, j)),
        out_shape=jax.ShapeDtypeStruct((M, N), out_dtype),
        scratch_shapes=[pltpu.VMEM((tm, tn), F32)],
        compiler_params=_cparams(3),
    )(*args)


def rw_pre(Pc, Ps, mu, w0, w_up, a0, a_up, g_up, k_k, k_a):
    Pm = Pc + (Ps - Pc) * mu
    r, k, v = Pm[:, 0:512], Pm[:, 512:1024], Pm[:, 1024:1536]
    xw, xa, xg = Pm[:, OFF_XW:OFF_XA], Pm[:, OFF_XA:OFF_XG], Pm[:, OFF_XG:RW_PAD]
    w = -_softplus(-(w0 + bdot(jnp.tanh(xw), w_up))) - 0.5
    decay = jnp.exp(-jnp.exp(w))
    a = _sig(a0 + bdot(xa, a_up))
    g = bdot(_sig(xg), g_up)
    kk = k * k_k
    kk = kk / jnp.maximum(jnp.sqrt(segsum(kk * kk)), 1e-12)
    k2 = k * (1.0 + (a - 1.0) * k_a)
    return r, decay, k2, v, -kk, kk * a, g


def rw_post(y, r, k2, v, g, ln_g, ln_b, r_k):
    mean = segsum(y) * (1.0 / HEAD)
    d = y - mean
    var = segsum(d * d) * (1.0 / HEAD)
    yn = d * lax.rsqrt(var + RW_LN_EPS) * ln_g + ln_b
    bonus = segsum(r * k2 * r_k) * v
    return (yn + bonus) * g


def att_combine(o1, o2, o3, l1, l2, l3):
    m = jnp.maximum(jnp.maximum(l1, l2), l3)
    e1, e2, e3 = jnp.exp(l1 - m), jnp.exp(l2 - m), jnp.exp(l3 - m)
    return (e1 * o1 + e2 * o2 + e3 * o3) / (e1 + e2 + e3)


def merge_fn(gp, bg, za, zb):
    s = _sig(gp + bg)
    half = za.shape[1]
    return s[:, :half] * za + s[:, half:] * zb


def tail_loss(x2, zg, pe, g_final, target):
    x3 = x2 + _sig(zg) * pe
    y = _rms(x3, g_final)
    err = (y - target) * (y - target)
    return 0.5 * jnp.sum(jnp.mean(err, axis=-1, keepdims=True))


SCAN_CHUNK = HEAD
SCAN_LANES = 256
SCAN_UNROLL_FWD, SCAN_UNROLL_BWD = 8, 8


def _to_head_time(z):
    T = z.shape[0]
    return z.reshape(T // HEAD, HEAD, RW_WIDTH // HEAD, HEAD).transpose(0, 3, 2, 1).reshape(T // HEAD, HEAD, RW_WIDTH)


def _from_head_time(zt):
    C = zt.shape[0]
    return zt.reshape(C, HEAD, RW_WIDTH // HEAD, HEAD).transpose(0, 3, 2, 1).reshape(C * HEAD, RW_WIDTH)


def _unrolled_loop(n, step, init, unroll):
    def body(i, carry):
        for j in range(unroll):
            carry = step(i * unroll + j, carry)
        return carry

    return lax.fori_loop(0, n // unroll, body, init)


def _lane_groups():
    return [slice(j * SCAN_LANES, (j + 1) * SCAN_LANES) for j in range(RW_WIDTH // SCAN_LANES)]


def scan_pair_terms(a, w, b, k, tm=256):
    T = a.shape[0]

    def fn(pid, a_t, nxt, w_t, b_t, k_t):
        a_next = _shift_up(a_t, jnp.where(pid[0] < T // tm - 1, nxt, 0.0), 1)
        return w_t * a_next, segsum(b_t * a_next), segsum(k_t * a_next)

    return tile_call("scan_pair_terms", fn, (T // tm,),
                     [_rows(a, tm), _next_halo(a, tm, RW_WIDTH, T), _rows(w, tm), _rows(b, tm), _rows(k, tm)],
                     [_row_out(T, RW_WIDTH, F32, tm)] * 3)


def rwkv_scan_fwd(a, w, b, k, r, vT, wa, ba, ka, exchange=()):
    T = a.shape[0]
    C, LW = SCAN_CHUNK, SCAN_LANES
    nC = T // C
    nx = len(exchange)

    def body(*refs):
        a_ref, w_ref, b_ref, k_ref, r_ref, vT_ref, wa_ref, ba_ref, ka_ref = refs[:9]
        x_refs, refs = refs[9:9 + nx], refs[9 + nx:]
        yT_ref, S_ref, saT_ref = refs[:3]
        land_refs, refs = refs[3:3 + nx], refs[3 + nx:]
        st_ref, vb0_ref, vb1_ref, seg_ref = refs[:4]
        if nx:
            start, wait = _exchange_ops([c for _, c in exchange], x_refs, land_refs, *refs[4:])

        @pl.when(pl.program_id(0) == 0)
        def _():
            st_ref[...] = jnp.zeros_like(st_ref)
            seg_ref[...] = _seg_mat(LW)
            if nx:
                start()

        seg = seg_ref[...]
        lane = jnp.bitwise_and(lax.broadcasted_iota(jnp.int32, (1, LW), 1), HEAD - 1)
        groups = _lane_groups()

        def vsel(t, gsl):
            return jnp.where(lane == t, vT_ref[0, :, gsl], 0.0)

        first = _segb_stack([(vsel(s, gsl), 1) for gsl in groups for s in (0, 1)], seg)
        for g, gsl in enumerate(groups):
            vb0_ref[:, gsl] = first[2 * g]
            vb1_ref[:, gsl] = first[2 * g + 1]
        saT_ref[...] = jnp.zeros_like(saT_ref)

        def pair(i, yacc):
            t = 2 * i
            t1 = t + 1
            tp = jnp.maximum(t - 1, 0)
            row = lambda ref, s, gsl: ref[pl.ds(s, 1), gsl]
            Sps = [st_ref[:, gsl] for gsl in groups]
            chain = _segb_stack([(Sp * row(ref, t, gsl), 2) for gsl, Sp in zip(groups, Sps) for ref in (a_ref, wa_ref)],
                                seg)
            sas, us = chain[0::2], chain[1::2]
            S1s = []
            for gsl, Sp, sa, u in zip(groups, Sps, sas, us):
                vb0, vb1 = vb0_ref[:, gsl], vb1_ref[:, gsl]
                S1 = Sp * row(w_ref, t, gsl) + sa * row(b_ref, t, gsl) + vb0 * row(k_ref, t, gsl)
                sa1 = u + sa * row(ba_ref, t, gsl) + vb0 * row(ka_ref, t, gsl)
                st_ref[:, gsl] = S1 * row(w_ref, t1, gsl) + sa1 * row(b_ref, t1, gsl) + vb1 * row(k_ref, t1, gsl)
                S_ref[0, t, :, gsl] = Sp
                S_ref[0, t1, :, gsl] = S1
                S1s.append(S1)
                saT_ref[0, :, gsl] = jnp.where(lane == t, sa, jnp.where(lane == t1, sa1, saT_ref[0, :, gsl]))
            side = _segb_stack([(x, 1) for gsl, Sp, S1 in zip(groups, Sps, S1s)
                                for x in (Sp * row(r_ref, tp, gsl), S1 * row(r_ref, t, gsl),
                                          vsel(t + 2, gsl), vsel(t + 3, gsl))], seg)
            out = []
            for g, (gsl, ya) in enumerate(zip(groups, yacc)):
                yb0, yb1, vb0_ref[:, gsl], vb1_ref[:, gsl] = side[4 * g:4 * g + 4]
                out.append(jnp.where(lane == t, yb1, jnp.where(lane == t - 1, yb0, ya)))
            return tuple(out)

        yacc = _unrolled_loop(C // 2, pair, tuple(jnp.zeros((HEAD, LW), F32) for _ in groups), SCAN_UNROLL_FWD)
        for gsl, ya in zip(groups, yacc):
            S_last = st_ref[:, gsl]
            S_ref[0, C, :, gsl] = S_last
            yb = _segb1(S_last * r_ref[pl.ds(C - 1, 1), gsl], seg)
            yT_ref[0, :, gsl] = jnp.where(lane == C - 1, yb, ya)

        if nx:
            @pl.when(pl.program_id(0) == nC - 1)
            def _():
                wait()

    row = pl.BlockSpec((C, RW_WIDTH), lambda c: (c, 0))
    ht = pl.BlockSpec((1, HEAD, RW_WIDTH), lambda c: (c, 0, 0))
    hbm = pl.BlockSpec(memory_space=pl.ANY)
    res = pl.pallas_call(
        body, name="rwkv_scan_fwd", grid=(nC,),
        in_specs=[row, row, row, row, row, ht, row, row, row] + [hbm] * nx,
        out_specs=[ht, pl.BlockSpec((1, C + 1, HEAD, RW_WIDTH), lambda c: (c, 0, 0, 0)), ht] + [hbm] * nx,
        out_shape=[jax.ShapeDtypeStruct((nC, HEAD, RW_WIDTH), F32),
                   jax.ShapeDtypeStruct((nC, C + 1, HEAD, RW_WIDTH), F32),
                   jax.ShapeDtypeStruct((nC, HEAD, RW_WIDTH), F32)] + _exchange_shapes(exchange),
        scratch_shapes=[pltpu.VMEM((HEAD, RW_WIDTH), F32)] * 3 + [pltpu.VMEM((LW, LW), BF16)]
        + (_exchange_sems(nx) if nx else []),
        compiler_params=pltpu.CompilerParams(dimension_semantics=("arbitrary",), vmem_limit_bytes=VMEM_LIMIT_BYTES,
                                             has_side_effects=bool(nx)),
    )(a, w, b, k, r, vT, wa, ba, ka, *[z for z, _ in exchange])
    return res[:3], res[3:]


def rwkv_scan_bwd(a, w, b, k, r, v, dy, S_all, saT, exchange=()):
    T = a.shape[0]
    C, LW = SCAN_CHUNK, SCAN_LANES
    nC = T // C
    nx = len(exchange)
    n_heads = RW_WIDTH // HEAD
    dyT = _to_head_time(dy).astype(BF16)
    v_rows, dy_rows = v.reshape(T, n_heads, HEAD), dy.reshape(T, n_heads, HEAD)
    sa_rows = _from_head_time(saT).reshape(T, n_heads, HEAD)

    def body(*refs):
        a_ref, w_ref, b_ref, k_ref, r_ref, vR_ref, saR_ref, dyR_ref, dyT_ref, S_ref = refs[:10]
        x_refs, refs = refs[10:10 + nx], refs[10 + nx:]
        da_ref, dw_ref, db_ref, dk_ref, dr_ref, dvT_ref = refs[:6]
        land_refs, refs = refs[6:6 + nx], refs[6 + nx:]
        ds_ref, dyb_ref, seg_ref = refs[:3]
        if nx:
            start, wait = _exchange_ops([c for _, c in exchange], x_refs, land_refs, *refs[3:])

        @pl.when(pl.program_id(0) == 0)
        def _():
            ds_ref[...] = jnp.zeros_like(ds_ref)
            seg_ref[...] = _seg_mat(LW)
            if nx:
                start()

        seg = seg_ref[...]
        lane = jnp.bitwise_and(lax.broadcasted_iota(jnp.int32, (1, LW), 1), HEAD - 1)
        groups = _lane_groups()
        head_row = lax.broadcasted_iota(jnp.int32, (n_heads, LW), 0)
        lane_head = lax.shift_right_logical(lax.broadcasted_iota(jnp.int32, (n_heads, LW), 1), 6)

        def colsum(z):
            return jnp.sum(z, axis=0, keepdims=True)

        def dysel(t, gsl):
            return jnp.where(lane == t, dyT_ref[0, :, gsl], 0.0)

        for gsl, dyb in zip(groups, _segb_stack([(dysel(C - 1, gsl), 1) for gsl in groups], seg)):
            dyb_ref[:, gsl] = dyb

        def step(i, dvacc):
            t = C - 1 - i
            dybs = [dyb_ref[:, gsl] for gsl in groups]
            dSs = [ds_ref[:, gsl] + dyb * r_ref[pl.ds(t, 1), gsl] for gsl, dyb in zip(groups, dybs)]
            dsabs = _segb_stack([(dS * b_ref[pl.ds(t, 1), gsl], 2) for gsl, dS in zip(groups, dSs)], seg)
            for gsl, dS, dsab in zip(groups, dSs, dsabs):
                ds_ref[:, gsl] = dS * w_ref[pl.ds(t, 1), gsl] + dsab * a_ref[pl.ds(t, 1), gsl]
            out = []
            dy_rows = dyR_ref[t].astype(BF16)
            v_sa_rows = jnp.concatenate([vR_ref[t], saR_ref[t]], axis=0).astype(BF16)
            side = _segb_stack([(x, 1) for gsl, dS in zip(groups, dSs)
                                for x in (dS * k_ref[pl.ds(t, 1), gsl], dysel(t - 1, gsl))], seg)
            for g, (gsl, dva, dS, dsab) in enumerate(zip(groups, dvacc, dSs, dsabs)):
                dvb, dyb_ref[:, gsl] = side[2 * g:2 * g + 2]
                Sp = S_ref[0, t, :, gsl]
                own = head_row == lane_head + g * (LW // HEAD)

                def rows_in(rows, mat):
                    full = jnp.dot(rows, mat.astype(BF16), preferred_element_type=F32)
                    return [jnp.sum(jnp.where(own, full[s:s + n_heads], 0.0), axis=0, keepdims=True)
                            for s in range(0, rows.shape[0], n_heads)]

                (dr,) = rows_in(dy_rows, S_ref[0, t + 1, :, gsl])
                dk, db = rows_in(v_sa_rows, dS)
                dr_ref[pl.ds(t, 1), gsl] = dr
                dk_ref[pl.ds(t, 1), gsl] = dk
                db_ref[pl.ds(t, 1), gsl] = db
                dw_ref[pl.ds(t, 1), gsl] = colsum(dS * Sp)
                da_ref[pl.ds(t, 1), gsl] = colsum(Sp * dsab)
                out.append(jnp.where(lane == t, dvb, dva))
            return tuple(out)

        dvacc = _unrolled_loop(C, step, tuple(jnp.zeros((HEAD, LW), F32) for _ in groups), SCAN_UNROLL_BWD)
        for gsl, dva in zip(groups, dvacc):
            dvT_ref[0, :, gsl] = dva

        if nx:
            @pl.when(pl.program_id(0) == nC - 1)
            def _():
                wait()

    row = pl.BlockSpec((C, RW_WIDTH), lambda c: (nC - 1 - c, 0))
    ht = pl.BlockSpec((1, HEAD, RW_WIDTH), lambda c: (nC - 1 - c, 0, 0))
    hbm = pl.BlockSpec(memory_space=pl.ANY)
    per_head = pl.BlockSpec((C, n_heads, HEAD), lambda c: (nC - 1 - c, 0, 0))
    rows_shape = jax.ShapeDtypeStruct((T, RW_WIDTH), F32)
    res = pl.pallas_call(
        body, name="rwkv_scan_bwd", grid=(nC,),
        in_specs=[row, row, row, row, row, per_head, per_head, per_head, ht,
                  pl.BlockSpec((1, C + 1, HEAD, RW_WIDTH), lambda c: (nC - 1 - c, 0, 0, 0))] + [hbm] * nx,
        out_specs=[row, row, row, row, row, ht] + [hbm] * nx,
        out_shape=[rows_shape] * 5 + [jax.ShapeDtypeStruct((nC, HEAD, RW_WIDTH), F32)] + _exchange_shapes(exchange),
        scratch_shapes=[pltpu.VMEM((HEAD, RW_WIDTH), F32), pltpu.VMEM((HEAD, RW_WIDTH), F32),
                        pltpu.VMEM((LW, LW), BF16)] + (_exchange_sems(nx) if nx else []),
        compiler_params=pltpu.CompilerParams(dimension_semantics=("arbitrary",), vmem_limit_bytes=VMEM_LIMIT_BYTES,
                                             has_side_effects=bool(nx)),
    )(a, w, b, k, r, v_rows, sa_rows, dy_rows, dyT, S_all, *[z for z, _ in exchange])
    return res[:6], res[6:]


def _alibi_slope(h):
    return float(np.float32(2.0 ** (-8.0 * (h + 1) / ATT_HEADS)))


ATT_GROUP_HEADS = 4


def _stack_heads(x, lane_head, fill=0.0):
    return jnp.concatenate([jnp.where(lane_head == hh, x, fill) for hh in range(ATT_GROUP_HEADS)], axis=0)


def _unstack_heads(x, lane_head, L):
    out = jnp.zeros((L, x.shape[1]), F32)
    for hh in range(ATT_GROUP_HEADS):
        out = jnp.where(lane_head == hh, x[hh * L:(hh + 1) * L], out)
    return out


def _att_logits(qs, kcat, gi, d, L, n):
    qi = lax.broadcasted_iota(jnp.int32, (L, 2 * L), 0)
    kj = lax.broadcasted_iota(jnp.int32, (L, 2 * L), 1)
    steps = qi + L - kj
    valid = (steps >= 0) & (steps <= L) & ((kj >= L) | (n > 0))
    dist = (d * steps).astype(F32)
    bias = jnp.concatenate([jnp.where(valid, -_alibi_slope(gi * ATT_GROUP_HEADS + hh) * dist, NEG_BIG)
                            for hh in range(ATT_GROUP_HEADS)], axis=0)
    s = lax.dot_general(qs.astype(BF16), kcat, NT_DIMS, preferred_element_type=F32) * (HEAD ** -0.5)
    return jnp.where(bias > 0.5 * NEG_BIG, s + bias, NEG_BIG)


def att_fwd(pa, gi, T):
    window, d = ATT_GROUPS[gi]
    L = window // d
    Tj = T // d
    nb = Tj // L
    pv = pa.reshape(Tj, d * ATT_COLS)
    nblk = ATT_COLS // ATT_OUT

    def fn(pids, q, kp, kc, vp, vc):
        lane_head = lax.shift_right_logical(lax.broadcasted_iota(jnp.int32, (1, ATT_OUT), 1), 6)
        kcat = jnp.concatenate([kp, kc], axis=0).astype(BF16)
        vcat = jnp.concatenate([vp, vc], axis=0).astype(BF16)
        s = _att_logits(_stack_heads(q, lane_head), kcat, gi, d, L, pids[1])
        m = jnp.max(s, axis=-1, keepdims=True)
        p = jnp.exp(s - m)
        l = jnp.sum(p, axis=-1, keepdims=True)
        o = jnp.dot(p.astype(BF16), vcat, preferred_element_type=F32) / l
        lse = jnp.broadcast_to(m + jnp.log(l), o.shape)
        return _unstack_heads(o, lane_head, L), _unstack_heads(lse, lane_head, L)

    blk = (L, ATT_OUT)
    ins = [(pv, blk, lambda r, n: (n, r * nblk + gi)),
           (pv, blk, lambda r, n: (jnp.maximum(n - 1, 0), r * nblk + 3 + gi)),
           (pv, blk, lambda r, n: (n, r * nblk + 3 + gi)),
           (pv, blk, lambda r, n: (jnp.maximum(n - 1, 0), r * nblk + 6 + gi)),
           (pv, blk, lambda r, n: (n, r * nblk + 6 + gi))]
    out = ((Tj, d * ATT_OUT), F32, blk, lambda r, n: (n, r), None)
    o, lseb = tile_call(f"att_fwd_g{gi}", fn, (d, nb), ins, [out, out])
    return o.reshape(T, ATT_OUT), lseb.reshape(T, ATT_OUT)


def att_bwd(pa, o, lseb, do, dlseb, gi, T):
    window, d = ATT_GROUPS[gi]
    L = window // d
    Tj = T // d
    nb = Tj // L
    pv = pa.reshape(Tj, d * ATT_COLS)
    nblk = ATT_COLS // ATT_OUT
    view = lambda z: z.reshape(Tj, d * ATT_OUT)

    def body(q_ref, kp_ref, kc_ref, vp_ref, vc_ref, o_ref, l_ref, do_ref, dl_ref, dq_ref, dk_ref, dv_ref):
        n = pl.program_id(1)

        @pl.when(n == 0)
        def _():
            dk_ref[...] = jnp.zeros_like(dk_ref)
            dv_ref[...] = jnp.zeros_like(dv_ref)

        lane_head = lax.shift_right_logical(lax.broadcasted_iota(jnp.int32, (1, ATT_OUT), 1), 6)
        kcat = jnp.concatenate([kp_ref[...], kc_ref[...]], axis=0).astype(BF16)
        vcat = jnp.concatenate([vp_ref[...], vc_ref[...]], axis=0).astype(BF16)
        qs = _stack_heads(q_ref[...], lane_head)
        dos = _stack_heads(do_ref[...], lane_head)
        lse = jnp.max(_stack_heads(l_ref[...], lane_head, NEG_BIG), axis=-1, keepdims=True)
        dlse = jnp.sum(_stack_heads(dl_ref[...], lane_head), axis=-1, keepdims=True)
        delta = jnp.sum(dos * jnp.concatenate([o_ref[...]] * ATT_GROUP_HEADS, axis=0), axis=-1, keepdims=True)
        p = jnp.exp(_att_logits(qs, kcat, gi, d, L, n) - lse)
        dp = lax.dot_general(dos.astype(BF16), vcat, NT_DIMS, preferred_element_type=F32)
        ds = (p * (dp - delta + dlse)).astype(BF16)
        dq = _unstack_heads(jnp.dot(ds, kcat, preferred_element_type=F32), lane_head, L)
        dkc = lax.dot_general(ds, qs.astype(BF16), TN_DIMS, preferred_element_type=F32)
        dvc = lax.dot_general(p.astype(BF16), dos.astype(BF16), TN_DIMS, preferred_element_type=F32)
        scale = HEAD ** -0.5
        dq_ref[...] = dq * scale
        cur = pl.ds(pl.multiple_of(n * L, L), L)
        dk_ref[cur, :] += dkc[L:] * scale
        dv_ref[cur, :] += dvc[L:]

        @pl.when(n > 0)
        def _():
            prev = pl.ds(pl.multiple_of((n - 1) * L, L), L)
            dk_ref[prev, :] += dkc[:L] * scale
            dv_ref[prev, :] += dvc[:L]

    blk = pl.BlockSpec((L, ATT_OUT), lambda r, n: (n, r))
    res = pl.BlockSpec((Tj, ATT_OUT), lambda r, n: (0, r))
    qspec = lambda off, prev: pl.BlockSpec(
        (L, ATT_OUT), (lambda r, n: (jnp.maximum(n - 1, 0), r * nblk + off + gi)) if prev
        else (lambda r, n: (n, r * nblk + off + gi)))
    shape = jax.ShapeDtypeStruct((Tj, d * ATT_OUT), F32)
    dq, dk, dv = pl.pallas_call(
        body, name=f"att_bwd_g{gi}", grid=(d, nb),
        in_specs=[qspec(0, False), qspec(3, True), qspec(3, False), qspec(6, True), qspec(6, False),
                  blk, blk, blk, blk],
        out_specs=[blk, res, res],
        out_shape=[shape, shape, shape],
        compiler_params=_cparams(2),
    )(pv, pv, pv, pv, pv, view(o), view(lseb), view(do), view(dlseb))
    return dq.reshape(T, ATT_OUT), dk.reshape(T, ATT_OUT), dv.reshape(T, ATT_OUT)


FFN_TM, FFN_TC = 512, 512


def _conv3(u, prev8, cw, cb):
    return cb + cw[0:1] * u + cw[1:2] * _shift_down(u, prev8, 1) + cw[2:3] * _shift_down(u, prev8, 2)


def conv_glu_fwd(u, conv_w, conv_b):
    T = u.shape[0]
    tm, tc = FFN_TM, FFN_TC
    nj, ni = D_FF // tc, T // tm

    def fn(pids, ug, ugh, uv, uvh, cwg, cbg, cwv, cbv):
        first = pids[1] > 0
        cg = _conv3(ug, jnp.where(first, ugh, 0.0), cwg, cbg)
        cv = _conv3(uv, jnp.where(first, uvh, 0.0), cwv, cbv)
        return _gelu_tanh(cg) * cv

    halo = lambda off: (lambda j, i: (jnp.maximum(i * (tm // 8) - 1, 0), j + off))
    ins = [(u, (tm, tc), lambda j, i: (i, j)), (u, (8, tc), halo(0)),
           (u, (tm, tc), lambda j, i: (i, j + nj)), (u, (8, tc), halo(nj)),
           (conv_w, (3, tc), lambda j, i: (0, j)), (conv_b, (1, tc), lambda j, i: (0, j)),
           (conv_w, (3, tc), lambda j, i: (0, j + nj)), (conv_b, (1, tc), lambda j, i: (0, j + nj))]
    out = ((T, D_FF), BF16, (tm, tc), lambda j, i: (i, j), None)
    return tile_call("conv_glu_fwd", fn, (nj, ni), ins, [out])[0]


def conv_glu_bwd(u, conv_w, conv_b, df):
    T = u.shape[0]
    tm, tc = FFN_TM, FFN_TC
    nj, ni = D_FF // tc, T // tm

    def fn(pids, ug, ugh, uv, uvh, cwg, cbg, cwv, cbv, df_t, nxt_g, nxt_v):
        i = ni - 1 - pids[1]
        ugh = jnp.where(i > 0, ugh, 0.0)
        uvh = jnp.where(i > 0, uvh, 0.0)
        cg = _conv3(ug, ugh, cwg, cbg)
        cv = _conv3(uv, uvh, cwv, cbv)
        _, vjp = jax.vjp(lambda g_, v_: _gelu_tanh(g_) * v_, cg, cv)
        dcg, dcv = vjp(df_t.astype(F32))
        cs = lambda z: jnp.sum(z, axis=0, keepdims=True)

        @pl.when(pids[1] == 0)
        def _():
            nxt_g[...] = jnp.zeros_like(nxt_g)
            nxt_v[...] = jnp.zeros_like(nxt_v)

        outs = []
        for dc, cw, nxt_ref in ((dcg, cwg, nxt_g), (dcv, cwv, nxt_v)):
            nxt = nxt_ref[...]
            outs.append(cw[0:1] * dc + cw[1:2] * _shift_up(dc, nxt, 1) + cw[2:3] * _shift_up(dc, nxt, 2))
            nxt_ref[...] = dc[:8]
        for dc, uu, hh in ((dcg, ug, ugh), (dcv, uv, uvh)):
            outs += [cs(dc * uu), cs(dc * _shift_down(uu, hh, 1)), cs(dc * _shift_down(uu, hh, 2)), cs(dc)]
        return outs

    rows = lambda off: (lambda j, r: (ni - 1 - r, j + off))
    halo = lambda off: (lambda j, r: (jnp.maximum((ni - 1 - r) * (tm // 8) - 1, 0), j + off))
    ins = [(u, (tm, tc), rows(0)), (u, (8, tc), halo(0)),
           (u, (tm, tc), rows(nj)), (u, (8, tc), halo(nj)),
           (conv_w, (3, tc), lambda j, r: (0, j)), (conv_b, (1, tc), lambda j, r: (0, j)),
           (conv_w, (3, tc), lambda j, r: (0, j + nj)), (conv_b, (1, tc), lambda j, r: (0, j + nj)),
           (df, (tm, tc), rows(0))]
    big = ((T, D_FF), BF16, (tm, tc), rows(0), None)
    acc = ((1, D_FF), F32, (1, tc), lambda j, r: (0, j), 1)
    res = tile_call("conv_glu_bwd", fn, (nj, ni), ins, [big, big] + [acc] * 8,
                    scratch=[((8, tc), F32), ((8, tc), F32)])
    dconv_w = jnp.concatenate([jnp.concatenate([res[2 + j], res[6 + j]], axis=1) for j in range(3)], axis=0)
    dconv_b = jnp.concatenate([res[5], res[9]], axis=1)
    return res[0], res[1], dconv_w, dconv_b


def _pad_cols(w, total):
    return jnp.pad(w, ((0, 0), (0, total - w.shape[1])))


def _pad_rows(w, total):
    return jnp.pad(w, ((0, total - w.shape[0]), (0, 0)))


def _proj_pad(w):
    z = lambda n: jnp.zeros((w.shape[0], n), w.dtype)
    return jnp.concatenate([w[:, :1600], z(64), w[:, 1600:1664], z(64), w[:, 1664:1824], z(96), w[:, 1824:],
                            z(PROJ_TAIL)], axis=1)


def _proj_unpad(g):
    return jnp.concatenate([g[:, :1600], g[:, OFF_XA:OFF_XA + 64], g[:, OFF_XG:OFF_XG + 160],
                            g[:, RW_PAD:RW_PAD + ATT_COLS]], axis=1)


def _rw_unpad(g):
    return jnp.concatenate([g[:, :1600], g[:, OFF_XA:OFF_XA + 64], g[:, OFF_XG:OFF_XG + 160]], axis=1)


def rms_fwd(name, x, g, tm=256):
    T, D = x.shape
    return tile_call(name, lambda pid, x_t, g_t: _rms(x_t, g_t), (T // tm,),
                     [_rows(x, tm), _par(g)], [_row_out(T, D, BF16, tm)])[0]


def rms_bwd(name, x, g, dh, dres, with_bf16=True, tm=256):
    T, D = x.shape
    out_dtypes = (F32, BF16) if with_bf16 else (F32,)

    def fn(pid, x_t, g_t, dh_t, dres_t):
        _, vjp = jax.vjp(_rms, x_t, g_t)
        dx, dg = vjp(dh_t.astype(F32))
        return (dres_t + dx,) * len(out_dtypes) + (dg,)

    return tile_call(name, fn, (T // tm,), [_rows(x, tm), _par(g), _rows(dh, tm), _rows(dres, tm)],
                     [_row_out(T, D, dt, tm) for dt in out_dtypes] + [_acc_out(1, D)])


def local_step(x, p, target, W):
    T, D = x.shape
    G = {}

    w_in_p = W["w_in_p"]
    mu_p = _proj_pad(_pad_cols(W["rw_mu"], 4128))[:, :RW_PAD]
    w_up_p = _pad_rows(W["rw_w_up"], 128)
    a_up_p = _pad_rows(W["rw_a_up"], 128)
    g_up_p = _pad_rows(W["rw_g_up"], 256)
    r_k = W["rw_r_k"].reshape(1, RW_WIDTH)
    rw_params = [mu_p, W["rw_w0"], w_up_p, W["rw_a0"], a_up_p, g_up_p, W["rw_k_k"], W["rw_k_a"]]

    h = rms_fwd("rms_mix", x, W["g_mix"])
    proj = matmul("proj_in_rw", h, w_in_p[:, :RW_PAD])
    pa = matmul("proj_in_att", h, w_in_p[:, RW_PAD:RW_PAD + ATT_COLS])
    gp = matmul("proj_gate", h, W["w_gate"])

    tm = 256
    rw_in = (proj, (tm, RW_PAD), lambda i: (i, 0))
    rw_halo = _prev_halo(proj, tm, RW_PAD)

    def rw_pre_tile(pid, Pc, halo, *params):
        prev8 = jnp.where(pid[0] > 0, halo, 0.0)
        params = [q.astype(F32) for q in params]
        return rw_pre(Pc, _shift_down(Pc, prev8, 1), *params)

    r, decay, k2, v, avec, bvec, g = tile_call(
        "rw_pre", rw_pre_tile, (T // tm,), [rw_in, rw_halo] + [_par(q) for q in rw_params],
        [_row_out(T, RW_WIDTH, F32, tm)] * 7)

    wa, ba, ka = scan_pair_terms(avec, decay, bvec, k2)
    vT = _to_head_time(v).astype(BF16)
    (yT, S_all, saT), late_slots = rwkv_scan_fwd(avec, decay, bvec, k2, r, vT, wa, ba, ka,
                                            exchange=_late_weight_sources(W))
    y = _from_head_time(yT)
    W = dict(W, **_late_weights(late_slots))

    post_params = [W["rw_ln_g"], W["rw_ln_b"], r_k]
    ya = tile_call("rw_post", lambda pid, *t: rw_post(*t), (T // tm,),
                   [_rows(z, tm) for z in (y, r, k2, v, g)] + [_par(q) for q in post_params],
                   [_row_out(T, RW_WIDTH, BF16, tm)])[0]

    att = [att_fwd(pa, gi, T) for gi in range(3)]
    o_l = [att[0][0], att[1][0], att[2][0], att[0][1], att[1][1], att[2][1]]
    yb = tile_call("att_combine", lambda pid, *t: att_combine(*t), (T // tm,),
                   [_rows(z, tm) for z in o_l], [_row_out(T, ATT_OUT, BF16, tm)])[0]

    za = matmul("branch_a", ya, W["w_branch_a"])
    zb = matmul("branch_b", yb, W["w_branch_b"])
    merged = tile_call("merge", lambda pid, *t: merge_fn(*t), (T // tm,),
                       [_rows(gp, tm), _par(W["b_gate"]), _rows(za, tm), _rows(zb, tm)],
                       [_row_out(T, D, BF16, tm)])[0]
    x1 = matmul("mix_out", merged, W["w_out"], res=x)

    h2 = rms_fwd("rms_ffn", x1, W["g_ffn"])
    u = matmul("ffn_up", h2, W["w_up"])
    f = conv_glu_fwd(u, W["conv_w"], W["conv_b"])
    x2 = matmul("ffn_down", f, W["w_down"], res=x1)

    h3 = rms_fwd("rms_ple", x2, W["g_ple"])
    zg = matmul("ple_gate", h3, W["w_ple_gate"])
    pe = matmul("ple_embed", p, W["w_ple"])

    def tail_tile(pid, x2_t, zg_t, pe_t, gf, tgt):
        loss, vjp = jax.vjp(lambda a_, b_, c_, d_: tail_loss(a_, b_, c_, d_, tgt), x2_t, zg_t, pe_t, gf)
        dx2, dzg, dpe, dgf = vjp(jnp.ones((), F32))
        return dx2, dzg, dpe, dgf, jnp.full((1, 128), loss, F32)

    tmt = 128
    dx3, dzg, dpe, dgf, loss_acc = tile_call(
        "tail_loss", tail_tile, (T // tmt,),
        [_rows(x2, tmt), _rows(zg, tmt), _rows(pe, tmt), _par(W["g_final"]), _rows(target, tmt)],
        [_row_out(T, D, F32, tmt), _row_out(T, D, BF16, tmt), _row_out(T, D, BF16, tmt),
         _acc_out(1, D), _acc_out(1, 128)])
    loss = loss_acc[0, 0]
    G["g_final"] = dgf

    wgrad = functools.partial(matmul, mode="tn", out_dtype=GRAD_WIRE)
    G["w_ple"] = wgrad("d_w_ple", p, dpe)
    G["w_ple_gate"] = wgrad("d_w_ple_gate", h3, dzg)
    dh3 = matmul("d_h3", dzg, W["w_ple_gate"], "nt")
    dx2, dx2b, G["g_ple"] = rms_bwd("rms_ple_bwd", x2, W["g_ple"], dh3, dx3)

    G["w_down"] = wgrad("d_w_down", f, dx2b)
    df = matmul("d_f", dx2b, W["w_down"], "nt", out_dtype=BF16)
    du_g, du_v, G["conv_w"], G["conv_b"] = conv_glu_bwd(u, W["conv_w"], W["conv_b"], df)
    du = jnp.concatenate([du_g, du_v], axis=1)
    G["w_up"] = wgrad("d_w_up", h2, du)
    dh2 = matmul("d_h2", du, W["w_up"], "nt")
    dx1, dx1b, G["g_ffn"] = rms_bwd("rms_ffn_bwd", x1, W["g_ffn"], dh2, dx2)

    G["w_out"] = wgrad("d_w_out", merged, dx1b)
    dmerged = matmul("d_merged", dx1b, W["w_out"], "nt", out_dtype=BF16)

    def merge_bwd_tile(pid, gp_t, bg, za_t, zb_t, dm_t):
        _, vjp = jax.vjp(merge_fn, gp_t, bg, za_t, zb_t)
        return vjp(dm_t.astype(F32))

    dgp, G["b_gate"], dza, dzb = tile_call(
        "merge_bwd", merge_bwd_tile, (T // tm,),
        [_rows(gp, tm), _par(W["b_gate"]), _rows(za, tm), _rows(zb, tm), _rows(dmerged, tm)],
        [_row_out(T, 2 * D, BF16, tm), _acc_out(1, 2 * D), _row_out(T, D, BF16, tm), _row_out(T, D, BF16, tm)])
    G["w_branch_a"] = wgrad("d_w_branch_a", ya, dza)
    dya = matmul("d_ya", dza, W["w_branch_a"], "nt")
    G["w_branch_b"] = wgrad("d_w_branch_b", yb, dzb)
    dyb = matmul("d_yb", dzb, W["w_branch_b"], "nt")
    G["w_gate"] = wgrad("d_w_gate", h, dgp)
    dh_gate = matmul("d_h_gate", dgp, W["w_gate"], "nt")

    def comb_bwd_tile(pid, *t):
        _, vjp = jax.vjp(att_combine, *t[:6])
        return vjp(t[6])

    d_ol = tile_call("att_combine_bwd", comb_bwd_tile, (T // tm,),
                     [_rows(z, tm) for z in o_l] + [_rows(dyb, tm)],
                     [_row_out(T, ATT_OUT, F32, tm)] * 6)
    dqkv = [att_bwd(pa, att[gi][0], att[gi][1], d_ol[gi], d_ol[3 + gi], gi, T) for gi in range(3)]
    d_att = [dqkv[gi][j] for j in range(3) for gi in range(3)]

    def post_bwd_tile(pid, *t):
        _, vjp = jax.vjp(rw_post, *t[:8])
        return vjp(t[8])

    dy, dr_p, dk2_p, dv_p, dg, G["rw_ln_g"], G["rw_ln_b"], d_rk = tile_call(
        "rw_post_bwd", post_bwd_tile, (T // tm,),
        [_rows(z, tm) for z in (y, r, k2, v, g)] + [_par(q) for q in post_params] + [_rows(dya, tm)],
        [_row_out(T, RW_WIDTH, F32, tm)] * 5 + [_acc_out(1, RW_WIDTH)] * 3)
    G["rw_r_k"] = d_rk.reshape(W["rw_r_k"].shape)

    (da, dw, db, dk_s, dr_s, dvT), G["_early_parts"] = rwkv_scan_bwd(
        avec, decay, bvec, k2, r, v, dy, S_all, saT, exchange=_early_grad_sources(G))
    dv_s = _from_head_time(dvT)

    tmb = 128
    rw_in_b = (proj, (tmb, RW_PAD), lambda i: (i, 0))

    def pre_bwd_tile(pid, Pc, halo, *t):
        prev8 = jnp.where(pid[0] > 0, halo, 0.0)
        params = [q.astype(F32) for q in t[:8]]
        dr1, dr2, dw_, dk1, dk2_, dv1, dv2, da_, db_, dg_ = t[8:]
        _, vjp = jax.vjp(rw_pre, Pc, _shift_down(Pc, prev8, 1), *params)
        return vjp((dr1 + dr2, dw_, dk1 + dk2_, dv1 + dv2, da_, db_, dg_))

    cts = (dr_s, dr_p, dw, dk_s, dk2_p, dv_s, dv_p, da, db, dg)
    res = tile_call(
        "rw_pre_bwd", pre_bwd_tile, (T // tmb,),
        [rw_in_b, _prev_halo(proj, tmb, RW_PAD)] + [_par(q) for q in rw_params] + [_rows(z, tmb) for z in cts],
        [_row_out(T, RW_PAD, F32, tmb)] * 2 + [_acc_out(*q.shape) for q in rw_params])
    dPc, dPs = res[0], res[1]
    d_mu, G["rw_w0"], d_wup, G["rw_a0"], d_aup, d_gup, G["rw_k_k"], G["rw_k_a"] = res[2:]
    G["rw_mu"] = _rw_unpad(d_mu)
    G["rw_w_up"], G["rw_a_up"], G["rw_g_up"] = d_wup[:64], d_aup[:64], d_gup[:160]

    def dproj_tile(pid, dPc_t, dPs_t, nxt, *att_t):
        nxt = jnp.where(pid[0] < T // tm - 1, nxt, 0.0)
        tail = jnp.zeros((dPc_t.shape[0], PROJ_TAIL), F32)
        return jnp.concatenate([dPc_t + _shift_up(dPs_t, nxt, 1)] + list(att_t) + [tail], axis=1)

    dproj = tile_call("d_proj", dproj_tile, (T // tm,),
                      [_rows(dPc, tm), _rows(dPs, tm), _next_halo(dPs, tm, RW_PAD, T)] + [_rows(z, tm) for z in d_att],
                      [_row_out(T, PROJ_PAD, BF16, tm)])[0]
    G["w_in_p"] = wgrad("d_w_in", h, dproj)
    dh = matmul("d_h", dproj, w_in_p, "nt", res=dh_gate)
    dx, G["g_mix"] = rms_bwd("rms_mix_bwd", x, W["g_mix"], dh, dx1, with_bf16=False)
    return loss, dx, G


def _mesh_pos():
    return lax.axis_index("x"), lax.axis_index("y"), lax.axis_index("c")


def _peer(pos, k):
    x, y, c = pos
    px = 1 - x if k & 4 else x
    py = 1 - y if k & 2 else y
    pc = 1 - c if k & 1 else c
    return (px, py, pc), 4 * px + 2 * py + pc


def all_gather_blocks(name, blocks):
    n = len(blocks)

    def body(*refs):
        x_refs, out_refs = refs[:n], refs[n:2 * n]
        send_sems, recv_sems, local_sems = refs[2 * n:]
        x, y, c = _mesh_pos()
        me, sibling = (x, y, c), (x, y, 1 - c)
        chips = [(1 - x, y), (x, 1 - y), (1 - x, 1 - y)]
        ops = range(n)

        def slot(i, px, py, pc):
            return out_refs[i].at[4 * px + 2 * py + pc]

        def copy(k, i, block, to, own=False):
            return pltpu.make_async_remote_copy(
                src_ref=x_refs[i] if own else slot(i, *block), dst_ref=slot(i, *block),
                send_sem=send_sems.at[k, i], recv_sem=recv_sems.at[k, i],
                device_id=to, device_id_type=pl.DeviceIdType.MESH)

        mine = [pltpu.make_async_copy(x_refs[i], slot(i, *me), local_sems.at[i]) for i in ops]
        first = [copy(0, i, me, sibling, own=True) for i in ops]
        first += [copy(1 + j, i, me, (*chip, c), own=True) for j, chip in enumerate(chips) for i in ops]
        for cp in mine + first:
            cp.start()
        passed = []
        for j, chip in enumerate(chips):
            for i in ops:
                copy(1 + j, i, (*chip, c), me).wait_recv()
                passed.append(copy(4 + j, i, (*chip, c), sibling))
                passed[-1].start()
        for i in ops:
            copy(0, i, sibling, me).wait_recv()
        for j, chip in enumerate(chips):
            for i in ops:
                copy(4 + j, i, (*chip, 1 - c), me).wait_recv()
        for cp in first + passed:
            cp.wait_send()
        for cp in mine:
            cp.wait()

    return pl.pallas_call(
        body, name=name,
        in_specs=[pl.BlockSpec(memory_space=pl.ANY)] * n,
        out_specs=[pl.BlockSpec(memory_space=pl.ANY)] * n,
        out_shape=[jax.ShapeDtypeStruct((N_DEV,) + b.shape, b.dtype) for b in blocks],
        scratch_shapes=[pltpu.SemaphoreType.DMA((N_DEV - 1, n)), pltpu.SemaphoreType.DMA((N_DEV - 1, n)),
                        pltpu.SemaphoreType.DMA((n,))],
        compiler_params=pltpu.CompilerParams(has_side_effects=True),
    )(*blocks)


WHOLE = 0


def _exchange_shapes(srcs):
    shapes = [a.shape[1:] if cols is None else a.shape if cols == WHOLE else (a.shape[0], cols) for a, cols in srcs]
    return [jax.ShapeDtypeStruct((N_DEV,) + s, a.dtype) for s, (a, _) in zip(shapes, srcs)]


def _exchange_sems(n):
    return [pltpu.SemaphoreType.DMA((N_DEV - 1, n)), pltpu.SemaphoreType.DMA((N_DEV - 1, n)),
            pltpu.SemaphoreType.DMA((n,))]


def _exchange_ops(col_widths, x_refs, out_refs, send_sems, recv_sems, local_sems):
    n = len(col_widths)
    pos = _mesh_pos()
    me = 4 * pos[0] + 2 * pos[1] + pos[2]

    def piece(i, d):
        cols = col_widths[i]
        if cols is None:
            return x_refs[i].at[d]
        if cols == WHOLE:
            return x_refs[i]
        return x_refs[i].at[:, pl.ds(pl.multiple_of(d * cols, 128), cols)]

    def local(i):
        return pltpu.make_async_copy(piece(i, me), out_refs[i].at[me], local_sems.at[i])

    def remote(k, i, landing):
        peer, idx = _peer(pos, k)
        return pltpu.make_async_remote_copy(
            src_ref=piece(i, idx), dst_ref=out_refs[i].at[idx if landing else me],
            send_sem=send_sems.at[k - 1, i], recv_sem=recv_sems.at[k - 1, i],
            device_id=peer, device_id_type=pl.DeviceIdType.MESH)

    pairs = [(k, i) for k in range(1, N_DEV) for i in range(n)]

    def start():
        for i in range(n):
            local(i).start()
        for k, i in pairs:
            remote(k, i, False).start()

    def wait():
        for k, i in pairs:
            remote(k, i, True).wait_recv()
        for k, i in pairs:
            remote(k, i, False).wait_send()
        for i in range(n):
            local(i).wait()

    return start, wait


def all_to_all_blocks(name, srcs):
    n = len(srcs)

    def body(*refs):
        start, wait = _exchange_ops([c for _, c in srcs], refs[:n], refs[n:2 * n], *refs[2 * n:])
        start()
        wait()

    return pl.pallas_call(
        body, name=name,
        in_specs=[pl.BlockSpec(memory_space=pl.ANY)] * n,
        out_specs=[pl.BlockSpec(memory_space=pl.ANY)] * n,
        out_shape=_exchange_shapes(srcs),
        scratch_shapes=_exchange_sems(n),
        compiler_params=pltpu.CompilerParams(has_side_effects=True),
    )(*[a for a, _ in srcs])


def _adam_row_tile(R, C):
    best = None
    for t in range(16, R + 1, 16):
        if R % t == 0 and t * C <= ADAM_TILE_ELEMS:
            best = t
    return best if best is not None else R


def reduce_adamw(name, parts, w, m, v):
    _, R, C = parts.shape
    tr = _adam_row_tile(R, C)

    def fn(pid, parts_t, w_t, m_t, v_t):
        g = parts_t[0].astype(F32)
        for i in range(1, N_DEV):
            g = g + parts_t[i].astype(F32)
        m_n = ADAM_B1 * m_t + (1.0 - ADAM_B1) * g
        v_n = ADAM_B2 * v_t + (1.0 - ADAM_B2) * (g * g)
        m_hat = m_n / (1.0 - ADAM_B1 ** ADAM_STEP)
        v_hat = v_n / (1.0 - ADAM_B2 ** ADAM_STEP)
        delta = -ADAM_LR * (m_hat / (jnp.sqrt(v_hat) + ADAM_EPS) + ADAM_WD * w_t)
        return g, delta, m_n, v_n

    row = lambda a: (a, (tr, C), lambda i: (i, 0))
    out = ((R, C), F32, (tr, C), lambda i: (i, 0), None)
    return tile_call(name, fn, (R // tr,),
                     [(parts, (N_DEV, tr, C), lambda i: (0, i, 0)), row(w), row(m), row(v)], [out] * 4)


PARAMS = (
    ("g_mix", (1, 1024), None), ("w_in", (1024, 4128), 1), ("rw_mu", (1, 1824), None), ("rw_w0", (1, 512), None),
    ("rw_w_up", (64, 512), 1), ("rw_a0", (1, 512), None), ("rw_a_up", (64, 512), 1), ("rw_g_up", (160, 512), 1),
    ("rw_k_k", (1, 512), None), ("rw_k_a", (1, 512), None), ("rw_r_k", (8, 64), None), ("rw_ln_g", (1, 512), None),
    ("rw_ln_b", (1, 512), None), ("w_branch_a", (512, 1024), 1), ("w_branch_b", (256, 1024), 1),
    ("w_gate", (1024, 2048), 1), ("b_gate", (1, 2048), None), ("w_out", (1024, 1024), 0), ("g_ffn", (1, 1024), None),
    ("w_up", (1024, 6144), 1), ("conv_w", (3, 6144), 1), ("conv_b", (1, 6144), None), ("w_down", (3072, 1024), 0),
    ("g_ple", (1, 1024), None), ("w_ple_gate", (1024, 1024), 0), ("w_ple", (256, 1024), 1), ("g_final", (1, 1024), None),
)
SHARDED = tuple(q for q in PARAMS if q[2] is not None)
REPLICATED = tuple(q for q in PARAMS if q[2] is None)
BIG_NAMES = ("w_in", "w_up", "w_gate", "w_out", "w_down", "w_ple_gate", "w_branch_a", "w_branch_b", "w_ple")
BIG = tuple(q for q in SHARDED if q[0] in BIG_NAMES)
SMALL_SHARDED = tuple(q for q in SHARDED if q[0] not in BIG_NAMES)
PACK_COLS = 1024
F32_GATHERED = ("conv_w",)


def _local_shape(shape, axis):
    s = list(shape)
    s[axis] //= N_DEV
    return tuple(s)


def _numel(shape):
    return int(np.prod(shape))


def _pad_flat(z, mult):
    n = z.shape[-1]
    total = -(-n // mult) * mult
    return jnp.pad(z, [(0, 0)] * (z.ndim - 1) + [(0, total - n)])


def _full_from_slots(slots, shape, axis):
    loc = _local_shape(shape, axis)
    z = slots.reshape((N_DEV,) + loc)
    if axis == 0:
        return z.reshape(shape)
    return z.transpose(1, 0, 2).reshape(shape)


def _slots_from_full(full, shape, axis):
    loc = _local_shape(shape, axis)
    if axis == 0:
        return full.reshape(N_DEV, _numel(loc))
    return full.reshape(shape[0], N_DEV, loc[1]).transpose(1, 0, 2).reshape(N_DEV, _numel(loc))


W_IN_SLOT = 640
W_IN_LOCAL = 4128 // N_DEV


def _block_shape(shape, axis):
    return _local_shape(shape, axis) if axis is not None else shape


def _pad_w_in(block):
    return jnp.pad(block, ((0, 0), (0, W_IN_SLOT - W_IN_LOCAL)))


def _proj_col(s):
    return s + jnp.where(s >= 1600, 64, 0) + jnp.where(s >= 1664, 64, 0) + jnp.where(s >= 1824, 96, 0)


def _perm_tile(d, c0, width):
    j = lax.broadcasted_iota(jnp.int32, (W_IN_SLOT, width), 0)
    c = c0 + lax.broadcasted_iota(jnp.int32, (W_IN_SLOT, width), 1)
    hit = (_proj_col(d * W_IN_LOCAL + j) == c) & (j < W_IN_LOCAL)
    return jnp.where(hit, 1.0, 0.0).astype(BF16)


PERM_TILE = 768


def w_in_unshuffle(slots):
    _, K, _ = slots.shape
    tn = PERM_TILE
    reach = 3

    def first_slot(j):
        return j + jnp.where(j >= 3, 1, 0) + jnp.where(j >= 5, 1, 0)

    def body(a_ref, o_ref, acc_ref):
        j, kk = pl.program_id(0), pl.program_id(1)
        d = first_slot(j) + kk

        @pl.when(kk == 0)
        def _():
            acc_ref[...] = jnp.zeros_like(acc_ref)

        @pl.when(d < N_DEV)
        def _():
            acc_ref[...] += jnp.dot(a_ref[0], _perm_tile(d, j * tn, tn), preferred_element_type=F32)

        @pl.when(kk == reach - 1)
        def _():
            o_ref[...] = acc_ref[...].astype(o_ref.dtype)

    return pl.pallas_call(
        body, name="w_in_unshuffle", grid=(PROJ_PAD // tn, reach),
        in_specs=[pl.BlockSpec((1, K, W_IN_SLOT), lambda j, kk: (jnp.minimum(first_slot(j) + kk, N_DEV - 1), 0, 0))],
        out_specs=pl.BlockSpec((K, tn), lambda j, kk: (0, j)),
        out_shape=jax.ShapeDtypeStruct((K, PROJ_PAD), BF16),
        scratch_shapes=[pltpu.VMEM((K, tn), F32)],
        compiler_params=_cparams(2),
    )(slots)


def w_in_shuffle_grad(dw):
    K = dw.shape[0]
    tk = PERM_TILE

    def first_tile(d):
        return _proj_col(d * W_IN_LOCAL) // tk

    def body(g_ref, o_ref, acc_ref):
        d, kk = pl.program_id(0), pl.program_id(1)
        perm = _perm_tile(d, (first_tile(d) + kk) * tk, tk)
        part = lax.dot_general(g_ref[...].astype(BF16), perm, NT_DIMS, preferred_element_type=F32)

        @pl.when(kk == 0)
        def _():
            acc_ref[...] = part

        @pl.when(kk == 1)
        def _():
            o_ref[0] = (acc_ref[...] + part).astype(o_ref.dtype)

    return pl.pallas_call(
        body, name="w_in_shuffle_grad", grid=(N_DEV, 2),
        in_specs=[pl.BlockSpec((K, tk), lambda d, kk: (0, first_tile(d) + kk))],
        out_specs=pl.BlockSpec((1, K, W_IN_SLOT), lambda d, kk: (d, 0, 0)),
        out_shape=jax.ShapeDtypeStruct((N_DEV, K, W_IN_SLOT), GRAD_WIRE),
        scratch_shapes=[pltpu.VMEM((K, W_IN_SLOT), F32)],
        compiler_params=_cparams(2),
    )(dw)


def _flat_rows(pieces, dtype, row_mult):
    flat = jnp.concatenate([z.astype(dtype) for z in pieces], axis=-1)
    flat = _pad_flat(flat, row_mult * PACK_COLS)
    return flat.reshape(flat.shape[:-1] + (-1, PACK_COLS))


FIRST = tuple(q for q in BIG if q[0] in ("w_in", "w_gate"))
LATE = tuple(q for q in BIG if q not in FIRST)


def _matrix_from_slots(slots, shape, axis):
    return slots.reshape(shape) if axis == 0 else slots.transpose(1, 0, 2).reshape(shape)


def _late_weight_sources(W):
    return [(blk, WHOLE) for blk in W["_late_blocks"]]


def _late_weights(slots):
    return {n: _matrix_from_slots(s, shape, axis) for (n, shape, axis), s in zip(LATE, slots)}


def gather_weights(local):
    blocks = [(_pad_w_in(local[n]) if n == "w_in" else local[n]).astype(BF16) for n, _, _ in FIRST]
    small = [q for q in SMALL_SHARDED if q[0] not in F32_GATHERED]
    exact = [q for q in SMALL_SHARDED if q[0] in F32_GATHERED]
    blocks.append(_flat_rows([local[n].reshape(-1) for n, _, _ in small], BF16, 16))
    blocks.append(_flat_rows([local[n].reshape(-1) for n, _, _ in exact], F32, 8))
    got = all_gather_blocks("weight_all_gather", blocks)
    full = {"_late_blocks": [local[n].astype(BF16) for n, _, _ in LATE]}
    for (n, shape, axis), slots in zip(FIRST, got):
        if n == "w_in":
            full["w_in_p"] = w_in_unshuffle(slots)
        else:
            full[n] = _matrix_from_slots(slots, shape, axis)
    for group, slots in ((small, got[-2]), (exact, got[-1])):
        slots, off = slots.reshape(N_DEV, -1), 0
        for n, shape, axis in group:
            size = _numel(_local_shape(shape, axis))
            full[n] = _full_from_slots(slots[:, off:off + size], shape, axis)
            off += size
    for n, _, _ in REPLICATED:
        full[n] = local[n]
    return full


LOSS_SLOT = ("_loss", (1, 2), None)
PACKED_SMALL = SMALL_SHARDED + REPLICATED + (LOSS_SLOT,)


def _pack_small(vals):
    pieces = [vals[n].reshape(-1) if n in vals else jnp.zeros((_numel(shape),), F32) for n, shape, _ in PACKED_SMALL]
    return _flat_rows(pieces, F32, 16)


def _unpack_small(packed):
    flat, out, off = packed.reshape(-1), {}, 0
    for n, shape, axis in PACKED_SMALL:
        loc = _block_shape(shape, axis)
        out[n] = flat[off:off + _numel(loc)].reshape(loc)
        off += _numel(loc)
    return out


EARLY = tuple(q for q in BIG if q[0] != "w_in")


def _early_grad_sources(G):
    srcs = []
    for n, shape, axis in EARLY:
        if axis == 0:
            srcs.append((G[n].astype(GRAD_WIRE).reshape((N_DEV,) + _local_shape(shape, axis)), None))
        else:
            srcs.append((G[n].astype(GRAD_WIRE), shape[1] // N_DEV))
    return srcs


def _late_grad_sources(G, loss_local):
    srcs = [(w_in_shuffle_grad(G["w_in_p"]), None)]
    rows = [_slots_from_full(G[n].reshape(shape), shape, axis) for n, shape, axis in SMALL_SHARDED]
    loss_hi = loss_local.astype(GRAD_WIRE).astype(F32)
    rep = jnp.concatenate([G[n].reshape(-1) for n, _, _ in REPLICATED] + [jnp.stack([loss_hi, loss_local - loss_hi])])
    rows.append(jnp.broadcast_to(rep[None, :], (N_DEV, rep.shape[0])))
    srcs.append((_flat_rows(rows, GRAD_WIRE, 16), None))
    return srcs


def _step(x, p, target, local_w, local_m, local_v):
    full = gather_weights(local_w)
    loss_local, dx, G = local_step(x, p, target, full)
    late = all_to_all_blocks("grad_all_to_all", _late_grad_sources(G, loss_local))
    parts = [late[0]] + list(G["_early_parts"]) + [late[1]]
    outs = [{}, {}, {}, {}]
    for (n, shape, axis), part in zip((BIG[0],) + EARLY, parts):
        prep = _pad_w_in if n == "w_in" else (lambda z: z)
        res = reduce_adamw("adamw_" + n, part, prep(local_w[n]), prep(local_m[n]), prep(local_v[n]))
        for o, z in zip(outs, res):
            o[n] = z[:, :W_IN_LOCAL] if n == "w_in" else z
    res = reduce_adamw("adamw_small", parts[-1], _pack_small(local_w), _pack_small(local_m), _pack_small(local_v))
    for o, z in zip(outs, res):
        o.update(_unpack_small(z))
    loss = jnp.sum(outs[0]["_loss"])
    return loss, dx, outs


def kernel(x, p, g_mix, w_in, rw_mu, rw_w0, rw_w_up, rw_a0, rw_a_up, rw_g_up, rw_k_k, rw_k_a, rw_r_k, rw_ln_g, rw_ln_b, w_branch_a, w_branch_b, w_gate, b_gate, w_out, g_ffn, w_up, conv_w, conv_b, w_down, g_ple, w_ple_gate, w_ple, g_final, loss_target, m_g_mix, m_w_in, m_rw_mu, m_rw_w0, m_rw_w_up, m_rw_a0, m_rw_a_up, m_rw_g_up, m_rw_k_k, m_rw_k_a, m_rw_r_k, m_rw_ln_g, m_rw_ln_b, m_w_branch_a, m_w_branch_b, m_w_gate, m_b_gate, m_w_out, m_g_ffn, m_w_up, m_conv_w, m_conv_b, m_w_down, m_g_ple, m_w_ple_gate, m_w_ple, m_g_final, v_g_mix, v_w_in, v_rw_mu, v_rw_w0, v_rw_w_up, v_rw_a0, v_rw_a_up, v_rw_g_up, v_rw_k_k, v_rw_k_a, v_rw_r_k, v_rw_ln_g, v_rw_ln_b, v_w_branch_a, v_w_branch_b, v_w_gate, v_b_gate, v_w_out, v_g_ffn, v_w_up, v_conv_w, v_conv_b, v_w_down, v_g_ple, v_w_ple_gate, v_w_ple, v_g_final):
    args = dict(locals())
    names = [n for n, _, _ in PARAMS]
    orig_shape = {n: args[n].shape for n in names}

    def strip(prefix):
        out = {}
        for n, shape, axis in PARAMS:
            a = args[prefix + n]
            loc = _local_shape(shape, axis) if axis is not None else shape
            out[n] = a.reshape(loc)
        return out

    local_w, local_m, local_v = strip(""), strip("m_"), strip("v_")
    T, D = x.shape[-2], x.shape[-1]
    loss, dx, (g, delta, m_n, v_n) = _step(x.reshape(T, D), p.reshape(T, p.shape[-1]), loss_target.reshape(T, D),
                                           local_w, local_m, local_v)
    outs = [loss, dx.reshape(x.shape)]
    for group in (g, delta, m_n, v_n):
        outs += [group[n].reshape(orig_shape[n]) for n in names]
    return tuple(outs)
```

```python
import functools
import math

import numpy as np
import jax
import jax.numpy as jnp
from jax import lax
from jax.experimental import pallas as pl
from jax.experimental.pallas import tpu as pltpu

F32 = jnp.float32
BF16 = jnp.bfloat16
GRAD_WIRE = jnp.bfloat16

N_DEV = 8
NORM_EPS = 1e-6
RW_LN_EPS = 64e-5
HEAD = 64
RW_WIDTH = 512
ATT_GROUPS = ((128, 1), (512, 4), (2048, 16))
ATT_HEADS = 12
ATT_OUT = 256
ATT_COLS = 2304
OFF_XW, OFF_XA, OFF_XG, RW_PAD, PROJ_PAD = 1536, 1664, 1792, 2048, 4608
PROJ_TAIL = PROJ_PAD - RW_PAD - ATT_COLS
D_FF = 3072

ADAM_LR, ADAM_B1, ADAM_B2, ADAM_EPS, ADAM_WD, ADAM_STEP = 0.001, 0.9, 0.999, 1e-08, 0.01, 10

VMEM_LIMIT_BYTES = 56 * 1024 * 1024
ADAM_TILE_ELEMS = 256 * 1024
NEG_BIG = -1e30

NT_DIMS = (((1,), (1,)), ((), ()))
TN_DIMS = (((0,), (0,)), ((), ()))
NN_DIMS = (((1,), (0,)), ((), ()))


def _cparams(n_axes):
    return pltpu.CompilerParams(dimension_semantics=("arbitrary",) * n_axes,
                                vmem_limit_bytes=VMEM_LIMIT_BYTES)


def _split2(x):
    hi = x.astype(BF16)
    lo = (x - hi.astype(F32)).astype(BF16)
    return hi, lo


def _seg_mat(n):
    r = lax.shift_right_logical(lax.broadcasted_iota(jnp.int32, (n, n), 0), 6)
    c = lax.shift_right_logical(lax.broadcasted_iota(jnp.int32, (n, n), 1), 6)
    return jnp.where(r == c, 1.0, 0.0).astype(BF16)


def _segb(x, seg):
    return _segb_stack([(x, 2)], seg)[0]


def _segb_stack(items, seg):
    rows = items[0][0].shape[0]
    parts = []
    for x, passes in items:
        parts += list(_split2(x)) if passes == 2 else [x.astype(BF16)]
    res = jnp.dot(jnp.concatenate(parts, axis=0), seg, preferred_element_type=F32)
    out, at = [], 0
    for _, passes in items:
        piece = res[at * rows:(at + 1) * rows]
        if passes == 2:
            piece = piece + res[(at + 1) * rows:(at + 2) * rows]
        out.append(piece)
        at += passes
    return out


def _segb1(x, seg):
    return jnp.dot(x.astype(BF16), seg, preferred_element_type=F32)


@jax.custom_vjp
def segsum(x):
    return _segb(x, _seg_mat(x.shape[1]))


def _segsum_fwd(x):
    return segsum(x), None


def _segsum_bwd(_, ct):
    return (segsum(ct),)


segsum.defvjp(_segsum_fwd, _segsum_bwd)


@jax.custom_vjp
def bdot(a, b):
    return jnp.dot(a.astype(BF16), b.astype(BF16), preferred_element_type=F32)


def _bdot_fwd(a, b):
    return bdot(a, b), (a, b)


def _bdot_bwd(res, ct):
    a, b = res
    ctb = ct.astype(BF16)
    da = lax.dot_general(ctb, b.astype(BF16), NT_DIMS, preferred_element_type=F32)
    db = lax.dot_general(a.astype(BF16), ctb, TN_DIMS, preferred_element_type=F32)
    return da.astype(a.dtype), db.astype(b.dtype)


bdot.defvjp(_bdot_fwd, _bdot_bwd)


def _sig(x):
    return 1.0 / (1.0 + jnp.exp(-x))


def _softplus(z):
    return jnp.maximum(z, 0.0) + jnp.log(1.0 + jnp.exp(-jnp.abs(z)))


def _gelu_tanh(x):
    return 0.5 * x * (1.0 + jnp.tanh(0.7978845608028654 * (x + 0.044715 * (x * x * x))))


def _rms(x, g):
    return x * lax.rsqrt(jnp.mean(x * x, axis=-1, keepdims=True) + NORM_EPS) * g


def _shift_down(x, prev8, n):
    rolled = pltpu.roll(x, n, 0)
    top = pltpu.roll(prev8, n, 0)
    rid = lax.broadcasted_iota(jnp.int32, (8, x.shape[1]), 0)
    head = jnp.where(rid < n, top, rolled[:8])
    return jnp.concatenate([head, rolled[8:]], axis=0)


def _shift_up(x, next8, n):
    rows = x.shape[0]
    rolled = pltpu.roll(x, rows - n, 0)
    bottom = pltpu.roll(next8, 8 - n, 0)
    rid = lax.broadcasted_iota(jnp.int32, (8, x.shape[1]), 0)
    tail = jnp.where(rid >= 8 - n, bottom, rolled[rows - 8:])
    return jnp.concatenate([rolled[:rows - 8], tail], axis=0)


def tile_call(name, fn, grid, ins, outs, scratch=()):
    n_in, n_out = len(ins), len(outs)
    acc_axes = [o[4] for o in outs]

    def body(*refs):
        pids = tuple(pl.program_id(a) for a in range(len(grid)))
        vals = fn(pids, *[r[...] for r in refs[:n_in]], *refs[n_in + n_out:])
        if not isinstance(vals, (tuple, list)):
            vals = (vals,)
        for o_ref, val, ax in zip(refs[n_in:n_in + n_out], vals, acc_axes):
            if ax is None:
                o_ref[...] = val.astype(o_ref.dtype)
            else:
                @pl.when(pids[ax] == 0)
                def _(o_ref=o_ref):
                    o_ref[...] = jnp.zeros_like(o_ref)

                o_ref[...] += val.astype(o_ref.dtype)

    res = pl.pallas_call(
        body, name=name, grid=grid,
        in_specs=[pl.BlockSpec(b, im) for _, b, im in ins],
        out_specs=[pl.BlockSpec(o[2], o[3]) for o in outs],
        out_shape=[jax.ShapeDtypeStruct(o[0], o[1]) for o in outs],
        scratch_shapes=[pltpu.VMEM(s, d) for s, d in scratch],
        compiler_params=_cparams(len(grid)),
    )(*[a for a, _, _ in ins])
    return res


def _rows(a, tm):
    return (a, (tm, a.shape[1]), lambda i: (i, 0))


def _par(a):
    return (a, a.shape, lambda i: (0, 0))


def _row_out(T, C, dtype, tm):
    return ((T, C), dtype, (tm, C), lambda i: (i, 0), None)


def _acc_out(R, C):
    return ((R, C), F32, (R, C), lambda i: (0, 0), 0)


def _prev_halo(a, tm, C):
    return (a, (8, C), lambda i: (jnp.maximum(i * (tm // 8) - 1, 0), 0))


def _next_halo(a, tm, C, T):
    return (a, (8, C), lambda i: (jnp.minimum((i + 1) * (tm // 8), T // 8 - 1), 0))


def _pick(n, target):
    for t in (target, 2048, 1536, 1024, 768, 512, 384, 256, 128):
        if t <= target and n % t == 0:
            return t
    return n


def matmul(name, a, b, mode="nn", res=None, out_dtype=F32, tm=1024, tn=2048, tk=2048):
    if mode == "nn":
        (M, K), (K2, N) = a.shape, b.shape
    elif mode == "tn":
        (K, M), (K2, N) = a.shape, b.shape
    else:
        (M, K), (N, K2) = a.shape, b.shape
    assert K == K2, (name, a.shape, b.shape, mode)
    tm, tn, tk = _pick(M, tm), _pick(N, tn), _pick(K, tk)
    nk = K // tk
    dims = {"nn": NN_DIMS, "tn": TN_DIMS, "nt": NT_DIMS}[mode]
    a_spec = {"nn": pl.BlockSpec((tm, tk), lambda i, j, k: (i, k)),
              "tn": pl.BlockSpec((tk, tm), lambda i, j, k: (k, i)),
              "nt": pl.BlockSpec((tm, tk), lambda i, j, k: (i, k))}[mode]
    b_spec = {"nn": pl.BlockSpec((tk, tn), lambda i, j, k: (k, j)),
              "tn": pl.BlockSpec((tk, tn), lambda i, j, k: (k, j)),
              "nt": pl.BlockSpec((tn, tk), lambda i, j, k: (j, k))}[mode]
    has_res = res is not None

    def body(*refs):
        if has_res:
            a_ref, b_ref, r_ref, o_ref, acc_ref = refs
        else:
            a_ref, b_ref, o_ref, acc_ref = refs
        k = pl.program_id(2)

        @pl.when(k == 0)
        def _():
            acc_ref[...] = jnp.zeros_like(acc_ref)

        acc_ref[...] += lax.dot_general(a_ref[...].astype(BF16), b_ref[...].astype(BF16), dims,
                                        preferred_element_type=F32)

        @pl.when(k == nk - 1)
        def _():
            out = acc_ref[...]
            if has_res:
                out = out + r_ref[...].astype(F32)
            o_ref[...] = out.astype(o_ref.dtype)

    in_specs = [a_spec, b_spec]
    args = [a, b]
    if has_res:
        in_specs.append(pl.BlockSpec((tm, tn), lambda i, j, k: (i, j)))
        args.append(res)
    return pl.pallas_call(
        body, name=name, grid=(M // tm, N // tn, nk),
        in_specs=in_specs,
        out_specs=pl.BlockSpec((tm, tn), lambda i, j, k: (i, j)),
        out_shape=jax.ShapeDtypeStruct((M, N), out_dtype),
        scratch_shapes=[pltpu.VMEM((tm, tn), F32)],
        compiler_params=_cparams(3),
    )(*args)


def rw_pre(Pc, Ps, mu, w0, w_up, a0, a_up, g_up, k_k, k_a):
    Pm = Pc + (Ps - Pc) * mu
    r, k, v = Pm[:, 0:512], Pm[:, 512:1024], Pm[:, 1024:1536]
    xw, xa, xg = Pm[:, OFF_XW:OFF_XA], Pm[:, OFF_XA:OFF_XG], Pm[:, OFF_XG:RW_PAD]
    w = -_softplus(-(w0 + bdot(jnp.tanh(xw), w_up))) - 0.5
    decay = jnp.exp(-jnp.exp(w))
    a = _sig(a0 + bdot(xa, a_up))
    g = bdot(_sig(xg), g_up)
    kk = k * k_k
    kk = kk / jnp.maximum(jnp.sqrt(segsum(kk * kk)), 1e-12)
    k2 = k * (1.0 + (a - 1.0) * k_a)
    return r, decay, k2, v, -kk, kk * a, g


def rw_post(y, r, k2, v, g, ln_g, ln_b, r_k):
    mean = segsum(y) * (1.0 / HEAD)
    d = y - mean
    var = segsum(d * d) * (1.0 / HEAD)
    yn = d * lax.rsqrt(var + RW_LN_EPS) * ln_g + ln_b
    bonus = segsum(r * k2 * r_k) * v
    return (yn + bonus) * g


def att_combine(o1, o2, o3, l1, l2, l3):
    m = jnp.maximum(jnp.maximum(l1, l2), l3)
    e1, e2, e3 = jnp.exp(l1 - m), jnp.exp(l2 - m), jnp.exp(l3 - m)
    return (e1 * o1 + e2 * o2 + e3 * o3) / (e1 + e2 + e3)


def merge_fn(gp, bg, za, zb):
    s = _sig(gp + bg)
    half = za.shape[1]
    return s[:, :half] * za + s[:, half:] * zb


def tail_loss(x2, zg, pe, g_final, target):
    x3 = x2 + _sig(zg) * pe
    y = _rms(x3, g_final)
    err = (y - target) * (y - target)
    return 0.5 * jnp.sum(jnp.mean(err, axis=-1, keepdims=True))


SCAN_CHUNK = HEAD
SCAN_LANES = 256
SCAN_UNROLL_FWD, SCAN_UNROLL_BWD = 8, 8


def _to_head_time(z):
    T = z.shape[0]
    return z.reshape(T // HEAD, HEAD, RW_WIDTH // HEAD, HEAD).transpose(0, 3, 2, 1).reshape(T // HEAD, HEAD, RW_WIDTH)


def _from_head_time(zt):
    C = zt.shape[0]
    return zt.reshape(C, HEAD, RW_WIDTH // HEAD, HEAD).transpose(0, 3, 2, 1).reshape(C * HEAD, RW_WIDTH)


def _unrolled_loop(n, step, init, unroll):
    def body(i, carry):
        for j in range(unroll):
            carry = step(i * unroll + j, carry)
        return carry

    return lax.fori_loop(0, n // unroll, body, init)


def _lane_groups():
    return [slice(j * SCAN_LANES, (j + 1) * SCAN_LANES) for j in range(RW_WIDTH // SCAN_LANES)]


def scan_pair_terms(a, w, b, k, tm=512):
    T = a.shape[0]

    def fn(pid, a_t, nxt, w_t, b_t, k_t):
        a_next = _shift_up(a_t, jnp.where(pid[0] < T // tm - 1, nxt, 0.0), 1)
        return w_t * a_next, segsum(b_t * a_next), segsum(k_t * a_next)

    return tile_call("scan_pair_terms", fn, (T // tm,),
                     [_rows(a, tm), _next_halo(a, tm, RW_WIDTH, T), _rows(w, tm), _rows(b, tm), _rows(k, tm)],
                     [_row_out(T, RW_WIDTH, F32, tm)] * 3)


def rwkv_scan_fwd(a, w, b, k, r, vT, wa, ba, ka, exchange=()):
    T = a.shape[0]
    C, LW = SCAN_CHUNK, SCAN_LANES
    nC = T // C
    nx = len(exchange)

    def body(*refs):
        a_ref, w_ref, b_ref, k_ref, r_ref, vT_ref, wa_ref, ba_ref, ka_ref = refs[:9]
        x_refs, refs = refs[9:9 + nx], refs[9 + nx:]
        yT_ref, S_ref, saT_ref = refs[:3]
        land_refs, refs = refs[3:3 + nx], refs[3 + nx:]
        st_ref, vb0_ref, vb1_ref, seg_ref = refs[:4]
        if nx:
            start, wait = _exchange_ops([c for _, c in exchange], x_refs, land_refs, *refs[4:])

        @pl.when(pl.program_id(0) == 0)
        def _():
            st_ref[...] = jnp.zeros_like(st_ref)
            seg_ref[...] = _seg_mat(LW)
            if nx:
                start()

        seg = seg_ref[...]
        lane = jnp.bitwise_and(lax.broadcasted_iota(jnp.int32, (1, LW), 1), HEAD - 1)
        groups = _lane_groups()

        def vsel(t, gsl):
            return jnp.where(lane == t, vT_ref[0, :, gsl], 0.0)

        first = _segb_stack([(vsel(s, gsl), 1) for gsl in groups for s in (0, 1)], seg)
        for g, gsl in enumerate(groups):
            vb0_ref[:, gsl] = first[2 * g]
            vb1_ref[:, gsl] = first[2 * g + 1]
        saT_ref[...] = jnp.zeros_like(saT_ref)

        def pair(i, yacc):
            t = 2 * i
            t1 = t + 1
            tp = jnp.maximum(t - 1, 0)
            row = lambda ref, s, gsl: ref[pl.ds(s, 1), gsl]
            Sps = [st_ref[:, gsl] for gsl in groups]
            chain = _segb_stack([(Sp * row(ref, t, gsl), 2) for gsl, Sp in zip(groups, Sps) for ref in (a_ref, wa_ref)],
                                seg)
            sas, us = chain[0::2], chain[1::2]
            S1s = []
            for gsl, Sp, sa, u in zip(groups, Sps, sas, us):
                vb0, vb1 = vb0_ref[:, gsl], vb1_ref[:, gsl]
                S1 = Sp * row(w_ref, t, gsl) + sa * row(b_ref, t, gsl) + vb0 * row(k_ref, t, gsl)
                sa1 = u + sa * row(ba_ref, t, gsl) + vb0 * row(ka_ref, t, gsl)
                st_ref[:, gsl] = S1 * row(w_ref, t1, gsl) + sa1 * row(b_ref, t1, gsl) + vb1 * row(k_ref, t1, gsl)
                S_ref[0, t, :, gsl] = Sp
                S_ref[0, t1, :, gsl] = S1
                S1s.append(S1)
                saT_ref[0, :, gsl] = jnp.where(lane == t, sa, jnp.where(lane == t1, sa1, saT_ref[0, :, gsl]))
            side = _segb_stack([(x, 1) for gsl, Sp, S1 in zip(groups, Sps, S1s)
                                for x in (Sp * row(r_ref, tp, gsl), S1 * row(r_ref, t, gsl),
                                          vsel(t + 2, gsl), vsel(t + 3, gsl))], seg)
            out = []
            for g, (gsl, ya) in enumerate(zip(groups, yacc)):
                yb0, yb1, vb0_ref[:, gsl], vb1_ref[:, gsl] = side[4 * g:4 * g + 4]
                out.append(jnp.where(lane == t, yb1, jnp.where(lane == t - 1, yb0, ya)))
            return tuple(out)

        yacc = _unrolled_loop(C // 2, pair, tuple(jnp.zeros((HEAD, LW), F32) for _ in groups), SCAN_UNROLL_FWD)
        for gsl, ya in zip(groups, yacc):
            S_last = st_ref[:, gsl]
            S_ref[0, C, :, gsl] = S_last
            yb = _segb1(S_last * r_ref[pl.ds(C - 1, 1), gsl], seg)
            yT_ref[0, :, gsl] = jnp.where(lane == C - 1, yb, ya)

        if nx:
            @pl.when(pl.program_id(0) == nC - 1)
            def _():
                wait()

    row = pl.BlockSpec((C, RW_WIDTH), lambda c: (c, 0))
    ht = pl.BlockSpec((1, HEAD, RW_WIDTH), lambda c: (c, 0, 0))
    hbm = pl.BlockSpec(memory_space=pl.ANY)
    res = pl.pallas_call(
        body, name="rwkv_scan_fwd", grid=(nC,),
        in_specs=[row, row, row, row, row, ht, row, row, row] + [hbm] * nx,
        out_specs=[ht, pl.BlockSpec((1, C + 1, HEAD, RW_WIDTH), lambda c: (c, 0, 0, 0)), ht] + [hbm] * nx,
        out_shape=[jax.ShapeDtypeStruct((nC, HEAD, RW_WIDTH), F32),
                   jax.ShapeDtypeStruct((nC, C + 1, HEAD, RW_WIDTH), F32),
                   jax.ShapeDtypeStruct((nC, HEAD, RW_WIDTH), F32)] + _exchange_shapes(exchange),
        scratch_shapes=[pltpu.VMEM((HEAD, RW_WIDTH), F32)] * 3 + [pltpu.VMEM((LW, LW), BF16)]
        + (_exchange_sems(nx) if nx else []),
        compiler_params=pltpu.CompilerParams(dimension_semantics=("arbitrary",), vmem_limit_bytes=VMEM_LIMIT_BYTES,
                                             has_side_effects=bool(nx)),
    )(a, w, b, k, r, vT, wa, ba, ka, *[z for z, _ in exchange])
    return res[:3], res[3:]


def rwkv_scan_bwd(a, w, b, k, r, v, dy, S_all, saT, exchange=()):
    T = a.shape[0]
    C, LW = SCAN_CHUNK, SCAN_LANES
    nC = T // C
    nx = len(exchange)
    n_heads = RW_WIDTH // HEAD
    dyT = _to_head_time(dy).astype(BF16)
    v_rows, dy_rows = v.reshape(T, n_heads, HEAD), dy.reshape(T, n_heads, HEAD)
    sa_rows = _from_head_time(saT).reshape(T, n_heads, HEAD)

    def body(*refs):
        a_ref, w_ref, b_ref, k_ref, r_ref, vR_ref, saR_ref, dyR_ref, dyT_ref, S_ref = refs[:10]
        x_refs, refs = refs[10:10 + nx], refs[10 + nx:]
        da_ref, dw_ref, db_ref, dk_ref, dr_ref, dvT_ref = refs[:6]
        land_refs, refs = refs[6:6 + nx], refs[6 + nx:]
        ds_ref, dyb_ref, seg_ref = refs[:3]
        if nx:
            start, wait = _exchange_ops([c for _, c in exchange], x_refs, land_refs, *refs[3:])

        @pl.when(pl.program_id(0) == 0)
        def _():
            ds_ref[...] = jnp.zeros_like(ds_ref)
            seg_ref[...] = _seg_mat(LW)
            if nx:
                start()

        seg = seg_ref[...]
        lane = jnp.bitwise_and(lax.broadcasted_iota(jnp.int32, (1, LW), 1), HEAD - 1)
        groups = _lane_groups()
        head_row = lax.broadcasted_iota(jnp.int32, (n_heads, LW), 0)
        lane_head = lax.shift_right_logical(lax.broadcasted_iota(jnp.int32, (n_heads, LW), 1), 6)

        def colsum(z):
            return jnp.sum(z, axis=0, keepdims=True)

        def dysel(t, gsl):
            return jnp.where(lane == t, dyT_ref[0, :, gsl], 0.0)

        for gsl, dyb in zip(groups, _segb_stack([(dysel(C - 1, gsl), 1) for gsl in groups], seg)):
            dyb_ref[:, gsl] = dyb

        def step(i, dvacc):
            t = C - 1 - i
            dybs = [dyb_ref[:, gsl] for gsl in groups]
            dSs = [ds_ref[:, gsl] + dyb * r_ref[pl.ds(t, 1), gsl] for gsl, dyb in zip(groups, dybs)]
            dsabs = _segb_stack([(dS * b_ref[pl.ds(t, 1), gsl], 2) for gsl, dS in zip(groups, dSs)], seg)
            for gsl, dS, dsab in zip(groups, dSs, dsabs):
                ds_ref[:, gsl] = dS * w_ref[pl.ds(t, 1), gsl] + dsab * a_ref[pl.ds(t, 1), gsl]
            out = []
            dy_rows = dyR_ref[t].astype(BF16)
            v_sa_rows = jnp.concatenate([vR_ref[t], saR_ref[t]], axis=0).astype(BF16)
            side = _segb_stack([(x, 1) for gsl, dS in zip(groups, dSs)
                                for x in (dS * k_ref[pl.ds(t, 1), gsl], dysel(t - 1, gsl))], seg)
            for g, (gsl, dva, dS, dsab) in enumerate(zip(groups, dvacc, dSs, dsabs)):
                dvb, dyb_ref[:, gsl] = side[2 * g:2 * g + 2]
                Sp = S_ref[0, t, :, gsl]
                own = head_row == lane_head + g * (LW // HEAD)

                def rows_in(rows, mat):
                    full = jnp.dot(rows, mat.astype(BF16), preferred_element_type=F32)
                    return [jnp.sum(jnp.where(own, full[s:s + n_heads], 0.0), axis=0, keepdims=True)
                            for s in range(0, rows.shape[0], n_heads)]

                (dr,) = rows_in(dy_rows, S_ref[0, t + 1, :, gsl])
                dk, db = rows_in(v_sa_rows, dS)
                dr_ref[pl.ds(t, 1), gsl] = dr
                dk_ref[pl.ds(t, 1), gsl] = dk
                db_ref[pl.ds(t, 1), gsl] = db
                dw_ref[pl.ds(t, 1), gsl] = colsum(dS * Sp)
                da_ref[pl.ds(t, 1), gsl] = colsum(Sp * dsab)
                out.append(jnp.where(lane == t, dvb, dva))
            return tuple(out)

        dvacc = _unrolled_loop(C, step, tuple(jnp.zeros((HEAD, LW), F32) for _ in groups), SCAN_UNROLL_BWD)
        for gsl, dva in zip(groups, dvacc):
            dvT_ref[0, :, gsl] = dva

        if nx:
            @pl.when(pl.program_id(0) == nC - 1)
            def _():
                wait()

    row = pl.BlockSpec((C, RW_WIDTH), lambda c: (nC - 1 - c, 0))
    ht = pl.BlockSpec((1, HEAD, RW_WIDTH), lambda c: (nC - 1 - c, 0, 0))
    hbm = pl.BlockSpec(memory_space=pl.ANY)
    per_head = pl.BlockSpec((C, n_heads, HEAD), lambda c: (nC - 1 - c, 0, 0))
    rows_shape = jax.ShapeDtypeStruct((T, RW_WIDTH), F32)
    res = pl.pallas_call(
        body, name="rwkv_scan_bwd", grid=(nC,),
        in_specs=[row, row, row, row, row, per_head, per_head, per_head, ht,
                  pl.BlockSpec((1, C + 1, HEAD, RW_WIDTH), lambda c: (nC - 1 - c, 0, 0, 0))] + [hbm] * nx,
        out_specs=[row, row, row, row, row, ht] + [hbm] * nx,
        out_shape=[rows_shape] * 5 + [jax.ShapeDtypeStruct((nC, HEAD, RW_WIDTH), F32)] + _exchange_shapes(exchange),
        scratch_shapes=[pltpu.VMEM((HEAD, RW_WIDTH), F32), pltpu.VMEM((HEAD, RW_WIDTH), F32),
                        pltpu.VMEM((LW, LW), BF16)] + (_exchange_sems(nx) if nx else []),
        compiler_params=pltpu.CompilerParams(dimension_semantics=("arbitrary",), vmem_limit_bytes=VMEM_LIMIT_BYTES,
                                             has_side_effects=bool(nx)),
    )(a, w, b, k, r, v_rows, sa_rows, dy_rows, dyT, S_all, *[z for z, _ in exchange])
    return res[:6], res[6:]


def _alibi_slope(h):
    return float(np.float32(2.0 ** (-8.0 * (h + 1) / ATT_HEADS)))


ATT_GROUP_HEADS = 4


def _stack_heads(x, lane_head, fill=0.0):
    return jnp.concatenate([jnp.where(lane_head == hh, x, fill) for hh in range(ATT_GROUP_HEADS)], axis=0)


def _unstack_heads(x, lane_head, L):
    out = jnp.zeros((L, x.shape[1]), F32)
    for hh in range(ATT_GROUP_HEADS):
        out = jnp.where(lane_head == hh, x[hh * L:(hh + 1) * L], out)
    return out


def _att_logits(qs, kcat, gi, d, L, n):
    qi = lax.broadcasted_iota(jnp.int32, (L, 2 * L), 0)
    kj = lax.broadcasted_iota(jnp.int32, (L, 2 * L), 1)
    steps = qi + L - kj
    valid = (steps >= 0) & (steps <= L) & ((kj >= L) | (n > 0))
    dist = (d * steps).astype(F32)
    bias = jnp.concatenate([jnp.where(valid, -_alibi_slope(gi * ATT_GROUP_HEADS + hh) * dist, NEG_BIG)
                            for hh in range(ATT_GROUP_HEADS)], axis=0)
    s = lax.dot_general(qs.astype(BF16), kcat, NT_DIMS, preferred_element_type=F32) * (HEAD ** -0.5)
    return jnp.where(bias > 0.5 * NEG_BIG, s + bias, NEG_BIG)


def att_fwd(pa, gi, T):
    window, d = ATT_GROUPS[gi]
    L = window // d
    Tj = T // d
    nb = Tj // L
    pv = pa.reshape(Tj, d * ATT_COLS)
    nblk = ATT_COLS // ATT_OUT

    def fn(pids, q, kp, kc, vp, vc):
        lane_head = lax.shift_right_logical(lax.broadcasted_iota(jnp.int32, (1, ATT_OUT), 1), 6)
        kcat = jnp.concatenate([kp, kc], axis=0).astype(BF16)
        vcat = jnp.concatenate([vp, vc], axis=0).astype(BF16)
        s = _att_logits(_stack_heads(q, lane_head), kcat, gi, d, L, pids[1])
        m = jnp.max(s, axis=-1, keepdims=True)
        p = jnp.exp(s - m)
        l = jnp.sum(p, axis=-1, keepdims=True)
        o = jnp.dot(p.astype(BF16), vcat, preferred_element_type=F32) / l
        lse = jnp.broadcast_to(m + jnp.log(l), o.shape)
        return _unstack_heads(o, lane_head, L), _unstack_heads(lse, lane_head, L)

    blk = (L, ATT_OUT)
    ins = [(pv, blk, lambda r, n: (n, r * nblk + gi)),
           (pv, blk, lambda r, n: (jnp.maximum(n - 1, 0), r * nblk + 3 + gi)),
           (pv, blk, lambda r, n: (n, r * nblk + 3 + gi)),
           (pv, blk, lambda r, n: (jnp.maximum(n - 1, 0), r * nblk + 6 + gi)),
           (pv, blk, lambda r, n: (n, r * nblk + 6 + gi))]
    out = ((Tj, d * ATT_OUT), F32, blk, lambda r, n: (n, r), None)
    o, lseb = tile_call(f"att_fwd_g{gi}", fn, (d, nb), ins, [out, out])
    return o.reshape(T, ATT_OUT), lseb.reshape(T, ATT_OUT)


def att_bwd(pa, o, lseb, do, dlseb, gi, T):
    window, d = ATT_GROUPS[gi]
    L = window // d
    Tj = T // d
    nb = Tj // L
    pv = pa.reshape(Tj, d * ATT_COLS)
    nblk = ATT_COLS // ATT_OUT
    view = lambda z: z.reshape(Tj, d * ATT_OUT)

    def body(q_ref, kp_ref, kc_ref, vp_ref, vc_ref, o_ref, l_ref, do_ref, dl_ref, dq_ref, dk_ref, dv_ref):
        n = pl.program_id(1)

        @pl.when(n == 0)
        def _():
            dk_ref[...] = jnp.zeros_like(dk_ref)
            dv_ref[...] = jnp.zeros_like(dv_ref)

        lane_head = lax.shift_right_logical(lax.broadcasted_iota(jnp.int32, (1, ATT_OUT), 1), 6)
        kcat = jnp.concatenate([kp_ref[...], kc_ref[...]], axis=0).astype(BF16)
        vcat = jnp.concatenate([vp_ref[...], vc_ref[...]], axis=0).astype(BF16)
        qs = _stack_heads(q_ref[...], lane_head)
        dos = _stack_heads(do_ref[...], lane_head)
        lse = jnp.max(_stack_heads(l_ref[...], lane_head, NEG_BIG), axis=-1, keepdims=True)
        dlse = jnp.sum(_stack_heads(dl_ref[...], lane_head), axis=-1, keepdims=True)
        delta = jnp.sum(dos * jnp.concatenate([o_ref[...]] * ATT_GROUP_HEADS, axis=0), axis=-1, keepdims=True)
        p = jnp.exp(_att_logits(qs, kcat, gi, d, L, n) - lse)
        dp = lax.dot_general(dos.astype(BF16), vcat, NT_DIMS, preferred_element_type=F32)
        ds = (p * (dp - delta + dlse)).astype(BF16)
        dq = _unstack_heads(jnp.dot(ds, kcat, preferred_element_type=F32), lane_head, L)
        dkc = lax.dot_general(ds, qs.astype(BF16), TN_DIMS, preferred_element_type=F32)
        dvc = lax.dot_general(p.astype(BF16), dos.astype(BF16), TN_DIMS, preferred_element_type=F32)
        scale = HEAD ** -0.5
        dq_ref[...] = dq * scale
        cur = pl.ds(pl.multiple_of(n * L, L), L)
        dk_ref[cur, :] += dkc[L:] * scale
        dv_ref[cur, :] += dvc[L:]

        @pl.when(n > 0)
        def _():
            prev = pl.ds(pl.multiple_of((n - 1) * L, L), L)
            dk_ref[prev, :] += dkc[:L] * scale
            dv_ref[prev, :] += dvc[:L]

    blk = pl.BlockSpec((L, ATT_OUT), lambda r, n: (n, r))
    res = pl.BlockSpec((Tj, ATT_OUT), lambda r, n: (0, r))
    qspec = lambda off, prev: pl.BlockSpec(
        (L, ATT_OUT), (lambda r, n: (jnp.maximum(n - 1, 0), r * nblk + off + gi)) if prev
        else (lambda r, n: (n, r * nblk + off + gi)))
    shape = jax.ShapeDtypeStruct((Tj, d * ATT_OUT), F32)
    dq, dk, dv = pl.pallas_call(
        body, name=f"att_bwd_g{gi}", grid=(d, nb),
        in_specs=[qspec(0, False), qspec(3, True), qspec(3, False), qspec(6, True), qspec(6, False),
                  blk, blk, blk, blk],
        out_specs=[blk, res, res],
        out_shape=[shape, shape, shape],
        compiler_params=_cparams(2),
    )(pv, pv, pv, pv, pv, view(o), view(lseb), view(do), view(dlseb))
    return dq.reshape(T, ATT_OUT), dk.reshape(T, ATT_OUT), dv.reshape(T, ATT_OUT)


FFN_TM, FFN_TC = 512, 512


def _conv3(u, prev8, cw, cb):
    return cb + cw[0:1] * u + cw[1:2] * _shift_down(u, prev8, 1) + cw[2:3] * _shift_down(u, prev8, 2)


def conv_glu_fwd(u, conv_w, conv_b):
    T = u.shape[0]
    tm, tc = FFN_TM, FFN_TC
    nj, ni = D_FF // tc, T // tm

    def fn(pids, ug, ugh, uv, uvh, cwg, cbg, cwv, cbv):
        first = pids[1] > 0
        cg = _conv3(ug, jnp.where(first, ugh, 0.0), cwg, cbg)
        cv = _conv3(uv, jnp.where(first, uvh, 0.0), cwv, cbv)
        return _gelu_tanh(cg) * cv

    halo = lambda off: (lambda j, i: (jnp.maximum(i * (tm // 8) - 1, 0), j + off))
    ins = [(u, (tm, tc), lambda j, i: (i, j)), (u, (8, tc), halo(0)),
           (u, (tm, tc), lambda j, i: (i, j + nj)), (u, (8, tc), halo(nj)),
           (conv_w, (3, tc), lambda j, i: (0, j)), (conv_b, (1, tc), lambda j, i: (0, j)),
           (conv_w, (3, tc), lambda j, i: (0, j + nj)), (conv_b, (1, tc), lambda j, i: (0, j + nj))]
    out = ((T, D_FF), BF16, (tm, tc), lambda j, i: (i, j), None)
    return tile_call("conv_glu_fwd", fn, (nj, ni), ins, [out])[0]


def conv_glu_bwd(u, conv_w, conv_b, df):
    T = u.shape[0]
    tm, tc = FFN_TM, FFN_TC
    nj, ni = D_FF // tc, T // tm

    def fn(pids, ug, ugh, uv, uvh, cwg, cbg, cwv, cbv, df_t, nxt_g, nxt_v):
        i = ni - 1 - pids[1]
        ugh = jnp.where(i > 0, ugh, 0.0)
        uvh = jnp.where(i > 0, uvh, 0.0)
        cg = _conv3(ug, ugh, cwg, cbg)
        cv = _conv3(uv, uvh, cwv, cbv)
        _, vjp = jax.vjp(lambda g_, v_: _gelu_tanh(g_) * v_, cg, cv)
        dcg, dcv = vjp(df_t.astype(F32))
        cs = lambda z: jnp.sum(z, axis=0, keepdims=True)

        @pl.when(pids[1] == 0)
        def _():
            nxt_g[...] = jnp.zeros_like(nxt_g)
            nxt_v[...] = jnp.zeros_like(nxt_v)

        outs = []
        for dc, cw, nxt_ref in ((dcg, cwg, nxt_g), (dcv, cwv, nxt_v)):
            nxt = nxt_ref[...]
            outs.append(cw[0:1] * dc + cw[1:2] * _shift_up(dc, nxt, 1) + cw[2:3] * _shift_up(dc, nxt, 2))
            nxt_ref[...] = dc[:8]
        for dc, uu, hh in ((dcg, ug, ugh), (dcv, uv, uvh)):
            outs += [cs(dc * uu), cs(dc * _shift_down(uu, hh, 1)), cs(dc * _shift_down(uu, hh, 2)), cs(dc)]
        return outs

    rows = lambda off: (lambda j, r: (ni - 1 - r, j + off))
    halo = lambda off: (lambda j, r: (jnp.maximum((ni - 1 - r) * (tm // 8) - 1, 0), j + off))
    ins = [(u, (tm, tc), rows(0)), (u, (8, tc), halo(0)),
           (u, (tm, tc), rows(nj)), (u, (8, tc), halo(nj)),
           (conv_w, (3, tc), lambda j, r: (0, j)), (conv_b, (1, tc), lambda j, r: (0, j)),
           (conv_w, (3, tc), lambda j, r: (0, j + nj)), (conv_b, (1, tc), lambda j, r: (0, j + nj)),
           (df, (tm, tc), rows(0))]
    big = ((T, D_FF), BF16, (tm, tc), rows(0), None)
    acc = ((1, D_FF), F32, (1, tc), lambda j, r: (0, j), 1)
    res = tile_call("conv_glu_bwd", fn, (nj, ni), ins, [big, big] + [acc] * 8,
                    scratch=[((8, tc), F32), ((8, tc), F32)])
    dconv_w = jnp.concatenate([jnp.concatenate([res[2 + j], res[6 + j]], axis=1) for j in range(3)], axis=0)
    dconv_b = jnp.concatenate([res[5], res[9]], axis=1)
    return res[0], res[1], dconv_w, dconv_b


def _pad_cols(w, total):
    return jnp.pad(w, ((0, 0), (0, total - w.shape[1])))


def _pad_rows(w, total):
    return jnp.pad(w, ((0, total - w.shape[0]), (0, 0)))


def _proj_pad(w):
    z = lambda n: jnp.zeros((w.shape[0], n), w.dtype)
    return jnp.concatenate([w[:, :1600], z(64), w[:, 1600:1664], z(64), w[:, 1664:1824], z(96), w[:, 1824:],
                            z(PROJ_TAIL)], axis=1)


def _proj_unpad(g):
    return jnp.concatenate([g[:, :1600], g[:, OFF_XA:OFF_XA + 64], g[:, OFF_XG:OFF_XG + 160],
                            g[:, RW_PAD:RW_PAD + ATT_COLS]], axis=1)


def _rw_unpad(g):
    return jnp.concatenate([g[:, :1600], g[:, OFF_XA:OFF_XA + 64], g[:, OFF_XG:OFF_XG + 160]], axis=1)


def rms_fwd(name, x, g, tm=512):
    T, D = x.shape
    return tile_call(name, lambda pid, x_t, g_t: _rms(x_t, g_t), (T // tm,),
                     [_rows(x, tm), _par(g)], [_row_out(T, D, BF16, tm)])[0]


def rms_bwd(name, x, g, dh, dres, with_bf16=True, tm=512):
    T, D = x.shape
    out_dtypes = (F32, BF16) if with_bf16 else (F32,)

    def fn(pid, x_t, g_t, dh_t, dres_t):
        _, vjp = jax.vjp(_rms, x_t, g_t)
        dx, dg = vjp(dh_t.astype(F32))
        return (dres_t + dx,) * len(out_dtypes) + (dg,)

    return tile_call(name, fn, (T // tm,), [_rows(x, tm), _par(g), _rows(dh, tm), _rows(dres, tm)],
                     [_row_out(T, D, dt, tm) for dt in out_dtypes] + [_acc_out(1, D)])


def local_step(x, p, target, W):
    T, D = x.shape
    G = {}

    w_in_p = W["w_in_p"]
    mu_p = _proj_pad(_pad_cols(W["rw_mu"], 4128))[:, :RW_PAD]
    w_up_p = _pad_rows(W["rw_w_up"], 128)
    a_up_p = _pad_rows(W["rw_a_up"], 128)
    g_up_p = _pad_rows(W["rw_g_up"], 256)
    r_k = W["rw_r_k"].reshape(1, RW_WIDTH)
    rw_params = [mu_p, W["rw_w0"], w_up_p, W["rw_a0"], a_up_p, g_up_p, W["rw_k_k"], W["rw_k_a"]]

    h = rms_fwd("rms_mix", x, W["g_mix"])
    proj = matmul("proj_in_rw", h, w_in_p[:, :RW_PAD])
    pa = matmul("proj_in_att", h, w_in_p[:, RW_PAD:RW_PAD + ATT_COLS])
    gp = matmul("proj_gate", h, W["w_gate"])

    tm = 512
    rw_in = (proj, (tm, RW_PAD), lambda i: (i, 0))
    rw_halo = _prev_halo(proj, tm, RW_PAD)

    def rw_pre_tile(pid, Pc, halo, *params):
        prev8 = jnp.where(pid[0] > 0, halo, 0.0)
        params = [q.astype(F32) for q in params]
        return rw_pre(Pc, _shift_down(Pc, prev8, 1), *params)

    r, decay, k2, v, avec, bvec, g = tile_call(
        "rw_pre", rw_pre_tile, (T // tm,), [rw_in, rw_halo] + [_par(q) for q in rw_params],
        [_row_out(T, RW_WIDTH, F32, tm)] * 7)

    wa, ba, ka = scan_pair_terms(avec, decay, bvec, k2)
    vT = _to_head_time(v).astype(BF16)
    (yT, S_all, saT), late_slots = rwkv_scan_fwd(avec, decay, bvec, k2, r, vT, wa, ba, ka,
                                            exchange=_late_weight_sources(W))
    y = _from_head_time(yT)
    W = dict(W, **_late_weights(late_slots))

    post_params = [W["rw_ln_g"], W["rw_ln_b"], r_k]
    ya = tile_call("rw_post", lambda pid, *t: rw_post(*t), (T // tm,),
                   [_rows(z, tm) for z in (y, r, k2, v, g)] + [_par(q) for q in post_params],
                   [_row_out(T, RW_WIDTH, BF16, tm)])[0]

    att = [att_fwd(pa, gi, T) for gi in range(3)]
    o_l = [att[0][0], att[1][0], att[2][0], att[0][1], att[1][1], att[2][1]]
    yb = tile_call("att_combine", lambda pid, *t: att_combine(*t), (T // tm,),
                   [_rows(z, tm) for z in o_l], [_row_out(T, ATT_OUT, BF16, tm)])[0]

    za = matmul("branch_a", ya, W["w_branch_a"])
    zb = matmul("branch_b", yb, W["w_branch_b"])
    merged = tile_call("merge", lambda pid, *t: merge_fn(*t), (T // tm,),
                       [_rows(gp, tm), _par(W["b_gate"]), _rows(za, tm), _rows(zb, tm)],
                       [_row_out(T, D, BF16, tm)])[0]
    x1 = matmul("mix_out", merged, W["w_out"], res=x)

    h2 = rms_fwd("rms_ffn", x1, W["g_ffn"])
    u = matmul("ffn_up", h2, W["w_up"])
    f = conv_glu_fwd(u, W["conv_w"], W["conv_b"])
    x2 = matmul("ffn_down", f, W["w_down"], res=x1)

    h3 = rms_fwd("rms_ple", x2, W["g_ple"])
    zg = matmul("ple_gate", h3, W["w_ple_gate"])
    pe = matmul("ple_embed", p, W["w_ple"])

    def tail_tile(pid, x2_t, zg_t, pe_t, gf, tgt):
        loss, vjp = jax.vjp(lambda a_, b_, c_, d_: tail_loss(a_, b_, c_, d_, tgt), x2_t, zg_t, pe_t, gf)
        dx2, dzg, dpe, dgf = vjp(jnp.ones((), F32))
        return dx2, dzg, dpe, dgf, jnp.full((1, 128), loss, F32)

    tmt = 256
    dx3, dzg, dpe, dgf, loss_acc = tile_call(
        "tail_loss", tail_tile, (T // tmt,),
        [_rows(x2, tmt), _rows(zg, tmt), _rows(pe, tmt), _par(W["g_final"]), _rows(target, tmt)],
        [_row_out(T, D, F32, tmt), _row_out(T, D, BF16, tmt), _row_out(T, D, BF16, tmt),
         _acc_out(1, D), _acc_out(1, 128)])
    loss = loss_acc[0, 0]
    G["g_final"] = dgf

    wgrad = functools.partial(matmul, mode="tn", out_dtype=GRAD_WIRE)
    G["w_ple"] = wgrad("d_w_ple", p, dpe)
    G["w_ple_gate"] = wgrad("d_w_ple_gate", h3, dzg)
    dh3 = matmul("d_h3", dzg, W["w_ple_gate"], "nt")
    dx2, dx2b, G["g_ple"] = rms_bwd("rms_ple_bwd", x2, W["g_ple"], dh3, dx3)

    G["w_down"] = wgrad("d_w_down", f, dx2b)
    df = matmul("d_f", dx2b, W["w_down"], "nt", out_dtype=BF16)
    du_g, du_v, G["conv_w"], G["conv_b"] = conv_glu_bwd(u, W["conv_w"], W["conv_b"], df)
    du = jnp.concatenate([du_g, du_v], axis=1)
    G["w_up"] = wgrad("d_w_up", h2, du)
    dh2 = matmul("d_h2", du, W["w_up"], "nt")
    dx1, dx1b, G["g_ffn"] = rms_bwd("rms_ffn_bwd", x1, W["g_ffn"], dh2, dx2)

    G["w_out"] = wgrad("d_w_out", merged, dx1b)
    dmerged = matmul("d_merged", dx1b, W["w_out"], "nt", out_dtype=BF16)

    def merge_bwd_tile(pid, gp_t, bg, za_t, zb_t, dm_t):
        _, vjp = jax.vjp(merge_fn, gp_t, bg, za_t, zb_t)
        return vjp(dm_t.astype(F32))

    dgp, G["b_gate"], dza, dzb = tile_call(
        "merge_bwd", merge_bwd_tile, (T // tm,),
        [_rows(gp, tm), _par(W["b_gate"]), _rows(za, tm), _rows(zb, tm), _rows(dmerged, tm)],
        [_row_out(T, 2 * D, BF16, tm), _acc_out(1, 2 * D), _row_out(T, D, BF16, tm), _row_out(T, D, BF16, tm)])
    G["w_branch_a"] = wgrad("d_w_branch_a", ya, dza)
    dya = matmul("d_ya", dza, W["w_branch_a"], "nt")
    G["w_branch_b"] = wgrad("d_w_branch_b", yb, dzb)
    dyb = matmul("d_yb", dzb, W["w_branch_b"], "nt")
    G["w_gate"] = wgrad("d_w_gate", h, dgp)
    dh_gate = matmul("d_h_gate", dgp, W["w_gate"], "nt")

    def comb_bwd_tile(pid, *t):
        _, vjp = jax.vjp(att_combine, *t[:6])
        return vjp(t[6])

    d_ol = tile_call("att_combine_bwd", comb_bwd_tile, (T // tm,),
                     [_rows(z, tm) for z in o_l] + [_rows(dyb, tm)],
                     [_row_out(T, ATT_OUT, F32, tm)] * 6)
    dqkv = [att_bwd(pa, att[gi][0], att[gi][1], d_ol[gi], d_ol[3 + gi], gi, T) for gi in range(3)]
    d_att = [dqkv[gi][j] for j in range(3) for gi in range(3)]

    def post_bwd_tile(pid, *t):
        _, vjp = jax.vjp(rw_post, *t[:8])
        return vjp(t[8])

    dy, dr_p, dk2_p, dv_p, dg, G["rw_ln_g"], G["rw_ln_b"], d_rk = tile_call(
        "rw_post_bwd", post_bwd_tile, (T // tm,),
        [_rows(z, tm) for z in (y, r, k2, v, g)] + [_par(q) for q in post_params] + [_rows(dya, tm)],
        [_row_out(T, RW_WIDTH, F32, tm)] * 5 + [_acc_out(1, RW_WIDTH)] * 3)
    G["rw_r_k"] = d_rk.reshape(W["rw_r_k"].shape)

    (da, dw, db, dk_s, dr_s, dvT), G["_early_parts"] = rwkv_scan_bwd(
        avec, decay, bvec, k2, r, v, dy, S_all, saT, exchange=_early_grad_sources(G))
    dv_s = _from_head_time(dvT)

    tmb = 128
    rw_in_b = (proj, (tmb, RW_PAD), lambda i: (i, 0))

    def pre_bwd_tile(pid, Pc, halo, *t):
        prev8 = jnp.where(pid[0] > 0, halo, 0.0)
        params = [q.astype(F32) for q in t[:8]]
        dr1, dr2, dw_, dk1, dk2_, dv1, dv2, da_, db_, dg_ = t[8:]
        _, vjp = jax.vjp(rw_pre, Pc, _shift_down(Pc, prev8, 1), *params)
        return vjp((dr1 + dr2, dw_, dk1 + dk2_, dv1 + dv2, da_, db_, dg_))

    cts = (dr_s, dr_p, dw, dk_s, dk2_p, dv_s, dv_p, da, db, dg)
    res = tile_call(
        "rw_pre_bwd", pre_bwd_tile, (T // tmb,),
        [rw_in_b, _prev_halo(proj, tmb, RW_PAD)] + [_par(q) for q in rw_params] + [_rows(z, tmb) for z in cts],
        [_row_out(T, RW_PAD, F32, tmb)] * 2 + [_acc_out(*q.shape) for q in rw_params])
    dPc, dPs = res[0], res[1]
    d_mu, G["rw_w0"], d_wup, G["rw_a0"], d_aup, d_gup, G["rw_k_k"], G["rw_k_a"] = res[2:]
    G["rw_mu"] = _rw_unpad(d_mu)
    G["rw_w_up"], G["rw_a_up"], G["rw_g_up"] = d_wup[:64], d_aup[:64], d_gup[:160]

    def dproj_tile(pid, dPc_t, dPs_t, nxt, *att_t):
        nxt = jnp.where(pid[0] < T // tm - 1, nxt, 0.0)
        tail = jnp.zeros((dPc_t.shape[0], PROJ_TAIL), F32)
        return jnp.concatenate([dPc_t + _shift_up(dPs_t, nxt, 1)] + list(att_t) + [tail], axis=1)

    dproj = tile_call("d_proj", dproj_tile, (T // tm,),
                      [_rows(dPc, tm), _rows(dPs, tm), _next_halo(dPs, tm, RW_PAD, T)] + [_rows(z, tm) for z in d_att],
                      [_row_out(T, PROJ_PAD, BF16, tm)])[0]
    G["w_in_p"] = wgrad("d_w_in", h, dproj)
    dh = matmul("d_h", dproj, w_in_p, "nt", res=dh_gate)
    dx, G["g_mix"] = rms_bwd("rms_mix_bwd", x, W["g_mix"], dh, dx1, with_bf16=False)
    return loss, dx, G


def _mesh_pos():
    return lax.axis_index("x"), lax.axis_index("y"), lax.axis_index("c")


def _peer(pos, k):
    x, y, c = pos
    px = 1 - x if k & 4 else x
    py = 1 - y if k & 2 else y
    pc = 1 - c if k & 1 else c
    return (px, py, pc), 4 * px + 2 * py + pc


def all_gather_blocks(name, blocks):
    n = len(blocks)

    def body(*refs):
        x_refs, out_refs = refs[:n], refs[n:2 * n]
        send_sems, recv_sems, local_sems = refs[2 * n:]
        x, y, c = _mesh_pos()
        me, sibling = (x, y, c), (x, y, 1 - c)
        chips = [(1 - x, y), (x, 1 - y), (1 - x, 1 - y)]
        ops = range(n)

        def slot(i, px, py, pc):
            return out_refs[i].at[4 * px + 2 * py + pc]

        def copy(k, i, block, to, own=False):
            return pltpu.make_async_remote_copy(
                src_ref=x_refs[i] if own else slot(i, *block), dst_ref=slot(i, *block),
                send_sem=send_sems.at[k, i], recv_sem=recv_sems.at[k, i],
                device_id=to, device_id_type=pl.DeviceIdType.MESH)

        mine = [pltpu.make_async_copy(x_refs[i], slot(i, *me), local_sems.at[i]) for i in ops]
        first = [copy(0, i, me, sibling, own=True) for i in ops]
        first += [copy(1 + j, i, me, (*chip, c), own=True) for j, chip in enumerate(chips) for i in ops]
        for cp in mine + first:
            cp.start()
        passed = []
        for j, chip in enumerate(chips):
            for i in ops:
                copy(1 + j, i, (*chip, c), me).wait_recv()
                passed.append(copy(4 + j, i, (*chip, c), sibling))
                passed[-1].start()
        for i in ops:
            copy(0, i, sibling, me).wait_recv()
        for j, chip in enumerate(chips):
            for i in ops:
                copy(4 + j, i, (*chip, 1 - c), me).wait_recv()
        for cp in first + passed:
            cp.wait_send()
        for cp in mine:
            cp.wait()

    return pl.pallas_call(
        body, name=name,
        in_specs=[pl.BlockSpec(memory_space=pl.ANY)] * n,
        out_specs=[pl.BlockSpec(memory_space=pl.ANY)] * n,
        out_shape=[jax.ShapeDtypeStruct((N_DEV,) + b.shape, b.dtype) for b in blocks],
        scratch_shapes=[pltpu.SemaphoreType.DMA((N_DEV - 1, n)), pltpu.SemaphoreType.DMA((N_DEV - 1, n)),
                        pltpu.SemaphoreType.DMA((n,))],
        compiler_params=pltpu.CompilerParams(has_side_effects=True),
    )(*blocks)


WHOLE = 0


def _exchange_shapes(srcs):
    shapes = [a.shape[1:] if cols is None else a.shape if cols == WHOLE else (a.shape[0], cols) for a, cols in srcs]
    return [jax.ShapeDtypeStruct((N_DEV,) + s, a.dtype) for s, (a, _) in zip(shapes, srcs)]


def _exchange_sems(n):
    return [pltpu.SemaphoreType.DMA((N_DEV - 1, n)), pltpu.SemaphoreType.DMA((N_DEV - 1, n)),
            pltpu.SemaphoreType.DMA((n,))]


def _exchange_ops(col_widths, x_refs, out_refs, send_sems, recv_sems, local_sems):
    n = len(col_widths)
    pos = _mesh_pos()
    me = 4 * pos[0] + 2 * pos[1] + pos[2]

    def piece(i, d):
        cols = col_widths[i]
        if cols is None:
            return x_refs[i].at[d]
        if cols == WHOLE:
            return x_refs[i]
        return x_refs[i].at[:, pl.ds(pl.multiple_of(d * cols, 128), cols)]

    def local(i):
        return pltpu.make_async_copy(piece(i, me), out_refs[i].at[me], local_sems.at[i])

    def remote(k, i, landing):
        peer, idx = _peer(pos, k)
        return pltpu.make_async_remote_copy(
            src_ref=piece(i, idx), dst_ref=out_refs[i].at[idx if landing else me],
            send_sem=send_sems.at[k - 1, i], recv_sem=recv_sems.at[k - 1, i],
            device_id=peer, device_id_type=pl.DeviceIdType.MESH)

    pairs = [(k, i) for k in range(1, N_DEV) for i in range(n)]

    def start():
        for i in range(n):
            local(i).start()
        for k, i in pairs:
            remote(k, i, False).start()

    def wait():
        for k, i in pairs:
            remote(k, i, True).wait_recv()
        for k, i in pairs:
            remote(k, i, False).wait_send()
        for i in range(n):
            local(i).wait()

    return start, wait


def all_to_all_blocks(name, srcs):
    n = len(srcs)

    def body(*refs):
        start, wait = _exchange_ops([c for _, c in srcs], refs[:n], refs[n:2 * n], *refs[2 * n:])
        start()
        wait()

    return pl.pallas_call(
        body, name=name,
        in_specs=[pl.BlockSpec(memory_space=pl.ANY)] * n,
        out_specs=[pl.BlockSpec(memory_space=pl.ANY)] * n,
        out_shape=_exchange_shapes(srcs),
        scratch_shapes=_exchange_sems(n),
        compiler_params=pltpu.CompilerParams(has_side_effects=True),
    )(*[a for a, _ in srcs])


def _adam_row_tile(R, C):
    best = None
    for t in range(16, R + 1, 16):
        if R % t == 0 and t * C <= ADAM_TILE_ELEMS:
            best = t
    return best if best is not None else R


def reduce_adamw(name, parts, w, m, v):
    _, R, C = parts.shape
    tr = _adam_row_tile(R, C)

    def fn(pid, parts_t, w_t, m_t, v_t):
        g = parts_t[0].astype(F32)
        for i in range(1, N_DEV):
            g = g + parts_t[i].astype(F32)
        m_n = ADAM_B1 * m_t + (1.0 - ADAM_B1) * g
        v_n = ADAM_B2 * v_t + (1.0 - ADAM_B2) * (g * g)
        m_hat = m_n / (1.0 - ADAM_B1 ** ADAM_STEP)
        v_hat = v_n / (1.0 - ADAM_B2 ** ADAM_STEP)
        delta = -ADAM_LR * (m_hat / (jnp.sqrt(v_hat) + ADAM_EPS) + ADAM_WD * w_t)
        return g, delta, m_n, v_n

    row = lambda a: (a, (tr, C), lambda i: (i, 0))
    out = ((R, C), F32, (tr, C), lambda i: (i, 0), None)
    return tile_call(name, fn, (R // tr,),
                     [(parts, (N_DEV, tr, C), lambda i: (0, i, 0)), row(w), row(m), row(v)], [out] * 4)


PARAMS = (
    ("g_mix", (1, 1024), None), ("w_in", (1024, 4128), 1), ("rw_mu", (1, 1824), None), ("rw_w0", (1, 512), None),
    ("rw_w_up", (64, 512), 1), ("rw_a0", (1, 512), None), ("rw_a_up", (64, 512), 1), ("rw_g_up", (160, 512), 1),
    ("rw_k_k", (1, 512), None), ("rw_k_a", (1, 512), None), ("rw_r_k", (8, 64), None), ("rw_ln_g", (1, 512), None),
    ("rw_ln_b", (1, 512), None), ("w_branch_a", (512, 1024), 1), ("w_branch_b", (256, 1024), 1),
    ("w_gate", (1024, 2048), 1), ("b_gate", (1, 2048), None), ("w_out", (1024, 1024), 0), ("g_ffn", (1, 1024), None),
    ("w_up", (1024, 6144), 1), ("conv_w", (3, 6144), 1), ("conv_b", (1, 6144), None), ("w_down", (3072, 1024), 0),
    ("g_ple", (1, 1024), None), ("w_ple_gate", (1024, 1024), 0), ("w_ple", (256, 1024), 1), ("g_final", (1, 1024), None),
)
SHARDED = tuple(q for q in PARAMS if q[2] is not None)
REPLICATED = tuple(q for q in PARAMS if q[2] is None)
BIG_NAMES = ("w_in", "w_up", "w_gate", "w_out", "w_down", "w_ple_gate", "w_branch_a", "w_branch_b", "w_ple")
BIG = tuple(q for q in SHARDED if q[0] in BIG_NAMES)
SMALL_SHARDED = tuple(q for q in SHARDED if q[0] not in BIG_NAMES)
PACK_COLS = 1024
F32_GATHERED = ("conv_w",)


def _local_shape(shape, axis):
    s = list(shape)
    s[axis] //= N_DEV
    return tuple(s)


def _numel(shape):
    return int(np.prod(shape))


def _pad_flat(z, mult):
    n = z.shape[-1]
    total = -(-n // mult) * mult
    return jnp.pad(z, [(0, 0)] * (z.ndim - 1) + [(0, total - n)])


def _full_from_slots(slots, shape, axis):
    loc = _local_shape(shape, axis)
    z = slots.reshape((N_DEV,) + loc)
    if axis == 0:
        return z.reshape(shape)
    return z.transpose(1, 0, 2).reshape(shape)


def _slots_from_full(full, shape, axis):
    loc = _local_shape(shape, axis)
    if axis == 0:
        return full.reshape(N_DEV, _numel(loc))
    return full.reshape(shape[0], N_DEV, loc[1]).transpose(1, 0, 2).reshape(N_DEV, _numel(loc))


W_IN_SLOT = 640
W_IN_LOCAL = 4128 // N_DEV


def _block_shape(shape, axis):
    return _local_shape(shape, axis) if axis is not None else shape


def _pad_w_in(block):
    return jnp.pad(block, ((0, 0), (0, W_IN_SLOT - W_IN_LOCAL)))


def _proj_col(s):
    return s + jnp.where(s >= 1600, 64, 0) + jnp.where(s >= 1664, 64, 0) + jnp.where(s >= 1824, 96, 0)


def _perm_tile(d, c0, width):
    j = lax.broadcasted_iota(jnp.int32, (W_IN_SLOT, width), 0)
    c = c0 + lax.broadcasted_iota(jnp.int32, (W_IN_SLOT, width), 1)
    hit = (_proj_col(d * W_IN_LOCAL + j) == c) & (j < W_IN_LOCAL)
    return jnp.where(hit, 1.0, 0.0).astype(BF16)


PERM_TILE = 768


def w_in_unshuffle(slots):
    _, K, _ = slots.shape
    tn = PERM_TILE
    reach = 3

    def first_slot(j):
        return j + jnp.where(j >= 3, 1, 0) + jnp.where(j >= 5, 1, 0)

    def body(a_ref, o_ref, acc_ref):
        j, kk = pl.program_id(0), pl.program_id(1)
        d = first_slot(j) + kk

        @pl.when(kk == 0)
        def _():
            acc_ref[...] = jnp.zeros_like(acc_ref)

        @pl.when(d < N_DEV)
        def _():
            acc_ref[...] += jnp.dot(a_ref[0], _perm_tile(d, j * tn, tn), preferred_element_type=F32)

        @pl.when(kk == reach - 1)
        def _():
            o_ref[...] = acc_ref[...].astype(o_ref.dtype)

    return pl.pallas_call(
        body, name="w_in_unshuffle", grid=(PROJ_PAD // tn, reach),
        in_specs=[pl.BlockSpec((1, K, W_IN_SLOT), lambda j, kk: (jnp.minimum(first_slot(j) + kk, N_DEV - 1), 0, 0))],
        out_specs=pl.BlockSpec((K, tn), lambda j, kk: (0, j)),
        out_shape=jax.ShapeDtypeStruct((K, PROJ_PAD), BF16),
        scratch_shapes=[pltpu.VMEM((K, tn), F32)],
        compiler_params=_cparams(2),
    )(slots)


def w_in_shuffle_grad(dw):
    K = dw.shape[0]
    tk = PERM_TILE

    def first_tile(d):
        return _proj_col(d * W_IN_LOCAL) // tk

    def body(g_ref, o_ref, acc_ref):
        d, kk = pl.program_id(0), pl.program_id(1)
        perm = _perm_tile(d, (first_tile(d) + kk) * tk, tk)
        part = lax.dot_general(g_ref[...].astype(BF16), perm, NT_DIMS, preferred_element_type=F32)

        @pl.when(kk == 0)
        def _():
            acc_ref[...] = part

        @pl.when(kk == 1)
        def _():
            o_ref[0] = (acc_ref[...] + part).astype(o_ref.dtype)

    return pl.pallas_call(
        body, name="w_in_shuffle_grad", grid=(N_DEV, 2),
        in_specs=[pl.BlockSpec((K, tk), lambda d, kk: (0, first_tile(d) + kk))],
        out_specs=pl.BlockSpec((1, K, W_IN_SLOT), lambda d, kk: (d, 0, 0)),
        out_shape=jax.ShapeDtypeStruct((N_DEV, K, W_IN_SLOT), GRAD_WIRE),
        scratch_shapes=[pltpu.VMEM((K, W_IN_SLOT), F32)],
        compiler_params=_cparams(2),
    )(dw)


def _flat_rows(pieces, dtype, row_mult):
    flat = jnp.concatenate([z.astype(dtype) for z in pieces], axis=-1)
    flat = _pad_flat(flat, row_mult * PACK_COLS)
    return flat.reshape(flat.shape[:-1] + (-1, PACK_COLS))


FIRST = tuple(q for q in BIG if q[0] in ("w_in", "w_gate"))
LATE = tuple(q for q in BIG if q not in FIRST)


def _matrix_from_slots(slots, shape, axis):
    return slots.reshape(shape) if axis == 0 else slots.transpose(1, 0, 2).reshape(shape)


def _late_weight_sources(W):
    return [(blk, WHOLE) for blk in W["_late_blocks"]]


def _late_weights(slots):
    return {n: _matrix_from_slots(s, shape, axis) for (n, shape, axis), s in zip(LATE, slots)}


def gather_weights(local):
    blocks = [(_pad_w_in(local[n]) if n == "w_in" else local[n]).astype(BF16) for n, _, _ in FIRST]
    small = [q for q in SMALL_SHARDED if q[0] not in F32_GATHERED]
    exact = [q for q in SMALL_SHARDED if q[0] in F32_GATHERED]
    blocks.append(_flat_rows([local[n].reshape(-1) for n, _, _ in small], BF16, 16))
    blocks.append(_flat_rows([local[n].reshape(-1) for n, _, _ in exact], F32, 8))
    got = all_gather_blocks("weight_all_gather", blocks)
    full = {"_late_blocks": [local[n].astype(BF16) for n, _, _ in LATE]}
    for (n, shape, axis), slots in zip(FIRST, got):
        if n == "w_in":
            full["w_in_p"] = w_in_unshuffle(slots)
        else:
            full[n] = _matrix_from_slots(slots, shape, axis)
    for group, slots in ((small, got[-2]), (exact, got[-1])):
        slots, off = slots.reshape(N_DEV, -1), 0
        for n, shape, axis in group:
            size = _numel(_local_shape(shape, axis))
            full[n] = _full_from_slots(slots[:, off:off + size], shape, axis)
            off += size
    for n, _, _ in REPLICATED:
        full[n] = local[n]
    return full


LOSS_SLOT = ("_loss", (1, 2), None)
PACKED_SMALL = SMALL_SHARDED + REPLICATED + (LOSS_SLOT,)


def _pack_small(vals):
    pieces = [vals[n].reshape(-1) if n in vals else jnp.zeros((_numel(shape),), F32) for n, shape, _ in PACKED_SMALL]
    return _flat_rows(pieces, F32, 16)


def _unpack_small(packed):
    flat, out, off = packed.reshape(-1), {}, 0
    for n, shape, axis in PACKED_SMALL:
        loc = _block_shape(shape, axis)
        out[n] = flat[off:off + _numel(loc)].reshape(loc)
        off += _numel(loc)
    return out


EARLY = tuple(q for q in BIG if q[0] != "w_in")


def _early_grad_sources(G):
    srcs = []
    for n, shape, axis in EARLY:
        if axis == 0:
            srcs.append((G[n].astype(GRAD_WIRE).reshape((N_DEV,) + _local_shape(shape, axis)), None))
        else:
            srcs.append((G[n].astype(GRAD_WIRE), shape[1] // N_DEV))
    return srcs


def _late_grad_sources(G, loss_local):
    srcs = [(w_in_shuffle_grad(G["w_in_p"]), None)]
    rows = [_slots_from_full(G[n].reshape(shape), shape, axis) for n, shape, axis in SMALL_SHARDED]
    loss_hi = loss_local.astype(GRAD_WIRE).astype(F32)
    rep = jnp.concatenate([G[n].reshape(-1) for n, _, _ in REPLICATED] + [jnp.stack([loss_hi, loss_local - loss_hi])])
    rows.append(jnp.broadcast_to(rep[None, :], (N_DEV, rep.shape[0])))
    srcs.append((_flat_rows(rows, GRAD_WIRE, 16), None))
    return srcs


def _step(x, p, target, local_w, local_m, local_v):
    full = gather_weights(local_w)
    loss_local, dx, G = local_step(x, p, target, full)
    late = all_to_all_blocks("grad_all_to_all", _late_grad_sources(G, loss_local))
    parts = [late[0]] + list(G["_early_parts"]) + [late[1]]
    outs = [{}, {}, {}, {}]
    for (n, shape, axis), part in zip((BIG[0],) + EARLY, parts):
        prep = _pad_w_in if n == "w_in" else (lambda z: z)
        res = reduce_adamw("adamw_" + n, part, prep(local_w[n]), prep(local_m[n]), prep(local_v[n]))
        for o, z in zip(outs, res):
            o[n] = z[:, :W_IN_LOCAL] if n == "w_in" else z
    res = reduce_adamw("adamw_small", parts[-1], _pack_small(local_w), _pack_small(local_m), _pack_small(local_v))
    for o, z in zip(outs, res):
        o.update(_unpack_small(z))
    loss = jnp.sum(outs[0]["_loss"])
    return loss, dx, outs


def kernel(x, p, g_mix, w_in, rw_mu, rw_w0, rw_w_up, rw_a0, rw_a_up, rw_g_up, rw_k_k, rw_k_a, rw_r_k, rw_ln_g, rw_ln_b, w_branch_a, w_branch_b, w_gate, b_gate, w_out, g_ffn, w_up, conv_w, conv_b, w_down, g_ple, w_ple_gate, w_ple, g_final, loss_target, m_g_mix, m_w_in, m_rw_mu, m_rw_w0, m_rw_w_up, m_rw_a0, m_rw_a_up, m_rw_g_up, m_rw_k_k, m_rw_k_a, m_rw_r_k, m_rw_ln_g, m_rw_ln_b, m_w_branch_a, m_w_branch_b, m_w_gate, m_b_gate, m_w_out, m_g_ffn, m_w_up, m_conv_w, m_conv_b, m_w_down, m_g_ple, m_w_ple_gate, m_w_ple, m_g_final, v_g_mix, v_w_in, v_rw_mu, v_rw_w0, v_rw_w_up, v_rw_a0, v_rw_a_up, v_rw_g_up, v_rw_k_k, v_rw_k_a, v_rw_r_k, v_rw_ln_g, v_rw_ln_b, v_w_branch_a, v_w_branch_b, v_w_gate, v_b_gate, v_w_out, v_g_ffn, v_w_up, v_conv_w, v_conv_b, v_w_down, v_g_ple, v_w_ple_gate, v_w_ple, v_g_final):
    args = dict(locals())
    names = [n for n, _, _ in PARAMS]
    orig_shape = {n: args[n].shape for n in names}

    def strip(prefix):
        out = {}
        for n, shape, axis in PARAMS:
            a = args[prefix + n]
            loc = _local_shape(shape, axis) if axis is not None else shape
            out[n] = a.reshape(loc)
        return out

    local_w, local_m, local_v = strip(""), strip("m_"), strip("v_")
    T, D = x.shape[-2], x.shape[-1]
    loss, dx, (g, delta, m_n, v_n) = _step(x.reshape(T, D), p.reshape(T, p.shape[-1]), loss_target.reshape(T, D),
                                           local_w, local_m, local_v)
    outs = [loss, dx.reshape(x.shape)]
    for group in (g, delta, m_n, v_n):
        outs += [group[n].reshape(orig_shape[n]) for n in names]
    return tuple(outs)
```

```python
import functools
import math

import numpy as np
import jax
import jax.numpy as jnp
from jax import lax
from jax.experimental import pallas as pl
from jax.experimental.pallas import tpu as pltpu

F32 = jnp.float32
BF16 = jnp.bfloat16
GRAD_WIRE = jnp.bfloat16

N_DEV = 8
NORM_EPS = 1e-6
RW_LN_EPS = 64e-5
HEAD = 64
RW_WIDTH = 512
ATT_GROUPS = ((128, 1), (512, 4), (2048, 16))
ATT_HEADS = 12
ATT_OUT = 256
ATT_COLS = 2304
OFF_XW, OFF_XA, OFF_XG, RW_PAD, PROJ_PAD = 1536, 1664, 1792, 2048, 4608
PROJ_TAIL = PROJ_PAD - RW_PAD - ATT_COLS
D_FF = 3072

ADAM_LR, ADAM_B1, ADAM_B2, ADAM_EPS, ADAM_WD, ADAM_STEP = 0.001, 0.9, 0.999, 1e-08, 0.01, 10

VMEM_LIMIT_BYTES = 56 * 1024 * 1024
ADAM_TILE_ELEMS = 256 * 1024
NEG_BIG = -1e30

NT_DIMS = (((1,), (1,)), ((), ()))
TN_DIMS = (((0,), (0,)), ((), ()))
NN_DIMS = (((1,), (0,)), ((), ()))


def _cparams(n_axes):
    return pltpu.CompilerParams(dimension_semantics=("arbitrary",) * n_axes,
                                vmem_limit_bytes=VMEM_LIMIT_BYTES)


def _split2(x):
    hi = x.astype(BF16)
    lo = (x - hi.astype(F32)).astype(BF16)
    return hi, lo


def _seg_mat(n):
    r = lax.shift_right_logical(lax.broadcasted_iota(jnp.int32, (n, n), 0), 6)
    c = lax.shift_right_logical(lax.broadcasted_iota(jnp.int32, (n, n), 1), 6)
    return jnp.where(r == c, 1.0, 0.0).astype(BF16)


def _segb(x, seg):
    return _segb_stack([(x, 2)], seg)[0]


def _segb_stack(items, seg):
    rows = items[0][0].shape[0]
    parts = []
    for x, passes in items:
        parts += list(_split2(x)) if passes == 2 else [x.astype(BF16)]
    res = jnp.dot(jnp.concatenate(parts, axis=0), seg, preferred_element_type=F32)
    out, at = [], 0
    for _, passes in items:
        piece = res[at * rows:(at + 1) * rows]
        if passes == 2:
            piece = piece + res[(at + 1) * rows:(at + 2) * rows]
        out.append(piece)
        at += passes
    return out


def _segb1(x, seg):
    return jnp.dot(x.astype(BF16), seg, preferred_element_type=F32)


@jax.custom_vjp
def segsum(x):
    return _segb(x, _seg_mat(x.shape[1]))


def _segsum_fwd(x):
    return segsum(x), None


def _segsum_bwd(_, ct):
    return (segsum(ct),)


segsum.defvjp(_segsum_fwd, _segsum_bwd)


@jax.custom_vjp
def bdot(a, b):
    return jnp.dot(a.astype(BF16), b.astype(BF16), preferred_element_type=F32)


def _bdot_fwd(a, b):
    return bdot(a, b), (a, b)


def _bdot_bwd(res, ct):
    a, b = res
    ctb = ct.astype(BF16)
    da = lax.dot_general(ctb, b.astype(BF16), NT_DIMS, preferred_element_type=F32)
    db = lax.dot_general(a.astype(BF16), ctb, TN_DIMS, preferred_element_type=F32)
    return da.astype(a.dtype), db.astype(b.dtype)


bdot.defvjp(_bdot_fwd, _bdot_bwd)


def _sig(x):
    return 1.0 / (1.0 + jnp.exp(-x))


def _softplus(z):
    return jnp.maximum(z, 0.0) + jnp.log(1.0 + jnp.exp(-jnp.abs(z)))


def _gelu_tanh(x):
    return 0.5 * x * (1.0 + jnp.tanh(0.7978845608028654 * (x + 0.044715 * (x * x * x))))


def _rms(x, g):
    return x * lax.rsqrt(jnp.mean(x * x, axis=-1, keepdims=True) + NORM_EPS) * g


def _shift_down(x, prev8, n):
    rolled = pltpu.roll(x, n, 0)
    top = pltpu.roll(prev8, n, 0)
    rid = lax.broadcasted_iota(jnp.int32, (8, x.shape[1]), 0)
    head = jnp.where(rid < n, top, rolled[:8])
    return jnp.concatenate([head, rolled[8:]], axis=0)


def _shift_up(x, next8, n):
    rows = x.shape[0]
    rolled = pltpu.roll(x, rows - n, 0)
    bottom = pltpu.roll(next8, 8 - n, 0)
    rid = lax.broadcasted_iota(jnp.int32, (8, x.shape[1]), 0)
    tail = jnp.where(rid >= 8 - n, bottom, rolled[rows - 8:])
    return jnp.concatenate([rolled[:rows - 8], tail], axis=0)


def tile_call(name, fn, grid, ins, outs, scratch=()):
    n_in, n_out = len(ins), len(outs)
    acc_axes = [o[4] for o in outs]

    def body(*refs):
        pids = tuple(pl.program_id(a) for a in range(len(grid)))
        vals = fn(pids, *[r[...] for r in refs[:n_in]], *refs[n_in + n_out:])
        if not isinstance(vals, (tuple, list)):
            vals = (vals,)
        for o_ref, val, ax in zip(refs[n_in:n_in + n_out], vals, acc_axes):
            if ax is None:
                o_ref[...] = val.astype(o_ref.dtype)
            else:
                @pl.when(pids[ax] == 0)
                def _(o_ref=o_ref):
                    o_ref[...] = jnp.zeros_like(o_ref)

                o_ref[...] += val.astype(o_ref.dtype)

    res = pl.pallas_call(
        body, name=name, grid=grid,
        in_specs=[pl.BlockSpec(b, im) for _, b, im in ins],
        out_specs=[pl.BlockSpec(o[2], o[3]) for o in outs],
        out_shape=[jax.ShapeDtypeStruct(o[0], o[1]) for o in outs],
        scratch_shapes=[pltpu.VMEM(s, d) for s, d in scratch],
        compiler_params=_cparams(len(grid)),
    )(*[a for a, _, _ in ins])
    return res


def _rows(a, tm):
    return (a, (tm, a.shape[1]), lambda i: (i, 0))


def _par(a):
    return (a, a.shape, lambda i: (0, 0))


def _row_out(T, C, dtype, tm):
    return ((T, C), dtype, (tm, C), lambda i: (i, 0), None)


def _acc_out(R, C):
    return ((R, C), F32, (R, C), lambda i: (0, 0), 0)


def _prev_halo(a, tm, C):
    return (a, (8, C), lambda i: (jnp.maximum(i * (tm // 8) - 1, 0), 0))


def _next_halo(a, tm, C, T):
    return (a, (8, C), lambda i: (jnp.minimum((i + 1) * (tm // 8), T // 8 - 1), 0))


def _pick(n, target):
    for t in (target, 2048, 1536, 1024, 768, 512, 384, 256, 128):
        if t <= target and n % t == 0:
            return t
    return n


def matmul(name, a, b, mode="nn", res=None, out_dtype=F32, tm=1024, tn=2048, tk=2048, exchange=()):
    if mode == "nn":
        (M, K), (K2, N) = a.shape, b.shape
    elif mode == "tn":
        (K, M), (K2, N) = a.shape, b.shape
    else:
        (M, K), (N, K2) = a.shape, b.shape
    assert K == K2, (name, a.shape, b.shape, mode)
    tm, tn, tk = _pick(M, tm), _pick(N, tn), _pick(K, tk)
    nk = K // tk
    dims = {"nn": NN_DIMS, "tn": TN_DIMS, "nt": NT_DIMS}[mode]
    a_spec = {"nn": pl.BlockSpec((tm, tk), lambda i, j, k: (i, k)),
              "tn": pl.BlockSpec((tk, tm), lambda i, j, k: (k, i)),
              "nt": pl.BlockSpec((tm, tk), lambda i, j, k: (i, k))}[mode]
    b_spec = {"nn": pl.BlockSpec((tk, tn), lambda i, j, k: (k, j)),
              "tn": pl.BlockSpec((tk, tn), lambda i, j, k: (k, j)),
              "nt": pl.BlockSpec((tn, tk), lambda i, j, k: (j, k))}[mode]
    has_res = res is not None
    nx = len(exchange)
    grid = (M // tm, N // tn, nk)

    def body(*refs):
        a_ref, b_ref = refs[:2]
        r_ref = refs[2] if has_res else None
        refs = refs[2 + has_res:]
        x_refs, o_ref, land_refs, acc_ref = refs[:nx], refs[nx], refs[nx + 1:2 * nx + 1], refs[2 * nx + 1]
        k = pl.program_id(2)
        if nx:
            step = (pl.program_id(0) * grid[1] + pl.program_id(1)) * nk + k
            start, wait = _exchange_ops([c for _, c in exchange], x_refs, land_refs, *refs[2 * nx + 2:])

            @pl.when(step == 0)
            def _():
                start()

        @pl.when(k == 0)
        def _():
            acc_ref[...] = jnp.zeros_like(acc_ref)

        acc_ref[...] += lax.dot_general(a_ref[...].astype(BF16), b_ref[...].astype(BF16), dims,
                                        preferred_element_type=F32)

        @pl.when(k == nk - 1)
        def _():
            out = acc_ref[...]
            if has_res:
                out = out + r_ref[...].astype(F32)
            o_ref[...] = out.astype(o_ref.dtype)

        if nx:
            @pl.when(step == grid[0] * grid[1] * nk - 1)
            def _():
                wait()

    in_specs = [a_spec, b_spec]
    args = [a, b]
    if has_res:
        in_specs.append(pl.BlockSpec((tm, tn), lambda i, j, k: (i, j)))
        args.append(res)
    hbm = pl.BlockSpec(memory_space=pl.ANY)
    out = pl.pallas_call(
        body, name=name, grid=grid,
        in_specs=in_specs + [hbm] * nx,
        out_specs=[pl.BlockSpec((tm, tn), lambda i, j, k: (i, j))] + [hbm] * nx,
        out_shape=[jax.ShapeDtypeStruct((M, N), out_dtype)] + _exchange_shapes(exchange),
        scratch_shapes=[pltpu.VMEM((tm, tn), F32)] + (_exchange_sems(nx) if nx else []),
        compiler_params=pltpu.CompilerParams(dimension_semantics=("arbitrary",) * 3, vmem_limit_bytes=VMEM_LIMIT_BYTES,
                                             has_side_effects=bool(nx)),
    )(*args, *[z for z, _ in exchange])
    return (out[0], out[1:]) if nx else out[0]


def rw_pre(Pc, Ps, mu, w0, w_up, a0, a_up, g_up, k_k, k_a):
    Pm = Pc + (Ps - Pc) * mu
    r, k, v = Pm[:, 0:512], Pm[:, 512:1024], Pm[:, 1024:1536]
    xw, xa, xg = Pm[:, OFF_XW:OFF_XA], Pm[:, OFF_XA:OFF_XG], Pm[:, OFF_XG:RW_PAD]
    w = -_softplus(-(w0 + bdot(jnp.tanh(xw), w_up))) - 0.5
    decay = jnp.exp(-jnp.exp(w))
    a = _sig(a0 + bdot(xa, a_up))
    g = bdot(_sig(xg), g_up)
    kk = k * k_k
    kk = kk / jnp.maximum(jnp.sqrt(segsum(kk * kk)), 1e-12)
    k2 = k * (1.0 + (a - 1.0) * k_a)
    return r, decay, k2, v, -kk, kk * a, g


def rw_post(y, r, k2, v, g, ln_g, ln_b, r_k):
    mean = segsum(y) * (1.0 / HEAD)
    d = y - mean
    var = segsum(d * d) * (1.0 / HEAD)
    yn = d * lax.rsqrt(var + RW_LN_EPS) * ln_g + ln_b
    bonus = segsum(r * k2 * r_k) * v
    return (yn + bonus) * g


def att_combine(o1, o2, o3, l1, l2, l3):
    m = jnp.maximum(jnp.maximum(l1, l2), l3)
    e1, e2, e3 = jnp.exp(l1 - m), jnp.exp(l2 - m), jnp.exp(l3 - m)
    return (e1 * o1 + e2 * o2 + e3 * o3) / (e1 + e2 + e3)


def merge_fn(gp, bg, za, zb):
    s = _sig(gp + bg)
    half = za.shape[1]
    return s[:, :half] * za + s[:, half:] * zb


def tail_loss(x2, zg, pe, g_final, target):
    x3 = x2 + _sig(zg) * pe
    y = _rms(x3, g_final)
    err = (y - target) * (y - target)
    return 0.5 * jnp.sum(jnp.mean(err, axis=-1, keepdims=True))


SCAN_CHUNK = HEAD
SCAN_LANES = 256
SCAN_UNROLL_FWD, SCAN_UNROLL_BWD = 8, 8


def _to_head_time(z):
    T = z.shape[0]
    return z.reshape(T // HEAD, HEAD, RW_WIDTH // HEAD, HEAD).transpose(0, 3, 2, 1).reshape(T // HEAD, HEAD, RW_WIDTH)


def _from_head_time(zt):
    C = zt.shape[0]
    return zt.reshape(C, HEAD, RW_WIDTH // HEAD, HEAD).transpose(0, 3, 2, 1).reshape(C * HEAD, RW_WIDTH)


def _unrolled_loop(n, step, init, unroll):
    def body(i, carry):
        for j in range(unroll):
            carry = step(i * unroll + j, carry)
        return carry

    return lax.fori_loop(0, n // unroll, body, init)


def _lane_groups():
    return [slice(j * SCAN_LANES, (j + 1) * SCAN_LANES) for j in range(RW_WIDTH // SCAN_LANES)]


def scan_pair_terms(a, w, b, k, tm=512):
    T = a.shape[0]

    def fn(pid, a_t, nxt, w_t, b_t, k_t):
        a_next = _shift_up(a_t, jnp.where(pid[0] < T // tm - 1, nxt, 0.0), 1)
        return w_t * a_next, segsum(b_t * a_next), segsum(k_t * a_next)

    return tile_call("scan_pair_terms", fn, (T // tm,),
                     [_rows(a, tm), _next_halo(a, tm, RW_WIDTH, T), _rows(w, tm), _rows(b, tm), _rows(k, tm)],
                     [_row_out(T, RW_WIDTH, F32, tm)] * 3)


def rwkv_scan_fwd(a, w, b, k, r, vT, wa, ba, ka, exchange=()):
    T = a.shape[0]
    C, LW = SCAN_CHUNK, SCAN_LANES
    nC = T // C
    nx = len(exchange)

    def body(*refs):
        a_ref, w_ref, b_ref, k_ref, r_ref, vT_ref, wa_ref, ba_ref, ka_ref = refs[:9]
        x_refs, refs = refs[9:9 + nx], refs[9 + nx:]
        yT_ref, S_ref, saT_ref = refs[:3]
        land_refs, refs = refs[3:3 + nx], refs[3 + nx:]
        st_ref, vb0_ref, vb1_ref, seg_ref = refs[:4]
        if nx:
            start, wait = _exchange_ops([c for _, c in exchange], x_refs, land_refs, *refs[4:])

        @pl.when(pl.program_id(0) == 0)
        def _():
            st_ref[...] = jnp.zeros_like(st_ref)
            seg_ref[...] = _seg_mat(LW)
            if nx:
                start()

        seg = seg_ref[...]
        lane = jnp.bitwise_and(lax.broadcasted_iota(jnp.int32, (1, LW), 1), HEAD - 1)
        groups = _lane_groups()

        def vsel(t, gsl):
            return jnp.where(lane == t, vT_ref[0, :, gsl], 0.0)

        first = _segb_stack([(vsel(s, gsl), 1) for gsl in groups for s in (0, 1)], seg)
        for g, gsl in enumerate(groups):
            vb0_ref[:, gsl] = first[2 * g]
            vb1_ref[:, gsl] = first[2 * g + 1]
        saT_ref[...] = jnp.zeros_like(saT_ref)

        def pair(i, yacc):
            t = 2 * i
            t1 = t + 1
            tp = jnp.maximum(t - 1, 0)
            row = lambda ref, s, gsl: ref[pl.ds(s, 1), gsl]
            Sps = [st_ref[:, gsl] for gsl in groups]
            chain = _segb_stack([(Sp * row(ref, t, gsl), 2) for gsl, Sp in zip(groups, Sps) for ref in (a_ref, wa_ref)],
                                seg)
            sas, us = chain[0::2], chain[1::2]
            S1s = []
            for gsl, Sp, sa, u in zip(groups, Sps, sas, us):
                vb0, vb1 = vb0_ref[:, gsl], vb1_ref[:, gsl]
                S1 = Sp * row(w_ref, t, gsl) + sa * row(b_ref, t, gsl) + vb0 * row(k_ref, t, gsl)
                sa1 = u + sa * row(ba_ref, t, gsl) + vb0 * row(ka_ref, t, gsl)
                st_ref[:, gsl] = S1 * row(w_ref, t1, gsl) + sa1 * row(b_ref, t1, gsl) + vb1 * row(k_ref, t1, gsl)
                S_ref[0, t, :, gsl] = Sp
                S_ref[0, t1, :, gsl] = S1
                S1s.append(S1)
                saT_ref[0, :, gsl] = jnp.where(lane == t, sa, jnp.where(lane == t1, sa1, saT_ref[0, :, gsl]))
            side = _segb_stack([(x, 1) for gsl, Sp, S1 in zip(groups, Sps, S1s)
                                for x in (Sp * row(r_ref, tp, gsl), S1 * row(r_ref, t, gsl),
                                          vsel(t + 2, gsl), vsel(t + 3, gsl))], seg)
            out = []
            for g, (gsl, ya) in enumerate(zip(groups, yacc)):
                yb0, yb1, vb0_ref[:, gsl], vb1_ref[:, gsl] = side[4 * g:4 * g + 4]
                out.append(jnp.where(lane == t, yb1, jnp.where(lane == t - 1, yb0, ya)))
            return tuple(out)

        yacc = _unrolled_loop(C // 2, pair, tuple(jnp.zeros((HEAD, LW), F32) for _ in groups), SCAN_UNROLL_FWD)
        for gsl, ya in zip(groups, yacc):
            S_last = st_ref[:, gsl]
            S_ref[0, C, :, gsl] = S_last
            yb = _segb1(S_last * r_ref[pl.ds(C - 1, 1), gsl], seg)
            yT_ref[0, :, gsl] = jnp.where(lane == C - 1, yb, ya)

        if nx:
            @pl.when(pl.program_id(0) == nC - 1)
            def _():
                wait()

    row = pl.BlockSpec((C, RW_WIDTH), lambda c: (c, 0))
    ht = pl.BlockSpec((1, HEAD, RW_WIDTH), lambda c: (c, 0, 0))
    hbm = pl.BlockSpec(memory_space=pl.ANY)
    res = pl.pallas_call(
        body, name="rwkv_scan_fwd", grid=(nC,),
        in_specs=[row, row, row, row, row, ht, row, row, row] + [hbm] * nx,
        out_specs=[ht, pl.BlockSpec((1, C + 1, HEAD, RW_WIDTH), lambda c: (c, 0, 0, 0)), ht] + [hbm] * nx,
        out_shape=[jax.ShapeDtypeStruct((nC, HEAD, RW_WIDTH), F32),
                   jax.ShapeDtypeStruct((nC, C + 1, HEAD, RW_WIDTH), F32),
                   jax.ShapeDtypeStruct((nC, HEAD, RW_WIDTH), F32)] + _exchange_shapes(exchange),
        scratch_shapes=[pltpu.VMEM((HEAD, RW_WIDTH), F32)] * 3 + [pltpu.VMEM((LW, LW), BF16)]
        + (_exchange_sems(nx) if nx else []),
        compiler_params=pltpu.CompilerParams(dimension_semantics=("arbitrary",), vmem_limit_bytes=VMEM_LIMIT_BYTES,
                                             has_side_effects=bool(nx)),
    )(a, w, b, k, r, vT, wa, ba, ka, *[z for z, _ in exchange])
    return res[:3], res[3:]


def rwkv_scan_bwd(a, w, b, k, r, v, dy, S_all, saT, exchange=()):
    T = a.shape[0]
    C, LW = SCAN_CHUNK, SCAN_LANES
    nC = T // C
    nx = len(exchange)
    n_heads = RW_WIDTH // HEAD
    dyT = _to_head_time(dy).astype(BF16)
    v_rows, dy_rows = v.reshape(T, n_heads, HEAD), dy.reshape(T, n_heads, HEAD)
    sa_rows = _from_head_time(saT).reshape(T, n_heads, HEAD)

    def body(*refs):
        a_ref, w_ref, b_ref, k_ref, r_ref, vR_ref, saR_ref, dyR_ref, dyT_ref, S_ref = refs[:10]
        x_refs, refs = refs[10:10 + nx], refs[10 + nx:]
        da_ref, dw_ref, db_ref, dk_ref, dr_ref, dvT_ref = refs[:6]
        land_refs, refs = refs[6:6 + nx], refs[6 + nx:]
        ds_ref, dyb_ref, seg_ref = refs[:3]
        if nx:
            start, wait = _exchange_ops([c for _, c in exchange], x_refs, land_refs, *refs[3:])

        @pl.when(pl.program_id(0) == 0)
        def _():
            ds_ref[...] = jnp.zeros_like(ds_ref)
            seg_ref[...] = _seg_mat(LW)
            if nx:
                start()

        seg = seg_ref[...]
        lane = jnp.bitwise_and(lax.broadcasted_iota(jnp.int32, (1, LW), 1), HEAD - 1)
        groups = _lane_groups()
        head_row = lax.broadcasted_iota(jnp.int32, (n_heads, LW), 0)
        lane_head = lax.shift_right_logical(lax.broadcasted_iota(jnp.int32, (n_heads, LW), 1), 6)

        def colsum(z):
            return jnp.sum(z, axis=0, keepdims=True)

        def dysel(t, gsl):
            return jnp.where(lane == t, dyT_ref[0, :, gsl], 0.0)

        for gsl, dyb in zip(groups, _segb_stack([(dysel(C - 1, gsl), 1) for gsl in groups], seg)):
            dyb_ref[:, gsl] = dyb

        def step(i, dvacc):
            t = C - 1 - i
            dybs = [dyb_ref[:, gsl] for gsl in groups]
            dSs = [ds_ref[:, gsl] + dyb * r_ref[pl.ds(t, 1), gsl] for gsl, dyb in zip(groups, dybs)]
            dsabs = _segb_stack([(dS * b_ref[pl.ds(t, 1), gsl], 2) for gsl, dS in zip(groups, dSs)], seg)
            for gsl, dS, dsab in zip(groups, dSs, dsabs):
                ds_ref[:, gsl] = dS * w_ref[pl.ds(t, 1), gsl] + dsab * a_ref[pl.ds(t, 1), gsl]
            out = []
            dy_rows = dyR_ref[t].astype(BF16)
            v_sa_rows = jnp.concatenate([vR_ref[t], saR_ref[t]], axis=0).astype(BF16)
            side = _segb_stack([(x, 1) for gsl, dS in zip(groups, dSs)
                                for x in (dS * k_ref[pl.ds(t, 1), gsl], dysel(t - 1, gsl))], seg)
            for g, (gsl, dva, dS, dsab) in enumerate(zip(groups, dvacc, dSs, dsabs)):
                dvb, dyb_ref[:, gsl] = side[2 * g:2 * g + 2]
                Sp = S_ref[0, t, :, gsl]
                own = head_row == lane_head + g * (LW // HEAD)

                def rows_in(rows, mat):
                    full = jnp.dot(rows, mat.astype(BF16), preferred_element_type=F32)
                    return [jnp.sum(jnp.where(own, full[s:s + n_heads], 0.0), axis=0, keepdims=True)
                            for s in range(0, rows.shape[0], n_heads)]

                (dr,) = rows_in(dy_rows, S_ref[0, t + 1, :, gsl])
                dk, db = rows_in(v_sa_rows, dS)
                dr_ref[pl.ds(t, 1), gsl] = dr
                dk_ref[pl.ds(t, 1), gsl] = dk
                db_ref[pl.ds(t, 1), gsl] = db
                dw_ref[pl.ds(t, 1), gsl] = colsum(dS * Sp)
                da_ref[pl.ds(t, 1), gsl] = colsum(Sp * dsab)
                out.append(jnp.where(lane == t, dvb, dva))
            return tuple(out)

        dvacc = _unrolled_loop(C, step, tuple(jnp.zeros((HEAD, LW), F32) for _ in groups), SCAN_UNROLL_BWD)
        for gsl, dva in zip(groups, dvacc):
            dvT_ref[0, :, gsl] = dva

        if nx:
            @pl.when(pl.program_id(0) == nC - 1)
            def _():
                wait()

    row = pl.BlockSpec((C, RW_WIDTH), lambda c: (nC - 1 - c, 0))
    ht = pl.BlockSpec((1, HEAD, RW_WIDTH), lambda c: (nC - 1 - c, 0, 0))
    hbm = pl.BlockSpec(memory_space=pl.ANY)
    per_head = pl.BlockSpec((C, n_heads, HEAD), lambda c: (nC - 1 - c, 0, 0))
    rows_shape = jax.ShapeDtypeStruct((T, RW_WIDTH), F32)
    res = pl.pallas_call(
        body, name="rwkv_scan_bwd", grid=(nC,),
        in_specs=[row, row, row, row, row, per_head, per_head, per_head, ht,
                  pl.BlockSpec((1, C + 1, HEAD, RW_WIDTH), lambda c: (nC - 1 - c, 0, 0, 0))] + [hbm] * nx,
        out_specs=[row, row, row, row, row, ht] + [hbm] * nx,
        out_shape=[rows_shape] * 5 + [jax.ShapeDtypeStruct((nC, HEAD, RW_WIDTH), F32)] + _exchange_shapes(exchange),
        scratch_shapes=[pltpu.VMEM((HEAD, RW_WIDTH), F32), pltpu.VMEM((HEAD, RW_WIDTH), F32),
                        pltpu.VMEM((LW, LW), BF16)] + (_exchange_sems(nx) if nx else []),
        compiler_params=pltpu.CompilerParams(dimension_semantics=("arbitrary",), vmem_limit_bytes=VMEM_LIMIT_BYTES,
                                             has_side_effects=bool(nx)),
    )(a, w, b, k, r, v_rows, sa_rows, dy_rows, dyT, S_all, *[z for z, _ in exchange])
    return res[:6], res[6:]


def _alibi_slope(h):
    return float(np.float32(2.0 ** (-8.0 * (h + 1) / ATT_HEADS)))


ATT_GROUP_HEADS = 4


def _stack_heads(x, lane_head, fill=0.0):
    return jnp.concatenate([jnp.where(lane_head == hh, x, fill) for hh in range(ATT_GROUP_HEADS)], axis=0)


def _unstack_heads(x, lane_head, L):
    out = jnp.zeros((L, x.shape[1]), F32)
    for hh in range(ATT_GROUP_HEADS):
        out = jnp.where(lane_head == hh, x[hh * L:(hh + 1) * L], out)
    return out


def _att_logits(qs, kcat, gi, d, L, n):
    qi = lax.broadcasted_iota(jnp.int32, (L, 2 * L), 0)
    kj = lax.broadcasted_iota(jnp.int32, (L, 2 * L), 1)
    steps = qi + L - kj
    valid = (steps >= 0) & (steps <= L) & ((kj >= L) | (n > 0))
    dist = (d * steps).astype(F32)
    bias = jnp.concatenate([jnp.where(valid, -_alibi_slope(gi * ATT_GROUP_HEADS + hh) * dist, NEG_BIG)
                            for hh in range(ATT_GROUP_HEADS)], axis=0)
    s = lax.dot_general(qs.astype(BF16), kcat, NT_DIMS, preferred_element_type=F32) * (HEAD ** -0.5)
    return jnp.where(bias > 0.5 * NEG_BIG, s + bias, NEG_BIG)


def att_fwd(pa, gi, T):
    window, d = ATT_GROUPS[gi]
    L = window // d
    Tj = T // d
    nb = Tj // L
    pv = pa.reshape(Tj, d * ATT_COLS)
    nblk = ATT_COLS // ATT_OUT

    def fn(pids, q, kp, kc, vp, vc):
        lane_head = lax.shift_right_logical(lax.broadcasted_iota(jnp.int32, (1, ATT_OUT), 1), 6)
        kcat = jnp.concatenate([kp, kc], axis=0).astype(BF16)
        vcat = jnp.concatenate([vp, vc], axis=0).astype(BF16)
        s = _att_logits(_stack_heads(q, lane_head), kcat, gi, d, L, pids[1])
        m = jnp.max(s, axis=-1, keepdims=True)
        p = jnp.exp(s - m)
        l = jnp.sum(p, axis=-1, keepdims=True)
        o = jnp.dot(p.astype(BF16), vcat, preferred_element_type=F32) / l
        lse = jnp.broadcast_to(m + jnp.log(l), o.shape)
        return _unstack_heads(o, lane_head, L), _unstack_heads(lse, lane_head, L)

    blk = (L, ATT_OUT)
    ins = [(pv, blk, lambda r, n: (n, r * nblk + gi)),
           (pv, blk, lambda r, n: (jnp.maximum(n - 1, 0), r * nblk + 3 + gi)),
           (pv, blk, lambda r, n: (n, r * nblk + 3 + gi)),
           (pv, blk, lambda r, n: (jnp.maximum(n - 1, 0), r * nblk + 6 + gi)),
           (pv, blk, lambda r, n: (n, r * nblk + 6 + gi))]
    out = ((Tj, d * ATT_OUT), F32, blk, lambda r, n: (n, r), None)
    o, lseb = tile_call(f"att_fwd_g{gi}", fn, (d, nb), ins, [out, out])
    return o.reshape(T, ATT_OUT), lseb.reshape(T, ATT_OUT)


def att_bwd(pa, o, lseb, do, dlseb, gi, T):
    window, d = ATT_GROUPS[gi]
    L = window // d
    Tj = T // d
    nb = Tj // L
    pv = pa.reshape(Tj, d * ATT_COLS)
    nblk = ATT_COLS // ATT_OUT
    view = lambda z: z.reshape(Tj, d * ATT_OUT)

    def body(q_ref, kp_ref, kc_ref, vp_ref, vc_ref, o_ref, l_ref, do_ref, dl_ref, dq_ref, dk_ref, dv_ref):
        n = pl.program_id(1)

        @pl.when(n == 0)
        def _():
            dk_ref[...] = jnp.zeros_like(dk_ref)
            dv_ref[...] = jnp.zeros_like(dv_ref)

        lane_head = lax.shift_right_logical(lax.broadcasted_iota(jnp.int32, (1, ATT_OUT), 1), 6)
        kcat = jnp.concatenate([kp_ref[...], kc_ref[...]], axis=0).astype(BF16)
        vcat = jnp.concatenate([vp_ref[...], vc_ref[...]], axis=0).astype(BF16)
        qs = _stack_heads(q_ref[...], lane_head)
        dos = _stack_heads(do_ref[...], lane_head)
        lse = jnp.max(_stack_heads(l_ref[...], lane_head, NEG_BIG), axis=-1, keepdims=True)
        dlse = jnp.sum(_stack_heads(dl_ref[...], lane_head), axis=-1, keepdims=True)
        delta = jnp.sum(dos * jnp.concatenate([o_ref[...]] * ATT_GROUP_HEADS, axis=0), axis=-1, keepdims=True)
        p = jnp.exp(_att_logits(qs, kcat, gi, d, L, n) - lse)
        dp = lax.dot_general(dos.astype(BF16), vcat, NT_DIMS, preferred_element_type=F32)
        ds = (p * (dp - delta + dlse)).astype(BF16)
        dq = _unstack_heads(jnp.dot(ds, kcat, preferred_element_type=F32), lane_head, L)
        dkc = lax.dot_general(ds, qs.astype(BF16), TN_DIMS, preferred_element_type=F32)
        dvc = lax.dot_general(p.astype(BF16), dos.astype(BF16), TN_DIMS, preferred_element_type=F32)
        scale = HEAD ** -0.5
        dq_ref[...] = dq * scale
        cur = pl.ds(pl.multiple_of(n * L, L), L)
        dk_ref[cur, :] += dkc[L:] * scale
        dv_ref[cur, :] += dvc[L:]

        @pl.when(n > 0)
        def _():
            prev = pl.ds(pl.multiple_of((n - 1) * L, L), L)
            dk_ref[prev, :] += dkc[:L] * scale
            dv_ref[prev, :] += dvc[:L]

    blk = pl.BlockSpec((L, ATT_OUT), lambda r, n: (n, r))
    res = pl.BlockSpec((Tj, ATT_OUT), lambda r, n: (0, r))
    qspec = lambda off, prev: pl.BlockSpec(
        (L, ATT_OUT), (lambda r, n: (jnp.maximum(n - 1, 0), r * nblk + off + gi)) if prev
        else (lambda r, n: (n, r * nblk + off + gi)))
    shape = jax.ShapeDtypeStruct((Tj, d * ATT_OUT), F32)
    dq, dk, dv = pl.pallas_call(
        body, name=f"att_bwd_g{gi}", grid=(d, nb),
        in_specs=[qspec(0, False), qspec(3, True), qspec(3, False), qspec(6, True), qspec(6, False),
                  blk, blk, blk, blk],
        out_specs=[blk, res, res],
        out_shape=[shape, shape, shape],
        compiler_params=_cparams(2),
    )(pv, pv, pv, pv, pv, view(o), view(lseb), view(do), view(dlseb))
    return dq.reshape(T, ATT_OUT), dk.reshape(T, ATT_OUT), dv.reshape(T, ATT_OUT)


FFN_TM, FFN_TC = 512, 512


def _conv3(u, prev8, cw, cb):
    return cb + cw[0:1] * u + cw[1:2] * _shift_down(u, prev8, 1) + cw[2:3] * _shift_down(u, prev8, 2)


def conv_glu_fwd(u, conv_w, conv_b):
    T = u.shape[0]
    tm, tc = FFN_TM, FFN_TC
    nj, ni = D_FF // tc, T // tm

    def fn(pids, ug, ugh, uv, uvh, cwg, cbg, cwv, cbv):
        first = pids[1] > 0
        cg = _conv3(ug, jnp.where(first, ugh, 0.0), cwg, cbg)
        cv = _conv3(uv, jnp.where(first, uvh, 0.0), cwv, cbv)
        return _gelu_tanh(cg) * cv

    halo = lambda off: (lambda j, i: (jnp.maximum(i * (tm // 8) - 1, 0), j + off))
    ins = [(u, (tm, tc), lambda j, i: (i, j)), (u, (8, tc), halo(0)),
           (u, (tm, tc), lambda j, i: (i, j + nj)), (u, (8, tc), halo(nj)),
           (conv_w, (3, tc), lambda j, i: (0, j)), (conv_b, (1, tc), lambda j, i: (0, j)),
           (conv_w, (3, tc), lambda j, i: (0, j + nj)), (conv_b, (1, tc), lambda j, i: (0, j + nj))]
    out = ((T, D_FF), BF16, (tm, tc), lambda j, i: (i, j), None)
    return tile_call("conv_glu_fwd", fn, (nj, ni), ins, [out])[0]


def conv_glu_bwd(u, conv_w, conv_b, df):
    T = u.shape[0]
    tm, tc = FFN_TM, FFN_TC
    nj, ni = D_FF // tc, T // tm

    def fn(pids, ug, ugh, uv, uvh, cwg, cbg, cwv, cbv, df_t, nxt_g, nxt_v):
        i = ni - 1 - pids[1]
        ugh = jnp.where(i > 0, ugh, 0.0)
        uvh = jnp.where(i > 0, uvh, 0.0)
        cg = _conv3(ug, ugh, cwg, cbg)
        cv = _conv3(uv, uvh, cwv, cbv)
        _, vjp = jax.vjp(lambda g_, v_: _gelu_tanh(g_) * v_, cg, cv)
        dcg, dcv = vjp(df_t.astype(F32))
        cs = lambda z: jnp.sum(z, axis=0, keepdims=True)

        @pl.when(pids[1] == 0)
        def _():
            nxt_g[...] = jnp.zeros_like(nxt_g)
            nxt_v[...] = jnp.zeros_like(nxt_v)

        outs = []
        for dc, cw, nxt_ref in ((dcg, cwg, nxt_g), (dcv, cwv, nxt_v)):
            nxt = nxt_ref[...]
            outs.append(cw[0:1] * dc + cw[1:2] * _shift_up(dc, nxt, 1) + cw[2:3] * _shift_up(dc, nxt, 2))
            nxt_ref[...] = dc[:8]
        for dc, uu, hh in ((dcg, ug, ugh), (dcv, uv, uvh)):
            outs += [cs(dc * uu), cs(dc * _shift_down(uu, hh, 1)), cs(dc * _shift_down(uu, hh, 2)), cs(dc)]
        return outs

    rows = lambda off: (lambda j, r: (ni - 1 - r, j + off))
    halo = lambda off: (lambda j, r: (jnp.maximum((ni - 1 - r) * (tm // 8) - 1, 0), j + off))
    ins = [(u, (tm, tc), rows(0)), (u, (8, tc), halo(0)),
           (u, (tm, tc), rows(nj)), (u, (8, tc), halo(nj)),
           (conv_w, (3, tc), lambda j, r: (0, j)), (conv_b, (1, tc), lambda j, r: (0, j)),
           (conv_w, (3, tc), lambda j, r: (0, j + nj)), (conv_b, (1, tc), lambda j, r: (0, j + nj)),
           (df, (tm, tc), rows(0))]
    big = ((T, D_FF), BF16, (tm, tc), rows(0), None)
    acc = ((1, D_FF), F32, (1, tc), lambda j, r: (0, j), 1)
    res = tile_call("conv_glu_bwd", fn, (nj, ni), ins, [big, big] + [acc] * 8,
                    scratch=[((8, tc), F32), ((8, tc), F32)])
    dconv_w = jnp.concatenate([jnp.concatenate([res[2 + j], res[6 + j]], axis=1) for j in range(3)], axis=0)
    dconv_b = jnp.concatenate([res[5], res[9]], axis=1)
    return res[0], res[1], dconv_w, dconv_b


def _pad_cols(w, total):
    return jnp.pad(w, ((0, 0), (0, total - w.shape[1])))


def _pad_rows(w, total):
    return jnp.pad(w, ((0, total - w.shape[0]), (0, 0)))


def _proj_pad(w):
    z = lambda n: jnp.zeros((w.shape[0], n), w.dtype)
    return jnp.concatenate([w[:, :1600], z(64), w[:, 1600:1664], z(64), w[:, 1664:1824], z(96), w[:, 1824:],
                            z(PROJ_TAIL)], axis=1)


def _proj_unpad(g):
    return jnp.concatenate([g[:, :1600], g[:, OFF_XA:OFF_XA + 64], g[:, OFF_XG:OFF_XG + 160],
                            g[:, RW_PAD:RW_PAD + ATT_COLS]], axis=1)


def _rw_unpad(g):
    return jnp.concatenate([g[:, :1600], g[:, OFF_XA:OFF_XA + 64], g[:, OFF_XG:OFF_XG + 160]], axis=1)


def rms_fwd(name, x, g, tm=512):
    T, D = x.shape
    return tile_call(name, lambda pid, x_t, g_t: _rms(x_t, g_t), (T // tm,),
                     [_rows(x, tm), _par(g)], [_row_out(T, D, BF16, tm)])[0]


def rms_bwd(name, x, g, dh, dres, with_bf16=True, tm=512):
    T, D = x.shape
    out_dtypes = (F32, BF16) if with_bf16 else (F32,)

    def fn(pid, x_t, g_t, dh_t, dres_t):
        _, vjp = jax.vjp(_rms, x_t, g_t)
        dx, dg = vjp(dh_t.astype(F32))
        return (dres_t + dx,) * len(out_dtypes) + (dg,)

    return tile_call(name, fn, (T // tm,), [_rows(x, tm), _par(g), _rows(dh, tm), _rows(dres, tm)],
                     [_row_out(T, D, dt, tm) for dt in out_dtypes] + [_acc_out(1, D)])


def local_step(x, p, target, W):
    T, D = x.shape
    G = {}

    w_in_p = W["w_in_p"]
    mu_p = _proj_pad(_pad_cols(W["rw_mu"], 4128))[:, :RW_PAD]
    w_up_p = _pad_rows(W["rw_w_up"], 128)
    a_up_p = _pad_rows(W["rw_a_up"], 128)
    g_up_p = _pad_rows(W["rw_g_up"], 256)
    r_k = W["rw_r_k"].reshape(1, RW_WIDTH)
    rw_params = [mu_p, W["rw_w0"], w_up_p, W["rw_a0"], a_up_p, g_up_p, W["rw_k_k"], W["rw_k_a"]]

    h = rms_fwd("rms_mix", x, W["g_mix"])
    proj = matmul("proj_in_rw", h, w_in_p[:, :RW_PAD])
    pa = matmul("proj_in_att", h, w_in_p[:, RW_PAD:RW_PAD + ATT_COLS])
    gp = matmul("proj_gate", h, W["w_gate"])

    tm = 512
    rw_in = (proj, (tm, RW_PAD), lambda i: (i, 0))
    rw_halo = _prev_halo(proj, tm, RW_PAD)

    def rw_pre_tile(pid, Pc, halo, *params):
        prev8 = jnp.where(pid[0] > 0, halo, 0.0)
        params = [q.astype(F32) for q in params]
        return rw_pre(Pc, _shift_down(Pc, prev8, 1), *params)

    r, decay, k2, v, avec, bvec, g = tile_call(
        "rw_pre", rw_pre_tile, (T // tm,), [rw_in, rw_halo] + [_par(q) for q in rw_params],
        [_row_out(T, RW_WIDTH, F32, tm)] * 7)

    wa, ba, ka = scan_pair_terms(avec, decay, bvec, k2)
    vT = _to_head_time(v).astype(BF16)
    (yT, S_all, saT), late_slots = rwkv_scan_fwd(avec, decay, bvec, k2, r, vT, wa, ba, ka,
                                            exchange=_late_weight_sources(W))
    y = _from_head_time(yT)
    W = dict(W, **_late_weights(late_slots))

    post_params = [W["rw_ln_g"], W["rw_ln_b"], r_k]
    ya = tile_call("rw_post", lambda pid, *t: rw_post(*t), (T // tm,),
                   [_rows(z, tm) for z in (y, r, k2, v, g)] + [_par(q) for q in post_params],
                   [_row_out(T, RW_WIDTH, BF16, tm)])[0]

    att = [att_fwd(pa, gi, T) for gi in range(3)]
    o_l = [att[0][0], att[1][0], att[2][0], att[0][1], att[1][1], att[2][1]]
    yb = tile_call("att_combine", lambda pid, *t: att_combine(*t), (T // tm,),
                   [_rows(z, tm) for z in o_l], [_row_out(T, ATT_OUT, BF16, tm)])[0]

    za = matmul("branch_a", ya, W["w_branch_a"])
    zb = matmul("branch_b", yb, W["w_branch_b"])
    merged = tile_call("merge", lambda pid, *t: merge_fn(*t), (T // tm,),
                       [_rows(gp, tm), _par(W["b_gate"]), _rows(za, tm), _rows(zb, tm)],
                       [_row_out(T, D, BF16, tm)])[0]
    x1 = matmul("mix_out", merged, W["w_out"], res=x)

    h2 = rms_fwd("rms_ffn", x1, W["g_ffn"])
    u = matmul("ffn_up", h2, W["w_up"])
    f = conv_glu_fwd(u, W["conv_w"], W["conv_b"])
    x2 = matmul("ffn_down", f, W["w_down"], res=x1)

    h3 = rms_fwd("rms_ple", x2, W["g_ple"])
    zg = matmul("ple_gate", h3, W["w_ple_gate"])
    pe = matmul("ple_embed", p, W["w_ple"])

    def tail_tile(pid, x2_t, zg_t, pe_t, gf, tgt):
        loss, vjp = jax.vjp(lambda a_, b_, c_, d_: tail_loss(a_, b_, c_, d_, tgt), x2_t, zg_t, pe_t, gf)
        dx2, dzg, dpe, dgf = vjp(jnp.ones((), F32))
        return dx2, dzg, dpe, dgf, jnp.full((1, 128), loss, F32)

    tmt = 256
    dx3, dzg, dpe, dgf, loss_acc = tile_call(
        "tail_loss", tail_tile, (T // tmt,),
        [_rows(x2, tmt), _rows(zg, tmt), _rows(pe, tmt), _par(W["g_final"]), _rows(target, tmt)],
        [_row_out(T, D, F32, tmt), _row_out(T, D, BF16, tmt), _row_out(T, D, BF16, tmt),
         _acc_out(1, D), _acc_out(1, 128)])
    loss = loss_acc[0, 0]
    G["g_final"] = dgf

    wgrad = functools.partial(matmul, mode="tn", out_dtype=GRAD_WIRE)
    G["w_ple"] = wgrad("d_w_ple", p, dpe)
    G["w_ple_gate"] = wgrad("d_w_ple_gate", h3, dzg)
    dh3 = matmul("d_h3", dzg, W["w_ple_gate"], "nt")
    dx2, dx2b, G["g_ple"] = rms_bwd("rms_ple_bwd", x2, W["g_ple"], dh3, dx3)

    G["w_down"] = wgrad("d_w_down", f, dx2b)
    df = matmul("d_f", dx2b, W["w_down"], "nt", out_dtype=BF16)
    du_g, du_v, G["conv_w"], G["conv_b"] = conv_glu_bwd(u, W["conv_w"], W["conv_b"], df)
    du = jnp.concatenate([du_g, du_v], axis=1)
    G["w_up"] = wgrad("d_w_up", h2, du)
    dh2 = matmul("d_h2", du, W["w_up"], "nt")
    dx1, dx1b, G["g_ffn"] = rms_bwd("rms_ffn_bwd", x1, W["g_ffn"], dh2, dx2)

    G["w_out"] = wgrad("d_w_out", merged, dx1b)
    dmerged = matmul("d_merged", dx1b, W["w_out"], "nt", out_dtype=BF16)

    def merge_bwd_tile(pid, gp_t, bg, za_t, zb_t, dm_t):
        _, vjp = jax.vjp(merge_fn, gp_t, bg, za_t, zb_t)
        return vjp(dm_t.astype(F32))

    dgp, G["b_gate"], dza, dzb = tile_call(
        "merge_bwd", merge_bwd_tile, (T // tm,),
        [_rows(gp, tm), _par(W["b_gate"]), _rows(za, tm), _rows(zb, tm), _rows(dmerged, tm)],
        [_row_out(T, 2 * D, BF16, tm), _acc_out(1, 2 * D), _row_out(T, D, BF16, tm), _row_out(T, D, BF16, tm)])
    G["w_branch_a"] = wgrad("d_w_branch_a", ya, dza)
    dya = matmul("d_ya", dza, W["w_branch_a"], "nt")
    G["w_branch_b"] = wgrad("d_w_branch_b", yb, dzb)
    dyb = matmul("d_yb", dzb, W["w_branch_b"], "nt")
    G["w_gate"] = wgrad("d_w_gate", h, dgp)
    dh_gate = matmul("d_h_gate", dgp, W["w_gate"], "nt")

    def comb_bwd_tile(pid, *t):
        _, vjp = jax.vjp(att_combine, *t[:6])
        return vjp(t[6])

    d_ol = tile_call("att_combine_bwd", comb_bwd_tile, (T // tm,),
                     [_rows(z, tm) for z in o_l] + [_rows(dyb, tm)],
                     [_row_out(T, ATT_OUT, F32, tm)] * 6)
    dqkv = [att_bwd(pa, att[gi][0], att[gi][1], d_ol[gi], d_ol[3 + gi], gi, T) for gi in range(3)]
    d_att = [dqkv[gi][j] for j in range(3) for gi in range(3)]

    def post_bwd_tile(pid, *t):
        _, vjp = jax.vjp(rw_post, *t[:8])
        return vjp(t[8])

    dy, dr_p, dk2_p, dv_p, dg, G["rw_ln_g"], G["rw_ln_b"], d_rk = tile_call(
        "rw_post_bwd", post_bwd_tile, (T // tm,),
        [_rows(z, tm) for z in (y, r, k2, v, g)] + [_par(q) for q in post_params] + [_rows(dya, tm)],
        [_row_out(T, RW_WIDTH, F32, tm)] * 5 + [_acc_out(1, RW_WIDTH)] * 3)
    G["rw_r_k"] = d_rk.reshape(W["rw_r_k"].shape)

    (da, dw, db, dk_s, dr_s, dvT), G["_early_parts"] = rwkv_scan_bwd(
        avec, decay, bvec, k2, r, v, dy, S_all, saT, exchange=_early_grad_sources(G))
    dv_s = _from_head_time(dvT)

    tmb = 128
    rw_in_b = (proj, (tmb, RW_PAD), lambda i: (i, 0))

    def pre_bwd_tile(pid, Pc, halo, *t):
        prev8 = jnp.where(pid[0] > 0, halo, 0.0)
        params = [q.astype(F32) for q in t[:8]]
        dr1, dr2, dw_, dk1, dk2_, dv1, dv2, da_, db_, dg_ = t[8:]
        _, vjp = jax.vjp(rw_pre, Pc, _shift_down(Pc, prev8, 1), *params)
        return vjp((dr1 + dr2, dw_, dk1 + dk2_, dv1 + dv2, da_, db_, dg_))

    cts = (dr_s, dr_p, dw, dk_s, dk2_p, dv_s, dv_p, da, db, dg)
    res = tile_call(
        "rw_pre_bwd", pre_bwd_tile, (T // tmb,),
        [rw_in_b, _prev_halo(proj, tmb, RW_PAD)] + [_par(q) for q in rw_params] + [_rows(z, tmb) for z in cts],
        [_row_out(T, RW_PAD, F32, tmb)] * 2 + [_acc_out(*q.shape) for q in rw_params])
    dPc, dPs = res[0], res[1]
    d_mu, G["rw_w0"], d_wup, G["rw_a0"], d_aup, d_gup, G["rw_k_k"], G["rw_k_a"] = res[2:]
    G["rw_mu"] = _rw_unpad(d_mu)
    G["rw_w_up"], G["rw_a_up"], G["rw_g_up"] = d_wup[:64], d_aup[:64], d_gup[:160]

    def dproj_tile(pid, dPc_t, dPs_t, nxt, *att_t):
        nxt = jnp.where(pid[0] < T // tm - 1, nxt, 0.0)
        tail = jnp.zeros((dPc_t.shape[0], PROJ_TAIL), F32)
        return jnp.concatenate([dPc_t + _shift_up(dPs_t, nxt, 1)] + list(att_t) + [tail], axis=1)

    dproj = tile_call("d_proj", dproj_tile, (T // tm,),
                      [_rows(dPc, tm), _rows(dPs, tm), _next_halo(dPs, tm, RW_PAD, T)] + [_rows(z, tm) for z in d_att],
                      [_row_out(T, PROJ_PAD, BF16, tm)])[0]
    G["w_in_p"] = wgrad("d_w_in", h, dproj)
    w_in_srcs = _w_in_grad_sources(G)
    dh = matmul("d_h", dproj, w_in_p, "nt", res=dh_gate, exchange=w_in_srcs)
    if w_in_srcs:
        dh, G["_w_in_parts"] = dh
    dx, G["g_mix"] = rms_bwd("rms_mix_bwd", x, W["g_mix"], dh, dx1, with_bf16=False)
    return loss, dx, G


def _mesh_pos():
    return lax.axis_index("x"), lax.axis_index("y"), lax.axis_index("c")


def _peer(pos, k):
    x, y, c = pos
    px = 1 - x if k & 4 else x
    py = 1 - y if k & 2 else y
    pc = 1 - c if k & 1 else c
    return (px, py, pc), 4 * px + 2 * py + pc


def all_gather_blocks(name, blocks):
    n = len(blocks)

    def body(*refs):
        x_refs, out_refs = refs[:n], refs[n:2 * n]
        send_sems, recv_sems, local_sems = refs[2 * n:]
        x, y, c = _mesh_pos()
        me, sibling = (x, y, c), (x, y, 1 - c)
        chips = [(1 - x, y), (x, 1 - y), (1 - x, 1 - y)]
        ops = range(n)

        def slot(i, px, py, pc):
            return out_refs[i].at[4 * px + 2 * py + pc]

        def copy(k, i, block, to, own=False):
            return pltpu.make_async_remote_copy(
                src_ref=x_refs[i] if own else slot(i, *block), dst_ref=slot(i, *block),
                send_sem=send_sems.at[k, i], recv_sem=recv_sems.at[k, i],
                device_id=to, device_id_type=pl.DeviceIdType.MESH)

        mine = [pltpu.make_async_copy(x_refs[i], slot(i, *me), local_sems.at[i]) for i in ops]
        first = [copy(0, i, me, sibling, own=True) for i in ops]
        first += [copy(1 + j, i, me, (*chip, c), own=True) for j, chip in enumerate(chips) for i in ops]
        for cp in mine + first:
            cp.start()
        passed = []
        for j, chip in enumerate(chips):
            for i in ops:
                copy(1 + j, i, (*chip, c), me).wait_recv()
                passed.append(copy(4 + j, i, (*chip, c), sibling))
                passed[-1].start()
        for i in ops:
            copy(0, i, sibling, me).wait_recv()
        for j, chip in enumerate(chips):
            for i in ops:
                copy(4 + j, i, (*chip, 1 - c), me).wait_recv()
        for cp in first + passed:
            cp.wait_send()
        for cp in mine:
            cp.wait()

    return pl.pallas_call(
        body, name=name,
        in_specs=[pl.BlockSpec(memory_space=pl.ANY)] * n,
        out_specs=[pl.BlockSpec(memory_space=pl.ANY)] * n,
        out_shape=[jax.ShapeDtypeStruct((N_DEV,) + b.shape, b.dtype) for b in blocks],
        scratch_shapes=[pltpu.SemaphoreType.DMA((N_DEV - 1, n)), pltpu.SemaphoreType.DMA((N_DEV - 1, n)),
                        pltpu.SemaphoreType.DMA((n,))],
        compiler_params=pltpu.CompilerParams(has_side_effects=True),
    )(*blocks)


WHOLE = 0


def _exchange_shapes(srcs):
    shapes = [a.shape[1:] if cols is None else a.shape if cols == WHOLE else (a.shape[0], cols) for a, cols in srcs]
    return [jax.ShapeDtypeStruct((N_DEV,) + s, a.dtype) for s, (a, _) in zip(shapes, srcs)]


def _exchange_sems(n):
    return [pltpu.SemaphoreType.DMA((N_DEV - 1, n)), pltpu.SemaphoreType.DMA((N_DEV - 1, n)),
            pltpu.SemaphoreType.DMA((n,))]


def _exchange_ops(col_widths, x_refs, out_refs, send_sems, recv_sems, local_sems):
    n = len(col_widths)
    pos = _mesh_pos()
    me = 4 * pos[0] + 2 * pos[1] + pos[2]

    def piece(i, d):
        cols = col_widths[i]
        if cols is None:
            return x_refs[i].at[d]
        if cols == WHOLE:
            return x_refs[i]
        return x_refs[i].at[:, pl.ds(pl.multiple_of(d * cols, 128), cols)]

    def local(i):
        return pltpu.make_async_copy(piece(i, me), out_refs[i].at[me], local_sems.at[i])

    def remote(k, i, landing):
        peer, idx = _peer(pos, k)
        return pltpu.make_async_remote_copy(
            src_ref=piece(i, idx), dst_ref=out_refs[i].at[idx if landing else me],
            send_sem=send_sems.at[k - 1, i], recv_sem=recv_sems.at[k - 1, i],
            device_id=peer, device_id_type=pl.DeviceIdType.MESH)

    pairs = [(k, i) for k in range(1, N_DEV) for i in range(n)]

    def start():
        for i in range(n):
            local(i).start()
        for k, i in pairs:
            remote(k, i, False).start()

    def wait():
        for k, i in pairs:
            remote(k, i, True).wait_recv()
        for k, i in pairs:
            remote(k, i, False).wait_send()
        for i in range(n):
            local(i).wait()

    return start, wait


def all_to_all_blocks(name, srcs):
    n = len(srcs)

    def body(*refs):
        start, wait = _exchange_ops([c for _, c in srcs], refs[:n], refs[n:2 * n], *refs[2 * n:])
        start()
        wait()

    return pl.pallas_call(
        body, name=name,
        in_specs=[pl.BlockSpec(memory_space=pl.ANY)] * n,
        out_specs=[pl.BlockSpec(memory_space=pl.ANY)] * n,
        out_shape=_exchange_shapes(srcs),
        scratch_shapes=_exchange_sems(n),
        compiler_params=pltpu.CompilerParams(has_side_effects=True),
    )(*[a for a, _ in srcs])


def _adam_row_tile(R, C):
    best = None
    for t in range(16, R + 1, 16):
        if R % t == 0 and t * C <= ADAM_TILE_ELEMS:
            best = t
    return best if best is not None else R


def reduce_adamw(name, parts, w, m, v):
    _, R, C = parts.shape
    tr = _adam_row_tile(R, C)

    def fn(pid, parts_t, w_t, m_t, v_t):
        g = parts_t[0].astype(F32)
        for i in range(1, N_DEV):
            g = g + parts_t[i].astype(F32)
        m_n = ADAM_B1 * m_t + (1.0 - ADAM_B1) * g
        v_n = ADAM_B2 * v_t + (1.0 - ADAM_B2) * (g * g)
        m_hat = m_n / (1.0 - ADAM_B1 ** ADAM_STEP)
        v_hat = v_n / (1.0 - ADAM_B2 ** ADAM_STEP)
        delta = -ADAM_LR * (m_hat / (jnp.sqrt(v_hat) + ADAM_EPS) + ADAM_WD * w_t)
        return g, delta, m_n, v_n

    row = lambda a: (a, (tr, C), lambda i: (i, 0))
    out = ((R, C), F32, (tr, C), lambda i: (i, 0), None)
    return tile_call(name, fn, (R // tr,),
                     [(parts, (N_DEV, tr, C), lambda i: (0, i, 0)), row(w), row(m), row(v)], [out] * 4)


PARAMS = (
    ("g_mix", (1, 1024), None), ("w_in", (1024, 4128), 1), ("rw_mu", (1, 1824), None), ("rw_w0", (1, 512), None),
    ("rw_w_up", (64, 512), 1), ("rw_a0", (1, 512), None), ("rw_a_up", (64, 512), 1), ("rw_g_up", (160, 512), 1),
    ("rw_k_k", (1, 512), None), ("rw_k_a", (1, 512), None), ("rw_r_k", (8, 64), None), ("rw_ln_g", (1, 512), None),
    ("rw_ln_b", (1, 512), None), ("w_branch_a", (512, 1024), 1), ("w_branch_b", (256, 1024), 1),
    ("w_gate", (1024, 2048), 1), ("b_gate", (1, 2048), None), ("w_out", (1024, 1024), 0), ("g_ffn", (1, 1024), None),
    ("w_up", (1024, 6144), 1), ("conv_w", (3, 6144), 1), ("conv_b", (1, 6144), None), ("w_down", (3072, 1024), 0),
    ("g_ple", (1, 1024), None), ("w_ple_gate", (1024, 1024), 0), ("w_ple", (256, 1024), 1), ("g_final", (1, 1024), None),
)
SHARDED = tuple(q for q in PARAMS if q[2] is not None)
REPLICATED = tuple(q for q in PARAMS if q[2] is None)
BIG_NAMES = ("w_in", "w_up", "w_gate", "w_out", "w_down", "w_ple_gate", "w_branch_a", "w_branch_b", "w_ple")
BIG = tuple(q for q in SHARDED if q[0] in BIG_NAMES)
SMALL_SHARDED = tuple(q for q in SHARDED if q[0] not in BIG_NAMES)
PACK_COLS = 1024
F32_GATHERED = ("conv_w",)


def _local_shape(shape, axis):
    s = list(shape)
    s[axis] //= N_DEV
    return tuple(s)


def _numel(shape):
    return int(np.prod(shape))


def _pad_flat(z, mult):
    n = z.shape[-1]
    total = -(-n // mult) * mult
    return jnp.pad(z, [(0, 0)] * (z.ndim - 1) + [(0, total - n)])


def _full_from_slots(slots, shape, axis):
    loc = _local_shape(shape, axis)
    z = slots.reshape((N_DEV,) + loc)
    if axis == 0:
        return z.reshape(shape)
    return z.transpose(1, 0, 2).reshape(shape)


def _slots_from_full(full, shape, axis):
    loc = _local_shape(shape, axis)
    if axis == 0:
        return full.reshape(N_DEV, _numel(loc))
    return full.reshape(shape[0], N_DEV, loc[1]).transpose(1, 0, 2).reshape(N_DEV, _numel(loc))


W_IN_SLOT = 640
W_IN_LOCAL = 4128 // N_DEV


def _block_shape(shape, axis):
    return _local_shape(shape, axis) if axis is not None else shape


def _pad_w_in(block):
    return jnp.pad(block, ((0, 0), (0, W_IN_SLOT - W_IN_LOCAL)))


def _proj_col(s):
    return s + jnp.where(s >= 1600, 64, 0) + jnp.where(s >= 1664, 64, 0) + jnp.where(s >= 1824, 96, 0)


def _perm_tile(d, c0, width):
    j = lax.broadcasted_iota(jnp.int32, (W_IN_SLOT, width), 0)
    c = c0 + lax.broadcasted_iota(jnp.int32, (W_IN_SLOT, width), 1)
    hit = (_proj_col(d * W_IN_LOCAL + j) == c) & (j < W_IN_LOCAL)
    return jnp.where(hit, 1.0, 0.0).astype(BF16)


PERM_TILE = 768


def w_in_unshuffle(slots):
    _, K, _ = slots.shape
    tn = PERM_TILE
    reach = 3

    def first_slot(j):
        return j + jnp.where(j >= 3, 1, 0) + jnp.where(j >= 5, 1, 0)

    def body(a_ref, o_ref, acc_ref):
        j, kk = pl.program_id(0), pl.program_id(1)
        d = first_slot(j) + kk

        @pl.when(kk == 0)
        def _():
            acc_ref[...] = jnp.zeros_like(acc_ref)

        @pl.when(d < N_DEV)
        def _():
            acc_ref[...] += jnp.dot(a_ref[0], _perm_tile(d, j * tn, tn), preferred_element_type=F32)

        @pl.when(kk == reach - 1)
        def _():
            o_ref[...] = acc_ref[...].astype(o_ref.dtype)

    return pl.pallas_call(
        body, name="w_in_unshuffle", grid=(PROJ_PAD // tn, reach),
        in_specs=[pl.BlockSpec((1, K, W_IN_SLOT), lambda j, kk: (jnp.minimum(first_slot(j) + kk, N_DEV - 1), 0, 0))],
        out_specs=pl.BlockSpec((K, tn), lambda j, kk: (0, j)),
        out_shape=jax.ShapeDtypeStruct((K, PROJ_PAD), BF16),
        scratch_shapes=[pltpu.VMEM((K, tn), F32)],
        compiler_params=_cparams(2),
    )(slots)


def w_in_shuffle_grad(dw):
    K = dw.shape[0]
    tk = PERM_TILE

    def first_tile(d):
        return _proj_col(d * W_IN_LOCAL) // tk

    def body(g_ref, o_ref, acc_ref):
        d, kk = pl.program_id(0), pl.program_id(1)
        perm = _perm_tile(d, (first_tile(d) + kk) * tk, tk)
        part = lax.dot_general(g_ref[...].astype(BF16), perm, NT_DIMS, preferred_element_type=F32)

        @pl.when(kk == 0)
        def _():
            acc_ref[...] = part

        @pl.when(kk == 1)
        def _():
            o_ref[0] = (acc_ref[...] + part).astype(o_ref.dtype)

    return pl.pallas_call(
        body, name="w_in_shuffle_grad", grid=(N_DEV, 2),
        in_specs=[pl.BlockSpec((K, tk), lambda d, kk: (0, first_tile(d) + kk))],
        out_specs=pl.BlockSpec((1, K, W_IN_SLOT), lambda d, kk: (d, 0, 0)),
        out_shape=jax.ShapeDtypeStruct((N_DEV, K, W_IN_SLOT), GRAD_WIRE),
        scratch_shapes=[pltpu.VMEM((K, W_IN_SLOT), F32)],
        compiler_params=_cparams(2),
    )(dw)


def _flat_rows(pieces, dtype, row_mult):
    flat = jnp.concatenate([z.astype(dtype) for z in pieces], axis=-1)
    flat = _pad_flat(flat, row_mult * PACK_COLS)
    return flat.reshape(flat.shape[:-1] + (-1, PACK_COLS))


FIRST = tuple(q for q in BIG if q[0] in ("w_in", "w_gate"))
LATE = tuple(q for q in BIG if q not in FIRST)


def _matrix_from_slots(slots, shape, axis):
    return slots.reshape(shape) if axis == 0 else slots.transpose(1, 0, 2).reshape(shape)


def _late_weight_sources(W):
    return [(blk, WHOLE) for blk in W["_late_blocks"]]


def _late_weights(slots):
    return {n: _matrix_from_slots(s, shape, axis) for (n, shape, axis), s in zip(LATE, slots)}


def gather_weights(local):
    blocks = [(_pad_w_in(local[n]) if n == "w_in" else local[n]).astype(BF16) for n, _, _ in FIRST]
    small = [q for q in SMALL_SHARDED if q[0] not in F32_GATHERED]
    exact = [q for q in SMALL_SHARDED if q[0] in F32_GATHERED]
    blocks.append(_flat_rows([local[n].reshape(-1) for n, _, _ in small], BF16, 16))
    blocks.append(_flat_rows([local[n].reshape(-1) for n, _, _ in exact], F32, 8))
    got = all_gather_blocks("weight_all_gather", blocks)
    full = {"_late_blocks": [local[n].astype(BF16) for n, _, _ in LATE]}
    for (n, shape, axis), slots in zip(FIRST, got):
        if n == "w_in":
            full["w_in_p"] = w_in_unshuffle(slots)
        else:
            full[n] = _matrix_from_slots(slots, shape, axis)
    for group, slots in ((small, got[-2]), (exact, got[-1])):
        slots, off = slots.reshape(N_DEV, -1), 0
        for n, shape, axis in group:
            size = _numel(_local_shape(shape, axis))
            full[n] = _full_from_slots(slots[:, off:off + size], shape, axis)
            off += size
    for n, _, _ in REPLICATED:
        full[n] = local[n]
    return full


LOSS_SLOT = ("_loss", (1, 2), None)
PACKED_SMALL = SMALL_SHARDED + REPLICATED + (LOSS_SLOT,)


def _pack_small(vals):
    pieces = [vals[n].reshape(-1) if n in vals else jnp.zeros((_numel(shape),), F32) for n, shape, _ in PACKED_SMALL]
    return _flat_rows(pieces, F32, 16)


def _unpack_small(packed):
    flat, out, off = packed.reshape(-1), {}, 0
    for n, shape, axis in PACKED_SMALL:
        loc = _block_shape(shape, axis)
        out[n] = flat[off:off + _numel(loc)].reshape(loc)
        off += _numel(loc)
    return out


EARLY = tuple(q for q in BIG if q[0] != "w_in")


def _early_grad_sources(G):
    srcs = []
    for n, shape, axis in EARLY:
        if axis == 0:
            srcs.append((G[n].astype(GRAD_WIRE).reshape((N_DEV,) + _local_shape(shape, axis)), None))
        else:
            srcs.append((G[n].astype(GRAD_WIRE), shape[1] // N_DEV))
    return srcs


def _w_in_grad_sources(G):
    return [(w_in_shuffle_grad(G["w_in_p"]), None)]


def _closing_grad_sources(G, loss_local):
    srcs = []
    rows = [_slots_from_full(G[n].reshape(shape), shape, axis) for n, shape, axis in SMALL_SHARDED]
    loss_hi = loss_local.astype(GRAD_WIRE).astype(F32)
    rep = jnp.concatenate([G[n].reshape(-1) for n, _, _ in REPLICATED] + [jnp.stack([loss_hi, loss_local - loss_hi])])
    rows.append(jnp.broadcast_to(rep[None, :], (N_DEV, rep.shape[0])))
    srcs.append((_flat_rows(rows, GRAD_WIRE, 16), None))
    return srcs


def _step(x, p, target, local_w, local_m, local_v):
    full = gather_weights(local_w)
    loss_local, dx, G = local_step(x, p, target, full)
    closing = all_to_all_blocks("grad_all_to_all", _closing_grad_sources(G, loss_local))
    parts = list(G["_w_in_parts"]) + list(G["_early_parts"]) + list(closing)
    outs = [{}, {}, {}, {}]
    for (n, shape, axis), part in zip((BIG[0],) + EARLY, parts):
        prep = _pad_w_in if n == "w_in" else (lambda z: z)
        res = reduce_adamw("adamw_" + n, part, prep(local_w[n]), prep(local_m[n]), prep(local_v[n]))
        for o, z in zip(outs, res):
            o[n] = z[:, :W_IN_LOCAL] if n == "w_in" else z
    res = reduce_adamw("adamw_small", parts[-1], _pack_small(local_w), _pack_small(local_m), _pack_small(local_v))
    for o, z in zip(outs, res):
        o.update(_unpack_small(z))
    loss = jnp.sum(outs[0]["_loss"])
    return loss, dx, outs


def kernel(x, p, g_mix, w_in, rw_mu, rw_w0, rw_w_up, rw_a0, rw_a_up, rw_g_up, rw_k_k, rw_k_a, rw_r_k, rw_ln_g, rw_ln_b, w_branch_a, w_branch_b, w_gate, b_gate, w_out, g_ffn, w_up, conv_w, conv_b, w_down, g_ple, w_ple_gate, w_ple, g_final, loss_target, m_g_mix, m_w_in, m_rw_mu, m_rw_w0, m_rw_w_up, m_rw_a0, m_rw_a_up, m_rw_g_up, m_rw_k_k, m_rw_k_a, m_rw_r_k, m_rw_ln_g, m_rw_ln_b, m_w_branch_a, m_w_branch_b, m_w_gate, m_b_gate, m_w_out, m_g_ffn, m_w_up, m_conv_w, m_conv_b, m_w_down, m_g_ple, m_w_ple_gate, m_w_ple, m_g_final, v_g_mix, v_w_in, v_rw_mu, v_rw_w0, v_rw_w_up, v_rw_a0, v_rw_a_up, v_rw_g_up, v_rw_k_k, v_rw_k_a, v_rw_r_k, v_rw_ln_g, v_rw_ln_b, v_w_branch_a, v_w_branch_b, v_w_gate, v_b_gate, v_w_out, v_g_ffn, v_w_up, v_conv_w, v_conv_b, v_w_down, v_g_ple, v_w_ple_gate, v_w_ple, v_g_final):
    args = dict(locals())
    names = [n for n, _, _ in PARAMS]
    orig_shape = {n: args[n].shape for n in names}

    def strip(prefix):
        out = {}
        for n, shape, axis in PARAMS:
            a = args[prefix + n]
            loc = _local_shape(shape, axis) if axis is not None else shape
            out[n] = a.reshape(loc)
        return out

    local_w, local_m, local_v = strip(""), strip("m_"), strip("v_")
    T, D = x.shape[-2], x.shape[-1]
    loss, dx, (g, delta, m_n, v_n) = _step(x.reshape(T, D), p.reshape(T, p.shape[-1]), loss_target.reshape(T, D),
                                           local_w, local_m, local_v)
    outs = [loss, dx.reshape(x.shape)]
    for group in (g, delta, m_n, v_n):
        outs += [group[n].reshape(orig_shape[n]) for n in names]
    return tuple(outs)
```

```python
import functools
import math

import numpy as np
import jax
import jax.numpy as jnp
from jax import lax
from jax.experimental import pallas as pl
from jax.experimental.pallas import tpu as pltpu

F32 = jnp.float32
BF16 = jnp.bfloat16
GRAD_WIRE = jnp.bfloat16

N_DEV = 8
NORM_EPS = 1e-6
RW_LN_EPS = 64e-5
HEAD = 64
RW_WIDTH = 512
ATT_GROUPS = ((128, 1), (512, 4), (2048, 16))
ATT_HEADS = 12
ATT_OUT = 256
ATT_COLS = 2304
OFF_XW, OFF_XA, OFF_XG, RW_PAD, PROJ_PAD = 1536, 1664, 1792, 2048, 4608
PROJ_TAIL = PROJ_PAD - RW_PAD - ATT_COLS
D_FF = 3072

ADAM_LR, ADAM_B1, ADAM_B2, ADAM_EPS, ADAM_WD, ADAM_STEP = 0.001, 0.9, 0.999, 1e-08, 0.01, 10

VMEM_LIMIT_BYTES = 56 * 1024 * 1024
ADAM_TILE_ELEMS = 256 * 1024
NEG_BIG = -1e30

NT_DIMS = (((1,), (1,)), ((), ()))
TN_DIMS = (((0,), (0,)), ((), ()))
NN_DIMS = (((1,), (0,)), ((), ()))


def _cparams(n_axes):
    return pltpu.CompilerParams(dimension_semantics=("arbitrary",) * n_axes,
                                vmem_limit_bytes=VMEM_LIMIT_BYTES)


def _split2(x):
    hi = x.astype(BF16)
    lo = (x - hi.astype(F32)).astype(BF16)
    return hi, lo


def _seg_mat(n):
    r = lax.shift_right_logical(lax.broadcasted_iota(jnp.int32, (n, n), 0), 6)
    c = lax.shift_right_logical(lax.broadcasted_iota(jnp.int32, (n, n), 1), 6)
    return jnp.where(r == c, 1.0, 0.0).astype(BF16)


def _segb(x, seg):
    return _segb_stack([(x, 2)], seg)[0]


def _segb_stack(items, seg):
    rows = items[0][0].shape[0]
    parts = []
    for x, passes in items:
        parts += list(_split2(x)) if passes == 2 else [x.astype(BF16)]
    res = jnp.dot(jnp.concatenate(parts, axis=0), seg, preferred_element_type=F32)
    out, at = [], 0
    for _, passes in items:
        piece = res[at * rows:(at + 1) * rows]
        if passes == 2:
            piece = piece + res[(at + 1) * rows:(at + 2) * rows]
        out.append(piece)
        at += passes
    return out


def _segb1(x, seg):
    return jnp.dot(x.astype(BF16), seg, preferred_element_type=F32)


@jax.custom_vjp
def segsum(x):
    return _segb(x, _seg_mat(x.shape[1]))


def _segsum_fwd(x):
    return segsum(x), None


def _segsum_bwd(_, ct):
    return (segsum(ct),)


segsum.defvjp(_segsum_fwd, _segsum_bwd)


@jax.custom_vjp
def bdot(a, b):
    return jnp.dot(a.astype(BF16), b.astype(BF16), preferred_element_type=F32)


def _bdot_fwd(a, b):
    return bdot(a, b), (a, b)


def _bdot_bwd(res, ct):
    a, b = res
    ctb = ct.astype(BF16)
    da = lax.dot_general(ctb, b.astype(BF16), NT_DIMS, preferred_element_type=F32)
    db = lax.dot_general(a.astype(BF16), ctb, TN_DIMS, preferred_element_type=F32)
    return da.astype(a.dtype), db.astype(b.dtype)


bdot.defvjp(_bdot_fwd, _bdot_bwd)


def _sig(x):
    return 1.0 / (1.0 + jnp.exp(-x))


def _softplus(z):
    return jnp.maximum(z, 0.0) + jnp.log(1.0 + jnp.exp(-jnp.abs(z)))


def _gelu_tanh(x):
    return 0.5 * x * (1.0 + jnp.tanh(0.7978845608028654 * (x + 0.044715 * (x * x * x))))


def _rms(x, g):
    return x * lax.rsqrt(jnp.mean(x * x, axis=-1, keepdims=True) + NORM_EPS) * g


def _shift_down(x, prev8, n):
    rolled = pltpu.roll(x, n, 0)
    top = pltpu.roll(prev8, n, 0)
    rid = lax.broadcasted_iota(jnp.int32, (8, x.shape[1]), 0)
    head = jnp.where(rid < n, top, rolled[:8])
    return jnp.concatenate([head, rolled[8:]], axis=0)


def _shift_up(x, next8, n):
    rows = x.shape[0]
    rolled = pltpu.roll(x, rows - n, 0)
    bottom = pltpu.roll(next8, 8 - n, 0)
    rid = lax.broadcasted_iota(jnp.int32, (8, x.shape[1]), 0)
    tail = jnp.where(rid >= 8 - n, bottom, rolled[rows - 8:])
    return jnp.concatenate([rolled[:rows - 8], tail], axis=0)


def tile_call(name, fn, grid, ins, outs, scratch=()):
    n_in, n_out = len(ins), len(outs)
    acc_axes = [o[4] for o in outs]

    def body(*refs):
        pids = tuple(pl.program_id(a) for a in range(len(grid)))
        vals = fn(pids, *[r[...] for r in refs[:n_in]], *refs[n_in + n_out:])
        if not isinstance(vals, (tuple, list)):
            vals = (vals,)
        for o_ref, val, ax in zip(refs[n_in:n_in + n_out], vals, acc_axes):
            if ax is None:
                o_ref[...] = val.astype(o_ref.dtype)
            else:
                @pl.when(pids[ax] == 0)
                def _(o_ref=o_ref):
                    o_ref[...] = jnp.zeros_like(o_ref)

                o_ref[...] += val.astype(o_ref.dtype)

    res = pl.pallas_call(
        body, name=name, grid=grid,
        in_specs=[pl.BlockSpec(b, im) for _, b, im in ins],
        out_specs=[pl.BlockSpec(o[2], o[3]) for o in outs],
        out_shape=[jax.ShapeDtypeStruct(o[0], o[1]) for o in outs],
        scratch_shapes=[pltpu.VMEM(s, d) for s, d in scratch],
        compiler_params=_cparams(len(grid)),
    )(*[a for a, _, _ in ins])
    return res


def _rows(a, tm):
    return (a, (tm, a.shape[1]), lambda i: (i, 0))


def _par(a):
    return (a, a.shape, lambda i: (0, 0))


def _row_out(T, C, dtype, tm):
    return ((T, C), dtype, (tm, C), lambda i: (i, 0), None)


def _acc_out(R, C):
    return ((R, C), F32, (R, C), lambda i: (0, 0), 0)


def _prev_halo(a, tm, C):
    return (a, (8, C), lambda i: (jnp.maximum(i * (tm // 8) - 1, 0), 0))


def _next_halo(a, tm, C, T):
    return (a, (8, C), lambda i: (jnp.minimum((i + 1) * (tm // 8), T // 8 - 1), 0))


def _pick(n, target):
    for t in (target, 2048, 1536, 1024, 768, 512, 384, 256, 128):
        if t <= target and n % t == 0:
            return t
    return n


def matmul(name, a, b, mode="nn", res=None, out_dtype=F32, tm=1024, tn=2048, tk=2048, exchange=()):
    if mode == "nn":
        (M, K), (K2, N) = a.shape, b.shape
    elif mode == "tn":
        (K, M), (K2, N) = a.shape, b.shape
    else:
        (M, K), (N, K2) = a.shape, b.shape
    assert K == K2, (name, a.shape, b.shape, mode)
    tm, tn, tk = _pick(M, tm), _pick(N, tn), _pick(K, tk)
    nk = K // tk
    dims = {"nn": NN_DIMS, "tn": TN_DIMS, "nt": NT_DIMS}[mode]
    a_spec = {"nn": pl.BlockSpec((tm, tk), lambda i, j, k: (i, k)),
              "tn": pl.BlockSpec((tk, tm), lambda i, j, k: (k, i)),
              "nt": pl.BlockSpec((tm, tk), lambda i, j, k: (i, k))}[mode]
    b_spec = {"nn": pl.BlockSpec((tk, tn), lambda i, j, k: (k, j)),
              "tn": pl.BlockSpec((tk, tn), lambda i, j, k: (k, j)),
              "nt": pl.BlockSpec((tn, tk), lambda i, j, k: (j, k))}[mode]
    has_res = res is not None
    nx = len(exchange)
    grid = (M // tm, N // tn, nk)

    def body(*refs):
        a_ref, b_ref = refs[:2]
        r_ref = refs[2] if has_res else None
        refs = refs[2 + has_res:]
        x_refs, o_ref, land_refs, acc_ref = refs[:nx], refs[nx], refs[nx + 1:2 * nx + 1], refs[2 * nx + 1]
        k = pl.program_id(2)
        if nx:
            step = (pl.program_id(0) * grid[1] + pl.program_id(1)) * nk + k
            start, wait = _exchange_ops([c for _, c in exchange], x_refs, land_refs, *refs[2 * nx + 2:])

            @pl.when(step == 0)
            def _():
                start()

        @pl.when(k == 0)
        def _():
            acc_ref[...] = jnp.zeros_like(acc_ref)

        acc_ref[...] += lax.dot_general(a_ref[...].astype(BF16), b_ref[...].astype(BF16), dims,
                                        preferred_element_type=F32)

        @pl.when(k == nk - 1)
        def _():
            out = acc_ref[...]
            if has_res:
                out = out + r_ref[...].astype(F32)
            o_ref[...] = out.astype(o_ref.dtype)

        if nx:
            @pl.when(step == grid[0] * grid[1] * nk - 1)
            def _():
                wait()

    in_specs = [a_spec, b_spec]
    args = [a, b]
    if has_res:
        in_specs.append(pl.BlockSpec((tm, tn), lambda i, j, k: (i, j)))
        args.append(res)
    hbm = pl.BlockSpec(memory_space=pl.ANY)
    out = pl.pallas_call(
        body, name=name, grid=grid,
        in_specs=in_specs + [hbm] * nx,
        out_specs=[pl.BlockSpec((tm, tn), lambda i, j, k: (i, j))] + [hbm] * nx,
        out_shape=[jax.ShapeDtypeStruct((M, N), out_dtype)] + _exchange_shapes(exchange),
        scratch_shapes=[pltpu.VMEM((tm, tn), F32)] + (_exchange_sems(nx) if nx else []),
        compiler_params=pltpu.CompilerParams(dimension_semantics=("arbitrary",) * 3, vmem_limit_bytes=VMEM_LIMIT_BYTES,
                                             has_side_effects=bool(nx)),
    )(*args, *[z for z, _ in exchange])
    return (out[0], out[1:]) if nx else out[0]


def rw_pre(Pc, Ps, mu, w0, w_up, a0, a_up, g_up, k_k, k_a):
    Pm = Pc + (Ps - Pc) * mu
    r, k, v = Pm[:, 0:512], Pm[:, 512:1024], Pm[:, 1024:1536]
    xw, xa, xg = Pm[:, OFF_XW:OFF_XA], Pm[:, OFF_XA:OFF_XG], Pm[:, OFF_XG:RW_PAD]
    w = -_softplus(-(w0 + bdot(jnp.tanh(xw), w_up))) - 0.5
    decay = jnp.exp(-jnp.exp(w))
    a = _sig(a0 + bdot(xa, a_up))
    g = bdot(_sig(xg), g_up)
    kk = k * k_k
    kk = kk / jnp.maximum(jnp.sqrt(segsum(kk * kk)), 1e-12)
    k2 = k * (1.0 + (a - 1.0) * k_a)
    return r, decay, k2, v, -kk, kk * a, g


def rw_post(y, r, k2, v, g, ln_g, ln_b, r_k):
    mean = segsum(y) * (1.0 / HEAD)
    d = y - mean
    var = segsum(d * d) * (1.0 / HEAD)
    yn = d * lax.rsqrt(var + RW_LN_EPS) * ln_g + ln_b
    bonus = segsum(r * k2 * r_k) * v
    return (yn + bonus) * g


def att_combine(o1, o2, o3, l1, l2, l3):
    m = jnp.maximum(jnp.maximum(l1, l2), l3)
    e1, e2, e3 = jnp.exp(l1 - m), jnp.exp(l2 - m), jnp.exp(l3 - m)
    return (e1 * o1 + e2 * o2 + e3 * o3) / (e1 + e2 + e3)


def merge_fn(gp, bg, za, zb):
    s = _sig(gp + bg)
    half = za.shape[1]
    return s[:, :half] * za + s[:, half:] * zb


def tail_loss(x2, zg, pe, g_final, target):
    x3 = x2 + _sig(zg) * pe
    y = _rms(x3, g_final)
    err = (y - target) * (y - target)
    return 0.5 * jnp.sum(jnp.mean(err, axis=-1, keepdims=True))


SCAN_CHUNK = HEAD
SCAN_LANES = 256
SCAN_UNROLL_FWD, SCAN_UNROLL_BWD = 8, 8


def _to_head_time(z):
    T = z.shape[0]
    return z.reshape(T // HEAD, HEAD, RW_WIDTH // HEAD, HEAD).transpose(0, 3, 2, 1).reshape(T // HEAD, HEAD, RW_WIDTH)


def _from_head_time(zt):
    C = zt.shape[0]
    return zt.reshape(C, HEAD, RW_WIDTH // HEAD, HEAD).transpose(0, 3, 2, 1).reshape(C * HEAD, RW_WIDTH)


def _unrolled_loop(n, step, init, unroll):
    def body(i, carry):
        for j in range(unroll):
            carry = step(i * unroll + j, carry)
        return carry

    return lax.fori_loop(0, n // unroll, body, init)


def _lane_groups():
    return [slice(j * SCAN_LANES, (j + 1) * SCAN_LANES) for j in range(RW_WIDTH // SCAN_LANES)]


def scan_pair_terms(a, w, b, k, tm=512):
    T = a.shape[0]

    def fn(pid, a_t, nxt, w_t, b_t, k_t):
        a_next = _shift_up(a_t, jnp.where(pid[0] < T // tm - 1, nxt, 0.0), 1)
        return w_t * a_next, segsum(b_t * a_next), segsum(k_t * a_next)

    return tile_call("scan_pair_terms", fn, (T // tm,),
                     [_rows(a, tm), _next_halo(a, tm, RW_WIDTH, T), _rows(w, tm), _rows(b, tm), _rows(k, tm)],
                     [_row_out(T, RW_WIDTH, F32, tm)] * 3)


def rwkv_scan_fwd(a, w, b, k, r, vT, wa, ba, ka, exchange=()):
    T = a.shape[0]
    C, LW = SCAN_CHUNK, SCAN_LANES
    nC = T // C
    nx = len(exchange)

    def body(*refs):
        a_ref, w_ref, b_ref, k_ref, r_ref, vT_ref, wa_ref, ba_ref, ka_ref = refs[:9]
        x_refs, refs = refs[9:9 + nx], refs[9 + nx:]
        yT_ref, S_ref, saT_ref = refs[:3]
        land_refs, refs = refs[3:3 + nx], refs[3 + nx:]
        st_ref, vb0_ref, vb1_ref, seg_ref = refs[:4]
        if nx:
            start, wait = _exchange_ops([c for _, c in exchange], x_refs, land_refs, *refs[4:])

        @pl.when(pl.program_id(0) == 0)
        def _():
            st_ref[...] = jnp.zeros_like(st_ref)
            seg_ref[...] = _seg_mat(LW)
            if nx:
                start()

        seg = seg_ref[...]
        lane = jnp.bitwise_and(lax.broadcasted_iota(jnp.int32, (1, LW), 1), HEAD - 1)
        groups = _lane_groups()

        def vsel(t, gsl):
            return jnp.where(lane == t, vT_ref[0, :, gsl], 0.0)

        first = _segb_stack([(vsel(s, gsl), 1) for gsl in groups for s in (0, 1)], seg)
        for g, gsl in enumerate(groups):
            vb0_ref[:, gsl] = first[2 * g]
            vb1_ref[:, gsl] = first[2 * g + 1]
        saT_ref[...] = jnp.zeros_like(saT_ref)

        def pair(i, yacc):
            t = 2 * i
            t1 = t + 1
            tp = jnp.maximum(t - 1, 0)
            row = lambda ref, s, gsl: ref[pl.ds(s, 1), gsl]
            Sps = [st_ref[:, gsl] for gsl in groups]
            chain = _segb_stack([(Sp * row(ref, t, gsl), 2) for gsl, Sp in zip(groups, Sps) for ref in (a_ref, wa_ref)],
                                seg)
            sas, us = chain[0::2], chain[1::2]
            S1s = []
            for gsl, Sp, sa, u in zip(groups, Sps, sas, us):
                vb0, vb1 = vb0_ref[:, gsl], vb1_ref[:, gsl]
                S1 = Sp * row(w_ref, t, gsl) + sa * row(b_ref, t, gsl) + vb0 * row(k_ref, t, gsl)
                sa1 = u + sa * row(ba_ref, t, gsl) + vb0 * row(ka_ref, t, gsl)
                st_ref[:, gsl] = S1 * row(w_ref, t1, gsl) + sa1 * row(b_ref, t1, gsl) + vb1 * row(k_ref, t1, gsl)
                S_ref[0, t, :, gsl] = Sp
                S_ref[0, t1, :, gsl] = S1
                S1s.append(S1)
                saT_ref[0, :, gsl] = jnp.where(lane == t, sa, jnp.where(lane == t1, sa1, saT_ref[0, :, gsl]))
            side = _segb_stack([(x, 1) for gsl, Sp, S1 in zip(groups, Sps, S1s)
                                for x in (Sp * row(r_ref, tp, gsl), S1 * row(r_ref, t, gsl),
                                          vsel(t + 2, gsl), vsel(t + 3, gsl))], seg)
            out = []
            for g, (gsl, ya) in enumerate(zip(groups, yacc)):
                yb0, yb1, vb0_ref[:, gsl], vb1_ref[:, gsl] = side[4 * g:4 * g + 4]
                out.append(jnp.where(lane == t, yb1, jnp.where(lane == t - 1, yb0, ya)))
            return tuple(out)

        yacc = _unrolled_loop(C // 2, pair, tuple(jnp.zeros((HEAD, LW), F32) for _ in groups), SCAN_UNROLL_FWD)
        for gsl, ya in zip(groups, yacc):
            S_last = st_ref[:, gsl]
            S_ref[0, C, :, gsl] = S_last
            yb = _segb1(S_last * r_ref[pl.ds(C - 1, 1), gsl], seg)
            yT_ref[0, :, gsl] = jnp.where(lane == C - 1, yb, ya)

        if nx:
            @pl.when(pl.program_id(0) == nC - 1)
            def _():
                wait()

    row = pl.BlockSpec((C, RW_WIDTH), lambda c: (c, 0))
    ht = pl.BlockSpec((1, HEAD, RW_WIDTH), lambda c: (c, 0, 0))
    hbm = pl.BlockSpec(memory_space=pl.ANY)
    res = pl.pallas_call(
        body, name="rwkv_scan_fwd", grid=(nC,),
        in_specs=[row, row, row, row, row, ht, row, row, row] + [hbm] * nx,
        out_specs=[ht, pl.BlockSpec((1, C + 1, HEAD, RW_WIDTH), lambda c: (c, 0, 0, 0)), ht] + [hbm] * nx,
        out_shape=[jax.ShapeDtypeStruct((nC, HEAD, RW_WIDTH), F32),
                   jax.ShapeDtypeStruct((nC, C + 1, HEAD, RW_WIDTH), F32),
                   jax.ShapeDtypeStruct((nC, HEAD, RW_WIDTH), F32)] + _exchange_shapes(exchange),
        scratch_shapes=[pltpu.VMEM((HEAD, RW_WIDTH), F32)] * 3 + [pltpu.VMEM((LW, LW), BF16)]
        + (_exchange_sems(nx) if nx else []),
        compiler_params=pltpu.CompilerParams(dimension_semantics=("arbitrary",), vmem_limit_bytes=VMEM_LIMIT_BYTES,
                                             has_side_effects=bool(nx)),
    )(a, w, b, k, r, vT, wa, ba, ka, *[z for z, _ in exchange])
    return res[:3], res[3:]


def rwkv_scan_bwd(a, w, b, k, r, v, dy, S_all, saT, exchange=()):
    T = a.shape[0]
    C, LW = SCAN_CHUNK, SCAN_LANES
    nC = T // C
    nx = len(exchange)
    n_heads = RW_WIDTH // HEAD
    dyT = _to_head_time(dy).astype(BF16)
    v_rows, dy_rows = v.reshape(T, n_heads, HEAD), dy.reshape(T, n_heads, HEAD)
    sa_rows = _from_head_time(saT).reshape(T, n_heads, HEAD)

    def body(*refs):
        a_ref, w_ref, b_ref, k_ref, r_ref, vR_ref, saR_ref, dyR_ref, dyT_ref, S_ref = refs[:10]
        x_refs, refs = refs[10:10 + nx], refs[10 + nx:]
        da_ref, dw_ref, db_ref, dk_ref, dr_ref, dvT_ref = refs[:6]
        land_refs, refs = refs[6:6 + nx], refs[6 + nx:]
        ds_ref, dyb_ref, seg_ref = refs[:3]
        if nx:
            start, wait = _exchange_ops([c for _, c in exchange], x_refs, land_refs, *refs[3:])

        @pl.when(pl.program_id(0) == 0)
        def _():
            ds_ref[...] = jnp.zeros_like(ds_ref)
            seg_ref[...] = _seg_mat(LW)
            if nx:
                start()

        seg = seg_ref[...]
        lane = jnp.bitwise_and(lax.broadcasted_iota(jnp.int32, (1, LW), 1), HEAD - 1)
        groups = _lane_groups()
        head_row = lax.broadcasted_iota(jnp.int32, (n_heads, LW), 0)
        lane_head = lax.shift_right_logical(lax.broadcasted_iota(jnp.int32, (n_heads, LW), 1), 6)

        def colsum(z):
            return jnp.sum(z, axis=0, keepdims=True)

        def dysel(t, gsl):
            return jnp.where(lane == t, dyT_ref[0, :, gsl], 0.0)

        for gsl, dyb in zip(groups, _segb_stack([(dysel(C - 1, gsl), 1) for gsl in groups], seg)):
            dyb_ref[:, gsl] = dyb

        def step(i, dvacc):
            t = C - 1 - i
            dybs = [dyb_ref[:, gsl] for gsl in groups]
            dSs = [ds_ref[:, gsl] + dyb * r_ref[pl.ds(t, 1), gsl] for gsl, dyb in zip(groups, dybs)]
            dsabs = _segb_stack([(dS * b_ref[pl.ds(t, 1), gsl], 2) for gsl, dS in zip(groups, dSs)], seg)
            for gsl, dS, dsab in zip(groups, dSs, dsabs):
                ds_ref[:, gsl] = dS * w_ref[pl.ds(t, 1), gsl] + dsab * a_ref[pl.ds(t, 1), gsl]
            out = []
            dy_rows = dyR_ref[t].astype(BF16)
            v_sa_rows = jnp.concatenate([vR_ref[t], saR_ref[t]], axis=0).astype(BF16)
            side = _segb_stack([(x, 1) for gsl, dS in zip(groups, dSs)
                                for x in (dS * k_ref[pl.ds(t, 1), gsl], dysel(t - 1, gsl))], seg)
            for g, (gsl, dva, dS, dsab) in enumerate(zip(groups, dvacc, dSs, dsabs)):
                dvb, dyb_ref[:, gsl] = side[2 * g:2 * g + 2]
                Sp = S_ref[0, t, :, gsl]
                own = head_row == lane_head + g * (LW // HEAD)

                def rows_in(rows, mat):
                    full = jnp.dot(rows, mat.astype(BF16), preferred_element_type=F32)
                    return [jnp.sum(jnp.where(own, full[s:s + n_heads], 0.0), axis=0, keepdims=True)
                            for s in range(0, rows.shape[0], n_heads)]

                (dr,) = rows_in(dy_rows, S_ref[0, t + 1, :, gsl])
                dk, db = rows_in(v_sa_rows, dS)
                dr_ref[pl.ds(t, 1), gsl] = dr
                dk_ref[pl.ds(t, 1), gsl] = dk
                db_ref[pl.ds(t, 1), gsl] = db
                dw_ref[pl.ds(t, 1), gsl] = colsum(dS * Sp)
                da_ref[pl.ds(t, 1), gsl] = colsum(Sp * dsab)
                out.append(jnp.where(lane == t, dvb, dva))
            return tuple(out)

        dvacc = _unrolled_loop(C, step, tuple(jnp.zeros((HEAD, LW), F32) for _ in groups), SCAN_UNROLL_BWD)
        for gsl, dva in zip(groups, dvacc):
            dvT_ref[0, :, gsl] = dva

        if nx:
            @pl.when(pl.program_id(0) == nC - 1)
            def _():
                wait()

    row = pl.BlockSpec((C, RW_WIDTH), lambda c: (nC - 1 - c, 0))
    ht = pl.BlockSpec((1, HEAD, RW_WIDTH), lambda c: (nC - 1 - c, 0, 0))
    hbm = pl.BlockSpec(memory_space=pl.ANY)
    per_head = pl.BlockSpec((C, n_heads, HEAD), lambda c: (nC - 1 - c, 0, 0))
    rows_shape = jax.ShapeDtypeStruct((T, RW_WIDTH), F32)
    res = pl.pallas_call(
        body, name="rwkv_scan_bwd", grid=(nC,),
        in_specs=[row, row, row, row, row, per_head, per_head, per_head, ht,
                  pl.BlockSpec((1, C + 1, HEAD, RW_WIDTH), lambda c: (nC - 1 - c, 0, 0, 0))] + [hbm] * nx,
        out_specs=[row, row, row, row, row, ht] + [hbm] * nx,
        out_shape=[rows_shape] * 5 + [jax.ShapeDtypeStruct((nC, HEAD, RW_WIDTH), F32)] + _exchange_shapes(exchange),
        scratch_shapes=[pltpu.VMEM((HEAD, RW_WIDTH), F32), pltpu.VMEM((HEAD, RW_WIDTH), F32),
                        pltpu.VMEM((LW, LW), BF16)] + (_exchange_sems(nx) if nx else []),
        compiler_params=pltpu.CompilerParams(dimension_semantics=("arbitrary",), vmem_limit_bytes=VMEM_LIMIT_BYTES,
                                             has_side_effects=bool(nx)),
    )(a, w, b, k, r, v_rows, sa_rows, dy_rows, dyT, S_all, *[z for z, _ in exchange])
    return res[:6], res[6:]


def _alibi_slope(h):
    return float(np.float32(2.0 ** (-8.0 * (h + 1) / ATT_HEADS)))


ATT_GROUP_HEADS = 4


def _stack_heads(x, lane_head, fill=0.0):
    return jnp.concatenate([jnp.where(lane_head == hh, x, fill) for hh in range(ATT_GROUP_HEADS)], axis=0)


def _unstack_heads(x, lane_head, L):
    out = jnp.zeros((L, x.shape[1]), F32)
    for hh in range(ATT_GROUP_HEADS):
        out = jnp.where(lane_head == hh, x[hh * L:(hh + 1) * L], out)
    return out


def _att_logits(qs, kcat, gi, d, L, n):
    qi = lax.broadcasted_iota(jnp.int32, (L, 2 * L), 0)
    kj = lax.broadcasted_iota(jnp.int32, (L, 2 * L), 1)
    steps = qi + L - kj
    valid = (steps >= 0) & (steps <= L) & ((kj >= L) | (n > 0))
    dist = (d * steps).astype(F32)
    bias = jnp.concatenate([jnp.where(valid, -_alibi_slope(gi * ATT_GROUP_HEADS + hh) * dist, NEG_BIG)
                            for hh in range(ATT_GROUP_HEADS)], axis=0)
    s = lax.dot_general(qs.astype(BF16), kcat, NT_DIMS, preferred_element_type=F32) * (HEAD ** -0.5)
    return jnp.where(bias > 0.5 * NEG_BIG, s + bias, NEG_BIG)


def att_fwd(pa, gi, T):
    window, d = ATT_GROUPS[gi]
    L = window // d
    Tj = T // d
    nb = Tj // L
    pv = pa.reshape(Tj, d * ATT_COLS)
    nblk = ATT_COLS // ATT_OUT

    def fn(pids, q, kp, kc, vp, vc):
        lane_head = lax.shift_right_logical(lax.broadcasted_iota(jnp.int32, (1, ATT_OUT), 1), 6)
        kcat = jnp.concatenate([kp, kc], axis=0).astype(BF16)
        vcat = jnp.concatenate([vp, vc], axis=0).astype(BF16)
        s = _att_logits(_stack_heads(q, lane_head), kcat, gi, d, L, pids[1])
        m = jnp.max(s, axis=-1, keepdims=True)
        p = jnp.exp(s - m)
        l = jnp.sum(p, axis=-1, keepdims=True)
        o = jnp.dot(p.astype(BF16), vcat, preferred_element_type=F32) / l
        lse = jnp.broadcast_to(m + jnp.log(l), o.shape)
        return _unstack_heads(o, lane_head, L), _unstack_heads(lse, lane_head, L)

    blk = (L, ATT_OUT)
    ins = [(pv, blk, lambda r, n: (n, r * nblk + gi)),
           (pv, blk, lambda r, n: (jnp.maximum(n - 1, 0), r * nblk + 3 + gi)),
           (pv, blk, lambda r, n: (n, r * nblk + 3 + gi)),
           (pv, blk, lambda r, n: (jnp.maximum(n - 1, 0), r * nblk + 6 + gi)),
           (pv, blk, lambda r, n: (n, r * nblk + 6 + gi))]
    out = ((Tj, d * ATT_OUT), F32, blk, lambda r, n: (n, r), None)
    o, lseb = tile_call(f"att_fwd_g{gi}", fn, (d, nb), ins, [out, out])
    return o.reshape(T, ATT_OUT), lseb.reshape(T, ATT_OUT)


def att_bwd(pa, o, lseb, do, dlseb, gi, T):
    window, d = ATT_GROUPS[gi]
    L = window // d
    Tj = T // d
    nb = Tj // L
    pv = pa.reshape(Tj, d * ATT_COLS)
    nblk = ATT_COLS // ATT_OUT
    view = lambda z: z.reshape(Tj, d * ATT_OUT)

    def body(q_ref, kp_ref, kc_ref, vp_ref, vc_ref, o_ref, l_ref, do_ref, dl_ref, dq_ref, dk_ref, dv_ref):
        n = pl.program_id(1)

        @pl.when(n == 0)
        def _():
            dk_ref[...] = jnp.zeros_like(dk_ref)
            dv_ref[...] = jnp.zeros_like(dv_ref)

        lane_head = lax.shift_right_logical(lax.broadcasted_iota(jnp.int32, (1, ATT_OUT), 1), 6)
        kcat = jnp.concatenate([kp_ref[...], kc_ref[...]], axis=0).astype(BF16)
        vcat = jnp.concatenate([vp_ref[...], vc_ref[...]], axis=0).astype(BF16)
        qs = _stack_heads(q_ref[...], lane_head)
        dos = _stack_heads(do_ref[...], lane_head)
        lse = jnp.max(_stack_heads(l_ref[...], lane_head, NEG_BIG), axis=-1, keepdims=True)
        dlse = jnp.sum(_stack_heads(dl_ref[...], lane_head), axis=-1, keepdims=True)
        delta = jnp.sum(dos * jnp.concatenate([o_ref[...]] * ATT_GROUP_HEADS, axis=0), axis=-1, keepdims=True)
        p = jnp.exp(_att_logits(qs, kcat, gi, d, L, n) - lse)
        dp = lax.dot_general(dos.astype(BF16), vcat, NT_DIMS, preferred_element_type=F32)
        ds = (p * (dp - delta + dlse)).astype(BF16)
        dq = _unstack_heads(jnp.dot(ds, kcat, preferred_element_type=F32), lane_head, L)
        dkc = lax.dot_general(ds, qs.astype(BF16), TN_DIMS, preferred_element_type=F32)
        dvc = lax.dot_general(p.astype(BF16), dos.astype(BF16), TN_DIMS, preferred_element_type=F32)
        scale = HEAD ** -0.5
        dq_ref[...] = dq * scale
        cur = pl.ds(pl.multiple_of(n * L, L), L)
        dk_ref[cur, :] += dkc[L:] * scale
        dv_ref[cur, :] += dvc[L:]

        @pl.when(n > 0)
        def _():
            prev = pl.ds(pl.multiple_of((n - 1) * L, L), L)
            dk_ref[prev, :] += dkc[:L] * scale
            dv_ref[prev, :] += dvc[:L]

    blk = pl.BlockSpec((L, ATT_OUT), lambda r, n: (n, r))
    res = pl.BlockSpec((Tj, ATT_OUT), lambda r, n: (0, r))
    qspec = lambda off, prev: pl.BlockSpec(
        (L, ATT_OUT), (lambda r, n: (jnp.maximum(n - 1, 0), r * nblk + off + gi)) if prev
        else (lambda r, n: (n, r * nblk + off + gi)))
    shape = jax.ShapeDtypeStruct((Tj, d * ATT_OUT), F32)
    dq, dk, dv = pl.pallas_call(
        body, name=f"att_bwd_g{gi}", grid=(d, nb),
        in_specs=[qspec(0, False), qspec(3, True), qspec(3, False), qspec(6, True), qspec(6, False),
                  blk, blk, blk, blk],
        out_specs=[blk, res, res],
        out_shape=[shape, shape, shape],
        compiler_params=_cparams(2),
    )(pv, pv, pv, pv, pv, view(o), view(lseb), view(do), view(dlseb))
    return dq.reshape(T, ATT_OUT), dk.reshape(T, ATT_OUT), dv.reshape(T, ATT_OUT)


FFN_TM, FFN_TC = 512, 512


def _conv3(u, prev8, cw, cb):
    return cb + cw[0:1] * u + cw[1:2] * _shift_down(u, prev8, 1) + cw[2:3] * _shift_down(u, prev8, 2)


def conv_glu_fwd(u, conv_w, conv_b):
    T = u.shape[0]
    tm, tc = FFN_TM, FFN_TC
    nj, ni = D_FF // tc, T // tm

    def fn(pids, ug, ugh, uv, uvh, cwg, cbg, cwv, cbv):
        first = pids[1] > 0
        cg = _conv3(ug, jnp.where(first, ugh, 0.0), cwg, cbg)
        cv = _conv3(uv, jnp.where(first, uvh, 0.0), cwv, cbv)
        return _gelu_tanh(cg) * cv

    halo = lambda off: (lambda j, i: (jnp.maximum(i * (tm // 8) - 1, 0), j + off))
    ins = [(u, (tm, tc), lambda j, i: (i, j)), (u, (8, tc), halo(0)),
           (u, (tm, tc), lambda j, i: (i, j + nj)), (u, (8, tc), halo(nj)),
           (conv_w, (3, tc), lambda j, i: (0, j)), (conv_b, (1, tc), lambda j, i: (0, j)),
           (conv_w, (3, tc), lambda j, i: (0, j + nj)), (conv_b, (1, tc), lambda j, i: (0, j + nj))]
    out = ((T, D_FF), BF16, (tm, tc), lambda j, i: (i, j), None)
    return tile_call("conv_glu_fwd", fn, (nj, ni), ins, [out])[0]


def conv_glu_bwd(u, conv_w, conv_b, df):
    T = u.shape[0]
    tm, tc = FFN_TM, FFN_TC
    nj, ni = D_FF // tc, T // tm

    def fn(pids, ug, ugh, uv, uvh, cwg, cbg, cwv, cbv, df_t, nxt_g, nxt_v):
        i = ni - 1 - pids[1]
        ugh = jnp.where(i > 0, ugh, 0.0)
        uvh = jnp.where(i > 0, uvh, 0.0)
        cg = _conv3(ug, ugh, cwg, cbg)
        cv = _conv3(uv, uvh, cwv, cbv)
        _, vjp = jax.vjp(lambda g_, v_: _gelu_tanh(g_) * v_, cg, cv)
        dcg, dcv = vjp(df_t.astype(F32))
        cs = lambda z: jnp.sum(z, axis=0, keepdims=True)

        @pl.when(pids[1] == 0)
        def _():
            nxt_g[...] = jnp.zeros_like(nxt_g)
            nxt_v[...] = jnp.zeros_like(nxt_v)

        outs = []
        for dc, cw, nxt_ref in ((dcg, cwg, nxt_g), (dcv, cwv, nxt_v)):
            nxt = nxt_ref[...]
            outs.append(cw[0:1] * dc + cw[1:2] * _shift_up(dc, nxt, 1) + cw[2:3] * _shift_up(dc, nxt, 2))
            nxt_ref[...] = dc[:8]
        for dc, uu, hh in ((dcg, ug, ugh), (dcv, uv, uvh)):
            outs += [cs(dc * uu), cs(dc * _shift_down(uu, hh, 1)), cs(dc * _shift_down(uu, hh, 2)), cs(dc)]
        return outs

    rows = lambda off: (lambda j, r: (ni - 1 - r, j + off))
    halo = lambda off: (lambda j, r: (jnp.maximum((ni - 1 - r) * (tm // 8) - 1, 0), j + off))
    ins = [(u, (tm, tc), rows(0)), (u, (8, tc), halo(0)),
           (u, (tm, tc), rows(nj)), (u, (8, tc), halo(nj)),
           (conv_w, (3, tc), lambda j, r: (0, j)), (conv_b, (1, tc), lambda j, r: (0, j)),
           (conv_w, (3, tc), lambda j, r: (0, j + nj)), (conv_b, (1, tc), lambda j, r: (0, j + nj)),
           (df, (tm, tc), rows(0))]
    big = ((T, D_FF), BF16, (tm, tc), rows(0), None)
    acc = ((1, D_FF), F32, (1, tc), lambda j, r: (0, j), 1)
    res = tile_call("conv_glu_bwd", fn, (nj, ni), ins, [big, big] + [acc] * 8,
                    scratch=[((8, tc), F32), ((8, tc), F32)])
    dconv_w = jnp.concatenate([jnp.concatenate([res[2 + j], res[6 + j]], axis=1) for j in range(3)], axis=0)
    dconv_b = jnp.concatenate([res[5], res[9]], axis=1)
    return res[0], res[1], dconv_w, dconv_b


def _pad_cols(w, total):
    return jnp.pad(w, ((0, 0), (0, total - w.shape[1])))


def _pad_rows(w, total):
    return jnp.pad(w, ((0, total - w.shape[0]), (0, 0)))


def _proj_pad(w):
    z = lambda n: jnp.zeros((w.shape[0], n), w.dtype)
    return jnp.concatenate([w[:, :1600], z(64), w[:, 1600:1664], z(64), w[:, 1664:1824], z(96), w[:, 1824:],
                            z(PROJ_TAIL)], axis=1)


def _proj_unpad(g):
    return jnp.concatenate([g[:, :1600], g[:, OFF_XA:OFF_XA + 64], g[:, OFF_XG:OFF_XG + 160],
                            g[:, RW_PAD:RW_PAD + ATT_COLS]], axis=1)


def _rw_unpad(g):
    return jnp.concatenate([g[:, :1600], g[:, OFF_XA:OFF_XA + 64], g[:, OFF_XG:OFF_XG + 160]], axis=1)


def rms_fwd(name, x, g, tm=512):
    T, D = x.shape
    return tile_call(name, lambda pid, x_t, g_t: _rms(x_t, g_t), (T // tm,),
                     [_rows(x, tm), _par(g)], [_row_out(T, D, BF16, tm)])[0]


def rms_bwd(name, x, g, dh, dres, with_bf16=True, tm=512):
    T, D = x.shape
    out_dtypes = (F32, BF16) if with_bf16 else (F32,)

    def fn(pid, x_t, g_t, dh_t, dres_t):
        _, vjp = jax.vjp(_rms, x_t, g_t)
        dx, dg = vjp(dh_t.astype(F32))
        return (dres_t + dx,) * len(out_dtypes) + (dg,)

    return tile_call(name, fn, (T // tm,), [_rows(x, tm), _par(g), _rows(dh, tm), _rows(dres, tm)],
                     [_row_out(T, D, dt, tm) for dt in out_dtypes] + [_acc_out(1, D)])


def local_step(x, p, target, W):
    T, D = x.shape
    G = {}

    w_in_p = W["w_in_p"]
    mu_p = _proj_pad(_pad_cols(W["rw_mu"], 4128))[:, :RW_PAD]
    w_up_p = _pad_rows(W["rw_w_up"], 128)
    a_up_p = _pad_rows(W["rw_a_up"], 128)
    g_up_p = _pad_rows(W["rw_g_up"], 256)
    r_k = W["rw_r_k"].reshape(1, RW_WIDTH)
    rw_params = [mu_p, W["rw_w0"], w_up_p, W["rw_a0"], a_up_p, g_up_p, W["rw_k_k"], W["rw_k_a"]]

    h = rms_fwd("rms_mix", x, W["g_mix"])
    proj = matmul("proj_in_rw", h, w_in_p[:, :RW_PAD])
    pa = matmul("proj_in_att", h, w_in_p[:, RW_PAD:RW_PAD + ATT_COLS])
    gp = matmul("proj_gate", h, W["w_gate"])

    tm = 512
    rw_in = (proj, (tm, RW_PAD), lambda i: (i, 0))
    rw_halo = _prev_halo(proj, tm, RW_PAD)

    def rw_pre_tile(pid, Pc, halo, *params):
        prev8 = jnp.where(pid[0] > 0, halo, 0.0)
        params = [q.astype(F32) for q in params]
        return rw_pre(Pc, _shift_down(Pc, prev8, 1), *params)

    r, decay, k2, v, avec, bvec, g = tile_call(
        "rw_pre", rw_pre_tile, (T // tm,), [rw_in, rw_halo] + [_par(q) for q in rw_params],
        [_row_out(T, RW_WIDTH, F32, tm)] * 7)

    wa, ba, ka = scan_pair_terms(avec, decay, bvec, k2)
    vT = _to_head_time(v).astype(BF16)
    (yT, S_all, saT), late_slots = rwkv_scan_fwd(avec, decay, bvec, k2, r, vT, wa, ba, ka,
                                            exchange=_late_weight_sources(W))
    y = _from_head_time(yT)
    W = dict(W, **_late_weights(late_slots))

    post_params = [W["rw_ln_g"], W["rw_ln_b"], r_k]
    ya = tile_call("rw_post", lambda pid, *t: rw_post(*t), (T // tm,),
                   [_rows(z, tm) for z in (y, r, k2, v, g)] + [_par(q) for q in post_params],
                   [_row_out(T, RW_WIDTH, BF16, tm)])[0]

    att = [att_fwd(pa, gi, T) for gi in range(3)]
    o_l = [att[0][0], att[1][0], att[2][0], att[0][1], att[1][1], att[2][1]]
    yb = tile_call("att_combine", lambda pid, *t: att_combine(*t), (T // tm,),
                   [_rows(z, tm) for z in o_l], [_row_out(T, ATT_OUT, BF16, tm)])[0]

    za = matmul("branch_a", ya, W["w_branch_a"])
    zb = matmul("branch_b", yb, W["w_branch_b"])
    merged = tile_call("merge", lambda pid, *t: merge_fn(*t), (T // tm,),
                       [_rows(gp, tm), _par(W["b_gate"]), _rows(za, tm), _rows(zb, tm)],
                       [_row_out(T, D, BF16, tm)])[0]
    x1 = matmul("mix_out", merged, W["w_out"], res=x)

    h2 = rms_fwd("rms_ffn", x1, W["g_ffn"])
    u = matmul("ffn_up", h2, W["w_up"])
    f = conv_glu_fwd(u, W["conv_w"], W["conv_b"])
    x2 = matmul("ffn_down", f, W["w_down"], res=x1)

    h3 = rms_fwd("rms_ple", x2, W["g_ple"])
    zg = matmul("ple_gate", h3, W["w_ple_gate"])
    pe = matmul("ple_embed", p, W["w_ple"])

    def tail_tile(pid, x2_t, zg_t, pe_t, gf, tgt):
        loss, vjp = jax.vjp(lambda a_, b_, c_, d_: tail_loss(a_, b_, c_, d_, tgt), x2_t, zg_t, pe_t, gf)
        dx2, dzg, dpe, dgf = vjp(jnp.ones((), F32))
        return dx2, dzg, dpe, dgf, jnp.full((1, 128), loss, F32)

    tmt = 512
    dx3, dzg, dpe, dgf, loss_acc = tile_call(
        "tail_loss", tail_tile, (T // tmt,),
        [_rows(x2, tmt), _rows(zg, tmt), _rows(pe, tmt), _par(W["g_final"]), _rows(target, tmt)],
        [_row_out(T, D, F32, tmt), _row_out(T, D, BF16, tmt), _row_out(T, D, BF16, tmt),
         _acc_out(1, D), _acc_out(1, 128)])
    loss = loss_acc[0, 0]
    G["g_final"] = dgf

    wgrad = functools.partial(matmul, mode="tn", out_dtype=GRAD_WIRE)
    G["w_ple"] = wgrad("d_w_ple", p, dpe)
    G["w_ple_gate"] = wgrad("d_w_ple_gate", h3, dzg)
    dh3 = matmul("d_h3", dzg, W["w_ple_gate"], "nt")
    dx2, dx2b, G["g_ple"] = rms_bwd("rms_ple_bwd", x2, W["g_ple"], dh3, dx3)

    G["w_down"] = wgrad("d_w_down", f, dx2b)
    df = matmul("d_f", dx2b, W["w_down"], "nt", out_dtype=BF16)
    du_g, du_v, G["conv_w"], G["conv_b"] = conv_glu_bwd(u, W["conv_w"], W["conv_b"], df)
    du = jnp.concatenate([du_g, du_v], axis=1)
    G["w_up"] = wgrad("d_w_up", h2, du)
    dh2 = matmul("d_h2", du, W["w_up"], "nt")
    dx1, dx1b, G["g_ffn"] = rms_bwd("rms_ffn_bwd", x1, W["g_ffn"], dh2, dx2)

    G["w_out"] = wgrad("d_w_out", merged, dx1b)
    dmerged = matmul("d_merged", dx1b, W["w_out"], "nt", out_dtype=BF16)

    def merge_bwd_tile(pid, gp_t, bg, za_t, zb_t, dm_t):
        _, vjp = jax.vjp(merge_fn, gp_t, bg, za_t, zb_t)
        return vjp(dm_t.astype(F32))

    dgp, G["b_gate"], dza, dzb = tile_call(
        "merge_bwd", merge_bwd_tile, (T // tm,),
        [_rows(gp, tm), _par(W["b_gate"]), _rows(za, tm), _rows(zb, tm), _rows(dmerged, tm)],
        [_row_out(T, 2 * D, BF16, tm), _acc_out(1, 2 * D), _row_out(T, D, BF16, tm), _row_out(T, D, BF16, tm)])
    G["w_branch_a"] = wgrad("d_w_branch_a", ya, dza)
    dya = matmul("d_ya", dza, W["w_branch_a"], "nt")
    G["w_branch_b"] = wgrad("d_w_branch_b", yb, dzb)
    dyb = matmul("d_yb", dzb, W["w_branch_b"], "nt")
    G["w_gate"] = wgrad("d_w_gate", h, dgp)
    dh_gate = matmul("d_h_gate", dgp, W["w_gate"], "nt")

    def comb_bwd_tile(pid, *t):
        _, vjp = jax.vjp(att_combine, *t[:6])
        return vjp(t[6])

    d_ol = tile_call("att_combine_bwd", comb_bwd_tile, (T // tm,),
                     [_rows(z, tm) for z in o_l] + [_rows(dyb, tm)],
                     [_row_out(T, ATT_OUT, F32, tm)] * 6)
    dqkv = [att_bwd(pa, att[gi][0], att[gi][1], d_ol[gi], d_ol[3 + gi], gi, T) for gi in range(3)]
    d_att = [dqkv[gi][j] for j in range(3) for gi in range(3)]

    def post_bwd_tile(pid, *t):
        _, vjp = jax.vjp(rw_post, *t[:8])
        return vjp(t[8])

    dy, dr_p, dk2_p, dv_p, dg, G["rw_ln_g"], G["rw_ln_b"], d_rk = tile_call(
        "rw_post_bwd", post_bwd_tile, (T // tm,),
        [_rows(z, tm) for z in (y, r, k2, v, g)] + [_par(q) for q in post_params] + [_rows(dya, tm)],
        [_row_out(T, RW_WIDTH, F32, tm)] * 5 + [_acc_out(1, RW_WIDTH)] * 3)
    G["rw_r_k"] = d_rk.reshape(W["rw_r_k"].shape)

    (da, dw, db, dk_s, dr_s, dvT), G["_early_parts"] = rwkv_scan_bwd(
        avec, decay, bvec, k2, r, v, dy, S_all, saT, exchange=_early_grad_sources(G))
    dv_s = _from_head_time(dvT)

    tmb = 256
    rw_in_b = (proj, (tmb, RW_PAD), lambda i: (i, 0))

    def pre_bwd_tile(pid, Pc, halo, *t):
        prev8 = jnp.where(pid[0] > 0, halo, 0.0)
        params = [q.astype(F32) for q in t[:8]]
        dr1, dr2, dw_, dk1, dk2_, dv1, dv2, da_, db_, dg_ = t[8:]
        _, vjp = jax.vjp(rw_pre, Pc, _shift_down(Pc, prev8, 1), *params)
        return vjp((dr1 + dr2, dw_, dk1 + dk2_, dv1 + dv2, da_, db_, dg_))

    cts = (dr_s, dr_p, dw, dk_s, dk2_p, dv_s, dv_p, da, db, dg)
    res = tile_call(
        "rw_pre_bwd", pre_bwd_tile, (T // tmb,),
        [rw_in_b, _prev_halo(proj, tmb, RW_PAD)] + [_par(q) for q in rw_params] + [_rows(z, tmb) for z in cts],
        [_row_out(T, RW_PAD, F32, tmb)] * 2 + [_acc_out(*q.shape) for q in rw_params])
    dPc, dPs = res[0], res[1]
    d_mu, G["rw_w0"], d_wup, G["rw_a0"], d_aup, d_gup, G["rw_k_k"], G["rw_k_a"] = res[2:]
    G["rw_mu"] = _rw_unpad(d_mu)
    G["rw_w_up"], G["rw_a_up"], G["rw_g_up"] = d_wup[:64], d_aup[:64], d_gup[:160]

    def dproj_tile(pid, dPc_t, dPs_t, nxt, *att_t):
        nxt = jnp.where(pid[0] < T // tm - 1, nxt, 0.0)
        tail = jnp.zeros((dPc_t.shape[0], PROJ_TAIL), F32)
        return jnp.concatenate([dPc_t + _shift_up(dPs_t, nxt, 1)] + list(att_t) + [tail], axis=1)

    dproj = tile_call("d_proj", dproj_tile, (T // tm,),
                      [_rows(dPc, tm), _rows(dPs, tm), _next_halo(dPs, tm, RW_PAD, T)] + [_rows(z, tm) for z in d_att],
                      [_row_out(T, PROJ_PAD, BF16, tm)])[0]
    G["w_in_p"] = wgrad("d_w_in", h, dproj)
    w_in_srcs = _w_in_grad_sources(G)
    dh = matmul("d_h", dproj, w_in_p, "nt", res=dh_gate, exchange=w_in_srcs)
    if w_in_srcs:
        dh, G["_w_in_parts"] = dh
    dx, G["g_mix"] = rms_bwd("rms_mix_bwd", x, W["g_mix"], dh, dx1, with_bf16=False)
    return loss, dx, G


def _mesh_pos():
    return lax.axis_index("x"), lax.axis_index("y"), lax.axis_index("c")


def _peer(pos, k):
    x, y, c = pos
    px = 1 - x if k & 4 else x
    py = 1 - y if k & 2 else y
    pc = 1 - c if k & 1 else c
    return (px, py, pc), 4 * px + 2 * py + pc


def all_gather_blocks(name, blocks):
    n = len(blocks)

    def body(*refs):
        x_refs, out_refs = refs[:n], refs[n:2 * n]
        send_sems, recv_sems, local_sems = refs[2 * n:]
        x, y, c = _mesh_pos()
        me, sibling = (x, y, c), (x, y, 1 - c)
        chips = [(1 - x, y), (x, 1 - y), (1 - x, 1 - y)]
        ops = range(n)

        def slot(i, px, py, pc):
            return out_refs[i].at[4 * px + 2 * py + pc]

        def copy(k, i, block, to, own=False):
            return pltpu.make_async_remote_copy(
                src_ref=x_refs[i] if own else slot(i, *block), dst_ref=slot(i, *block),
                send_sem=send_sems.at[k, i], recv_sem=recv_sems.at[k, i],
                device_id=to, device_id_type=pl.DeviceIdType.MESH)

        mine = [pltpu.make_async_copy(x_refs[i], slot(i, *me), local_sems.at[i]) for i in ops]
        first = [copy(0, i, me, sibling, own=True) for i in ops]
        first += [copy(1 + j, i, me, (*chip, c), own=True) for j, chip in enumerate(chips) for i in ops]
        for cp in mine + first:
            cp.start()
        passed = []
        for j, chip in enumerate(chips):
            for i in ops:
                copy(1 + j, i, (*chip, c), me).wait_recv()
                passed.append(copy(4 + j, i, (*chip, c), sibling))
                passed[-1].start()
        for i in ops:
            copy(0, i, sibling, me).wait_recv()
        for j, chip in enumerate(chips):
            for i in ops:
                copy(4 + j, i, (*chip, 1 - c), me).wait_recv()
        for cp in first + passed:
            cp.wait_send()
        for cp in mine:
            cp.wait()

    return pl.pallas_call(
        body, name=name,
        in_specs=[pl.BlockSpec(memory_space=pl.ANY)] * n,
        out_specs=[pl.BlockSpec(memory_space=pl.ANY)] * n,
        out_shape=[jax.ShapeDtypeStruct((N_DEV,) + b.shape, b.dtype) for b in blocks],
        scratch_shapes=[pltpu.SemaphoreType.DMA((N_DEV - 1, n)), pltpu.SemaphoreType.DMA((N_DEV - 1, n)),
                        pltpu.SemaphoreType.DMA((n,))],
        compiler_params=pltpu.CompilerParams(has_side_effects=True),
    )(*blocks)


WHOLE = 0


def _exchange_shapes(srcs):
    shapes = [a.shape[1:] if cols is None else a.shape if cols == WHOLE else (a.shape[0], cols) for a, cols in srcs]
    return [jax.ShapeDtypeStruct((N_DEV,) + s, a.dtype) for s, (a, _) in zip(shapes, srcs)]


def _exchange_sems(n):
    return [pltpu.SemaphoreType.DMA((N_DEV - 1, n)), pltpu.SemaphoreType.DMA((N_DEV - 1, n)),
            pltpu.SemaphoreType.DMA((n,))]


def _exchange_ops(col_widths, x_refs, out_refs, send_sems, recv_sems, local_sems):
    n = len(col_widths)
    pos = _mesh_pos()
    me = 4 * pos[0] + 2 * pos[1] + pos[2]

    def piece(i, d):
        cols = col_widths[i]
        if cols is None:
            return x_refs[i].at[d]
        if cols == WHOLE:
            return x_refs[i]
        return x_refs[i].at[:, pl.ds(pl.multiple_of(d * cols, 128), cols)]

    def local(i):
        return pltpu.make_async_copy(piece(i, me), out_refs[i].at[me], local_sems.at[i])

    def remote(k, i, landing):
        peer, idx = _peer(pos, k)
        return pltpu.make_async_remote_copy(
            src_ref=piece(i, idx), dst_ref=out_refs[i].at[idx if landing else me],
            send_sem=send_sems.at[k - 1, i], recv_sem=recv_sems.at[k - 1, i],
            device_id=peer, device_id_type=pl.DeviceIdType.MESH)

    pairs = [(k, i) for k in range(1, N_DEV) for i in range(n)]

    def start():
        for i in range(n):
            local(i).start()
        for k, i in pairs:
            remote(k, i, False).start()

    def wait():
        for k, i in pairs:
            remote(k, i, True).wait_recv()
        for k, i in pairs:
            remote(k, i, False).wait_send()
        for i in range(n):
            local(i).wait()

    return start, wait


def all_to_all_blocks(name, srcs):
    n = len(srcs)

    def body(*refs):
        start, wait = _exchange_ops([c for _, c in srcs], refs[:n], refs[n:2 * n], *refs[2 * n:])
        start()
        wait()

    return pl.pallas_call(
        body, name=name,
        in_specs=[pl.BlockSpec(memory_space=pl.ANY)] * n,
        out_specs=[pl.BlockSpec(memory_space=pl.ANY)] * n,
        out_shape=_exchange_shapes(srcs),
        scratch_shapes=_exchange_sems(n),
        compiler_params=pltpu.CompilerParams(has_side_effects=True),
    )(*[a for a, _ in srcs])


def _adam_row_tile(R, C):
    best = None
    for t in range(16, R + 1, 16):
        if R % t == 0 and t * C <= ADAM_TILE_ELEMS:
            best = t
    return best if best is not None else R


def reduce_adamw(name, parts, w, m, v):
    _, R, C = parts.shape
    tr = _adam_row_tile(R, C)

    def fn(pid, parts_t, w_t, m_t, v_t):
        g = parts_t[0].astype(F32)
        for i in range(1, N_DEV):
            g = g + parts_t[i].astype(F32)
        m_n = ADAM_B1 * m_t + (1.0 - ADAM_B1) * g
        v_n = ADAM_B2 * v_t + (1.0 - ADAM_B2) * (g * g)
        m_hat = m_n / (1.0 - ADAM_B1 ** ADAM_STEP)
        v_hat = v_n / (1.0 - ADAM_B2 ** ADAM_STEP)
        delta = -ADAM_LR * (m_hat / (jnp.sqrt(v_hat) + ADAM_EPS) + ADAM_WD * w_t)
        return g, delta, m_n, v_n

    row = lambda a: (a, (tr, C), lambda i: (i, 0))
    out = ((R, C), F32, (tr, C), lambda i: (i, 0), None)
    return tile_call(name, fn, (R // tr,),
                     [(parts, (N_DEV, tr, C), lambda i: (0, i, 0)), row(w), row(m), row(v)], [out] * 4)


PARAMS = (
    ("g_mix", (1, 1024), None), ("w_in", (1024, 4128), 1), ("rw_mu", (1, 1824), None), ("rw_w0", (1, 512), None),
    ("rw_w_up", (64, 512), 1), ("rw_a0", (1, 512), None), ("rw_a_up", (64, 512), 1), ("rw_g_up", (160, 512), 1),
    ("rw_k_k", (1, 512), None), ("rw_k_a", (1, 512), None), ("rw_r_k", (8, 64), None), ("rw_ln_g", (1, 512), None),
    ("rw_ln_b", (1, 512), None), ("w_branch_a", (512, 1024), 1), ("w_branch_b", (256, 1024), 1),
    ("w_gate", (1024, 2048), 1), ("b_gate", (1, 2048), None), ("w_out", (1024, 1024), 0), ("g_ffn", (1, 1024), None),
    ("w_up", (1024, 6144), 1), ("conv_w", (3, 6144), 1), ("conv_b", (1, 6144), None), ("w_down", (3072, 1024), 0),
    ("g_ple", (1, 1024), None), ("w_ple_gate", (1024, 1024), 0), ("w_ple", (256, 1024), 1), ("g_final", (1, 1024), None),
)
SHARDED = tuple(q for q in PARAMS if q[2] is not None)
REPLICATED = tuple(q for q in PARAMS if q[2] is None)
BIG_NAMES = ("w_in", "w_up", "w_gate", "w_out", "w_down", "w_ple_gate", "w_branch_a", "w_branch_b", "w_ple")
BIG = tuple(q for q in SHARDED if q[0] in BIG_NAMES)
SMALL_SHARDED = tuple(q for q in SHARDED if q[0] not in BIG_NAMES)
PACK_COLS = 1024
F32_GATHERED = ("conv_w",)


def _local_shape(shape, axis):
    s = list(shape)
    s[axis] //= N_DEV
    return tuple(s)


def _numel(shape):
    return int(np.prod(shape))


def _pad_flat(z, mult):
    n = z.shape[-1]
    total = -(-n // mult) * mult
    return jnp.pad(z, [(0, 0)] * (z.ndim - 1) + [(0, total - n)])


def _full_from_slots(slots, shape, axis):
    loc = _local_shape(shape, axis)
    z = slots.reshape((N_DEV,) + loc)
    if axis == 0:
        return z.reshape(shape)
    return z.transpose(1, 0, 2).reshape(shape)


def _slots_from_full(full, shape, axis):
    loc = _local_shape(shape, axis)
    if axis == 0:
        return full.reshape(N_DEV, _numel(loc))
    return full.reshape(shape[0], N_DEV, loc[1]).transpose(1, 0, 2).reshape(N_DEV, _numel(loc))


W_IN_SLOT = 640
W_IN_LOCAL = 4128 // N_DEV


def _block_shape(shape, axis):
    return _local_shape(shape, axis) if axis is not None else shape


def _pad_w_in(block):
    return jnp.pad(block, ((0, 0), (0, W_IN_SLOT - W_IN_LOCAL)))


def _proj_col(s):
    return s + jnp.where(s >= 1600, 64, 0) + jnp.where(s >= 1664, 64, 0) + jnp.where(s >= 1824, 96, 0)


def _perm_tile(d, c0, width):
    j = lax.broadcasted_iota(jnp.int32, (W_IN_SLOT, width), 0)
    c = c0 + lax.broadcasted_iota(jnp.int32, (W_IN_SLOT, width), 1)
    hit = (_proj_col(d * W_IN_LOCAL + j) == c) & (j < W_IN_LOCAL)
    return jnp.where(hit, 1.0, 0.0).astype(BF16)


PERM_TILE = 768


def w_in_unshuffle(slots):
    _, K, _ = slots.shape
    tn = PERM_TILE
    reach = 3

    def first_slot(j):
        return j + jnp.where(j >= 3, 1, 0) + jnp.where(j >= 5, 1, 0)

    def body(a_ref, o_ref, acc_ref):
        j, kk = pl.program_id(0), pl.program_id(1)
        d = first_slot(j) + kk

        @pl.when(kk == 0)
        def _():
            acc_ref[...] = jnp.zeros_like(acc_ref)

        @pl.when(d < N_DEV)
        def _():
            acc_ref[...] += jnp.dot(a_ref[0], _perm_tile(d, j * tn, tn), preferred_element_type=F32)

        @pl.when(kk == reach - 1)
        def _():
            o_ref[...] = acc_ref[...].astype(o_ref.dtype)

    return pl.pallas_call(
        body, name="w_in_unshuffle", grid=(PROJ_PAD // tn, reach),
        in_specs=[pl.BlockSpec((1, K, W_IN_SLOT), lambda j, kk: (jnp.minimum(first_slot(j) + kk, N_DEV - 1), 0, 0))],
        out_specs=pl.BlockSpec((K, tn), lambda j, kk: (0, j)),
        out_shape=jax.ShapeDtypeStruct((K, PROJ_PAD), BF16),
        scratch_shapes=[pltpu.VMEM((K, tn), F32)],
        compiler_params=_cparams(2),
    )(slots)


def w_in_shuffle_grad(dw):
    K = dw.shape[0]
    tk = PERM_TILE

    def first_tile(d):
        return _proj_col(d * W_IN_LOCAL) // tk

    def body(g_ref, o_ref, acc_ref):
        d, kk = pl.program_id(0), pl.program_id(1)
        perm = _perm_tile(d, (first_tile(d) + kk) * tk, tk)
        part = lax.dot_general(g_ref[...].astype(BF16), perm, NT_DIMS, preferred_element_type=F32)

        @pl.when(kk == 0)
        def _():
            acc_ref[...] = part

        @pl.when(kk == 1)
        def _():
            o_ref[0] = (acc_ref[...] + part).astype(o_ref.dtype)

    return pl.pallas_call(
        body, name="w_in_shuffle_grad", grid=(N_DEV, 2),
        in_specs=[pl.BlockSpec((K, tk), lambda d, kk: (0, first_tile(d) + kk))],
        out_specs=pl.BlockSpec((1, K, W_IN_SLOT), lambda d, kk: (d, 0, 0)),
        out_shape=jax.ShapeDtypeStruct((N_DEV, K, W_IN_SLOT), GRAD_WIRE),
        scratch_shapes=[pltpu.VMEM((K, W_IN_SLOT), F32)],
        compiler_params=_cparams(2),
    )(dw)


def _flat_rows(pieces, dtype, row_mult):
    flat = jnp.concatenate([z.astype(dtype) for z in pieces], axis=-1)
    flat = _pad_flat(flat, row_mult * PACK_COLS)
    return flat.reshape(flat.shape[:-1] + (-1, PACK_COLS))


FIRST = tuple(q for q in BIG if q[0] in ("w_in", "w_gate"))
LATE = tuple(q for q in BIG if q not in FIRST)


def _matrix_from_slots(slots, shape, axis):
    return slots.reshape(shape) if axis == 0 else slots.transpose(1, 0, 2).reshape(shape)


def _late_weight_sources(W):
    return [(blk, WHOLE) for blk in W["_late_blocks"]]


def _late_weights(slots):
    return {n: _matrix_from_slots(s, shape, axis) for (n, shape, axis), s in zip(LATE, slots)}


def gather_weights(local):
    blocks = [(_pad_w_in(local[n]) if n == "w_in" else local[n]).astype(BF16) for n, _, _ in FIRST]
    small = [q for q in SMALL_SHARDED if q[0] not in F32_GATHERED]
    exact = [q for q in SMALL_SHARDED if q[0] in F32_GATHERED]
    blocks.append(_flat_rows([local[n].reshape(-1) for n, _, _ in small], BF16, 16))
    blocks.append(_flat_rows([local[n].reshape(-1) for n, _, _ in exact], F32, 8))
    got = all_gather_blocks("weight_all_gather", blocks)
    full = {"_late_blocks": [local[n].astype(BF16) for n, _, _ in LATE]}
    for (n, shape, axis), slots in zip(FIRST, got):
        if n == "w_in":
            full["w_in_p"] = w_in_unshuffle(slots)
        else:
            full[n] = _matrix_from_slots(slots, shape, axis)
    for group, slots in ((small, got[-2]), (exact, got[-1])):
        slots, off = slots.reshape(N_DEV, -1), 0
        for n, shape, axis in group:
            size = _numel(_local_shape(shape, axis))
            full[n] = _full_from_slots(slots[:, off:off + size], shape, axis)
            off += size
    for n, _, _ in REPLICATED:
        full[n] = local[n]
    return full


LOSS_SLOT = ("_loss", (1, 2), None)
PACKED_SMALL = SMALL_SHARDED + REPLICATED + (LOSS_SLOT,)


def _pack_small(vals):
    pieces = [vals[n].reshape(-1) if n in vals else jnp.zeros((_numel(shape),), F32) for n, shape, _ in PACKED_SMALL]
    return _flat_rows(pieces, F32, 16)


def _unpack_small(packed):
    flat, out, off = packed.reshape(-1), {}, 0
    for n, shape, axis in PACKED_SMALL:
        loc = _block_shape(shape, axis)
        out[n] = flat[off:off + _numel(loc)].reshape(loc)
        off += _numel(loc)
    return out


EARLY = tuple(q for q in BIG if q[0] != "w_in")


def _early_grad_sources(G):
    srcs = []
    for n, shape, axis in EARLY:
        if axis == 0:
            srcs.append((G[n].astype(GRAD_WIRE).reshape((N_DEV,) + _local_shape(shape, axis)), None))
        else:
            srcs.append((G[n].astype(GRAD_WIRE), shape[1] // N_DEV))
    return srcs


def _w_in_grad_sources(G):
    return [(w_in_shuffle_grad(G["w_in_p"]), None)]


def _closing_grad_sources(G, loss_local):
    srcs = []
    rows = [_slots_from_full(G[n].reshape(shape), shape, axis) for n, shape, axis in SMALL_SHARDED]
    loss_hi = loss_local.astype(GRAD_WIRE).astype(F32)
    rep = jnp.concatenate([G[n].reshape(-1) for n, _, _ in REPLICATED] + [jnp.stack([loss_hi, loss_local - loss_hi])])
    rows.append(jnp.broadcast_to(rep[None, :], (N_DEV, rep.shape[0])))
    srcs.append((_flat_rows(rows, GRAD_WIRE, 16), None))
    return srcs


def _step(x, p, target, local_w, local_m, local_v):
    full = gather_weights(local_w)
    loss_local, dx, G = local_step(x, p, target, full)
    closing = all_to_all_blocks("grad_all_to_all", _closing_grad_sources(G, loss_local))
    parts = list(G["_w_in_parts"]) + list(G["_early_parts"]) + list(closing)
    outs = [{}, {}, {}, {}]
    for (n, shape, axis), part in zip((BIG[0],) + EARLY, parts):
        prep = _pad_w_in if n == "w_in" else (lambda z: z)
        res = reduce_adamw("adamw_" + n, part, prep(local_w[n]), prep(local_m[n]), prep(local_v[n]))
        for o, z in zip(outs, res):
            o[n] = z[:, :W_IN_LOCAL] if n == "w_in" else z
    res = reduce_adamw("adamw_small", parts[-1], _pack_small(local_w), _pack_small(local_m), _pack_small(local_v))
    for o, z in zip(outs, res):
        o.update(_unpack_small(z))
    loss = jnp.sum(outs[0]["_loss"])
    return loss, dx, outs


def kernel(x, p, g_mix, w_in, rw_mu, rw_w0, rw_w_up, rw_a0, rw_a_up, rw_g_up, rw_k_k, rw_k_a, rw_r_k, rw_ln_g, rw_ln_b, w_branch_a, w_branch_b, w_gate, b_gate, w_out, g_ffn, w_up, conv_w, conv_b, w_down, g_ple, w_ple_gate, w_ple, g_final, loss_target, m_g_mix, m_w_in, m_rw_mu, m_rw_w0, m_rw_w_up, m_rw_a0, m_rw_a_up, m_rw_g_up, m_rw_k_k, m_rw_k_a, m_rw_r_k, m_rw_ln_g, m_rw_ln_b, m_w_branch_a, m_w_branch_b, m_w_gate, m_b_gate, m_w_out, m_g_ffn, m_w_up, m_conv_w, m_conv_b, m_w_down, m_g_ple, m_w_ple_gate, m_w_ple, m_g_final, v_g_mix, v_w_in, v_rw_mu, v_rw_w0, v_rw_w_up, v_rw_a0, v_rw_a_up, v_rw_g_up, v_rw_k_k, v_rw_k_a, v_rw_r_k, v_rw_ln_g, v_rw_ln_b, v_w_branch_a, v_w_branch_b, v_w_gate, v_b_gate, v_w_out, v_g_ffn, v_w_up, v_conv_w, v_conv_b, v_w_down, v_g_ple, v_w_ple_gate, v_w_ple, v_g_final):
    args = dict(locals())
    names = [n for n, _, _ in PARAMS]
    orig_shape = {n: args[n].shape for n in names}

    def strip(prefix):
        out = {}
        for n, shape, axis in PARAMS:
            a = args[prefix + n]
            loc = _local_shape(shape, axis) if axis is not None else shape
            out[n] = a.reshape(loc)
        return out

    local_w, local_m, local_v = strip(""), strip("m_"), strip("v_")
    T, D = x.shape[-2], x.shape[-1]
    loss, dx, (g, delta, m_n, v_n) = _step(x.reshape(T, D), p.reshape(T, p.shape[-1]), loss_target.reshape(T, D),
                                           local_w, local_m, local_v)
    outs = [loss, dx.reshape(x.shape)]
    for group in (g, delta, m_n, v_n):
        outs += [group[n].reshape(orig_shape[n]) for n in names]
    return tuple(outs)
```

```python
import functools
import math

import numpy as np
import jax
import jax.numpy as jnp
from jax import lax
from jax.experimental import pallas as pl
from jax.experimental.pallas import tpu as pltpu

F32 = jnp.float32
BF16 = jnp.bfloat16
GRAD_WIRE = jnp.bfloat16

N_DEV = 8
NORM_EPS = 1e-6
RW_LN_EPS = 64e-5
HEAD = 64
RW_WIDTH = 512
ATT_GROUPS = ((128, 1), (512, 4), (2048, 16))
ATT_HEADS = 12
ATT_OUT = 256
ATT_COLS = 2304
OFF_XW, OFF_XA, OFF_XG, RW_PAD, PROJ_PAD = 1536, 1664, 1792, 2048, 4608
PROJ_TAIL = PROJ_PAD - RW_PAD - ATT_COLS
D_FF = 3072

ADAM_LR, ADAM_B1, ADAM_B2, ADAM_EPS, ADAM_WD, ADAM_STEP = 0.001, 0.9, 0.999, 1e-08, 0.01, 10

VMEM_LIMIT_BYTES = 56 * 1024 * 1024
ADAM_TILE_ELEMS = 256 * 1024
NEG_BIG = -1e30

NT_DIMS = (((1,), (1,)), ((), ()))
TN_DIMS = (((0,), (0,)), ((), ()))
NN_DIMS = (((1,), (0,)), ((), ()))


def _cparams(n_axes):
    return pltpu.CompilerParams(dimension_semantics=("arbitrary",) * n_axes,
                                vmem_limit_bytes=VMEM_LIMIT_BYTES)


def _split2(x):
    hi = x.astype(BF16)
    lo = (x - hi.astype(F32)).astype(BF16)
    return hi, lo


def _seg_mat(n):
    r = lax.shift_right_logical(lax.broadcasted_iota(jnp.int32, (n, n), 0), 6)
    c = lax.shift_right_logical(lax.broadcasted_iota(jnp.int32, (n, n), 1), 6)
    return jnp.where(r == c, 1.0, 0.0).astype(BF16)


def _segb(x, seg):
    return _segb_stack([(x, 2)], seg)[0]


def _segb_stack(items, seg):
    rows = items[0][0].shape[0]
    parts = []
    for x, passes in items:
        parts += list(_split2(x)) if passes == 2 else [x.astype(BF16)]
    res = jnp.dot(jnp.concatenate(parts, axis=0), seg, preferred_element_type=F32)
    out, at = [], 0
    for _, passes in items:
        piece = res[at * rows:(at + 1) * rows]
        if passes == 2:
            piece = piece + res[(at + 1) * rows:(at + 2) * rows]
        out.append(piece)
        at += passes
    return out


def _segb1(x, seg):
    return jnp.dot(x.astype(BF16), seg, preferred_element_type=F32)


@jax.custom_vjp
def segsum(x):
    return _segb(x, _seg_mat(x.shape[1]))


def _segsum_fwd(x):
    return segsum(x), None


def _segsum_bwd(_, ct):
    return (segsum(ct),)


segsum.defvjp(_segsum_fwd, _segsum_bwd)


@jax.custom_vjp
def bdot(a, b):
    return jnp.dot(a.astype(BF16), b.astype(BF16), preferred_element_type=F32)


def _bdot_fwd(a, b):
    return bdot(a, b), (a, b)


def _bdot_bwd(res, ct):
    a, b = res
    ctb = ct.astype(BF16)
    da = lax.dot_general(ctb, b.astype(BF16), NT_DIMS, preferred_element_type=F32)
    db = lax.dot_general(a.astype(BF16), ctb, TN_DIMS, preferred_element_type=F32)
    return da.astype(a.dtype), db.astype(b.dtype)


bdot.defvjp(_bdot_fwd, _bdot_bwd)


def _sig(x):
    return 1.0 / (1.0 + jnp.exp(-x))


def _softplus(z):
    return jnp.maximum(z, 0.0) + jnp.log(1.0 + jnp.exp(-jnp.abs(z)))


def _gelu_tanh(x):
    return 0.5 * x * (1.0 + jnp.tanh(0.7978845608028654 * (x + 0.044715 * (x * x * x))))


def _rms(x, g):
    return x * lax.rsqrt(jnp.mean(x * x, axis=-1, keepdims=True) + NORM_EPS) * g


def _shift_down(x, prev8, n):
    rolled = pltpu.roll(x, n, 0)
    top = pltpu.roll(prev8, n, 0)
    rid = lax.broadcasted_iota(jnp.int32, (8, x.shape[1]), 0)
    head = jnp.where(rid < n, top, rolled[:8])
    return jnp.concatenate([head, rolled[8:]], axis=0)


def _shift_up(x, next8, n):
    rows = x.shape[0]
    rolled = pltpu.roll(x, rows - n, 0)
    bottom = pltpu.roll(next8, 8 - n, 0)
    rid = lax.broadcasted_iota(jnp.int32, (8, x.shape[1]), 0)
    tail = jnp.where(rid >= 8 - n, bottom, rolled[rows - 8:])
    return jnp.concatenate([rolled[:rows - 8], tail], axis=0)


def tile_call(name, fn, grid, ins, outs, scratch=()):
    n_in, n_out = len(ins), len(outs)
    acc_axes = [o[4] for o in outs]

    def body(*refs):
        pids = tuple(pl.program_id(a) for a in range(len(grid)))
        vals = fn(pids, *[r[...] for r in refs[:n_in]], *refs[n_in + n_out:])
        if not isinstance(vals, (tuple, list)):
            vals = (vals,)
        for o_ref, val, ax in zip(refs[n_in:n_in + n_out], vals, acc_axes):
            if ax is None:
                o_ref[...] = val.astype(o_ref.dtype)
            else:
                @pl.when(pids[ax] == 0)
                def _(o_ref=o_ref):
                    o_ref[...] = jnp.zeros_like(o_ref)

                o_ref[...] += val.astype(o_ref.dtype)

    res = pl.pallas_call(
        body, name=name, grid=grid,
        in_specs=[pl.BlockSpec(b, im) for _, b, im in ins],
        out_specs=[pl.BlockSpec(o[2], o[3]) for o in outs],
        out_shape=[jax.ShapeDtypeStruct(o[0], o[1]) for o in outs],
        scratch_shapes=[pltpu.VMEM(s, d) for s, d in scratch],
        compiler_params=_cparams(len(grid)),
    )(*[a for a, _, _ in ins])
    return res


def _rows(a, tm):
    return (a, (tm, a.shape[1]), lambda i: (i, 0))


def _par(a):
    return (a, a.shape, lambda i: (0, 0))


def _row_out(T, C, dtype, tm):
    return ((T, C), dtype, (tm, C), lambda i: (i, 0), None)


def _acc_out(R, C):
    return ((R, C), F32, (R, C), lambda i: (0, 0), 0)


def _prev_halo(a, tm, C):
    return (a, (8, C), lambda i: (jnp.maximum(i * (tm // 8) - 1, 0), 0))


def _next_halo(a, tm, C, T):
    return (a, (8, C), lambda i: (jnp.minimum((i + 1) * (tm // 8), T // 8 - 1), 0))


def _pick(n, target):
    for t in (target, 2048, 1536, 1024, 768, 512, 384, 256, 128):
        if t <= target and n % t == 0:
            return t
    return n


def matmul(name, a, b, mode="nn", res=None, out_dtype=F32, tm=1024, tn=2048, tk=2048, exchange=()):
    if mode == "nn":
        (M, K), (K2, N) = a.shape, b.shape
    elif mode == "tn":
        (K, M), (K2, N) = a.shape, b.shape
    else:
        (M, K), (N, K2) = a.shape, b.shape
    assert K == K2, (name, a.shape, b.shape, mode)
    tm, tn, tk = _pick(M, tm), _pick(N, tn), _pick(K, tk)
    nk = K // tk
    dims = {"nn": NN_DIMS, "tn": TN_DIMS, "nt": NT_DIMS}[mode]
    a_spec = {"nn": pl.BlockSpec((tm, tk), lambda i, j, k: (i, k)),
              "tn": pl.BlockSpec((tk, tm), lambda i, j, k: (k, i)),
              "nt": pl.BlockSpec((tm, tk), lambda i, j, k: (i, k))}[mode]
    b_spec = {"nn": pl.BlockSpec((tk, tn), lambda i, j, k: (k, j)),
              "tn": pl.BlockSpec((tk, tn), lambda i, j, k: (k, j)),
              "nt": pl.BlockSpec((tn, tk), lambda i, j, k: (j, k))}[mode]
    has_res = res is not None
    nx = len(exchange)
    grid = (M // tm, N // tn, nk)

    def body(*refs):
        a_ref, b_ref = refs[:2]
        r_ref = refs[2] if has_res else None
        refs = refs[2 + has_res:]
        x_refs, o_ref, land_refs, acc_ref = refs[:nx], refs[nx], refs[nx + 1:2 * nx + 1], refs[2 * nx + 1]
        k = pl.program_id(2)
        if nx:
            step = (pl.program_id(0) * grid[1] + pl.program_id(1)) * nk + k
            start, wait = _exchange_ops([c for _, c in exchange], x_refs, land_refs, *refs[2 * nx + 2:])

            @pl.when(step == 0)
            def _():
                start()

        @pl.when(k == 0)
        def _():
            acc_ref[...] = jnp.zeros_like(acc_ref)

        acc_ref[...] += lax.dot_general(a_ref[...].astype(BF16), b_ref[...].astype(BF16), dims,
                                        preferred_element_type=F32)

        @pl.when(k == nk - 1)
        def _():
            out = acc_ref[...]
            if has_res:
                out = out + r_ref[...].astype(F32)
            o_ref[...] = out.astype(o_ref.dtype)

        if nx:
            @pl.when(step == grid[0] * grid[1] * nk - 1)
            def _():
                wait()

    in_specs = [a_spec, b_spec]
    args = [a, b]
    if has_res:
        in_specs.append(pl.BlockSpec((tm, tn), lambda i, j, k: (i, j)))
        args.append(res)
    hbm = pl.BlockSpec(memory_space=pl.ANY)
    out = pl.pallas_call(
        body, name=name, grid=grid,
        in_specs=in_specs + [hbm] * nx,
        out_specs=[pl.BlockSpec((tm, tn), lambda i, j, k: (i, j))] + [hbm] * nx,
        out_shape=[jax.ShapeDtypeStruct((M, N), out_dtype)] + _exchange_shapes(exchange),
        scratch_shapes=[pltpu.VMEM((tm, tn), F32)] + (_exchange_sems(nx) if nx else []),
        compiler_params=pltpu.CompilerParams(dimension_semantics=("arbitrary",) * 3, vmem_limit_bytes=VMEM_LIMIT_BYTES,
                                             has_side_effects=bool(nx)),
    )(*args, *[z for z, _ in exchange])
    return (out[0], out[1:]) if nx else out[0]


def rw_pre(Pc, Ps, mu, w0, w_up, a0, a_up, g_up, k_k, k_a):
    Pm = Pc + (Ps - Pc) * mu
    r, k, v = Pm[:, 0:512], Pm[:, 512:1024], Pm[:, 1024:1536]
    xw, xa, xg = Pm[:, OFF_XW:OFF_XA], Pm[:, OFF_XA:OFF_XG], Pm[:, OFF_XG:RW_PAD]
    w = -_softplus(-(w0 + bdot(jnp.tanh(xw), w_up))) - 0.5
    decay = jnp.exp(-jnp.exp(w))
    a = _sig(a0 + bdot(xa, a_up))
    g = bdot(_sig(xg), g_up)
    kk = k * k_k
    kk = kk / jnp.maximum(jnp.sqrt(segsum(kk * kk)), 1e-12)
    k2 = k * (1.0 + (a - 1.0) * k_a)
    return r, decay, k2, v, -kk, kk * a, g


def rw_post(y, r, k2, v, g, ln_g, ln_b, r_k):
    mean = segsum(y) * (1.0 / HEAD)
    d = y - mean
    var = segsum(d * d) * (1.0 / HEAD)
    yn = d * lax.rsqrt(var + RW_LN_EPS) * ln_g + ln_b
    bonus = segsum(r * k2 * r_k) * v
    return (yn + bonus) * g


def att_combine(o1, o2, o3, l1, l2, l3):
    m = jnp.maximum(jnp.maximum(l1, l2), l3)
    e1, e2, e3 = jnp.exp(l1 - m), jnp.exp(l2 - m), jnp.exp(l3 - m)
    return (e1 * o1 + e2 * o2 + e3 * o3) / (e1 + e2 + e3)


def merge_fn(gp, bg, za, zb):
    s = _sig(gp + bg)
    half = za.shape[1]
    return s[:, :half] * za + s[:, half:] * zb


def tail_loss(x2, zg, pe, g_final, target):
    x3 = x2 + _sig(zg) * pe
    y = _rms(x3, g_final)
    err = (y - target) * (y - target)
    return 0.5 * jnp.sum(jnp.mean(err, axis=-1, keepdims=True))


SCAN_CHUNK = HEAD
SCAN_LANES = 256
SCAN_UNROLL_FWD, SCAN_UNROLL_BWD = 8, 8


def _to_head_time(z):
    T = z.shape[0]
    return z.reshape(T // HEAD, HEAD, RW_WIDTH // HEAD, HEAD).transpose(0, 3, 2, 1).reshape(T // HEAD, HEAD, RW_WIDTH)


def _from_head_time(zt):
    C = zt.shape[0]
    return zt.reshape(C, HEAD, RW_WIDTH // HEAD, HEAD).transpose(0, 3, 2, 1).reshape(C * HEAD, RW_WIDTH)


def _unrolled_loop(n, step, init, unroll):
    def body(i, carry):
        for j in range(unroll):
            carry = step(i * unroll + j, carry)
        return carry

    return lax.fori_loop(0, n // unroll, body, init)


def _lane_groups():
    return [slice(j * SCAN_LANES, (j + 1) * SCAN_LANES) for j in range(RW_WIDTH // SCAN_LANES)]


def scan_pair_terms(a, w, b, k, tm=512):
    T = a.shape[0]

    def fn(pid, a_t, nxt, w_t, b_t, k_t):
        a_next = _shift_up(a_t, jnp.where(pid[0] < T // tm - 1, nxt, 0.0), 1)
        return w_t * a_next, segsum(b_t * a_next), segsum(k_t * a_next)

    return tile_call("scan_pair_terms", fn, (T // tm,),
                     [_rows(a, tm), _next_halo(a, tm, RW_WIDTH, T), _rows(w, tm), _rows(b, tm), _rows(k, tm)],
                     [_row_out(T, RW_WIDTH, F32, tm)] * 3)


def rwkv_scan_fwd(a, w, b, k, r, vT, wa, ba, ka, exchange=()):
    T = a.shape[0]
    C, LW = SCAN_CHUNK, SCAN_LANES
    nC = T // C
    nx = len(exchange)

    def body(*refs):
        a_ref, w_ref, b_ref, k_ref, r_ref, vT_ref, wa_ref, ba_ref, ka_ref = refs[:9]
        x_refs, refs = refs[9:9 + nx], refs[9 + nx:]
        yT_ref, S_ref, saT_ref = refs[:3]
        land_refs, refs = refs[3:3 + nx], refs[3 + nx:]
        st_ref, vb0_ref, vb1_ref, seg_ref = refs[:4]
        if nx:
            start, wait = _exchange_ops([c for _, c in exchange], x_refs, land_refs, *refs[4:])

        @pl.when(pl.program_id(0) == 0)
        def _():
            st_ref[...] = jnp.zeros_like(st_ref)
            seg_ref[...] = _seg_mat(LW)
            if nx:
                start()

        seg = seg_ref[...]
        lane = jnp.bitwise_and(lax.broadcasted_iota(jnp.int32, (1, LW), 1), HEAD - 1)
        groups = _lane_groups()

        def vsel(t, gsl):
            return jnp.where(lane == t, vT_ref[0, :, gsl], 0.0)

        first = _segb_stack([(vsel(s, gsl), 1) for gsl in groups for s in (0, 1)], seg)
        for g, gsl in enumerate(groups):
            vb0_ref[:, gsl] = first[2 * g]
            vb1_ref[:, gsl] = first[2 * g + 1]
        saT_ref[...] = jnp.zeros_like(saT_ref)

        def pair(i, yacc):
            t = 2 * i
            t1 = t + 1
            tp = jnp.maximum(t - 1, 0)
            row = lambda ref, s, gsl: ref[pl.ds(s, 1), gsl]
            Sps = [st_ref[:, gsl] for gsl in groups]
            chain = _segb_stack([(Sp * row(ref, t, gsl), 2) for gsl, Sp in zip(groups, Sps) for ref in (a_ref, wa_ref)],
                                seg)
            sas, us = chain[0::2], chain[1::2]
            S1s = []
            for gsl, Sp, sa, u in zip(groups, Sps, sas, us):
                vb0, vb1 = vb0_ref[:, gsl], vb1_ref[:, gsl]
                S1 = Sp * row(w_ref, t, gsl) + sa * row(b_ref, t, gsl) + vb0 * row(k_ref, t, gsl)
                sa1 = u + sa * row(ba_ref, t, gsl) + vb0 * row(ka_ref, t, gsl)
                st_ref[:, gsl] = S1 * row(w_ref, t1, gsl) + sa1 * row(b_ref, t1, gsl) + vb1 * row(k_ref, t1, gsl)
                S_ref[0, t, :, gsl] = Sp
                S_ref[0, t1, :, gsl] = S1
                S1s.append(S1)
                saT_ref[0, :, gsl] = jnp.where(lane == t, sa, jnp.where(lane == t1, sa1, saT_ref[0, :, gsl]))
            side = _segb_stack([(x, 1) for gsl, Sp, S1 in zip(groups, Sps, S1s)
                                for x in (Sp * row(r_ref, tp, gsl), S1 * row(r_ref, t, gsl),
                                          vsel(t + 2, gsl), vsel(t + 3, gsl))], seg)
            out = []
            for g, (gsl, ya) in enumerate(zip(groups, yacc)):
                yb0, yb1, vb0_ref[:, gsl], vb1_ref[:, gsl] = side[4 * g:4 * g + 4]
                out.append(jnp.where(lane == t, yb1, jnp.where(lane == t - 1, yb0, ya)))
            return tuple(out)

        yacc = _unrolled_loop(C // 2, pair, tuple(jnp.zeros((HEAD, LW), F32) for _ in groups), SCAN_UNROLL_FWD)
        for gsl, ya in zip(groups, yacc):
            S_last = st_ref[:, gsl]
            S_ref[0, C, :, gsl] = S_last
            yb = _segb1(S_last * r_ref[pl.ds(C - 1, 1), gsl], seg)
            yT_ref[0, :, gsl] = jnp.where(lane == C - 1, yb, ya)

        if nx:
            @pl.when(pl.program_id(0) == nC - 1)
            def _():
                wait()

    row = pl.BlockSpec((C, RW_WIDTH), lambda c: (c, 0))
    ht = pl.BlockSpec((1, HEAD, RW_WIDTH), lambda c: (c, 0, 0))
    hbm = pl.BlockSpec(memory_space=pl.ANY)
    res = pl.pallas_call(
        body, name="rwkv_scan_fwd", grid=(nC,),
        in_specs=[row, row, row, row, row, ht, row, row, row] + [hbm] * nx,
        out_specs=[ht, pl.BlockSpec((1, C + 1, HEAD, RW_WIDTH), lambda c: (c, 0, 0, 0)), ht] + [hbm] * nx,
        out_shape=[jax.ShapeDtypeStruct((nC, HEAD, RW_WIDTH), F32),
                   jax.ShapeDtypeStruct((nC, C + 1, HEAD, RW_WIDTH), F32),
                   jax.ShapeDtypeStruct((nC, HEAD, RW_WIDTH), F32)] + _exchange_shapes(exchange),
        scratch_shapes=[pltpu.VMEM((HEAD, RW_WIDTH), F32)] * 3 + [pltpu.VMEM((LW, LW), BF16)]
        + (_exchange_sems(nx) if nx else []),
        compiler_params=pltpu.CompilerParams(dimension_semantics=("arbitrary",), vmem_limit_bytes=VMEM_LIMIT_BYTES,
                                             has_side_effects=bool(nx)),
    )(a, w, b, k, r, vT, wa, ba, ka, *[z for z, _ in exchange])
    return res[:3], res[3:]


def rwkv_scan_bwd(a, w, b, k, r, v, dy, S_all, saT, exchange=()):
    T = a.shape[0]
    C, LW = SCAN_CHUNK, SCAN_LANES
    nC = T // C
    nx = len(exchange)
    n_heads = RW_WIDTH // HEAD
    dyT = _to_head_time(dy).astype(BF16)
    v_rows, dy_rows = v.reshape(T, n_heads, HEAD), dy.reshape(T, n_heads, HEAD)
    sa_rows = _from_head_time(saT).reshape(T, n_heads, HEAD)

    def body(*refs):
        a_ref, w_ref, b_ref, k_ref, r_ref, vR_ref, saR_ref, dyR_ref, dyT_ref, S_ref = refs[:10]
        x_refs, refs = refs[10:10 + nx], refs[10 + nx:]
        da_ref, dw_ref, db_ref, dk_ref, dr_ref, dvT_ref = refs[:6]
        land_refs, refs = refs[6:6 + nx], refs[6 + nx:]
        ds_ref, dyb_ref, seg_ref = refs[:3]
        if nx:
            start, wait = _exchange_ops([c for _, c in exchange], x_refs, land_refs, *refs[3:])

        @pl.when(pl.program_id(0) == 0)
        def _():
            ds_ref[...] = jnp.zeros_like(ds_ref)
            seg_ref[...] = _seg_mat(LW)
            if nx:
                start()

        seg = seg_ref[...]
        lane = jnp.bitwise_and(lax.broadcasted_iota(jnp.int32, (1, LW), 1), HEAD - 1)
        groups = _lane_groups()
        head_row = lax.broadcasted_iota(jnp.int32, (n_heads, LW), 0)
        lane_head = lax.shift_right_logical(lax.broadcasted_iota(jnp.int32, (n_heads, LW), 1), 6)

        def colsum(z):
            return jnp.sum(z, axis=0, keepdims=True)

        def dysel(t, gsl):
            return jnp.where(lane == t, dyT_ref[0, :, gsl], 0.0)

        for gsl, dyb in zip(groups, _segb_stack([(dysel(C - 1, gsl), 1) for gsl in groups], seg)):
            dyb_ref[:, gsl] = dyb

        def step(i, dvacc):
            t = C - 1 - i
            dybs = [dyb_ref[:, gsl] for gsl in groups]
            dSs = [ds_ref[:, gsl] + dyb * r_ref[pl.ds(t, 1), gsl] for gsl, dyb in zip(groups, dybs)]
            dsabs = _segb_stack([(dS * b_ref[pl.ds(t, 1), gsl], 2) for gsl, dS in zip(groups, dSs)], seg)
            for gsl, dS, dsab in zip(groups, dSs, dsabs):
                ds_ref[:, gsl] = dS * w_ref[pl.ds(t, 1), gsl] + dsab * a_ref[pl.ds(t, 1), gsl]
            out = []
            dy_rows = dyR_ref[t].astype(BF16)
            v_sa_rows = jnp.concatenate([vR_ref[t], saR_ref[t]], axis=0).astype(BF16)
            side = _segb_stack([(x, 1) for gsl, dS in zip(groups, dSs)
                                for x in (dS * k_ref[pl.ds(t, 1), gsl], dysel(t - 1, gsl))], seg)
            for g, (gsl, dva, dS, dsab) in enumerate(zip(groups, dvacc, dSs, dsabs)):
                dvb, dyb_ref[:, gsl] = side[2 * g:2 * g + 2]
                Sp = S_ref[0, t, :, gsl]
                own = head_row == lane_head + g * (LW // HEAD)

                def rows_in(rows, mat):
                    full = jnp.dot(rows, mat.astype(BF16), preferred_element_type=F32)
                    return [jnp.sum(jnp.where(own, full[s:s + n_heads], 0.0), axis=0, keepdims=True)
                            for s in range(0, rows.shape[0], n_heads)]

                (dr,) = rows_in(dy_rows, S_ref[0, t + 1, :, gsl])
                dk, db = rows_in(v_sa_rows, dS)
                dr_ref[pl.ds(t, 1), gsl] = dr
                dk_ref[pl.ds(t, 1), gsl] = dk
                db_ref[pl.ds(t, 1), gsl] = db
                dw_ref[pl.ds(t, 1), gsl] = colsum(dS * Sp)
                da_ref[pl.ds(t, 1), gsl] = colsum(Sp * dsab)
                out.append(jnp.where(lane == t, dvb, dva))
            return tuple(out)

        dvacc = _unrolled_loop(C, step, tuple(jnp.zeros((HEAD, LW), F32) for _ in groups), SCAN_UNROLL_BWD)
        for gsl, dva in zip(groups, dvacc):
            dvT_ref[0, :, gsl] = dva

        if nx:
            @pl.when(pl.program_id(0) == nC - 1)
            def _():
                wait()

    row = pl.BlockSpec((C, RW_WIDTH), lambda c: (nC - 1 - c, 0))
    ht = pl.BlockSpec((1, HEAD, RW_WIDTH), lambda c: (nC - 1 - c, 0, 0))
    hbm = pl.BlockSpec(memory_space=pl.ANY)
    per_head = pl.BlockSpec((C, n_heads, HEAD), lambda c: (nC - 1 - c, 0, 0))
    rows_shape = jax.ShapeDtypeStruct((T, RW_WIDTH), F32)
    res = pl.pallas_call(
        body, name="rwkv_scan_bwd", grid=(nC,),
        in_specs=[row, row, row, row, row, per_head, per_head, per_head, ht,
                  pl.BlockSpec((1, C + 1, HEAD, RW_WIDTH), lambda c: (nC - 1 - c, 0, 0, 0))] + [hbm] * nx,
        out_specs=[row, row, row, row, row, ht] + [hbm] * nx,
        out_shape=[rows_shape] * 5 + [jax.ShapeDtypeStruct((nC, HEAD, RW_WIDTH), F32)] + _exchange_shapes(exchange),
        scratch_shapes=[pltpu.VMEM((HEAD, RW_WIDTH), F32), pltpu.VMEM((HEAD, RW_WIDTH), F32),
                        pltpu.VMEM((LW, LW), BF16)] + (_exchange_sems(nx) if nx else []),
        compiler_params=pltpu.CompilerParams(dimension_semantics=("arbitrary",), vmem_limit_bytes=VMEM_LIMIT_BYTES,
                                             has_side_effects=bool(nx)),
    )(a, w, b, k, r, v_rows, sa_rows, dy_rows, dyT, S_all, *[z for z, _ in exchange])
    return res[:6], res[6:]


def _alibi_slope(h):
    return float(np.float32(2.0 ** (-8.0 * (h + 1) / ATT_HEADS)))


ATT_GROUP_HEADS = 4


def _stack_heads(x, lane_head, fill=0.0):
    return jnp.concatenate([jnp.where(lane_head == hh, x, fill) for hh in range(ATT_GROUP_HEADS)], axis=0)


def _unstack_heads(x, lane_head, L):
    out = jnp.zeros((L, x.shape[1]), F32)
    for hh in range(ATT_GROUP_HEADS):
        out = jnp.where(lane_head == hh, x[hh * L:(hh + 1) * L], out)
    return out


def _att_logits(qs, kcat, gi, d, L, n):
    qi = lax.broadcasted_iota(jnp.int32, (L, 2 * L), 0)
    kj = lax.broadcasted_iota(jnp.int32, (L, 2 * L), 1)
    steps = qi + L - kj
    valid = (steps >= 0) & (steps <= L) & ((kj >= L) | (n > 0))
    dist = (d * steps).astype(F32)
    bias = jnp.concatenate([jnp.where(valid, -_alibi_slope(gi * ATT_GROUP_HEADS + hh) * dist, NEG_BIG)
                            for hh in range(ATT_GROUP_HEADS)], axis=0)
    s = lax.dot_general(qs.astype(BF16), kcat, NT_DIMS, preferred_element_type=F32) * (HEAD ** -0.5)
    return jnp.where(bias > 0.5 * NEG_BIG, s + bias, NEG_BIG)


def att_fwd(pa, gi, T):
    window, d = ATT_GROUPS[gi]
    L = window // d
    Tj = T // d
    nb = Tj // L
    pv = pa.reshape(Tj, d * ATT_COLS)
    nblk = ATT_COLS // ATT_OUT

    def fn(pids, q, kp, kc, vp, vc):
        lane_head = lax.shift_right_logical(lax.broadcasted_iota(jnp.int32, (1, ATT_OUT), 1), 6)
        kcat = jnp.concatenate([kp, kc], axis=0).astype(BF16)
        vcat = jnp.concatenate([vp, vc], axis=0).astype(BF16)
        s = _att_logits(_stack_heads(q, lane_head), kcat, gi, d, L, pids[1])
        m = jnp.max(s, axis=-1, keepdims=True)
        p = jnp.exp(s - m)
        l = jnp.sum(p, axis=-1, keepdims=True)
        o = jnp.dot(p.astype(BF16), vcat, preferred_element_type=F32) / l
        lse = jnp.broadcast_to(m + jnp.log(l), o.shape)
        return _unstack_heads(o, lane_head, L), _unstack_heads(lse, lane_head, L)

    blk = (L, ATT_OUT)
    ins = [(pv, blk, lambda r, n: (n, r * nblk + gi)),
           (pv, blk, lambda r, n: (jnp.maximum(n - 1, 0), r * nblk + 3 + gi)),
           (pv, blk, lambda r, n: (n, r * nblk + 3 + gi)),
           (pv, blk, lambda r, n: (jnp.maximum(n - 1, 0), r * nblk + 6 + gi)),
           (pv, blk, lambda r, n: (n, r * nblk + 6 + gi))]
    out = ((Tj, d * ATT_OUT), F32, blk, lambda r, n: (n, r), None)
    o, lseb = tile_call(f"att_fwd_g{gi}", fn, (d, nb), ins, [out, out])
    return o.reshape(T, ATT_OUT), lseb.reshape(T, ATT_OUT)


def att_bwd(pa, o, lseb, do, dlseb, gi, T):
    window, d = ATT_GROUPS[gi]
    L = window // d
    Tj = T // d
    nb = Tj // L
    pv = pa.reshape(Tj, d * ATT_COLS)
    nblk = ATT_COLS // ATT_OUT
    view = lambda z: z.reshape(Tj, d * ATT_OUT)

    def body(q_ref, kp_ref, kc_ref, vp_ref, vc_ref, o_ref, l_ref, do_ref, dl_ref, dq_ref, dk_ref, dv_ref):
        n = pl.program_id(1)

        @pl.when(n == 0)
        def _():
            dk_ref[...] = jnp.zeros_like(dk_ref)
            dv_ref[...] = jnp.zeros_like(dv_ref)

        lane_head = lax.shift_right_logical(lax.broadcasted_iota(jnp.int32, (1, ATT_OUT), 1), 6)
        kcat = jnp.concatenate([kp_ref[...], kc_ref[...]], axis=0).astype(BF16)
        vcat = jnp.concatenate([vp_ref[...], vc_ref[...]], axis=0).astype(BF16)
        qs = _stack_heads(q_ref[...], lane_head)
        dos = _stack_heads(do_ref[...], lane_head)
        lse = jnp.max(_stack_heads(l_ref[...], lane_head, NEG_BIG), axis=-1, keepdims=True)
        dlse = jnp.sum(_stack_heads(dl_ref[...], lane_head), axis=-1, keepdims=True)
        delta = jnp.sum(dos * jnp.concatenate([o_ref[...]] * ATT_GROUP_HEADS, axis=0), axis=-1, keepdims=True)
        p = jnp.exp(_att_logits(qs, kcat, gi, d, L, n) - lse)
        dp = lax.dot_general(dos.astype(BF16), vcat, NT_DIMS, preferred_element_type=F32)
        ds = (p * (dp - delta + dlse)).astype(BF16)
        dq = _unstack_heads(jnp.dot(ds, kcat, preferred_element_type=F32), lane_head, L)
        dkc = lax.dot_general(ds, qs.astype(BF16), TN_DIMS, preferred_element_type=F32)
        dvc = lax.dot_general(p.astype(BF16), dos.astype(BF16), TN_DIMS, preferred_element_type=F32)
        scale = HEAD ** -0.5
        dq_ref[...] = dq * scale
        cur = pl.ds(pl.multiple_of(n * L, L), L)
        dk_ref[cur, :] += dkc[L:] * scale
        dv_ref[cur, :] += dvc[L:]

        @pl.when(n > 0)
        def _():
            prev = pl.ds(pl.multiple_of((n - 1) * L, L), L)
            dk_ref[prev, :] += dkc[:L] * scale
            dv_ref[prev, :] += dvc[:L]

    blk = pl.BlockSpec((L, ATT_OUT), lambda r, n: (n, r))
    res = pl.BlockSpec((Tj, ATT_OUT), lambda r, n: (0, r))
    qspec = lambda off, prev: pl.BlockSpec(
        (L, ATT_OUT), (lambda r, n: (jnp.maximum(n - 1, 0), r * nblk + off + gi)) if prev
        else (lambda r, n: (n, r * nblk + off + gi)))
    shape = jax.ShapeDtypeStruct((Tj, d * ATT_OUT), F32)
    dq, dk, dv = pl.pallas_call(
        body, name=f"att_bwd_g{gi}", grid=(d, nb),
        in_specs=[qspec(0, False), qspec(3, True), qspec(3, False), qspec(6, True), qspec(6, False),
                  blk, blk, blk, blk],
        out_specs=[blk, res, res],
        out_shape=[shape, shape, shape],
        compiler_params=_cparams(2),
    )(pv, pv, pv, pv, pv, view(o), view(lseb), view(do), view(dlseb))
    return dq.reshape(T, ATT_OUT), dk.reshape(T, ATT_OUT), dv.reshape(T, ATT_OUT)


FFN_TM, FFN_TC = 512, 512


def _conv3(u, prev8, cw, cb):
    return cb + cw[0:1] * u + cw[1:2] * _shift_down(u, prev8, 1) + cw[2:3] * _shift_down(u, prev8, 2)


def conv_glu_fwd(u, conv_w, conv_b):
    T = u.shape[0]
    tm, tc = FFN_TM, FFN_TC
    nj, ni = D_FF // tc, T // tm

    def fn(pids, ug, ugh, uv, uvh, cwg, cbg, cwv, cbv):
        first = pids[1] > 0
        cg = _conv3(ug, jnp.where(first, ugh, 0.0), cwg, cbg)
        cv = _conv3(uv, jnp.where(first, uvh, 0.0), cwv, cbv)
        return _gelu_tanh(cg) * cv

    halo = lambda off: (lambda j, i: (jnp.maximum(i * (tm // 8) - 1, 0), j + off))
    ins = [(u, (tm, tc), lambda j, i: (i, j)), (u, (8, tc), halo(0)),
           (u, (tm, tc), lambda j, i: (i, j + nj)), (u, (8, tc), halo(nj)),
           (conv_w, (3, tc), lambda j, i: (0, j)), (conv_b, (1, tc), lambda j, i: (0, j)),
           (conv_w, (3, tc), lambda j, i: (0, j + nj)), (conv_b, (1, tc), lambda j, i: (0, j + nj))]
    out = ((T, D_FF), BF16, (tm, tc), lambda j, i: (i, j), None)
    return tile_call("conv_glu_fwd", fn, (nj, ni), ins, [out])[0]


def conv_glu_bwd(u, conv_w, conv_b, df):
    T = u.shape[0]
    tm, tc = FFN_TM, FFN_TC
    nj, ni = D_FF // tc, T // tm

    def fn(pids, ug, ugh, uv, uvh, cwg, cbg, cwv, cbv, df_t, nxt_g, nxt_v):
        i = ni - 1 - pids[1]
        ugh = jnp.where(i > 0, ugh, 0.0)
        uvh = jnp.where(i > 0, uvh, 0.0)
        cg = _conv3(ug, ugh, cwg, cbg)
        cv = _conv3(uv, uvh, cwv, cbv)
        _, vjp = jax.vjp(lambda g_, v_: _gelu_tanh(g_) * v_, cg, cv)
        dcg, dcv = vjp(df_t.astype(F32))
        cs = lambda z: jnp.sum(z, axis=0, keepdims=True)

        @pl.when(pids[1] == 0)
        def _():
            nxt_g[...] = jnp.zeros_like(nxt_g)
            nxt_v[...] = jnp.zeros_like(nxt_v)

        outs = []
        for dc, cw, nxt_ref in ((dcg, cwg, nxt_g), (dcv, cwv, nxt_v)):
            nxt = nxt_ref[...]
            outs.append(cw[0:1] * dc + cw[1:2] * _shift_up(dc, nxt, 1) + cw[2:3] * _shift_up(dc, nxt, 2))
            nxt_ref[...] = dc[:8]
        for dc, uu, hh in ((dcg, ug, ugh), (dcv, uv, uvh)):
            outs += [cs(dc * uu), cs(dc * _shift_down(uu, hh, 1)), cs(dc * _shift_down(uu, hh, 2)), cs(dc)]
        return outs

    rows = lambda off: (lambda j, r: (ni - 1 - r, j + off))
    halo = lambda off: (lambda j, r: (jnp.maximum((ni - 1 - r) * (tm // 8) - 1, 0), j + off))
    ins = [(u, (tm, tc), rows(0)), (u, (8, tc), halo(0)),
           (u, (tm, tc), rows(nj)), (u, (8, tc), halo(nj)),
           (conv_w, (3, tc), lambda j, r: (0, j)), (conv_b, (1, tc), lambda j, r: (0, j)),
           (conv_w, (3, tc), lambda j, r: (0, j + nj)), (conv_b, (1, tc), lambda j, r: (0, j + nj)),
           (df, (tm, tc), rows(0))]
    big = ((T, D_FF), BF16, (tm, tc), rows(0), None)
    acc = ((1, D_FF), F32, (1, tc), lambda j, r: (0, j), 1)
    res = tile_call("conv_glu_bwd", fn, (nj, ni), ins, [big, big] + [acc] * 8,
                    scratch=[((8, tc), F32), ((8, tc), F32)])
    dconv_w = jnp.concatenate([jnp.concatenate([res[2 + j], res[6 + j]], axis=1) for j in range(3)], axis=0)
    dconv_b = jnp.concatenate([res[5], res[9]], axis=1)
    return res[0], res[1], dconv_w, dconv_b


def _pad_cols(w, total):
    return jnp.pad(w, ((0, 0), (0, total - w.shape[1])))


def _pad_rows(w, total):
    return jnp.pad(w, ((0, total - w.shape[0]), (0, 0)))


def _proj_pad(w):
    z = lambda n: jnp.zeros((w.shape[0], n), w.dtype)
    return jnp.concatenate([w[:, :1600], z(64), w[:, 1600:1664], z(64), w[:, 1664:1824], z(96), w[:, 1824:],
                            z(PROJ_TAIL)], axis=1)


def _proj_unpad(g):
    return jnp.concatenate([g[:, :1600], g[:, OFF_XA:OFF_XA + 64], g[:, OFF_XG:OFF_XG + 160],
                            g[:, RW_PAD:RW_PAD + ATT_COLS]], axis=1)


def _rw_unpad(g):
    return jnp.concatenate([g[:, :1600], g[:, OFF_XA:OFF_XA + 64], g[:, OFF_XG:OFF_XG + 160]], axis=1)


def rms_fwd(name, x, g, tm=512):
    T, D = x.shape
    return tile_call(name, lambda pid, x_t, g_t: _rms(x_t, g_t), (T // tm,),
                     [_rows(x, tm), _par(g)], [_row_out(T, D, BF16, tm)])[0]


def rms_bwd(name, x, g, dh, dres, with_bf16=True, tm=512):
    T, D = x.shape
    out_dtypes = (F32, BF16) if with_bf16 else (F32,)

    def fn(pid, x_t, g_t, dh_t, dres_t):
        _, vjp = jax.vjp(_rms, x_t, g_t)
        dx, dg = vjp(dh_t.astype(F32))
        return (dres_t + dx,) * len(out_dtypes) + (dg,)

    return tile_call(name, fn, (T // tm,), [_rows(x, tm), _par(g), _rows(dh, tm), _rows(dres, tm)],
                     [_row_out(T, D, dt, tm) for dt in out_dtypes] + [_acc_out(1, D)])


def local_step(x, p, target, W):
    T, D = x.shape
    G = {}

    w_in_p = W["w_in_p"]
    mu_p = _proj_pad(_pad_cols(W["rw_mu"], 4128))[:, :RW_PAD]
    w_up_p = _pad_rows(W["rw_w_up"], 128)
    a_up_p = _pad_rows(W["rw_a_up"], 128)
    g_up_p = _pad_rows(W["rw_g_up"], 256)
    r_k = W["rw_r_k"].reshape(1, RW_WIDTH)
    rw_params = [mu_p, W["rw_w0"], w_up_p, W["rw_a0"], a_up_p, g_up_p, W["rw_k_k"], W["rw_k_a"]]

    h = rms_fwd("rms_mix", x, W["g_mix"])
    proj = matmul("proj_in_rw", h, w_in_p[:, :RW_PAD])
    pa = matmul("proj_in_att", h, w_in_p[:, RW_PAD:RW_PAD + ATT_COLS], out_dtype=BF16)
    gp = matmul("proj_gate", h, W["w_gate"])

    tm = 512
    rw_in = (proj, (tm, RW_PAD), lambda i: (i, 0))
    rw_halo = _prev_halo(proj, tm, RW_PAD)

    def rw_pre_tile(pid, Pc, halo, *params):
        prev8 = jnp.where(pid[0] > 0, halo, 0.0)
        params = [q.astype(F32) for q in params]
        return rw_pre(Pc, _shift_down(Pc, prev8, 1), *params)

    r, decay, k2, v, avec, bvec, g = tile_call(
        "rw_pre", rw_pre_tile, (T // tm,), [rw_in, rw_halo] + [_par(q) for q in rw_params],
        [_row_out(T, RW_WIDTH, F32, tm)] * 7)

    wa, ba, ka = scan_pair_terms(avec, decay, bvec, k2)
    vT = _to_head_time(v).astype(BF16)
    (yT, S_all, saT), late_slots = rwkv_scan_fwd(avec, decay, bvec, k2, r, vT, wa, ba, ka,
                                            exchange=_late_weight_sources(W))
    y = _from_head_time(yT)
    W = dict(W, **_late_weights(late_slots))

    post_params = [W["rw_ln_g"], W["rw_ln_b"], r_k]
    ya = tile_call("rw_post", lambda pid, *t: rw_post(*t), (T // tm,),
                   [_rows(z, tm) for z in (y, r, k2, v, g)] + [_par(q) for q in post_params],
                   [_row_out(T, RW_WIDTH, BF16, tm)])[0]

    att = [att_fwd(pa, gi, T) for gi in range(3)]
    o_l = [att[0][0], att[1][0], att[2][0], att[0][1], att[1][1], att[2][1]]
    yb = tile_call("att_combine", lambda pid, *t: att_combine(*t), (T // tm,),
                   [_rows(z, tm) for z in o_l], [_row_out(T, ATT_OUT, BF16, tm)])[0]

    za = matmul("branch_a", ya, W["w_branch_a"])
    zb = matmul("branch_b", yb, W["w_branch_b"])
    merged = tile_call("merge", lambda pid, *t: merge_fn(*t), (T // tm,),
                       [_rows(gp, tm), _par(W["b_gate"]), _rows(za, tm), _rows(zb, tm)],
                       [_row_out(T, D, BF16, tm)])[0]
    x1 = matmul("mix_out", merged, W["w_out"], res=x)

    h2 = rms_fwd("rms_ffn", x1, W["g_ffn"])
    u = matmul("ffn_up", h2, W["w_up"])
    f = conv_glu_fwd(u, W["conv_w"], W["conv_b"])
    x2 = matmul("ffn_down", f, W["w_down"], res=x1)

    h3 = rms_fwd("rms_ple", x2, W["g_ple"])
    zg = matmul("ple_gate", h3, W["w_ple_gate"])
    pe = matmul("ple_embed", p, W["w_ple"])

    def tail_tile(pid, x2_t, zg_t, pe_t, gf, tgt):
        loss, vjp = jax.vjp(lambda a_, b_, c_, d_: tail_loss(a_, b_, c_, d_, tgt), x2_t, zg_t, pe_t, gf)
        dx2, dzg, dpe, dgf = vjp(jnp.ones((), F32))
        return dx2, dzg, dpe, dgf, jnp.full((1, 128), loss, F32)

    tmt = 512
    dx3, dzg, dpe, dgf, loss_acc = tile_call(
        "tail_loss", tail_tile, (T // tmt,),
        [_rows(x2, tmt), _rows(zg, tmt), _rows(pe, tmt), _par(W["g_final"]), _rows(target, tmt)],
        [_row_out(T, D, F32, tmt), _row_out(T, D, BF16, tmt), _row_out(T, D, BF16, tmt),
         _acc_out(1, D), _acc_out(1, 128)])
    loss = loss_acc[0, 0]
    G["g_final"] = dgf

    wgrad = functools.partial(matmul, mode="tn", out_dtype=GRAD_WIRE)
    G["w_ple"] = wgrad("d_w_ple", p, dpe)
    G["w_ple_gate"] = wgrad("d_w_ple_gate", h3, dzg)
    dh3 = matmul("d_h3", dzg, W["w_ple_gate"], "nt")
    dx2, dx2b, G["g_ple"] = rms_bwd("rms_ple_bwd", x2, W["g_ple"], dh3, dx3)

    G["w_down"] = wgrad("d_w_down", f, dx2b)
    df = matmul("d_f", dx2b, W["w_down"], "nt", out_dtype=BF16)
    du_g, du_v, G["conv_w"], G["conv_b"] = conv_glu_bwd(u, W["conv_w"], W["conv_b"], df)
    du = jnp.concatenate([du_g, du_v], axis=1)
    G["w_up"] = wgrad("d_w_up", h2, du)
    dh2 = matmul("d_h2", du, W["w_up"], "nt")
    dx1, dx1b, G["g_ffn"] = rms_bwd("rms_ffn_bwd", x1, W["g_ffn"], dh2, dx2)

    G["w_out"] = wgrad("d_w_out", merged, dx1b)
    dmerged = matmul("d_merged", dx1b, W["w_out"], "nt", out_dtype=BF16)

    def merge_bwd_tile(pid, gp_t, bg, za_t, zb_t, dm_t):
        _, vjp = jax.vjp(merge_fn, gp_t, bg, za_t, zb_t)
        return vjp(dm_t.astype(F32))

    dgp, G["b_gate"], dza, dzb = tile_call(
        "merge_bwd", merge_bwd_tile, (T // tm,),
        [_rows(gp, tm), _par(W["b_gate"]), _rows(za, tm), _rows(zb, tm), _rows(dmerged, tm)],
        [_row_out(T, 2 * D, BF16, tm), _acc_out(1, 2 * D), _row_out(T, D, BF16, tm), _row_out(T, D, BF16, tm)])
    G["w_branch_a"] = wgrad("d_w_branch_a", ya, dza)
    dya = matmul("d_ya", dza, W["w_branch_a"], "nt")
    G["w_branch_b"] = wgrad("d_w_branch_b", yb, dzb)
    dyb = matmul("d_yb", dzb, W["w_branch_b"], "nt")
    G["w_gate"] = wgrad("d_w_gate", h, dgp)
    dh_gate = matmul("d_h_gate", dgp, W["w_gate"], "nt")

    def comb_bwd_tile(pid, *t):
        _, vjp = jax.vjp(att_combine, *t[:6])
        return vjp(t[6])

    d_ol = tile_call("att_combine_bwd", comb_bwd_tile, (T // tm,),
                     [_rows(z, tm) for z in o_l] + [_rows(dyb, tm)],
                     [_row_out(T, ATT_OUT, F32, tm)] * 6)
    dqkv = [att_bwd(pa, att[gi][0], att[gi][1], d_ol[gi], d_ol[3 + gi], gi, T) for gi in range(3)]
    d_att = [dqkv[gi][j] for j in range(3) for gi in range(3)]

    def post_bwd_tile(pid, *t):
        _, vjp = jax.vjp(rw_post, *t[:8])
        return vjp(t[8])

    dy, dr_p, dk2_p, dv_p, dg, G["rw_ln_g"], G["rw_ln_b"], d_rk = tile_call(
        "rw_post_bwd", post_bwd_tile, (T // tm,),
        [_rows(z, tm) for z in (y, r, k2, v, g)] + [_par(q) for q in post_params] + [_rows(dya, tm)],
        [_row_out(T, RW_WIDTH, F32, tm)] * 5 + [_acc_out(1, RW_WIDTH)] * 3)
    G["rw_r_k"] = d_rk.reshape(W["rw_r_k"].shape)

    (da, dw, db, dk_s, dr_s, dvT), G["_early_parts"] = rwkv_scan_bwd(
        avec, decay, bvec, k2, r, v, dy, S_all, saT, exchange=_early_grad_sources(G))
    dv_s = _from_head_time(dvT)

    tmb = 256
    rw_in_b = (proj, (tmb, RW_PAD), lambda i: (i, 0))

    def pre_bwd_tile(pid, Pc, halo, *t):
        prev8 = jnp.where(pid[0] > 0, halo, 0.0)
        params = [q.astype(F32) for q in t[:8]]
        dr1, dr2, dw_, dk1, dk2_, dv1, dv2, da_, db_, dg_ = t[8:]
        _, vjp = jax.vjp(rw_pre, Pc, _shift_down(Pc, prev8, 1), *params)
        return vjp((dr1 + dr2, dw_, dk1 + dk2_, dv1 + dv2, da_, db_, dg_))

    cts = (dr_s, dr_p, dw, dk_s, dk2_p, dv_s, dv_p, da, db, dg)
    res = tile_call(
        "rw_pre_bwd", pre_bwd_tile, (T // tmb,),
        [rw_in_b, _prev_halo(proj, tmb, RW_PAD)] + [_par(q) for q in rw_params] + [_rows(z, tmb) for z in cts],
        [_row_out(T, RW_PAD, F32, tmb)] * 2 + [_acc_out(*q.shape) for q in rw_params])
    dPc, dPs = res[0], res[1]
    d_mu, G["rw_w0"], d_wup, G["rw_a0"], d_aup, d_gup, G["rw_k_k"], G["rw_k_a"] = res[2:]
    G["rw_mu"] = _rw_unpad(d_mu)
    G["rw_w_up"], G["rw_a_up"], G["rw_g_up"] = d_wup[:64], d_aup[:64], d_gup[:160]

    def dproj_tile(pid, dPc_t, dPs_t, nxt, *att_t):
        nxt = jnp.where(pid[0] < T // tm - 1, nxt, 0.0)
        tail = jnp.zeros((dPc_t.shape[0], PROJ_TAIL), F32)
        return jnp.concatenate([dPc_t + _shift_up(dPs_t, nxt, 1)] + list(att_t) + [tail], axis=1)

    dproj = tile_call("d_proj", dproj_tile, (T // tm,),
                      [_rows(dPc, tm), _rows(dPs, tm), _next_halo(dPs, tm, RW_PAD, T)] + [_rows(z, tm) for z in d_att],
                      [_row_out(T, PROJ_PAD, BF16, tm)])[0]
    G["w_in_p"] = wgrad("d_w_in", h, dproj)
    w_in_srcs = _w_in_grad_sources(G)
    dh = matmul("d_h", dproj, w_in_p, "nt", res=dh_gate, exchange=w_in_srcs)
    if w_in_srcs:
        dh, G["_w_in_parts"] = dh
    dx, G["g_mix"] = rms_bwd("rms_mix_bwd", x, W["g_mix"], dh, dx1, with_bf16=False)
    return loss, dx, G


def _mesh_pos():
    return lax.axis_index("x"), lax.axis_index("y"), lax.axis_index("c")


def _peer(pos, k):
    x, y, c = pos
    px = 1 - x if k & 4 else x
    py = 1 - y if k & 2 else y
    pc = 1 - c if k & 1 else c
    return (px, py, pc), 4 * px + 2 * py + pc


def all_gather_blocks(name, blocks):
    n = len(blocks)

    def body(*refs):
        x_refs, out_refs = refs[:n], refs[n:2 * n]
        send_sems, recv_sems, local_sems = refs[2 * n:]
        x, y, c = _mesh_pos()
        me, sibling = (x, y, c), (x, y, 1 - c)
        chips = [(1 - x, y), (x, 1 - y), (1 - x, 1 - y)]
        ops = range(n)

        def slot(i, px, py, pc):
            return out_refs[i].at[4 * px + 2 * py + pc]

        def copy(k, i, block, to, own=False):
            return pltpu.make_async_remote_copy(
                src_ref=x_refs[i] if own else slot(i, *block), dst_ref=slot(i, *block),
                send_sem=send_sems.at[k, i], recv_sem=recv_sems.at[k, i],
                device_id=to, device_id_type=pl.DeviceIdType.MESH)

        mine = [pltpu.make_async_copy(x_refs[i], slot(i, *me), local_sems.at[i]) for i in ops]
        first = [copy(0, i, me, sibling, own=True) for i in ops]
        first += [copy(1 + j, i, me, (*chip, c), own=True) for j, chip in enumerate(chips) for i in ops]
        for cp in mine + first:
            cp.start()
        passed = []
        for j, chip in enumerate(chips):
            for i in ops:
                copy(1 + j, i, (*chip, c), me).wait_recv()
                passed.append(copy(4 + j, i, (*chip, c), sibling))
                passed[-1].start()
        for i in ops:
            copy(0, i, sibling, me).wait_recv()
        for j, chip in enumerate(chips):
            for i in ops:
                copy(4 + j, i, (*chip, 1 - c), me).wait_recv()
        for cp in first + passed:
            cp.wait_send()
        for cp in mine:
            cp.wait()

    return pl.pallas_call(
        body, name=name,
        in_specs=[pl.BlockSpec(memory_space=pl.ANY)] * n,
        out_specs=[pl.BlockSpec(memory_space=pl.ANY)] * n,
        out_shape=[jax.ShapeDtypeStruct((N_DEV,) + b.shape, b.dtype) for b in blocks],
        scratch_shapes=[pltpu.SemaphoreType.DMA((N_DEV - 1, n)), pltpu.SemaphoreType.DMA((N_DEV - 1, n)),
                        pltpu.SemaphoreType.DMA((n,))],
        compiler_params=pltpu.CompilerParams(has_side_effects=True),
    )(*blocks)


WHOLE = 0


def _exchange_shapes(srcs):
    shapes = [a.shape[1:] if cols is None else a.shape if cols == WHOLE else (a.shape[0], cols) for a, cols in srcs]
    return [jax.ShapeDtypeStruct((N_DEV,) + s, a.dtype) for s, (a, _) in zip(shapes, srcs)]


def _exchange_sems(n):
    return [pltpu.SemaphoreType.DMA((N_DEV - 1, n)), pltpu.SemaphoreType.DMA((N_DEV - 1, n)),
            pltpu.SemaphoreType.DMA((n,))]


def _exchange_ops(col_widths, x_refs, out_refs, send_sems, recv_sems, local_sems):
    n = len(col_widths)
    pos = _mesh_pos()
    me = 4 * pos[0] + 2 * pos[1] + pos[2]

    def piece(i, d):
        cols = col_widths[i]
        if cols is None:
            return x_refs[i].at[d]
        if cols == WHOLE:
            return x_refs[i]
        return x_refs[i].at[:, pl.ds(pl.multiple_of(d * cols, 128), cols)]

    def local(i):
        return pltpu.make_async_copy(piece(i, me), out_refs[i].at[me], local_sems.at[i])

    def remote(k, i, landing):
        peer, idx = _peer(pos, k)
        return pltpu.make_async_remote_copy(
            src_ref=piece(i, idx), dst_ref=out_refs[i].at[idx if landing else me],
            send_sem=send_sems.at[k - 1, i], recv_sem=recv_sems.at[k - 1, i],
            device_id=peer, device_id_type=pl.DeviceIdType.MESH)

    pairs = [(k, i) for k in range(1, N_DEV) for i in range(n)]

    def start():
        for i in range(n):
            local(i).start()
        for k, i in pairs:
            remote(k, i, False).start()

    def wait():
        for k, i in pairs:
            remote(k, i, True).wait_recv()
        for k, i in pairs:
            remote(k, i, False).wait_send()
        for i in range(n):
            local(i).wait()

    return start, wait


def all_to_all_blocks(name, srcs):
    n = len(srcs)

    def body(*refs):
        start, wait = _exchange_ops([c for _, c in srcs], refs[:n], refs[n:2 * n], *refs[2 * n:])
        start()
        wait()

    return pl.pallas_call(
        body, name=name,
        in_specs=[pl.BlockSpec(memory_space=pl.ANY)] * n,
        out_specs=[pl.BlockSpec(memory_space=pl.ANY)] * n,
        out_shape=_exchange_shapes(srcs),
        scratch_shapes=_exchange_sems(n),
        compiler_params=pltpu.CompilerParams(has_side_effects=True),
    )(*[a for a, _ in srcs])


def _adam_row_tile(R, C):
    best = None
    for t in range(16, R + 1, 16):
        if R % t == 0 and t * C <= ADAM_TILE_ELEMS:
            best = t
    return best if best is not None else R


def reduce_adamw(name, parts, w, m, v):
    _, R, C = parts.shape
    tr = _adam_row_tile(R, C)

    def fn(pid, parts_t, w_t, m_t, v_t):
        g = parts_t[0].astype(F32)
        for i in range(1, N_DEV):
            g = g + parts_t[i].astype(F32)
        m_n = ADAM_B1 * m_t + (1.0 - ADAM_B1) * g
        v_n = ADAM_B2 * v_t + (1.0 - ADAM_B2) * (g * g)
        m_hat = m_n / (1.0 - ADAM_B1 ** ADAM_STEP)
        v_hat = v_n / (1.0 - ADAM_B2 ** ADAM_STEP)
        delta = -ADAM_LR * (m_hat / (jnp.sqrt(v_hat) + ADAM_EPS) + ADAM_WD * w_t)
        return g, delta, m_n, v_n

    row = lambda a: (a, (tr, C), lambda i: (i, 0))
    out = ((R, C), F32, (tr, C), lambda i: (i, 0), None)
    return tile_call(name, fn, (R // tr,),
                     [(parts, (N_DEV, tr, C), lambda i: (0, i, 0)), row(w), row(m), row(v)], [out] * 4)


PARAMS = (
    ("g_mix", (1, 1024), None), ("w_in", (1024, 4128), 1), ("rw_mu", (1, 1824), None), ("rw_w0", (1, 512), None),
    ("rw_w_up", (64, 512), 1), ("rw_a0", (1, 512), None), ("rw_a_up", (64, 512), 1), ("rw_g_up", (160, 512), 1),
    ("rw_k_k", (1, 512), None), ("rw_k_a", (1, 512), None), ("rw_r_k", (8, 64), None), ("rw_ln_g", (1, 512), None),
    ("rw_ln_b", (1, 512), None), ("w_branch_a", (512, 1024), 1), ("w_branch_b", (256, 1024), 1),
    ("w_gate", (1024, 2048), 1), ("b_gate", (1, 2048), None), ("w_out", (1024, 1024), 0), ("g_ffn", (1, 1024), None),
    ("w_up", (1024, 6144), 1), ("conv_w", (3, 6144), 1), ("conv_b", (1, 6144), None), ("w_down", (3072, 1024), 0),
    ("g_ple", (1, 1024), None), ("w_ple_gate", (1024, 1024), 0), ("w_ple", (256, 1024), 1), ("g_final", (1, 1024), None),
)
SHARDED = tuple(q for q in PARAMS if q[2] is not None)
REPLICATED = tuple(q for q in PARAMS if q[2] is None)
BIG_NAMES = ("w_in", "w_up", "w_gate", "w_out", "w_down", "w_ple_gate", "w_branch_a", "w_branch_b", "w_ple")
BIG = tuple(q for q in SHARDED if q[0] in BIG_NAMES)
SMALL_SHARDED = tuple(q for q in SHARDED if q[0] not in BIG_NAMES)
PACK_COLS = 1024
F32_GATHERED = ("conv_w",)


def _local_shape(shape, axis):
    s = list(shape)
    s[axis] //= N_DEV
    return tuple(s)


def _numel(shape):
    return int(np.prod(shape))


def _pad_flat(z, mult):
    n = z.shape[-1]
    total = -(-n // mult) * mult
    return jnp.pad(z, [(0, 0)] * (z.ndim - 1) + [(0, total - n)])


def _full_from_slots(slots, shape, axis):
    loc = _local_shape(shape, axis)
    z = slots.reshape((N_DEV,) + loc)
    if axis == 0:
        return z.reshape(shape)
    return z.transpose(1, 0, 2).reshape(shape)


def _slots_from_full(full, shape, axis):
    loc = _local_shape(shape, axis)
    if axis == 0:
        return full.reshape(N_DEV, _numel(loc))
    return full.reshape(shape[0], N_DEV, loc[1]).transpose(1, 0, 2).reshape(N_DEV, _numel(loc))


W_IN_SLOT = 640
W_IN_LOCAL = 4128 // N_DEV


def _block_shape(shape, axis):
    return _local_shape(shape, axis) if axis is not None else shape


def _pad_w_in(block):
    return jnp.pad(block, ((0, 0), (0, W_IN_SLOT - W_IN_LOCAL)))


def _proj_col(s):
    return s + jnp.where(s >= 1600, 64, 0) + jnp.where(s >= 1664, 64, 0) + jnp.where(s >= 1824, 96, 0)


def _perm_tile(d, c0, width):
    j = lax.broadcasted_iota(jnp.int32, (W_IN_SLOT, width), 0)
    c = c0 + lax.broadcasted_iota(jnp.int32, (W_IN_SLOT, width), 1)
    hit = (_proj_col(d * W_IN_LOCAL + j) == c) & (j < W_IN_LOCAL)
    return jnp.where(hit, 1.0, 0.0).astype(BF16)


PERM_TILE = 768


def w_in_unshuffle(slots):
    _, K, _ = slots.shape
    tn = PERM_TILE
    reach = 3

    def first_slot(j):
        return j + jnp.where(j >= 3, 1, 0) + jnp.where(j >= 5, 1, 0)

    def body(a_ref, o_ref, acc_ref):
        j, kk = pl.program_id(0), pl.program_id(1)
        d = first_slot(j) + kk

        @pl.when(kk == 0)
        def _():
            acc_ref[...] = jnp.zeros_like(acc_ref)

        @pl.when(d < N_DEV)
        def _():
            acc_ref[...] += jnp.dot(a_ref[0], _perm_tile(d, j * tn, tn), preferred_element_type=F32)

        @pl.when(kk == reach - 1)
        def _():
            o_ref[...] = acc_ref[...].astype(o_ref.dtype)

    return pl.pallas_call(
        body, name="w_in_unshuffle", grid=(PROJ_PAD // tn, reach),
        in_specs=[pl.BlockSpec((1, K, W_IN_SLOT), lambda j, kk: (jnp.minimum(first_slot(j) + kk, N_DEV - 1), 0, 0))],
        out_specs=pl.BlockSpec((K, tn), lambda j, kk: (0, j)),
        out_shape=jax.ShapeDtypeStruct((K, PROJ_PAD), BF16),
        scratch_shapes=[pltpu.VMEM((K, tn), F32)],
        compiler_params=_cparams(2),
    )(slots)


def w_in_shuffle_grad(dw):
    K = dw.shape[0]
    tk = PERM_TILE

    def first_tile(d):
        return _proj_col(d * W_IN_LOCAL) // tk

    def body(g_ref, o_ref, acc_ref):
        d, kk = pl.program_id(0), pl.program_id(1)
        perm = _perm_tile(d, (first_tile(d) + kk) * tk, tk)
        part = lax.dot_general(g_ref[...].astype(BF16), perm, NT_DIMS, preferred_element_type=F32)

        @pl.when(kk == 0)
        def _():
            acc_ref[...] = part

        @pl.when(kk == 1)
        def _():
            o_ref[0] = (acc_ref[...] + part).astype(o_ref.dtype)

    return pl.pallas_call(
        body, name="w_in_shuffle_grad", grid=(N_DEV, 2),
        in_specs=[pl.BlockSpec((K, tk), lambda d, kk: (0, first_tile(d) + kk))],
        out_specs=pl.BlockSpec((1, K, W_IN_SLOT), lambda d, kk: (d, 0, 0)),
        out_shape=jax.ShapeDtypeStruct((N_DEV, K, W_IN_SLOT), GRAD_WIRE),
        scratch_shapes=[pltpu.VMEM((K, W_IN_SLOT), F32)],
        compiler_params=_cparams(2),
    )(dw)


def _flat_rows(pieces, dtype, row_mult):
    flat = jnp.concatenate([z.astype(dtype) for z in pieces], axis=-1)
    flat = _pad_flat(flat, row_mult * PACK_COLS)
    return flat.reshape(flat.shape[:-1] + (-1, PACK_COLS))


FIRST = tuple(q for q in BIG if q[0] in ("w_in", "w_gate"))
LATE = tuple(q for q in BIG if q not in FIRST)


def _matrix_from_slots(slots, shape, axis):
    return slots.reshape(shape) if axis == 0 else slots.transpose(1, 0, 2).reshape(shape)


def _late_weight_sources(W):
    return [(blk, WHOLE) for blk in W["_late_blocks"]]


def _late_weights(slots):
    return {n: _matrix_from_slots(s, shape, axis) for (n, shape, axis), s in zip(LATE, slots)}


def gather_weights(local):
    blocks = [(_pad_w_in(local[n]) if n == "w_in" else local[n]).astype(BF16) for n, _, _ in FIRST]
    small = [q for q in SMALL_SHARDED if q[0] not in F32_GATHERED]
    exact = [q for q in SMALL_SHARDED if q[0] in F32_GATHERED]
    blocks.append(_flat_rows([local[n].reshape(-1) for n, _, _ in small], BF16, 16))
    blocks.append(_flat_rows([local[n].reshape(-1) for n, _, _ in exact], F32, 8))
    got = all_gather_blocks("weight_all_gather", blocks)
    full = {"_late_blocks": [local[n].astype(BF16) for n, _, _ in LATE]}
    for (n, shape, axis), slots in zip(FIRST, got):
        if n == "w_in":
            full["w_in_p"] = w_in_unshuffle(slots)
        else:
            full[n] = _matrix_from_slots(slots, shape, axis)
    for group, slots in ((small, got[-2]), (exact, got[-1])):
        slots, off = slots.reshape(N_DEV, -1), 0
        for n, shape, axis in group:
            size = _numel(_local_shape(shape, axis))
            full[n] = _full_from_slots(slots[:, off:off + size], shape, axis)
            off += size
    for n, _, _ in REPLICATED:
        full[n] = local[n]
    return full


LOSS_SLOT = ("_loss", (1, 2), None)
PACKED_SMALL = SMALL_SHARDED + REPLICATED + (LOSS_SLOT,)


def _pack_small(vals):
    pieces = [vals[n].reshape(-1) if n in vals else jnp.zeros((_numel(shape),), F32) for n, shape, _ in PACKED_SMALL]
    return _flat_rows(pieces, F32, 16)


def _unpack_small(packed):
    flat, out, off = packed.reshape(-1), {}, 0
    for n, shape, axis in PACKED_SMALL:
        loc = _block_shape(shape, axis)
        out[n] = flat[off:off + _numel(loc)].reshape(loc)
        off += _numel(loc)
    return out


EARLY = tuple(q for q in BIG if q[0] != "w_in")


def _early_grad_sources(G):
    srcs = []
    for n, shape, axis in EARLY:
        if axis == 0:
            srcs.append((G[n].astype(GRAD_WIRE).reshape((N_DEV,) + _local_shape(shape, axis)), None))
        else:
            srcs.append((G[n].astype(GRAD_WIRE), shape[1] // N_DEV))
    return srcs


def _w_in_grad_sources(G):
    return [(w_in_shuffle_grad(G["w_in_p"]), None)]


def _closing_grad_sources(G, loss_local):
    srcs = []
    rows = [_slots_from_full(G[n].reshape(shape), shape, axis) for n, shape, axis in SMALL_SHARDED]
    loss_hi = loss_local.astype(GRAD_WIRE).astype(F32)
    rep = jnp.concatenate([G[n].reshape(-1) for n, _, _ in REPLICATED] + [jnp.stack([loss_hi, loss_local - loss_hi])])
    rows.append(jnp.broadcast_to(rep[None, :], (N_DEV, rep.shape[0])))
    srcs.append((_flat_rows(rows, GRAD_WIRE, 16), None))
    return srcs


def _step(x, p, target, local_w, local_m, local_v):
    full = gather_weights(local_w)
    loss_local, dx, G = local_step(x, p, target, full)
    closing = all_to_all_blocks("grad_all_to_all", _closing_grad_sources(G, loss_local))
    parts = list(G["_w_in_parts"]) + list(G["_early_parts"]) + list(closing)
    outs = [{}, {}, {}, {}]
    for (n, shape, axis), part in zip((BIG[0],) + EARLY, parts):
        prep = _pad_w_in if n == "w_in" else (lambda z: z)
        res = reduce_adamw("adamw_" + n, part, prep(local_w[n]), prep(local_m[n]), prep(local_v[n]))
        for o, z in zip(outs, res):
            o[n] = z[:, :W_IN_LOCAL] if n == "w_in" else z
    res = reduce_adamw("adamw_small", parts[-1], _pack_small(local_w), _pack_small(local_m), _pack_small(local_v))
    for o, z in zip(outs, res):
        o.update(_unpack_small(z))
    loss = jnp.sum(outs[0]["_loss"])
    return loss, dx, outs


def kernel(x, p, g_mix, w_in, rw_mu, rw_w0, rw_w_up, rw_a0, rw_a_up, rw_g_up, rw_k_k, rw_k_a, rw_r_k, rw_ln_g, rw_ln_b, w_branch_a, w_branch_b, w_gate, b_gate, w_out, g_ffn, w_up, conv_w, conv_b, w_down, g_ple, w_ple_gate, w_ple, g_final, loss_target, m_g_mix, m_w_in, m_rw_mu, m_rw_w0, m_rw_w_up, m_rw_a0, m_rw_a_up, m_rw_g_up, m_rw_k_k, m_rw_k_a, m_rw_r_k, m_rw_ln_g, m_rw_ln_b, m_w_branch_a, m_w_branch_b, m_w_gate, m_b_gate, m_w_out, m_g_ffn, m_w_up, m_conv_w, m_conv_b, m_w_down, m_g_ple, m_w_ple_gate, m_w_ple, m_g_final, v_g_mix, v_w_in, v_rw_mu, v_rw_w0, v_rw_w_up, v_rw_a0, v_rw_a_up, v_rw_g_up, v_rw_k_k, v_rw_k_a, v_rw_r_k, v_rw_ln_g, v_rw_ln_b, v_w_branch_a, v_w_branch_b, v_w_gate, v_b_gate, v_w_out, v_g_ffn, v_w_up, v_conv_w, v_conv_b, v_w_down, v_g_ple, v_w_ple_gate, v_w_ple, v_g_final):
    args = dict(locals())
    names = [n for n, _, _ in PARAMS]
    orig_shape = {n: args[n].shape for n in names}

    def strip(prefix):
        out = {}
        for n, shape, axis in PARAMS:
            a = args[prefix + n]
            loc = _local_shape(shape, axis) if axis is not None else shape
            out[n] = a.reshape(loc)
        return out

    local_w, local_m, local_v = strip(""), strip("m_"), strip("v_")
    T, D = x.shape[-2], x.shape[-1]
    loss, dx, (g, delta, m_n, v_n) = _step(x.reshape(T, D), p.reshape(T, p.shape[-1]), loss_target.reshape(T, D),
                                           local_w, local_m, local_v)
    outs = [loss, dx.reshape(x.shape)]
    for group in (g, delta, m_n, v_n):
        outs += [group[n].reshape(orig_shape[n]) for n in names]
    return tuple(outs)
```

```python
import functools
import math

import numpy as np
import jax
import jax.numpy as jnp
from jax import lax
from jax.experimental import pallas as pl
from jax.experimental.pallas import tpu as pltpu

F32 = jnp.float32
BF16 = jnp.bfloat16
GRAD_WIRE = jnp.bfloat16

N_DEV = 8
NORM_EPS = 1e-6
RW_LN_EPS = 64e-5
HEAD = 64
RW_WIDTH = 512
ATT_GROUPS = ((128, 1), (512, 4), (2048, 16))
ATT_HEADS = 12
ATT_OUT = 256
ATT_COLS = 2304
OFF_XW, OFF_XA, OFF_XG, RW_PAD, PROJ_PAD = 1536, 1664, 1792, 2048, 4608
PROJ_TAIL = PROJ_PAD - RW_PAD - ATT_COLS
D_FF = 3072

ADAM_LR, ADAM_B1, ADAM_B2, ADAM_EPS, ADAM_WD, ADAM_STEP = 0.001, 0.9, 0.999, 1e-08, 0.01, 10

VMEM_LIMIT_BYTES = 56 * 1024 * 1024
ADAM_TILE_ELEMS = 256 * 1024
NEG_BIG = -1e30

NT_DIMS = (((1,), (1,)), ((), ()))
TN_DIMS = (((0,), (0,)), ((), ()))
NN_DIMS = (((1,), (0,)), ((), ()))


def _cparams(n_axes):
    return pltpu.CompilerParams(dimension_semantics=("arbitrary",) * n_axes,
                                vmem_limit_bytes=VMEM_LIMIT_BYTES)


def _split2(x):
    hi = x.astype(BF16)
    lo = (x - hi.astype(F32)).astype(BF16)
    return hi, lo


def _seg_mat(n):
    r = lax.shift_right_logical(lax.broadcasted_iota(jnp.int32, (n, n), 0), 6)
    c = lax.shift_right_logical(lax.broadcasted_iota(jnp.int32, (n, n), 1), 6)
    return jnp.where(r == c, 1.0, 0.0).astype(BF16)


def _segb(x, seg):
    return _segb_stack([(x, 2)], seg)[0]


def _segb_stack(items, seg):
    rows = items[0][0].shape[0]
    parts = []
    for x, passes in items:
        parts += list(_split2(x)) if passes == 2 else [x.astype(BF16)]
    res = jnp.dot(jnp.concatenate(parts, axis=0), seg, preferred_element_type=F32)
    out, at = [], 0
    for _, passes in items:
        piece = res[at * rows:(at + 1) * rows]
        if passes == 2:
            piece = piece + res[(at + 1) * rows:(at + 2) * rows]
        out.append(piece)
        at += passes
    return out


def _segb1(x, seg):
    return jnp.dot(x.astype(BF16), seg, preferred_element_type=F32)


@jax.custom_vjp
def segsum(x):
    return _segb(x, _seg_mat(x.shape[1]))


def _segsum_fwd(x):
    return segsum(x), None


def _segsum_bwd(_, ct):
    return (segsum(ct),)


segsum.defvjp(_segsum_fwd, _segsum_bwd)


@jax.custom_vjp
def bdot(a, b):
    return jnp.dot(a.astype(BF16), b.astype(BF16), preferred_element_type=F32)


def _bdot_fwd(a, b):
    return bdot(a, b), (a, b)


def _bdot_bwd(res, ct):
    a, b = res
    ctb = ct.astype(BF16)
    da = lax.dot_general(ctb, b.astype(BF16), NT_DIMS, preferred_element_type=F32)
    db = lax.dot_general(a.astype(BF16), ctb, TN_DIMS, preferred_element_type=F32)
    return da.astype(a.dtype), db.astype(b.dtype)


bdot.defvjp(_bdot_fwd, _bdot_bwd)


def _sig(x):
    return 1.0 / (1.0 + jnp.exp(-x))


def _softplus(z):
    return jnp.maximum(z, 0.0) + jnp.log(1.0 + jnp.exp(-jnp.abs(z)))


def _gelu_tanh(x):
    return 0.5 * x * (1.0 + jnp.tanh(0.7978845608028654 * (x + 0.044715 * (x * x * x))))


def _rms(x, g):
    return x * lax.rsqrt(jnp.mean(x * x, axis=-1, keepdims=True) + NORM_EPS) * g


def _shift_down(x, prev8, n):
    rolled = pltpu.roll(x, n, 0)
    top = pltpu.roll(prev8, n, 0)
    rid = lax.broadcasted_iota(jnp.int32, (8, x.shape[1]), 0)
    head = jnp.where(rid < n, top, rolled[:8])
    return jnp.concatenate([head, rolled[8:]], axis=0)


def _shift_up(x, next8, n):
    rows = x.shape[0]
    rolled = pltpu.roll(x, rows - n, 0)
    bottom = pltpu.roll(next8, 8 - n, 0)
    rid = lax.broadcasted_iota(jnp.int32, (8, x.shape[1]), 0)
    tail = jnp.where(rid >= 8 - n, bottom, rolled[rows - 8:])
    return jnp.concatenate([rolled[:rows - 8], tail], axis=0)


def tile_call(name, fn, grid, ins, outs, scratch=()):
    n_in, n_out = len(ins), len(outs)
    acc_axes = [o[4] for o in outs]

    def body(*refs):
        pids = tuple(pl.program_id(a) for a in range(len(grid)))
        vals = fn(pids, *[r[...] for r in refs[:n_in]], *refs[n_in + n_out:])
        if not isinstance(vals, (tuple, list)):
            vals = (vals,)
        for o_ref, val, ax in zip(refs[n_in:n_in + n_out], vals, acc_axes):
            if ax is None:
                o_ref[...] = val.astype(o_ref.dtype)
            else:
                @pl.when(pids[ax] == 0)
                def _(o_ref=o_ref):
                    o_ref[...] = jnp.zeros_like(o_ref)

                o_ref[...] += val.astype(o_ref.dtype)

    res = pl.pallas_call(
        body, name=name, grid=grid,
        in_specs=[pl.BlockSpec(b, im) for _, b, im in ins],
        out_specs=[pl.BlockSpec(o[2], o[3]) for o in outs],
        out_shape=[jax.ShapeDtypeStruct(o[0], o[1]) for o in outs],
        scratch_shapes=[pltpu.VMEM(s, d) for s, d in scratch],
        compiler_params=_cparams(len(grid)),
    )(*[a for a, _, _ in ins])
    return res


def _rows(a, tm):
    return (a, (tm, a.shape[1]), lambda i: (i, 0))


def _par(a):
    return (a, a.shape, lambda i: (0, 0))


def _row_out(T, C, dtype, tm):
    return ((T, C), dtype, (tm, C), lambda i: (i, 0), None)


def _acc_out(R, C):
    return ((R, C), F32, (R, C), lambda i: (0, 0), 0)


def _prev_halo(a, tm, C):
    return (a, (8, C), lambda i: (jnp.maximum(i * (tm // 8) - 1, 0), 0))


def _next_halo(a, tm, C, T):
    return (a, (8, C), lambda i: (jnp.minimum((i + 1) * (tm // 8), T // 8 - 1), 0))


def _pick(n, target):
    for t in (target, 2048, 1536, 1024, 768, 512, 384, 256, 128):
        if t <= target and n % t == 0:
            return t
    return n


def matmul(name, a, b, mode="nn", res=None, out_dtype=F32, tm=1024, tn=2048, tk=2048, exchange=()):
    if mode == "nn":
        (M, K), (K2, N) = a.shape, b.shape
    elif mode == "tn":
        (K, M), (K2, N) = a.shape, b.shape
    else:
        (M, K), (N, K2) = a.shape, b.shape
    assert K == K2, (name, a.shape, b.shape, mode)
    tm, tn, tk = _pick(M, tm), _pick(N, tn), _pick(K, tk)
    nk = K // tk
    dims = {"nn": NN_DIMS, "tn": TN_DIMS, "nt": NT_DIMS}[mode]
    a_spec = {"nn": pl.BlockSpec((tm, tk), lambda i, j, k: (i, k)),
              "tn": pl.BlockSpec((tk, tm), lambda i, j, k: (k, i)),
              "nt": pl.BlockSpec((tm, tk), lambda i, j, k: (i, k))}[mode]
    b_spec = {"nn": pl.BlockSpec((tk, tn), lambda i, j, k: (k, j)),
              "tn": pl.BlockSpec((tk, tn), lambda i, j, k: (k, j)),
              "nt": pl.BlockSpec((tn, tk), lambda i, j, k: (j, k))}[mode]
    has_res = res is not None
    nx = len(exchange)
    grid = (M // tm, N // tn, nk)

    def body(*refs):
        a_ref, b_ref = refs[:2]
        r_ref = refs[2] if has_res else None
        refs = refs[2 + has_res:]
        x_refs, o_ref, land_refs, acc_ref = refs[:nx], refs[nx], refs[nx + 1:2 * nx + 1], refs[2 * nx + 1]
        k = pl.program_id(2)
        if nx:
            step = (pl.program_id(0) * grid[1] + pl.program_id(1)) * nk + k
            start, wait = _exchange_ops([c for _, c in exchange], x_refs, land_refs, *refs[2 * nx + 2:])

            @pl.when(step == 0)
            def _():
                start()

        @pl.when(k == 0)
        def _():
            acc_ref[...] = jnp.zeros_like(acc_ref)

        acc_ref[...] += lax.dot_general(a_ref[...].astype(BF16), b_ref[...].astype(BF16), dims,
                                        preferred_element_type=F32)

        @pl.when(k == nk - 1)
        def _():
            out = acc_ref[...]
            if has_res:
                out = out + r_ref[...].astype(F32)
            o_ref[...] = out.astype(o_ref.dtype)

        if nx:
            @pl.when(step == grid[0] * grid[1] * nk - 1)
            def _():
                wait()

    in_specs = [a_spec, b_spec]
    args = [a, b]
    if has_res:
        in_specs.append(pl.BlockSpec((tm, tn), lambda i, j, k: (i, j)))
        args.append(res)
    hbm = pl.BlockSpec(memory_space=pl.ANY)
    out = pl.pallas_call(
        body, name=name, grid=grid,
        in_specs=in_specs + [hbm] * nx,
        out_specs=[pl.BlockSpec((tm, tn), lambda i, j, k: (i, j))] + [hbm] * nx,
        out_shape=[jax.ShapeDtypeStruct((M, N), out_dtype)] + _exchange_shapes(exchange),
        scratch_shapes=[pltpu.VMEM((tm, tn), F32)] + (_exchange_sems(nx) if nx else []),
        compiler_params=pltpu.CompilerParams(dimension_semantics=("arbitrary",) * 3, vmem_limit_bytes=VMEM_LIMIT_BYTES,
                                             has_side_effects=bool(nx)),
    )(*args, *[z for z, _ in exchange])
    return (out[0], out[1:]) if nx else out[0]


def rw_pre(Pc, Ps, mu, w0, w_up, a0, a_up, g_up, k_k, k_a):
    Pm = Pc + (Ps - Pc) * mu
    r, k, v = Pm[:, 0:512], Pm[:, 512:1024], Pm[:, 1024:1536]
    xw, xa, xg = Pm[:, OFF_XW:OFF_XA], Pm[:, OFF_XA:OFF_XG], Pm[:, OFF_XG:RW_PAD]
    w = -_softplus(-(w0 + bdot(jnp.tanh(xw), w_up))) - 0.5
    decay = jnp.exp(-jnp.exp(w))
    a = _sig(a0 + bdot(xa, a_up))
    g = bdot(_sig(xg), g_up)
    kk = k * k_k
    kk = kk / jnp.maximum(jnp.sqrt(segsum(kk * kk)), 1e-12)
    k2 = k * (1.0 + (a - 1.0) * k_a)
    return r, decay, k2, v, -kk, kk * a, g


def rw_post(y, r, k2, v, g, ln_g, ln_b, r_k):
    mean = segsum(y) * (1.0 / HEAD)
    d = y - mean
    var = segsum(d * d) * (1.0 / HEAD)
    yn = d * lax.rsqrt(var + RW_LN_EPS) * ln_g + ln_b
    bonus = segsum(r * k2 * r_k) * v
    return (yn + bonus) * g


def att_combine(o1, o2, o3, l1, l2, l3):
    m = jnp.maximum(jnp.maximum(l1, l2), l3)
    e1, e2, e3 = jnp.exp(l1 - m), jnp.exp(l2 - m), jnp.exp(l3 - m)
    return (e1 * o1 + e2 * o2 + e3 * o3) / (e1 + e2 + e3)


def merge_fn(gp, bg, za, zb):
    s = _sig(gp + bg)
    half = za.shape[1]
    return s[:, :half] * za + s[:, half:] * zb


def tail_loss(x2, zg, pe, g_final, target):
    x3 = x2 + _sig(zg) * pe
    y = _rms(x3, g_final)
    err = (y - target) * (y - target)
    return 0.5 * jnp.sum(jnp.mean(err, axis=-1, keepdims=True))


SCAN_CHUNK = HEAD
SCAN_LANES = 256
SCAN_UNROLL_FWD, SCAN_UNROLL_BWD = 16, 16


def _to_head_time(z):
    T = z.shape[0]
    return z.reshape(T // HEAD, HEAD, RW_WIDTH // HEAD, HEAD).transpose(0, 3, 2, 1).reshape(T // HEAD, HEAD, RW_WIDTH)


def _from_head_time(zt):
    C = zt.shape[0]
    return zt.reshape(C, HEAD, RW_WIDTH // HEAD, HEAD).transpose(0, 3, 2, 1).reshape(C * HEAD, RW_WIDTH)


def _unrolled_loop(n, step, init, unroll):
    def body(i, carry):
        for j in range(unroll):
            carry = step(i * unroll + j, carry)
        return carry

    return lax.fori_loop(0, n // unroll, body, init)


def _lane_groups():
    return [slice(j * SCAN_LANES, (j + 1) * SCAN_LANES) for j in range(RW_WIDTH // SCAN_LANES)]


def scan_pair_terms(a, w, b, k, tm=512):
    T = a.shape[0]

    def fn(pid, a_t, nxt, w_t, b_t, k_t):
        a_next = _shift_up(a_t, jnp.where(pid[0] < T // tm - 1, nxt, 0.0), 1)
        return w_t * a_next, segsum(b_t * a_next), segsum(k_t * a_next)

    return tile_call("scan_pair_terms", fn, (T // tm,),
                     [_rows(a, tm), _next_halo(a, tm, RW_WIDTH, T), _rows(w, tm), _rows(b, tm), _rows(k, tm)],
                     [_row_out(T, RW_WIDTH, F32, tm)] * 3)


def rwkv_scan_fwd(a, w, b, k, r, vT, wa, ba, ka, exchange=()):
    T = a.shape[0]
    C, LW = SCAN_CHUNK, SCAN_LANES
    nC = T // C
    nx = len(exchange)

    def body(*refs):
        a_ref, w_ref, b_ref, k_ref, r_ref, vT_ref, wa_ref, ba_ref, ka_ref = refs[:9]
        x_refs, refs = refs[9:9 + nx], refs[9 + nx:]
        yT_ref, S_ref, saT_ref = refs[:3]
        land_refs, refs = refs[3:3 + nx], refs[3 + nx:]
        st_ref, vb0_ref, vb1_ref, seg_ref = refs[:4]
        if nx:
            start, wait = _exchange_ops([c for _, c in exchange], x_refs, land_refs, *refs[4:])

        @pl.when(pl.program_id(0) == 0)
        def _():
            st_ref[...] = jnp.zeros_like(st_ref)
            seg_ref[...] = _seg_mat(LW)
            if nx:
                start()

        seg = seg_ref[...]
        lane = jnp.bitwise_and(lax.broadcasted_iota(jnp.int32, (1, LW), 1), HEAD - 1)
        groups = _lane_groups()

        def vsel(t, gsl):
            return jnp.where(lane == t, vT_ref[0, :, gsl], 0.0)

        first = _segb_stack([(vsel(s, gsl), 1) for gsl in groups for s in (0, 1)], seg)
        for g, gsl in enumerate(groups):
            vb0_ref[:, gsl] = first[2 * g]
            vb1_ref[:, gsl] = first[2 * g + 1]
        saT_ref[...] = jnp.zeros_like(saT_ref)

        def pair(i, yacc):
            t = 2 * i
            t1 = t + 1
            tp = jnp.maximum(t - 1, 0)
            row = lambda ref, s, gsl: ref[pl.ds(s, 1), gsl]
            Sps = [st_ref[:, gsl] for gsl in groups]
            chain = _segb_stack([(Sp * row(ref, t, gsl), 2) for gsl, Sp in zip(groups, Sps) for ref in (a_ref, wa_ref)],
                                seg)
            sas, us = chain[0::2], chain[1::2]
            S1s = []
            for gsl, Sp, sa, u in zip(groups, Sps, sas, us):
                vb0, vb1 = vb0_ref[:, gsl], vb1_ref[:, gsl]
                S1 = Sp * row(w_ref, t, gsl) + sa * row(b_ref, t, gsl) + vb0 * row(k_ref, t, gsl)
                sa1 = u + sa * row(ba_ref, t, gsl) + vb0 * row(ka_ref, t, gsl)
                st_ref[:, gsl] = S1 * row(w_ref, t1, gsl) + sa1 * row(b_ref, t1, gsl) + vb1 * row(k_ref, t1, gsl)
                S_ref[0, t, :, gsl] = Sp
                S_ref[0, t1, :, gsl] = S1
                S1s.append(S1)
                saT_ref[0, :, gsl] = jnp.where(lane == t, sa, jnp.where(lane == t1, sa1, saT_ref[0, :, gsl]))
            side = _segb_stack([(x, 1) for gsl, Sp, S1 in zip(groups, Sps, S1s)
                                for x in (Sp * row(r_ref, tp, gsl), S1 * row(r_ref, t, gsl),
                                          vsel(t + 2, gsl), vsel(t + 3, gsl))], seg)
            out = []
            for g, (gsl, ya) in enumerate(zip(groups, yacc)):
                yb0, yb1, vb0_ref[:, gsl], vb1_ref[:, gsl] = side[4 * g:4 * g + 4]
                out.append(jnp.where(lane == t, yb1, jnp.where(lane == t - 1, yb0, ya)))
            return tuple(out)

        yacc = _unrolled_loop(C // 2, pair, tuple(jnp.zeros((HEAD, LW), F32) for _ in groups), SCAN_UNROLL_FWD)
        for gsl, ya in zip(groups, yacc):
            S_last = st_ref[:, gsl]
            S_ref[0, C, :, gsl] = S_last
            yb = _segb1(S_last * r_ref[pl.ds(C - 1, 1), gsl], seg)
            yT_ref[0, :, gsl] = jnp.where(lane == C - 1, yb, ya)

        if nx:
            @pl.when(pl.program_id(0) == nC - 1)
            def _():
                wait()

    row = pl.BlockSpec((C, RW_WIDTH), lambda c: (c, 0))
    ht = pl.BlockSpec((1, HEAD, RW_WIDTH), lambda c: (c, 0, 0))
    hbm = pl.BlockSpec(memory_space=pl.ANY)
    res = pl.pallas_call(
        body, name="rwkv_scan_fwd", grid=(nC,),
        in_specs=[row, row, row, row, row, ht, row, row, row] + [hbm] * nx,
        out_specs=[ht, pl.BlockSpec((1, C + 1, HEAD, RW_WIDTH), lambda c: (c, 0, 0, 0)), ht] + [hbm] * nx,
        out_shape=[jax.ShapeDtypeStruct((nC, HEAD, RW_WIDTH), F32),
                   jax.ShapeDtypeStruct((nC, C + 1, HEAD, RW_WIDTH), F32),
                   jax.ShapeDtypeStruct((nC, HEAD, RW_WIDTH), F32)] + _exchange_shapes(exchange),
        scratch_shapes=[pltpu.VMEM((HEAD, RW_WIDTH), F32)] * 3 + [pltpu.VMEM((LW, LW), BF16)]
        + (_exchange_sems(nx) if nx else []),
        compiler_params=pltpu.CompilerParams(dimension_semantics=("arbitrary",), vmem_limit_bytes=VMEM_LIMIT_BYTES,
                                             has_side_effects=bool(nx)),
    )(a, w, b, k, r, vT, wa, ba, ka, *[z for z, _ in exchange])
    return res[:3], res[3:]


def rwkv_scan_bwd(a, w, b, k, r, v, dy, S_all, saT, exchange=()):
    T = a.shape[0]
    C, LW = SCAN_CHUNK, SCAN_LANES
    nC = T // C
    nx = len(exchange)
    n_heads = RW_WIDTH // HEAD
    dyT = _to_head_time(dy).astype(BF16)
    v_rows, dy_rows = v.reshape(T, n_heads, HEAD), dy.reshape(T, n_heads, HEAD)
    sa_rows = _from_head_time(saT).reshape(T, n_heads, HEAD)

    def body(*refs):
        a_ref, w_ref, b_ref, k_ref, r_ref, vR_ref, saR_ref, dyR_ref, dyT_ref, S_ref = refs[:10]
        x_refs, refs = refs[10:10 + nx], refs[10 + nx:]
        da_ref, dw_ref, db_ref, dk_ref, dr_ref, dvT_ref = refs[:6]
        land_refs, refs = refs[6:6 + nx], refs[6 + nx:]
        ds_ref, dyb_ref, seg_ref = refs[:3]
        if nx:
            start, wait = _exchange_ops([c for _, c in exchange], x_refs, land_refs, *refs[3:])

        @pl.when(pl.program_id(0) == 0)
        def _():
            ds_ref[...] = jnp.zeros_like(ds_ref)
            seg_ref[...] = _seg_mat(LW)
            if nx:
                start()

        seg = seg_ref[...]
        lane = jnp.bitwise_and(lax.broadcasted_iota(jnp.int32, (1, LW), 1), HEAD - 1)
        groups = _lane_groups()
        head_row = lax.broadcasted_iota(jnp.int32, (n_heads, LW), 0)
        lane_head = lax.shift_right_logical(lax.broadcasted_iota(jnp.int32, (n_heads, LW), 1), 6)

        def colsum(z):
            return jnp.sum(z, axis=0, keepdims=True)

        def dysel(t, gsl):
            return jnp.where(lane == t, dyT_ref[0, :, gsl], 0.0)

        for gsl, dyb in zip(groups, _segb_stack([(dysel(C - 1, gsl), 1) for gsl in groups], seg)):
            dyb_ref[:, gsl] = dyb

        def step(i, dvacc):
            t = C - 1 - i
            dybs = [dyb_ref[:, gsl] for gsl in groups]
            dSs = [ds_ref[:, gsl] + dyb * r_ref[pl.ds(t, 1), gsl] for gsl, dyb in zip(groups, dybs)]
            dsabs = _segb_stack([(dS * b_ref[pl.ds(t, 1), gsl], 2) for gsl, dS in zip(groups, dSs)], seg)
            for gsl, dS, dsab in zip(groups, dSs, dsabs):
                ds_ref[:, gsl] = dS * w_ref[pl.ds(t, 1), gsl] + dsab * a_ref[pl.ds(t, 1), gsl]
            out = []
            dy_rows = dyR_ref[t].astype(BF16)
            v_sa_rows = jnp.concatenate([vR_ref[t], saR_ref[t]], axis=0).astype(BF16)
            side = _segb_stack([(x, 1) for gsl, dS in zip(groups, dSs)
                                for x in (dS * k_ref[pl.ds(t, 1), gsl], dysel(t - 1, gsl))], seg)
            for g, (gsl, dva, dS, dsab) in enumerate(zip(groups, dvacc, dSs, dsabs)):
                dvb, dyb_ref[:, gsl] = side[2 * g:2 * g + 2]
                Sp = S_ref[0, t, :, gsl]
                own = head_row == lane_head + g * (LW // HEAD)

                def rows_in(rows, mat):
                    full = jnp.dot(rows, mat.astype(BF16), preferred_element_type=F32)
                    return [jnp.sum(jnp.where(own, full[s:s + n_heads], 0.0), axis=0, keepdims=True)
                            for s in range(0, rows.shape[0], n_heads)]

                (dr,) = rows_in(dy_rows, S_ref[0, t + 1, :, gsl])
                dk, db = rows_in(v_sa_rows, dS)
                dr_ref[pl.ds(t, 1), gsl] = dr
                dk_ref[pl.ds(t, 1), gsl] = dk
                db_ref[pl.ds(t, 1), gsl] = db
                dw_ref[pl.ds(t, 1), gsl] = colsum(dS * Sp)
                da_ref[pl.ds(t, 1), gsl] = colsum(Sp * dsab)
                out.append(jnp.where(lane == t, dvb, dva))
            return tuple(out)

        dvacc = _unrolled_loop(C, step, tuple(jnp.zeros((HEAD, LW), F32) for _ in groups), SCAN_UNROLL_BWD)
        for gsl, dva in zip(groups, dvacc):
            dvT_ref[0, :, gsl] = dva

        if nx:
            @pl.when(pl.program_id(0) == nC - 1)
            def _():
                wait()

    row = pl.BlockSpec((C, RW_WIDTH), lambda c: (nC - 1 - c, 0))
    ht = pl.BlockSpec((1, HEAD, RW_WIDTH), lambda c: (nC - 1 - c, 0, 0))
    hbm = pl.BlockSpec(memory_space=pl.ANY)
    per_head = pl.BlockSpec((C, n_heads, HEAD), lambda c: (nC - 1 - c, 0, 0))
    rows_shape = jax.ShapeDtypeStruct((T, RW_WIDTH), F32)
    res = pl.pallas_call(
        body, name="rwkv_scan_bwd", grid=(nC,),
        in_specs=[row, row, row, row, row, per_head, per_head, per_head, ht,
                  pl.BlockSpec((1, C + 1, HEAD, RW_WIDTH), lambda c: (nC - 1 - c, 0, 0, 0))] + [hbm] * nx,
        out_specs=[row, row, row, row, row, ht] + [hbm] * nx,
        out_shape=[rows_shape] * 5 + [jax.ShapeDtypeStruct((nC, HEAD, RW_WIDTH), F32)] + _exchange_shapes(exchange),
        scratch_shapes=[pltpu.VMEM((HEAD, RW_WIDTH), F32), pltpu.VMEM((HEAD, RW_WIDTH), F32),
                        pltpu.VMEM((LW, LW), BF16)] + (_exchange_sems(nx) if nx else []),
        compiler_params=pltpu.CompilerParams(dimension_semantics=("arbitrary",), vmem_limit_bytes=VMEM_LIMIT_BYTES,
                                             has_side_effects=bool(nx)),
    )(a, w, b, k, r, v_rows, sa_rows, dy_rows, dyT, S_all, *[z for z, _ in exchange])
    return res[:6], res[6:]


def _alibi_slope(h):
    return float(np.float32(2.0 ** (-8.0 * (h + 1) / ATT_HEADS)))


ATT_GROUP_HEADS = 4


def _stack_heads(x, lane_head, fill=0.0):
    return jnp.concatenate([jnp.where(lane_head == hh, x, fill) for hh in range(ATT_GROUP_HEADS)], axis=0)


def _unstack_heads(x, lane_head, L):
    out = jnp.zeros((L, x.shape[1]), F32)
    for hh in range(ATT_GROUP_HEADS):
        out = jnp.where(lane_head == hh, x[hh * L:(hh + 1) * L], out)
    return out


def _att_logits(qs, kcat, gi, d, L, n):
    qi = lax.broadcasted_iota(jnp.int32, (L, 2 * L), 0)
    kj = lax.broadcasted_iota(jnp.int32, (L, 2 * L), 1)
    steps = qi + L - kj
    valid = (steps >= 0) & (steps <= L) & ((kj >= L) | (n > 0))
    dist = (d * steps).astype(F32)
    bias = jnp.concatenate([jnp.where(valid, -_alibi_slope(gi * ATT_GROUP_HEADS + hh) * dist, NEG_BIG)
                            for hh in range(ATT_GROUP_HEADS)], axis=0)
    s = lax.dot_general(qs.astype(BF16), kcat, NT_DIMS, preferred_element_type=F32) * (HEAD ** -0.5)
    return jnp.where(bias > 0.5 * NEG_BIG, s + bias, NEG_BIG)


def att_fwd(pa, gi, T):
    window, d = ATT_GROUPS[gi]
    L = window // d
    Tj = T // d
    nb = Tj // L
    pv = pa.reshape(Tj, d * ATT_COLS)
    nblk = ATT_COLS // ATT_OUT

    def fn(pids, q, kp, kc, vp, vc):
        lane_head = lax.shift_right_logical(lax.broadcasted_iota(jnp.int32, (1, ATT_OUT), 1), 6)
        kcat = jnp.concatenate([kp, kc], axis=0).astype(BF16)
        vcat = jnp.concatenate([vp, vc], axis=0).astype(BF16)
        s = _att_logits(_stack_heads(q, lane_head), kcat, gi, d, L, pids[1])
        m = jnp.max(s, axis=-1, keepdims=True)
        p = jnp.exp(s - m)
        l = jnp.sum(p, axis=-1, keepdims=True)
        o = jnp.dot(p.astype(BF16), vcat, preferred_element_type=F32) / l
        lse = jnp.broadcast_to(m + jnp.log(l), o.shape)
        return _unstack_heads(o, lane_head, L), _unstack_heads(lse, lane_head, L)

    blk = (L, ATT_OUT)
    ins = [(pv, blk, lambda r, n: (n, r * nblk + gi)),
           (pv, blk, lambda r, n: (jnp.maximum(n - 1, 0), r * nblk + 3 + gi)),
           (pv, blk, lambda r, n: (n, r * nblk + 3 + gi)),
           (pv, blk, lambda r, n: (jnp.maximum(n - 1, 0), r * nblk + 6 + gi)),
           (pv, blk, lambda r, n: (n, r * nblk + 6 + gi))]
    out = ((Tj, d * ATT_OUT), F32, blk, lambda r, n: (n, r), None)
    o, lseb = tile_call(f"att_fwd_g{gi}", fn, (d, nb), ins, [out, out])
    return o.reshape(T, ATT_OUT), lseb.reshape(T, ATT_OUT)


def att_bwd(pa, o, lseb, do, dlseb, gi, T):
    window, d = ATT_GROUPS[gi]
    L = window // d
    Tj = T // d
    nb = Tj // L
    pv = pa.reshape(Tj, d * ATT_COLS)
    nblk = ATT_COLS // ATT_OUT
    view = lambda z: z.reshape(Tj, d * ATT_OUT)

    def body(q_ref, kp_ref, kc_ref, vp_ref, vc_ref, o_ref, l_ref, do_ref, dl_ref, dq_ref, dk_ref, dv_ref):
        n = pl.program_id(1)

        @pl.when(n == 0)
        def _():
            dk_ref[...] = jnp.zeros_like(dk_ref)
            dv_ref[...] = jnp.zeros_like(dv_ref)

        lane_head = lax.shift_right_logical(lax.broadcasted_iota(jnp.int32, (1, ATT_OUT), 1), 6)
        kcat = jnp.concatenate([kp_ref[...], kc_ref[...]], axis=0).astype(BF16)
        vcat = jnp.concatenate([vp_ref[...], vc_ref[...]], axis=0).astype(BF16)
        qs = _stack_heads(q_ref[...], lane_head)
        dos = _stack_heads(do_ref[...], lane_head)
        lse = jnp.max(_stack_heads(l_ref[...], lane_head, NEG_BIG), axis=-1, keepdims=True)
        dlse = jnp.sum(_stack_heads(dl_ref[...], lane_head), axis=-1, keepdims=True)
        delta = jnp.sum(dos * jnp.concatenate([o_ref[...]] * ATT_GROUP_HEADS, axis=0), axis=-1, keepdims=True)
        p = jnp.exp(_att_logits(qs, kcat, gi, d, L, n) - lse)
        dp = lax.dot_general(dos.astype(BF16), vcat, NT_DIMS, preferred_element_type=F32)
        ds = (p * (dp - delta + dlse)).astype(BF16)
        dq = _unstack_heads(jnp.dot(ds, kcat, preferred_element_type=F32), lane_head, L)
        dkc = lax.dot_general(ds, qs.astype(BF16), TN_DIMS, preferred_element_type=F32)
        dvc = lax.dot_general(p.astype(BF16), dos.astype(BF16), TN_DIMS, preferred_element_type=F32)
        scale = HEAD ** -0.5
        dq_ref[...] = dq * scale
        cur = pl.ds(pl.multiple_of(n * L, L), L)
        dk_ref[cur, :] += dkc[L:] * scale
        dv_ref[cur, :] += dvc[L:]

        @pl.when(n > 0)
        def _():
            prev = pl.ds(pl.multiple_of((n - 1) * L, L), L)
            dk_ref[prev, :] += dkc[:L] * scale
            dv_ref[prev, :] += dvc[:L]

    blk = pl.BlockSpec((L, ATT_OUT), lambda r, n: (n, r))
    res = pl.BlockSpec((Tj, ATT_OUT), lambda r, n: (0, r))
    qspec = lambda off, prev: pl.BlockSpec(
        (L, ATT_OUT), (lambda r, n: (jnp.maximum(n - 1, 0), r * nblk + off + gi)) if prev
        else (lambda r, n: (n, r * nblk + off + gi)))
    shape = jax.ShapeDtypeStruct((Tj, d * ATT_OUT), F32)
    dq, dk, dv = pl.pallas_call(
        body, name=f"att_bwd_g{gi}", grid=(d, nb),
        in_specs=[qspec(0, False), qspec(3, True), qspec(3, False), qspec(6, True), qspec(6, False),
                  blk, blk, blk, blk],
        out_specs=[blk, res, res],
        out_shape=[shape, shape, shape],
        compiler_params=_cparams(2),
    )(pv, pv, pv, pv, pv, view(o), view(lseb), view(do), view(dlseb))
    return dq.reshape(T, ATT_OUT), dk.reshape(T, ATT_OUT), dv.reshape(T, ATT_OUT)


FFN_TM, FFN_TC = 512, 512


def _conv3(u, prev8, cw, cb):
    return cb + cw[0:1] * u + cw[1:2] * _shift_down(u, prev8, 1) + cw[2:3] * _shift_down(u, prev8, 2)


def conv_glu_fwd(u, conv_w, conv_b):
    T = u.shape[0]
    tm, tc = FFN_TM, FFN_TC
    nj, ni = D_FF // tc, T // tm

    def fn(pids, ug, ugh, uv, uvh, cwg, cbg, cwv, cbv):
        first = pids[1] > 0
        cg = _conv3(ug, jnp.where(first, ugh, 0.0), cwg, cbg)
        cv = _conv3(uv, jnp.where(first, uvh, 0.0), cwv, cbv)
        return _gelu_tanh(cg) * cv

    halo = lambda off: (lambda j, i: (jnp.maximum(i * (tm // 8) - 1, 0), j + off))
    ins = [(u, (tm, tc), lambda j, i: (i, j)), (u, (8, tc), halo(0)),
           (u, (tm, tc), lambda j, i: (i, j + nj)), (u, (8, tc), halo(nj)),
           (conv_w, (3, tc), lambda j, i: (0, j)), (conv_b, (1, tc), lambda j, i: (0, j)),
           (conv_w, (3, tc), lambda j, i: (0, j + nj)), (conv_b, (1, tc), lambda j, i: (0, j + nj))]
    out = ((T, D_FF), BF16, (tm, tc), lambda j, i: (i, j), None)
    return tile_call("conv_glu_fwd", fn, (nj, ni), ins, [out])[0]


def conv_glu_bwd(u, conv_w, conv_b, df):
    T = u.shape[0]
    tm, tc = FFN_TM, FFN_TC
    nj, ni = D_FF // tc, T // tm

    def fn(pids, ug, ugh, uv, uvh, cwg, cbg, cwv, cbv, df_t, nxt_g, nxt_v):
        i = ni - 1 - pids[1]
        ugh = jnp.where(i > 0, ugh, 0.0)
        uvh = jnp.where(i > 0, uvh, 0.0)
        cg = _conv3(ug, ugh, cwg, cbg)
        cv = _conv3(uv, uvh, cwv, cbv)
        _, vjp = jax.vjp(lambda g_, v_: _gelu_tanh(g_) * v_, cg, cv)
        dcg, dcv = vjp(df_t.astype(F32))
        cs = lambda z: jnp.sum(z, axis=0, keepdims=True)

        @pl.when(pids[1] == 0)
        def _():
            nxt_g[...] = jnp.zeros_like(nxt_g)
            nxt_v[...] = jnp.zeros_like(nxt_v)

        outs = []
        for dc, cw, nxt_ref in ((dcg, cwg, nxt_g), (dcv, cwv, nxt_v)):
            nxt = nxt_ref[...]
            outs.append(cw[0:1] * dc + cw[1:2] * _shift_up(dc, nxt, 1) + cw[2:3] * _shift_up(dc, nxt, 2))
            nxt_ref[...] = dc[:8]
        for dc, uu, hh in ((dcg, ug, ugh), (dcv, uv, uvh)):
            outs += [cs(dc * uu), cs(dc * _shift_down(uu, hh, 1)), cs(dc * _shift_down(uu, hh, 2)), cs(dc)]
        return outs

    rows = lambda off: (lambda j, r: (ni - 1 - r, j + off))
    halo = lambda off: (lambda j, r: (jnp.maximum((ni - 1 - r) * (tm // 8) - 1, 0), j + off))
    ins = [(u, (tm, tc), rows(0)), (u, (8, tc), halo(0)),
           (u, (tm, tc), rows(nj)), (u, (8, tc), halo(nj)),
           (conv_w, (3, tc), lambda j, r: (0, j)), (conv_b, (1, tc), lambda j, r: (0, j)),
           (conv_w, (3, tc), lambda j, r: (0, j + nj)), (conv_b, (1, tc), lambda j, r: (0, j + nj)),
           (df, (tm, tc), rows(0))]
    big = ((T, D_FF), BF16, (tm, tc), rows(0), None)
    acc = ((1, D_FF), F32, (1, tc), lambda j, r: (0, j), 1)
    res = tile_call("conv_glu_bwd", fn, (nj, ni), ins, [big, big] + [acc] * 8,
                    scratch=[((8, tc), F32), ((8, tc), F32)])
    dconv_w = jnp.concatenate([jnp.concatenate([res[2 + j], res[6 + j]], axis=1) for j in range(3)], axis=0)
    dconv_b = jnp.concatenate([res[5], res[9]], axis=1)
    return res[0], res[1], dconv_w, dconv_b


def _pad_cols(w, total):
    return jnp.pad(w, ((0, 0), (0, total - w.shape[1])))


def _pad_rows(w, total):
    return jnp.pad(w, ((0, total - w.shape[0]), (0, 0)))


def _proj_pad(w):
    z = lambda n: jnp.zeros((w.shape[0], n), w.dtype)
    return jnp.concatenate([w[:, :1600], z(64), w[:, 1600:1664], z(64), w[:, 1664:1824], z(96), w[:, 1824:],
                            z(PROJ_TAIL)], axis=1)


def _proj_unpad(g):
    return jnp.concatenate([g[:, :1600], g[:, OFF_XA:OFF_XA + 64], g[:, OFF_XG:OFF_XG + 160],
                            g[:, RW_PAD:RW_PAD + ATT_COLS]], axis=1)


def _rw_unpad(g):
    return jnp.concatenate([g[:, :1600], g[:, OFF_XA:OFF_XA + 64], g[:, OFF_XG:OFF_XG + 160]], axis=1)


def rms_fwd(name, x, g, tm=512):
    T, D = x.shape
    return tile_call(name, lambda pid, x_t, g_t: _rms(x_t, g_t), (T // tm,),
                     [_rows(x, tm), _par(g)], [_row_out(T, D, BF16, tm)])[0]


def rms_bwd(name, x, g, dh, dres, with_bf16=True, tm=512):
    T, D = x.shape
    out_dtypes = (F32, BF16) if with_bf16 else (F32,)

    def fn(pid, x_t, g_t, dh_t, dres_t):
        _, vjp = jax.vjp(_rms, x_t, g_t)
        dx, dg = vjp(dh_t.astype(F32))
        return (dres_t + dx,) * len(out_dtypes) + (dg,)

    return tile_call(name, fn, (T // tm,), [_rows(x, tm), _par(g), _rows(dh, tm), _rows(dres, tm)],
                     [_row_out(T, D, dt, tm) for dt in out_dtypes] + [_acc_out(1, D)])


def local_step(x, p, target, W):
    T, D = x.shape
    G = {}

    w_in_p = W["w_in_p"]
    mu_p = _proj_pad(_pad_cols(W["rw_mu"], 4128))[:, :RW_PAD]
    w_up_p = _pad_rows(W["rw_w_up"], 128)
    a_up_p = _pad_rows(W["rw_a_up"], 128)
    g_up_p = _pad_rows(W["rw_g_up"], 256)
    r_k = W["rw_r_k"].reshape(1, RW_WIDTH)
    rw_params = [mu_p, W["rw_w0"], w_up_p, W["rw_a0"], a_up_p, g_up_p, W["rw_k_k"], W["rw_k_a"]]

    h = rms_fwd("rms_mix", x, W["g_mix"])
    proj = matmul("proj_in_rw", h, w_in_p[:, :RW_PAD])
    pa = matmul("proj_in_att", h, w_in_p[:, RW_PAD:RW_PAD + ATT_COLS], out_dtype=BF16)
    gp = matmul("proj_gate", h, W["w_gate"])

    tm = 512
    rw_in = (proj, (tm, RW_PAD), lambda i: (i, 0))
    rw_halo = _prev_halo(proj, tm, RW_PAD)

    def rw_pre_tile(pid, Pc, halo, *params):
        prev8 = jnp.where(pid[0] > 0, halo, 0.0)
        params = [q.astype(F32) for q in params]
        return rw_pre(Pc, _shift_down(Pc, prev8, 1), *params)

    r, decay, k2, v, avec, bvec, g = tile_call(
        "rw_pre", rw_pre_tile, (T // tm,), [rw_in, rw_halo] + [_par(q) for q in rw_params],
        [_row_out(T, RW_WIDTH, F32, tm)] * 7)

    wa, ba, ka = scan_pair_terms(avec, decay, bvec, k2)
    vT = _to_head_time(v).astype(BF16)
    (yT, S_all, saT), late_slots = rwkv_scan_fwd(avec, decay, bvec, k2, r, vT, wa, ba, ka,
                                            exchange=_late_weight_sources(W))
    y = _from_head_time(yT)
    W = dict(W, **_late_weights(late_slots))

    post_params = [W["rw_ln_g"], W["rw_ln_b"], r_k]
    ya = tile_call("rw_post", lambda pid, *t: rw_post(*t), (T // tm,),
                   [_rows(z, tm) for z in (y, r, k2, v, g)] + [_par(q) for q in post_params],
                   [_row_out(T, RW_WIDTH, BF16, tm)])[0]

    att = [att_fwd(pa, gi, T) for gi in range(3)]
    o_l = [att[0][0], att[1][0], att[2][0], att[0][1], att[1][1], att[2][1]]
    yb = tile_call("att_combine", lambda pid, *t: att_combine(*t), (T // tm,),
                   [_rows(z, tm) for z in o_l], [_row_out(T, ATT_OUT, BF16, tm)])[0]

    za = matmul("branch_a", ya, W["w_branch_a"])
    zb = matmul("branch_b", yb, W["w_branch_b"])
    merged = tile_call("merge", lambda pid, *t: merge_fn(*t), (T // tm,),
                       [_rows(gp, tm), _par(W["b_gate"]), _rows(za, tm), _rows(zb, tm)],
                       [_row_out(T, D, BF16, tm)])[0]
    x1 = matmul("mix_out", merged, W["w_out"], res=x)

    h2 = rms_fwd("rms_ffn", x1, W["g_ffn"])
    u = matmul("ffn_up", h2, W["w_up"])
    f = conv_glu_fwd(u, W["conv_w"], W["conv_b"])
    x2 = matmul("ffn_down", f, W["w_down"], res=x1)

    h3 = rms_fwd("rms_ple", x2, W["g_ple"])
    zg = matmul("ple_gate", h3, W["w_ple_gate"])
    pe = matmul("ple_embed", p, W["w_ple"])

    def tail_tile(pid, x2_t, zg_t, pe_t, gf, tgt):
        loss, vjp = jax.vjp(lambda a_, b_, c_, d_: tail_loss(a_, b_, c_, d_, tgt), x2_t, zg_t, pe_t, gf)
        dx2, dzg, dpe, dgf = vjp(jnp.ones((), F32))
        return dx2, dzg, dpe, dgf, jnp.full((1, 128), loss, F32)

    tmt = 512
    dx3, dzg, dpe, dgf, loss_acc = tile_call(
        "tail_loss", tail_tile, (T // tmt,),
        [_rows(x2, tmt), _rows(zg, tmt), _rows(pe, tmt), _par(W["g_final"]), _rows(target, tmt)],
        [_row_out(T, D, F32, tmt), _row_out(T, D, BF16, tmt), _row_out(T, D, BF16, tmt),
         _acc_out(1, D), _acc_out(1, 128)])
    loss = loss_acc[0, 0]
    G["g_final"] = dgf

    wgrad = functools.partial(matmul, mode="tn", out_dtype=GRAD_WIRE)
    G["w_ple"] = wgrad("d_w_ple", p, dpe)
    G["w_ple_gate"] = wgrad("d_w_ple_gate", h3, dzg)
    dh3 = matmul("d_h3", dzg, W["w_ple_gate"], "nt")
    dx2, dx2b, G["g_ple"] = rms_bwd("rms_ple_bwd", x2, W["g_ple"], dh3, dx3)

    G["w_down"] = wgrad("d_w_down", f, dx2b)
    df = matmul("d_f", dx2b, W["w_down"], "nt", out_dtype=BF16)
    du_g, du_v, G["conv_w"], G["conv_b"] = conv_glu_bwd(u, W["conv_w"], W["conv_b"], df)
    du = jnp.concatenate([du_g, du_v], axis=1)
    G["w_up"] = wgrad("d_w_up", h2, du)
    dh2 = matmul("d_h2", du, W["w_up"], "nt")
    dx1, dx1b, G["g_ffn"] = rms_bwd("rms_ffn_bwd", x1, W["g_ffn"], dh2, dx2)

    G["w_out"] = wgrad("d_w_out", merged, dx1b)
    dmerged = matmul("d_merged", dx1b, W["w_out"], "nt", out_dtype=BF16)

    def merge_bwd_tile(pid, gp_t, bg, za_t, zb_t, dm_t):
        _, vjp = jax.vjp(merge_fn, gp_t, bg, za_t, zb_t)
        return vjp(dm_t.astype(F32))

    dgp, G["b_gate"], dza, dzb = tile_call(
        "merge_bwd", merge_bwd_tile, (T // tm,),
        [_rows(gp, tm), _par(W["b_gate"]), _rows(za, tm), _rows(zb, tm), _rows(dmerged, tm)],
        [_row_out(T, 2 * D, BF16, tm), _acc_out(1, 2 * D), _row_out(T, D, BF16, tm), _row_out(T, D, BF16, tm)])
    G["w_branch_a"] = wgrad("d_w_branch_a", ya, dza)
    dya = matmul("d_ya", dza, W["w_branch_a"], "nt")
    G["w_branch_b"] = wgrad("d_w_branch_b", yb, dzb)
    dyb = matmul("d_yb", dzb, W["w_branch_b"], "nt")
    G["w_gate"] = wgrad("d_w_gate", h, dgp)
    dh_gate = matmul("d_h_gate", dgp, W["w_gate"], "nt")

    def comb_bwd_tile(pid, *t):
        _, vjp = jax.vjp(att_combine, *t[:6])
        return vjp(t[6])

    d_ol = tile_call("att_combine_bwd", comb_bwd_tile, (T // tm,),
                     [_rows(z, tm) for z in o_l] + [_rows(dyb, tm)],
                     [_row_out(T, ATT_OUT, F32, tm)] * 6)
    dqkv = [att_bwd(pa, att[gi][0], att[gi][1], d_ol[gi], d_ol[3 + gi], gi, T) for gi in range(3)]
    d_att = [dqkv[gi][j] for j in range(3) for gi in range(3)]

    def post_bwd_tile(pid, *t):
        _, vjp = jax.vjp(rw_post, *t[:8])
        return vjp(t[8])

    dy, dr_p, dk2_p, dv_p, dg, G["rw_ln_g"], G["rw_ln_b"], d_rk = tile_call(
        "rw_post_bwd", post_bwd_tile, (T // tm,),
        [_rows(z, tm) for z in (y, r, k2, v, g)] + [_par(q) for q in post_params] + [_rows(dya, tm)],
        [_row_out(T, RW_WIDTH, F32, tm)] * 5 + [_acc_out(1, RW_WIDTH)] * 3)
    G["rw_r_k"] = d_rk.reshape(W["rw_r_k"].shape)

    (da, dw, db, dk_s, dr_s, dvT), G["_early_parts"] = rwkv_scan_bwd(
        avec, decay, bvec, k2, r, v, dy, S_all, saT, exchange=_early_grad_sources(G))
    dv_s = _from_head_time(dvT)

    tmb = 256
    rw_in_b = (proj, (tmb, RW_PAD), lambda i: (i, 0))

    def pre_bwd_tile(pid, Pc, halo, *t):
        prev8 = jnp.where(pid[0] > 0, halo, 0.0)
        params = [q.astype(F32) for q in t[:8]]
        dr1, dr2, dw_, dk1, dk2_, dv1, dv2, da_, db_, dg_ = t[8:]
        _, vjp = jax.vjp(rw_pre, Pc, _shift_down(Pc, prev8, 1), *params)
        return vjp((dr1 + dr2, dw_, dk1 + dk2_, dv1 + dv2, da_, db_, dg_))

    cts = (dr_s, dr_p, dw, dk_s, dk2_p, dv_s, dv_p, da, db, dg)
    res = tile_call(
        "rw_pre_bwd", pre_bwd_tile, (T // tmb,),
        [rw_in_b, _prev_halo(proj, tmb, RW_PAD)] + [_par(q) for q in rw_params] + [_rows(z, tmb) for z in cts],
        [_row_out(T, RW_PAD, F32, tmb)] * 2 + [_acc_out(*q.shape) for q in rw_params])
    dPc, dPs = res[0], res[1]
    d_mu, G["rw_w0"], d_wup, G["rw_a0"], d_aup, d_gup, G["rw_k_k"], G["rw_k_a"] = res[2:]
    G["rw_mu"] = _rw_unpad(d_mu)
    G["rw_w_up"], G["rw_a_up"], G["rw_g_up"] = d_wup[:64], d_aup[:64], d_gup[:160]

    def dproj_tile(pid, dPc_t, dPs_t, nxt, *att_t):
        nxt = jnp.where(pid[0] < T // tm - 1, nxt, 0.0)
        tail = jnp.zeros((dPc_t.shape[0], PROJ_TAIL), F32)
        return jnp.concatenate([dPc_t + _shift_up(dPs_t, nxt, 1)] + list(att_t) + [tail], axis=1)

    dproj = tile_call("d_proj", dproj_tile, (T // tm,),
                      [_rows(dPc, tm), _rows(dPs, tm), _next_halo(dPs, tm, RW_PAD, T)] + [_rows(z, tm) for z in d_att],
                      [_row_out(T, PROJ_PAD, BF16, tm)])[0]
    G["w_in_p"] = wgrad("d_w_in", h, dproj)
    w_in_srcs = _w_in_grad_sources(G)
    dh = matmul("d_h", dproj, w_in_p, "nt", res=dh_gate, exchange=w_in_srcs)
    if w_in_srcs:
        dh, G["_w_in_parts"] = dh
    dx, G["g_mix"] = rms_bwd("rms_mix_bwd", x, W["g_mix"], dh, dx1, with_bf16=False)
    return loss, dx, G


def _mesh_pos():
    return lax.axis_index("x"), lax.axis_index("y"), lax.axis_index("c")


def _peer(pos, k):
    x, y, c = pos
    px = 1 - x if k & 4 else x
    py = 1 - y if k & 2 else y
    pc = 1 - c if k & 1 else c
    return (px, py, pc), 4 * px + 2 * py + pc


def all_gather_blocks(name, blocks):
    n = len(blocks)

    def body(*refs):
        x_refs, out_refs = refs[:n], refs[n:2 * n]
        send_sems, recv_sems, local_sems = refs[2 * n:]
        x, y, c = _mesh_pos()
        me, sibling = (x, y, c), (x, y, 1 - c)
        chips = [(1 - x, y), (x, 1 - y), (1 - x, 1 - y)]
        ops = range(n)

        def slot(i, px, py, pc):
            return out_refs[i].at[4 * px + 2 * py + pc]

        def copy(k, i, block, to, own=False):
            return pltpu.make_async_remote_copy(
                src_ref=x_refs[i] if own else slot(i, *block), dst_ref=slot(i, *block),
                send_sem=send_sems.at[k, i], recv_sem=recv_sems.at[k, i],
                device_id=to, device_id_type=pl.DeviceIdType.MESH)

        mine = [pltpu.make_async_copy(x_refs[i], slot(i, *me), local_sems.at[i]) for i in ops]
        first = [copy(0, i, me, sibling, own=True) for i in ops]
        first += [copy(1 + j, i, me, (*chip, c), own=True) for j, chip in enumerate(chips) for i in ops]
        for cp in mine + first:
            cp.start()
        passed = []
        for j, chip in enumerate(chips):
            for i in ops:
                copy(1 + j, i, (*chip, c), me).wait_recv()
                passed.append(copy(4 + j, i, (*chip, c), sibling))
                passed[-1].start()
        for i in ops:
            copy(0, i, sibling, me).wait_recv()
        for j, chip in enumerate(chips):
            for i in ops:
                copy(4 + j, i, (*chip, 1 - c), me).wait_recv()
        for cp in first + passed:
            cp.wait_send()
        for cp in mine:
            cp.wait()

    return pl.pallas_call(
        body, name=name,
        in_specs=[pl.BlockSpec(memory_space=pl.ANY)] * n,
        out_specs=[pl.BlockSpec(memory_space=pl.ANY)] * n,
        out_shape=[jax.ShapeDtypeStruct((N_DEV,) + b.shape, b.dtype) for b in blocks],
        scratch_shapes=[pltpu.SemaphoreType.DMA((N_DEV - 1, n)), pltpu.SemaphoreType.DMA((N_DEV - 1, n)),
                        pltpu.SemaphoreType.DMA((n,))],
        compiler_params=pltpu.CompilerParams(has_side_effects=True),
    )(*blocks)


WHOLE = 0


def _exchange_shapes(srcs):
    shapes = [a.shape[1:] if cols is None else a.shape if cols == WHOLE else (a.shape[0], cols) for a, cols in srcs]
    return [jax.ShapeDtypeStruct((N_DEV,) + s, a.dtype) for s, (a, _) in zip(shapes, srcs)]


def _exchange_sems(n):
    return [pltpu.SemaphoreType.DMA((N_DEV - 1, n)), pltpu.SemaphoreType.DMA((N_DEV - 1, n)),
            pltpu.SemaphoreType.DMA((n,))]


def _exchange_ops(col_widths, x_refs, out_refs, send_sems, recv_sems, local_sems):
    n = len(col_widths)
    pos = _mesh_pos()
    me = 4 * pos[0] + 2 * pos[1] + pos[2]

    def piece(i, d):
        cols = col_widths[i]
        if cols is None:
            return x_refs[i].at[d]
        if cols == WHOLE:
            return x_refs[i]
        return x_refs[i].at[:, pl.ds(pl.multiple_of(d * cols, 128), cols)]

    def local(i):
        return pltpu.make_async_copy(piece(i, me), out_refs[i].at[me], local_sems.at[i])

    def remote(k, i, landing):
        peer, idx = _peer(pos, k)
        return pltpu.make_async_remote_copy(
            src_ref=piece(i, idx), dst_ref=out_refs[i].at[idx if landing else me],
            send_sem=send_sems.at[k - 1, i], recv_sem=recv_sems.at[k - 1, i],
            device_id=peer, device_id_type=pl.DeviceIdType.MESH)

    pairs = [(k, i) for k in range(1, N_DEV) for i in range(n)]

    def start():
        for i in range(n):
            local(i).start()
        for k, i in pairs:
            remote(k, i, False).start()

    def wait():
        for k, i in pairs:
            remote(k, i, True).wait_recv()
        for k, i in pairs:
            remote(k, i, False).wait_send()
        for i in range(n):
            local(i).wait()

    return start, wait


def all_to_all_blocks(name, srcs):
    n = len(srcs)

    def body(*refs):
        start, wait = _exchange_ops([c for _, c in srcs], refs[:n], refs[n:2 * n], *refs[2 * n:])
        start()
        wait()

    return pl.pallas_call(
        body, name=name,
        in_specs=[pl.BlockSpec(memory_space=pl.ANY)] * n,
        out_specs=[pl.BlockSpec(memory_space=pl.ANY)] * n,
        out_shape=_exchange_shapes(srcs),
        scratch_shapes=_exchange_sems(n),
        compiler_params=pltpu.CompilerParams(has_side_effects=True),
    )(*[a for a, _ in srcs])


def _adam_row_tile(R, C):
    best = None
    for t in range(16, R + 1, 16):
        if R % t == 0 and t * C <= ADAM_TILE_ELEMS:
            best = t
    return best if best is not None else R


def reduce_adamw(name, parts, w, m, v):
    _, R, C = parts.shape
    tr = _adam_row_tile(R, C)

    def fn(pid, parts_t, w_t, m_t, v_t):
        g = parts_t[0].astype(F32)
        for i in range(1, N_DEV):
            g = g + parts_t[i].astype(F32)
        m_n = ADAM_B1 * m_t + (1.0 - ADAM_B1) * g
        v_n = ADAM_B2 * v_t + (1.0 - ADAM_B2) * (g * g)
        m_hat = m_n / (1.0 - ADAM_B1 ** ADAM_STEP)
        v_hat = v_n / (1.0 - ADAM_B2 ** ADAM_STEP)
        delta = -ADAM_LR * (m_hat / (jnp.sqrt(v_hat) + ADAM_EPS) + ADAM_WD * w_t)
        return g, delta, m_n, v_n

    row = lambda a: (a, (tr, C), lambda i: (i, 0))
    out = ((R, C), F32, (tr, C), lambda i: (i, 0), None)
    return tile_call(name, fn, (R // tr,),
                     [(parts, (N_DEV, tr, C), lambda i: (0, i, 0)), row(w), row(m), row(v)], [out] * 4)


PARAMS = (
    ("g_mix", (1, 1024), None), ("w_in", (1024, 4128), 1), ("rw_mu", (1, 1824), None), ("rw_w0", (1, 512), None),
    ("rw_w_up", (64, 512), 1), ("rw_a0", (1, 512), None), ("rw_a_up", (64, 512), 1), ("rw_g_up", (160, 512), 1),
    ("rw_k_k", (1, 512), None), ("rw_k_a", (1, 512), None), ("rw_r_k", (8, 64), None), ("rw_ln_g", (1, 512), None),
    ("rw_ln_b", (1, 512), None), ("w_branch_a", (512, 1024), 1), ("w_branch_b", (256, 1024), 1),
    ("w_gate", (1024, 2048), 1), ("b_gate", (1, 2048), None), ("w_out", (1024, 1024), 0), ("g_ffn", (1, 1024), None),
    ("w_up", (1024, 6144), 1), ("conv_w", (3, 6144), 1), ("conv_b", (1, 6144), None), ("w_down", (3072, 1024), 0),
    ("g_ple", (1, 1024), None), ("w_ple_gate", (1024, 1024), 0), ("w_ple", (256, 1024), 1), ("g_final", (1, 1024), None),
)
SHARDED = tuple(q for q in PARAMS if q[2] is not None)
REPLICATED = tuple(q for q in PARAMS if q[2] is None)
BIG_NAMES = ("w_in", "w_up", "w_gate", "w_out", "w_down", "w_ple_gate", "w_branch_a", "w_branch_b", "w_ple")
BIG = tuple(q for q in SHARDED if q[0] in BIG_NAMES)
SMALL_SHARDED = tuple(q for q in SHARDED if q[0] not in BIG_NAMES)
PACK_COLS = 1024
F32_GATHERED = ("conv_w",)


def _local_shape(shape, axis):
    s = list(shape)
    s[axis] //= N_DEV
    return tuple(s)


def _numel(shape):
    return int(np.prod(shape))


def _pad_flat(z, mult):
    n = z.shape[-1]
    total = -(-n // mult) * mult
    return jnp.pad(z, [(0, 0)] * (z.ndim - 1) + [(0, total - n)])


def _full_from_slots(slots, shape, axis):
    loc = _local_shape(shape, axis)
    z = slots.reshape((N_DEV,) + loc)
    if axis == 0:
        return z.reshape(shape)
    return z.transpose(1, 0, 2).reshape(shape)


def _slots_from_full(full, shape, axis):
    loc = _local_shape(shape, axis)
    if axis == 0:
        return full.reshape(N_DEV, _numel(loc))
    return full.reshape(shape[0], N_DEV, loc[1]).transpose(1, 0, 2).reshape(N_DEV, _numel(loc))


W_IN_SLOT = 640
W_IN_LOCAL = 4128 // N_DEV


def _block_shape(shape, axis):
    return _local_shape(shape, axis) if axis is not None else shape


def _pad_w_in(block):
    return jnp.pad(block, ((0, 0), (0, W_IN_SLOT - W_IN_LOCAL)))


def _proj_col(s):
    return s + jnp.where(s >= 1600, 64, 0) + jnp.where(s >= 1664, 64, 0) + jnp.where(s >= 1824, 96, 0)


def _perm_tile(d, c0, width):
    j = lax.broadcasted_iota(jnp.int32, (W_IN_SLOT, width), 0)
    c = c0 + lax.broadcasted_iota(jnp.int32, (W_IN_SLOT, width), 1)
    hit = (_proj_col(d * W_IN_LOCAL + j) == c) & (j < W_IN_LOCAL)
    return jnp.where(hit, 1.0, 0.0).astype(BF16)


PERM_TILE = 768


def w_in_unshuffle(slots):
    _, K, _ = slots.shape
    tn = PERM_TILE
    reach = 3

    def first_slot(j):
        return j + jnp.where(j >= 3, 1, 0) + jnp.where(j >= 5, 1, 0)

    def body(a_ref, o_ref, acc_ref):
        j, kk = pl.program_id(0), pl.program_id(1)
        d = first_slot(j) + kk

        @pl.when(kk == 0)
        def _():
            acc_ref[...] = jnp.zeros_like(acc_ref)

        @pl.when(d < N_DEV)
        def _():
            acc_ref[...] += jnp.dot(a_ref[0], _perm_tile(d, j * tn, tn), preferred_element_type=F32)

        @pl.when(kk == reach - 1)
        def _():
            o_ref[...] = acc_ref[...].astype(o_ref.dtype)

    return pl.pallas_call(
        body, name="w_in_unshuffle", grid=(PROJ_PAD // tn, reach),
        in_specs=[pl.BlockSpec((1, K, W_IN_SLOT), lambda j, kk: (jnp.minimum(first_slot(j) + kk, N_DEV - 1), 0, 0))],
        out_specs=pl.BlockSpec((K, tn), lambda j, kk: (0, j)),
        out_shape=jax.ShapeDtypeStruct((K, PROJ_PAD), BF16),
        scratch_shapes=[pltpu.VMEM((K, tn), F32)],
        compiler_params=_cparams(2),
    )(slots)


def w_in_shuffle_grad(dw):
    K = dw.shape[0]
    tk = PERM_TILE

    def first_tile(d):
        return _proj_col(d * W_IN_LOCAL) // tk

    def body(g_ref, o_ref, acc_ref):
        d, kk = pl.program_id(0), pl.program_id(1)
        perm = _perm_tile(d, (first_tile(d) + kk) * tk, tk)
        part = lax.dot_general(g_ref[...].astype(BF16), perm, NT_DIMS, preferred_element_type=F32)

        @pl.when(kk == 0)
        def _():
            acc_ref[...] = part

        @pl.when(kk == 1)
        def _():
            o_ref[0] = (acc_ref[...] + part).astype(o_ref.dtype)

    return pl.pallas_call(
        body, name="w_in_shuffle_grad", grid=(N_DEV, 2),
        in_specs=[pl.BlockSpec((K, tk), lambda d, kk: (0, first_tile(d) + kk))],
        out_specs=pl.BlockSpec((1, K, W_IN_SLOT), lambda d, kk: (d, 0, 0)),
        out_shape=jax.ShapeDtypeStruct((N_DEV, K, W_IN_SLOT), GRAD_WIRE),
        scratch_shapes=[pltpu.VMEM((K, W_IN_SLOT), F32)],
        compiler_params=_cparams(2),
    )(dw)


def _flat_rows(pieces, dtype, row_mult):
    flat = jnp.concatenate([z.astype(dtype) for z in pieces], axis=-1)
    flat = _pad_flat(flat, row_mult * PACK_COLS)
    return flat.reshape(flat.shape[:-1] + (-1, PACK_COLS))


FIRST = tuple(q for q in BIG if q[0] in ("w_in", "w_gate"))
LATE = tuple(q for q in BIG if q not in FIRST)


def _matrix_from_slots(slots, shape, axis):
    return slots.reshape(shape) if axis == 0 else slots.transpose(1, 0, 2).reshape(shape)


def _late_weight_sources(W):
    return [(blk, WHOLE) for blk in W["_late_blocks"]]


def _late_weights(slots):
    return {n: _matrix_from_slots(s, shape, axis) for (n, shape, axis), s in zip(LATE, slots)}


def gather_weights(local):
    blocks = [(_pad_w_in(local[n]) if n == "w_in" else local[n]).astype(BF16) for n, _, _ in FIRST]
    small = [q for q in SMALL_SHARDED if q[0] not in F32_GATHERED]
    exact = [q for q in SMALL_SHARDED if q[0] in F32_GATHERED]
    blocks.append(_flat_rows([local[n].reshape(-1) for n, _, _ in small], BF16, 16))
    blocks.append(_flat_rows([local[n].reshape(-1) for n, _, _ in exact], F32, 8))
    got = all_gather_blocks("weight_all_gather", blocks)
    full = {"_late_blocks": [local[n].astype(BF16) for n, _, _ in LATE]}
    for (n, shape, axis), slots in zip(FIRST, got):
        if n == "w_in":
            full["w_in_p"] = w_in_unshuffle(slots)
        else:
            full[n] = _matrix_from_slots(slots, shape, axis)
    for group, slots in ((small, got[-2]), (exact, got[-1])):
        slots, off = slots.reshape(N_DEV, -1), 0
        for n, shape, axis in group:
            size = _numel(_local_shape(shape, axis))
            full[n] = _full_from_slots(slots[:, off:off + size], shape, axis)
            off += size
    for n, _, _ in REPLICATED:
        full[n] = local[n]
    return full


LOSS_SLOT = ("_loss", (1, 2), None)
PACKED_SMALL = SMALL_SHARDED + REPLICATED + (LOSS_SLOT,)


def _pack_small(vals):
    pieces = [vals[n].reshape(-1) if n in vals else jnp.zeros((_numel(shape),), F32) for n, shape, _ in PACKED_SMALL]
    return _flat_rows(pieces, F32, 16)


def _unpack_small(packed):
    flat, out, off = packed.reshape(-1), {}, 0
    for n, shape, axis in PACKED_SMALL:
        loc = _block_shape(shape, axis)
        out[n] = flat[off:off + _numel(loc)].reshape(loc)
        off += _numel(loc)
    return out


EARLY = tuple(q for q in BIG if q[0] != "w_in")


def _early_grad_sources(G):
    srcs = []
    for n, shape, axis in EARLY:
        if axis == 0:
            srcs.append((G[n].astype(GRAD_WIRE).reshape((N_DEV,) + _local_shape(shape, axis)), None))
        else:
            srcs.append((G[n].astype(GRAD_WIRE), shape[1] // N_DEV))
    return srcs


def _w_in_grad_sources(G):
    return [(w_in_shuffle_grad(G["w_in_p"]), None)]


def _closing_grad_sources(G, loss_local):
    srcs = []
    rows = [_slots_from_full(G[n].reshape(shape), shape, axis) for n, shape, axis in SMALL_SHARDED]
    loss_hi = loss_local.astype(GRAD_WIRE).astype(F32)
    rep = jnp.concatenate([G[n].reshape(-1) for n, _, _ in REPLICATED] + [jnp.stack([loss_hi, loss_local - loss_hi])])
    rows.append(jnp.broadcast_to(rep[None, :], (N_DEV, rep.shape[0])))
    srcs.append((_flat_rows(rows, GRAD_WIRE, 16), None))
    return srcs


def _step(x, p, target, local_w, local_m, local_v):
    full = gather_weights(local_w)
    loss_local, dx, G = local_step(x, p, target, full)
    closing = all_to_all_blocks("grad_all_to_all", _closing_grad_sources(G, loss_local))
    parts = list(G["_w_in_parts"]) + list(G["_early_parts"]) + list(closing)
    outs = [{}, {}, {}, {}]
    for (n, shape, axis), part in zip((BIG[0],) + EARLY, parts):
        prep = _pad_w_in if n == "w_in" else (lambda z: z)
        res = reduce_adamw("adamw_" + n, part, prep(local_w[n]), prep(local_m[n]), prep(local_v[n]))
        for o, z in zip(outs, res):
            o[n] = z[:, :W_IN_LOCAL] if n == "w_in" else z
    res = reduce_adamw("adamw_small", parts[-1], _pack_small(local_w), _pack_small(local_m), _pack_small(local_v))
    for o, z in zip(outs, res):
        o.update(_unpack_small(z))
    loss = jnp.sum(outs[0]["_loss"])
    return loss, dx, outs


def kernel(x, p, g_mix, w_in, rw_mu, rw_w0, rw_w_up, rw_a0, rw_a_up, rw_g_up, rw_k_k, rw_k_a, rw_r_k, rw_ln_g, rw_ln_b, w_branch_a, w_branch_b, w_gate, b_gate, w_out, g_ffn, w_up, conv_w, conv_b, w_down, g_ple, w_ple_gate, w_ple, g_final, loss_target, m_g_mix, m_w_in, m_rw_mu, m_rw_w0, m_rw_w_up, m_rw_a0, m_rw_a_up, m_rw_g_up, m_rw_k_k, m_rw_k_a, m_rw_r_k, m_rw_ln_g, m_rw_ln_b, m_w_branch_a, m_w_branch_b, m_w_gate, m_b_gate, m_w_out, m_g_ffn, m_w_up, m_conv_w, m_conv_b, m_w_down, m_g_ple, m_w_ple_gate, m_w_ple, m_g_final, v_g_mix, v_w_in, v_rw_mu, v_rw_w0, v_rw_w_up, v_rw_a0, v_rw_a_up, v_rw_g_up, v_rw_k_k, v_rw_k_a, v_rw_r_k, v_rw_ln_g, v_rw_ln_b, v_w_branch_a, v_w_branch_b, v_w_gate, v_b_gate, v_w_out, v_g_ffn, v_w_up, v_conv_w, v_conv_b, v_w_down, v_g_ple, v_w_ple_gate, v_w_ple, v_g_final):
    args = dict(locals())
    names = [n for n, _, _ in PARAMS]
    orig_shape = {n: args[n].shape for n in names}

    def strip(prefix):
        out = {}
        for n, shape, axis in PARAMS:
            a = args[prefix + n]
            loc = _local_shape(shape, axis) if axis is not None else shape
            out[n] = a.reshape(loc)
        return out

    local_w, local_m, local_v = strip(""), strip("m_"), strip("v_")
    T, D = x.shape[-2], x.shape[-1]
    loss, dx, (g, delta, m_n, v_n) = _step(x.reshape(T, D), p.reshape(T, p.shape[-1]), loss_target.reshape(T, D),
                                           local_w, local_m, local_v)
    outs = [loss, dx.reshape(x.shape)]
    for group in (g, delta, m_n, v_n):
        outs += [group[n].reshape(orig_shape[n]) for n in names]
    return tuple(outs)
```

```python
import functools
import math

import numpy as np
import jax
import jax.numpy as jnp
from jax import lax
from jax.experimental import pallas as pl
from jax.experimental.pallas import tpu as pltpu

F32 = jnp.float32
BF16 = jnp.bfloat16
GRAD_WIRE = jnp.bfloat16

N_DEV = 8
NORM_EPS = 1e-6
RW_LN_EPS = 64e-5
HEAD = 64
RW_WIDTH = 512
ATT_GROUPS = ((128, 1), (512, 4), (2048, 16))
ATT_HEADS = 12
ATT_OUT = 256
ATT_COLS = 2304
OFF_XW, OFF_XA, OFF_XG, RW_PAD, PROJ_PAD = 1536, 1664, 1792, 2048, 4608
PROJ_TAIL = PROJ_PAD - RW_PAD - ATT_COLS
D_FF = 3072

ADAM_LR, ADAM_B1, ADAM_B2, ADAM_EPS, ADAM_WD, ADAM_STEP = 0.001, 0.9, 0.999, 1e-08, 0.01, 10

VMEM_LIMIT_BYTES = 56 * 1024 * 1024
ADAM_TILE_ELEMS = 256 * 1024
NEG_BIG = -1e30

NT_DIMS = (((1,), (1,)), ((), ()))
TN_DIMS = (((0,), (0,)), ((), ()))
NN_DIMS = (((1,), (0,)), ((), ()))


def _cparams(n_axes):
    return pltpu.CompilerParams(dimension_semantics=("arbitrary",) * n_axes,
                                vmem_limit_bytes=VMEM_LIMIT_BYTES)


def _split2(x):
    hi = x.astype(BF16)
    lo = (x - hi.astype(F32)).astype(BF16)
    return hi, lo


def _seg_mat(n):
    r = lax.shift_right_logical(lax.broadcasted_iota(jnp.int32, (n, n), 0), 6)
    c = lax.shift_right_logical(lax.broadcasted_iota(jnp.int32, (n, n), 1), 6)
    return jnp.where(r == c, 1.0, 0.0).astype(BF16)


def _segb(x, seg):
    return _segb_stack([(x, 2)], seg)[0]


def _segb_stack(items, seg):
    rows = items[0][0].shape[0]
    parts = []
    for x, passes in items:
        parts += list(_split2(x)) if passes == 2 else [x.astype(BF16)]
    res = jnp.dot(jnp.concatenate(parts, axis=0), seg, preferred_element_type=F32)
    out, at = [], 0
    for _, passes in items:
        piece = res[at * rows:(at + 1) * rows]
        if passes == 2:
            piece = piece + res[(at + 1) * rows:(at + 2) * rows]
        out.append(piece)
        at += passes
    return out


def _segb1(x, seg):
    return jnp.dot(x.astype(BF16), seg, preferred_element_type=F32)


@jax.custom_vjp
def segsum(x):
    return _segb(x, _seg_mat(x.shape[1]))


def _segsum_fwd(x):
    return segsum(x), None


def _segsum_bwd(_, ct):
    return (segsum(ct),)


segsum.defvjp(_segsum_fwd, _segsum_bwd)


@jax.custom_vjp
def bdot(a, b):
    return jnp.dot(a.astype(BF16), b.astype(BF16), preferred_element_type=F32)


def _bdot_fwd(a, b):
    return bdot(a, b), (a, b)


def _bdot_bwd(res, ct):
    a, b = res
    ctb = ct.astype(BF16)
    da = lax.dot_general(ctb, b.astype(BF16), NT_DIMS, preferred_element_type=F32)
    db = lax.dot_general(a.astype(BF16), ctb, TN_DIMS, preferred_element_type=F32)
    return da.astype(a.dtype), db.astype(b.dtype)


bdot.defvjp(_bdot_fwd, _bdot_bwd)


def _sig(x):
    return 1.0 / (1.0 + jnp.exp(-x))


def _softplus(z):
    return jnp.maximum(z, 0.0) + jnp.log(1.0 + jnp.exp(-jnp.abs(z)))


def _gelu_tanh(x):
    return 0.5 * x * (1.0 + jnp.tanh(0.7978845608028654 * (x + 0.044715 * (x * x * x))))


def _rms(x, g):
    return x * lax.rsqrt(jnp.mean(x * x, axis=-1, keepdims=True) + NORM_EPS) * g


def _shift_down(x, prev8, n):
    rolled = pltpu.roll(x, n, 0)
    top = pltpu.roll(prev8, n, 0)
    rid = lax.broadcasted_iota(jnp.int32, (8, x.shape[1]), 0)
    head = jnp.where(rid < n, top, rolled[:8])
    return jnp.concatenate([head, rolled[8:]], axis=0)


def _shift_up(x, next8, n):
    rows = x.shape[0]
    rolled = pltpu.roll(x, rows - n, 0)
    bottom = pltpu.roll(next8, 8 - n, 0)
    rid = lax.broadcasted_iota(jnp.int32, (8, x.shape[1]), 0)
    tail = jnp.where(rid >= 8 - n, bottom, rolled[rows - 8:])
    return jnp.concatenate([rolled[:rows - 8], tail], axis=0)


def tile_call(name, fn, grid, ins, outs, scratch=()):
    n_in, n_out = len(ins), len(outs)
    acc_axes = [o[4] for o in outs]

    def body(*refs):
        pids = tuple(pl.program_id(a) for a in range(len(grid)))
        vals = fn(pids, *[r[...] for r in refs[:n_in]], *refs[n_in + n_out:])
        if not isinstance(vals, (tuple, list)):
            vals = (vals,)
        for o_ref, val, ax in zip(refs[n_in:n_in + n_out], vals, acc_axes):
            if ax is None:
                o_ref[...] = val.astype(o_ref.dtype)
            else:
                @pl.when(pids[ax] == 0)
                def _(o_ref=o_ref):
                    o_ref[...] = jnp.zeros_like(o_ref)

                o_ref[...] += val.astype(o_ref.dtype)

    res = pl.pallas_call(
        body, name=name, grid=grid,
        in_specs=[pl.BlockSpec(b, im) for _, b, im in ins],
        out_specs=[pl.BlockSpec(o[2], o[3]) for o in outs],
        out_shape=[jax.ShapeDtypeStruct(o[0], o[1]) for o in outs],
        scratch_shapes=[pltpu.VMEM(s, d) for s, d in scratch],
        compiler_params=_cparams(len(grid)),
    )(*[a for a, _, _ in ins])
    return res


def _rows(a, tm):
    return (a, (tm, a.shape[1]), lambda i: (i, 0))


def _par(a):
    return (a, a.shape, lambda i: (0, 0))


def _row_out(T, C, dtype, tm):
    return ((T, C), dtype, (tm, C), lambda i: (i, 0), None)


def _acc_out(R, C):
    return ((R, C), F32, (R, C), lambda i: (0, 0), 0)


def _prev_halo(a, tm, C):
    return (a, (8, C), lambda i: (jnp.maximum(i * (tm // 8) - 1, 0), 0))


def _next_halo(a, tm, C, T):
    return (a, (8, C), lambda i: (jnp.minimum((i + 1) * (tm // 8), T // 8 - 1), 0))


def _pick(n, target):
    for t in (target, 2048, 1536, 1024, 768, 512, 384, 256, 128):
        if t <= target and n % t == 0:
            return t
    return n


def matmul(name, a, b, mode="nn", res=None, out_dtype=F32, tm=1024, tn=2048, tk=2048, exchange=()):
    if mode == "nn":
        (M, K), (K2, N) = a.shape, b.shape
    elif mode == "tn":
        (K, M), (K2, N) = a.shape, b.shape
    else:
        (M, K), (N, K2) = a.shape, b.shape
    assert K == K2, (name, a.shape, b.shape, mode)
    tm, tn, tk = _pick(M, tm), _pick(N, tn), _pick(K, tk)
    nk = K // tk
    dims = {"nn": NN_DIMS, "tn": TN_DIMS, "nt": NT_DIMS}[mode]
    a_spec = {"nn": pl.BlockSpec((tm, tk), lambda i, j, k: (i, k)),
              "tn": pl.BlockSpec((tk, tm), lambda i, j, k: (k, i)),
              "nt": pl.BlockSpec((tm, tk), lambda i, j, k: (i, k))}[mode]
    b_spec = {"nn": pl.BlockSpec((tk, tn), lambda i, j, k: (k, j)),
              "tn": pl.BlockSpec((tk, tn), lambda i, j, k: (k, j)),
              "nt": pl.BlockSpec((tn, tk), lambda i, j, k: (j, k))}[mode]
    has_res = res is not None
    nx = len(exchange)
    grid = (M // tm, N // tn, nk)

    def body(*refs):
        a_ref, b_ref = refs[:2]
        r_ref = refs[2] if has_res else None
        refs = refs[2 + has_res:]
        x_refs, o_ref, land_refs, acc_ref = refs[:nx], refs[nx], refs[nx + 1:2 * nx + 1], refs[2 * nx + 1]
        k = pl.program_id(2)
        if nx:
            step = (pl.program_id(0) * grid[1] + pl.program_id(1)) * nk + k
            start, wait = _exchange_ops([c for _, c in exchange], x_refs, land_refs, *refs[2 * nx + 2:])

            @pl.when(step == 0)
            def _():
                start()

        @pl.when(k == 0)
        def _():
            acc_ref[...] = jnp.zeros_like(acc_ref)

        acc_ref[...] += lax.dot_general(a_ref[...].astype(BF16), b_ref[...].astype(BF16), dims,
                                        preferred_element_type=F32)

        @pl.when(k == nk - 1)
        def _():
            out = acc_ref[...]
            if has_res:
                out = out + r_ref[...].astype(F32)
            o_ref[...] = out.astype(o_ref.dtype)

        if nx:
            @pl.when(step == grid[0] * grid[1] * nk - 1)
            def _():
                wait()

    in_specs = [a_spec, b_spec]
    args = [a, b]
    if has_res:
        in_specs.append(pl.BlockSpec((tm, tn), lambda i, j, k: (i, j)))
        args.append(res)
    hbm = pl.BlockSpec(memory_space=pl.ANY)
    out = pl.pallas_call(
        body, name=name, grid=grid,
        in_specs=in_specs + [hbm] * nx,
        out_specs=[pl.BlockSpec((tm, tn), lambda i, j, k: (i, j))] + [hbm] * nx,
        out_shape=[jax.ShapeDtypeStruct((M, N), out_dtype)] + _exchange_shapes(exchange),
        scratch_shapes=[pltpu.VMEM((tm, tn), F32)] + (_exchange_sems(nx) if nx else []),
        compiler_params=pltpu.CompilerParams(dimension_semantics=("arbitrary",) * 3, vmem_limit_bytes=VMEM_LIMIT_BYTES,
                                             has_side_effects=bool(nx)),
    )(*args, *[z for z, _ in exchange])
    return (out[0], out[1:]) if nx else out[0]


def rw_pre(Pc, Ps, mu, w0, w_up, a0, a_up, g_up, k_k, k_a):
    Pm = Pc + (Ps - Pc) * mu
    r, k, v = Pm[:, 0:512], Pm[:, 512:1024], Pm[:, 1024:1536]
    xw, xa, xg = Pm[:, OFF_XW:OFF_XA], Pm[:, OFF_XA:OFF_XG], Pm[:, OFF_XG:RW_PAD]
    w = -_softplus(-(w0 + bdot(jnp.tanh(xw), w_up))) - 0.5
    decay = jnp.exp(-jnp.exp(w))
    a = _sig(a0 + bdot(xa, a_up))
    g = bdot(_sig(xg), g_up)
    kk = k * k_k
    kk = kk / jnp.maximum(jnp.sqrt(segsum(kk * kk)), 1e-12)
    k2 = k * (1.0 + (a - 1.0) * k_a)
    return r, decay, k2, v, -kk, kk * a, g


def rw_post(y, r, k2, v, g, ln_g, ln_b, r_k):
    mean = segsum(y) * (1.0 / HEAD)
    d = y - mean
    var = segsum(d * d) * (1.0 / HEAD)
    yn = d * lax.rsqrt(var + RW_LN_EPS) * ln_g + ln_b
    bonus = segsum(r * k2 * r_k) * v
    return (yn + bonus) * g


def att_combine(o1, o2, o3, l1, l2, l3):
    m = jnp.maximum(jnp.maximum(l1, l2), l3)
    e1, e2, e3 = jnp.exp(l1 - m), jnp.exp(l2 - m), jnp.exp(l3 - m)
    return (e1 * o1 + e2 * o2 + e3 * o3) / (e1 + e2 + e3)


def merge_fn(gp, bg, za, zb):
    s = _sig(gp + bg)
    half = za.shape[1]
    return s[:, :half] * za + s[:, half:] * zb


def tail_loss(x2, zg, pe, g_final, target):
    x3 = x2 + _sig(zg) * pe
    y = _rms(x3, g_final)
    err = (y - target) * (y - target)
    return 0.5 * jnp.sum(jnp.mean(err, axis=-1, keepdims=True))


SCAN_CHUNK = HEAD
SCAN_LANES = 256
SCAN_UNROLL_FWD, SCAN_UNROLL_BWD = 32, 32


def _to_head_time(z):
    T = z.shape[0]
    return z.reshape(T // HEAD, HEAD, RW_WIDTH // HEAD, HEAD).transpose(0, 3, 2, 1).reshape(T // HEAD, HEAD, RW_WIDTH)


def _from_head_time(zt):
    C = zt.shape[0]
    return zt.reshape(C, HEAD, RW_WIDTH // HEAD, HEAD).transpose(0, 3, 2, 1).reshape(C * HEAD, RW_WIDTH)


def _unrolled_loop(n, step, init, unroll):
    def body(i, carry):
        for j in range(unroll):
            carry = step(i * unroll + j, carry)
        return carry

    return lax.fori_loop(0, n // unroll, body, init)


def _lane_groups():
    return [slice(j * SCAN_LANES, (j + 1) * SCAN_LANES) for j in range(RW_WIDTH // SCAN_LANES)]


def scan_pair_terms(a, w, b, k, tm=512):
    T = a.shape[0]

    def fn(pid, a_t, nxt, w_t, b_t, k_t):
        a_next = _shift_up(a_t, jnp.where(pid[0] < T // tm - 1, nxt, 0.0), 1)
        return w_t * a_next, segsum(b_t * a_next), segsum(k_t * a_next)

    return tile_call("scan_pair_terms", fn, (T // tm,),
                     [_rows(a, tm), _next_halo(a, tm, RW_WIDTH, T), _rows(w, tm), _rows(b, tm), _rows(k, tm)],
                     [_row_out(T, RW_WIDTH, F32, tm)] * 3)


def rwkv_scan_fwd(a, w, b, k, r, vT, wa, ba, ka, exchange=()):
    T = a.shape[0]
    C, LW = SCAN_CHUNK, SCAN_LANES
    nC = T // C
    nx = len(exchange)

    def body(*refs):
        a_ref, w_ref, b_ref, k_ref, r_ref, vT_ref, wa_ref, ba_ref, ka_ref = refs[:9]
        x_refs, refs = refs[9:9 + nx], refs[9 + nx:]
        yT_ref, S_ref, saT_ref = refs[:3]
        land_refs, refs = refs[3:3 + nx], refs[3 + nx:]
        st_ref, vb0_ref, vb1_ref, seg_ref = refs[:4]
        if nx:
            start, wait = _exchange_ops([c for _, c in exchange], x_refs, land_refs, *refs[4:])

        @pl.when(pl.program_id(0) == 0)
        def _():
            st_ref[...] = jnp.zeros_like(st_ref)
            seg_ref[...] = _seg_mat(LW)
            if nx:
                start()

        seg = seg_ref[...]
        lane = jnp.bitwise_and(lax.broadcasted_iota(jnp.int32, (1, LW), 1), HEAD - 1)
        groups = _lane_groups()

        def vsel(t, gsl):
            return jnp.where(lane == t, vT_ref[0, :, gsl], 0.0)

        first = _segb_stack([(vsel(s, gsl), 1) for gsl in groups for s in (0, 1)], seg)
        for g, gsl in enumerate(groups):
            vb0_ref[:, gsl] = first[2 * g]
            vb1_ref[:, gsl] = first[2 * g + 1]
        saT_ref[...] = jnp.zeros_like(saT_ref)

        def pair(i, yacc):
            t = 2 * i
            t1 = t + 1
            tp = jnp.maximum(t - 1, 0)
            row = lambda ref, s, gsl: ref[pl.ds(s, 1), gsl]
            Sps = [st_ref[:, gsl] for gsl in groups]
            chain = _segb_stack([(Sp * row(ref, t, gsl), 2) for gsl, Sp in zip(groups, Sps) for ref in (a_ref, wa_ref)],
                                seg)
            sas, us = chain[0::2], chain[1::2]
            S1s = []
            for gsl, Sp, sa, u in zip(groups, Sps, sas, us):
                vb0, vb1 = vb0_ref[:, gsl], vb1_ref[:, gsl]
                S1 = Sp * row(w_ref, t, gsl) + sa * row(b_ref, t, gsl) + vb0 * row(k_ref, t, gsl)
                sa1 = u + sa * row(ba_ref, t, gsl) + vb0 * row(ka_ref, t, gsl)
                st_ref[:, gsl] = S1 * row(w_ref, t1, gsl) + sa1 * row(b_ref, t1, gsl) + vb1 * row(k_ref, t1, gsl)
                S_ref[0, t, :, gsl] = Sp
                S_ref[0, t1, :, gsl] = S1
                S1s.append(S1)
                saT_ref[0, :, gsl] = jnp.where(lane == t, sa, jnp.where(lane == t1, sa1, saT_ref[0, :, gsl]))
            side = _segb_stack([(x, 1) for gsl, Sp, S1 in zip(groups, Sps, S1s)
                                for x in (Sp * row(r_ref, tp, gsl), S1 * row(r_ref, t, gsl),
                                          vsel(t + 2, gsl), vsel(t + 3, gsl))], seg)
            out = []
            for g, (gsl, ya) in enumerate(zip(groups, yacc)):
                yb0, yb1, vb0_ref[:, gsl], vb1_ref[:, gsl] = side[4 * g:4 * g + 4]
                out.append(jnp.where(lane == t, yb1, jnp.where(lane == t - 1, yb0, ya)))
            return tuple(out)

        yacc = _unrolled_loop(C // 2, pair, tuple(jnp.zeros((HEAD, LW), F32) for _ in groups), SCAN_UNROLL_FWD)
        for gsl, ya in zip(groups, yacc):
            S_last = st_ref[:, gsl]
            S_ref[0, C, :, gsl] = S_last
            yb = _segb1(S_last * r_ref[pl.ds(C - 1, 1), gsl], seg)
            yT_ref[0, :, gsl] = jnp.where(lane == C - 1, yb, ya)

        if nx:
            @pl.when(pl.program_id(0) == nC - 1)
            def _():
                wait()

    row = pl.BlockSpec((C, RW_WIDTH), lambda c: (c, 0))
    ht = pl.BlockSpec((1, HEAD, RW_WIDTH), lambda c: (c, 0, 0))
    hbm = pl.BlockSpec(memory_space=pl.ANY)
    res = pl.pallas_call(
        body, name="rwkv_scan_fwd", grid=(nC,),
        in_specs=[row, row, row, row, row, ht, row, row, row] + [hbm] * nx,
        out_specs=[ht, pl.BlockSpec((1, C + 1, HEAD, RW_WIDTH), lambda c: (c, 0, 0, 0)), ht] + [hbm] * nx,
        out_shape=[jax.ShapeDtypeStruct((nC, HEAD, RW_WIDTH), F32),
                   jax.ShapeDtypeStruct((nC, C + 1, HEAD, RW_WIDTH), F32),
                   jax.ShapeDtypeStruct((nC, HEAD, RW_WIDTH), F32)] + _exchange_shapes(exchange),
        scratch_shapes=[pltpu.VMEM((HEAD, RW_WIDTH), F32)] * 3 + [pltpu.VMEM((LW, LW), BF16)]
        + (_exchange_sems(nx) if nx else []),
        compiler_params=pltpu.CompilerParams(dimension_semantics=("arbitrary",), vmem_limit_bytes=VMEM_LIMIT_BYTES,
                                             has_side_effects=bool(nx)),
    )(a, w, b, k, r, vT, wa, ba, ka, *[z for z, _ in exchange])
    return res[:3], res[3:]


def rwkv_scan_bwd(a, w, b, k, r, v, dy, S_all, saT, exchange=()):
    T = a.shape[0]
    C, LW = SCAN_CHUNK, SCAN_LANES
    nC = T // C
    nx = len(exchange)
    n_heads = RW_WIDTH // HEAD
    dyT = _to_head_time(dy).astype(BF16)
    v_rows, dy_rows = v.reshape(T, n_heads, HEAD), dy.reshape(T, n_heads, HEAD)
    sa_rows = _from_head_time(saT).reshape(T, n_heads, HEAD)

    def body(*refs):
        a_ref, w_ref, b_ref, k_ref, r_ref, vR_ref, saR_ref, dyR_ref, dyT_ref, S_ref = refs[:10]
        x_refs, refs = refs[10:10 + nx], refs[10 + nx:]
        da_ref, dw_ref, db_ref, dk_ref, dr_ref, dvT_ref = refs[:6]
        land_refs, refs = refs[6:6 + nx], refs[6 + nx:]
        ds_ref, dyb_ref, seg_ref = refs[:3]
        if nx:
            start, wait = _exchange_ops([c for _, c in exchange], x_refs, land_refs, *refs[3:])

        @pl.when(pl.program_id(0) == 0)
        def _():
            ds_ref[...] = jnp.zeros_like(ds_ref)
            seg_ref[...] = _seg_mat(LW)
            if nx:
                start()

        seg = seg_ref[...]
        lane = jnp.bitwise_and(lax.broadcasted_iota(jnp.int32, (1, LW), 1), HEAD - 1)
        groups = _lane_groups()
        head_row = lax.broadcasted_iota(jnp.int32, (n_heads, LW), 0)
        lane_head = lax.shift_right_logical(lax.broadcasted_iota(jnp.int32, (n_heads, LW), 1), 6)

        def colsum(z):
            return jnp.sum(z, axis=0, keepdims=True)

        def dysel(t, gsl):
            return jnp.where(lane == t, dyT_ref[0, :, gsl], 0.0)

        for gsl, dyb in zip(groups, _segb_stack([(dysel(C - 1, gsl), 1) for gsl in groups], seg)):
            dyb_ref[:, gsl] = dyb

        def step(i, dvacc):
            t = C - 1 - i
            dybs = [dyb_ref[:, gsl] for gsl in groups]
            dSs = [ds_ref[:, gsl] + dyb * r_ref[pl.ds(t, 1), gsl] for gsl, dyb in zip(groups, dybs)]
            dsabs = _segb_stack([(dS * b_ref[pl.ds(t, 1), gsl], 2) for gsl, dS in zip(groups, dSs)], seg)
            for gsl, dS, dsab in zip(groups, dSs, dsabs):
                ds_ref[:, gsl] = dS * w_ref[pl.ds(t, 1), gsl] + dsab * a_ref[pl.ds(t, 1), gsl]
            out = []
            dy_rows = dyR_ref[t].astype(BF16)
            v_sa_rows = jnp.concatenate([vR_ref[t], saR_ref[t]], axis=0).astype(BF16)
            side = _segb_stack([(x, 1) for gsl, dS in zip(groups, dSs)
                                for x in (dS * k_ref[pl.ds(t, 1), gsl], dysel(t - 1, gsl))], seg)
            for g, (gsl, dva, dS, dsab) in enumerate(zip(groups, dvacc, dSs, dsabs)):
                dvb, dyb_ref[:, gsl] = side[2 * g:2 * g + 2]
                Sp = S_ref[0, t, :, gsl]
                own = head_row == lane_head + g * (LW // HEAD)

                def rows_in(rows, mat):
                    full = jnp.dot(rows, mat.astype(BF16), preferred_element_type=F32)
                    return [jnp.sum(jnp.where(own, full[s:s + n_heads], 0.0), axis=0, keepdims=True)
                            for s in range(0, rows.shape[0], n_heads)]

                (dr,) = rows_in(dy_rows, S_ref[0, t + 1, :, gsl])
                dk, db = rows_in(v_sa_rows, dS)
                dr_ref[pl.ds(t, 1), gsl] = dr
                dk_ref[pl.ds(t, 1), gsl] = dk
                db_ref[pl.ds(t, 1), gsl] = db
                dw_ref[pl.ds(t, 1), gsl] = colsum(dS * Sp)
                da_ref[pl.ds(t, 1), gsl] = colsum(Sp * dsab)
                out.append(jnp.where(lane == t, dvb, dva))
            return tuple(out)

        dvacc = _unrolled_loop(C, step, tuple(jnp.zeros((HEAD, LW), F32) for _ in groups), SCAN_UNROLL_BWD)
        for gsl, dva in zip(groups, dvacc):
            dvT_ref[0, :, gsl] = dva

        if nx:
            @pl.when(pl.program_id(0) == nC - 1)
            def _():
                wait()

    row = pl.BlockSpec((C, RW_WIDTH), lambda c: (nC - 1 - c, 0))
    ht = pl.BlockSpec((1, HEAD, RW_WIDTH), lambda c: (nC - 1 - c, 0, 0))
    hbm = pl.BlockSpec(memory_space=pl.ANY)
    per_head = pl.BlockSpec((C, n_heads, HEAD), lambda c: (nC - 1 - c, 0, 0))
    rows_shape = jax.ShapeDtypeStruct((T, RW_WIDTH), F32)
    res = pl.pallas_call(
        body, name="rwkv_scan_bwd", grid=(nC,),
        in_specs=[row, row, row, row, row, per_head, per_head, per_head, ht,
                  pl.BlockSpec((1, C + 1, HEAD, RW_WIDTH), lambda c: (nC - 1 - c, 0, 0, 0))] + [hbm] * nx,
        out_specs=[row, row, row, row, row, ht] + [hbm] * nx,
        out_shape=[rows_shape] * 5 + [jax.ShapeDtypeStruct((nC, HEAD, RW_WIDTH), F32)] + _exchange_shapes(exchange),
        scratch_shapes=[pltpu.VMEM((HEAD, RW_WIDTH), F32), pltpu.VMEM((HEAD, RW_WIDTH), F32),
                        pltpu.VMEM((LW, LW), BF16)] + (_exchange_sems(nx) if nx else []),
        compiler_params=pltpu.CompilerParams(dimension_semantics=("arbitrary",), vmem_limit_bytes=VMEM_LIMIT_BYTES,
                                             has_side_effects=bool(nx)),
    )(a, w, b, k, r, v_rows, sa_rows, dy_rows, dyT, S_all, *[z for z, _ in exchange])
    return res[:6], res[6:]


def _alibi_slope(h):
    return float(np.float32(2.0 ** (-8.0 * (h + 1) / ATT_HEADS)))


ATT_GROUP_HEADS = 4


def _stack_heads(x, lane_head, fill=0.0):
    return jnp.concatenate([jnp.where(lane_head == hh, x, fill) for hh in range(ATT_GROUP_HEADS)], axis=0)


def _unstack_heads(x, lane_head, L):
    out = jnp.zeros((L, x.shape[1]), F32)
    for hh in range(ATT_GROUP_HEADS):
        out = jnp.where(lane_head == hh, x[hh * L:(hh + 1) * L], out)
    return out


def _att_logits(qs, kcat, gi, d, L, n):
    qi = lax.broadcasted_iota(jnp.int32, (L, 2 * L), 0)
    kj = lax.broadcasted_iota(jnp.int32, (L, 2 * L), 1)
    steps = qi + L - kj
    valid = (steps >= 0) & (steps <= L) & ((kj >= L) | (n > 0))
    dist = (d * steps).astype(F32)
    bias = jnp.concatenate([jnp.where(valid, -_alibi_slope(gi * ATT_GROUP_HEADS + hh) * dist, NEG_BIG)
                            for hh in range(ATT_GROUP_HEADS)], axis=0)
    s = lax.dot_general(qs.astype(BF16), kcat, NT_DIMS, preferred_element_type=F32) * (HEAD ** -0.5)
    return jnp.where(bias > 0.5 * NEG_BIG, s + bias, NEG_BIG)


def att_fwd(pa, gi, T):
    window, d = ATT_GROUPS[gi]
    L = window // d
    Tj = T // d
    nb = Tj // L
    pv = pa.reshape(Tj, d * ATT_COLS)
    nblk = ATT_COLS // ATT_OUT

    def fn(pids, q, kp, kc, vp, vc):
        lane_head = lax.shift_right_logical(lax.broadcasted_iota(jnp.int32, (1, ATT_OUT), 1), 6)
        kcat = jnp.concatenate([kp, kc], axis=0).astype(BF16)
        vcat = jnp.concatenate([vp, vc], axis=0).astype(BF16)
        s = _att_logits(_stack_heads(q, lane_head), kcat, gi, d, L, pids[1])
        m = jnp.max(s, axis=-1, keepdims=True)
        p = jnp.exp(s - m)
        l = jnp.sum(p, axis=-1, keepdims=True)
        o = jnp.dot(p.astype(BF16), vcat, preferred_element_type=F32) / l
        lse = jnp.broadcast_to(m + jnp.log(l), o.shape)
        return _unstack_heads(o, lane_head, L), _unstack_heads(lse, lane_head, L)

    blk = (L, ATT_OUT)
    ins = [(pv, blk, lambda r, n: (n, r * nblk + gi)),
           (pv, blk, lambda r, n: (jnp.maximum(n - 1, 0), r * nblk + 3 + gi)),
           (pv, blk, lambda r, n: (n, r * nblk + 3 + gi)),
           (pv, blk, lambda r, n: (jnp.maximum(n - 1, 0), r * nblk + 6 + gi)),
           (pv, blk, lambda r, n: (n, r * nblk + 6 + gi))]
    out = ((Tj, d * ATT_OUT), F32, blk, lambda r, n: (n, r), None)
    o, lseb = tile_call(f"att_fwd_g{gi}", fn, (d, nb), ins, [out, out])
    return o.reshape(T, ATT_OUT), lseb.reshape(T, ATT_OUT)


def att_bwd(pa, o, lseb, do, dlseb, gi, T):
    window, d = ATT_GROUPS[gi]
    L = window // d
    Tj = T // d
    nb = Tj // L
    pv = pa.reshape(Tj, d * ATT_COLS)
    nblk = ATT_COLS // ATT_OUT
    view = lambda z: z.reshape(Tj, d * ATT_OUT)

    def body(q_ref, kp_ref, kc_ref, vp_ref, vc_ref, o_ref, l_ref, do_ref, dl_ref, dq_ref, dk_ref, dv_ref):
        n = pl.program_id(1)

        @pl.when(n == 0)
        def _():
            dk_ref[...] = jnp.zeros_like(dk_ref)
            dv_ref[...] = jnp.zeros_like(dv_ref)

        lane_head = lax.shift_right_logical(lax.broadcasted_iota(jnp.int32, (1, ATT_OUT), 1), 6)
        kcat = jnp.concatenate([kp_ref[...], kc_ref[...]], axis=0).astype(BF16)
        vcat = jnp.concatenate([vp_ref[...], vc_ref[...]], axis=0).astype(BF16)
        qs = _stack_heads(q_ref[...], lane_head)
        dos = _stack_heads(do_ref[...], lane_head)
        lse = jnp.max(_stack_heads(l_ref[...], lane_head, NEG_BIG), axis=-1, keepdims=True)
        dlse = jnp.sum(_stack_heads(dl_ref[...], lane_head), axis=-1, keepdims=True)
        delta = jnp.sum(dos * jnp.concatenate([o_ref[...]] * ATT_GROUP_HEADS, axis=0), axis=-1, keepdims=True)
        p = jnp.exp(_att_logits(qs, kcat, gi, d, L, n) - lse)
        dp = lax.dot_general(dos.astype(BF16), vcat, NT_DIMS, preferred_element_type=F32)
        ds = (p * (dp - delta + dlse)).astype(BF16)
        dq = _unstack_heads(jnp.dot(ds, kcat, preferred_element_type=F32), lane_head, L)
        dkc = lax.dot_general(ds, qs.astype(BF16), TN_DIMS, preferred_element_type=F32)
        dvc = lax.dot_general(p.astype(BF16), dos.astype(BF16), TN_DIMS, preferred_element_type=F32)
        scale = HEAD ** -0.5
        dq_ref[...] = dq * scale
        cur = pl.ds(pl.multiple_of(n * L, L), L)
        dk_ref[cur, :] += dkc[L:] * scale
        dv_ref[cur, :] += dvc[L:]

        @pl.when(n > 0)
        def _():
            prev = pl.ds(pl.multiple_of((n - 1) * L, L), L)
            dk_ref[prev, :] += dkc[:L] * scale
            dv_ref[prev, :] += dvc[:L]

    blk = pl.BlockSpec((L, ATT_OUT), lambda r, n: (n, r))
    res = pl.BlockSpec((Tj, ATT_OUT), lambda r, n: (0, r))
    qspec = lambda off, prev: pl.BlockSpec(
        (L, ATT_OUT), (lambda r, n: (jnp.maximum(n - 1, 0), r * nblk + off + gi)) if prev
        else (lambda r, n: (n, r * nblk + off + gi)))
    shape = jax.ShapeDtypeStruct((Tj, d * ATT_OUT), F32)
    dq, dk, dv = pl.pallas_call(
        body, name=f"att_bwd_g{gi}", grid=(d, nb),
        in_specs=[qspec(0, False), qspec(3, True), qspec(3, False), qspec(6, True), qspec(6, False),
                  blk, blk, blk, blk],
        out_specs=[blk, res, res],
        out_shape=[shape, shape, shape],
        compiler_params=_cparams(2),
    )(pv, pv, pv, pv, pv, view(o), view(lseb), view(do), view(dlseb))
    return dq.reshape(T, ATT_OUT), dk.reshape(T, ATT_OUT), dv.reshape(T, ATT_OUT)


FFN_TM, FFN_TC = 512, 512


def _conv3(u, prev8, cw, cb):
    return cb + cw[0:1] * u + cw[1:2] * _shift_down(u, prev8, 1) + cw[2:3] * _shift_down(u, prev8, 2)


def conv_glu_fwd(u, conv_w, conv_b):
    T = u.shape[0]
    tm, tc = FFN_TM, FFN_TC
    nj, ni = D_FF // tc, T // tm

    def fn(pids, ug, ugh, uv, uvh, cwg, cbg, cwv, cbv):
        first = pids[1] > 0
        cg = _conv3(ug, jnp.where(first, ugh, 0.0), cwg, cbg)
        cv = _conv3(uv, jnp.where(first, uvh, 0.0), cwv, cbv)
        return _gelu_tanh(cg) * cv

    halo = lambda off: (lambda j, i: (jnp.maximum(i * (tm // 8) - 1, 0), j + off))
    ins = [(u, (tm, tc), lambda j, i: (i, j)), (u, (8, tc), halo(0)),
           (u, (tm, tc), lambda j, i: (i, j + nj)), (u, (8, tc), halo(nj)),
           (conv_w, (3, tc), lambda j, i: (0, j)), (conv_b, (1, tc), lambda j, i: (0, j)),
           (conv_w, (3, tc), lambda j, i: (0, j + nj)), (conv_b, (1, tc), lambda j, i: (0, j + nj))]
    out = ((T, D_FF), BF16, (tm, tc), lambda j, i: (i, j), None)
    return tile_call("conv_glu_fwd", fn, (nj, ni), ins, [out])[0]


def conv_glu_bwd(u, conv_w, conv_b, df):
    T = u.shape[0]
    tm, tc = FFN_TM, FFN_TC
    nj, ni = D_FF // tc, T // tm

    def fn(pids, ug, ugh, uv, uvh, cwg, cbg, cwv, cbv, df_t, nxt_g, nxt_v):
        i = ni - 1 - pids[1]
        ugh = jnp.where(i > 0, ugh, 0.0)
        uvh = jnp.where(i > 0, uvh, 0.0)
        cg = _conv3(ug, ugh, cwg, cbg)
        cv = _conv3(uv, uvh, cwv, cbv)
        _, vjp = jax.vjp(lambda g_, v_: _gelu_tanh(g_) * v_, cg, cv)
        dcg, dcv = vjp(df_t.astype(F32))
        cs = lambda z: jnp.sum(z, axis=0, keepdims=True)

        @pl.when(pids[1] == 0)
        def _():
            nxt_g[...] = jnp.zeros_like(nxt_g)
            nxt_v[...] = jnp.zeros_like(nxt_v)

        outs = []
        for dc, cw, nxt_ref in ((dcg, cwg, nxt_g), (dcv, cwv, nxt_v)):
            nxt = nxt_ref[...]
            outs.append(cw[0:1] * dc + cw[1:2] * _shift_up(dc, nxt, 1) + cw[2:3] * _shift_up(dc, nxt, 2))
            nxt_ref[...] = dc[:8]
        for dc, uu, hh in ((dcg, ug, ugh), (dcv, uv, uvh)):
            outs += [cs(dc * uu), cs(dc * _shift_down(uu, hh, 1)), cs(dc * _shift_down(uu, hh, 2)), cs(dc)]
        return outs

    rows = lambda off: (lambda j, r: (ni - 1 - r, j + off))
    halo = lambda off: (lambda j, r: (jnp.maximum((ni - 1 - r) * (tm // 8) - 1, 0), j + off))
    ins = [(u, (tm, tc), rows(0)), (u, (8, tc), halo(0)),
           (u, (tm, tc), rows(nj)), (u, (8, tc), halo(nj)),
           (conv_w, (3, tc), lambda j, r: (0, j)), (conv_b, (1, tc), lambda j, r: (0, j)),
           (conv_w, (3, tc), lambda j, r: (0, j + nj)), (conv_b, (1, tc), lambda j, r: (0, j + nj)),
           (df, (tm, tc), rows(0))]
    big = ((T, D_FF), BF16, (tm, tc), rows(0), None)
    acc = ((1, D_FF), F32, (1, tc), lambda j, r: (0, j), 1)
    res = tile_call("conv_glu_bwd", fn, (nj, ni), ins, [big, big] + [acc] * 8,
                    scratch=[((8, tc), F32), ((8, tc), F32)])
    dconv_w = jnp.concatenate([jnp.concatenate([res[2 + j], res[6 + j]], axis=1) for j in range(3)], axis=0)
    dconv_b = jnp.concatenate([res[5], res[9]], axis=1)
    return res[0], res[1], dconv_w, dconv_b


def _pad_cols(w, total):
    return jnp.pad(w, ((0, 0), (0, total - w.shape[1])))


def _pad_rows(w, total):
    return jnp.pad(w, ((0, total - w.shape[0]), (0, 0)))


def _proj_pad(w):
    z = lambda n: jnp.zeros((w.shape[0], n), w.dtype)
    return jnp.concatenate([w[:, :1600], z(64), w[:, 1600:1664], z(64), w[:, 1664:1824], z(96), w[:, 1824:],
                            z(PROJ_TAIL)], axis=1)


def _proj_unpad(g):
    return jnp.concatenate([g[:, :1600], g[:, OFF_XA:OFF_XA + 64], g[:, OFF_XG:OFF_XG + 160],
                            g[:, RW_PAD:RW_PAD + ATT_COLS]], axis=1)


def _rw_unpad(g):
    return jnp.concatenate([g[:, :1600], g[:, OFF_XA:OFF_XA + 64], g[:, OFF_XG:OFF_XG + 160]], axis=1)


def rms_fwd(name, x, g, tm=512):
    T, D = x.shape
    return tile_call(name, lambda pid, x_t, g_t: _rms(x_t, g_t), (T // tm,),
                     [_rows(x, tm), _par(g)], [_row_out(T, D, BF16, tm)])[0]


def rms_bwd(name, x, g, dh, dres, with_bf16=True, tm=512):
    T, D = x.shape
    out_dtypes = (F32, BF16) if with_bf16 else (F32,)

    def fn(pid, x_t, g_t, dh_t, dres_t):
        _, vjp = jax.vjp(_rms, x_t, g_t)
        dx, dg = vjp(dh_t.astype(F32))
        return (dres_t + dx,) * len(out_dtypes) + (dg,)

    return tile_call(name, fn, (T // tm,), [_rows(x, tm), _par(g), _rows(dh, tm), _rows(dres, tm)],
                     [_row_out(T, D, dt, tm) for dt in out_dtypes] + [_acc_out(1, D)])


def local_step(x, p, target, W):
    T, D = x.shape
    G = {}

    w_in_p = W["w_in_p"]
    mu_p = _proj_pad(_pad_cols(W["rw_mu"], 4128))[:, :RW_PAD]
    w_up_p = _pad_rows(W["rw_w_up"], 128)
    a_up_p = _pad_rows(W["rw_a_up"], 128)
    g_up_p = _pad_rows(W["rw_g_up"], 256)
    r_k = W["rw_r_k"].reshape(1, RW_WIDTH)
    rw_params = [mu_p, W["rw_w0"], w_up_p, W["rw_a0"], a_up_p, g_up_p, W["rw_k_k"], W["rw_k_a"]]

    h = rms_fwd("rms_mix", x, W["g_mix"])
    proj = matmul("proj_in_rw", h, w_in_p[:, :RW_PAD])
    pa = matmul("proj_in_att", h, w_in_p[:, RW_PAD:RW_PAD + ATT_COLS], out_dtype=BF16)
    gp = matmul("proj_gate", h, W["w_gate"])

    tm = 512
    rw_in = (proj, (tm, RW_PAD), lambda i: (i, 0))
    rw_halo = _prev_halo(proj, tm, RW_PAD)

    def rw_pre_tile(pid, Pc, halo, *params):
        prev8 = jnp.where(pid[0] > 0, halo, 0.0)
        params = [q.astype(F32) for q in params]
        return rw_pre(Pc, _shift_down(Pc, prev8, 1), *params)

    r, decay, k2, v, avec, bvec, g = tile_call(
        "rw_pre", rw_pre_tile, (T // tm,), [rw_in, rw_halo] + [_par(q) for q in rw_params],
        [_row_out(T, RW_WIDTH, F32, tm)] * 7)

    wa, ba, ka = scan_pair_terms(avec, decay, bvec, k2)
    vT = _to_head_time(v).astype(BF16)
    (yT, S_all, saT), late_slots = rwkv_scan_fwd(avec, decay, bvec, k2, r, vT, wa, ba, ka,
                                            exchange=_late_weight_sources(W))
    y = _from_head_time(yT)
    W = dict(W, **_late_weights(late_slots))

    post_params = [W["rw_ln_g"], W["rw_ln_b"], r_k]
    ya = tile_call("rw_post", lambda pid, *t: rw_post(*t), (T // tm,),
                   [_rows(z, tm) for z in (y, r, k2, v, g)] + [_par(q) for q in post_params],
                   [_row_out(T, RW_WIDTH, BF16, tm)])[0]

    att = [att_fwd(pa, gi, T) for gi in range(3)]
    o_l = [att[0][0], att[1][0], att[2][0], att[0][1], att[1][1], att[2][1]]
    yb = tile_call("att_combine", lambda pid, *t: att_combine(*t), (T // tm,),
                   [_rows(z, tm) for z in o_l], [_row_out(T, ATT_OUT, BF16, tm)])[0]

    za = matmul("branch_a", ya, W["w_branch_a"])
    zb = matmul("branch_b", yb, W["w_branch_b"])
    merged = tile_call("merge", lambda pid, *t: merge_fn(*t), (T // tm,),
                       [_rows(gp, tm), _par(W["b_gate"]), _rows(za, tm), _rows(zb, tm)],
                       [_row_out(T, D, BF16, tm)])[0]
    x1 = matmul("mix_out", merged, W["w_out"], res=x)

    h2 = rms_fwd("rms_ffn", x1, W["g_ffn"])
    u = matmul("ffn_up", h2, W["w_up"])
    f = conv_glu_fwd(u, W["conv_w"], W["conv_b"])
    x2 = matmul("ffn_down", f, W["w_down"], res=x1)

    h3 = rms_fwd("rms_ple", x2, W["g_ple"])
    zg = matmul("ple_gate", h3, W["w_ple_gate"])
    pe = matmul("ple_embed", p, W["w_ple"])

    def tail_tile(pid, x2_t, zg_t, pe_t, gf, tgt):
        loss, vjp = jax.vjp(lambda a_, b_, c_, d_: tail_loss(a_, b_, c_, d_, tgt), x2_t, zg_t, pe_t, gf)
        dx2, dzg, dpe, dgf = vjp(jnp.ones((), F32))
        return dx2, dzg, dpe, dgf, jnp.full((1, 128), loss, F32)

    tmt = 512
    dx3, dzg, dpe, dgf, loss_acc = tile_call(
        "tail_loss", tail_tile, (T // tmt,),
        [_rows(x2, tmt), _rows(zg, tmt), _rows(pe, tmt), _par(W["g_final"]), _rows(target, tmt)],
        [_row_out(T, D, F32, tmt), _row_out(T, D, BF16, tmt), _row_out(T, D, BF16, tmt),
         _acc_out(1, D), _acc_out(1, 128)])
    loss = loss_acc[0, 0]
    G["g_final"] = dgf

    wgrad = functools.partial(matmul, mode="tn", out_dtype=GRAD_WIRE)
    G["w_ple"] = wgrad("d_w_ple", p, dpe)
    G["w_ple_gate"] = wgrad("d_w_ple_gate", h3, dzg)
    dh3 = matmul("d_h3", dzg, W["w_ple_gate"], "nt")
    dx2, dx2b, G["g_ple"] = rms_bwd("rms_ple_bwd", x2, W["g_ple"], dh3, dx3)

    G["w_down"] = wgrad("d_w_down", f, dx2b)
    df = matmul("d_f", dx2b, W["w_down"], "nt", out_dtype=BF16)
    du_g, du_v, G["conv_w"], G["conv_b"] = conv_glu_bwd(u, W["conv_w"], W["conv_b"], df)
    du = jnp.concatenate([du_g, du_v], axis=1)
    G["w_up"] = wgrad("d_w_up", h2, du)
    dh2 = matmul("d_h2", du, W["w_up"], "nt")
    dx1, dx1b, G["g_ffn"] = rms_bwd("rms_ffn_bwd", x1, W["g_ffn"], dh2, dx2)

    G["w_out"] = wgrad("d_w_out", merged, dx1b)
    dmerged = matmul("d_merged", dx1b, W["w_out"], "nt", out_dtype=BF16)

    def merge_bwd_tile(pid, gp_t, bg, za_t, zb_t, dm_t):
        _, vjp = jax.vjp(merge_fn, gp_t, bg, za_t, zb_t)
        return vjp(dm_t.astype(F32))

    dgp, G["b_gate"], dza, dzb = tile_call(
        "merge_bwd", merge_bwd_tile, (T // tm,),
        [_rows(gp, tm), _par(W["b_gate"]), _rows(za, tm), _rows(zb, tm), _rows(dmerged, tm)],
        [_row_out(T, 2 * D, BF16, tm), _acc_out(1, 2 * D), _row_out(T, D, BF16, tm), _row_out(T, D, BF16, tm)])
    G["w_branch_a"] = wgrad("d_w_branch_a", ya, dza)
    dya = matmul("d_ya", dza, W["w_branch_a"], "nt")
    G["w_branch_b"] = wgrad("d_w_branch_b", yb, dzb)
    dyb = matmul("d_yb", dzb, W["w_branch_b"], "nt")
    G["w_gate"] = wgrad("d_w_gate", h, dgp)
    dh_gate = matmul("d_h_gate", dgp, W["w_gate"], "nt")

    def comb_bwd_tile(pid, *t):
        _, vjp = jax.vjp(att_combine, *t[:6])
        return vjp(t[6])

    d_ol = tile_call("att_combine_bwd", comb_bwd_tile, (T // tm,),
                     [_rows(z, tm) for z in o_l] + [_rows(dyb, tm)],
                     [_row_out(T, ATT_OUT, F32, tm)] * 6)
    dqkv = [att_bwd(pa, att[gi][0], att[gi][1], d_ol[gi], d_ol[3 + gi], gi, T) for gi in range(3)]
    d_att = [dqkv[gi][j] for j in range(3) for gi in range(3)]

    def post_bwd_tile(pid, *t):
        _, vjp = jax.vjp(rw_post, *t[:8])
        return vjp(t[8])

    dy, dr_p, dk2_p, dv_p, dg, G["rw_ln_g"], G["rw_ln_b"], d_rk = tile_call(
        "rw_post_bwd", post_bwd_tile, (T // tm,),
        [_rows(z, tm) for z in (y, r, k2, v, g)] + [_par(q) for q in post_params] + [_rows(dya, tm)],
        [_row_out(T, RW_WIDTH, F32, tm)] * 5 + [_acc_out(1, RW_WIDTH)] * 3)
    G["rw_r_k"] = d_rk.reshape(W["rw_r_k"].shape)

    (da, dw, db, dk_s, dr_s, dvT), G["_early_parts"] = rwkv_scan_bwd(
        avec, decay, bvec, k2, r, v, dy, S_all, saT, exchange=_early_grad_sources(G))
    dv_s = _from_head_time(dvT)

    tmb = 256
    rw_in_b = (proj, (tmb, RW_PAD), lambda i: (i, 0))

    def pre_bwd_tile(pid, Pc, halo, *t):
        prev8 = jnp.where(pid[0] > 0, halo, 0.0)
        params = [q.astype(F32) for q in t[:8]]
        dr1, dr2, dw_, dk1, dk2_, dv1, dv2, da_, db_, dg_ = t[8:]
        _, vjp = jax.vjp(rw_pre, Pc, _shift_down(Pc, prev8, 1), *params)
        return vjp((dr1 + dr2, dw_, dk1 + dk2_, dv1 + dv2, da_, db_, dg_))

    cts = (dr_s, dr_p, dw, dk_s, dk2_p, dv_s, dv_p, da, db, dg)
    res = tile_call(
        "rw_pre_bwd", pre_bwd_tile, (T // tmb,),
        [rw_in_b, _prev_halo(proj, tmb, RW_PAD)] + [_par(q) for q in rw_params] + [_rows(z, tmb) for z in cts],
        [_row_out(T, RW_PAD, F32, tmb)] * 2 + [_acc_out(*q.shape) for q in rw_params])
    dPc, dPs = res[0], res[1]
    d_mu, G["rw_w0"], d_wup, G["rw_a0"], d_aup, d_gup, G["rw_k_k"], G["rw_k_a"] = res[2:]
    G["rw_mu"] = _rw_unpad(d_mu)
    G["rw_w_up"], G["rw_a_up"], G["rw_g_up"] = d_wup[:64], d_aup[:64], d_gup[:160]

    def dproj_tile(pid, dPc_t, dPs_t, nxt, *att_t):
        nxt = jnp.where(pid[0] < T // tm - 1, nxt, 0.0)
        tail = jnp.zeros((dPc_t.shape[0], PROJ_TAIL), F32)
        return jnp.concatenate([dPc_t + _shift_up(dPs_t, nxt, 1)] + list(att_t) + [tail], axis=1)

    dproj = tile_call("d_proj", dproj_tile, (T // tm,),
                      [_rows(dPc, tm), _rows(dPs, tm), _next_halo(dPs, tm, RW_PAD, T)] + [_rows(z, tm) for z in d_att],
                      [_row_out(T, PROJ_PAD, BF16, tm)])[0]
    G["w_in_p"] = wgrad("d_w_in", h, dproj)
    w_in_srcs = _w_in_grad_sources(G)
    dh = matmul("d_h", dproj, w_in_p, "nt", res=dh_gate, exchange=w_in_srcs)
    if w_in_srcs:
        dh, G["_w_in_parts"] = dh
    dx, G["g_mix"] = rms_bwd("rms_mix_bwd", x, W["g_mix"], dh, dx1, with_bf16=False)
    return loss, dx, G


def _mesh_pos():
    return lax.axis_index("x"), lax.axis_index("y"), lax.axis_index("c")


def _peer(pos, k):
    x, y, c = pos
    px = 1 - x if k & 4 else x
    py = 1 - y if k & 2 else y
    pc = 1 - c if k & 1 else c
    return (px, py, pc), 4 * px + 2 * py + pc


def all_gather_blocks(name, blocks):
    n = len(blocks)

    def body(*refs):
        x_refs, out_refs = refs[:n], refs[n:2 * n]
        send_sems, recv_sems, local_sems = refs[2 * n:]
        x, y, c = _mesh_pos()
        me, sibling = (x, y, c), (x, y, 1 - c)
        chips = [(1 - x, y), (x, 1 - y), (1 - x, 1 - y)]
        ops = range(n)

        def slot(i, px, py, pc):
            return out_refs[i].at[4 * px + 2 * py + pc]

        def copy(k, i, block, to, own=False):
            return pltpu.make_async_remote_copy(
                src_ref=x_refs[i] if own else slot(i, *block), dst_ref=slot(i, *block),
                send_sem=send_sems.at[k, i], recv_sem=recv_sems.at[k, i],
                device_id=to, device_id_type=pl.DeviceIdType.MESH)

        mine = [pltpu.make_async_copy(x_refs[i], slot(i, *me), local_sems.at[i]) for i in ops]
        first = [copy(0, i, me, sibling, own=True) for i in ops]
        first += [copy(1 + j, i, me, (*chip, c), own=True) for j, chip in enumerate(chips) for i in ops]
        for cp in mine + first:
            cp.start()
        passed = []
        for j, chip in enumerate(chips):
            for i in ops:
                copy(1 + j, i, (*chip, c), me).wait_recv()
                passed.append(copy(4 + j, i, (*chip, c), sibling))
                passed[-1].start()
        for i in ops:
            copy(0, i, sibling, me).wait_recv()
        for j, chip in enumerate(chips):
            for i in ops:
                copy(4 + j, i, (*chip, 1 - c), me).wait_recv()
        for cp in first + passed:
            cp.wait_send()
        for cp in mine:
            cp.wait()

    return pl.pallas_call(
        body, name=name,
        in_specs=[pl.BlockSpec(memory_space=pl.ANY)] * n,
        out_specs=[pl.BlockSpec(memory_space=pl.ANY)] * n,
        out_shape=[jax.ShapeDtypeStruct((N_DEV,) + b.shape, b.dtype) for b in blocks],
        scratch_shapes=[pltpu.SemaphoreType.DMA((N_DEV - 1, n)), pltpu.SemaphoreType.DMA((N_DEV - 1, n)),
                        pltpu.SemaphoreType.DMA((n,))],
        compiler_params=pltpu.CompilerParams(has_side_effects=True),
    )(*blocks)


WHOLE = 0


def _exchange_shapes(srcs):
    shapes = [a.shape[1:] if cols is None else a.shape if cols == WHOLE else (a.shape[0], cols) for a, cols in srcs]
    return [jax.ShapeDtypeStruct((N_DEV,) + s, a.dtype) for s, (a, _) in zip(shapes, srcs)]


def _exchange_sems(n):
    return [pltpu.SemaphoreType.DMA((N_DEV - 1, n)), pltpu.SemaphoreType.DMA((N_DEV - 1, n)),
            pltpu.SemaphoreType.DMA((n,))]


def _exchange_ops(col_widths, x_refs, out_refs, send_sems, recv_sems, local_sems):
    n = len(col_widths)
    pos = _mesh_pos()
    me = 4 * pos[0] + 2 * pos[1] + pos[2]

    def piece(i, d):
        cols = col_widths[i]
        if cols is None:
            return x_refs[i].at[d]
        if cols == WHOLE:
            return x_refs[i]
        return x_refs[i].at[:, pl.ds(pl.multiple_of(d * cols, 128), cols)]

    def local(i):
        return pltpu.make_async_copy(piece(i, me), out_refs[i].at[me], local_sems.at[i])

    def remote(k, i, landing):
        peer, idx = _peer(pos, k)
        return pltpu.make_async_remote_copy(
            src_ref=piece(i, idx), dst_ref=out_refs[i].at[idx if landing else me],
            send_sem=send_sems.at[k - 1, i], recv_sem=recv_sems.at[k - 1, i],
            device_id=peer, device_id_type=pl.DeviceIdType.MESH)

    pairs = [(k, i) for k in range(1, N_DEV) for i in range(n)]

    def start():
        for i in range(n):
            local(i).start()
        for k, i in pairs:
            remote(k, i, False).start()

    def wait():
        for k, i in pairs:
            remote(k, i, True).wait_recv()
        for k, i in pairs:
            remote(k, i, False).wait_send()
        for i in range(n):
            local(i).wait()

    return start, wait


def all_to_all_blocks(name, srcs):
    n = len(srcs)

    def body(*refs):
        start, wait = _exchange_ops([c for _, c in srcs], refs[:n], refs[n:2 * n], *refs[2 * n:])
        start()
        wait()

    return pl.pallas_call(
        body, name=name,
        in_specs=[pl.BlockSpec(memory_space=pl.ANY)] * n,
        out_specs=[pl.BlockSpec(memory_space=pl.ANY)] * n,
        out_shape=_exchange_shapes(srcs),
        scratch_shapes=_exchange_sems(n),
        compiler_params=pltpu.CompilerParams(has_side_effects=True),
    )(*[a for a, _ in srcs])


def _adam_row_tile(R, C):
    best = None
    for t in range(16, R + 1, 16):
        if R % t == 0 and t * C <= ADAM_TILE_ELEMS:
            best = t
    return best if best is not None else R


def reduce_adamw(name, parts, w, m, v):
    _, R, C = parts.shape
    tr = _adam_row_tile(R, C)

    def fn(pid, parts_t, w_t, m_t, v_t):
        g = parts_t[0].astype(F32)
        for i in range(1, N_DEV):
            g = g + parts_t[i].astype(F32)
        m_n = ADAM_B1 * m_t + (1.0 - ADAM_B1) * g
        v_n = ADAM_B2 * v_t + (1.0 - ADAM_B2) * (g * g)
        m_hat = m_n / (1.0 - ADAM_B1 ** ADAM_STEP)
        v_hat = v_n / (1.0 - ADAM_B2 ** ADAM_STEP)
        delta = -ADAM_LR * (m_hat / (jnp.sqrt(v_hat) + ADAM_EPS) + ADAM_WD * w_t)
        return g, delta, m_n, v_n

    row = lambda a: (a, (tr, C), lambda i: (i, 0))
    out = ((R, C), F32, (tr, C), lambda i: (i, 0), None)
    return tile_call(name, fn, (R // tr,),
                     [(parts, (N_DEV, tr, C), lambda i: (0, i, 0)), row(w), row(m), row(v)], [out] * 4)


PARAMS = (
    ("g_mix", (1, 1024), None), ("w_in", (1024, 4128), 1), ("rw_mu", (1, 1824), None), ("rw_w0", (1, 512), None),
    ("rw_w_up", (64, 512), 1), ("rw_a0", (1, 512), None), ("rw_a_up", (64, 512), 1), ("rw_g_up", (160, 512), 1),
    ("rw_k_k", (1, 512), None), ("rw_k_a", (1, 512), None), ("rw_r_k", (8, 64), None), ("rw_ln_g", (1, 512), None),
    ("rw_ln_b", (1, 512), None), ("w_branch_a", (512, 1024), 1), ("w_branch_b", (256, 1024), 1),
    ("w_gate", (1024, 2048), 1), ("b_gate", (1, 2048), None), ("w_out", (1024, 1024), 0), ("g_ffn", (1, 1024), None),
    ("w_up", (1024, 6144), 1), ("conv_w", (3, 6144), 1), ("conv_b", (1, 6144), None), ("w_down", (3072, 1024), 0),
    ("g_ple", (1, 1024), None), ("w_ple_gate", (1024, 1024), 0), ("w_ple", (256, 1024), 1), ("g_final", (1, 1024), None),
)
SHARDED = tuple(q for q in PARAMS if q[2] is not None)
REPLICATED = tuple(q for q in PARAMS if q[2] is None)
BIG_NAMES = ("w_in", "w_up", "w_gate", "w_out", "w_down", "w_ple_gate", "w_branch_a", "w_branch_b", "w_ple")
BIG = tuple(q for q in SHARDED if q[0] in BIG_NAMES)
SMALL_SHARDED = tuple(q for q in SHARDED if q[0] not in BIG_NAMES)
PACK_COLS = 1024
F32_GATHERED = ("conv_w",)


def _local_shape(shape, axis):
    s = list(shape)
    s[axis] //= N_DEV
    return tuple(s)


def _numel(shape):
    return int(np.prod(shape))


def _pad_flat(z, mult):
    n = z.shape[-1]
    total = -(-n // mult) * mult
    return jnp.pad(z, [(0, 0)] * (z.ndim - 1) + [(0, total - n)])


def _full_from_slots(slots, shape, axis):
    loc = _local_shape(shape, axis)
    z = slots.reshape((N_DEV,) + loc)
    if axis == 0:
        return z.reshape(shape)
    return z.transpose(1, 0, 2).reshape(shape)


def _slots_from_full(full, shape, axis):
    loc = _local_shape(shape, axis)
    if axis == 0:
        return full.reshape(N_DEV, _numel(loc))
    return full.reshape(shape[0], N_DEV, loc[1]).transpose(1, 0, 2).reshape(N_DEV, _numel(loc))


W_IN_SLOT = 640
W_IN_LOCAL = 4128 // N_DEV


def _block_shape(shape, axis):
    return _local_shape(shape, axis) if axis is not None else shape


def _pad_w_in(block):
    return jnp.pad(block, ((0, 0), (0, W_IN_SLOT - W_IN_LOCAL)))


def _proj_col(s):
    return s + jnp.where(s >= 1600, 64, 0) + jnp.where(s >= 1664, 64, 0) + jnp.where(s >= 1824, 96, 0)


def _perm_tile(d, c0, width):
    j = lax.broadcasted_iota(jnp.int32, (W_IN_SLOT, width), 0)
    c = c0 + lax.broadcasted_iota(jnp.int32, (W_IN_SLOT, width), 1)
    hit = (_proj_col(d * W_IN_LOCAL + j) == c) & (j < W_IN_LOCAL)
    return jnp.where(hit, 1.0, 0.0).astype(BF16)


PERM_TILE = 768


def w_in_unshuffle(slots):
    _, K, _ = slots.shape
    tn = PERM_TILE
    reach = 3

    def first_slot(j):
        return j + jnp.where(j >= 3, 1, 0) + jnp.where(j >= 5, 1, 0)

    def body(a_ref, o_ref, acc_ref):
        j, kk = pl.program_id(0), pl.program_id(1)
        d = first_slot(j) + kk

        @pl.when(kk == 0)
        def _():
            acc_ref[...] = jnp.zeros_like(acc_ref)

        @pl.when(d < N_DEV)
        def _():
            acc_ref[...] += jnp.dot(a_ref[0], _perm_tile(d, j * tn, tn), preferred_element_type=F32)

        @pl.when(kk == reach - 1)
        def _():
            o_ref[...] = acc_ref[...].astype(o_ref.dtype)

    return pl.pallas_call(
        body, name="w_in_unshuffle", grid=(PROJ_PAD // tn, reach),
        in_specs=[pl.BlockSpec((1, K, W_IN_SLOT), lambda j, kk: (jnp.minimum(first_slot(j) + kk, N_DEV - 1), 0, 0))],
        out_specs=pl.BlockSpec((K, tn), lambda j, kk: (0, j)),
        out_shape=jax.ShapeDtypeStruct((K, PROJ_PAD), BF16),
        scratch_shapes=[pltpu.VMEM((K, tn), F32)],
        compiler_params=_cparams(2),
    )(slots)


def w_in_shuffle_grad(dw):
    K = dw.shape[0]
    tk = PERM_TILE

    def first_tile(d):
        return _proj_col(d * W_IN_LOCAL) // tk

    def body(g_ref, o_ref, acc_ref):
        d, kk = pl.program_id(0), pl.program_id(1)
        perm = _perm_tile(d, (first_tile(d) + kk) * tk, tk)
        part = lax.dot_general(g_ref[...].astype(BF16), perm, NT_DIMS, preferred_element_type=F32)

        @pl.when(kk == 0)
        def _():
            acc_ref[...] = part

        @pl.when(kk == 1)
        def _():
            o_ref[0] = (acc_ref[...] + part).astype(o_ref.dtype)

    return pl.pallas_call(
        body, name="w_in_shuffle_grad", grid=(N_DEV, 2),
        in_specs=[pl.BlockSpec((K, tk), lambda d, kk: (0, first_tile(d) + kk))],
        out_specs=pl.BlockSpec((1, K, W_IN_SLOT), lambda d, kk: (d, 0, 0)),
        out_shape=jax.ShapeDtypeStruct((N_DEV, K, W_IN_SLOT), GRAD_WIRE),
        scratch_shapes=[pltpu.VMEM((K, W_IN_SLOT), F32)],
        compiler_params=_cparams(2),
    )(dw)


def _flat_rows(pieces, dtype, row_mult):
    flat = jnp.concatenate([z.astype(dtype) for z in pieces], axis=-1)
    flat = _pad_flat(flat, row_mult * PACK_COLS)
    return flat.reshape(flat.shape[:-1] + (-1, PACK_COLS))


FIRST = tuple(q for q in BIG if q[0] in ("w_in", "w_gate"))
LATE = tuple(q for q in BIG if q not in FIRST)


def _matrix_from_slots(slots, shape, axis):
    return slots.reshape(shape) if axis == 0 else slots.transpose(1, 0, 2).reshape(shape)


def _late_weight_sources(W):
    return [(blk, WHOLE) for blk in W["_late_blocks"]]


def _late_weights(slots):
    return {n: _matrix_from_slots(s, shape, axis) for (n, shape, axis), s in zip(LATE, slots)}


def gather_weights(local):
    blocks = [(_pad_w_in(local[n]) if n == "w_in" else local[n]).astype(BF16) for n, _, _ in FIRST]
    small = [q for q in SMALL_SHARDED if q[0] not in F32_GATHERED]
    exact = [q for q in SMALL_SHARDED if q[0] in F32_GATHERED]
    blocks.append(_flat_rows([local[n].reshape(-1) for n, _, _ in small], BF16, 16))
    blocks.append(_flat_rows([local[n].reshape(-1) for n, _, _ in exact], F32, 8))
    got = all_gather_blocks("weight_all_gather", blocks)
    full = {"_late_blocks": [local[n].astype(BF16) for n, _, _ in LATE]}
    for (n, shape, axis), slots in zip(FIRST, got):
        if n == "w_in":
            full["w_in_p"] = w_in_unshuffle(slots)
        else:
            full[n] = _matrix_from_slots(slots, shape, axis)
    for group, slots in ((small, got[-2]), (exact, got[-1])):
        slots, off = slots.reshape(N_DEV, -1), 0
        for n, shape, axis in group:
            size = _numel(_local_shape(shape, axis))
            full[n] = _full_from_slots(slots[:, off:off + size], shape, axis)
            off += size
    for n, _, _ in REPLICATED:
        full[n] = local[n]
    return full


LOSS_SLOT = ("_loss", (1, 2), None)
PACKED_SMALL = SMALL_SHARDED + REPLICATED + (LOSS_SLOT,)


def _pack_small(vals):
    pieces = [vals[n].reshape(-1) if n in vals else jnp.zeros((_numel(shape),), F32) for n, shape, _ in PACKED_SMALL]
    return _flat_rows(pieces, F32, 16)


def _unpack_small(packed):
    flat, out, off = packed.reshape(-1), {}, 0
    for n, shape, axis in PACKED_SMALL:
        loc = _block_shape(shape, axis)
        out[n] = flat[off:off + _numel(loc)].reshape(loc)
        off += _numel(loc)
    return out


EARLY = tuple(q for q in BIG if q[0] != "w_in")


def _early_grad_sources(G):
    srcs = []
    for n, shape, axis in EARLY:
        if axis == 0:
            srcs.append((G[n].astype(GRAD_WIRE).reshape((N_DEV,) + _local_shape(shape, axis)), None))
        else:
            srcs.append((G[n].astype(GRAD_WIRE), shape[1] // N_DEV))
    return srcs


def _w_in_grad_sources(G):
    return [(w_in_shuffle_grad(G["w_in_p"]), None)]


def _closing_grad_sources(G, loss_local):
    srcs = []
    rows = [_slots_from_full(G[n].reshape(shape), shape, axis) for n, shape, axis in SMALL_SHARDED]
    loss_hi = loss_local.astype(GRAD_WIRE).astype(F32)
    rep = jnp.concatenate([G[n].reshape(-1) for n, _, _ in REPLICATED] + [jnp.stack([loss_hi, loss_local - loss_hi])])
    rows.append(jnp.broadcast_to(rep[None, :], (N_DEV, rep.shape[0])))
    srcs.append((_flat_rows(rows, GRAD_WIRE, 16), None))
    return srcs


def _step(x, p, target, local_w, local_m, local_v):
    full = gather_weights(local_w)
    loss_local, dx, G = local_step(x, p, target, full)
    closing = all_to_all_blocks("grad_all_to_all", _closing_grad_sources(G, loss_local))
    parts = list(G["_w_in_parts"]) + list(G["_early_parts"]) + list(closing)
    outs = [{}, {}, {}, {}]
    for (n, shape, axis), part in zip((BIG[0],) + EARLY, parts):
        prep = _pad_w_in if n == "w_in" else (lambda z: z)
        res = reduce_adamw("adamw_" + n, part, prep(local_w[n]), prep(local_m[n]), prep(local_v[n]))
        for o, z in zip(outs, res):
            o[n] = z[:, :W_IN_LOCAL] if n == "w_in" else z
    res = reduce_adamw("adamw_small", parts[-1], _pack_small(local_w), _pack_small(local_m), _pack_small(local_v))
    for o, z in zip(outs, res):
        o.update(_unpack_small(z))
    loss = jnp.sum(outs[0]["_loss"])
    return loss, dx, outs


def kernel(x, p, g_mix, w_in, rw_mu, rw_w0, rw_w_up, rw_a0, rw_a_up, rw_g_up, rw_k_k, rw_k_a, rw_r_k, rw_ln_g, rw_ln_b, w_branch_a, w_branch_b, w_gate, b_gate, w_out, g_ffn, w_up, conv_w, conv_b, w_down, g_ple, w_ple_gate, w_ple, g_final, loss_target, m_g_mix, m_w_in, m_rw_mu, m_rw_w0, m_rw_w_up, m_rw_a0, m_rw_a_up, m_rw_g_up, m_rw_k_k, m_rw_k_a, m_rw_r_k, m_rw_ln_g, m_rw_ln_b, m_w_branch_a, m_w_branch_b, m_w_gate, m_b_gate, m_w_out, m_g_ffn, m_w_up, m_conv_w, m_conv_b, m_w_down, m_g_ple, m_w_ple_gate, m_w_ple, m_g_final, v_g_mix, v_w_in, v_rw_mu, v_rw_w0, v_rw_w_up, v_rw_a0, v_rw_a_up, v_rw_g_up, v_rw_k_k, v_rw_k_a, v_rw_r_k, v_rw_ln_g, v_rw_ln_b, v_w_branch_a, v_w_branch_b, v_w_gate, v_b_gate, v_w_out, v_g_ffn, v_w_up, v_conv_w, v_conv_b, v_w_down, v_g_ple, v_w_ple_gate, v_w_ple, v_g_final):
    args = dict(locals())
    names = [n for n, _, _ in PARAMS]
    orig_shape = {n: args[n].shape for n in names}

    def strip(prefix):
        out = {}
        for n, shape, axis in PARAMS:
            a = args[prefix + n]
            loc = _local_shape(shape, axis) if axis is not None else shape
            out[n] = a.reshape(loc)
        return out

    local_w, local_m, local_v = strip(""), strip("m_"), strip("v_")
    T, D = x.shape[-2], x.shape[-1]
    loss, dx, (g, delta, m_n, v_n) = _step(x.reshape(T, D), p.reshape(T, p.shape[-1]), loss_target.reshape(T, D),
                                           local_w, local_m, local_v)
    outs = [loss, dx.reshape(x.shape)]
    for group in (g, delta, m_n, v_n):
        outs += [group[n].reshape(orig_shape[n]) for n in names]
    return tuple(outs)
```

```python
import functools

import numpy as np
import jax
import jax.numpy as jnp
from jax import lax
from jax.experimental import pallas as pl
from jax.experimental.pallas import tpu as pltpu

F32 = jnp.float32
BF16 = jnp.bfloat16
GRAD_WIRE = jnp.bfloat16

N_DEV = 8
NORM_EPS = 1e-6
RW_LN_EPS = 64e-5
HEAD = 64
RW_WIDTH = 512
ATT_GROUPS = ((128, 1), (512, 4), (2048, 16))
ATT_HEADS = 12
ATT_OUT = 256
ATT_COLS = 2304
OFF_XW, OFF_XA, OFF_XG, RW_PAD, PROJ_PAD = 1536, 1664, 1792, 2048, 4608
PROJ_TAIL = PROJ_PAD - RW_PAD - ATT_COLS
D_FF = 3072

ADAM_LR, ADAM_B1, ADAM_B2, ADAM_EPS, ADAM_WD, ADAM_STEP = 0.001, 0.9, 0.999, 1e-08, 0.01, 10

VMEM_LIMIT_BYTES = 56 * 1024 * 1024
ADAM_TILE_ELEMS = 256 * 1024
NEG_BIG = -1e30

NT_DIMS = (((1,), (1,)), ((), ()))
TN_DIMS = (((0,), (0,)), ((), ()))
NN_DIMS = (((1,), (0,)), ((), ()))


def _cparams(n_axes):
    return pltpu.CompilerParams(dimension_semantics=("arbitrary",) * n_axes,
                                vmem_limit_bytes=VMEM_LIMIT_BYTES)


def _split2(x):
    hi = x.astype(BF16)
    lo = (x - hi.astype(F32)).astype(BF16)
    return hi, lo


def _seg_mat(n):
    r = lax.shift_right_logical(lax.broadcasted_iota(jnp.int32, (n, n), 0), 6)
    c = lax.shift_right_logical(lax.broadcasted_iota(jnp.int32, (n, n), 1), 6)
    return jnp.where(r == c, 1.0, 0.0).astype(BF16)


def _segb(x, seg):
    return _segb_stack([(x, 2)], seg)[0]


def _segb_stack(items, seg):
    rows = items[0][0].shape[0]
    parts = []
    for x, passes in items:
        parts += list(_split2(x)) if passes == 2 else [x.astype(BF16)]
    res = jnp.dot(jnp.concatenate(parts, axis=0), seg, preferred_element_type=F32)
    out, at = [], 0
    for _, passes in items:
        piece = res[at * rows:(at + 1) * rows]
        if passes == 2:
            piece = piece + res[(at + 1) * rows:(at + 2) * rows]
        out.append(piece)
        at += passes
    return out


def _segb1(x, seg):
    return jnp.dot(x.astype(BF16), seg, preferred_element_type=F32)


@jax.custom_vjp
def segsum(x):
    return _segb(x, _seg_mat(x.shape[1]))


def _segsum_fwd(x):
    return segsum(x), None


def _segsum_bwd(_, ct):
    return (segsum(ct),)


segsum.defvjp(_segsum_fwd, _segsum_bwd)


@jax.custom_vjp
def bdot(a, b):
    return jnp.dot(a.astype(BF16), b.astype(BF16), preferred_element_type=F32)


def _bdot_fwd(a, b):
    return bdot(a, b), (a, b)


def _bdot_bwd(res, ct):
    a, b = res
    ctb = ct.astype(BF16)
    da = lax.dot_general(ctb, b.astype(BF16), NT_DIMS, preferred_element_type=F32)
    db = lax.dot_general(a.astype(BF16), ctb, TN_DIMS, preferred_element_type=F32)
    return da.astype(a.dtype), db.astype(b.dtype)


bdot.defvjp(_bdot_fwd, _bdot_bwd)


def _sig(x):
    return 1.0 / (1.0 + jnp.exp(-x))


def _softplus(z):
    return jnp.maximum(z, 0.0) + jnp.log(1.0 + jnp.exp(-jnp.abs(z)))


def _gelu_tanh(x):
    return 0.5 * x * (1.0 + jnp.tanh(0.7978845608028654 * (x + 0.044715 * (x * x * x))))


def _rms(x, g):
    return x * lax.rsqrt(jnp.mean(x * x, axis=-1, keepdims=True) + NORM_EPS) * g


def _shift_down(x, prev8, n):
    rolled = pltpu.roll(x, n, 0)
    top = pltpu.roll(prev8, n, 0)
    rid = lax.broadcasted_iota(jnp.int32, (8, x.shape[1]), 0)
    head = jnp.where(rid < n, top, rolled[:8])
    return jnp.concatenate([head, rolled[8:]], axis=0)


def _shift_up(x, next8, n):
    rows = x.shape[0]
    rolled = pltpu.roll(x, rows - n, 0)
    bottom = pltpu.roll(next8, 8 - n, 0)
    rid = lax.broadcasted_iota(jnp.int32, (8, x.shape[1]), 0)
    tail = jnp.where(rid >= 8 - n, bottom, rolled[rows - 8:])
    return jnp.concatenate([rolled[:rows - 8], tail], axis=0)


def tile_call(name, fn, grid, ins, outs, scratch=()):
    n_in, n_out = len(ins), len(outs)
    acc_axes = [o[4] for o in outs]

    def body(*refs):
        pids = tuple(pl.program_id(a) for a in range(len(grid)))
        vals = fn(pids, *[r[...] for r in refs[:n_in]], *refs[n_in + n_out:])
        if not isinstance(vals, (tuple, list)):
            vals = (vals,)
        for o_ref, val, ax in zip(refs[n_in:n_in + n_out], vals, acc_axes):
            if ax is None:
                o_ref[...] = val.astype(o_ref.dtype)
            else:
                @pl.when(pids[ax] == 0)
                def _(o_ref=o_ref):
                    o_ref[...] = jnp.zeros_like(o_ref)

                o_ref[...] += val.astype(o_ref.dtype)

    res = pl.pallas_call(
        body, name=name, grid=grid,
        in_specs=[pl.BlockSpec(b, im) for _, b, im in ins],
        out_specs=[pl.BlockSpec(o[2], o[3]) for o in outs],
        out_shape=[jax.ShapeDtypeStruct(o[0], o[1]) for o in outs],
        scratch_shapes=[pltpu.VMEM(s, d) for s, d in scratch],
        compiler_params=_cparams(len(grid)),
    )(*[a for a, _, _ in ins])
    return res


def _rows(a, tm):
    return (a, (tm, a.shape[1]), lambda i: (i, 0))


def _par(a):
    return (a, a.shape, lambda i: (0, 0))


def _row_out(T, C, dtype, tm):
    return ((T, C), dtype, (tm, C), lambda i: (i, 0), None)


def _acc_out(R, C):
    return ((R, C), F32, (R, C), lambda i: (0, 0), 0)


def _prev_halo(a, tm, C):
    return (a, (8, C), lambda i: (jnp.maximum(i * (tm // 8) - 1, 0), 0))


def _next_halo(a, tm, C, T):
    return (a, (8, C), lambda i: (jnp.minimum((i + 1) * (tm // 8), T // 8 - 1), 0))


def _pick(n, target):
    for t in (target, 2048, 1536, 1024, 768, 512, 384, 256, 128):
        if t <= target and n % t == 0:
            return t
    return n


def matmul(name, a, b, mode="nn", res=None, out_dtype=F32, tm=1024, tn=2048, tk=2048, exchange=()):
    if mode == "nn":
        (M, K), (K2, N) = a.shape, b.shape
    elif mode == "tn":
        (K, M), (K2, N) = a.shape, b.shape
    else:
        (M, K), (N, K2) = a.shape, b.shape
    assert K == K2, (name, a.shape, b.shape, mode)
    tm, tn, tk = _pick(M, tm), _pick(N, tn), _pick(K, tk)
    nk = K // tk
    dims = {"nn": NN_DIMS, "tn": TN_DIMS, "nt": NT_DIMS}[mode]
    a_spec = {"nn": pl.BlockSpec((tm, tk), lambda i, j, k: (i, k)),
              "tn": pl.BlockSpec((tk, tm), lambda i, j, k: (k, i)),
              "nt": pl.BlockSpec((tm, tk), lambda i, j, k: (i, k))}[mode]
    b_spec = {"nn": pl.BlockSpec((tk, tn), lambda i, j, k: (k, j)),
              "tn": pl.BlockSpec((tk, tn), lambda i, j, k: (k, j)),
              "nt": pl.BlockSpec((tn, tk), lambda i, j, k: (j, k))}[mode]
    has_res = res is not None
    nx = len(exchange)
    grid = (M // tm, N // tn, nk)

    def body(*refs):
        a_ref, b_ref = refs[:2]
        r_ref = refs[2] if has_res else None
        refs = refs[2 + has_res:]
        x_refs, o_ref, land_refs, acc_ref = refs[:nx], refs[nx], refs[nx + 1:2 * nx + 1], refs[2 * nx + 1]
        k = pl.program_id(2)
        if nx:
            step = (pl.program_id(0) * grid[1] + pl.program_id(1)) * nk + k
            start, wait = _exchange_ops([c for _, c in exchange], x_refs, land_refs, *refs[2 * nx + 2:])

            @pl.when(step == 0)
            def _():
                start()

        @pl.when(k == 0)
        def _():
            acc_ref[...] = jnp.zeros_like(acc_ref)

        acc_ref[...] += lax.dot_general(a_ref[...].astype(BF16), b_ref[...].astype(BF16), dims,
                                        preferred_element_type=F32)

        @pl.when(k == nk - 1)
        def _():
            out = acc_ref[...]
            if has_res:
                out = out + r_ref[...].astype(F32)
            o_ref[...] = out.astype(o_ref.dtype)

        if nx:
            @pl.when(step == grid[0] * grid[1] * nk - 1)
            def _():
                wait()

    in_specs = [a_spec, b_spec]
    args = [a, b]
    if has_res:
        in_specs.append(pl.BlockSpec((tm, tn), lambda i, j, k: (i, j)))
        args.append(res)
    hbm = pl.BlockSpec(memory_space=pl.ANY)
    out = pl.pallas_call(
        body, name=name, grid=grid,
        in_specs=in_specs + [hbm] * nx,
        out_specs=[pl.BlockSpec((tm, tn), lambda i, j, k: (i, j))] + [hbm] * nx,
        out_shape=[jax.ShapeDtypeStruct((M, N), out_dtype)] + _exchange_shapes(exchange),
        scratch_shapes=[pltpu.VMEM((tm, tn), F32)] + (_exchange_sems(nx) if nx else []),
        compiler_params=pltpu.CompilerParams(dimension_semantics=("arbitrary",) * 3, vmem_limit_bytes=VMEM_LIMIT_BYTES,
                                             has_side_effects=bool(nx)),
    )(*args, *[z for z, _ in exchange])
    return (out[0], out[1:]) if nx else out[0]


def rw_pre(Pc, Ps, mu, w0, w_up, a0, a_up, g_up, k_k, k_a):
    Pm = Pc + (Ps - Pc) * mu
    r, k, v = Pm[:, 0:512], Pm[:, 512:1024], Pm[:, 1024:1536]
    xw, xa, xg = Pm[:, OFF_XW:OFF_XA], Pm[:, OFF_XA:OFF_XG], Pm[:, OFF_XG:RW_PAD]
    w = -_softplus(-(w0 + bdot(jnp.tanh(xw), w_up))) - 0.5
    decay = jnp.exp(-jnp.exp(w))
    a = _sig(a0 + bdot(xa, a_up))
    g = bdot(_sig(xg), g_up)
    kk = k * k_k
    kk = kk / jnp.maximum(jnp.sqrt(segsum(kk * kk)), 1e-12)
    k2 = k * (1.0 + (a - 1.0) * k_a)
    return r, decay, k2, v, -kk, kk * a, g


def rw_post(y, r, k2, v, g, ln_g, ln_b, r_k):
    mean = segsum(y) * (1.0 / HEAD)
    d = y - mean
    var = segsum(d * d) * (1.0 / HEAD)
    yn = d * lax.rsqrt(var + RW_LN_EPS) * ln_g + ln_b
    bonus = segsum(r * k2 * r_k) * v
    return (yn + bonus) * g


def att_combine(o1, o2, o3, l1, l2, l3):
    m = jnp.maximum(jnp.maximum(l1, l2), l3)
    e1, e2, e3 = jnp.exp(l1 - m), jnp.exp(l2 - m), jnp.exp(l3 - m)
    return (e1 * o1 + e2 * o2 + e3 * o3) / (e1 + e2 + e3)


def merge_fn(gp, bg, za, zb):
    s = _sig(gp + bg)
    half = za.shape[1]
    return s[:, :half] * za + s[:, half:] * zb


def tail_loss(x2, zg, pe, g_final, target):
    x3 = x2 + _sig(zg) * pe
    y = _rms(x3, g_final)
    err = (y - target) * (y - target)
    return 0.5 * jnp.sum(jnp.mean(err, axis=-1, keepdims=True))


SCAN_CHUNK = HEAD
SCAN_LANES = 256
SCAN_UNROLL_FWD, SCAN_UNROLL_BWD = 32, 32


def _to_head_time(z):
    T = z.shape[0]
    return z.reshape(T // HEAD, HEAD, RW_WIDTH // HEAD, HEAD).transpose(0, 3, 2, 1).reshape(T // HEAD, HEAD, RW_WIDTH)


def _from_head_time(zt):
    C = zt.shape[0]
    return zt.reshape(C, HEAD, RW_WIDTH // HEAD, HEAD).transpose(0, 3, 2, 1).reshape(C * HEAD, RW_WIDTH)


def _unrolled_loop(n, step, init, unroll):
    def body(i, carry):
        for j in range(unroll):
            carry = step(i * unroll + j, carry)
        return carry

    return lax.fori_loop(0, n // unroll, body, init)


def _lane_groups():
    return [slice(j * SCAN_LANES, (j + 1) * SCAN_LANES) for j in range(RW_WIDTH // SCAN_LANES)]


def scan_pair_terms(a, w, b, k, tm=512):
    T = a.shape[0]

    def fn(pid, a_t, nxt, w_t, b_t, k_t):
        a_next = _shift_up(a_t, jnp.where(pid[0] < T // tm - 1, nxt, 0.0), 1)
        return w_t * a_next, segsum(b_t * a_next), segsum(k_t * a_next)

    return tile_call("scan_pair_terms", fn, (T // tm,),
                     [_rows(a, tm), _next_halo(a, tm, RW_WIDTH, T), _rows(w, tm), _rows(b, tm), _rows(k, tm)],
                     [_row_out(T, RW_WIDTH, F32, tm)] * 3)


def rwkv_scan_fwd(a, w, b, k, r, vT, wa, ba, ka, exchange=()):
    T = a.shape[0]
    C, LW = SCAN_CHUNK, SCAN_LANES
    nC = T // C
    nx = len(exchange)

    def body(*refs):
        a_ref, w_ref, b_ref, k_ref, r_ref, vT_ref, wa_ref, ba_ref, ka_ref = refs[:9]
        x_refs, refs = refs[9:9 + nx], refs[9 + nx:]
        yT_ref, S_ref, saT_ref = refs[:3]
        land_refs, refs = refs[3:3 + nx], refs[3 + nx:]
        st_ref, vb0_ref, vb1_ref, seg_ref = refs[:4]
        if nx:
            start, wait = _exchange_ops([c for _, c in exchange], x_refs, land_refs, *refs[4:])

        @pl.when(pl.program_id(0) == 0)
        def _():
            st_ref[...] = jnp.zeros_like(st_ref)
            seg_ref[...] = _seg_mat(LW)
            if nx:
                start()

        seg = seg_ref[...]
        lane = jnp.bitwise_and(lax.broadcasted_iota(jnp.int32, (1, LW), 1), HEAD - 1)
        groups = _lane_groups()

        def vsel(t, gsl):
            return jnp.where(lane == t, vT_ref[0, :, gsl], 0.0)

        first = _segb_stack([(vsel(s, gsl), 1) for gsl in groups for s in (0, 1)], seg)
        for g, gsl in enumerate(groups):
            vb0_ref[:, gsl] = first[2 * g]
            vb1_ref[:, gsl] = first[2 * g + 1]
        saT_ref[...] = jnp.zeros_like(saT_ref)

        def pair(i, yacc):
            t = 2 * i
            t1 = t + 1
            tp = jnp.maximum(t - 1, 0)
            row = lambda ref, s, gsl: ref[pl.ds(s, 1), gsl]
            Sps = [st_ref[:, gsl] for gsl in groups]
            chain = _segb_stack([(Sp * row(ref, t, gsl), 2) for gsl, Sp in zip(groups, Sps) for ref in (a_ref, wa_ref)],
                                seg)
            sas, us = chain[0::2], chain[1::2]
            S1s = []
            for gsl, Sp, sa, u in zip(groups, Sps, sas, us):
                vb0, vb1 = vb0_ref[:, gsl], vb1_ref[:, gsl]
                S1 = Sp * row(w_ref, t, gsl) + sa * row(b_ref, t, gsl) + vb0 * row(k_ref, t, gsl)
                sa1 = u + sa * row(ba_ref, t, gsl) + vb0 * row(ka_ref, t, gsl)
                st_ref[:, gsl] = S1 * row(w_ref, t1, gsl) + sa1 * row(b_ref, t1, gsl) + vb1 * row(k_ref, t1, gsl)
                S_ref[0, t, :, gsl] = Sp
                S_ref[0, t1, :, gsl] = S1
                S1s.append(S1)
                saT_ref[0, :, gsl] = jnp.where(lane == t, sa, jnp.where(lane == t1, sa1, saT_ref[0, :, gsl]))
            side = _segb_stack([(x, 1) for gsl, Sp, S1 in zip(groups, Sps, S1s)
                                for x in (Sp * row(r_ref, tp, gsl), S1 * row(r_ref, t, gsl),
                                          vsel(t + 2, gsl), vsel(t + 3, gsl))], seg)
            out = []
            for g, (gsl, ya) in enumerate(zip(groups, yacc)):
                yb0, yb1, vb0_ref[:, gsl], vb1_ref[:, gsl] = side[4 * g:4 * g + 4]
                out.append(jnp.where(lane == t, yb1, jnp.where(lane == t - 1, yb0, ya)))
            return tuple(out)

        yacc = _unrolled_loop(C // 2, pair, tuple(jnp.zeros((HEAD, LW), F32) for _ in groups), SCAN_UNROLL_FWD)
        for gsl, ya in zip(groups, yacc):
            S_last = st_ref[:, gsl]
            S_ref[0, C, :, gsl] = S_last
            yb = _segb1(S_last * r_ref[pl.ds(C - 1, 1), gsl], seg)
            yT_ref[0, :, gsl] = jnp.where(lane == C - 1, yb, ya)

        if nx:
            @pl.when(pl.program_id(0) == nC - 1)
            def _():
                wait()

    row = pl.BlockSpec((C, RW_WIDTH), lambda c: (c, 0))
    ht = pl.BlockSpec((1, HEAD, RW_WIDTH), lambda c: (c, 0, 0))
    hbm = pl.BlockSpec(memory_space=pl.ANY)
    res = pl.pallas_call(
        body, name="rwkv_scan_fwd", grid=(nC,),
        in_specs=[row, row, row, row, row, ht, row, row, row] + [hbm] * nx,
        out_specs=[ht, pl.BlockSpec((1, C + 1, HEAD, RW_WIDTH), lambda c: (c, 0, 0, 0)), ht] + [hbm] * nx,
        out_shape=[jax.ShapeDtypeStruct((nC, HEAD, RW_WIDTH), F32),
                   jax.ShapeDtypeStruct((nC, C + 1, HEAD, RW_WIDTH), F32),
                   jax.ShapeDtypeStruct((nC, HEAD, RW_WIDTH), F32)] + _exchange_shapes(exchange),
        scratch_shapes=[pltpu.VMEM((HEAD, RW_WIDTH), F32)] * 3 + [pltpu.VMEM((LW, LW), BF16)]
        + (_exchange_sems(nx) if nx else []),
        compiler_params=pltpu.CompilerParams(dimension_semantics=("arbitrary",), vmem_limit_bytes=VMEM_LIMIT_BYTES,
                                             has_side_effects=bool(nx)),
    )(a, w, b, k, r, vT, wa, ba, ka, *[z for z, _ in exchange])
    return res[:3], res[3:]


def rwkv_scan_bwd(a, w, b, k, r, v, dy, S_all, saT, exchange=()):
    T = a.shape[0]
    C, LW = SCAN_CHUNK, SCAN_LANES
    nC = T // C
    nx = len(exchange)
    n_heads = RW_WIDTH // HEAD
    dyT = _to_head_time(dy).astype(BF16)
    v_rows, dy_rows = v.reshape(T, n_heads, HEAD), dy.reshape(T, n_heads, HEAD)
    sa_rows = _from_head_time(saT).reshape(T, n_heads, HEAD)

    def body(*refs):
        a_ref, w_ref, b_ref, k_ref, r_ref, vR_ref, saR_ref, dyR_ref, dyT_ref, S_ref = refs[:10]
        x_refs, refs = refs[10:10 + nx], refs[10 + nx:]
        da_ref, dw_ref, db_ref, dk_ref, dr_ref, dvT_ref = refs[:6]
        land_refs, refs = refs[6:6 + nx], refs[6 + nx:]
        ds_ref, dyb_ref, seg_ref = refs[:3]
        if nx:
            start, wait = _exchange_ops([c for _, c in exchange], x_refs, land_refs, *refs[3:])

        @pl.when(pl.program_id(0) == 0)
        def _():
            ds_ref[...] = jnp.zeros_like(ds_ref)
            seg_ref[...] = _seg_mat(LW)
            if nx:
                start()

        seg = seg_ref[...]
        lane = jnp.bitwise_and(lax.broadcasted_iota(jnp.int32, (1, LW), 1), HEAD - 1)
        groups = _lane_groups()
        head_row = lax.broadcasted_iota(jnp.int32, (n_heads, LW), 0)
        lane_head = lax.shift_right_logical(lax.broadcasted_iota(jnp.int32, (n_heads, LW), 1), 6)

        def colsum(z):
            return jnp.sum(z, axis=0, keepdims=True)

        def dysel(t, gsl):
            return jnp.where(lane == t, dyT_ref[0, :, gsl], 0.0)

        for gsl, dyb in zip(groups, _segb_stack([(dysel(C - 1, gsl), 1) for gsl in groups], seg)):
            dyb_ref[:, gsl] = dyb

        def step(i, dvacc):
            t = C - 1 - i
            dybs = [dyb_ref[:, gsl] for gsl in groups]
            dSs = [ds_ref[:, gsl] + dyb * r_ref[pl.ds(t, 1), gsl] for gsl, dyb in zip(groups, dybs)]
            dsabs = _segb_stack([(dS * b_ref[pl.ds(t, 1), gsl], 2) for gsl, dS in zip(groups, dSs)], seg)
            for gsl, dS, dsab in zip(groups, dSs, dsabs):
                ds_ref[:, gsl] = dS * w_ref[pl.ds(t, 1), gsl] + dsab * a_ref[pl.ds(t, 1), gsl]
            out = []
            dy_rows = dyR_ref[t].astype(BF16)
            v_sa_rows = jnp.concatenate([vR_ref[t], saR_ref[t]], axis=0).astype(BF16)
            side = _segb_stack([(x, 1) for gsl, dS in zip(groups, dSs)
                                for x in (dS * k_ref[pl.ds(t, 1), gsl], dysel(t - 1, gsl))], seg)
            for g, (gsl, dva, dS, dsab) in enumerate(zip(groups, dvacc, dSs, dsabs)):
                dvb, dyb_ref[:, gsl] = side[2 * g:2 * g + 2]
                Sp = S_ref[0, t, :, gsl]
                own = head_row == lane_head + g * (LW // HEAD)

                def rows_in(rows, mat):
                    full = jnp.dot(rows, mat.astype(BF16), preferred_element_type=F32)
                    return [jnp.sum(jnp.where(own, full[s:s + n_heads], 0.0), axis=0, keepdims=True)
                            for s in range(0, rows.shape[0], n_heads)]

                (dr,) = rows_in(dy_rows, S_ref[0, t + 1, :, gsl])
                dk, db = rows_in(v_sa_rows, dS)
                dr_ref[pl.ds(t, 1), gsl] = dr
                dk_ref[pl.ds(t, 1), gsl] = dk
                db_ref[pl.ds(t, 1), gsl] = db
                dw_ref[pl.ds(t, 1), gsl] = colsum(dS * Sp)
                da_ref[pl.ds(t, 1), gsl] = colsum(Sp * dsab)
                out.append(jnp.where(lane == t, dvb, dva))
            return tuple(out)

        dvacc = _unrolled_loop(C, step, tuple(jnp.zeros((HEAD, LW), F32) for _ in groups), SCAN_UNROLL_BWD)
        for gsl, dva in zip(groups, dvacc):
            dvT_ref[0, :, gsl] = dva

        if nx:
            @pl.when(pl.program_id(0) == nC - 1)
            def _():
                wait()

    row = pl.BlockSpec((C, RW_WIDTH), lambda c: (nC - 1 - c, 0))
    ht = pl.BlockSpec((1, HEAD, RW_WIDTH), lambda c: (nC - 1 - c, 0, 0))
    hbm = pl.BlockSpec(memory_space=pl.ANY)
    per_head = pl.BlockSpec((C, n_heads, HEAD), lambda c: (nC - 1 - c, 0, 0))
    rows_shape = jax.ShapeDtypeStruct((T, RW_WIDTH), F32)
    res = pl.pallas_call(
        body, name="rwkv_scan_bwd", grid=(nC,),
        in_specs=[row, row, row, row, row, per_head, per_head, per_head, ht,
                  pl.BlockSpec((1, C + 1, HEAD, RW_WIDTH), lambda c: (nC - 1 - c, 0, 0, 0))] + [hbm] * nx,
        out_specs=[row, row, row, row, row, ht] + [hbm] * nx,
        out_shape=[rows_shape] * 5 + [jax.ShapeDtypeStruct((nC, HEAD, RW_WIDTH), F32)] + _exchange_shapes(exchange),
        scratch_shapes=[pltpu.VMEM((HEAD, RW_WIDTH), F32), pltpu.VMEM((HEAD, RW_WIDTH), F32),
                        pltpu.VMEM((LW, LW), BF16)] + (_exchange_sems(nx) if nx else []),
        compiler_params=pltpu.CompilerParams(dimension_semantics=("arbitrary",), vmem_limit_bytes=VMEM_LIMIT_BYTES,
                                             has_side_effects=bool(nx)),
    )(a, w, b, k, r, v_rows, sa_rows, dy_rows, dyT, S_all, *[z for z, _ in exchange])
    return res[:6], res[6:]


def _alibi_slope(h):
    return float(np.float32(2.0 ** (-8.0 * (h + 1) / ATT_HEADS)))


ATT_GROUP_HEADS = 4


def _stack_heads(x, lane_head, fill=0.0):
    return jnp.concatenate([jnp.where(lane_head == hh, x, fill) for hh in range(ATT_GROUP_HEADS)], axis=0)


def _unstack_heads(x, lane_head, L):
    out = jnp.zeros((L, x.shape[1]), F32)
    for hh in range(ATT_GROUP_HEADS):
        out = jnp.where(lane_head == hh, x[hh * L:(hh + 1) * L], out)
    return out


def _att_logits(qs, kcat, gi, d, L, n):
    qi = lax.broadcasted_iota(jnp.int32, (L, 2 * L), 0)
    kj = lax.broadcasted_iota(jnp.int32, (L, 2 * L), 1)
    steps = qi + L - kj
    valid = (steps >= 0) & (steps <= L) & ((kj >= L) | (n > 0))
    dist = (d * steps).astype(F32)
    bias = jnp.concatenate([jnp.where(valid, -_alibi_slope(gi * ATT_GROUP_HEADS + hh) * dist, NEG_BIG)
                            for hh in range(ATT_GROUP_HEADS)], axis=0)
    s = lax.dot_general(qs.astype(BF16), kcat, NT_DIMS, preferred_element_type=F32) * (HEAD ** -0.5)
    return jnp.where(bias > 0.5 * NEG_BIG, s + bias, NEG_BIG)


def att_fwd(pa, gi, T):
    window, d = ATT_GROUPS[gi]
    L = window // d
    Tj = T // d
    nb = Tj // L
    pv = pa.reshape(Tj, d * ATT_COLS)
    nblk = ATT_COLS // ATT_OUT

    def fn(pids, q, kp, kc, vp, vc):
        lane_head = lax.shift_right_logical(lax.broadcasted_iota(jnp.int32, (1, ATT_OUT), 1), 6)
        kcat = jnp.concatenate([kp, kc], axis=0).astype(BF16)
        vcat = jnp.concatenate([vp, vc], axis=0).astype(BF16)
        s = _att_logits(_stack_heads(q, lane_head), kcat, gi, d, L, pids[1])
        m = jnp.max(s, axis=-1, keepdims=True)
        p = jnp.exp(s - m)
        l = jnp.sum(p, axis=-1, keepdims=True)
        o = jnp.dot(p.astype(BF16), vcat, preferred_element_type=F32) / l
        lse = jnp.broadcast_to(m + jnp.log(l), o.shape)
        return _unstack_heads(o, lane_head, L), _unstack_heads(lse, lane_head, L)

    blk = (L, ATT_OUT)
    ins = [(pv, blk, lambda r, n: (n, r * nblk + gi)),
           (pv, blk, lambda r, n: (jnp.maximum(n - 1, 0), r * nblk + 3 + gi)),
           (pv, blk, lambda r, n: (n, r * nblk + 3 + gi)),
           (pv, blk, lambda r, n: (jnp.maximum(n - 1, 0), r * nblk + 6 + gi)),
           (pv, blk, lambda r, n: (n, r * nblk + 6 + gi))]
    out = ((Tj, d * ATT_OUT), F32, blk, lambda r, n: (n, r), None)
    o, lseb = tile_call(f"att_fwd_g{gi}", fn, (d, nb), ins, [out, out])
    return o.reshape(T, ATT_OUT), lseb.reshape(T, ATT_OUT)


def att_bwd(pa, o, lseb, do, dlseb, gi, T):
    window, d = ATT_GROUPS[gi]
    L = window // d
    Tj = T // d
    nb = Tj // L
    pv = pa.reshape(Tj, d * ATT_COLS)
    nblk = ATT_COLS // ATT_OUT
    view = lambda z: z.reshape(Tj, d * ATT_OUT)

    def body(q_ref, kp_ref, kc_ref, vp_ref, vc_ref, o_ref, l_ref, do_ref, dl_ref, dq_ref, dk_ref, dv_ref):
        n = pl.program_id(1)

        @pl.when(n == 0)
        def _():
            dk_ref[...] = jnp.zeros_like(dk_ref)
            dv_ref[...] = jnp.zeros_like(dv_ref)

        lane_head = lax.shift_right_logical(lax.broadcasted_iota(jnp.int32, (1, ATT_OUT), 1), 6)
        kcat = jnp.concatenate([kp_ref[...], kc_ref[...]], axis=0).astype(BF16)
        vcat = jnp.concatenate([vp_ref[...], vc_ref[...]], axis=0).astype(BF16)
        qs = _stack_heads(q_ref[...], lane_head)
        dos = _stack_heads(do_ref[...], lane_head)
        lse = jnp.max(_stack_heads(l_ref[...], lane_head, NEG_BIG), axis=-1, keepdims=True)
        dlse = jnp.sum(_stack_heads(dl_ref[...], lane_head), axis=-1, keepdims=True)
        delta = jnp.sum(dos * jnp.concatenate([o_ref[...]] * ATT_GROUP_HEADS, axis=0), axis=-1, keepdims=True)
        p = jnp.exp(_att_logits(qs, kcat, gi, d, L, n) - lse)
        dp = lax.dot_general(dos.astype(BF16), vcat, NT_DIMS, preferred_element_type=F32)
        ds = (p * (dp - delta + dlse)).astype(BF16)
        dq = _unstack_heads(jnp.dot(ds, kcat, preferred_element_type=F32), lane_head, L)
        dkc = lax.dot_general(ds, qs.astype(BF16), TN_DIMS, preferred_element_type=F32)
        dvc = lax.dot_general(p.astype(BF16), dos.astype(BF16), TN_DIMS, preferred_element_type=F32)
        scale = HEAD ** -0.5
        dq_ref[...] = (dq * scale).astype(dq_ref.dtype)
        cur = pl.ds(pl.multiple_of(n * L, L), L)
        dk_ref[cur, :] += dkc[L:] * scale
        dv_ref[cur, :] += dvc[L:]

        @pl.when(n > 0)
        def _():
            prev = pl.ds(pl.multiple_of((n - 1) * L, L), L)
            dk_ref[prev, :] += dkc[:L] * scale
            dv_ref[prev, :] += dvc[:L]

    blk = pl.BlockSpec((L, ATT_OUT), lambda r, n: (n, r))
    res = pl.BlockSpec((Tj, ATT_OUT), lambda r, n: (0, r))
    qspec = lambda off, prev: pl.BlockSpec(
        (L, ATT_OUT), (lambda r, n: (jnp.maximum(n - 1, 0), r * nblk + off + gi)) if prev
        else (lambda r, n: (n, r * nblk + off + gi)))
    shape = jax.ShapeDtypeStruct((Tj, d * ATT_OUT), F32)
    dq, dk, dv = pl.pallas_call(
        body, name=f"att_bwd_g{gi}", grid=(d, nb),
        in_specs=[qspec(0, False), qspec(3, True), qspec(3, False), qspec(6, True), qspec(6, False),
                  blk, blk, blk, blk],
        out_specs=[blk, res, res],
        out_shape=[jax.ShapeDtypeStruct(shape.shape, BF16), shape, shape],
        compiler_params=_cparams(2),
    )(pv, pv, pv, pv, pv, view(o), view(lseb), view(do), view(dlseb))
    return dq.reshape(T, ATT_OUT), dk.reshape(T, ATT_OUT), dv.reshape(T, ATT_OUT)


FFN_TM, FFN_TC = 512, 512


def _conv3(u, prev8, cw, cb):
    return cb + cw[0:1] * u + cw[1:2] * _shift_down(u, prev8, 1) + cw[2:3] * _shift_down(u, prev8, 2)


def conv_glu_fwd(u, conv_w, conv_b):
    T = u.shape[0]
    tm, tc = FFN_TM, FFN_TC
    nj, ni = D_FF // tc, T // tm

    def fn(pids, ug, ugh, uv, uvh, cwg, cbg, cwv, cbv):
        first = pids[1] > 0
        cg = _conv3(ug, jnp.where(first, ugh, 0.0), cwg, cbg)
        cv = _conv3(uv, jnp.where(first, uvh, 0.0), cwv, cbv)
        return _gelu_tanh(cg) * cv

    halo = lambda off: (lambda j, i: (jnp.maximum(i * (tm // 8) - 1, 0), j + off))
    ins = [(u, (tm, tc), lambda j, i: (i, j)), (u, (8, tc), halo(0)),
           (u, (tm, tc), lambda j, i: (i, j + nj)), (u, (8, tc), halo(nj)),
           (conv_w, (3, tc), lambda j, i: (0, j)), (conv_b, (1, tc), lambda j, i: (0, j)),
           (conv_w, (3, tc), lambda j, i: (0, j + nj)), (conv_b, (1, tc), lambda j, i: (0, j + nj))]
    out = ((T, D_FF), BF16, (tm, tc), lambda j, i: (i, j), None)
    return tile_call("conv_glu_fwd", fn, (nj, ni), ins, [out])[0]


def conv_glu_bwd(u, conv_w, conv_b, df):
    T = u.shape[0]
    tm, tc = FFN_TM, FFN_TC
    nj, ni = D_FF // tc, T // tm

    def fn(pids, ug, ugh, uv, uvh, cwg, cbg, cwv, cbv, df_t, nxt_g, nxt_v):
        i = ni - 1 - pids[1]
        ugh = jnp.where(i > 0, ugh, 0.0)
        uvh = jnp.where(i > 0, uvh, 0.0)
        cg = _conv3(ug, ugh, cwg, cbg)
        cv = _conv3(uv, uvh, cwv, cbv)
        _, vjp = jax.vjp(lambda g_, v_: _gelu_tanh(g_) * v_, cg, cv)
        dcg, dcv = vjp(df_t.astype(F32))
        cs = lambda z: jnp.sum(z, axis=0, keepdims=True)

        @pl.when(pids[1] == 0)
        def _():
            nxt_g[...] = jnp.zeros_like(nxt_g)
            nxt_v[...] = jnp.zeros_like(nxt_v)

        outs = []
        for dc, cw, nxt_ref in ((dcg, cwg, nxt_g), (dcv, cwv, nxt_v)):
            nxt = nxt_ref[...]
            outs.append(cw[0:1] * dc + cw[1:2] * _shift_up(dc, nxt, 1) + cw[2:3] * _shift_up(dc, nxt, 2))
            nxt_ref[...] = dc[:8]
        for dc, uu, hh in ((dcg, ug, ugh), (dcv, uv, uvh)):
            outs += [cs(dc * uu), cs(dc * _shift_down(uu, hh, 1)), cs(dc * _shift_down(uu, hh, 2)), cs(dc)]
        return outs

    rows = lambda off: (lambda j, r: (ni - 1 - r, j + off))
    halo = lambda off: (lambda j, r: (jnp.maximum((ni - 1 - r) * (tm // 8) - 1, 0), j + off))
    ins = [(u, (tm, tc), rows(0)), (u, (8, tc), halo(0)),
           (u, (tm, tc), rows(nj)), (u, (8, tc), halo(nj)),
           (conv_w, (3, tc), lambda j, r: (0, j)), (conv_b, (1, tc), lambda j, r: (0, j)),
           (conv_w, (3, tc), lambda j, r: (0, j + nj)), (conv_b, (1, tc), lambda j, r: (0, j + nj)),
           (df, (tm, tc), rows(0))]
    big = ((T, D_FF), BF16, (tm, tc), rows(0), None)
    acc = ((1, D_FF), F32, (1, tc), lambda j, r: (0, j), 1)
    res = tile_call("conv_glu_bwd", fn, (nj, ni), ins, [big, big] + [acc] * 8,
                    scratch=[((8, tc), F32), ((8, tc), F32)])
    dconv_w = jnp.concatenate([jnp.concatenate([res[2 + j], res[6 + j]], axis=1) for j in range(3)], axis=0)
    dconv_b = jnp.concatenate([res[5], res[9]], axis=1)
    return res[0], res[1], dconv_w, dconv_b


def _pad_cols(w, total):
    return jnp.pad(w, ((0, 0), (0, total - w.shape[1])))


def _pad_rows(w, total):
    return jnp.pad(w, ((0, total - w.shape[0]), (0, 0)))


def _proj_pad(w):
    z = lambda n: jnp.zeros((w.shape[0], n), w.dtype)
    return jnp.concatenate([w[:, :1600], z(64), w[:, 1600:1664], z(64), w[:, 1664:1824], z(96), w[:, 1824:],
                            z(PROJ_TAIL)], axis=1)


def _proj_unpad(g):
    return jnp.concatenate([g[:, :1600], g[:, OFF_XA:OFF_XA + 64], g[:, OFF_XG:OFF_XG + 160],
                            g[:, RW_PAD:RW_PAD + ATT_COLS]], axis=1)


def _rw_unpad(g):
    return jnp.concatenate([g[:, :1600], g[:, OFF_XA:OFF_XA + 64], g[:, OFF_XG:OFF_XG + 160]], axis=1)


def rms_fwd(name, x, g, tm=512):
    T, D = x.shape
    return tile_call(name, lambda pid, x_t, g_t: _rms(x_t, g_t), (T // tm,),
                     [_rows(x, tm), _par(g)], [_row_out(T, D, BF16, tm)])[0]


def rms_bwd(name, x, g, dh, dres, with_bf16=True, tm=512):
    T, D = x.shape
    out_dtypes = (F32, BF16) if with_bf16 else (F32,)

    def fn(pid, x_t, g_t, dh_t, dres_t):
        _, vjp = jax.vjp(_rms, x_t, g_t)
        dx, dg = vjp(dh_t.astype(F32))
        return (dres_t + dx,) * len(out_dtypes) + (dg,)

    return tile_call(name, fn, (T // tm,), [_rows(x, tm), _par(g), _rows(dh, tm), _rows(dres, tm)],
                     [_row_out(T, D, dt, tm) for dt in out_dtypes] + [_acc_out(1, D)])


def local_step(x, p, target, W):
    T, D = x.shape
    G = {}

    w_in_p = W["w_in_p"]
    mu_p = _proj_pad(_pad_cols(W["rw_mu"], 4128))[:, :RW_PAD]
    w_up_p = _pad_rows(W["rw_w_up"], 128)
    a_up_p = _pad_rows(W["rw_a_up"], 128)
    g_up_p = _pad_rows(W["rw_g_up"], 256)
    r_k = W["rw_r_k"].reshape(1, RW_WIDTH)
    rw_params = [mu_p, W["rw_w0"], w_up_p, W["rw_a0"], a_up_p, g_up_p, W["rw_k_k"], W["rw_k_a"]]

    h = rms_fwd("rms_mix", x, W["g_mix"])
    proj = matmul("proj_in_rw", h, w_in_p[:, :RW_PAD])
    pa = matmul("proj_in_att", h, w_in_p[:, RW_PAD:RW_PAD + ATT_COLS], out_dtype=BF16)
    gp = matmul("proj_gate", h, W["w_gate"])

    tm = 512
    rw_in = (proj, (tm, RW_PAD), lambda i: (i, 0))
    rw_halo = _prev_halo(proj, tm, RW_PAD)

    def rw_pre_tile(pid, Pc, halo, *params):
        prev8 = jnp.where(pid[0] > 0, halo, 0.0)
        params = [q.astype(F32) for q in params]
        return rw_pre(Pc, _shift_down(Pc, prev8, 1), *params)

    r, decay, k2, v, avec, bvec, g = tile_call(
        "rw_pre", rw_pre_tile, (T // tm,), [rw_in, rw_halo] + [_par(q) for q in rw_params],
        [_row_out(T, RW_WIDTH, F32, tm)] * 7)

    wa, ba, ka = scan_pair_terms(avec, decay, bvec, k2)
    vT = _to_head_time(v).astype(BF16)
    (yT, S_all, saT), late_slots = rwkv_scan_fwd(avec, decay, bvec, k2, r, vT, wa, ba, ka,
                                            exchange=_late_weight_sources(W))
    y = _from_head_time(yT)
    W = dict(W, **_late_weights(late_slots))

    post_params = [W["rw_ln_g"], W["rw_ln_b"], r_k]
    ya = tile_call("rw_post", lambda pid, *t: rw_post(*t), (T // tm,),
                   [_rows(z, tm) for z in (y, r, k2, v, g)] + [_par(q) for q in post_params],
                   [_row_out(T, RW_WIDTH, BF16, tm)])[0]

    att = [att_fwd(pa, gi, T) for gi in range(3)]
    o_l = [att[0][0], att[1][0], att[2][0], att[0][1], att[1][1], att[2][1]]
    yb = tile_call("att_combine", lambda pid, *t: att_combine(*t), (T // tm,),
                   [_rows(z, tm) for z in o_l], [_row_out(T, ATT_OUT, BF16, tm)])[0]

    za = matmul("branch_a", ya, W["w_branch_a"])
    zb = matmul("branch_b", yb, W["w_branch_b"])
    merged = tile_call("merge", lambda pid, *t: merge_fn(*t), (T // tm,),
                       [_rows(gp, tm), _par(W["b_gate"]), _rows(za, tm), _rows(zb, tm)],
                       [_row_out(T, D, BF16, tm)])[0]
    x1 = matmul("mix_out", merged, W["w_out"], res=x)

    h2 = rms_fwd("rms_ffn", x1, W["g_ffn"])
    u = matmul("ffn_up", h2, W["w_up"])
    f = conv_glu_fwd(u, W["conv_w"], W["conv_b"])
    x2 = matmul("ffn_down", f, W["w_down"], res=x1)

    h3 = rms_fwd("rms_ple", x2, W["g_ple"])
    zg = matmul("ple_gate", h3, W["w_ple_gate"])
    pe = matmul("ple_embed", p, W["w_ple"])

    def tail_tile(pid, x2_t, zg_t, pe_t, gf, tgt):
        loss, vjp = jax.vjp(lambda a_, b_, c_, d_: tail_loss(a_, b_, c_, d_, tgt), x2_t, zg_t, pe_t, gf)
        dx2, dzg, dpe, dgf = vjp(jnp.ones((), F32))
        return dx2, dzg, dpe, dgf, jnp.full((1, 128), loss, F32)

    tmt = 512
    dx3, dzg, dpe, dgf, loss_acc = tile_call(
        "tail_loss", tail_tile, (T // tmt,),
        [_rows(x2, tmt), _rows(zg, tmt), _rows(pe, tmt), _par(W["g_final"]), _rows(target, tmt)],
        [_row_out(T, D, F32, tmt), _row_out(T, D, BF16, tmt), _row_out(T, D, BF16, tmt),
         _acc_out(1, D), _acc_out(1, 128)])
    loss = loss_acc[0, 0]
    G["g_final"] = dgf

    wgrad = functools.partial(matmul, mode="tn", out_dtype=GRAD_WIRE)
    G["w_ple"] = wgrad("d_w_ple", p, dpe)
    G["w_ple_gate"] = wgrad("d_w_ple_gate", h3, dzg)
    dh3 = matmul("d_h3", dzg, W["w_ple_gate"], "nt")
    dx2, dx2b, G["g_ple"] = rms_bwd("rms_ple_bwd", x2, W["g_ple"], dh3, dx3)

    G["w_down"] = wgrad("d_w_down", f, dx2b)
    df = matmul("d_f", dx2b, W["w_down"], "nt", out_dtype=BF16)
    du_g, du_v, G["conv_w"], G["conv_b"] = conv_glu_bwd(u, W["conv_w"], W["conv_b"], df)
    du = jnp.concatenate([du_g, du_v], axis=1)
    G["w_up"] = wgrad("d_w_up", h2, du)
    dh2 = matmul("d_h2", du, W["w_up"], "nt")
    dx1, dx1b, G["g_ffn"] = rms_bwd("rms_ffn_bwd", x1, W["g_ffn"], dh2, dx2)

    G["w_out"] = wgrad("d_w_out", merged, dx1b)
    dmerged = matmul("d_merged", dx1b, W["w_out"], "nt", out_dtype=BF16)

    def merge_bwd_tile(pid, gp_t, bg, za_t, zb_t, dm_t):
        _, vjp = jax.vjp(merge_fn, gp_t, bg, za_t, zb_t)
        return vjp(dm_t.astype(F32))

    dgp, G["b_gate"], dza, dzb = tile_call(
        "merge_bwd", merge_bwd_tile, (T // tm,),
        [_rows(gp, tm), _par(W["b_gate"]), _rows(za, tm), _rows(zb, tm), _rows(dmerged, tm)],
        [_row_out(T, 2 * D, BF16, tm), _acc_out(1, 2 * D), _row_out(T, D, BF16, tm), _row_out(T, D, BF16, tm)])
    G["w_branch_a"] = wgrad("d_w_branch_a", ya, dza)
    dya = matmul("d_ya", dza, W["w_branch_a"], "nt")
    G["w_branch_b"] = wgrad("d_w_branch_b", yb, dzb)
    dyb = matmul("d_yb", dzb, W["w_branch_b"], "nt")
    G["w_gate"] = wgrad("d_w_gate", h, dgp)
    dh_gate = matmul("d_h_gate", dgp, W["w_gate"], "nt")

    def comb_bwd_tile(pid, *t):
        _, vjp = jax.vjp(att_combine, *t[:6])
        return vjp(t[6])

    d_ol = tile_call("att_combine_bwd", comb_bwd_tile, (T // tm,),
                     [_rows(z, tm) for z in o_l] + [_rows(dyb, tm)],
                     [_row_out(T, ATT_OUT, F32, tm)] * 6)
    dqkv = [att_bwd(pa, att[gi][0], att[gi][1], d_ol[gi], d_ol[3 + gi], gi, T) for gi in range(3)]
    d_att = [dqkv[gi][j] for j in range(3) for gi in range(3)]

    def post_bwd_tile(pid, *t):
        _, vjp = jax.vjp(rw_post, *t[:8])
        return vjp(t[8])

    dy, dr_p, dk2_p, dv_p, dg, G["rw_ln_g"], G["rw_ln_b"], d_rk = tile_call(
        "rw_post_bwd", post_bwd_tile, (T // tm,),
        [_rows(z, tm) for z in (y, r, k2, v, g)] + [_par(q) for q in post_params] + [_rows(dya, tm)],
        [_row_out(T, RW_WIDTH, F32, tm)] * 5 + [_acc_out(1, RW_WIDTH)] * 3)
    G["rw_r_k"] = d_rk.reshape(W["rw_r_k"].shape)

    (da, dw, db, dk_s, dr_s, dvT), G["_early_parts"] = rwkv_scan_bwd(
        avec, decay, bvec, k2, r, v, dy, S_all, saT, exchange=_early_grad_sources(G))
    dv_s = _from_head_time(dvT)

    tmb = 256
    rw_in_b = (proj, (tmb, RW_PAD), lambda i: (i, 0))

    def pre_bwd_tile(pid, Pc, halo, *t):
        prev8 = jnp.where(pid[0] > 0, halo, 0.0)
        params = [q.astype(F32) for q in t[:8]]
        dr1, dr2, dw_, dk1, dk2_, dv1, dv2, da_, db_, dg_ = t[8:]
        _, vjp = jax.vjp(rw_pre, Pc, _shift_down(Pc, prev8, 1), *params)
        return vjp((dr1 + dr2, dw_, dk1 + dk2_, dv1 + dv2, da_, db_, dg_))

    cts = (dr_s, dr_p, dw, dk_s, dk2_p, dv_s, dv_p, da, db, dg)
    res = tile_call(
        "rw_pre_bwd", pre_bwd_tile, (T // tmb,),
        [rw_in_b, _prev_halo(proj, tmb, RW_PAD)] + [_par(q) for q in rw_params] + [_rows(z, tmb) for z in cts],
        [_row_out(T, RW_PAD, F32, tmb)] * 2 + [_acc_out(*q.shape) for q in rw_params])
    dPc, dPs = res[0], res[1]
    d_mu, G["rw_w0"], d_wup, G["rw_a0"], d_aup, d_gup, G["rw_k_k"], G["rw_k_a"] = res[2:]
    G["rw_mu"] = _rw_unpad(d_mu)
    G["rw_w_up"], G["rw_a_up"], G["rw_g_up"] = d_wup[:64], d_aup[:64], d_gup[:160]

    def dproj_tile(pid, dPc_t, dPs_t, nxt, *att_t):
        nxt = jnp.where(pid[0] < T // tm - 1, nxt, 0.0)
        tail = jnp.zeros((dPc_t.shape[0], PROJ_TAIL), F32)
        return jnp.concatenate([dPc_t + _shift_up(dPs_t, nxt, 1)] + list(att_t) + [tail], axis=1)

    dproj = tile_call("d_proj", dproj_tile, (T // tm,),
                      [_rows(dPc, tm), _rows(dPs, tm), _next_halo(dPs, tm, RW_PAD, T)] + [_rows(z, tm) for z in d_att],
                      [_row_out(T, PROJ_PAD, BF16, tm)])[0]
    G["w_in_p"] = wgrad("d_w_in", h, dproj)
    w_in_srcs = _w_in_grad_sources(G)
    dh = matmul("d_h", dproj, w_in_p, "nt", res=dh_gate, exchange=w_in_srcs)
    if w_in_srcs:
        dh, G["_w_in_parts"] = dh
    dx, G["g_mix"] = rms_bwd("rms_mix_bwd", x, W["g_mix"], dh, dx1, with_bf16=False)
    return loss, dx, G


def _mesh_pos():
    return lax.axis_index("x"), lax.axis_index("y"), lax.axis_index("c")


def _peer(pos, k):
    x, y, c = pos
    px = 1 - x if k & 4 else x
    py = 1 - y if k & 2 else y
    pc = 1 - c if k & 1 else c
    return (px, py, pc), 4 * px + 2 * py + pc


def all_gather_blocks(name, blocks):
    n = len(blocks)

    def body(*refs):
        x_refs, out_refs = refs[:n], refs[n:2 * n]
        send_sems, recv_sems, local_sems = refs[2 * n:]
        x, y, c = _mesh_pos()
        me, sibling = (x, y, c), (x, y, 1 - c)
        chips = [(1 - x, y), (x, 1 - y), (1 - x, 1 - y)]
        ops = range(n)

        def slot(i, px, py, pc):
            return out_refs[i].at[4 * px + 2 * py + pc]

        def copy(k, i, block, to, own=False):
            return pltpu.make_async_remote_copy(
                src_ref=x_refs[i] if own else slot(i, *block), dst_ref=slot(i, *block),
                send_sem=send_sems.at[k, i], recv_sem=recv_sems.at[k, i],
                device_id=to, device_id_type=pl.DeviceIdType.MESH)

        mine = [pltpu.make_async_copy(x_refs[i], slot(i, *me), local_sems.at[i]) for i in ops]
        first = [copy(0, i, me, sibling, own=True) for i in ops]
        first += [copy(1 + j, i, me, (*chip, c), own=True) for j, chip in enumerate(chips) for i in ops]
        for cp in mine + first:
            cp.start()
        passed = []
        for j, chip in enumerate(chips):
            for i in ops:
                copy(1 + j, i, (*chip, c), me).wait_recv()
                passed.append(copy(4 + j, i, (*chip, c), sibling))
                passed[-1].start()
        for i in ops:
            copy(0, i, sibling, me).wait_recv()
        for j, chip in enumerate(chips):
            for i in ops:
                copy(4 + j, i, (*chip, 1 - c), me).wait_recv()
        for cp in first + passed:
            cp.wait_send()
        for cp in mine:
            cp.wait()

    return pl.pallas_call(
        body, name=name,
        in_specs=[pl.BlockSpec(memory_space=pl.ANY)] * n,
        out_specs=[pl.BlockSpec(memory_space=pl.ANY)] * n,
        out_shape=[jax.ShapeDtypeStruct((N_DEV,) + b.shape, b.dtype) for b in blocks],
        scratch_shapes=[pltpu.SemaphoreType.DMA((N_DEV - 1, n)), pltpu.SemaphoreType.DMA((N_DEV - 1, n)),
                        pltpu.SemaphoreType.DMA((n,))],
        compiler_params=pltpu.CompilerParams(has_side_effects=True),
    )(*blocks)


WHOLE = 0


def _exchange_shapes(srcs):
    shapes = [a.shape[1:] if cols is None else a.shape if cols == WHOLE else (a.shape[0], cols) for a, cols in srcs]
    return [jax.ShapeDtypeStruct((N_DEV,) + s, a.dtype) for s, (a, _) in zip(shapes, srcs)]


def _exchange_sems(n):
    return [pltpu.SemaphoreType.DMA((N_DEV - 1, n)), pltpu.SemaphoreType.DMA((N_DEV - 1, n)),
            pltpu.SemaphoreType.DMA((n,))]


def _exchange_ops(col_widths, x_refs, out_refs, send_sems, recv_sems, local_sems):
    n = len(col_widths)
    pos = _mesh_pos()
    me = 4 * pos[0] + 2 * pos[1] + pos[2]

    def piece(i, d):
        cols = col_widths[i]
        if cols is None:
            return x_refs[i].at[d]
        if cols == WHOLE:
            return x_refs[i]
        return x_refs[i].at[:, pl.ds(pl.multiple_of(d * cols, 128), cols)]

    def local(i):
        return pltpu.make_async_copy(piece(i, me), out_refs[i].at[me], local_sems.at[i])

    def remote(k, i, landing):
        peer, idx = _peer(pos, k)
        return pltpu.make_async_remote_copy(
            src_ref=piece(i, idx), dst_ref=out_refs[i].at[idx if landing else me],
            send_sem=send_sems.at[k - 1, i], recv_sem=recv_sems.at[k - 1, i],
            device_id=peer, device_id_type=pl.DeviceIdType.MESH)

    pairs = [(k, i) for k in range(1, N_DEV) for i in range(n)]

    def start():
        for i in range(n):
            local(i).start()
        for k, i in pairs:
            remote(k, i, False).start()

    def wait():
        for k, i in pairs:
            remote(k, i, True).wait_recv()
        for k, i in pairs:
            remote(k, i, False).wait_send()
        for i in range(n):
            local(i).wait()

    return start, wait


def all_to_all_blocks(name, srcs):
    n = len(srcs)

    def body(*refs):
        start, wait = _exchange_ops([c for _, c in srcs], refs[:n], refs[n:2 * n], *refs[2 * n:])
        start()
        wait()

    return pl.pallas_call(
        body, name=name,
        in_specs=[pl.BlockSpec(memory_space=pl.ANY)] * n,
        out_specs=[pl.BlockSpec(memory_space=pl.ANY)] * n,
        out_shape=_exchange_shapes(srcs),
        scratch_shapes=_exchange_sems(n),
        compiler_params=pltpu.CompilerParams(has_side_effects=True),
    )(*[a for a, _ in srcs])


def _adam_row_tile(R, C):
    best = None
    for t in range(16, R + 1, 16):
        if R % t == 0 and t * C <= ADAM_TILE_ELEMS:
            best = t
    return best if best is not None else R


def reduce_adamw(name, parts, w, m, v):
    _, R, C = parts.shape
    tr = _adam_row_tile(R, C)

    def fn(pid, parts_t, w_t, m_t, v_t):
        g = parts_t[0].astype(F32)
        for i in range(1, N_DEV):
            g = g + parts_t[i].astype(F32)
        m_n = ADAM_B1 * m_t + (1.0 - ADAM_B1) * g
        v_n = ADAM_B2 * v_t + (1.0 - ADAM_B2) * (g * g)
        m_hat = m_n / (1.0 - ADAM_B1 ** ADAM_STEP)
        v_hat = v_n / (1.0 - ADAM_B2 ** ADAM_STEP)
        delta = -ADAM_LR * (m_hat / (jnp.sqrt(v_hat) + ADAM_EPS) + ADAM_WD * w_t)
        return g, delta, m_n, v_n

    row = lambda a: (a, (tr, C), lambda i: (i, 0))
    out = ((R, C), F32, (tr, C), lambda i: (i, 0), None)
    return tile_call(name, fn, (R // tr,),
                     [(parts, (N_DEV, tr, C), lambda i: (0, i, 0)), row(w), row(m), row(v)], [out] * 4)


PARAMS = (
    ("g_mix", (1, 1024), None), ("w_in", (1024, 4128), 1), ("rw_mu", (1, 1824), None), ("rw_w0", (1, 512), None),
    ("rw_w_up", (64, 512), 1), ("rw_a0", (1, 512), None), ("rw_a_up", (64, 512), 1), ("rw_g_up", (160, 512), 1),
    ("rw_k_k", (1, 512), None), ("rw_k_a", (1, 512), None), ("rw_r_k", (8, 64), None), ("rw_ln_g", (1, 512), None),
    ("rw_ln_b", (1, 512), None), ("w_branch_a", (512, 1024), 1), ("w_branch_b", (256, 1024), 1),
    ("w_gate", (1024, 2048), 1), ("b_gate", (1, 2048), None), ("w_out", (1024, 1024), 0), ("g_ffn", (1, 1024), None),
    ("w_up", (1024, 6144), 1), ("conv_w", (3, 6144), 1), ("conv_b", (1, 6144), None), ("w_down", (3072, 1024), 0),
    ("g_ple", (1, 1024), None), ("w_ple_gate", (1024, 1024), 0), ("w_ple", (256, 1024), 1), ("g_final", (1, 1024), None),
)
SHARDED = tuple(q for q in PARAMS if q[2] is not None)
REPLICATED = tuple(q for q in PARAMS if q[2] is None)
BIG_NAMES = ("w_in", "w_up", "w_gate", "w_out", "w_down", "w_ple_gate", "w_branch_a", "w_branch_b", "w_ple")
BIG = tuple(q for q in SHARDED if q[0] in BIG_NAMES)
SMALL_SHARDED = tuple(q for q in SHARDED if q[0] not in BIG_NAMES)
PACK_COLS = 1024
F32_GATHERED = ("conv_w",)


def _local_shape(shape, axis):
    s = list(shape)
    s[axis] //= N_DEV
    return tuple(s)


def _numel(shape):
    return int(np.prod(shape))


def _pad_flat(z, mult):
    n = z.shape[-1]
    total = -(-n // mult) * mult
    return jnp.pad(z, [(0, 0)] * (z.ndim - 1) + [(0, total - n)])


def _full_from_slots(slots, shape, axis):
    loc = _local_shape(shape, axis)
    z = slots.reshape((N_DEV,) + loc)
    if axis == 0:
        return z.reshape(shape)
    return z.transpose(1, 0, 2).reshape(shape)


def _slots_from_full(full, shape, axis):
    loc = _local_shape(shape, axis)
    if axis == 0:
        return full.reshape(N_DEV, _numel(loc))
    return full.reshape(shape[0], N_DEV, loc[1]).transpose(1, 0, 2).reshape(N_DEV, _numel(loc))


W_IN_SLOT = 640
W_IN_LOCAL = 4128 // N_DEV


def _block_shape(shape, axis):
    return _local_shape(shape, axis) if axis is not None else shape


def _pad_w_in(block):
    return jnp.pad(block, ((0, 0), (0, W_IN_SLOT - W_IN_LOCAL)))


def _proj_col(s):
    return s + jnp.where(s >= 1600, 64, 0) + jnp.where(s >= 1664, 64, 0) + jnp.where(s >= 1824, 96, 0)


def _perm_tile(d, c0, width):
    j = lax.broadcasted_iota(jnp.int32, (W_IN_SLOT, width), 0)
    c = c0 + lax.broadcasted_iota(jnp.int32, (W_IN_SLOT, width), 1)
    hit = (_proj_col(d * W_IN_LOCAL + j) == c) & (j < W_IN_LOCAL)
    return jnp.where(hit, 1.0, 0.0).astype(BF16)


PERM_TILE = 768


def w_in_unshuffle(slots):
    _, K, _ = slots.shape
    tn = PERM_TILE
    reach = 3

    def first_slot(j):
        return j + jnp.where(j >= 3, 1, 0) + jnp.where(j >= 5, 1, 0)

    def body(a_ref, o_ref, acc_ref):
        j, kk = pl.program_id(0), pl.program_id(1)
        d = first_slot(j) + kk

        @pl.when(kk == 0)
        def _():
            acc_ref[...] = jnp.zeros_like(acc_ref)

        @pl.when(d < N_DEV)
        def _():
            acc_ref[...] += jnp.dot(a_ref[0], _perm_tile(d, j * tn, tn), preferred_element_type=F32)

        @pl.when(kk == reach - 1)
        def _():
            o_ref[...] = acc_ref[...].astype(o_ref.dtype)

    return pl.pallas_call(
        body, name="w_in_unshuffle", grid=(PROJ_PAD // tn, reach),
        in_specs=[pl.BlockSpec((1, K, W_IN_SLOT), lambda j, kk: (jnp.minimum(first_slot(j) + kk, N_DEV - 1), 0, 0))],
        out_specs=pl.BlockSpec((K, tn), lambda j, kk: (0, j)),
        out_shape=jax.ShapeDtypeStruct((K, PROJ_PAD), BF16),
        scratch_shapes=[pltpu.VMEM((K, tn), F32)],
        compiler_params=_cparams(2),
    )(slots)


def w_in_shuffle_grad(dw):
    K = dw.shape[0]
    tk = PERM_TILE

    def first_tile(d):
        return _proj_col(d * W_IN_LOCAL) // tk

    def body(g_ref, o_ref, acc_ref):
        d, kk = pl.program_id(0), pl.program_id(1)
        perm = _perm_tile(d, (first_tile(d) + kk) * tk, tk)
        part = lax.dot_general(g_ref[...].astype(BF16), perm, NT_DIMS, preferred_element_type=F32)

        @pl.when(kk == 0)
        def _():
            acc_ref[...] = part

        @pl.when(kk == 1)
        def _():
            o_ref[0] = (acc_ref[...] + part).astype(o_ref.dtype)

    return pl.pallas_call(
        body, name="w_in_shuffle_grad", grid=(N_DEV, 2),
        in_specs=[pl.BlockSpec((K, tk), lambda d, kk: (0, first_tile(d) + kk))],
        out_specs=pl.BlockSpec((1, K, W_IN_SLOT), lambda d, kk: (d, 0, 0)),
        out_shape=jax.ShapeDtypeStruct((N_DEV, K, W_IN_SLOT), GRAD_WIRE),
        scratch_shapes=[pltpu.VMEM((K, W_IN_SLOT), F32)],
        compiler_params=_cparams(2),
    )(dw)


def _flat_rows(pieces, dtype, row_mult):
    flat = jnp.concatenate([z.astype(dtype) for z in pieces], axis=-1)
    flat = _pad_flat(flat, row_mult * PACK_COLS)
    return flat.reshape(flat.shape[:-1] + (-1, PACK_COLS))


FIRST = tuple(q for q in BIG if q[0] in ("w_in", "w_gate"))
LATE = tuple(q for q in BIG if q not in FIRST)


def _matrix_from_slots(slots, shape, axis):
    return slots.reshape(shape) if axis == 0 else slots.transpose(1, 0, 2).reshape(shape)


def _late_weight_sources(W):
    return [(blk, WHOLE) for blk in W["_late_blocks"]]


def _late_weights(slots):
    return {n: _matrix_from_slots(s, shape, axis) for (n, shape, axis), s in zip(LATE, slots)}


def gather_weights(local):
    blocks = [(_pad_w_in(local[n]) if n == "w_in" else local[n]).astype(BF16) for n, _, _ in FIRST]
    small = [q for q in SMALL_SHARDED if q[0] not in F32_GATHERED]
    exact = [q for q in SMALL_SHARDED if q[0] in F32_GATHERED]
    blocks.append(_flat_rows([local[n].reshape(-1) for n, _, _ in small], BF16, 16))
    blocks.append(_flat_rows([local[n].reshape(-1) for n, _, _ in exact], F32, 8))
    got = all_gather_blocks("weight_all_gather", blocks)
    full = {"_late_blocks": [local[n].astype(BF16) for n, _, _ in LATE]}
    for (n, shape, axis), slots in zip(FIRST, got):
        if n == "w_in":
            full["w_in_p"] = w_in_unshuffle(slots)
        else:
            full[n] = _matrix_from_slots(slots, shape, axis)
    for group, slots in ((small, got[-2]), (exact, got[-1])):
        slots, off = slots.reshape(N_DEV, -1), 0
        for n, shape, axis in group:
            size = _numel(_local_shape(shape, axis))
            full[n] = _full_from_slots(slots[:, off:off + size], shape, axis)
            off += size
    for n, _, _ in REPLICATED:
        full[n] = local[n]
    return full


LOSS_SLOT = ("_loss", (1, 2), None)
PACKED_SMALL = SMALL_SHARDED + REPLICATED + (LOSS_SLOT,)


def _pack_small(vals):
    pieces = [vals[n].reshape(-1) if n in vals else jnp.zeros((_numel(shape),), F32) for n, shape, _ in PACKED_SMALL]
    return _flat_rows(pieces, F32, 16)


def _unpack_small(packed):
    flat, out, off = packed.reshape(-1), {}, 0
    for n, shape, axis in PACKED_SMALL:
        loc = _block_shape(shape, axis)
        out[n] = flat[off:off + _numel(loc)].reshape(loc)
        off += _numel(loc)
    return out


EARLY = tuple(q for q in BIG if q[0] != "w_in")


def _early_grad_sources(G):
    srcs = []
    for n, shape, axis in EARLY:
        if axis == 0:
            srcs.append((G[n].astype(GRAD_WIRE).reshape((N_DEV,) + _local_shape(shape, axis)), None))
        else:
            srcs.append((G[n].astype(GRAD_WIRE), shape[1] // N_DEV))
    return srcs


def _w_in_grad_sources(G):
    return [(w_in_shuffle_grad(G["w_in_p"]), None)]


def _closing_grad_sources(G, loss_local):
    srcs = []
    rows = [_slots_from_full(G[n].reshape(shape), shape, axis) for n, shape, axis in SMALL_SHARDED]
    loss_hi = loss_local.astype(GRAD_WIRE).astype(F32)
    rep = jnp.concatenate([G[n].reshape(-1) for n, _, _ in REPLICATED] + [jnp.stack([loss_hi, loss_local - loss_hi])])
    rows.append(jnp.broadcast_to(rep[None, :], (N_DEV, rep.shape[0])))
    srcs.append((_flat_rows(rows, GRAD_WIRE, 16), None))
    return srcs


def _step(x, p, target, local_w, local_m, local_v):
    full = gather_weights(local_w)
    loss_local, dx, G = local_step(x, p, target, full)
    closing = all_to_all_blocks("grad_all_to_all", _closing_grad_sources(G, loss_local))
    parts = list(G["_w_in_parts"]) + list(G["_early_parts"]) + list(closing)
    outs = [{}, {}, {}, {}]
    for (n, shape, axis), part in zip((BIG[0],) + EARLY, parts):
        prep = _pad_w_in if n == "w_in" else (lambda z: z)
        res = reduce_adamw("adamw_" + n, part, prep(local_w[n]), prep(local_m[n]), prep(local_v[n]))
        for o, z in zip(outs, res):
            o[n] = z[:, :W_IN_LOCAL] if n == "w_in" else z
    res = reduce_adamw("adamw_small", parts[-1], _pack_small(local_w), _pack_small(local_m), _pack_small(local_v))
    for o, z in zip(outs, res):
        o.update(_unpack_small(z))
    loss = jnp.sum(outs[0]["_loss"])
    return loss, dx, outs


def kernel(x, p, g_mix, w_in, rw_mu, rw_w0, rw_w_up, rw_a0, rw_a_up, rw_g_up, rw_k_k, rw_k_a, rw_r_k, rw_ln_g, rw_ln_b, w_branch_a, w_branch_b, w_gate, b_gate, w_out, g_ffn, w_up, conv_w, conv_b, w_down, g_ple, w_ple_gate, w_ple, g_final, loss_target, m_g_mix, m_w_in, m_rw_mu, m_rw_w0, m_rw_w_up, m_rw_a0, m_rw_a_up, m_rw_g_up, m_rw_k_k, m_rw_k_a, m_rw_r_k, m_rw_ln_g, m_rw_ln_b, m_w_branch_a, m_w_branch_b, m_w_gate, m_b_gate, m_w_out, m_g_ffn, m_w_up, m_conv_w, m_conv_b, m_w_down, m_g_ple, m_w_ple_gate, m_w_ple, m_g_final, v_g_mix, v_w_in, v_rw_mu, v_rw_w0, v_rw_w_up, v_rw_a0, v_rw_a_up, v_rw_g_up, v_rw_k_k, v_rw_k_a, v_rw_r_k, v_rw_ln_g, v_rw_ln_b, v_w_branch_a, v_w_branch_b, v_w_gate, v_b_gate, v_w_out, v_g_ffn, v_w_up, v_conv_w, v_conv_b, v_w_down, v_g_ple, v_w_ple_gate, v_w_ple, v_g_final):
    args = dict(locals())
    names = [n for n, _, _ in PARAMS]
    orig_shape = {n: args[n].shape for n in names}

    def strip(prefix):
        out = {}
        for n, shape, axis in PARAMS:
            a = args[prefix + n]
            loc = _local_shape(shape, axis) if axis is not None else shape
            out[n] = a.reshape(loc)
        return out

    local_w, local_m, local_v = strip(""), strip("m_"), strip("v_")
    T, D = x.shape[-2], x.shape[-1]
    loss, dx, (g, delta, m_n, v_n) = _step(x.reshape(T, D), p.reshape(T, p.shape[-1]), loss_target.reshape(T, D),
                                           local_w, local_m, local_v)
    outs = [loss, dx.reshape(x.shape)]
    for group in (g, delta, m_n, v_n):
        outs += [group[n].reshape(orig_shape[n]) for n in names]
    return tuple(outs)
```

```python
import functools

import numpy as np
import jax
import jax.numpy as jnp
from jax import lax
from jax.experimental import pallas as pl
from jax.experimental.pallas import tpu as pltpu

F32 = jnp.float32
BF16 = jnp.bfloat16
GRAD_WIRE = jnp.bfloat16

N_DEV = 8
NORM_EPS = 1e-6
RW_LN_EPS = 64e-5
HEAD = 64
RW_WIDTH = 512
ATT_GROUPS = ((128, 1), (512, 4), (2048, 16))
ATT_HEADS = 12
ATT_OUT = 256
ATT_COLS = 2304
OFF_XW, OFF_XA, OFF_XG, RW_PAD, PROJ_PAD = 1536, 1664, 1792, 2048, 4608
PROJ_TAIL = PROJ_PAD - RW_PAD - ATT_COLS
D_FF = 3072

ADAM_LR, ADAM_B1, ADAM_B2, ADAM_EPS, ADAM_WD, ADAM_STEP = 0.001, 0.9, 0.999, 1e-08, 0.01, 10

VMEM_LIMIT_BYTES = 56 * 1024 * 1024
ADAM_TILE_ELEMS = 256 * 1024
NEG_BIG = -1e30

NT_DIMS = (((1,), (1,)), ((), ()))
TN_DIMS = (((0,), (0,)), ((), ()))
NN_DIMS = (((1,), (0,)), ((), ()))


def _cparams(n_axes):
    return pltpu.CompilerParams(dimension_semantics=("arbitrary",) * n_axes,
                                vmem_limit_bytes=VMEM_LIMIT_BYTES)


def _split2(x):
    hi = x.astype(BF16)
    lo = (x - hi.astype(F32)).astype(BF16)
    return hi, lo


def _seg_mat(n):
    r = lax.shift_right_logical(lax.broadcasted_iota(jnp.int32, (n, n), 0), 6)
    c = lax.shift_right_logical(lax.broadcasted_iota(jnp.int32, (n, n), 1), 6)
    return jnp.where(r == c, 1.0, 0.0).astype(BF16)


def _segb(x, seg):
    return _segb_stack([(x, 2)], seg)[0]


def _segb_stack(items, seg):
    rows = items[0][0].shape[0]
    parts = []
    for x, passes in items:
        parts += list(_split2(x)) if passes == 2 else [x.astype(BF16)]
    res = jnp.dot(jnp.concatenate(parts, axis=0), seg, preferred_element_type=F32)
    out, at = [], 0
    for _, passes in items:
        piece = res[at * rows:(at + 1) * rows]
        if passes == 2:
            piece = piece + res[(at + 1) * rows:(at + 2) * rows]
        out.append(piece)
        at += passes
    return out


def _segb1(x, seg):
    return jnp.dot(x.astype(BF16), seg, preferred_element_type=F32)


@jax.custom_vjp
def segsum(x):
    return _segb(x, _seg_mat(x.shape[1]))


def _segsum_fwd(x):
    return segsum(x), None


def _segsum_bwd(_, ct):
    return (segsum(ct),)


segsum.defvjp(_segsum_fwd, _segsum_bwd)


@jax.custom_vjp
def bdot(a, b):
    return jnp.dot(a.astype(BF16), b.astype(BF16), preferred_element_type=F32)


def _bdot_fwd(a, b):
    return bdot(a, b), (a, b)


def _bdot_bwd(res, ct):
    a, b = res
    ctb = ct.astype(BF16)
    da = lax.dot_general(ctb, b.astype(BF16), NT_DIMS, preferred_element_type=F32)
    db = lax.dot_general(a.astype(BF16), ctb, TN_DIMS, preferred_element_type=F32)
    return da.astype(a.dtype), db.astype(b.dtype)


bdot.defvjp(_bdot_fwd, _bdot_bwd)


def _sig(x):
    return 1.0 / (1.0 + jnp.exp(-x))


def _softplus(z):
    return jnp.maximum(z, 0.0) + jnp.log(1.0 + jnp.exp(-jnp.abs(z)))


def _gelu_tanh(x):
    return 0.5 * x * (1.0 + jnp.tanh(0.7978845608028654 * (x + 0.044715 * (x * x * x))))


def _rms(x, g):
    return x * lax.rsqrt(jnp.mean(x * x, axis=-1, keepdims=True) + NORM_EPS) * g


def _shift_down(x, prev8, n):
    rolled = pltpu.roll(x, n, 0)
    top = pltpu.roll(prev8, n, 0)
    rid = lax.broadcasted_iota(jnp.int32, (8, x.shape[1]), 0)
    head = jnp.where(rid < n, top, rolled[:8])
    return jnp.concatenate([head, rolled[8:]], axis=0)


def _shift_up(x, next8, n):
    rows = x.shape[0]
    rolled = pltpu.roll(x, rows - n, 0)
    bottom = pltpu.roll(next8, 8 - n, 0)
    rid = lax.broadcasted_iota(jnp.int32, (8, x.shape[1]), 0)
    tail = jnp.where(rid >= 8 - n, bottom, rolled[rows - 8:])
    return jnp.concatenate([rolled[:rows - 8], tail], axis=0)


def tile_call(name, fn, grid, ins, outs, scratch=()):
    n_in, n_out = len(ins), len(outs)
    acc_axes = [o[4] for o in outs]

    def body(*refs):
        pids = tuple(pl.program_id(a) for a in range(len(grid)))
        vals = fn(pids, *[r[...] for r in refs[:n_in]], *refs[n_in + n_out:])
        if not isinstance(vals, (tuple, list)):
            vals = (vals,)
        for o_ref, val, ax in zip(refs[n_in:n_in + n_out], vals, acc_axes):
            if ax is None:
                o_ref[...] = val.astype(o_ref.dtype)
            else:
                @pl.when(pids[ax] == 0)
                def _(o_ref=o_ref):
                    o_ref[...] = jnp.zeros_like(o_ref)

                o_ref[...] += val.astype(o_ref.dtype)

    res = pl.pallas_call(
        body, name=name, grid=grid,
        in_specs=[pl.BlockSpec(b, im) for _, b, im in ins],
        out_specs=[pl.BlockSpec(o[2], o[3]) for o in outs],
        out_shape=[jax.ShapeDtypeStruct(o[0], o[1]) for o in outs],
        scratch_shapes=[pltpu.VMEM(s, d) for s, d in scratch],
        compiler_params=_cparams(len(grid)),
    )(*[a for a, _, _ in ins])
    return res


def _rows(a, tm):
    return (a, (tm, a.shape[1]), lambda i: (i, 0))


def _par(a):
    return (a, a.shape, lambda i: (0, 0))


def _row_out(T, C, dtype, tm):
    return ((T, C), dtype, (tm, C), lambda i: (i, 0), None)


def _acc_out(R, C):
    return ((R, C), F32, (R, C), lambda i: (0, 0), 0)


def _prev_halo(a, tm, C):
    return (a, (8, C), lambda i: (jnp.maximum(i * (tm // 8) - 1, 0), 0))


def _next_halo(a, tm, C, T):
    return (a, (8, C), lambda i: (jnp.minimum((i + 1) * (tm // 8), T // 8 - 1), 0))


def _pick(n, target):
    for t in (target, 2048, 1536, 1024, 768, 512, 384, 256, 128):
        if t <= target and n % t == 0:
            return t
    return n


def matmul(name, a, b, mode="nn", res=None, out_dtype=F32, tm=1024, tn=2048, tk=2048, exchange=()):
    if mode == "nn":
        (M, K), (K2, N) = a.shape, b.shape
    elif mode == "tn":
        (K, M), (K2, N) = a.shape, b.shape
    else:
        (M, K), (N, K2) = a.shape, b.shape
    assert K == K2, (name, a.shape, b.shape, mode)
    tm, tn, tk = _pick(M, tm), _pick(N, tn), _pick(K, tk)
    nk = K // tk
    dims = {"nn": NN_DIMS, "tn": TN_DIMS, "nt": NT_DIMS}[mode]
    a_spec = {"nn": pl.BlockSpec((tm, tk), lambda i, j, k: (i, k)),
              "tn": pl.BlockSpec((tk, tm), lambda i, j, k: (k, i)),
              "nt": pl.BlockSpec((tm, tk), lambda i, j, k: (i, k))}[mode]
    b_spec = {"nn": pl.BlockSpec((tk, tn), lambda i, j, k: (k, j)),
              "tn": pl.BlockSpec((tk, tn), lambda i, j, k: (k, j)),
              "nt": pl.BlockSpec((tn, tk), lambda i, j, k: (j, k))}[mode]
    has_res = res is not None
    nx = len(exchange)
    grid = (M // tm, N // tn, nk)

    def body(*refs):
        a_ref, b_ref = refs[:2]
        r_ref = refs[2] if has_res else None
        refs = refs[2 + has_res:]
        x_refs, o_ref, land_refs, acc_ref = refs[:nx], refs[nx], refs[nx + 1:2 * nx + 1], refs[2 * nx + 1]
        k = pl.program_id(2)
        if nx:
            step = (pl.program_id(0) * grid[1] + pl.program_id(1)) * nk + k
            start, wait = _exchange_ops([c for _, c in exchange], x_refs, land_refs, *refs[2 * nx + 2:])

            @pl.when(step == 0)
            def _():
                start()

        @pl.when(k == 0)
        def _():
            acc_ref[...] = jnp.zeros_like(acc_ref)

        acc_ref[...] += lax.dot_general(a_ref[...].astype(BF16), b_ref[...].astype(BF16), dims,
                                        preferred_element_type=F32)

        @pl.when(k == nk - 1)
        def _():
            out = acc_ref[...]
            if has_res:
                out = out + r_ref[...].astype(F32)
            o_ref[...] = out.astype(o_ref.dtype)

        if nx:
            @pl.when(step == grid[0] * grid[1] * nk - 1)
            def _():
                wait()

    in_specs = [a_spec, b_spec]
    args = [a, b]
    if has_res:
        in_specs.append(pl.BlockSpec((tm, tn), lambda i, j, k: (i, j)))
        args.append(res)
    hbm = pl.BlockSpec(memory_space=pl.ANY)
    out = pl.pallas_call(
        body, name=name, grid=grid,
        in_specs=in_specs + [hbm] * nx,
        out_specs=[pl.BlockSpec((tm, tn), lambda i, j, k: (i, j))] + [hbm] * nx,
        out_shape=[jax.ShapeDtypeStruct((M, N), out_dtype)] + _exchange_shapes(exchange),
        scratch_shapes=[pltpu.VMEM((tm, tn), F32)] + (_exchange_sems(nx) if nx else []),
        compiler_params=pltpu.CompilerParams(dimension_semantics=("arbitrary",) * 3, vmem_limit_bytes=VMEM_LIMIT_BYTES,
                                             has_side_effects=bool(nx)),
    )(*args, *[z for z, _ in exchange])
    return (out[0], out[1:]) if nx else out[0]


def rw_pre(Pc, Ps, mu, w0, w_up, a0, a_up, g_up, k_k, k_a):
    Pm = Pc + (Ps - Pc) * mu
    r, k, v = Pm[:, 0:512], Pm[:, 512:1024], Pm[:, 1024:1536]
    xw, xa, xg = Pm[:, OFF_XW:OFF_XA], Pm[:, OFF_XA:OFF_XG], Pm[:, OFF_XG:RW_PAD]
    w = -_softplus(-(w0 + bdot(jnp.tanh(xw), w_up))) - 0.5
    decay = jnp.exp(-jnp.exp(w))
    a = _sig(a0 + bdot(xa, a_up))
    g = bdot(_sig(xg), g_up)
    kk = k * k_k
    kk = kk / jnp.maximum(jnp.sqrt(segsum(kk * kk)), 1e-12)
    k2 = k * (1.0 + (a - 1.0) * k_a)
    return r, decay, k2, v, -kk, kk * a, g


def rw_post(y, r, k2, v, g, ln_g, ln_b, r_k):
    mean = segsum(y) * (1.0 / HEAD)
    d = y - mean
    var = segsum(d * d) * (1.0 / HEAD)
    yn = d * lax.rsqrt(var + RW_LN_EPS) * ln_g + ln_b
    bonus = segsum(r * k2 * r_k) * v
    return (yn + bonus) * g


def att_combine(o1, o2, o3, l1, l2, l3):
    m = jnp.maximum(jnp.maximum(l1, l2), l3)
    e1, e2, e3 = jnp.exp(l1 - m), jnp.exp(l2 - m), jnp.exp(l3 - m)
    return (e1 * o1 + e2 * o2 + e3 * o3) / (e1 + e2 + e3)


def merge_fn(gp, bg, za, zb):
    s = _sig(gp + bg)
    half = za.shape[1]
    return s[:, :half] * za + s[:, half:] * zb


def tail_loss(x2, zg, pe, g_final, target):
    x3 = x2 + _sig(zg) * pe
    y = _rms(x3, g_final)
    err = (y - target) * (y - target)
    return 0.5 * jnp.sum(jnp.mean(err, axis=-1, keepdims=True))


SCAN_CHUNK = HEAD
SCAN_LANES = 256
SCAN_UNROLL_FWD, SCAN_UNROLL_BWD = 32, 32


def _to_head_time(z):
    T = z.shape[0]
    return z.reshape(T // HEAD, HEAD, RW_WIDTH // HEAD, HEAD).transpose(0, 3, 2, 1).reshape(T // HEAD, HEAD, RW_WIDTH)


def _from_head_time(zt):
    C = zt.shape[0]
    return zt.reshape(C, HEAD, RW_WIDTH // HEAD, HEAD).transpose(0, 3, 2, 1).reshape(C * HEAD, RW_WIDTH)


def _unrolled_loop(n, step, init, unroll):
    def body(i, carry):
        for j in range(unroll):
            carry = step(i * unroll + j, carry)
        return carry

    return lax.fori_loop(0, n // unroll, body, init)


def _lane_groups():
    return [slice(j * SCAN_LANES, (j + 1) * SCAN_LANES) for j in range(RW_WIDTH // SCAN_LANES)]


def scan_pair_terms(a, w, b, k, tm=512):
    T = a.shape[0]

    def fn(pid, a_t, nxt, w_t, b_t, k_t):
        a_next = _shift_up(a_t, jnp.where(pid[0] < T // tm - 1, nxt, 0.0), 1)
        return w_t * a_next, segsum(b_t * a_next), segsum(k_t * a_next)

    return tile_call("scan_pair_terms", fn, (T // tm,),
                     [_rows(a, tm), _next_halo(a, tm, RW_WIDTH, T), _rows(w, tm), _rows(b, tm), _rows(k, tm)],
                     [_row_out(T, RW_WIDTH, F32, tm)] * 3)


def rwkv_scan_fwd(a, w, b, k, r, vT, wa, ba, ka, exchange=()):
    T = a.shape[0]
    C, LW = SCAN_CHUNK, SCAN_LANES
    nC = T // C
    nx = len(exchange)

    def body(*refs):
        a_ref, w_ref, b_ref, k_ref, r_ref, vT_ref, wa_ref, ba_ref, ka_ref = refs[:9]
        x_refs, refs = refs[9:9 + nx], refs[9 + nx:]
        yT_ref, S_ref, saT_ref = refs[:3]
        land_refs, refs = refs[3:3 + nx], refs[3 + nx:]
        st_ref, vb0_ref, vb1_ref, seg_ref = refs[:4]
        if nx:
            start, wait = _exchange_ops([c for _, c in exchange], x_refs, land_refs, *refs[4:])

        @pl.when(pl.program_id(0) == 0)
        def _():
            st_ref[...] = jnp.zeros_like(st_ref)
            seg_ref[...] = _seg_mat(LW)
            if nx:
                start()

        seg = seg_ref[...]
        lane = jnp.bitwise_and(lax.broadcasted_iota(jnp.int32, (1, LW), 1), HEAD - 1)
        groups = _lane_groups()

        def vsel(t, gsl):
            return jnp.where(lane == t, vT_ref[0, :, gsl], 0.0)

        first = _segb_stack([(vsel(s, gsl), 1) for gsl in groups for s in (0, 1)], seg)
        for g, gsl in enumerate(groups):
            vb0_ref[:, gsl] = first[2 * g]
            vb1_ref[:, gsl] = first[2 * g + 1]
        saT_ref[...] = jnp.zeros_like(saT_ref)

        def pair(i, yacc):
            t = 2 * i
            t1 = t + 1
            tp = jnp.maximum(t - 1, 0)
            row = lambda ref, s, gsl: ref[pl.ds(s, 1), gsl]
            Sps = [st_ref[:, gsl] for gsl in groups]
            chain = _segb_stack([(Sp * row(ref, t, gsl), 2) for gsl, Sp in zip(groups, Sps) for ref in (a_ref, wa_ref)],
                                seg)
            sas, us = chain[0::2], chain[1::2]
            S1s = []
            for gsl, Sp, sa, u in zip(groups, Sps, sas, us):
                vb0, vb1 = vb0_ref[:, gsl], vb1_ref[:, gsl]
                S1 = Sp * row(w_ref, t, gsl) + sa * row(b_ref, t, gsl) + vb0 * row(k_ref, t, gsl)
                sa1 = u + sa * row(ba_ref, t, gsl) + vb0 * row(ka_ref, t, gsl)
                st_ref[:, gsl] = S1 * row(w_ref, t1, gsl) + sa1 * row(b_ref, t1, gsl) + vb1 * row(k_ref, t1, gsl)
                S_ref[0, t, :, gsl] = Sp
                S_ref[0, t1, :, gsl] = S1
                S1s.append(S1)
                saT_ref[0, :, gsl] = jnp.where(lane == t, sa, jnp.where(lane == t1, sa1, saT_ref[0, :, gsl]))
            side = _segb_stack([(x, 1) for gsl, Sp, S1 in zip(groups, Sps, S1s)
                                for x in (Sp * row(r_ref, tp, gsl), S1 * row(r_ref, t, gsl),
                                          vsel(t + 2, gsl), vsel(t + 3, gsl))], seg)
            out = []
            for g, (gsl, ya) in enumerate(zip(groups, yacc)):
                yb0, yb1, vb0_ref[:, gsl], vb1_ref[:, gsl] = side[4 * g:4 * g + 4]
                out.append(jnp.where(lane == t, yb1, jnp.where(lane == t - 1, yb0, ya)))
            return tuple(out)

        yacc = _unrolled_loop(C // 2, pair, tuple(jnp.zeros((HEAD, LW), F32) for _ in groups), SCAN_UNROLL_FWD)
        for gsl, ya in zip(groups, yacc):
            S_last = st_ref[:, gsl]
            S_ref[0, C, :, gsl] = S_last
            yb = _segb1(S_last * r_ref[pl.ds(C - 1, 1), gsl], seg)
            yT_ref[0, :, gsl] = jnp.where(lane == C - 1, yb, ya)

        if nx:
            @pl.when(pl.program_id(0) == nC - 1)
            def _():
                wait()

    row = pl.BlockSpec((C, RW_WIDTH), lambda c: (c, 0))
    ht = pl.BlockSpec((1, HEAD, RW_WIDTH), lambda c: (c, 0, 0))
    hbm = pl.BlockSpec(memory_space=pl.ANY)
    res = pl.pallas_call(
        body, name="rwkv_scan_fwd", grid=(nC,),
        in_specs=[row, row, row, row, row, ht, row, row, row] + [hbm] * nx,
        out_specs=[ht, pl.BlockSpec((1, C + 1, HEAD, RW_WIDTH), lambda c: (c, 0, 0, 0)), ht] + [hbm] * nx,
        out_shape=[jax.ShapeDtypeStruct((nC, HEAD, RW_WIDTH), F32),
                   jax.ShapeDtypeStruct((nC, C + 1, HEAD, RW_WIDTH), F32),
                   jax.ShapeDtypeStruct((nC, HEAD, RW_WIDTH), F32)] + _exchange_shapes(exchange),
        scratch_shapes=[pltpu.VMEM((HEAD, RW_WIDTH), F32)] * 3 + [pltpu.VMEM((LW, LW), BF16)]
        + (_exchange_sems(nx) if nx else []),
        compiler_params=pltpu.CompilerParams(dimension_semantics=("arbitrary",), vmem_limit_bytes=VMEM_LIMIT_BYTES,
                                             has_side_effects=bool(nx)),
    )(a, w, b, k, r, vT, wa, ba, ka, *[z for z, _ in exchange])
    return res[:3], res[3:]


def rwkv_scan_bwd(a, w, b, k, r, v, dy, S_all, saT, exchange=()):
    T = a.shape[0]
    C, LW = SCAN_CHUNK, SCAN_LANES
    nC = T // C
    nx = len(exchange)
    n_heads = RW_WIDTH // HEAD
    dyT = _to_head_time(dy).astype(BF16)
    v_rows, dy_rows = v.reshape(T, n_heads, HEAD), dy.reshape(T, n_heads, HEAD)
    sa_rows = _from_head_time(saT).reshape(T, n_heads, HEAD)

    def body(*refs):
        a_ref, w_ref, b_ref, k_ref, r_ref, vR_ref, saR_ref, dyR_ref, dyT_ref, S_ref = refs[:10]
        x_refs, refs = refs[10:10 + nx], refs[10 + nx:]
        da_ref, dw_ref, db_ref, dk_ref, dr_ref, dvT_ref = refs[:6]
        land_refs, refs = refs[6:6 + nx], refs[6 + nx:]
        ds_ref, dyb_ref, seg_ref = refs[:3]
        if nx:
            start, wait = _exchange_ops([c for _, c in exchange], x_refs, land_refs, *refs[3:])

        @pl.when(pl.program_id(0) == 0)
        def _():
            ds_ref[...] = jnp.zeros_like(ds_ref)
            seg_ref[...] = _seg_mat(LW)
            if nx:
                start()

        seg = seg_ref[...]
        lane = jnp.bitwise_and(lax.broadcasted_iota(jnp.int32, (1, LW), 1), HEAD - 1)
        groups = _lane_groups()
        head_row = lax.broadcasted_iota(jnp.int32, (n_heads, LW), 0)
        lane_head = lax.shift_right_logical(lax.broadcasted_iota(jnp.int32, (n_heads, LW), 1), 6)

        def colsum(z):
            return jnp.sum(z, axis=0, keepdims=True)

        def dysel(t, gsl):
            return jnp.where(lane == t, dyT_ref[0, :, gsl], 0.0)

        for gsl, dyb in zip(groups, _segb_stack([(dysel(C - 1, gsl), 1) for gsl in groups], seg)):
            dyb_ref[:, gsl] = dyb

        def step(i, dvacc):
            t = C - 1 - i
            dybs = [dyb_ref[:, gsl] for gsl in groups]
            dSs = [ds_ref[:, gsl] + dyb * r_ref[pl.ds(t, 1), gsl] for gsl, dyb in zip(groups, dybs)]
            dsabs = _segb_stack([(dS * b_ref[pl.ds(t, 1), gsl], 2) for gsl, dS in zip(groups, dSs)], seg)
            for gsl, dS, dsab in zip(groups, dSs, dsabs):
                ds_ref[:, gsl] = dS * w_ref[pl.ds(t, 1), gsl] + dsab * a_ref[pl.ds(t, 1), gsl]
            out = []
            dy_rows = dyR_ref[t].astype(BF16)
            v_sa_rows = jnp.concatenate([vR_ref[t], saR_ref[t]], axis=0).astype(BF16)
            side = _segb_stack([(x, 1) for gsl, dS in zip(groups, dSs)
                                for x in (dS * k_ref[pl.ds(t, 1), gsl], dysel(t - 1, gsl))], seg)
            for g, (gsl, dva, dS, dsab) in enumerate(zip(groups, dvacc, dSs, dsabs)):
                dvb, dyb_ref[:, gsl] = side[2 * g:2 * g + 2]
                Sp = S_ref[0, t, :, gsl]
                own = head_row == lane_head + g * (LW // HEAD)

                def rows_in(rows, mat):
                    full = jnp.dot(rows, mat.astype(BF16), preferred_element_type=F32)
                    return [jnp.sum(jnp.where(own, full[s:s + n_heads], 0.0), axis=0, keepdims=True)
                            for s in range(0, rows.shape[0], n_heads)]

                (dr,) = rows_in(dy_rows, S_ref[0, t + 1, :, gsl])
                dk, db = rows_in(v_sa_rows, dS)
                dr_ref[pl.ds(t, 1), gsl] = dr
                dk_ref[pl.ds(t, 1), gsl] = dk
                db_ref[pl.ds(t, 1), gsl] = db
                dw_ref[pl.ds(t, 1), gsl] = colsum(dS * Sp)
                da_ref[pl.ds(t, 1), gsl] = colsum(Sp * dsab)
                out.append(jnp.where(lane == t, dvb, dva))
            return tuple(out)

        dvacc = _unrolled_loop(C, step, tuple(jnp.zeros((HEAD, LW), F32) for _ in groups), SCAN_UNROLL_BWD)
        for gsl, dva in zip(groups, dvacc):
            dvT_ref[0, :, gsl] = dva

        if nx:
            @pl.when(pl.program_id(0) == nC - 1)
            def _():
                wait()

    row = pl.BlockSpec((C, RW_WIDTH), lambda c: (nC - 1 - c, 0))
    ht = pl.BlockSpec((1, HEAD, RW_WIDTH), lambda c: (nC - 1 - c, 0, 0))
    hbm = pl.BlockSpec(memory_space=pl.ANY)
    per_head = pl.BlockSpec((C, n_heads, HEAD), lambda c: (nC - 1 - c, 0, 0))
    rows_shape = jax.ShapeDtypeStruct((T, RW_WIDTH), F32)
    res = pl.pallas_call(
        body, name="rwkv_scan_bwd", grid=(nC,),
        in_specs=[row, row, row, row, row, per_head, per_head, per_head, ht,
                  pl.BlockSpec((1, C + 1, HEAD, RW_WIDTH), lambda c: (nC - 1 - c, 0, 0, 0))] + [hbm] * nx,
        out_specs=[row, row, row, row, row, ht] + [hbm] * nx,
        out_shape=[rows_shape] * 5 + [jax.ShapeDtypeStruct((nC, HEAD, RW_WIDTH), F32)] + _exchange_shapes(exchange),
        scratch_shapes=[pltpu.VMEM((HEAD, RW_WIDTH), F32), pltpu.VMEM((HEAD, RW_WIDTH), F32),
                        pltpu.VMEM((LW, LW), BF16)] + (_exchange_sems(nx) if nx else []),
        compiler_params=pltpu.CompilerParams(dimension_semantics=("arbitrary",), vmem_limit_bytes=VMEM_LIMIT_BYTES,
                                             has_side_effects=bool(nx)),
    )(a, w, b, k, r, v_rows, sa_rows, dy_rows, dyT, S_all, *[z for z, _ in exchange])
    return res[:6], res[6:]


def _alibi_slope(h):
    return float(np.float32(2.0 ** (-8.0 * (h + 1) / ATT_HEADS)))


ATT_GROUP_HEADS = 4


def _stack_heads(x, lane_head, fill=0.0):
    return jnp.concatenate([jnp.where(lane_head == hh, x, fill) for hh in range(ATT_GROUP_HEADS)], axis=0)


def _unstack_heads(x, lane_head, L):
    out = jnp.zeros((L, x.shape[1]), F32)
    for hh in range(ATT_GROUP_HEADS):
        out = jnp.where(lane_head == hh, x[hh * L:(hh + 1) * L], out)
    return out


def _att_logits(qs, kcat, gi, d, L, n):
    qi = lax.broadcasted_iota(jnp.int32, (L, 2 * L), 0)
    kj = lax.broadcasted_iota(jnp.int32, (L, 2 * L), 1)
    steps = qi + L - kj
    valid = (steps >= 0) & (steps <= L) & ((kj >= L) | (n > 0))
    dist = (d * steps).astype(F32)
    bias = jnp.concatenate([jnp.where(valid, -_alibi_slope(gi * ATT_GROUP_HEADS + hh) * dist, NEG_BIG)
                            for hh in range(ATT_GROUP_HEADS)], axis=0)
    s = lax.dot_general(qs.astype(BF16), kcat, NT_DIMS, preferred_element_type=F32) * (HEAD ** -0.5)
    return jnp.where(bias > 0.5 * NEG_BIG, s + bias, NEG_BIG)


def att_fwd(pa, gi, T):
    window, d = ATT_GROUPS[gi]
    L = window // d
    Tj = T // d
    nb = Tj // L
    pv = pa.reshape(Tj, d * ATT_COLS)
    nblk = ATT_COLS // ATT_OUT

    def fn(pids, q, kp, kc, vp, vc):
        lane_head = lax.shift_right_logical(lax.broadcasted_iota(jnp.int32, (1, ATT_OUT), 1), 6)
        kcat = jnp.concatenate([kp, kc], axis=0).astype(BF16)
        vcat = jnp.concatenate([vp, vc], axis=0).astype(BF16)
        s = _att_logits(_stack_heads(q, lane_head), kcat, gi, d, L, pids[1])
        m = jnp.max(s, axis=-1, keepdims=True)
        p = jnp.exp(s - m)
        l = jnp.sum(p, axis=-1, keepdims=True)
        o = jnp.dot(p.astype(BF16), vcat, preferred_element_type=F32) / l
        lse = jnp.broadcast_to(m + jnp.log(l), o.shape)
        return _unstack_heads(o, lane_head, L), _unstack_heads(lse, lane_head, L)

    blk = (L, ATT_OUT)
    ins = [(pv, blk, lambda r, n: (n, r * nblk + gi)),
           (pv, blk, lambda r, n: (jnp.maximum(n - 1, 0), r * nblk + 3 + gi)),
           (pv, blk, lambda r, n: (n, r * nblk + 3 + gi)),
           (pv, blk, lambda r, n: (jnp.maximum(n - 1, 0), r * nblk + 6 + gi)),
           (pv, blk, lambda r, n: (n, r * nblk + 6 + gi))]
    out = ((Tj, d * ATT_OUT), F32, blk, lambda r, n: (n, r), None)
    o, lseb = tile_call(f"att_fwd_g{gi}", fn, (d, nb), ins, [out, out])
    return o.reshape(T, ATT_OUT), lseb.reshape(T, ATT_OUT)


def att_bwd(pa, o, lseb, do, dlseb, gi, T):
    window, d = ATT_GROUPS[gi]
    L = window // d
    Tj = T // d
    nb = Tj // L
    pv = pa.reshape(Tj, d * ATT_COLS)
    nblk = ATT_COLS // ATT_OUT
    view = lambda z: z.reshape(Tj, d * ATT_OUT)

    def body(q_ref, kp_ref, kc_ref, vp_ref, vc_ref, o_ref, l_ref, do_ref, dl_ref, dq_ref, dk_ref, dv_ref):
        n = pl.program_id(1)

        @pl.when(n == 0)
        def _():
            dk_ref[...] = jnp.zeros_like(dk_ref)
            dv_ref[...] = jnp.zeros_like(dv_ref)

        lane_head = lax.shift_right_logical(lax.broadcasted_iota(jnp.int32, (1, ATT_OUT), 1), 6)
        kcat = jnp.concatenate([kp_ref[...], kc_ref[...]], axis=0).astype(BF16)
        vcat = jnp.concatenate([vp_ref[...], vc_ref[...]], axis=0).astype(BF16)
        qs = _stack_heads(q_ref[...], lane_head)
        dos = _stack_heads(do_ref[...], lane_head)
        lse = jnp.max(_stack_heads(l_ref[...], lane_head, NEG_BIG), axis=-1, keepdims=True)
        dlse = jnp.sum(_stack_heads(dl_ref[...], lane_head), axis=-1, keepdims=True)
        delta = jnp.sum(dos * jnp.concatenate([o_ref[...]] * ATT_GROUP_HEADS, axis=0), axis=-1, keepdims=True)
        p = jnp.exp(_att_logits(qs, kcat, gi, d, L, n) - lse)
        dp = lax.dot_general(dos.astype(BF16), vcat, NT_DIMS, preferred_element_type=F32)
        ds = (p * (dp - delta + dlse)).astype(BF16)
        dq = _unstack_heads(jnp.dot(ds, kcat, preferred_element_type=F32), lane_head, L)
        dkc = lax.dot_general(ds, qs.astype(BF16), TN_DIMS, preferred_element_type=F32)
        dvc = lax.dot_general(p.astype(BF16), dos.astype(BF16), TN_DIMS, preferred_element_type=F32)
        scale = HEAD ** -0.5
        dq_ref[...] = (dq * scale).astype(dq_ref.dtype)
        cur = pl.ds(pl.multiple_of(n * L, L), L)
        dk_ref[cur, :] += dkc[L:] * scale
        dv_ref[cur, :] += dvc[L:]

        @pl.when(n > 0)
        def _():
            prev = pl.ds(pl.multiple_of((n - 1) * L, L), L)
            dk_ref[prev, :] += dkc[:L] * scale
            dv_ref[prev, :] += dvc[:L]

    blk = pl.BlockSpec((L, ATT_OUT), lambda r, n: (n, r))
    res = pl.BlockSpec((Tj, ATT_OUT), lambda r, n: (0, r))
    qspec = lambda off, prev: pl.BlockSpec(
        (L, ATT_OUT), (lambda r, n: (jnp.maximum(n - 1, 0), r * nblk + off + gi)) if prev
        else (lambda r, n: (n, r * nblk + off + gi)))
    shape = jax.ShapeDtypeStruct((Tj, d * ATT_OUT), F32)
    dq, dk, dv = pl.pallas_call(
        body, name=f"att_bwd_g{gi}", grid=(d, nb),
        in_specs=[qspec(0, False), qspec(3, True), qspec(3, False), qspec(6, True), qspec(6, False),
                  blk, blk, blk, blk],
        out_specs=[blk, res, res],
        out_shape=[jax.ShapeDtypeStruct(shape.shape, BF16), shape, shape],
        compiler_params=_cparams(2),
    )(pv, pv, pv, pv, pv, view(o), view(lseb), view(do), view(dlseb))
    return dq.reshape(T, ATT_OUT), dk.reshape(T, ATT_OUT), dv.reshape(T, ATT_OUT)


FFN_TM, FFN_TC = 512, 512
CONV_CHUNK_ROWS = 16


def _conv3(u, prev8, cw, cb):
    return cb + cw[0:1] * u + cw[1:2] * _shift_down(u, prev8, 1) + cw[2:3] * _shift_down(u, prev8, 2)


def conv_glu_fwd(u, conv_w, conv_b):
    T = u.shape[0]
    tm, tc = FFN_TM, FFN_TC
    nj, ni = D_FF // tc, T // tm

    ch = CONV_CHUNK_ROWS

    def body(ug_ref, ugh_ref, uv_ref, uvh_ref, cwg_ref, cbg_ref, cwv_ref, cbv_ref, o_ref):
        first_tile = pl.program_id(1) == 0
        halves = [(ug_ref, jnp.where(first_tile, 0.0, ugh_ref[...]), cwg_ref[...], cbg_ref[...]),
                  (uv_ref, jnp.where(first_tile, 0.0, uvh_ref[...]), cwv_ref[...], cbv_ref[...])]

        def chunk(r, carry):
            rows = pl.ds(pl.multiple_of(r * ch, ch), ch)
            before = pl.ds(pl.multiple_of(jnp.maximum(r * ch - 8, 0), 8), 8)
            cg, cv = [_conv3(u_ref[rows, :], jnp.where(r == 0, halo, u_ref[before, :]), cw, cb)
                      for u_ref, halo, cw, cb in halves]
            o_ref[rows, :] = (_gelu_tanh(cg) * cv).astype(o_ref.dtype)
            return carry

        _unrolled_loop(tm // ch, chunk, 0, 4)

    halo = lambda off: (lambda j, i: (jnp.maximum(i * (tm // 8) - 1, 0), j + off))
    blk = lambda shape, im: pl.BlockSpec(shape, im)
    return pl.pallas_call(
        body, name="conv_glu_fwd", grid=(nj, ni),
        in_specs=[blk((tm, tc), lambda j, i: (i, j)), blk((8, tc), halo(0)),
                  blk((tm, tc), lambda j, i: (i, j + nj)), blk((8, tc), halo(nj)),
                  blk((3, tc), lambda j, i: (0, j)), blk((1, tc), lambda j, i: (0, j)),
                  blk((3, tc), lambda j, i: (0, j + nj)), blk((1, tc), lambda j, i: (0, j + nj))],
        out_specs=blk((tm, tc), lambda j, i: (i, j)),
        out_shape=jax.ShapeDtypeStruct((T, D_FF), BF16),
        compiler_params=_cparams(2),
    )(u, u, u, u, conv_w, conv_b, conv_w, conv_b)


def conv_glu_bwd(u, conv_w, conv_b, df):
    T = u.shape[0]
    tm, tc = FFN_TM, FFN_TC
    nj, ni = D_FF // tc, T // tm

    def fn(pids, ug, ugh, uv, uvh, cwg, cbg, cwv, cbv, df_t, nxt_g, nxt_v):
        i = ni - 1 - pids[1]
        ugh = jnp.where(i > 0, ugh, 0.0)
        uvh = jnp.where(i > 0, uvh, 0.0)
        cg = _conv3(ug, ugh, cwg, cbg)
        cv = _conv3(uv, uvh, cwv, cbv)
        _, vjp = jax.vjp(lambda g_, v_: _gelu_tanh(g_) * v_, cg, cv)
        dcg, dcv = vjp(df_t.astype(F32))
        cs = lambda z: jnp.sum(z, axis=0, keepdims=True)

        @pl.when(pids[1] == 0)
        def _():
            nxt_g[...] = jnp.zeros_like(nxt_g)
            nxt_v[...] = jnp.zeros_like(nxt_v)

        outs = []
        for dc, cw, nxt_ref in ((dcg, cwg, nxt_g), (dcv, cwv, nxt_v)):
            nxt = nxt_ref[...]
            outs.append(cw[0:1] * dc + cw[1:2] * _shift_up(dc, nxt, 1) + cw[2:3] * _shift_up(dc, nxt, 2))
            nxt_ref[...] = dc[:8]
        for dc, uu, hh in ((dcg, ug, ugh), (dcv, uv, uvh)):
            outs += [cs(dc * uu), cs(dc * _shift_down(uu, hh, 1)), cs(dc * _shift_down(uu, hh, 2)), cs(dc)]
        return outs

    rows = lambda off: (lambda j, r: (ni - 1 - r, j + off))
    halo = lambda off: (lambda j, r: (jnp.maximum((ni - 1 - r) * (tm // 8) - 1, 0), j + off))
    ins = [(u, (tm, tc), rows(0)), (u, (8, tc), halo(0)),
           (u, (tm, tc), rows(nj)), (u, (8, tc), halo(nj)),
           (conv_w, (3, tc), lambda j, r: (0, j)), (conv_b, (1, tc), lambda j, r: (0, j)),
           (conv_w, (3, tc), lambda j, r: (0, j + nj)), (conv_b, (1, tc), lambda j, r: (0, j + nj)),
           (df, (tm, tc), rows(0))]
    big = ((T, D_FF), BF16, (tm, tc), rows(0), None)
    acc = ((1, D_FF), F32, (1, tc), lambda j, r: (0, j), 1)
    res = tile_call("conv_glu_bwd", fn, (nj, ni), ins, [big, big] + [acc] * 8,
                    scratch=[((8, tc), F32), ((8, tc), F32)])
    dconv_w = jnp.concatenate([jnp.concatenate([res[2 + j], res[6 + j]], axis=1) for j in range(3)], axis=0)
    dconv_b = jnp.concatenate([res[5], res[9]], axis=1)
    return res[0], res[1], dconv_w, dconv_b


def _pad_cols(w, total):
    return jnp.pad(w, ((0, 0), (0, total - w.shape[1])))


def _pad_rows(w, total):
    return jnp.pad(w, ((0, total - w.shape[0]), (0, 0)))


def _proj_pad(w):
    z = lambda n: jnp.zeros((w.shape[0], n), w.dtype)
    return jnp.concatenate([w[:, :1600], z(64), w[:, 1600:1664], z(64), w[:, 1664:1824], z(96), w[:, 1824:],
                            z(PROJ_TAIL)], axis=1)


def _proj_unpad(g):
    return jnp.concatenate([g[:, :1600], g[:, OFF_XA:OFF_XA + 64], g[:, OFF_XG:OFF_XG + 160],
                            g[:, RW_PAD:RW_PAD + ATT_COLS]], axis=1)


def _rw_unpad(g):
    return jnp.concatenate([g[:, :1600], g[:, OFF_XA:OFF_XA + 64], g[:, OFF_XG:OFF_XG + 160]], axis=1)


def rms_fwd(name, x, g, tm=512):
    T, D = x.shape
    return tile_call(name, lambda pid, x_t, g_t: _rms(x_t, g_t), (T // tm,),
                     [_rows(x, tm), _par(g)], [_row_out(T, D, BF16, tm)])[0]


def rms_bwd(name, x, g, dh, dres, with_bf16=True, tm=512):
    T, D = x.shape
    out_dtypes = (F32, BF16) if with_bf16 else (F32,)

    def fn(pid, x_t, g_t, dh_t, dres_t):
        _, vjp = jax.vjp(_rms, x_t, g_t)
        dx, dg = vjp(dh_t.astype(F32))
        return (dres_t + dx,) * len(out_dtypes) + (dg,)

    return tile_call(name, fn, (T // tm,), [_rows(x, tm), _par(g), _rows(dh, tm), _rows(dres, tm)],
                     [_row_out(T, D, dt, tm) for dt in out_dtypes] + [_acc_out(1, D)])


def local_step(x, p, target, W):
    T, D = x.shape
    G = {}

    w_in_p = W["w_in_p"]
    mu_p = _proj_pad(_pad_cols(W["rw_mu"], 4128))[:, :RW_PAD]
    w_up_p = _pad_rows(W["rw_w_up"], 128)
    a_up_p = _pad_rows(W["rw_a_up"], 128)
    g_up_p = _pad_rows(W["rw_g_up"], 256)
    r_k = W["rw_r_k"].reshape(1, RW_WIDTH)
    rw_params = [mu_p, W["rw_w0"], w_up_p, W["rw_a0"], a_up_p, g_up_p, W["rw_k_k"], W["rw_k_a"]]

    h = rms_fwd("rms_mix", x, W["g_mix"])
    proj = matmul("proj_in_rw", h, w_in_p[:, :RW_PAD])
    pa = matmul("proj_in_att", h, w_in_p[:, RW_PAD:RW_PAD + ATT_COLS], out_dtype=BF16)
    gp = matmul("proj_gate", h, W["w_gate"])

    tm = 512
    rw_in = (proj, (tm, RW_PAD), lambda i: (i, 0))
    rw_halo = _prev_halo(proj, tm, RW_PAD)

    def rw_pre_tile(pid, Pc, halo, *params):
        prev8 = jnp.where(pid[0] > 0, halo, 0.0)
        params = [q.astype(F32) for q in params]
        return rw_pre(Pc, _shift_down(Pc, prev8, 1), *params)

    r, decay, k2, v, avec, bvec, g = tile_call(
        "rw_pre", rw_pre_tile, (T // tm,), [rw_in, rw_halo] + [_par(q) for q in rw_params],
        [_row_out(T, RW_WIDTH, F32, tm)] * 7)

    wa, ba, ka = scan_pair_terms(avec, decay, bvec, k2)
    vT = _to_head_time(v).astype(BF16)
    (yT, S_all, saT), late_slots = rwkv_scan_fwd(avec, decay, bvec, k2, r, vT, wa, ba, ka,
                                            exchange=_late_weight_sources(W))
    y = _from_head_time(yT)
    W = dict(W, **_late_weights(late_slots))

    post_params = [W["rw_ln_g"], W["rw_ln_b"], r_k]
    ya = tile_call("rw_post", lambda pid, *t: rw_post(*t), (T // tm,),
                   [_rows(z, tm) for z in (y, r, k2, v, g)] + [_par(q) for q in post_params],
                   [_row_out(T, RW_WIDTH, BF16, tm)])[0]

    att = [att_fwd(pa, gi, T) for gi in range(3)]
    o_l = [att[0][0], att[1][0], att[2][0], att[0][1], att[1][1], att[2][1]]
    yb = tile_call("att_combine", lambda pid, *t: att_combine(*t), (T // tm,),
                   [_rows(z, tm) for z in o_l], [_row_out(T, ATT_OUT, BF16, tm)])[0]

    za = matmul("branch_a", ya, W["w_branch_a"])
    zb = matmul("branch_b", yb, W["w_branch_b"])
    merged = tile_call("merge", lambda pid, *t: merge_fn(*t), (T // tm,),
                       [_rows(gp, tm), _par(W["b_gate"]), _rows(za, tm), _rows(zb, tm)],
                       [_row_out(T, D, BF16, tm)])[0]
    x1 = matmul("mix_out", merged, W["w_out"], res=x)

    h2 = rms_fwd("rms_ffn", x1, W["g_ffn"])
    u = matmul("ffn_up", h2, W["w_up"])
    f = conv_glu_fwd(u, W["conv_w"], W["conv_b"])
    x2 = matmul("ffn_down", f, W["w_down"], res=x1)

    h3 = rms_fwd("rms_ple", x2, W["g_ple"])
    zg = matmul("ple_gate", h3, W["w_ple_gate"])
    pe = matmul("ple_embed", p, W["w_ple"])

    def tail_tile(pid, x2_t, zg_t, pe_t, gf, tgt):
        loss, vjp = jax.vjp(lambda a_, b_, c_, d_: tail_loss(a_, b_, c_, d_, tgt), x2_t, zg_t, pe_t, gf)
        dx2, dzg, dpe, dgf = vjp(jnp.ones((), F32))
        return dx2, dzg, dpe, dgf, jnp.full((1, 128), loss, F32)

    tmt = 512
    dx3, dzg, dpe, dgf, loss_acc = tile_call(
        "tail_loss", tail_tile, (T // tmt,),
        [_rows(x2, tmt), _rows(zg, tmt), _rows(pe, tmt), _par(W["g_final"]), _rows(target, tmt)],
        [_row_out(T, D, F32, tmt), _row_out(T, D, BF16, tmt), _row_out(T, D, BF16, tmt),
         _acc_out(1, D), _acc_out(1, 128)])
    loss = loss_acc[0, 0]
    G["g_final"] = dgf

    wgrad = functools.partial(matmul, mode="tn", out_dtype=GRAD_WIRE)
    G["w_ple"] = wgrad("d_w_ple", p, dpe)
    G["w_ple_gate"] = wgrad("d_w_ple_gate", h3, dzg)
    dh3 = matmul("d_h3", dzg, W["w_ple_gate"], "nt")
    dx2, dx2b, G["g_ple"] = rms_bwd("rms_ple_bwd", x2, W["g_ple"], dh3, dx3)

    G["w_down"] = wgrad("d_w_down", f, dx2b)
    df = matmul("d_f", dx2b, W["w_down"], "nt", out_dtype=BF16)
    du_g, du_v, G["conv_w"], G["conv_b"] = conv_glu_bwd(u, W["conv_w"], W["conv_b"], df)
    du = jnp.concatenate([du_g, du_v], axis=1)
    G["w_up"] = wgrad("d_w_up", h2, du)
    dh2 = matmul("d_h2", du, W["w_up"], "nt")
    dx1, dx1b, G["g_ffn"] = rms_bwd("rms_ffn_bwd", x1, W["g_ffn"], dh2, dx2)

    G["w_out"] = wgrad("d_w_out", merged, dx1b)
    dmerged = matmul("d_merged", dx1b, W["w_out"], "nt", out_dtype=BF16)

    def merge_bwd_tile(pid, gp_t, bg, za_t, zb_t, dm_t):
        _, vjp = jax.vjp(merge_fn, gp_t, bg, za_t, zb_t)
        return vjp(dm_t.astype(F32))

    dgp, G["b_gate"], dza, dzb = tile_call(
        "merge_bwd", merge_bwd_tile, (T // tm,),
        [_rows(gp, tm), _par(W["b_gate"]), _rows(za, tm), _rows(zb, tm), _rows(dmerged, tm)],
        [_row_out(T, 2 * D, BF16, tm), _acc_out(1, 2 * D), _row_out(T, D, BF16, tm), _row_out(T, D, BF16, tm)])
    G["w_branch_a"] = wgrad("d_w_branch_a", ya, dza)
    dya = matmul("d_ya", dza, W["w_branch_a"], "nt")
    G["w_branch_b"] = wgrad("d_w_branch_b", yb, dzb)
    dyb = matmul("d_yb", dzb, W["w_branch_b"], "nt")
    G["w_gate"] = wgrad("d_w_gate", h, dgp)
    dh_gate = matmul("d_h_gate", dgp, W["w_gate"], "nt")

    def comb_bwd_tile(pid, *t):
        _, vjp = jax.vjp(att_combine, *t[:6])
        return vjp(t[6])

    d_ol = tile_call("att_combine_bwd", comb_bwd_tile, (T // tm,),
                     [_rows(z, tm) for z in o_l] + [_rows(dyb, tm)],
                     [_row_out(T, ATT_OUT, F32, tm)] * 6)
    dqkv = [att_bwd(pa, att[gi][0], att[gi][1], d_ol[gi], d_ol[3 + gi], gi, T) for gi in range(3)]
    d_att = [dqkv[gi][j] for j in range(3) for gi in range(3)]

    def post_bwd_tile(pid, *t):
        _, vjp = jax.vjp(rw_post, *t[:8])
        return vjp(t[8])

    dy, dr_p, dk2_p, dv_p, dg, G["rw_ln_g"], G["rw_ln_b"], d_rk = tile_call(
        "rw_post_bwd", post_bwd_tile, (T // tm,),
        [_rows(z, tm) for z in (y, r, k2, v, g)] + [_par(q) for q in post_params] + [_rows(dya, tm)],
        [_row_out(T, RW_WIDTH, F32, tm)] * 5 + [_acc_out(1, RW_WIDTH)] * 3)
    G["rw_r_k"] = d_rk.reshape(W["rw_r_k"].shape)

    (da, dw, db, dk_s, dr_s, dvT), G["_early_parts"] = rwkv_scan_bwd(
        avec, decay, bvec, k2, r, v, dy, S_all, saT, exchange=_early_grad_sources(G))
    dv_s = _from_head_time(dvT)

    tmb = 256
    rw_in_b = (proj, (tmb, RW_PAD), lambda i: (i, 0))

    def pre_bwd_tile(pid, Pc, halo, *t):
        prev8 = jnp.where(pid[0] > 0, halo, 0.0)
        params = [q.astype(F32) for q in t[:8]]
        dr1, dr2, dw_, dk1, dk2_, dv1, dv2, da_, db_, dg_ = t[8:]
        _, vjp = jax.vjp(rw_pre, Pc, _shift_down(Pc, prev8, 1), *params)
        return vjp((dr1 + dr2, dw_, dk1 + dk2_, dv1 + dv2, da_, db_, dg_))

    cts = (dr_s, dr_p, dw, dk_s, dk2_p, dv_s, dv_p, da, db, dg)
    res = tile_call(
        "rw_pre_bwd", pre_bwd_tile, (T // tmb,),
        [rw_in_b, _prev_halo(proj, tmb, RW_PAD)] + [_par(q) for q in rw_params] + [_rows(z, tmb) for z in cts],
        [_row_out(T, RW_PAD, F32, tmb)] * 2 + [_acc_out(*q.shape) for q in rw_params])
    dPc, dPs = res[0], res[1]
    d_mu, G["rw_w0"], d_wup, G["rw_a0"], d_aup, d_gup, G["rw_k_k"], G["rw_k_a"] = res[2:]
    G["rw_mu"] = _rw_unpad(d_mu)
    G["rw_w_up"], G["rw_a_up"], G["rw_g_up"] = d_wup[:64], d_aup[:64], d_gup[:160]

    def dproj_tile(pid, dPc_t, dPs_t, nxt, *att_t):
        nxt = jnp.where(pid[0] < T // tm - 1, nxt, 0.0)
        tail = jnp.zeros((dPc_t.shape[0], PROJ_TAIL), F32)
        return jnp.concatenate([dPc_t + _shift_up(dPs_t, nxt, 1)] + list(att_t) + [tail], axis=1)

    dproj = tile_call("d_proj", dproj_tile, (T // tm,),
                      [_rows(dPc, tm), _rows(dPs, tm), _next_halo(dPs, tm, RW_PAD, T)] + [_rows(z, tm) for z in d_att],
                      [_row_out(T, PROJ_PAD, BF16, tm)])[0]
    G["w_in_p"] = wgrad("d_w_in", h, dproj)
    w_in_srcs = _w_in_grad_sources(G)
    dh = matmul("d_h", dproj, w_in_p, "nt", res=dh_gate, exchange=w_in_srcs)
    if w_in_srcs:
        dh, G["_w_in_parts"] = dh
    dx, G["g_mix"] = rms_bwd("rms_mix_bwd", x, W["g_mix"], dh, dx1, with_bf16=False)
    return loss, dx, G


def _mesh_pos():
    return lax.axis_index("x"), lax.axis_index("y"), lax.axis_index("c")


def _peer(pos, k):
    x, y, c = pos
    px = 1 - x if k & 4 else x
    py = 1 - y if k & 2 else y
    pc = 1 - c if k & 1 else c
    return (px, py, pc), 4 * px + 2 * py + pc


def all_gather_blocks(name, blocks):
    n = len(blocks)

    def body(*refs):
        x_refs, out_refs = refs[:n], refs[n:2 * n]
        send_sems, recv_sems, local_sems = refs[2 * n:]
        x, y, c = _mesh_pos()
        me, sibling = (x, y, c), (x, y, 1 - c)
        chips = [(1 - x, y), (x, 1 - y), (1 - x, 1 - y)]
        ops = range(n)

        def slot(i, px, py, pc):
            return out_refs[i].at[4 * px + 2 * py + pc]

        def copy(k, i, block, to, own=False):
            return pltpu.make_async_remote_copy(
                src_ref=x_refs[i] if own else slot(i, *block), dst_ref=slot(i, *block),
                send_sem=send_sems.at[k, i], recv_sem=recv_sems.at[k, i],
                device_id=to, device_id_type=pl.DeviceIdType.MESH)

        mine = [pltpu.make_async_copy(x_refs[i], slot(i, *me), local_sems.at[i]) for i in ops]
        first = [copy(0, i, me, sibling, own=True) for i in ops]
        first += [copy(1 + j, i, me, (*chip, c), own=True) for j, chip in enumerate(chips) for i in ops]
        for cp in mine + first:
            cp.start()
        passed = []
        for j, chip in enumerate(chips):
            for i in ops:
                copy(1 + j, i, (*chip, c), me).wait_recv()
                passed.append(copy(4 + j, i, (*chip, c), sibling))
                passed[-1].start()
        for i in ops:
            copy(0, i, sibling, me).wait_recv()
        for j, chip in enumerate(chips):
            for i in ops:
                copy(4 + j, i, (*chip, 1 - c), me).wait_recv()
        for cp in first + passed:
            cp.wait_send()
        for cp in mine:
            cp.wait()

    return pl.pallas_call(
        body, name=name,
        in_specs=[pl.BlockSpec(memory_space=pl.ANY)] * n,
        out_specs=[pl.BlockSpec(memory_space=pl.ANY)] * n,
        out_shape=[jax.ShapeDtypeStruct((N_DEV,) + b.shape, b.dtype) for b in blocks],
        scratch_shapes=[pltpu.SemaphoreType.DMA((N_DEV - 1, n)), pltpu.SemaphoreType.DMA((N_DEV - 1, n)),
                        pltpu.SemaphoreType.DMA((n,))],
        compiler_params=pltpu.CompilerParams(has_side_effects=True),
    )(*blocks)


WHOLE = 0


def _exchange_shapes(srcs):
    shapes = [a.shape[1:] if cols is None else a.shape if cols == WHOLE else (a.shape[0], cols) for a, cols in srcs]
    return [jax.ShapeDtypeStruct((N_DEV,) + s, a.dtype) for s, (a, _) in zip(shapes, srcs)]


def _exchange_sems(n):
    return [pltpu.SemaphoreType.DMA((N_DEV - 1, n)), pltpu.SemaphoreType.DMA((N_DEV - 1, n)),
            pltpu.SemaphoreType.DMA((n,))]


def _exchange_ops(col_widths, x_refs, out_refs, send_sems, recv_sems, local_sems):
    n = len(col_widths)
    pos = _mesh_pos()
    me = 4 * pos[0] + 2 * pos[1] + pos[2]

    def piece(i, d):
        cols = col_widths[i]
        if cols is None:
            return x_refs[i].at[d]
        if cols == WHOLE:
            return x_refs[i]
        return x_refs[i].at[:, pl.ds(pl.multiple_of(d * cols, 128), cols)]

    def local(i):
        return pltpu.make_async_copy(piece(i, me), out_refs[i].at[me], local_sems.at[i])

    def remote(k, i, landing):
        peer, idx = _peer(pos, k)
        return pltpu.make_async_remote_copy(
            src_ref=piece(i, idx), dst_ref=out_refs[i].at[idx if landing else me],
            send_sem=send_sems.at[k - 1, i], recv_sem=recv_sems.at[k - 1, i],
            device_id=peer, device_id_type=pl.DeviceIdType.MESH)

    pairs = [(k, i) for k in range(1, N_DEV) for i in range(n)]

    def start():
        for i in range(n):
            local(i).start()
        for k, i in pairs:
            remote(k, i, False).start()

    def wait():
        for k, i in pairs:
            remote(k, i, True).wait_recv()
        for k, i in pairs:
            remote(k, i, False).wait_send()
        for i in range(n):
            local(i).wait()

    return start, wait


def all_to_all_blocks(name, srcs):
    n = len(srcs)

    def body(*refs):
        start, wait = _exchange_ops([c for _, c in srcs], refs[:n], refs[n:2 * n], *refs[2 * n:])
        start()
        wait()

    return pl.pallas_call(
        body, name=name,
        in_specs=[pl.BlockSpec(memory_space=pl.ANY)] * n,
        out_specs=[pl.BlockSpec(memory_space=pl.ANY)] * n,
        out_shape=_exchange_shapes(srcs),
        scratch_shapes=_exchange_sems(n),
        compiler_params=pltpu.CompilerParams(has_side_effects=True),
    )(*[a for a, _ in srcs])


def _adam_row_tile(R, C):
    best = None
    for t in range(16, R + 1, 16):
        if R % t == 0 and t * C <= ADAM_TILE_ELEMS:
            best = t
    return best if best is not None else R


def reduce_adamw(name, parts, w, m, v):
    _, R, C = parts.shape
    tr = _adam_row_tile(R, C)

    def fn(pid, parts_t, w_t, m_t, v_t):
        g = parts_t[0].astype(F32)
        for i in range(1, N_DEV):
            g = g + parts_t[i].astype(F32)
        m_n = ADAM_B1 * m_t + (1.0 - ADAM_B1) * g
        v_n = ADAM_B2 * v_t + (1.0 - ADAM_B2) * (g * g)
        m_hat = m_n / (1.0 - ADAM_B1 ** ADAM_STEP)
        v_hat = v_n / (1.0 - ADAM_B2 ** ADAM_STEP)
        delta = -ADAM_LR * (m_hat / (jnp.sqrt(v_hat) + ADAM_EPS) + ADAM_WD * w_t)
        return g, delta, m_n, v_n

    row = lambda a: (a, (tr, C), lambda i: (i, 0))
    out = ((R, C), F32, (tr, C), lambda i: (i, 0), None)
    return tile_call(name, fn, (R // tr,),
                     [(parts, (N_DEV, tr, C), lambda i: (0, i, 0)), row(w), row(m), row(v)], [out] * 4)


PARAMS = (
    ("g_mix", (1, 1024), None), ("w_in", (1024, 4128), 1), ("rw_mu", (1, 1824), None), ("rw_w0", (1, 512), None),
    ("rw_w_up", (64, 512), 1), ("rw_a0", (1, 512), None), ("rw_a_up", (64, 512), 1), ("rw_g_up", (160, 512), 1),
    ("rw_k_k", (1, 512), None), ("rw_k_a", (1, 512), None), ("rw_r_k", (8, 64), None), ("rw_ln_g", (1, 512), None),
    ("rw_ln_b", (1, 512), None), ("w_branch_a", (512, 1024), 1), ("w_branch_b", (256, 1024), 1),
    ("w_gate", (1024, 2048), 1), ("b_gate", (1, 2048), None), ("w_out", (1024, 1024), 0), ("g_ffn", (1, 1024), None),
    ("w_up", (1024, 6144), 1), ("conv_w", (3, 6144), 1), ("conv_b", (1, 6144), None), ("w_down", (3072, 1024), 0),
    ("g_ple", (1, 1024), None), ("w_ple_gate", (1024, 1024), 0), ("w_ple", (256, 1024), 1), ("g_final", (1, 1024), None),
)
SHARDED = tuple(q for q in PARAMS if q[2] is not None)
REPLICATED = tuple(q for q in PARAMS if q[2] is None)
BIG_NAMES = ("w_in", "w_up", "w_gate", "w_out", "w_down", "w_ple_gate", "w_branch_a", "w_branch_b", "w_ple")
BIG = tuple(q for q in SHARDED if q[0] in BIG_NAMES)
SMALL_SHARDED = tuple(q for q in SHARDED if q[0] not in BIG_NAMES)
PACK_COLS = 1024
F32_GATHERED = ("conv_w",)


def _local_shape(shape, axis):
    s = list(shape)
    s[axis] //= N_DEV
    return tuple(s)


def _numel(shape):
    return int(np.prod(shape))


def _pad_flat(z, mult):
    n = z.shape[-1]
    total = -(-n // mult) * mult
    return jnp.pad(z, [(0, 0)] * (z.ndim - 1) + [(0, total - n)])


def _full_from_slots(slots, shape, axis):
    loc = _local_shape(shape, axis)
    z = slots.reshape((N_DEV,) + loc)
    if axis == 0:
        return z.reshape(shape)
    return z.transpose(1, 0, 2).reshape(shape)


def _slots_from_full(full, shape, axis):
    loc = _local_shape(shape, axis)
    if axis == 0:
        return full.reshape(N_DEV, _numel(loc))
    return full.reshape(shape[0], N_DEV, loc[1]).transpose(1, 0, 2).reshape(N_DEV, _numel(loc))


W_IN_SLOT = 640
W_IN_LOCAL = 4128 // N_DEV


def _block_shape(shape, axis):
    return _local_shape(shape, axis) if axis is not None else shape


def _pad_w_in(block):
    return jnp.pad(block, ((0, 0), (0, W_IN_SLOT - W_IN_LOCAL)))


def _proj_col(s):
    return s + jnp.where(s >= 1600, 64, 0) + jnp.where(s >= 1664, 64, 0) + jnp.where(s >= 1824, 96, 0)


def _perm_tile(d, c0, width):
    j = lax.broadcasted_iota(jnp.int32, (W_IN_SLOT, width), 0)
    c = c0 + lax.broadcasted_iota(jnp.int32, (W_IN_SLOT, width), 1)
    hit = (_proj_col(d * W_IN_LOCAL + j) == c) & (j < W_IN_LOCAL)
    return jnp.where(hit, 1.0, 0.0).astype(BF16)


PERM_TILE = 768


def w_in_unshuffle(slots):
    _, K, _ = slots.shape
    tn = PERM_TILE
    reach = 3

    def first_slot(j):
        return j + jnp.where(j >= 3, 1, 0) + jnp.where(j >= 5, 1, 0)

    def body(a_ref, o_ref, acc_ref):
        j, kk = pl.program_id(0), pl.program_id(1)
        d = first_slot(j) + kk

        @pl.when(kk == 0)
        def _():
            acc_ref[...] = jnp.zeros_like(acc_ref)

        @pl.when(d < N_DEV)
        def _():
            acc_ref[...] += jnp.dot(a_ref[0], _perm_tile(d, j * tn, tn), preferred_element_type=F32)

        @pl.when(kk == reach - 1)
        def _():
            o_ref[...] = acc_ref[...].astype(o_ref.dtype)

    return pl.pallas_call(
        body, name="w_in_unshuffle", grid=(PROJ_PAD // tn, reach),
        in_specs=[pl.BlockSpec((1, K, W_IN_SLOT), lambda j, kk: (jnp.minimum(first_slot(j) + kk, N_DEV - 1), 0, 0))],
        out_specs=pl.BlockSpec((K, tn), lambda j, kk: (0, j)),
        out_shape=jax.ShapeDtypeStruct((K, PROJ_PAD), BF16),
        scratch_shapes=[pltpu.VMEM((K, tn), F32)],
        compiler_params=_cparams(2),
    )(slots)


def w_in_shuffle_grad(dw):
    K = dw.shape[0]
    tk = PERM_TILE

    def first_tile(d):
        return _proj_col(d * W_IN_LOCAL) // tk

    def body(g_ref, o_ref, acc_ref):
        d, kk = pl.program_id(0), pl.program_id(1)
        perm = _perm_tile(d, (first_tile(d) + kk) * tk, tk)
        part = lax.dot_general(g_ref[...].astype(BF16), perm, NT_DIMS, preferred_element_type=F32)

        @pl.when(kk == 0)
        def _():
            acc_ref[...] = part

        @pl.when(kk == 1)
        def _():
            o_ref[0] = (acc_ref[...] + part).astype(o_ref.dtype)

    return pl.pallas_call(
        body, name="w_in_shuffle_grad", grid=(N_DEV, 2),
        in_specs=[pl.BlockSpec((K, tk), lambda d, kk: (0, first_tile(d) + kk))],
        out_specs=pl.BlockSpec((1, K, W_IN_SLOT), lambda d, kk: (d, 0, 0)),
        out_shape=jax.ShapeDtypeStruct((N_DEV, K, W_IN_SLOT), GRAD_WIRE),
        scratch_shapes=[pltpu.VMEM((K, W_IN_SLOT), F32)],
        compiler_params=_cparams(2),
    )(dw)


def _flat_rows(pieces, dtype, row_mult):
    flat = jnp.concatenate([z.astype(dtype) for z in pieces], axis=-1)
    flat = _pad_flat(flat, row_mult * PACK_COLS)
    return flat.reshape(flat.shape[:-1] + (-1, PACK_COLS))


FIRST = tuple(q for q in BIG if q[0] in ("w_in", "w_gate"))
LATE = tuple(q for q in BIG if q not in FIRST)


def _matrix_from_slots(slots, shape, axis):
    return slots.reshape(shape) if axis == 0 else slots.transpose(1, 0, 2).reshape(shape)


def _late_weight_sources(W):
    return [(blk, WHOLE) for blk in W["_late_blocks"]]


def _late_weights(slots):
    return {n: _matrix_from_slots(s, shape, axis) for (n, shape, axis), s in zip(LATE, slots)}


def gather_weights(local):
    blocks = [(_pad_w_in(local[n]) if n == "w_in" else local[n]).astype(BF16) for n, _, _ in FIRST]
    small = [q for q in SMALL_SHARDED if q[0] not in F32_GATHERED]
    exact = [q for q in SMALL_SHARDED if q[0] in F32_GATHERED]
    blocks.append(_flat_rows([local[n].reshape(-1) for n, _, _ in small], BF16, 16))
    blocks.append(_flat_rows([local[n].reshape(-1) for n, _, _ in exact], F32, 8))
    got = all_gather_blocks("weight_all_gather", blocks)
    full = {"_late_blocks": [local[n].astype(BF16) for n, _, _ in LATE]}
    for (n, shape, axis), slots in zip(FIRST, got):
        if n == "w_in":
            full["w_in_p"] = w_in_unshuffle(slots)
        else:
            full[n] = _matrix_from_slots(slots, shape, axis)
    for group, slots in ((small, got[-2]), (exact, got[-1])):
        slots, off = slots.reshape(N_DEV, -1), 0
        for n, shape, axis in group:
            size = _numel(_local_shape(shape, axis))
            full[n] = _full_from_slots(slots[:, off:off + size], shape, axis)
            off += size
    for n, _, _ in REPLICATED:
        full[n] = local[n]
    return full


LOSS_SLOT = ("_loss", (1, 2), None)
PACKED_SMALL = SMALL_SHARDED + REPLICATED + (LOSS_SLOT,)


def _pack_small(vals):
    pieces = [vals[n].reshape(-1) if n in vals else jnp.zeros((_numel(shape),), F32) for n, shape, _ in PACKED_SMALL]
    return _flat_rows(pieces, F32, 16)


def _unpack_small(packed):
    flat, out, off = packed.reshape(-1), {}, 0
    for n, shape, axis in PACKED_SMALL:
        loc = _block_shape(shape, axis)
        out[n] = flat[off:off + _numel(loc)].reshape(loc)
        off += _numel(loc)
    return out


EARLY = tuple(q for q in BIG if q[0] != "w_in")


def _early_grad_sources(G):
    srcs = []
    for n, shape, axis in EARLY:
        if axis == 0:
            srcs.append((G[n].astype(GRAD_WIRE).reshape((N_DEV,) + _local_shape(shape, axis)), None))
        else:
            srcs.append((G[n].astype(GRAD_WIRE), shape[1] // N_DEV))
    return srcs


def _w_in_grad_sources(G):
    return [(w_in_shuffle_grad(G["w_in_p"]), None)]


def _closing_grad_sources(G, loss_local):
    srcs = []
    rows = [_slots_from_full(G[n].reshape(shape), shape, axis) for n, shape, axis in SMALL_SHARDED]
    loss_hi = loss_local.astype(GRAD_WIRE).astype(F32)
    rep = jnp.concatenate([G[n].reshape(-1) for n, _, _ in REPLICATED] + [jnp.stack([loss_hi, loss_local - loss_hi])])
    rows.append(jnp.broadcast_to(rep[None, :], (N_DEV, rep.shape[0])))
    srcs.append((_flat_rows(rows, GRAD_WIRE, 16), None))
    return srcs


def _step(x, p, target, local_w, local_m, local_v):
    full = gather_weights(local_w)
    loss_local, dx, G = local_step(x, p, target, full)
    closing = all_to_all_blocks("grad_all_to_all", _closing_grad_sources(G, loss_local))
    parts = list(G["_w_in_parts"]) + list(G["_early_parts"]) + list(closing)
    outs = [{}, {}, {}, {}]
    for (n, shape, axis), part in zip((BIG[0],) + EARLY, parts):
        prep = _pad_w_in if n == "w_in" else (lambda z: z)
        res = reduce_adamw("adamw_" + n, part, prep(local_w[n]), prep(local_m[n]), prep(local_v[n]))
        for o, z in zip(outs, res):
            o[n] = z[:, :W_IN_LOCAL] if n == "w_in" else z
    res = reduce_adamw("adamw_small", parts[-1], _pack_small(local_w), _pack_small(local_m), _pack_small(local_v))
    for o, z in zip(outs, res):
        o.update(_unpack_small(z))
    loss = jnp.sum(outs[0]["_loss"])
    return loss, dx, outs


def kernel(x, p, g_mix, w_in, rw_mu, rw_w0, rw_w_up, rw_a0, rw_a_up, rw_g_up, rw_k_k, rw_k_a, rw_r_k, rw_ln_g, rw_ln_b, w_branch_a, w_branch_b, w_gate, b_gate, w_out, g_ffn, w_up, conv_w, conv_b, w_down, g_ple, w_ple_gate, w_ple, g_final, loss_target, m_g_mix, m_w_in, m_rw_mu, m_rw_w0, m_rw_w_up, m_rw_a0, m_rw_a_up, m_rw_g_up, m_rw_k_k, m_rw_k_a, m_rw_r_k, m_rw_ln_g, m_rw_ln_b, m_w_branch_a, m_w_branch_b, m_w_gate, m_b_gate, m_w_out, m_g_ffn, m_w_up, m_conv_w, m_conv_b, m_w_down, m_g_ple, m_w_ple_gate, m_w_ple, m_g_final, v_g_mix, v_w_in, v_rw_mu, v_rw_w0, v_rw_w_up, v_rw_a0, v_rw_a_up, v_rw_g_up, v_rw_k_k, v_rw_k_a, v_rw_r_k, v_rw_ln_g, v_rw_ln_b, v_w_branch_a, v_w_branch_b, v_w_gate, v_b_gate, v_w_out, v_g_ffn, v_w_up, v_conv_w, v_conv_b, v_w_down, v_g_ple, v_w_ple_gate, v_w_ple, v_g_final):
    args = dict(locals())
    names = [n for n, _, _ in PARAMS]
    orig_shape = {n: args[n].shape for n in names}

    def strip(prefix):
        out = {}
        for n, shape, axis in PARAMS:
            a = args[prefix + n]
            loc = _local_shape(shape, axis) if axis is not None else shape
            out[n] = a.reshape(loc)
        return out

    local_w, local_m, local_v = strip(""), strip("m_"), strip("v_")
    T, D = x.shape[-2], x.shape[-1]
    loss, dx, (g, delta, m_n, v_n) = _step(x.reshape(T, D), p.reshape(T, p.shape[-1]), loss_target.reshape(T, D),
                                           local_w, local_m, local_v)
    outs = [loss, dx.reshape(x.shape)]
    for group in (g, delta, m_n, v_n):
        outs += [group[n].reshape(orig_shape[n]) for n in names]
    return tuple(outs)
```
